```python
import math
import jax, jax.numpy as jnp
from jax import lax
import numpy as np

D_MODEL = 1024
BATCH = 16
SEQ = 2048
DEPTH = 1

M_HEADS = 4
M_HEAD_DIM = 128
M_WIDTH = M_HEADS * M_HEAD_DIM
M_CHUNK = 128
CONV_WIDTH = 4
N_HEADS = 8
N_KV_GROUPS = 2
N_REP = N_HEADS // N_KV_GROUPS
N_HEAD_DIM = 64
N_WIDTH = N_HEADS * N_HEAD_DIM
N_KV_WIDTH = N_KV_GROUPS * N_HEAD_DIM
CMP_BLOCK = 32
CMP_STRIDE = 16
CMP_HIDDEN = 2 * N_HEAD_DIM
SEL_BLOCK = 64
SEL_TOPK = 16
WINDOW = 512
NSA_Q_BLOCK = 64
REL_BUCKETS = 32
REL_MAX_DIST = 1024
N_BRANCH = 2
BRANCH_WIDTH = 512
D_FF = -(-8 * D_MODEL // (3 * 256)) * 256
RMS_EPS = 1e-6
BIG = 1e9

IN_SIZES = (M_WIDTH, M_WIDTH, M_WIDTH, M_WIDTH, M_HEADS, M_HEADS,
            N_WIDTH, 6 * N_KV_WIDTH, 3 * N_HEADS, N_BRANCH * D_MODEL)
D_IN_TOTAL = sum(IN_SIZES)

kernel_name = 'hybrid_mlstm_nsa_block'


def rms_norm(x, g):
    x32 = x.astype(jnp.float32)
    y = x32 * lax.rsqrt(jnp.mean(x32 * x32, axis=-1, keepdims=True) + RMS_EPS)
    return (y * g.astype(jnp.float32)).astype(x.dtype)


def split_columns(p, sizes):
    out = []
    start = 0
    for s in sizes:
        out.append(p[..., start:start + s])
        start += s
    return out


def causal_depthwise_conv(x, w):
    T = x.shape[1]
    xp = jnp.pad(x, ((0, 0), (CONV_WIDTH - 1, 0), (0, 0)))
    return sum(w[j] * xp[:, j:j + T] for j in range(CONV_WIDTH))


def rel_bucket(dist):
    n = jnp.maximum(dist, 0)
    max_exact = REL_BUCKETS // 2
    nf = jnp.maximum(n, 1).astype(jnp.float32)
    large = max_exact + (jnp.log(nf / max_exact) / math.log(REL_MAX_DIST / max_exact)
                         * (REL_BUCKETS - max_exact)).astype(jnp.int32)
    large = jnp.minimum(large, REL_BUCKETS - 1)
    return jnp.where(n < max_exact, n, large)


def masked_softmax(logits, mask):
    logits = logits.astype(jnp.float32)
    z = jnp.where(mask, logits, -1e30)
    z = z - jnp.max(z, axis=-1, keepdims=True)
    e = jnp.where(mask, jnp.exp(z), 0.0)
    return e / jnp.maximum(jnp.sum(e, axis=-1, keepdims=True), 1e-30)


def mlstm_chunkwise(q, k, v, i_pre, f_pre):
    B, H, T, dk = q.shape
    dv = v.shape[-1]
    L = M_CHUNK
    nc = T // L
    qc = q.reshape(B, H, nc, L, dk)
    kc = k.reshape(B, H, nc, L, dk)
    vc = v.reshape(B, H, nc, L, dv)
    ic = i_pre.reshape(B, H, nc, L)
    bc = jnp.cumsum(jax.nn.log_sigmoid(f_pre).reshape(B, H, nc, L), axis=-1)
    b_last = bc[..., -1]
    a = b_last[..., None] - bc + ic

    def step(carry, inp):
        C, n, m = carry
        k_j, v_j, a_j, bl = inp
        m_new = jnp.maximum(bl + m, jnp.max(a_j, axis=-1))
        decay = jnp.exp(bl + m - m_new)
        w = jnp.exp(a_j - m_new[..., None])
        C_new = decay[..., None, None] * C + jnp.einsum('bhl,bhlk,bhlv->bhkv', w, k_j, v_j)
        n_new = decay[..., None] * n + jnp.einsum('bhl,bhlk->bhk', w, k_j)
        return (C_new, n_new, m_new), (C, n, m)

    init = (jnp.zeros((B, H, dk, dv), jnp.float32), jnp.zeros((B, H, dk), jnp.float32),
            jnp.zeros((B, H), jnp.float32))
    xs = (jnp.moveaxis(kc, 2, 0), jnp.moveaxis(vc, 2, 0), jnp.moveaxis(a, 2, 0), jnp.moveaxis(b_last, 2, 0))
    _, (C_prev, n_prev, m_prev) = lax.scan(step, init, xs)
    C_prev = jnp.moveaxis(C_prev, 0, 2)
    n_prev = jnp.moveaxis(n_prev, 0, 2)
    m_prev = jnp.moveaxis(m_prev, 0, 2)

    causal = jnp.tril(jnp.ones((L, L), dtype=bool))
    log_d = jnp.where(causal, bc[..., :, None] - bc[..., None, :] + ic[..., None, :], -jnp.inf)
    log_inter = bc + m_prev[..., None]
    m_i = jnp.maximum(jnp.max(log_d, axis=-1), log_inter)
    P = jnp.exp(log_d - m_i[..., None]) * jnp.einsum('bhcid,bhcjd->bhcij', qc, kc)
    s_inter = jnp.exp(log_inter - m_i)
    num = (jnp.einsum('bhcij,bhcjv->bhciv', P, vc)
           + s_inter[..., None] * jnp.einsum('bhcid,bhcdv->bhciv', qc, C_prev))
    den = jnp.sum(P, axis=-1) + s_inter * jnp.einsum('bhcid,bhcd->bhci', qc, n_prev)
    h = num / jnp.maximum(jnp.abs(den), jnp.exp(-m_i))[..., None]
    return h.reshape(B, H, T, dv)


def mlstm_mixer(q_pre, k_pre, v, o_pre, i_pre, f_pre, b_fgate, conv_w, g_head):
    B, T, _ = q_pre.shape
    qk = jax.nn.silu(causal_depthwise_conv(jnp.concatenate([q_pre, k_pre], axis=-1), conv_w))
    to_heads = lambda t: t.reshape(B, T, M_HEADS, M_HEAD_DIM).transpose(0, 2, 1, 3).astype(jnp.float32)
    q = to_heads(qk[..., :M_WIDTH])
    k = to_heads(qk[..., M_WIDTH:]) * (M_HEAD_DIM ** -0.5)
    vh = to_heads(v)
    ig = i_pre.astype(jnp.float32).transpose(0, 2, 1)
    fg = (f_pre.astype(jnp.float32) + b_fgate.astype(jnp.float32)).transpose(0, 2, 1)
    hh = mlstm_chunkwise(q, k, vh, ig, fg)
    hh = (hh * lax.rsqrt(jnp.mean(hh * hh, axis=-1, keepdims=True) + RMS_EPS)
          * g_head.astype(jnp.float32).reshape(M_HEADS, 1, M_HEAD_DIM))
    hh = hh.transpose(0, 2, 1, 3).reshape(B, T, M_WIDTH)
    return (jax.nn.sigmoid(o_pre.astype(jnp.float32)) * hh).astype(q_pre.dtype)


def nsa_mixer(q_in, kv_in, gate_in, pe_cmp, w_cmp1, w_cmp2, rel_bias):
    B, T, _ = q_in.shape
    G, R, dh, Q = N_KV_GROUPS, N_REP, N_HEAD_DIM, NSA_Q_BLOCK
    q = q_in.reshape(B, T, G, R, dh).transpose(0, 2, 3, 1, 4) * (dh ** -0.5)
    kv = kv_in.reshape(B, T, 6, G, dh).transpose(2, 0, 3, 1, 4)
    k_c, v_c, k_s, v_s, k_w, v_w = kv[0], kv[1], kv[2], kv[3], kv[4], kv[5]
    gates = jax.nn.sigmoid(gate_in.astype(jnp.float32).reshape(B, T, G, R, 3).transpose(0, 2, 3, 1, 4))

    n_cmp = (T - CMP_BLOCK) // CMP_STRIDE + 1
    cmp_start = jnp.arange(n_cmp) * CMP_STRIDE
    cmp_idx = cmp_start[:, None] + jnp.arange(CMP_BLOCK)[None, :]
    cmp_end = cmp_start + CMP_BLOCK - 1

    def compress(kx, j):
        blk = kx[:, :, cmp_idx] + pe_cmp[j]
        flat = blk.reshape(B, G, n_cmp, CMP_BLOCK * dh)
        return jax.nn.silu(flat @ w_cmp1[j]) @ w_cmp2[j]

    kc_ = compress(k_c, 0)
    vc_ = compress(v_c, 1)

    n_slc = T // SEL_BLOCK
    n_top = min(SEL_TOPK, n_slc)
    k_sb = k_s.reshape(B, G, n_slc, SEL_BLOCK, dh)
    v_sb = v_s.reshape(B, G, n_slc, SEL_BLOCK, dh)
    slc_start = jnp.arange(n_slc) * SEL_BLOCK
    overlap = (jnp.clip(jnp.minimum(cmp_start[:, None] + CMP_BLOCK, slc_start[None, :] + SEL_BLOCK)
                        - jnp.maximum(cmp_start[:, None], slc_start[None, :]), 0)
               / CMP_STRIDE).astype(jnp.float32)

    k_wp = jnp.pad(k_w, ((0, 0), (0, 0), (WINDOW, 0), (0, 0)))
    v_wp = jnp.pad(v_w, ((0, 0), (0, 0), (WINDOW, 0), (0, 0)))

    table = rel_bias.astype(jnp.float32).T.reshape(G, R, REL_BUCKETS)
    g_ix = jnp.arange(G)[None, :, None, None, None]
    r_ix = jnp.arange(R)[None, None, :, None, None]
    gather_blocks = jax.vmap(jax.vmap(lambda blk, ix: blk[ix]))

    def block(i):
        qs = i * Q
        t = qs + jnp.arange(Q)
        qb = lax.dynamic_slice_in_dim(q, qs, Q, axis=3)
        lc = (jnp.einsum('bgrqd,bgcd->bgrqc', qb, kc_).astype(jnp.float32)
              + table[:, :, rel_bucket(t[:, None] - cmp_end[None, :])])
        p_c = masked_softmax(lc, cmp_end[None, :] <= t[:, None])
        o_c = jnp.einsum('bgrqc,bgcd->bgrqd', p_c, vc_.astype(jnp.float32))
        imp = jnp.einsum('bgrqc,cj->bgqj', p_c, overlap)
        cur = t // SEL_BLOCK
        jb = jnp.arange(n_slc)[None, :]
        forced = (jb == 0) | (jb == cur[:, None]) | (jb == cur[:, None] - 1)
        eligible = jb * SEL_BLOCK <= t[:, None]
        score = jnp.where(eligible, jnp.where(forced, BIG, imp), -BIG)
        top_val, top_idx = lax.top_k(score, n_top)
        N = n_top * SEL_BLOCK
        k_g = gather_blocks(k_sb, top_idx).reshape(B, G, Q, N, dh)
        v_g = gather_blocks(v_sb, top_idx).reshape(B, G, Q, N, dh)
        pos_u = top_idx[..., None] * SEL_BLOCK + jnp.arange(SEL_BLOCK)
        mask_u = (top_val > -BIG / 2)[..., None] & (pos_u <= t[:, None, None])
        pos = pos_u.reshape(B, G, Q, N)
        mask_s = mask_u.reshape(B, G, Q, N)
        bias_s = table[g_ix, r_ix, rel_bucket(t[:, None] - pos)[:, :, None]]
        ls = jnp.einsum('bgrqd,bgqnd->bgrqn', qb, k_g).astype(jnp.float32) + bias_s
        p_s = masked_softmax(ls, mask_s[:, :, None])
        o_s = jnp.einsum('bgrqn,bgqnd->bgrqd', p_s, v_g.astype(jnp.float32))
        span = Q + WINDOW
        k_wb = lax.dynamic_slice_in_dim(k_wp, qs, span, axis=2)
        v_wb = lax.dynamic_slice_in_dim(v_wp, qs, span, axis=2)
        pos_w = qs - WINDOW + jnp.arange(span)
        dist = t[:, None] - pos_w[None, :]
        mask_w = (pos_w[None, :] >= 0) & (dist >= 0) & (dist < WINDOW)
        lw = jnp.einsum('bgrqd,bgkd->bgrqk', qb, k_wb).astype(jnp.float32) + table[:, :, rel_bucket(dist)]
        p_w = masked_softmax(lw, mask_w)
        o_w = jnp.einsum('bgrqk,bgkd->bgrqd', p_w, v_wb.astype(jnp.float32))
        gb = lax.dynamic_slice_in_dim(gates, qs, Q, axis=3)
        return gb[..., 0:1] * o_c + gb[..., 1:2] * o_s + gb[..., 2:3] * o_w

    out = lax.map(block, jnp.arange(T // Q))
    out = out.transpose(1, 0, 4, 2, 3, 5).reshape(B, T, N_WIDTH)
    return out.astype(q_in.dtype)


def setup_inputs(seed: int = 0) -> dict:
    key = jax.random.key(seed)
    ks = jax.random.split(key, 20)
    nrm = lambda k, shape, scale: jax.random.normal(k, shape, jnp.float32) * scale
    x = nrm(ks[0], (BATCH, SEQ, D_MODEL), 1.0)
    g_norm_mix = 1.0 + nrm(ks[1], (DEPTH, D_MODEL), 0.02)
    w_in = nrm(ks[2], (DEPTH, D_MODEL, D_IN_TOTAL), D_MODEL ** -0.5)
    b_in = nrm(ks[3], (DEPTH, D_IN_TOTAL), 0.02)
    b_fgate = jnp.linspace(3.0, 6.0, M_HEADS, dtype=jnp.float32)[None, :] + nrm(ks[4], (DEPTH, M_HEADS), 0.1)
    conv_qk = nrm(ks[5], (DEPTH, CONV_WIDTH, 2 * M_WIDTH), CONV_WIDTH ** -0.5)
    g_mlstm_head = 1.0 + nrm(ks[6], (DEPTH, M_WIDTH), 0.02)
    pe_cmp = nrm(ks[7], (DEPTH, 2, CMP_BLOCK, N_HEAD_DIM), 0.1)
    w_cmp1 = nrm(ks[8], (DEPTH, 2, CMP_BLOCK * N_HEAD_DIM, CMP_HIDDEN), (CMP_BLOCK * N_HEAD_DIM) ** -0.5)
    w_cmp2 = nrm(ks[9], (DEPTH, 2, CMP_HIDDEN, N_HEAD_DIM), CMP_HIDDEN ** -0.5)
    rel_bias = nrm(ks[10], (REL_BUCKETS, N_HEADS), 0.5)
    w_branch = nrm(ks[11], (DEPTH, N_BRANCH, BRANCH_WIDTH, D_MODEL), BRANCH_WIDTH ** -0.5)
    w_out = nrm(ks[12], (DEPTH, D_MODEL, D_MODEL), D_MODEL ** -0.5)
    g_norm_ffn = 1.0 + nrm(ks[13], (DEPTH, D_MODEL), 0.02)
    w_gate = nrm(ks[14], (DEPTH, D_MODEL, D_FF), D_MODEL ** -0.5)
    w_up = nrm(ks[15], (DEPTH, D_MODEL, D_FF), D_MODEL ** -0.5)
    w_down = nrm(ks[16], (DEPTH, D_FF, D_MODEL), D_FF ** -0.5)
    g_final = 1.0 + nrm(ks[17], (D_MODEL,), 0.02)
    return {'x': x, 'g_norm_mix': g_norm_mix, 'w_in': w_in, 'b_in': b_in, 'b_fgate': b_fgate,
            'conv_qk': conv_qk, 'g_mlstm_head': g_mlstm_head, 'pe_cmp': pe_cmp, 'w_cmp1': w_cmp1,
            'w_cmp2': w_cmp2, 'rel_bias': rel_bias, 'w_branch': w_branch, 'w_out': w_out,
            'g_norm_ffn': g_norm_ffn, 'w_gate': w_gate, 'w_up': w_up, 'w_down': w_down, 'g_final': g_final}


def reference(x, g_norm_mix, w_in, b_in, b_fgate, conv_qk, g_mlstm_head, pe_cmp, w_cmp1, w_cmp2,
              rel_bias, w_branch, w_out, g_norm_ffn, w_gate, w_up, w_down, g_final):
    B, T, _ = x.shape
    h = x
    for l in range(DEPTH):
        u = rms_norm(h, g_norm_mix[l])
        proj = jnp.einsum('btd,de->bte', u, w_in[l]) + b_in[l]
        mq, mk, mv, mo, mi, mf, nq, nkv, ngate, merge = split_columns(proj, IN_SIZES)
        y_m = mlstm_mixer(mq, mk, mv, mo, mi, mf, b_fgate[l], conv_qk[l], g_mlstm_head[l])
        y_n = nsa_mixer(nq, nkv, ngate, pe_cmp[l], w_cmp1[l], w_cmp2[l], rel_bias)
        ys = jnp.stack([y_m, y_n], axis=2)
        branch = jnp.einsum('btnw,nwd->btnd', ys, w_branch[l])
        gates = jax.nn.sigmoid(merge.reshape(B, T, N_BRANCH, D_MODEL))
        mixed = jnp.sum(gates * branch, axis=2)
        h = h + mixed @ w_out[l]
        f = rms_norm(h, g_norm_ffn[l])
        h = h + (jax.nn.silu(f @ w_gate[l]) * (f @ w_up[l])) @ w_down[l]
    return rms_norm(h, g_final)
```

```python
import functools

import numpy as np
import jax
import jax.numpy as jnp
from jax import lax
from jax.experimental import pallas as pl
from jax.experimental.pallas import tpu as pltpu

F32 = jnp.float32
BF16 = jnp.bfloat16
HI = lax.Precision.HIGHEST

D_MODEL = 1024
M_HEADS = 4
M_HEAD_DIM = 128
M_WIDTH = M_HEADS * M_HEAD_DIM
M_CHUNK = 128
CONV_WIDTH = 4
N_HEADS = 8
N_KV_GROUPS = 2
N_REP = N_HEADS // N_KV_GROUPS
N_HEAD_DIM = 64
N_WIDTH = N_HEADS * N_HEAD_DIM
N_KV_WIDTH = N_KV_GROUPS * N_HEAD_DIM
CMP_BLOCK = 32
CMP_STRIDE = 16
CMP_HIDDEN = 2 * N_HEAD_DIM
SEL_BLOCK = 64
SEL_TOPK = 16
WINDOW = 512
REL_BUCKETS = 32
REL_MAX_DIST = 1024
N_BRANCH = 2
D_FF = 2816
RMS_EPS = 1e-6
BIG = 1e9
NEG = -1e30

LANES = 128
TQ = 128
VMEM_LIMIT = 56 * 1024 * 1024

_OFF_MQ, _OFF_MK, _OFF_MV, _OFF_MO = 0, 512, 1024, 1536
_OFF_MI, _OFF_MF, _OFF_NQ, _OFF_NKV = 2048, 2052, 2056, 2568
_OFF_NGATE, _OFF_MERGE, _D_IN = 3336, 3360, 5408
_SMALL_NGATE = 8

_SEGS = (("mqk", 1024), ("mv", 512), ("mo", 512), ("nq", 512), ("kc", 128),
         ("vc", 128), ("nsw", 512), ("mg", 2048), ("small", 128))
_D_IN_PAD = sum(w for _, w in _SEGS)


def _dot(a, b, **kw):
    return jnp.dot(a, b, preferred_element_type=F32, **kw)


def _dot_nt(a, b):
    return lax.dot_general(a, b, (((1,), (1,)), ((), ())), preferred_element_type=F32)


def _dot_tn(a, b):
    return lax.dot_general(a, b, (((0,), (0,)), ((), ())), preferred_element_type=F32)


def _const_spec(shape):
    nd = len(shape)
    return pl.BlockSpec(shape, lambda *_: (0,) * nd, pipeline_mode=pl.Buffered(1))


def _bucket_thresholds():
    max_exact = REL_BUCKETS // 2
    assert REL_MAX_DIST == 64 * max_exact and REL_BUCKETS - max_exact == 16
    thr = []
    for k in range(1, REL_BUCKETS - max_exact):
        t = max_exact
        while t ** 8 < (max_exact ** 8) * (2 ** (3 * k)):
            t += 1
        thr.append(t)
    return tuple(thr)


_BUCKET_THR = _bucket_thresholds()


def _inproj_perm():
    idx = np.zeros((_D_IN_PAD,), np.int32)
    scale = np.zeros((_D_IN_PAD,), np.float32)
    pos = 0

    def put(cols, s=1.0):
        nonlocal pos
        n = len(cols)
        idx[pos:pos + n] = cols
        scale[pos:pos + n] = s
        pos += n

    put(np.arange(_OFF_MQ, _OFF_MV))
    put(np.arange(_OFF_MV, _OFF_MO))
    put(np.arange(_OFF_MO, _OFF_MI))
    nq = np.zeros((N_WIDTH,), np.int32)
    for r in range(N_REP):
        for g in range(N_KV_GROUPS):
            for d in range(N_HEAD_DIM):
                nq[r * 128 + g * 64 + d] = _OFF_NQ + (g * N_REP + r) * N_HEAD_DIM + d
    put(nq, N_HEAD_DIM ** -0.5)
    put(np.arange(_OFF_NKV, _OFF_NKV + 768))
    put(np.arange(_OFF_MERGE, _D_IN))
    put(np.arange(_OFF_MI, _OFF_MI + 8))
    put(np.arange(_OFF_NGATE, _OFF_NGATE + 24))
    pos += LANES - 32
    assert pos == _D_IN_PAD
    return idx, scale


def _inproj_kernel(x_ref, g_ref, w_ref, b_ref, mqk_ref, mv_ref, mo_ref, nq_ref,
                   cin_ref, nsw_ref, mg_ref, small_ref):
    x = x_ref[...]
    ms = jnp.mean(x * x, axis=-1, keepdims=True)
    u = (x * lax.rsqrt(ms + RMS_EPS) * g_ref[...]).astype(BF16)

    def seg(a, n):
        return _dot(u, w_ref[:, a:a + n]) + b_ref[:, a:a + n]

    off = {}
    pos = 0
    for name, w in _SEGS:
        off[name] = pos
        pos += w
    mqk_ref[...] = seg(off["mqk"], 1024)
    mv_ref[...] = seg(off["mv"], 512).astype(BF16)
    mo_ref[...] = seg(off["mo"], 512)
    nq_ref[...] = seg(off["nq"], 512).astype(BF16)
    cin_ref[0] = seg(off["kc"], 128)
    cin_ref[1] = seg(off["vc"], 128)
    nsw_ref[...] = seg(off["nsw"], 512).astype(BF16)
    mg_ref[...] = seg(off["mg"], 2048)
    small_ref[...] = seg(off["small"], 128)


def _inproj(x2, g, w, b, tm=256):
    n = x2.shape[0]
    assert n % tm == 0
    row = lambda width: pl.BlockSpec((tm, width), lambda i: (i, 0))
    out_shape = (
        jax.ShapeDtypeStruct((n, 1024), F32), jax.ShapeDtypeStruct((n, 512), BF16),
        jax.ShapeDtypeStruct((n, 512), F32), jax.ShapeDtypeStruct((n, 512), BF16),
        jax.ShapeDtypeStruct((2, n, 128), F32), jax.ShapeDtypeStruct((n, 512), BF16),
        jax.ShapeDtypeStruct((n, 2048), F32), jax.ShapeDtypeStruct((n, 128), F32))
    out_specs = (row(1024), row(512), row(512), row(512),
                 pl.BlockSpec((2, tm, 128), lambda i: (0, i, 0)), row(512), row(2048), row(128))
    return pl.pallas_call(
        _inproj_kernel, out_shape=out_shape, grid=(n // tm,),
        in_specs=[row(D_MODEL), _const_spec((1, D_MODEL)), _const_spec((D_MODEL, _D_IN_PAD)),
                  _const_spec((1, _D_IN_PAD))],
        out_specs=out_specs, name="inproj",
        compiler_params=pltpu.CompilerParams(dimension_semantics=("arbitrary",),
                                             vmem_limit_bytes=VMEM_LIMIT),
    )(x2, g, w, b)


def _mlstm_kernel(mqk_ref, mv_ref, mo_ref, small_ref, cw_ref, bf_ref, gh_ref, ltri_ref,
                  ym_ref, ebuf, c_st, n_st, m_st):
    L = M_CHUNK
    c = pl.program_id(1)

    @pl.when(c == 0)
    def _():
        ebuf[0:8, :] = jnp.zeros((8, 2 * M_WIDTH), F32)
        c_st[...] = jnp.zeros_like(c_st)
        n_st[...] = jnp.zeros_like(n_st)
        m_st[...] = jnp.zeros_like(m_st)

    ebuf[8:8 + L, :] = mqk_ref[0]
    conv = cw_ref[0:1, :] * ebuf[5:5 + L, :]
    for j in range(1, CONV_WIDTH):
        conv = conv + cw_ref[j:j + 1, :] * ebuf[5 + j:5 + j + L, :]
    ebuf[0:8, :] = ebuf[L:L + 8, :]
    qk = conv * jax.nn.sigmoid(conv)

    gt = small_ref[0]
    ig = gt
    fg = pltpu.roll(gt, LANES - M_HEADS, 1) + bf_ref[...]
    lf = jnp.minimum(fg, 0.0) - jnp.log1p(jnp.exp(-jnp.abs(fg)))
    bc = jnp.dot(ltri_ref[...], lf, precision=HI, preferred_element_type=F32)
    blast = bc[L - 1:L, :]
    m_prev = m_st[0:1, :]
    a_all = blast - bc + ig
    m_new = jnp.maximum(blast + m_prev, jnp.max(a_all, axis=0, keepdims=True))
    decay = jnp.exp(blast + m_prev - m_new)
    w_all = jnp.exp(a_all - m_new)
    log_inter = bc + m_prev
    bc_t = bc.T
    ig_t = ig.T

    row = lax.broadcasted_iota(jnp.int32, (L, L), 0)
    col = lax.broadcasted_iota(jnp.int32, (L, L), 1)
    causal = row >= col

    for h in range(M_HEADS):
        sl = slice(h * M_HEAD_DIM, (h + 1) * M_HEAD_DIM)
        q = qk[:, sl]
        k = qk[:, M_WIDTH + h * M_HEAD_DIM:M_WIDTH + (h + 1) * M_HEAD_DIM] * (M_HEAD_DIM ** -0.5)
        v = mv_ref[0, :, sl]
        qb = q.astype(BF16)
        kb = k.astype(BF16)
        c_prev = c_st[h]
        n_prev = n_st[h, 0:1, :]

        log_d = jnp.where(causal, bc[:, h:h + 1] - bc_t[h:h + 1, :] + ig_t[h:h + 1, :], -jnp.inf)
        li = log_inter[:, h:h + 1]
        m_i = jnp.maximum(jnp.max(log_d, axis=-1, keepdims=True), li)
        p = jnp.exp(log_d - m_i) * _dot_nt(qb, kb)
        s_inter = jnp.exp(li - m_i)
        num = _dot(p.astype(BF16), v) + s_inter * _dot(qb, c_prev.astype(BF16))
        den = jnp.sum(p, axis=-1, keepdims=True) + s_inter * jnp.sum(q * n_prev, axis=-1, keepdims=True)
        hh = num / jnp.maximum(jnp.abs(den), jnp.exp(-m_i))
        hh = hh * lax.rsqrt(jnp.mean(hh * hh, axis=-1, keepdims=True) + RMS_EPS) * gh_ref[:, sl]
        ym_ref[0, :, sl] = (jax.nn.sigmoid(mo_ref[0, :, sl]) * hh).astype(ym_ref.dtype)

        w_col = w_all[:, h:h + 1]
        dec = decay[:, h:h + 1]
        vw = (v.astype(F32) * w_col).astype(BF16)
        c_st[h] = dec * c_prev + _dot_tn(kb, vw)
        n_st[h, 0:1, :] = dec * n_prev + jnp.sum(k * w_col, axis=0, keepdims=True)

    m_st[0:1, :] = m_new


def _mlstm(mqk, mv, mo, small, cw, bf, gh, ltri):
    B, T, _ = mqk.shape
    L = M_CHUNK
    assert T % L == 0
    blk = lambda w: pl.BlockSpec((1, L, w), lambda b, c: (b, c, 0))
    return pl.pallas_call(
        _mlstm_kernel, out_shape=jax.ShapeDtypeStruct((B, T, M_WIDTH), BF16),
        grid=(B, T // L),
        in_specs=[blk(2 * M_WIDTH), blk(M_WIDTH), blk(M_WIDTH), blk(LANES),
                  _const_spec((8, 2 * M_WIDTH)), _const_spec((1, LANES)),
                  _const_spec((1, M_WIDTH)), _const_spec((L, L))],
        out_specs=blk(M_WIDTH),
        scratch_shapes=[pltpu.VMEM((L + 8, 2 * M_WIDTH), F32),
                        pltpu.VMEM((M_HEADS, M_HEAD_DIM, M_HEAD_DIM), F32),
                        pltpu.VMEM((M_HEADS, 8, M_HEAD_DIM), F32),
                        pltpu.VMEM((8, LANES), F32)],
        name="mlstm",
        compiler_params=pltpu.CompilerParams(dimension_semantics=("arbitrary", "arbitrary")),
    )(mqk, mv, mo, small, cw, bf, gh, ltri)


def _compress_kernel(x_ref, pe_ref, w1t_ref, w1b_ref, w2_ref, o_ref, *, nch):
    half = CMP_BLOCK // 2
    top = jnp.zeros((nch, 2 * CMP_HIDDEN), F32)
    bot = jnp.zeros((nch, 2 * CMP_HIDDEN), F32)
    for p in range(half):
        xp = x_ref[0, 0, pl.ds(p, nch, stride=CMP_STRIDE), :]
        top = top + _dot((xp + pe_ref[0, p:p + 1, :]).astype(BF16), w1t_ref[0, p])
        bot = bot + _dot((xp + pe_ref[0, half + p:half + p + 1, :]).astype(BF16), w1b_ref[0, p])
    hid = top + pltpu.roll(bot, nch - 1, 0)
    act = hid * jax.nn.sigmoid(hid)
    o_ref[0, 0] = _dot(act.astype(BF16), w2_ref[0]).astype(o_ref.dtype)


def _compress(cin, pe2, w1t, w1b, w2):
    _, B, T, _ = cin.shape
    nch = T // CMP_STRIDE
    sel = lambda *shape: pl.BlockSpec((1,) + shape, lambda j, b: (j,) + (0,) * len(shape))
    return pl.pallas_call(
        functools.partial(_compress_kernel, nch=nch),
        out_shape=jax.ShapeDtypeStruct((2, B, nch, LANES), BF16),
        grid=(2, B),
        in_specs=[pl.BlockSpec((1, 1, T, LANES), lambda j, b: (j, b, 0, 0)),
                  sel(CMP_BLOCK, LANES), sel(CMP_BLOCK // 2, LANES, 2 * CMP_HIDDEN),
                  sel(CMP_BLOCK // 2, LANES, 2 * CMP_HIDDEN), sel(2 * CMP_HIDDEN, LANES)],
        out_specs=pl.BlockSpec((1, 1, nch, LANES), lambda j, b: (j, b, 0, 0)),
        name="compress",
        compiler_params=pltpu.CompilerParams(dimension_semantics=("arbitrary", "arbitrary")),
    )(cin, pe2, w1t, w1b, w2)


def _bias_kernel(tbl_ref, o_ref, *, kind):
    pid = pl.program_id(0)
    a = lax.broadcasted_iota(jnp.int32, (TQ, LANES), 0)
    b = lax.broadcasted_iota(jnp.int32, (TQ, LANES), 1)
    if kind == "cmp":
        dist = pid * TQ + a - (b * CMP_STRIDE + CMP_BLOCK - 1)
    else:
        dist = pid * TQ + a - b
    n = jnp.maximum(dist, 0)
    cnt = jnp.zeros_like(n)
    for t in _BUCKET_THR:
        cnt = cnt + jnp.where(n >= t, 1, 0)
    bucket = jnp.where(n < REL_BUCKETS // 2, n, REL_BUCKETS // 2 + cnt)
    if kind == "tok":
        madd = jnp.where(dist >= 0, 0.0, NEG)
    elif kind == "win":
        madd = jnp.where((dist >= 0) & (dist < WINDOW), 0.0, NEG)
    else:
        madd = jnp.zeros((TQ, LANES), F32)
    for h in range(N_HEADS):
        val = jnp.zeros((TQ, LANES), F32)
        for bb in range(REL_BUCKETS):
            val = jnp.where(bucket == bb, tbl_ref[h * REL_BUCKETS + bb], val)
        o_ref[0, h * TQ:(h + 1) * TQ, :] = val + madd


def _bias_tiles(tbl, n_tiles, kind):
    return pl.pallas_call(
        functools.partial(_bias_kernel, kind=kind),
        out_shape=jax.ShapeDtypeStruct((n_tiles, N_HEADS * TQ, LANES), F32),
        grid=(n_tiles,),
        in_specs=[pl.BlockSpec(memory_space=pltpu.SMEM)],
        out_specs=pl.BlockSpec((1, N_HEADS * TQ, LANES), lambda i: (i, 0, 0)),
        name="bias_" + kind,
        compiler_params=pltpu.CompilerParams(dimension_semantics=("arbitrary",)),
    )(tbl)


def _nsa_kernel(nq_ref, ks_ref, vs_ref, kw_ref, vw_ref, kc_ref, vc_ref, small_ref,
                bt_ref, wb_ref, cb_ref, ov_ref, e_ref, yn_ref,
                seladd, m_sc, l_sc, acc_sc, *, n_cmp, n_slc, n_top, n_win):
    qi = pl.program_id(1)
    HR = N_HEADS * TQ
    lane = lax.broadcasted_iota(jnp.int32, (TQ, LANES), 1)
    row = lax.broadcasted_iota(jnp.int32, (TQ, LANES), 0)
    lo = lane < N_HEAD_DIM

    q_all = nq_ref[0]
    zero = jnp.zeros((TQ, LANES), BF16)
    parts = []
    for g in range(N_KV_GROUPS):
        for r in range(N_REP):
            qr = q_all[:, r * LANES:(r + 1) * LANES]
            parts.append(jnp.where(lo if g == 0 else jnp.logical_not(lo), qr, zero))
    qs = jnp.concatenate(parts, axis=0)

    row8 = lax.broadcasted_iota(jnp.int32, (HR, LANES), 0)
    lane8 = lax.broadcasted_iota(jnp.int32, (HR, LANES), 1)
    t8 = qi * TQ + (row8 & (TQ - 1))
    mc = (lane8 * CMP_STRIDE + CMP_BLOCK - 1 <= t8) & (lane8 < n_cmp)
    sc = _dot_nt(qs, kc_ref[0, 0]) + cb_ref[0]
    z = jnp.where(mc, sc, NEG)
    z = z - jnp.max(z, axis=-1, keepdims=True)
    e = jnp.where(mc, jnp.exp(z), 0.0)
    p_c = e / jnp.maximum(jnp.sum(e, axis=-1, keepdims=True), 1e-30)
    o_c = _dot(p_c.astype(BF16), vc_ref[0, 0])

    t = qi * TQ + row
    cur = jnp.right_shift(t, 6)
    forced = (lane == 0) | (lane == cur) | (lane == cur - 1)
    elig = lane <= cur
    for g in range(N_KV_GROUPS):
        psum = p_c[(g * N_REP) * TQ:(g * N_REP + 1) * TQ]
        for r in range(1, N_REP):
            psum = psum + p_c[(g * N_REP + r) * TQ:(g * N_REP + r + 1) * TQ]
        imp = jnp.dot(psum, ov_ref[...], precision=HI, preferred_element_type=F32)
        score = jnp.where(elig, jnp.where(forced, BIG, imp), -BIG)
        score = jnp.where(lane < n_slc, score, -2.0 * BIG)
        cnt = jnp.zeros((TQ, LANES), F32)
        for i in range(n_slc):
            ci = score[:, i:i + 1]
            cnt = cnt + jnp.where(ci > score, 1.0, 0.0) + jnp.where((ci == score) & (lane > i), 1.0, 0.0)
        sel = jnp.where((cnt < n_top) & (score > -BIG / 2), 1.0, 0.0)
        selk = _dot(sel.astype(BF16), e_ref[...])
        seladd[g] = (selk - 1.0) * (-NEG)

    def flash(k_ref, v_ref, bias_ref, lo_kj, use_sel):
        m_sc[...] = jnp.full((HR, LANES), NEG, F32)
        l_sc[...] = jnp.zeros((HR, LANES), F32)
        acc_sc[...] = jnp.zeros((HR, LANES), F32)

        def body(kj, carry):
            off = pl.multiple_of(kj * TQ, TQ)
            k = k_ref[0, pl.ds(off, TQ), :]
            v = v_ref[0, pl.ds(off, TQ), :]
            s = _dot_nt(qs, k) + bias_ref[qi - kj]
            if use_sel:
                sa0 = seladd[0, :, pl.ds(off, TQ)]
                sa1 = seladd[1, :, pl.ds(off, TQ)]
                s = s + jnp.concatenate([sa0] * N_REP + [sa1] * N_REP, axis=0)
            m_prev = m_sc[...]
            m_new = jnp.maximum(m_prev, jnp.max(s, axis=-1, keepdims=True))
            alpha = jnp.exp(m_prev - m_new)
            p = jnp.exp(s - m_new)
            l_sc[...] = alpha * l_sc[...] + jnp.sum(p, axis=-1, keepdims=True)
            acc_sc[...] = alpha * acc_sc[...] + _dot(p.astype(BF16), v)
            m_sc[...] = m_new
            return carry

        lax.fori_loop(lo_kj, qi + 1, body, 0)
        return acc_sc[...] / l_sc[...]

    o_s = flash(ks_ref, vs_ref, bt_ref, 0, True)
    o_w = flash(kw_ref, vw_ref, wb_ref, jnp.maximum(qi - (n_win - 1), 0), False)

    sg = jax.nn.sigmoid(small_ref[0])
    for r in range(N_REP):
        h0, h1 = r, N_REP + r

        def pair(o):
            return jnp.where(lo, o[h0 * TQ:(h0 + 1) * TQ], o[h1 * TQ:(h1 + 1) * TQ])

        def gate(ci):
            c0 = _SMALL_NGATE + h0 * 3 + ci
            c1 = _SMALL_NGATE + h1 * 3 + ci
            return jnp.where(lo, sg[:, c0:c0 + 1], sg[:, c1:c1 + 1])

        out = gate(0) * pair(o_c) + gate(1) * pair(o_s) + gate(2) * pair(o_w)
        yn_ref[0, :, r * LANES:(r + 1) * LANES] = out.astype(yn_ref.dtype)


def _nsa(nq, nsw, ckv, small, bt, wb, cb, ov, emat):
    B, T, _ = nq.shape
    assert T == 2048, "single 128-wide compressed-key tile assumes T == 2048"
    nqt = T // TQ
    n_cmp = (T - CMP_BLOCK) // CMP_STRIDE + 1
    n_slc = T // SEL_BLOCK
    n_top = min(SEL_TOPK, n_slc)
    n_win = wb.shape[0]
    HR = N_HEADS * TQ
    kv = lambda j: pl.BlockSpec((1, T, LANES), lambda b, q: (b, 0, j))
    ck = lambda j: pl.BlockSpec((1, 1, T // CMP_STRIDE, LANES), lambda b, q: (j, b, 0, 0))
    kern = functools.partial(_nsa_kernel, n_cmp=n_cmp, n_slc=n_slc, n_top=n_top, n_win=n_win)
    return pl.pallas_call(
        kern, out_shape=jax.ShapeDtypeStruct((B, T, N_WIDTH), BF16),
        grid=(B, nqt),
        in_specs=[pl.BlockSpec((1, TQ, N_WIDTH), lambda b, q: (b, q, 0)),
                  kv(0), kv(1), kv(2), kv(3), ck(0), ck(1),
                  pl.BlockSpec((1, TQ, LANES), lambda b, q: (b, q, 0)),
                  _const_spec(bt.shape), _const_spec(wb.shape),
                  pl.BlockSpec((1, HR, LANES), lambda b, q: (q, 0, 0)),
                  _const_spec(ov.shape), _const_spec(emat.shape)],
        out_specs=pl.BlockSpec((1, TQ, N_WIDTH), lambda b, q: (b, q, 0)),
        scratch_shapes=[pltpu.VMEM((N_KV_GROUPS, TQ, T), F32),
                        pltpu.VMEM((HR, LANES), F32), pltpu.VMEM((HR, LANES), F32),
                        pltpu.VMEM((HR, LANES), F32)],
        name="nsa",
        compiler_params=pltpu.CompilerParams(dimension_semantics=("arbitrary", "arbitrary"),
                                             vmem_limit_bytes=VMEM_LIMIT),
    )(nq, nsw, nsw, nsw, nsw, ckv, ckv, small, bt, wb, cb, ov, emat)


def _merge_kernel(x_ref, ym_ref, yn_ref, mg_ref, wbm_ref, wbn_ref, wo_ref, gffn_ref,
                  wg_ref, wu_ref, wd_ref, gfin_ref, o_ref, *, tf):
    bm = _dot(ym_ref[...], wbm_ref[...])
    bn = _dot(yn_ref[...], wbn_ref[...])
    mixed = jax.nn.sigmoid(mg_ref[:, :D_MODEL]) * bm + jax.nn.sigmoid(mg_ref[:, D_MODEL:]) * bn
    h = x_ref[...] + _dot(mixed.astype(BF16), wo_ref[...])
    f = (h * lax.rsqrt(jnp.mean(h * h, axis=-1, keepdims=True) + RMS_EPS) * gffn_ref[...]).astype(BF16)
    acc = jnp.zeros(h.shape, F32)
    for j in range(D_FF // tf):
        gg = _dot(f, wg_ref[:, j * tf:(j + 1) * tf])
        uu = _dot(f, wu_ref[:, j * tf:(j + 1) * tf])
        act = (gg * jax.nn.sigmoid(gg) * uu).astype(BF16)
        acc = acc + _dot(act, wd_ref[j * tf:(j + 1) * tf, :])
    h2 = h + acc
    o_ref[...] = h2 * lax.rsqrt(jnp.mean(h2 * h2, axis=-1, keepdims=True) + RMS_EPS) * gfin_ref[...]


def _merge(x2, ym, yn, mg, wbm, wbn, wo, gffn, wg, wu, wd, gfin, tm=256, tf=256):
    n = x2.shape[0]
    assert n % tm == 0 and D_FF % tf == 0
    row = lambda width: pl.BlockSpec((tm, width), lambda i: (i, 0))
    return pl.pallas_call(
        functools.partial(_merge_kernel, tf=tf),
        out_shape=jax.ShapeDtypeStruct((n, D_MODEL), F32), grid=(n // tm,),
        in_specs=[row(D_MODEL), row(M_WIDTH), row(N_WIDTH), row(N_BRANCH * D_MODEL),
                  _const_spec(wbm.shape), _const_spec(wbn.shape), _const_spec(wo.shape),
                  _const_spec(gffn.shape), _const_spec(wg.shape), _const_spec(wu.shape),
                  _const_spec(wd.shape), _const_spec(gfin.shape)],
        out_specs=row(D_MODEL), name="merge_ffn",
        compiler_params=pltpu.CompilerParams(dimension_semantics=("arbitrary",),
                                             vmem_limit_bytes=VMEM_LIMIT),
    )(x2, ym, yn, mg, wbm, wbn, wo, gffn, wg, wu, wd, gfin)


def _nsa_constants(T):
    n_cmp = (T - CMP_BLOCK) // CMP_STRIDE + 1
    n_slc = T // SEL_BLOCK
    cs = np.arange(n_cmp) * CMP_STRIDE
    ss = np.arange(n_slc) * SEL_BLOCK
    ov = np.clip(np.minimum(cs[:, None] + CMP_BLOCK, ss[None, :] + SEL_BLOCK)
                 - np.maximum(cs[:, None], ss[None, :]), 0, None) / CMP_STRIDE
    ovp = np.zeros((LANES, LANES), np.float32)
    ovp[:n_cmp, :n_slc] = ov
    emat = (np.arange(T)[None, :] // SEL_BLOCK == np.arange(LANES)[:, None]).astype(np.float32)
    return jnp.asarray(ovp), jnp.asarray(emat, dtype=BF16)


def _compress_weights(pe_cmp, w_cmp1, w_cmp2):
    half = CMP_BLOCK // 2
    dh, hid = N_HEAD_DIM, CMP_HIDDEN
    eye = jnp.eye(N_KV_GROUPS, dtype=F32)
    w1 = w_cmp1.reshape(2, CMP_BLOCK, dh, hid)
    w1bd = jnp.einsum("jpdn,gk->jpgdkn", w1, eye).reshape(2, CMP_BLOCK, N_KV_GROUPS * dh, N_KV_GROUPS * hid)
    w2bd = jnp.einsum("jnd,gk->jgnkd", w_cmp2, eye).reshape(2, N_KV_GROUPS * hid, N_KV_GROUPS * dh)
    pe2 = jnp.tile(pe_cmp, (1, 1, N_KV_GROUPS))
    return pe2, w1bd[:, :half].astype(BF16), w1bd[:, half:].astype(BF16), w2bd.astype(BF16)


def kernel(x, g_norm_mix, w_in, b_in, b_fgate, conv_qk, g_mlstm_head, pe_cmp, w_cmp1, w_cmp2,
           rel_bias, w_branch, w_out, g_norm_ffn, w_gate, w_up, w_down, g_final):
    B, T, D = x.shape
    assert D == D_MODEL and w_in.shape[0] == 1, "one residual block (DEPTH == 1)"
    N = B * T
    x2 = x.reshape(N, D)

    idx, scale = _inproj_perm()
    w_r = (jnp.take(w_in[0], jnp.asarray(idx), axis=1) * jnp.asarray(scale)).astype(BF16)
    b_r = (jnp.take(b_in[0], jnp.asarray(idx)) * jnp.asarray(scale)).reshape(1, -1)
    mqk, mv, mo, nq, cin, nsw, mg, small = _inproj(x2, g_norm_mix[0].reshape(1, D), w_r, b_r)

    cw = jnp.zeros((8, 2 * M_WIDTH), F32).at[:CONV_WIDTH].set(conv_qk[0])
    bf = jnp.zeros((1, LANES), F32).at[0, :M_HEADS].set(b_fgate[0])
    ltri = jnp.asarray(np.tril(np.ones((M_CHUNK, M_CHUNK), np.float32)))
    r3 = lambda a: a.reshape(B, T, a.shape[-1])
    ym = _mlstm(r3(mqk), r3(mv), r3(mo), r3(small), cw, bf, g_mlstm_head[0].reshape(1, M_WIDTH), ltri)

    pe2, w1t, w1b, w2 = _compress_weights(pe_cmp[0], w_cmp1[0], w_cmp2[0])
    ckv = _compress(cin.reshape(2, B, T, LANES), pe2, w1t, w1b, w2)
    tbl = rel_bias.astype(F32).T.reshape(-1)
    nqt = T // TQ
    bt = _bias_tiles(tbl, nqt, "tok")
    wb = _bias_tiles(tbl, min(WINDOW // TQ + 1, nqt), "win")
    cb = _bias_tiles(tbl, nqt, "cmp")
    ov, emat = _nsa_constants(T)
    yn = _nsa(r3(nq), r3(nsw), ckv, r3(small), bt, wb, cb, ov, emat)

    perm = np.zeros((N_WIDTH,), np.int32)
    for r in range(N_REP):
        for g in range(N_KV_GROUPS):
            perm[r * 128 + g * 64:r * 128 + g * 64 + 64] = (g * N_REP + r) * N_HEAD_DIM + np.arange(N_HEAD_DIM)
    wbm = w_branch[0, 0].astype(BF16)
    wbn = jnp.take(w_branch[0, 1], jnp.asarray(perm), axis=0).astype(BF16)
    out = _merge(x2, ym.reshape(N, M_WIDTH), yn.reshape(N, N_WIDTH), mg, wbm, wbn,
                 w_out[0].astype(BF16), g_norm_ffn[0].reshape(1, D), w_gate[0].astype(BF16),
                 w_up[0].astype(BF16), w_down[0].astype(BF16), g_final.reshape(1, D))
    return out.reshape(B, T, D)
```

```python
import functools

import numpy as np
import jax
import jax.numpy as jnp
from jax import lax
from jax.experimental import pallas as pl
from jax.experimental.pallas import tpu as pltpu

F32 = jnp.float32
BF16 = jnp.bfloat16
HI = lax.Precision.HIGHEST

D_MODEL = 1024
M_HEADS = 4
M_HEAD_DIM = 128
M_WIDTH = M_HEADS * M_HEAD_DIM
M_CHUNK = 128
CONV_WIDTH = 4
N_HEADS = 8
N_KV_GROUPS = 2
N_REP = N_HEADS // N_KV_GROUPS
N_HEAD_DIM = 64
N_WIDTH = N_HEADS * N_HEAD_DIM
N_KV_WIDTH = N_KV_GROUPS * N_HEAD_DIM
CMP_BLOCK = 32
CMP_STRIDE = 16
CMP_HIDDEN = 2 * N_HEAD_DIM
SEL_BLOCK = 64
SEL_TOPK = 16
WINDOW = 512
REL_BUCKETS = 32
REL_MAX_DIST = 1024
N_BRANCH = 2
D_FF = 2816
RMS_EPS = 1e-6
BIG = 1e9
NEG = -1e30

LANES = 128
TQ = 128
VMEM_LIMIT = 56 * 1024 * 1024

_OFF_MQ, _OFF_MK, _OFF_MV, _OFF_MO = 0, 512, 1024, 1536
_OFF_MI, _OFF_MF, _OFF_NQ, _OFF_NKV = 2048, 2052, 2056, 2568
_OFF_NGATE, _OFF_MERGE, _D_IN = 3336, 3360, 5408
_SMALL_NGATE = 8

_SEGS = (("mqk", 1024), ("mv", 512), ("mo", 512), ("nq", 512), ("kc", 128),
         ("vc", 128), ("nsw", 512), ("mg", 2048), ("small", 128))
_D_IN_PAD = sum(w for _, w in _SEGS)


def _dot(a, b, **kw):
    return jnp.dot(a, b, preferred_element_type=F32, **kw)


def _dot_nt(a, b):
    return lax.dot_general(a, b, (((1,), (1,)), ((), ())), preferred_element_type=F32)


def _dot_tn(a, b):
    return lax.dot_general(a, b, (((0,), (0,)), ((), ())), preferred_element_type=F32)


def _const_spec(shape):
    nd = len(shape)
    return pl.BlockSpec(shape, lambda *_: (0,) * nd, pipeline_mode=pl.Buffered(1))


def _bucket_thresholds():
    max_exact = REL_BUCKETS // 2
    assert REL_MAX_DIST == 64 * max_exact and REL_BUCKETS - max_exact == 16
    thr = []
    for k in range(1, REL_BUCKETS - max_exact):
        t = max_exact
        while t ** 8 < (max_exact ** 8) * (2 ** (3 * k)):
            t += 1
        thr.append(t)
    return tuple(thr)


_BUCKET_THR = _bucket_thresholds()


def _inproj_perm():
    idx = np.zeros((_D_IN_PAD,), np.int32)
    scale = np.zeros((_D_IN_PAD,), np.float32)
    pos = 0

    def put(cols, s=1.0):
        nonlocal pos
        n = len(cols)
        idx[pos:pos + n] = cols
        scale[pos:pos + n] = s
        pos += n

    put(np.arange(_OFF_MQ, _OFF_MV))
    put(np.arange(_OFF_MV, _OFF_MO))
    put(np.arange(_OFF_MO, _OFF_MI))
    nq = np.zeros((N_WIDTH,), np.int32)
    for r in range(N_REP):
        for g in range(N_KV_GROUPS):
            for d in range(N_HEAD_DIM):
                nq[r * 128 + g * 64 + d] = _OFF_NQ + (g * N_REP + r) * N_HEAD_DIM + d
    put(nq, N_HEAD_DIM ** -0.5)
    put(np.arange(_OFF_NKV, _OFF_NKV + 768))
    put(np.arange(_OFF_MERGE, _D_IN))
    put(np.arange(_OFF_MI, _OFF_MI + 8))
    put(np.arange(_OFF_NGATE, _OFF_NGATE + 24))
    pos += LANES - 32
    assert pos == _D_IN_PAD
    return idx, scale


def _inproj_kernel(x_ref, g_ref, w_ref, b_ref, mqk_ref, mv_ref, mo_ref, nq_ref,
                   cin_ref, nsw_ref, mg_ref, small_ref):
    x = x_ref[...]
    ms = jnp.mean(x * x, axis=-1, keepdims=True)
    u = (x * lax.rsqrt(ms + RMS_EPS) * g_ref[...]).astype(BF16)

    def seg(a, n):
        return _dot(u, w_ref[:, a:a + n]) + b_ref[:, a:a + n]

    off = {}
    pos = 0
    for name, w in _SEGS:
        off[name] = pos
        pos += w
    mqk_ref[...] = seg(off["mqk"], 1024)
    mv_ref[...] = seg(off["mv"], 512).astype(BF16)
    mo_ref[...] = seg(off["mo"], 512)
    nq_ref[...] = seg(off["nq"], 512).astype(BF16)
    cin_ref[0] = seg(off["kc"], 128)
    cin_ref[1] = seg(off["vc"], 128)
    nsw_ref[...] = seg(off["nsw"], 512).astype(BF16)
    mg_ref[...] = seg(off["mg"], 2048)
    small_ref[...] = seg(off["small"], 128)


def _inproj(x2, g, w, b, tm=256):
    n = x2.shape[0]
    assert n % tm == 0
    row = lambda width: pl.BlockSpec((tm, width), lambda i: (i, 0))
    out_shape = (
        jax.ShapeDtypeStruct((n, 1024), F32), jax.ShapeDtypeStruct((n, 512), BF16),
        jax.ShapeDtypeStruct((n, 512), F32), jax.ShapeDtypeStruct((n, 512), BF16),
        jax.ShapeDtypeStruct((2, n, 128), F32), jax.ShapeDtypeStruct((n, 512), BF16),
        jax.ShapeDtypeStruct((n, 2048), F32), jax.ShapeDtypeStruct((n, 128), F32))
    out_specs = (row(1024), row(512), row(512), row(512),
                 pl.BlockSpec((2, tm, 128), lambda i: (0, i, 0)), row(512), row(2048), row(128))
    return pl.pallas_call(
        _inproj_kernel, out_shape=out_shape, grid=(n // tm,),
        in_specs=[row(D_MODEL), _const_spec((1, D_MODEL)), _const_spec((D_MODEL, _D_IN_PAD)),
                  _const_spec((1, _D_IN_PAD))],
        out_specs=out_specs, name="inproj",
        compiler_params=pltpu.CompilerParams(dimension_semantics=("arbitrary",),
                                             vmem_limit_bytes=VMEM_LIMIT),
    )(x2, g, w, b)


def _mlstm_kernel(mqk_ref, mv_ref, mo_ref, small_ref, cw_ref, bf_ref, gh_ref, ltri_ref,
                  ym_ref, ebuf, c_st, n_st, m_st):
    L = M_CHUNK
    c = pl.program_id(1)

    @pl.when(c == 0)
    def _():
        ebuf[0:8, :] = jnp.zeros((8, 2 * M_WIDTH), F32)
        c_st[...] = jnp.zeros_like(c_st)
        n_st[...] = jnp.zeros_like(n_st)
        m_st[...] = jnp.zeros_like(m_st)

    ebuf[8:8 + L, :] = mqk_ref[0]
    conv = cw_ref[0:1, :] * ebuf[5:5 + L, :]
    for j in range(1, CONV_WIDTH):
        conv = conv + cw_ref[j:j + 1, :] * ebuf[5 + j:5 + j + L, :]
    ebuf[0:8, :] = ebuf[L:L + 8, :]
    qk = conv * jax.nn.sigmoid(conv)

    gt = small_ref[0]
    ig = gt
    fg = pltpu.roll(gt, LANES - M_HEADS, 1) + bf_ref[...]
    lf = jnp.minimum(fg, 0.0) - jnp.log1p(jnp.exp(-jnp.abs(fg)))
    bc = jnp.dot(ltri_ref[...], lf, precision=HI, preferred_element_type=F32)
    blast = bc[L - 1:L, :]
    m_prev = m_st[0:1, :]
    a_all = blast - bc + ig
    m_new = jnp.maximum(blast + m_prev, jnp.max(a_all, axis=0, keepdims=True))
    decay = jnp.exp(blast + m_prev - m_new)
    w_all = jnp.exp(a_all - m_new)
    log_inter = bc + m_prev
    bc_t = bc.T
    ig_t = ig.T

    row = lax.broadcasted_iota(jnp.int32, (L, L), 0)
    col = lax.broadcasted_iota(jnp.int32, (L, L), 1)
    causal = row >= col

    for h in range(M_HEADS):
        sl = slice(h * M_HEAD_DIM, (h + 1) * M_HEAD_DIM)
        q = qk[:, sl]
        k = qk[:, M_WIDTH + h * M_HEAD_DIM:M_WIDTH + (h + 1) * M_HEAD_DIM] * (M_HEAD_DIM ** -0.5)
        v = mv_ref[0, :, sl]
        qb = q.astype(BF16)
        kb = k.astype(BF16)
        c_prev = c_st[h]
        n_prev = n_st[h, 0:1, :]

        log_d = jnp.where(causal, bc[:, h:h + 1] - bc_t[h:h + 1, :] + ig_t[h:h + 1, :], -jnp.inf)
        li = log_inter[:, h:h + 1]
        m_i = jnp.maximum(jnp.max(log_d, axis=-1, keepdims=True), li)
        p = jnp.exp(log_d - m_i) * _dot_nt(qb, kb)
        s_inter = jnp.exp(li - m_i)
        num = _dot(p.astype(BF16), v) + s_inter * _dot(qb, c_prev.astype(BF16))
        den = jnp.sum(p, axis=-1, keepdims=True) + s_inter * jnp.sum(q * n_prev, axis=-1, keepdims=True)
        hh = num / jnp.maximum(jnp.abs(den), jnp.exp(-m_i))
        hh = hh * lax.rsqrt(jnp.mean(hh * hh, axis=-1, keepdims=True) + RMS_EPS) * gh_ref[:, sl]
        ym_ref[0, :, sl] = (jax.nn.sigmoid(mo_ref[0, :, sl]) * hh).astype(ym_ref.dtype)

        w_col = w_all[:, h:h + 1]
        dec = decay[:, h:h + 1]
        vw = (v.astype(F32) * w_col).astype(BF16)
        c_st[h] = dec * c_prev + _dot_tn(kb, vw)
        n_st[h, 0:1, :] = dec * n_prev + jnp.sum(k * w_col, axis=0, keepdims=True)

    m_st[0:1, :] = m_new


def _mlstm(mqk, mv, mo, small, cw, bf, gh, ltri):
    B, T, _ = mqk.shape
    L = M_CHUNK
    assert T % L == 0
    blk = lambda w: pl.BlockSpec((1, L, w), lambda b, c: (b, c, 0))
    return pl.pallas_call(
        _mlstm_kernel, out_shape=jax.ShapeDtypeStruct((B, T, M_WIDTH), BF16),
        grid=(B, T // L),
        in_specs=[blk(2 * M_WIDTH), blk(M_WIDTH), blk(M_WIDTH), blk(LANES),
                  _const_spec((8, 2 * M_WIDTH)), _const_spec((1, LANES)),
                  _const_spec((1, M_WIDTH)), _const_spec((L, L))],
        out_specs=blk(M_WIDTH),
        scratch_shapes=[pltpu.VMEM((L + 8, 2 * M_WIDTH), F32),
                        pltpu.VMEM((M_HEADS, M_HEAD_DIM, M_HEAD_DIM), F32),
                        pltpu.VMEM((M_HEADS, 8, M_HEAD_DIM), F32),
                        pltpu.VMEM((8, LANES), F32)],
        name="mlstm",
        compiler_params=pltpu.CompilerParams(dimension_semantics=("arbitrary", "arbitrary")),
    )(mqk, mv, mo, small, cw, bf, gh, ltri)


def _compress_kernel(x_ref, pe_ref, w1t_ref, w1b_ref, w2_ref, o_ref, *, nch):
    half = CMP_BLOCK // 2
    top = jnp.zeros((nch, 2 * CMP_HIDDEN), F32)
    bot = jnp.zeros((nch, 2 * CMP_HIDDEN), F32)
    for p in range(half):
        xp = x_ref[0, 0, pl.ds(p, nch, stride=CMP_STRIDE), :]
        top = top + _dot((xp + pe_ref[0, p:p + 1, :]).astype(BF16), w1t_ref[0, p])
        bot = bot + _dot((xp + pe_ref[0, half + p:half + p + 1, :]).astype(BF16), w1b_ref[0, p])
    hid = top + pltpu.roll(bot, nch - 1, 0)
    act = hid * jax.nn.sigmoid(hid)
    o_ref[0, 0] = _dot(act.astype(BF16), w2_ref[0]).astype(o_ref.dtype)


def _compress(cin, pe2, w1t, w1b, w2):
    _, B, T, _ = cin.shape
    nch = T // CMP_STRIDE
    sel = lambda *shape: pl.BlockSpec((1,) + shape, lambda j, b: (j,) + (0,) * len(shape))
    return pl.pallas_call(
        functools.partial(_compress_kernel, nch=nch),
        out_shape=jax.ShapeDtypeStruct((2, B, nch, LANES), BF16),
        grid=(2, B),
        in_specs=[pl.BlockSpec((1, 1, T, LANES), lambda j, b: (j, b, 0, 0)),
                  sel(CMP_BLOCK, LANES), sel(CMP_BLOCK // 2, LANES, 2 * CMP_HIDDEN),
                  sel(CMP_BLOCK // 2, LANES, 2 * CMP_HIDDEN), sel(2 * CMP_HIDDEN, LANES)],
        out_specs=pl.BlockSpec((1, 1, nch, LANES), lambda j, b: (j, b, 0, 0)),
        name="compress",
        compiler_params=pltpu.CompilerParams(dimension_semantics=("arbitrary", "arbitrary")),
    )(cin, pe2, w1t, w1b, w2)


def _bias_kernel(tbl_ref, o_ref, *, kind):
    pid = pl.program_id(0)
    a = lax.broadcasted_iota(jnp.int32, (TQ, LANES), 0)
    b = lax.broadcasted_iota(jnp.int32, (TQ, LANES), 1)
    if kind == "cmp":
        dist = pid * TQ + a - (b * CMP_STRIDE + CMP_BLOCK - 1)
    else:
        dist = (pid - 1) * TQ + a - b
    n = jnp.maximum(dist, 0)
    cnt = jnp.zeros_like(n)
    for t in _BUCKET_THR:
        cnt = cnt + jnp.where(n >= t, 1, 0)
    bucket = jnp.where(n < REL_BUCKETS // 2, n, REL_BUCKETS // 2 + cnt)
    if kind == "tok":
        madd = jnp.where((dist >= 0) & (pid > 0), 0.0, NEG)
    elif kind == "win":
        madd = jnp.where((dist >= 0) & (dist < WINDOW) & (pid > 0), 0.0, NEG)
    else:
        madd = jnp.zeros((TQ, LANES), F32)
    for h in range(N_HEADS):
        val = jnp.zeros((TQ, LANES), F32)
        for bb in range(REL_BUCKETS):
            val = jnp.where(bucket == bb, tbl_ref[h * REL_BUCKETS + bb], val)
        o_ref[0, h * TQ:(h + 1) * TQ, :] = val + madd


def _bias_tiles(tbl, n_tiles, kind):
    return pl.pallas_call(
        functools.partial(_bias_kernel, kind=kind),
        out_shape=jax.ShapeDtypeStruct((n_tiles, N_HEADS * TQ, LANES), F32),
        grid=(n_tiles,),
        in_specs=[pl.BlockSpec(memory_space=pltpu.SMEM)],
        out_specs=pl.BlockSpec((1, N_HEADS * TQ, LANES), lambda i: (i, 0, 0)),
        name="bias_" + kind,
        compiler_params=pltpu.CompilerParams(dimension_semantics=("arbitrary",)),
    )(tbl)


def _nsa_kernel(nq_ref, ks_ref, vs_ref, kw_ref, vw_ref, kc_ref, vc_ref, small_ref,
                bt_ref, wb_ref, cb_ref, ovt_ref, et_ref, yn_ref,
                vsel_sc, vwin_sc, lhs_sc, m_sc, acc_sc, *, n_cmp, n_slc, n_top, n_win):
    qi = pl.program_id(1)
    HR = N_HEADS * TQ
    GR = N_REP * TQ
    T = ks_ref.shape[1]
    lane = lax.broadcasted_iota(jnp.int32, (TQ, LANES), 1)
    lo = lane < N_HEAD_DIM

    @pl.when(qi == 0)
    def _():
        lo_t = lax.broadcasted_iota(jnp.int32, (T, LANES), 1) < N_HEAD_DIM
        one = jnp.ones((T, LANES), BF16)
        vs = vs_ref[0]
        vw = vw_ref[0]
        vsel_sc[0] = jnp.where(lo_t, vs, one)
        vsel_sc[1] = jnp.where(lo_t, one, vs)
        vwin_sc[0] = jnp.where(lo_t, vw, one)
        vwin_sc[1] = jnp.where(lo_t, one, vw)

    q_all = nq_ref[0]
    zero = jnp.zeros((TQ, LANES), BF16)
    parts = []
    for g in range(N_KV_GROUPS):
        for r in range(N_REP):
            qr = q_all[:, r * LANES:(r + 1) * LANES]
            parts.append(jnp.where(lo if g == 0 else jnp.logical_not(lo), qr, zero))
    qs = jnp.concatenate(parts, axis=0)

    row8 = lax.broadcasted_iota(jnp.int32, (HR, LANES), 0)
    lane8 = lax.broadcasted_iota(jnp.int32, (HR, LANES), 1)
    t8 = qi * TQ + (row8 & (TQ - 1))
    mc = (lane8 * CMP_STRIDE + CMP_BLOCK - 1 <= t8) & (lane8 < n_cmp)
    sc = _dot_nt(qs, kc_ref[0, 0]) + cb_ref[0]
    z = jnp.where(mc, sc, NEG)
    z = z - jnp.max(z, axis=-1, keepdims=True)
    e = jnp.where(mc, jnp.exp(z), 0.0)
    p_c = e / jnp.maximum(jnp.sum(e, axis=-1, keepdims=True), 1e-30)
    o_c = _dot(p_c.astype(BF16), vc_ref[0, 0])

    jb = lax.broadcasted_iota(jnp.int32, (n_slc, TQ), 0)
    tq = qi * TQ + lax.broadcasted_iota(jnp.int32, (n_slc, TQ), 1)
    cur = jnp.right_shift(tq, 6)
    forced = (jb == 0) | (jb == cur) | (jb == cur - 1)
    elig = jb <= cur
    for g in range(N_KV_GROUPS):
        psum = p_c[(g * N_REP) * TQ:(g * N_REP + 1) * TQ]
        for r in range(1, N_REP):
            psum = psum + p_c[(g * N_REP + r) * TQ:(g * N_REP + r + 1) * TQ]
        imp = lax.dot_general(ovt_ref[...], psum, (((1,), (1,)), ((), ())),
                              precision=HI, preferred_element_type=F32)
        score = jnp.where(elig, jnp.where(forced, BIG, imp), -BIG)
        cnt = jnp.zeros((n_slc, TQ), F32)
        for i in range(n_slc):
            si = score[i:i + 1, :]
            tie = jnp.where(jb > i, 1.0, 0.0)
            cnt = cnt + jnp.where(si > score, 1.0, jnp.where(si == score, tie, 0.0))
        sel = jnp.where((cnt < n_top) & (score > -BIG / 2), 1.0, 0.0)
        seln = jnp.concatenate([sel - 1.0, jnp.zeros((LANES - n_slc, TQ), F32)], axis=0).T.astype(BF16)
        for r in range(N_REP):
            h = g * N_REP + r
            lhs_sc[h * TQ:(h + 1) * TQ, :] = jnp.concatenate([parts[h], seln], axis=1)

    CH = 4 * TQ
    m_sc[...] = jnp.full((HR, LANES), NEG, F32)
    acc_sc[...] = jnp.zeros((HR, LANES), F32)

    def sel_chunk(c, carry):
        koff = pl.multiple_of(c * CH, CH)
        zs = []
        for s in range(2):
            ksub = ks_ref[0, pl.ds(koff + s * 2 * TQ, 2 * TQ), :]
            esub = et_ref[pl.ds(koff + s * 2 * TQ, 2 * TQ), :]
            sz = _dot_nt(lhs_sc[...], jnp.concatenate([ksub, esub], axis=1))
            for u in range(2):
                kt = 4 * c + 2 * s + u
                zs.append(sz[:, u * TQ:(u + 1) * TQ] + bt_ref[jnp.maximum(qi - kt + 1, 0)])
        zmax = jnp.maximum(jnp.maximum(zs[0], zs[1]), jnp.maximum(zs[2], zs[3]))
        m_prev = m_sc[...]
        m_new = jnp.maximum(m_prev, jnp.max(zmax, axis=-1, keepdims=True))
        alpha = jnp.exp(m_prev - m_new)
        p = jnp.concatenate([jnp.exp(z - m_new).astype(BF16) for z in zs], axis=1)
        for g in range(N_KV_GROUPS):
            rows = slice(g * GR, (g + 1) * GR)
            acc_sc[rows, :] = alpha[rows] * acc_sc[rows, :] + _dot(p[rows], vsel_sc[g, pl.ds(koff, CH), :])
        m_sc[...] = m_new
        return carry

    lax.fori_loop(0, (qi + 4) // 4, sel_chunk, 0)
    acc_s = acc_sc[...]

    w0 = jnp.maximum(qi - (n_win - 1), 0)
    woff = pl.multiple_of(w0 * TQ, TQ)
    sw = _dot_nt(qs, kw_ref[0, pl.ds(woff, n_win * TQ), :])
    zw = [sw[:, j * TQ:(j + 1) * TQ] + wb_ref[jnp.maximum(qi - w0 - j + 1, 0)] for j in range(n_win)]
    zmax = zw[0]
    for j in range(1, n_win):
        zmax = jnp.maximum(zmax, zw[j])
    mw = jnp.max(zmax, axis=-1, keepdims=True)
    pw = jnp.concatenate([jnp.exp(z - mw).astype(BF16) for z in zw], axis=1)
    acc_w = jnp.concatenate(
        [_dot(pw[g * GR:(g + 1) * GR], vwin_sc[g, pl.ds(woff, n_win * TQ), :]) for g in range(N_KV_GROUPS)],
        axis=0)

    sg = jax.nn.sigmoid(small_ref[0])
    for r in range(N_REP):
        h0, h1 = r, N_REP + r

        def pair(o):
            return jnp.where(lo, o[h0 * TQ:(h0 + 1) * TQ], o[h1 * TQ:(h1 + 1) * TQ])

        def normed(acc):
            den = jnp.where(lo, acc[h1 * TQ:(h1 + 1) * TQ], acc[h0 * TQ:(h0 + 1) * TQ])
            return pair(acc) / pltpu.roll(den, N_HEAD_DIM, 1)

        def gate(ci):
            c0 = _SMALL_NGATE + h0 * 3 + ci
            c1 = _SMALL_NGATE + h1 * 3 + ci
            return jnp.where(lo, sg[:, c0:c0 + 1], sg[:, c1:c1 + 1])

        out = gate(0) * pair(o_c) + gate(1) * normed(acc_s) + gate(2) * normed(acc_w)
        yn_ref[0, :, r * LANES:(r + 1) * LANES] = out.astype(yn_ref.dtype)


def _nsa(nq, nsw, ckv, small, bt, wb, cb, ov, emat):
    B, T, _ = nq.shape
    assert T == 2048, "single 128-wide compressed-key tile assumes T == 2048"
    nqt = T // TQ
    n_cmp = (T - CMP_BLOCK) // CMP_STRIDE + 1
    n_slc = T // SEL_BLOCK
    n_top = min(SEL_TOPK, n_slc)
    n_win = wb.shape[0] - 1
    HR = N_HEADS * TQ
    assert nqt % 4 == 0 and n_slc % 8 == 0 and n_slc <= LANES
    kv = lambda j: pl.BlockSpec((1, T, LANES), lambda b, q: (b, 0, j))
    ck = lambda j: pl.BlockSpec((1, 1, T // CMP_STRIDE, LANES), lambda b, q: (j, b, 0, 0))
    kern = functools.partial(_nsa_kernel, n_cmp=n_cmp, n_slc=n_slc, n_top=n_top, n_win=n_win)
    return pl.pallas_call(
        kern, out_shape=jax.ShapeDtypeStruct((B, T, N_WIDTH), BF16),
        grid=(B, nqt),
        in_specs=[pl.BlockSpec((1, TQ, N_WIDTH), lambda b, q: (b, q, 0)),
                  kv(0), kv(1), kv(2), kv(3), ck(0), ck(1),
                  pl.BlockSpec((1, TQ, LANES), lambda b, q: (b, q, 0)),
                  _const_spec(bt.shape), _const_spec(wb.shape),
                  pl.BlockSpec((1, HR, LANES), lambda b, q: (q, 0, 0)),
                  _const_spec(ov.shape), _const_spec(emat.shape)],
        out_specs=pl.BlockSpec((1, TQ, N_WIDTH), lambda b, q: (b, q, 0)),
        scratch_shapes=[pltpu.VMEM((N_KV_GROUPS, T, LANES), BF16),
                        pltpu.VMEM((N_KV_GROUPS, T, LANES), BF16),
                        pltpu.VMEM((HR, 2 * LANES), BF16),
                        pltpu.VMEM((HR, LANES), F32), pltpu.VMEM((HR, LANES), F32)],
        name="nsa",
        compiler_params=pltpu.CompilerParams(dimension_semantics=("arbitrary", "arbitrary"),
                                             vmem_limit_bytes=VMEM_LIMIT),
    )(nq, nsw, nsw, nsw, nsw, ckv, ckv, small, bt, wb, cb, ov, emat)


def _merge_kernel(x_ref, ym_ref, yn_ref, mg_ref, wbm_ref, wbn_ref, wo_ref, gffn_ref,
                  wg_ref, wu_ref, wd_ref, gfin_ref, o_ref, *, tf):
    bm = _dot(ym_ref[...], wbm_ref[...])
    bn = _dot(yn_ref[...], wbn_ref[...])
    mixed = jax.nn.sigmoid(mg_ref[:, :D_MODEL]) * bm + jax.nn.sigmoid(mg_ref[:, D_MODEL:]) * bn
    h = x_ref[...] + _dot(mixed.astype(BF16), wo_ref[...])
    f = (h * lax.rsqrt(jnp.mean(h * h, axis=-1, keepdims=True) + RMS_EPS) * gffn_ref[...]).astype(BF16)
    acc = jnp.zeros(h.shape, F32)
    for j in range(D_FF // tf):
        gg = _dot(f, wg_ref[:, j * tf:(j + 1) * tf])
        uu = _dot(f, wu_ref[:, j * tf:(j + 1) * tf])
        act = (gg * jax.nn.sigmoid(gg) * uu).astype(BF16)
        acc = acc + _dot(act, wd_ref[j * tf:(j + 1) * tf, :])
    h2 = h + acc
    o_ref[...] = h2 * lax.rsqrt(jnp.mean(h2 * h2, axis=-1, keepdims=True) + RMS_EPS) * gfin_ref[...]


def _merge(x2, ym, yn, mg, wbm, wbn, wo, gffn, wg, wu, wd, gfin, tm=256, tf=256):
    n = x2.shape[0]
    assert n % tm == 0 and D_FF % tf == 0
    row = lambda width: pl.BlockSpec((tm, width), lambda i: (i, 0))
    return pl.pallas_call(
        functools.partial(_merge_kernel, tf=tf),
        out_shape=jax.ShapeDtypeStruct((n, D_MODEL), F32), grid=(n // tm,),
        in_specs=[row(D_MODEL), row(M_WIDTH), row(N_WIDTH), row(N_BRANCH * D_MODEL),
                  _const_spec(wbm.shape), _const_spec(wbn.shape), _const_spec(wo.shape),
                  _const_spec(gffn.shape), _const_spec(wg.shape), _const_spec(wu.shape),
                  _const_spec(wd.shape), _const_spec(gfin.shape)],
        out_specs=row(D_MODEL), name="merge_ffn",
        compiler_params=pltpu.CompilerParams(dimension_semantics=("arbitrary",),
                                             vmem_limit_bytes=VMEM_LIMIT),
    )(x2, ym, yn, mg, wbm, wbn, wo, gffn, wg, wu, wd, gfin)


def _nsa_constants(T):
    n_cmp = (T - CMP_BLOCK) // CMP_STRIDE + 1
    n_slc = T // SEL_BLOCK
    cs = np.arange(n_cmp) * CMP_STRIDE
    ss = np.arange(n_slc) * SEL_BLOCK
    ov = np.clip(np.minimum(cs[:, None] + CMP_BLOCK, ss[None, :] + SEL_BLOCK)
                 - np.maximum(cs[:, None], ss[None, :]), 0, None) / CMP_STRIDE
    ovt = np.zeros((n_slc, LANES), np.float32)
    ovt[:, :n_cmp] = ov.T
    et = (np.arange(T)[:, None] // SEL_BLOCK == np.arange(LANES)[None, :]).astype(np.float32) * (-NEG)
    return jnp.asarray(ovt), jnp.asarray(et, dtype=BF16)


def _compress_weights(pe_cmp, w_cmp1, w_cmp2):
    half = CMP_BLOCK // 2
    dh, hid = N_HEAD_DIM, CMP_HIDDEN
    eye = jnp.eye(N_KV_GROUPS, dtype=F32)
    w1 = w_cmp1.reshape(2, CMP_BLOCK, dh, hid)
    w1bd = jnp.einsum("jpdn,gk->jpgdkn", w1, eye).reshape(2, CMP_BLOCK, N_KV_GROUPS * dh, N_KV_GROUPS * hid)
    w2bd = jnp.einsum("jnd,gk->jgnkd", w_cmp2, eye).reshape(2, N_KV_GROUPS * hid, N_KV_GROUPS * dh)
    pe2 = jnp.tile(pe_cmp, (1, 1, N_KV_GROUPS))
    return pe2, w1bd[:, :half].astype(BF16), w1bd[:, half:].astype(BF16), w2bd.astype(BF16)


def kernel(x, g_norm_mix, w_in, b_in, b_fgate, conv_qk, g_mlstm_head, pe_cmp, w_cmp1, w_cmp2,
           rel_bias, w_branch, w_out, g_norm_ffn, w_gate, w_up, w_down, g_final):
    B, T, D = x.shape
    assert D == D_MODEL and w_in.shape[0] == 1, "one residual block (DEPTH == 1)"
    N = B * T
    x2 = x.reshape(N, D)

    idx, scale = _inproj_perm()
    w_r = (jnp.take(w_in[0], jnp.asarray(idx), axis=1) * jnp.asarray(scale)).astype(BF16)
    b_r = (jnp.take(b_in[0], jnp.asarray(idx)) * jnp.asarray(scale)).reshape(1, -1)
    mqk, mv, mo, nq, cin, nsw, mg, small = _inproj(x2, g_norm_mix[0].reshape(1, D), w_r, b_r)

    cw = jnp.zeros((8, 2 * M_WIDTH), F32).at[:CONV_WIDTH].set(conv_qk[0])
    bf = jnp.zeros((1, LANES), F32).at[0, :M_HEADS].set(b_fgate[0])
    ltri = jnp.asarray(np.tril(np.ones((M_CHUNK, M_CHUNK), np.float32)))
    r3 = lambda a: a.reshape(B, T, a.shape[-1])
    ym = _mlstm(r3(mqk), r3(mv), r3(mo), r3(small), cw, bf, g_mlstm_head[0].reshape(1, M_WIDTH), ltri)

    pe2, w1t, w1b, w2 = _compress_weights(pe_cmp[0], w_cmp1[0], w_cmp2[0])
    ckv = _compress(cin.reshape(2, B, T, LANES), pe2, w1t, w1b, w2)
    tbl = rel_bias.astype(F32).T.reshape(-1)
    nqt = T // TQ
    bt = _bias_tiles(tbl, nqt + 1, "tok")
    wb = _bias_tiles(tbl, min(WINDOW // TQ + 1, nqt) + 1, "win")
    cb = _bias_tiles(tbl, nqt, "cmp")
    ov, emat = _nsa_constants(T)
    yn = _nsa(r3(nq), r3(nsw), ckv, r3(small), bt, wb, cb, ov, emat)

    perm = np.zeros((N_WIDTH,), np.int32)
    for r in range(N_REP):
        for g in range(N_KV_GROUPS):
            perm[r * 128 + g * 64:r * 128 + g * 64 + 64] = (g * N_REP + r) * N_HEAD_DIM + np.arange(N_HEAD_DIM)
    wbm = w_branch[0, 0].astype(BF16)
    wbn = jnp.take(w_branch[0, 1], jnp.asarray(perm), axis=0).astype(BF16)
    out = _merge(x2, ym.reshape(N, M_WIDTH), yn.reshape(N, N_WIDTH), mg, wbm, wbn,
                 w_out[0].astype(BF16), g_norm_ffn[0].reshape(1, D), w_gate[0].astype(BF16),
                 w_up[0].astype(BF16), w_down[0].astype(BF16), g_final.reshape(1, D))
    return out.reshape(B, T, D)
```

```python
import functools

import numpy as np
import jax
import jax.numpy as jnp
from jax import lax
from jax.experimental import pallas as pl
from jax.experimental.pallas import tpu as pltpu

F32 = jnp.float32
BF16 = jnp.bfloat16
HI = lax.Precision.HIGHEST

D_MODEL = 1024
M_HEADS = 4
M_HEAD_DIM = 128
M_WIDTH = M_HEADS * M_HEAD_DIM
M_CHUNK = 128
CONV_WIDTH = 4
N_HEADS = 8
N_KV_GROUPS = 2
N_REP = N_HEADS // N_KV_GROUPS
N_HEAD_DIM = 64
N_WIDTH = N_HEADS * N_HEAD_DIM
N_KV_WIDTH = N_KV_GROUPS * N_HEAD_DIM
CMP_BLOCK = 32
CMP_STRIDE = 16
CMP_HIDDEN = 2 * N_HEAD_DIM
SEL_BLOCK = 64
SEL_TOPK = 16
WINDOW = 512
REL_BUCKETS = 32
REL_MAX_DIST = 1024
N_BRANCH = 2
D_FF = 2816
RMS_EPS = 1e-6
BIG = 1e9
NEG = -1e30

LANES = 128
TQ = 128
VMEM_LIMIT = 56 * 1024 * 1024

_OFF_MQ, _OFF_MK, _OFF_MV, _OFF_MO = 0, 512, 1024, 1536
_OFF_MI, _OFF_MF, _OFF_NQ, _OFF_NKV = 2048, 2052, 2056, 2568
_OFF_NGATE, _OFF_MERGE, _D_IN = 3336, 3360, 5408
_SMALL_NGATE = 8

_SEGS = (("mqk", 1024), ("mv", 512), ("mo", 512), ("nq", 512), ("kc", 128),
         ("vc", 128), ("nsw", 512), ("mg", 2048), ("small", 128))
_D_IN_PAD = sum(w for _, w in _SEGS)


def _dot(a, b, **kw):
    return jnp.dot(a, b, preferred_element_type=F32, **kw)


def _dot_nt(a, b):
    return lax.dot_general(a, b, (((1,), (1,)), ((), ())), preferred_element_type=F32)


def _dot_tn(a, b):
    return lax.dot_general(a, b, (((0,), (0,)), ((), ())), preferred_element_type=F32)


def _const_spec(shape):
    nd = len(shape)
    return pl.BlockSpec(shape, lambda *_: (0,) * nd, pipeline_mode=pl.Buffered(1))


def _bucket_thresholds():
    max_exact = REL_BUCKETS // 2
    assert REL_MAX_DIST == 64 * max_exact and REL_BUCKETS - max_exact == 16
    thr = []
    for k in range(1, REL_BUCKETS - max_exact):
        t = max_exact
        while t ** 8 < (max_exact ** 8) * (2 ** (3 * k)):
            t += 1
        thr.append(t)
    return tuple(thr)


_BUCKET_THR = _bucket_thresholds()


def _inproj_perm():
    idx = np.zeros((_D_IN_PAD,), np.int32)
    scale = np.zeros((_D_IN_PAD,), np.float32)
    pos = 0

    def put(cols, s=1.0):
        nonlocal pos
        n = len(cols)
        idx[pos:pos + n] = cols
        scale[pos:pos + n] = s
        pos += n

    put(np.arange(_OFF_MQ, _OFF_MV))
    put(np.arange(_OFF_MV, _OFF_MO))
    put(np.arange(_OFF_MO, _OFF_MI))
    nq = np.zeros((N_WIDTH,), np.int32)
    for r in range(N_REP):
        for g in range(N_KV_GROUPS):
            for d in range(N_HEAD_DIM):
                nq[r * 128 + g * 64 + d] = _OFF_NQ + (g * N_REP + r) * N_HEAD_DIM + d
    put(nq, N_HEAD_DIM ** -0.5)
    put(np.arange(_OFF_NKV, _OFF_NKV + 768))
    put(np.arange(_OFF_MERGE, _D_IN))
    put(np.arange(_OFF_MI, _OFF_MI + 8))
    put(np.arange(_OFF_NGATE, _OFF_NGATE + 24))
    pos += LANES - 32
    assert pos == _D_IN_PAD
    return idx, scale


def _inproj_kernel(x_ref, g_ref, w_ref, b_ref, mqk_ref, mv_ref, mo_ref, nq_ref,
                   cin_ref, nsw_ref, mg_ref, small_ref):
    x = x_ref[...]
    ms = jnp.mean(x * x, axis=-1, keepdims=True)
    u = (x * lax.rsqrt(ms + RMS_EPS) * g_ref[...]).astype(BF16)

    def seg(a, n):
        return _dot(u, w_ref[:, a:a + n]) + b_ref[:, a:a + n]

    off = {}
    pos = 0
    for name, w in _SEGS:
        off[name] = pos
        pos += w
    mqk_ref[...] = seg(off["mqk"], 1024).astype(BF16)
    mv_ref[...] = seg(off["mv"], 512).astype(BF16)
    mo_ref[...] = seg(off["mo"], 512).astype(BF16)
    nq_ref[...] = seg(off["nq"], 512).astype(BF16)
    cin_ref[0] = seg(off["kc"], 128)
    cin_ref[1] = seg(off["vc"], 128)
    nsw_ref[...] = seg(off["nsw"], 512).astype(BF16)
    mg_ref[...] = seg(off["mg"], 2048).astype(BF16)
    small_ref[...] = seg(off["small"], 128)


def _inproj(x2, g, w, b, tm=512):
    n = x2.shape[0]
    assert n % tm == 0
    row = lambda width: pl.BlockSpec((tm, width), lambda i: (i, 0))
    out_shape = (
        jax.ShapeDtypeStruct((n, 1024), BF16), jax.ShapeDtypeStruct((n, 512), BF16),
        jax.ShapeDtypeStruct((n, 512), BF16), jax.ShapeDtypeStruct((n, 512), BF16),
        jax.ShapeDtypeStruct((2, n, 128), F32), jax.ShapeDtypeStruct((n, 512), BF16),
        jax.ShapeDtypeStruct((n, 2048), BF16), jax.ShapeDtypeStruct((n, 128), F32))
    out_specs = (row(1024), row(512), row(512), row(512),
                 pl.BlockSpec((2, tm, 128), lambda i: (0, i, 0)), row(512), row(2048), row(128))
    return pl.pallas_call(
        _inproj_kernel, out_shape=out_shape, grid=(n // tm,),
        in_specs=[row(D_MODEL), _const_spec((1, D_MODEL)), _const_spec((D_MODEL, _D_IN_PAD)),
                  _const_spec((1, _D_IN_PAD))],
        out_specs=out_specs, name="inproj",
        compiler_params=pltpu.CompilerParams(dimension_semantics=("arbitrary",),
                                             vmem_limit_bytes=VMEM_LIMIT),
    )(x2, g, w, b)


def _mlstm_kernel(mqk_ref, mv_ref, mo_ref, small_ref, cw_ref, bf_ref, gh_ref, ltri_ref,
                  ym_ref, ebuf, c_st, n_st, m_st):
    L = M_CHUNK
    c = pl.program_id(1)

    @pl.when(c == 0)
    def _():
        ebuf[0:8, :] = jnp.zeros((8, 2 * M_WIDTH), F32)
        c_st[...] = jnp.zeros_like(c_st)
        n_st[...] = jnp.zeros_like(n_st)
        m_st[...] = jnp.zeros_like(m_st)

    ebuf[8:8 + L, :] = mqk_ref[0].astype(F32)
    conv = cw_ref[0:1, :] * ebuf[5:5 + L, :]
    for j in range(1, CONV_WIDTH):
        conv = conv + cw_ref[j:j + 1, :] * ebuf[5 + j:5 + j + L, :]
    ebuf[0:8, :] = ebuf[L:L + 8, :]
    qk = conv * jax.nn.sigmoid(conv)

    gt = small_ref[0]
    ig = gt
    fg = pltpu.roll(gt, LANES - M_HEADS, 1) + bf_ref[...]
    lf = jnp.minimum(fg, 0.0) - jnp.log1p(jnp.exp(-jnp.abs(fg)))
    bc = jnp.dot(ltri_ref[...], lf, precision=HI, preferred_element_type=F32)
    blast = bc[L - 1:L, :]
    m_prev = m_st[0:1, :]
    a_all = blast - bc + ig
    m_new = jnp.maximum(blast + m_prev, jnp.max(a_all, axis=0, keepdims=True))
    decay = jnp.exp(blast + m_prev - m_new)
    w_all = jnp.exp(a_all - m_new)
    log_inter = bc + m_prev
    bc_t = bc.T
    ig_t = ig.T

    row = lax.broadcasted_iota(jnp.int32, (L, L), 0)
    col = lax.broadcasted_iota(jnp.int32, (L, L), 1)
    causal = row >= col

    for h in range(M_HEADS):
        sl = slice(h * M_HEAD_DIM, (h + 1) * M_HEAD_DIM)
        q = qk[:, sl]
        k = qk[:, M_WIDTH + h * M_HEAD_DIM:M_WIDTH + (h + 1) * M_HEAD_DIM] * (M_HEAD_DIM ** -0.5)
        v = mv_ref[0, :, sl]
        qb = q.astype(BF16)
        kb = k.astype(BF16)
        c_prev = c_st[h]
        n_prev = n_st[h, 0:1, :]

        log_d = jnp.where(causal, bc[:, h:h + 1] - bc_t[h:h + 1, :] + ig_t[h:h + 1, :], -jnp.inf)
        li = log_inter[:, h:h + 1]
        m_i = jnp.maximum(jnp.max(log_d, axis=-1, keepdims=True), li)
        p = jnp.exp(log_d - m_i) * _dot_nt(qb, kb)
        s_inter = jnp.exp(li - m_i)
        num = _dot(p.astype(BF16), v) + s_inter * _dot(qb, c_prev.astype(BF16))
        den = jnp.sum(p, axis=-1, keepdims=True) + s_inter * jnp.sum(q * n_prev, axis=-1, keepdims=True)
        hh = num / jnp.maximum(jnp.abs(den), jnp.exp(-m_i))
        hh = hh * lax.rsqrt(jnp.mean(hh * hh, axis=-1, keepdims=True) + RMS_EPS) * gh_ref[:, sl]
        ym_ref[0, :, sl] = (jax.nn.sigmoid(mo_ref[0, :, sl].astype(F32)) * hh).astype(ym_ref.dtype)

        w_col = w_all[:, h:h + 1]
        dec = decay[:, h:h + 1]
        vw = (v.astype(F32) * w_col).astype(BF16)
        c_st[h] = dec * c_prev + _dot_tn(kb, vw)
        n_st[h, 0:1, :] = dec * n_prev + jnp.sum(k * w_col, axis=0, keepdims=True)

    m_st[0:1, :] = m_new


def _mlstm(mqk, mv, mo, small, cw, bf, gh, ltri):
    B, T, _ = mqk.shape
    L = M_CHUNK
    assert T % L == 0
    blk = lambda w: pl.BlockSpec((1, L, w), lambda b, c: (b, c, 0))
    return pl.pallas_call(
        _mlstm_kernel, out_shape=jax.ShapeDtypeStruct((B, T, M_WIDTH), BF16),
        grid=(B, T // L),
        in_specs=[blk(2 * M_WIDTH), blk(M_WIDTH), blk(M_WIDTH), blk(LANES),
                  _const_spec((8, 2 * M_WIDTH)), _const_spec((1, LANES)),
                  _const_spec((1, M_WIDTH)), _const_spec((L, L))],
        out_specs=blk(M_WIDTH),
        scratch_shapes=[pltpu.VMEM((L + 8, 2 * M_WIDTH), F32),
                        pltpu.VMEM((M_HEADS, M_HEAD_DIM, M_HEAD_DIM), F32),
                        pltpu.VMEM((M_HEADS, 8, M_HEAD_DIM), F32),
                        pltpu.VMEM((8, LANES), F32)],
        name="mlstm",
        compiler_params=pltpu.CompilerParams(dimension_semantics=("arbitrary", "arbitrary")),
    )(mqk, mv, mo, small, cw, bf, gh, ltri)


def _compress_kernel(x_ref, pe_ref, w1t_ref, w1b_ref, w2_ref, o_ref, *, nch):
    half = CMP_BLOCK // 2
    top = jnp.zeros((nch, 2 * CMP_HIDDEN), F32)
    bot = jnp.zeros((nch, 2 * CMP_HIDDEN), F32)
    for p in range(half):
        xp = x_ref[0, 0, pl.ds(p, nch, stride=CMP_STRIDE), :]
        top = top + _dot((xp + pe_ref[0, p:p + 1, :]).astype(BF16), w1t_ref[0, p])
        bot = bot + _dot((xp + pe_ref[0, half + p:half + p + 1, :]).astype(BF16), w1b_ref[0, p])
    hid = top + pltpu.roll(bot, nch - 1, 0)
    act = hid * jax.nn.sigmoid(hid)
    o_ref[0, 0] = _dot(act.astype(BF16), w2_ref[0]).astype(o_ref.dtype)


def _compress(cin, pe2, w1t, w1b, w2):
    _, B, T, _ = cin.shape
    nch = T // CMP_STRIDE
    sel = lambda *shape: pl.BlockSpec((1,) + shape, lambda j, b: (j,) + (0,) * len(shape))
    return pl.pallas_call(
        functools.partial(_compress_kernel, nch=nch),
        out_shape=jax.ShapeDtypeStruct((2, B, nch, LANES), BF16),
        grid=(2, B),
        in_specs=[pl.BlockSpec((1, 1, T, LANES), lambda j, b: (j, b, 0, 0)),
                  sel(CMP_BLOCK, LANES), sel(CMP_BLOCK // 2, LANES, 2 * CMP_HIDDEN),
                  sel(CMP_BLOCK // 2, LANES, 2 * CMP_HIDDEN), sel(2 * CMP_HIDDEN, LANES)],
        out_specs=pl.BlockSpec((1, 1, nch, LANES), lambda j, b: (j, b, 0, 0)),
        name="compress",
        compiler_params=pltpu.CompilerParams(dimension_semantics=("arbitrary", "arbitrary")),
    )(cin, pe2, w1t, w1b, w2)


def _bias_kernel(tbl_ref, o_ref, *, kind, n_cmp):
    pid = pl.program_id(0)
    a = lax.broadcasted_iota(jnp.int32, (TQ, LANES), 0)
    b = lax.broadcasted_iota(jnp.int32, (TQ, LANES), 1)
    if kind == "cmp":
        dist = pid * TQ + a - (b * CMP_STRIDE + CMP_BLOCK - 1)
    else:
        dist = (pid - 1) * TQ + a - b
    n = jnp.maximum(dist, 0)
    cnt = jnp.zeros_like(n)
    for t in _BUCKET_THR:
        cnt = cnt + jnp.where(n >= t, 1, 0)
    bucket = jnp.where(n < REL_BUCKETS // 2, n, REL_BUCKETS // 2 + cnt)
    if kind == "tok":
        madd = jnp.where((dist >= 0) & (pid > 0), 0.0, NEG)
    elif kind == "win":
        madd = jnp.where((dist >= 0) & (dist < WINDOW) & (pid > 0), 0.0, NEG)
    else:
        madd = jnp.where((dist >= 0) & (b < n_cmp), 0.0, NEG)
    for h in range(N_HEADS):
        val = jnp.zeros((TQ, LANES), F32)
        for bb in range(REL_BUCKETS):
            val = jnp.where(bucket == bb, tbl_ref[h * REL_BUCKETS + bb], val)
        o_ref[0, h * TQ:(h + 1) * TQ, :] = val + madd


def _bias_tiles(tbl, n_tiles, kind, n_cmp=0):
    return pl.pallas_call(
        functools.partial(_bias_kernel, kind=kind, n_cmp=n_cmp),
        out_shape=jax.ShapeDtypeStruct((n_tiles, N_HEADS * TQ, LANES), F32),
        grid=(n_tiles,),
        in_specs=[pl.BlockSpec(memory_space=pltpu.SMEM)],
        out_specs=pl.BlockSpec((1, N_HEADS * TQ, LANES), lambda i: (i, 0, 0)),
        name="bias_" + kind,
        compiler_params=pltpu.CompilerParams(dimension_semantics=("arbitrary",)),
    )(tbl)


def _nsa_kernel(nq_ref, ks_ref, vs_ref, kw_ref, vw_ref, kc_ref, vc_ref, small_ref,
                bt_ref, wb_ref, cb_ref, ovt_ref, et_ref, yn_ref,
                vsel_sc, vwin_sc, lhs_sc, m_sc, acc_sc, part_sc, *, n_slc, n_top, n_win):
    qi = pl.program_id(1)
    HR = N_HEADS * TQ
    GR = N_REP * TQ
    T = ks_ref.shape[1]
    lane = lax.broadcasted_iota(jnp.int32, (TQ, LANES), 1)
    lo = lane < N_HEAD_DIM

    @pl.when(qi == 0)
    def _():
        lo_t = lax.broadcasted_iota(jnp.int32, (T, LANES), 1) < N_HEAD_DIM
        one = jnp.ones((T, LANES), BF16)
        vs = vs_ref[0]
        vw = vw_ref[0]
        vsel_sc[0] = jnp.where(lo_t, vs, one)
        vsel_sc[1] = jnp.where(lo_t, one, vs)
        vwin_sc[0] = jnp.where(lo_t, vw, one)
        vwin_sc[1] = jnp.where(lo_t, one, vw)

    q_all = nq_ref[0]
    zero = jnp.zeros((TQ, LANES), BF16)
    parts = []
    for g in range(N_KV_GROUPS):
        for r in range(N_REP):
            qr = q_all[:, r * LANES:(r + 1) * LANES]
            parts.append(jnp.where(lo if g == 0 else jnp.logical_not(lo), qr, zero))
    qs = jnp.concatenate(parts, axis=0)

    z = _dot_nt(qs, kc_ref[0, 0]) + cb_ref[0]
    e = jnp.exp(z - jnp.max(z, axis=-1, keepdims=True))
    t8 = qi * TQ + (lax.broadcasted_iota(jnp.int32, (HR, 1), 0) & (TQ - 1))
    rinv = jnp.where(t8 >= CMP_BLOCK - 1,
                     1.0 / jnp.maximum(jnp.sum(e, axis=-1, keepdims=True), 1e-30), 0.0)
    p_c = e * rinv
    o_c = _dot(p_c.astype(BF16), vc_ref[0, 0])

    w0 = jnp.maximum(qi - (n_win - 1), 0)
    woff = pl.multiple_of(w0 * TQ, TQ)
    sw = _dot_nt(qs, kw_ref[0, pl.ds(woff, n_win * TQ), :])
    zw = [sw[:, j * TQ:(j + 1) * TQ] + wb_ref[jnp.maximum(qi - w0 - j + 1, 0)] for j in range(n_win)]
    zmax = zw[0]
    for j in range(1, n_win):
        zmax = jnp.maximum(zmax, zw[j])
    mw = jnp.max(zmax, axis=-1, keepdims=True)
    pw = jnp.concatenate([jnp.exp(zj - mw).astype(BF16) for zj in zw], axis=1)
    acc_w = jnp.concatenate(
        [_dot(pw[g * GR:(g + 1) * GR], vwin_sc[g, pl.ds(woff, n_win * TQ), :]) for g in range(N_KV_GROUPS)],
        axis=0)

    sg = jax.nn.sigmoid(small_ref[0])

    def pair(o, r):
        return jnp.where(lo, o[r * TQ:(r + 1) * TQ], o[(N_REP + r) * TQ:(N_REP + r + 1) * TQ])

    def normed(acc, r):
        den = jnp.where(lo, acc[(N_REP + r) * TQ:(N_REP + r + 1) * TQ], acc[r * TQ:(r + 1) * TQ])
        return pair(acc, r) / pltpu.roll(den, N_HEAD_DIM, 1)

    def gate(r, ci):
        c0 = _SMALL_NGATE + r * 3 + ci
        c1 = _SMALL_NGATE + (N_REP + r) * 3 + ci
        return jnp.where(lo, sg[:, c0:c0 + 1], sg[:, c1:c1 + 1])

    for r in range(N_REP):
        part_sc[r] = gate(r, 0) * pair(o_c, r) + gate(r, 2) * normed(acc_w, r)

    jb = lax.broadcasted_iota(jnp.int32, (n_slc, TQ), 0)
    tq = qi * TQ + lax.broadcasted_iota(jnp.int32, (n_slc, TQ), 1)
    cur = jnp.right_shift(tq, 6)
    forced = (jb == 0) | (jb == cur) | (jb == cur - 1)
    elig = jb <= cur
    for g in range(N_KV_GROUPS):
        psum = p_c[(g * N_REP) * TQ:(g * N_REP + 1) * TQ]
        for r in range(1, N_REP):
            psum = psum + p_c[(g * N_REP + r) * TQ:(g * N_REP + r + 1) * TQ]
        imp = lax.dot_general(ovt_ref[...], psum, (((1,), (1,)), ((), ())),
                              precision=HI, preferred_element_type=F32)
        score = jnp.where(elig, jnp.where(forced, BIG, imp), -BIG)
        cnt = jnp.zeros((n_slc, TQ), F32)
        for i in range(n_slc):
            si = score[i:i + 1, :]
            tie = jnp.where(jb > i, 1.0, 0.0)
            cnt = cnt + jnp.where(si > score, 1.0, jnp.where(si == score, tie, 0.0))
        sel = jnp.where((cnt < n_top) & (score > -BIG / 2), 1.0, 0.0)
        seln = jnp.concatenate([sel - 1.0, jnp.zeros((LANES - n_slc, TQ), F32)], axis=0).T.astype(BF16)
        for r in range(N_REP):
            h = g * N_REP + r
            lhs_sc[h * TQ:(h + 1) * TQ, :] = jnp.concatenate([parts[h], seln], axis=1)

    CH = 4 * TQ
    m_sc[...] = jnp.full((HR, LANES), NEG, F32)
    acc_sc[...] = jnp.zeros((HR, LANES), F32)

    def sel_chunk(c, carry):
        koff = pl.multiple_of(c * CH, CH)
        zs = []
        for s in range(2):
            ksub = ks_ref[0, pl.ds(koff + s * 2 * TQ, 2 * TQ), :]
            esub = et_ref[pl.ds(koff + s * 2 * TQ, 2 * TQ), :]
            sz = _dot_nt(lhs_sc[...], jnp.concatenate([ksub, esub], axis=1))
            for u in range(2):
                kt = 4 * c + 2 * s + u
                zs.append(sz[:, u * TQ:(u + 1) * TQ] + bt_ref[jnp.maximum(qi - kt + 1, 0)])
        zmax = jnp.maximum(jnp.maximum(zs[0], zs[1]), jnp.maximum(zs[2], zs[3]))
        m_prev = m_sc[...]
        m_new = jnp.maximum(m_prev, jnp.max(zmax, axis=-1, keepdims=True))
        alpha = jnp.exp(m_prev - m_new)
        p = jnp.concatenate([jnp.exp(z - m_new).astype(BF16) for z in zs], axis=1)
        for g in range(N_KV_GROUPS):
            rows = slice(g * GR, (g + 1) * GR)
            acc_sc[rows, :] = alpha[rows] * acc_sc[rows, :] + _dot(p[rows], vsel_sc[g, pl.ds(koff, CH), :])
        m_sc[...] = m_new
        return carry

    lax.fori_loop(0, (qi + 4) // 4, sel_chunk, 0)

    acc_s = acc_sc[...]
    for r in range(N_REP):
        out = part_sc[r] + gate(r, 1) * normed(acc_s, r)
        yn_ref[0, :, r * LANES:(r + 1) * LANES] = out.astype(yn_ref.dtype)


def _nsa(nq, nsw, ckv, small, bt, wb, cb, ov, emat):
    B, T, _ = nq.shape
    assert T == 2048, "single 128-wide compressed-key tile assumes T == 2048"
    nqt = T // TQ
    n_cmp = (T - CMP_BLOCK) // CMP_STRIDE + 1
    n_slc = T // SEL_BLOCK
    n_top = min(SEL_TOPK, n_slc)
    n_win = wb.shape[0] - 1
    HR = N_HEADS * TQ
    assert nqt % 4 == 0 and n_slc % 8 == 0 and n_slc <= LANES
    kv = lambda j: pl.BlockSpec((1, T, LANES), lambda b, q: (b, 0, j))
    ck = lambda j: pl.BlockSpec((1, 1, T // CMP_STRIDE, LANES), lambda b, q: (j, b, 0, 0))
    kern = functools.partial(_nsa_kernel, n_slc=n_slc, n_top=n_top, n_win=n_win)
    return pl.pallas_call(
        kern, out_shape=jax.ShapeDtypeStruct((B, T, N_WIDTH), BF16),
        grid=(B, nqt),
        in_specs=[pl.BlockSpec((1, TQ, N_WIDTH), lambda b, q: (b, q, 0)),
                  kv(0), kv(1), kv(2), kv(3), ck(0), ck(1),
                  pl.BlockSpec((1, TQ, LANES), lambda b, q: (b, q, 0)),
                  _const_spec(bt.shape), _const_spec(wb.shape),
                  pl.BlockSpec((1, HR, LANES), lambda b, q: (q, 0, 0)),
                  _const_spec(ov.shape), _const_spec(emat.shape)],
        out_specs=pl.BlockSpec((1, TQ, N_WIDTH), lambda b, q: (b, q, 0)),
        scratch_shapes=[pltpu.VMEM((N_KV_GROUPS, T, LANES), BF16),
                        pltpu.VMEM((N_KV_GROUPS, T, LANES), BF16),
                        pltpu.VMEM((HR, 2 * LANES), BF16),
                        pltpu.VMEM((HR, LANES), F32), pltpu.VMEM((HR, LANES), F32),
                        pltpu.VMEM((N_REP, TQ, LANES), F32)],
        name="nsa",
        compiler_params=pltpu.CompilerParams(dimension_semantics=("arbitrary", "arbitrary"),
                                             vmem_limit_bytes=VMEM_LIMIT),
    )(nq, nsw, nsw, nsw, nsw, ckv, ckv, small, bt, wb, cb, ov, emat)


def _merge_kernel(x_ref, ym_ref, yn_ref, mg_ref, wbm_ref, wbn_ref, wo_ref, gffn_ref,
                  wg_ref, wu_ref, wd_ref, gfin_ref, o_ref, *, tf):
    bm = _dot(ym_ref[...], wbm_ref[...])
    bn = _dot(yn_ref[...], wbn_ref[...])
    mixed = (jax.nn.sigmoid(mg_ref[:, :D_MODEL].astype(F32)) * bm
             + jax.nn.sigmoid(mg_ref[:, D_MODEL:].astype(F32)) * bn)
    h = x_ref[...] + _dot(mixed.astype(BF16), wo_ref[...])
    f = (h * lax.rsqrt(jnp.mean(h * h, axis=-1, keepdims=True) + RMS_EPS) * gffn_ref[...]).astype(BF16)
    acc = jnp.zeros(h.shape, F32)
    for j in range(D_FF // tf):
        gg = _dot(f, wg_ref[:, j * tf:(j + 1) * tf])
        uu = _dot(f, wu_ref[:, j * tf:(j + 1) * tf])
        act = (gg * jax.nn.sigmoid(gg) * uu).astype(BF16)
        acc = acc + _dot(act, wd_ref[j * tf:(j + 1) * tf, :])
    h2 = h + acc
    o_ref[...] = h2 * lax.rsqrt(jnp.mean(h2 * h2, axis=-1, keepdims=True) + RMS_EPS) * gfin_ref[...]


def _merge(x2, ym, yn, mg, wbm, wbn, wo, gffn, wg, wu, wd, gfin, tm=512, tf=256):
    n = x2.shape[0]
    assert n % tm == 0 and D_FF % tf == 0
    row = lambda width: pl.BlockSpec((tm, width), lambda i: (i, 0))
    return pl.pallas_call(
        functools.partial(_merge_kernel, tf=tf),
        out_shape=jax.ShapeDtypeStruct((n, D_MODEL), F32), grid=(n // tm,),
        in_specs=[row(D_MODEL), row(M_WIDTH), row(N_WIDTH), row(N_BRANCH * D_MODEL),
                  _const_spec(wbm.shape), _const_spec(wbn.shape), _const_spec(wo.shape),
                  _const_spec(gffn.shape), _const_spec(wg.shape), _const_spec(wu.shape),
                  _const_spec(wd.shape), _const_spec(gfin.shape)],
        out_specs=row(D_MODEL), name="merge_ffn",
        compiler_params=pltpu.CompilerParams(dimension_semantics=("arbitrary",),
                                             vmem_limit_bytes=VMEM_LIMIT),
    )(x2, ym, yn, mg, wbm, wbn, wo, gffn, wg, wu, wd, gfin)


def _nsa_constants(T):
    n_cmp = (T - CMP_BLOCK) // CMP_STRIDE + 1
    n_slc = T // SEL_BLOCK
    cs = np.arange(n_cmp) * CMP_STRIDE
    ss = np.arange(n_slc) * SEL_BLOCK
    ov = np.clip(np.minimum(cs[:, None] + CMP_BLOCK, ss[None, :] + SEL_BLOCK)
                 - np.maximum(cs[:, None], ss[None, :]), 0, None) / CMP_STRIDE
    ovt = np.zeros((n_slc, LANES), np.float32)
    ovt[:, :n_cmp] = ov.T
    et = (np.arange(T)[:, None] // SEL_BLOCK == np.arange(LANES)[None, :]).astype(np.float32) * (-NEG)
    return jnp.asarray(ovt), jnp.asarray(et, dtype=BF16)


def _compress_weights(pe_cmp, w_cmp1, w_cmp2):
    half = CMP_BLOCK // 2
    dh, hid = N_HEAD_DIM, CMP_HIDDEN
    eye = jnp.eye(N_KV_GROUPS, dtype=F32)
    w1 = w_cmp1.reshape(2, CMP_BLOCK, dh, hid)
    w1bd = jnp.einsum("jpdn,gk->jpgdkn", w1, eye).reshape(2, CMP_BLOCK, N_KV_GROUPS * dh, N_KV_GROUPS * hid)
    w2bd = jnp.einsum("jnd,gk->jgnkd", w_cmp2, eye).reshape(2, N_KV_GROUPS * hid, N_KV_GROUPS * dh)
    pe2 = jnp.tile(pe_cmp, (1, 1, N_KV_GROUPS))
    return pe2, w1bd[:, :half].astype(BF16), w1bd[:, half:].astype(BF16), w2bd.astype(BF16)


def kernel(x, g_norm_mix, w_in, b_in, b_fgate, conv_qk, g_mlstm_head, pe_cmp, w_cmp1, w_cmp2,
           rel_bias, w_branch, w_out, g_norm_ffn, w_gate, w_up, w_down, g_final):
    B, T, D = x.shape
    assert D == D_MODEL and w_in.shape[0] == 1, "one residual block (DEPTH == 1)"
    N = B * T
    x2 = x.reshape(N, D)

    idx, scale = _inproj_perm()
    w_r = (jnp.take(w_in[0], jnp.asarray(idx), axis=1) * jnp.asarray(scale)).astype(BF16)
    b_r = (jnp.take(b_in[0], jnp.asarray(idx)) * jnp.asarray(scale)).reshape(1, -1)
    mqk, mv, mo, nq, cin, nsw, mg, small = _inproj(x2, g_norm_mix[0].reshape(1, D), w_r, b_r)

    cw = jnp.zeros((8, 2 * M_WIDTH), F32).at[:CONV_WIDTH].set(conv_qk[0])
    bf = jnp.zeros((1, LANES), F32).at[0, :M_HEADS].set(b_fgate[0])
    ltri = jnp.asarray(np.tril(np.ones((M_CHUNK, M_CHUNK), np.float32)))
    r3 = lambda a: a.reshape(B, T, a.shape[-1])
    ym = _mlstm(r3(mqk), r3(mv), r3(mo), r3(small), cw, bf, g_mlstm_head[0].reshape(1, M_WIDTH), ltri)

    pe2, w1t, w1b, w2 = _compress_weights(pe_cmp[0], w_cmp1[0], w_cmp2[0])
    ckv = _compress(cin.reshape(2, B, T, LANES), pe2, w1t, w1b, w2)
    tbl = rel_bias.astype(F32).T.reshape(-1)
    nqt = T // TQ
    bt = _bias_tiles(tbl, nqt + 1, "tok")
    wb = _bias_tiles(tbl, min(WINDOW // TQ + 1, nqt) + 1, "win")
    cb = _bias_tiles(tbl, nqt, "cmp", n_cmp=(T - CMP_BLOCK) // CMP_STRIDE + 1)
    ov, emat = _nsa_constants(T)
    yn = _nsa(r3(nq), r3(nsw), ckv, r3(small), bt, wb, cb, ov, emat)

    perm = np.zeros((N_WIDTH,), np.int32)
    for r in range(N_REP):
        for g in range(N_KV_GROUPS):
            perm[r * 128 + g * 64:r * 128 + g * 64 + 64] = (g * N_REP + r) * N_HEAD_DIM + np.arange(N_HEAD_DIM)
    wbm = w_branch[0, 0].astype(BF16)
    wbn = jnp.take(w_branch[0, 1], jnp.asarray(perm), axis=0).astype(BF16)
    out = _merge(x2, ym.reshape(N, M_WIDTH), yn.reshape(N, N_WIDTH), mg, wbm, wbn,
                 w_out[0].astype(BF16), g_norm_ffn[0].reshape(1, D), w_gate[0].astype(BF16),
                 w_up[0].astype(BF16), w_down[0].astype(BF16), g_final.reshape(1, D))
    return out.reshape(B, T, D)
```

```python
import functools

import numpy as np
import jax
import jax.numpy as jnp
from jax import lax
from jax.experimental import pallas as pl
from jax.experimental.pallas import tpu as pltpu

F32 = jnp.float32
BF16 = jnp.bfloat16
HI = lax.Precision.HIGHEST

D_MODEL = 1024
M_HEADS = 4
M_HEAD_DIM = 128
M_WIDTH = M_HEADS * M_HEAD_DIM
M_CHUNK = 128
CONV_WIDTH = 4
N_HEADS = 8
N_KV_GROUPS = 2
N_REP = N_HEADS // N_KV_GROUPS
N_HEAD_DIM = 64
N_WIDTH = N_HEADS * N_HEAD_DIM
N_KV_WIDTH = N_KV_GROUPS * N_HEAD_DIM
CMP_BLOCK = 32
CMP_STRIDE = 16
CMP_HIDDEN = 2 * N_HEAD_DIM
SEL_BLOCK = 64
SEL_TOPK = 16
WINDOW = 512
REL_BUCKETS = 32
REL_MAX_DIST = 1024
N_BRANCH = 2
D_FF = 2816
RMS_EPS = 1e-6
BIG = 1e9
NEG = -1e30

LANES = 128
TQ = 128
NSA_BATCH = 1
VMEM_LIMIT = 56 * 1024 * 1024

_OFF_MQ, _OFF_MK, _OFF_MV, _OFF_MO = 0, 512, 1024, 1536
_OFF_MI, _OFF_MF, _OFF_NQ, _OFF_NKV = 2048, 2052, 2056, 2568
_OFF_NGATE, _OFF_MERGE, _D_IN = 3336, 3360, 5408
_SMALL_NGATE = 8

_SEGS = (("mqk", 1024), ("mv", 512), ("mo", 512), ("nq", 512), ("kc", 128),
         ("vc", 128), ("nsw", 768), ("mg", 2048), ("small", 128))
_D_IN_PAD = sum(w for _, w in _SEGS)
LOG2E = 1.4426950408889634


def _dot(a, b, **kw):
    return jnp.dot(a, b, preferred_element_type=F32, **kw)


def _dot_nt(a, b):
    return lax.dot_general(a, b, (((1,), (1,)), ((), ())), preferred_element_type=F32)


def _dot_tn(a, b):
    return lax.dot_general(a, b, (((0,), (0,)), ((), ())), preferred_element_type=F32)


def _const_spec(shape):
    nd = len(shape)
    return pl.BlockSpec(shape, lambda *_: (0,) * nd, pipeline_mode=pl.Buffered(1))


def _bucket_thresholds():
    max_exact = REL_BUCKETS // 2
    assert REL_MAX_DIST == 64 * max_exact and REL_BUCKETS - max_exact == 16
    thr = []
    for k in range(1, REL_BUCKETS - max_exact):
        t = max_exact
        while t ** 8 < (max_exact ** 8) * (2 ** (3 * k)):
            t += 1
        thr.append(t)
    return tuple(thr)


_BUCKET_THR = _bucket_thresholds()


def _inproj_perm():
    idx = np.zeros((_D_IN_PAD,), np.int32)
    scale = np.zeros((_D_IN_PAD,), np.float32)
    const = np.zeros((_D_IN_PAD,), np.float32)
    pos = 0

    def put(cols, s=1.0):
        nonlocal pos
        n = len(cols)
        idx[pos:pos + n] = cols
        scale[pos:pos + n] = s
        pos += n

    def ones(n):
        nonlocal pos
        const[pos:pos + n] = 1.0
        pos += n

    put(np.arange(_OFF_MQ, _OFF_MV))
    put(np.arange(_OFF_MV, _OFF_MO))
    put(np.arange(_OFF_MO, _OFF_MI))
    nq = np.zeros((N_WIDTH,), np.int32)
    for r in range(N_REP):
        for g in range(N_KV_GROUPS):
            for d in range(N_HEAD_DIM):
                nq[r * 128 + g * 64 + d] = _OFF_NQ + (g * N_REP + r) * N_HEAD_DIM + d
    put(nq, N_HEAD_DIM ** -0.5 * LOG2E)
    kv = lambda j, g: np.arange(_OFF_NKV + (j * N_KV_GROUPS + g) * N_HEAD_DIM,
                                _OFF_NKV + (j * N_KV_GROUPS + g + 1) * N_HEAD_DIM)
    put(np.arange(_OFF_NKV, _OFF_NKV + 256))
    put(np.concatenate([kv(2, 0), kv(2, 1)]))
    put(np.concatenate([kv(4, 0), kv(4, 1)]))
    for j in (3, 5):
        put(kv(j, 0))
        ones(N_HEAD_DIM)
        ones(N_HEAD_DIM)
        put(kv(j, 1))
    put(np.arange(_OFF_MERGE, _D_IN))
    put(np.arange(_OFF_MI, _OFF_MI + 8))
    put(np.arange(_OFF_NGATE, _OFF_NGATE + 24))
    pos += LANES - 32
    assert pos == _D_IN_PAD
    return idx, scale, const


def _inproj_kernel(x_ref, g_ref, w_ref, b_ref, mqk_ref, mv_ref, mo_ref, nq_ref,
                   cin_ref, nsw_ref, mg_ref, small_ref):
    x = x_ref[...]
    ms = jnp.mean(x * x, axis=-1, keepdims=True)
    u = (x * lax.rsqrt(ms + RMS_EPS) * g_ref[...]).astype(BF16)

    def seg(a, n):
        return _dot(u, w_ref[:, a:a + n]) + b_ref[:, a:a + n]

    off = {}
    pos = 0
    for name, w in _SEGS:
        off[name] = pos
        pos += w
    mqk_ref[...] = seg(off["mqk"], 1024).astype(BF16)
    mv_ref[...] = seg(off["mv"], 512).astype(BF16)
    mo_ref[...] = seg(off["mo"], 512).astype(BF16)
    nq_ref[...] = seg(off["nq"], 512).astype(BF16)
    cin_ref[0] = seg(off["kc"], 128)
    cin_ref[1] = seg(off["vc"], 128)
    nsw_ref[...] = seg(off["nsw"], 768).astype(BF16)
    mg_ref[...] = seg(off["mg"], 2048).astype(BF16)
    small_ref[...] = seg(off["small"], 128)


def _inproj(x2, g, w, b, tm=512):
    n = x2.shape[0]
    assert n % tm == 0
    row = lambda width: pl.BlockSpec((tm, width), lambda i: (i, 0))
    out_shape = (
        jax.ShapeDtypeStruct((n, 1024), BF16), jax.ShapeDtypeStruct((n, 512), BF16),
        jax.ShapeDtypeStruct((n, 512), BF16), jax.ShapeDtypeStruct((n, 512), BF16),
        jax.ShapeDtypeStruct((2, n, 128), F32), jax.ShapeDtypeStruct((n, 768), BF16),
        jax.ShapeDtypeStruct((n, 2048), BF16), jax.ShapeDtypeStruct((n, 128), F32))
    out_specs = (row(1024), row(512), row(512), row(512),
                 pl.BlockSpec((2, tm, 128), lambda i: (0, i, 0)), row(768), row(2048), row(128))
    return pl.pallas_call(
        _inproj_kernel, out_shape=out_shape, grid=(n // tm,),
        in_specs=[row(D_MODEL), _const_spec((1, D_MODEL)), _const_spec((D_MODEL, _D_IN_PAD)),
                  _const_spec((1, _D_IN_PAD))],
        out_specs=out_specs, name="inproj",
        compiler_params=pltpu.CompilerParams(dimension_semantics=("arbitrary",),
                                             vmem_limit_bytes=VMEM_LIMIT),
    )(x2, g, w, b)


def _mlstm_kernel(mqk_ref, mv_ref, mo_ref, small_ref, cw_ref, bf_ref, gh_ref, ltri_ref,
                  ym_ref, ebuf, c_st, n_st, m_st):
    L = M_CHUNK
    c = pl.program_id(1)

    @pl.when(c == 0)
    def _():
        ebuf[0:8, :] = jnp.zeros((8, 2 * M_WIDTH), F32)
        c_st[...] = jnp.zeros_like(c_st)
        n_st[...] = jnp.zeros_like(n_st)
        m_st[...] = jnp.zeros_like(m_st)

    ebuf[8:8 + L, :] = mqk_ref[0].astype(F32)
    conv = cw_ref[0:1, :] * ebuf[5:5 + L, :]
    for j in range(1, CONV_WIDTH):
        conv = conv + cw_ref[j:j + 1, :] * ebuf[5 + j:5 + j + L, :]
    ebuf[0:8, :] = ebuf[L:L + 8, :]
    qk = conv * jax.nn.sigmoid(conv)

    gt = small_ref[0]
    ig = gt
    fg = pltpu.roll(gt, LANES - M_HEADS, 1) + bf_ref[...]
    lf = jnp.minimum(fg, 0.0) - jnp.log1p(jnp.exp(-jnp.abs(fg)))
    bc = jnp.dot(ltri_ref[...], lf, precision=HI, preferred_element_type=F32)
    blast = bc[L - 1:L, :]
    m_prev = m_st[0:1, :]
    a_all = blast - bc + ig
    m_new = jnp.maximum(blast + m_prev, jnp.max(a_all, axis=0, keepdims=True))
    decay = jnp.exp(blast + m_prev - m_new)
    w_all = jnp.exp(a_all - m_new)
    log_inter = bc + m_prev
    bc_t = bc.T
    ig_t = ig.T

    row = lax.broadcasted_iota(jnp.int32, (L, L), 0)
    col = lax.broadcasted_iota(jnp.int32, (L, L), 1)
    causal = row >= col

    for h in range(M_HEADS):
        sl = slice(h * M_HEAD_DIM, (h + 1) * M_HEAD_DIM)
        q = qk[:, sl]
        k = qk[:, M_WIDTH + h * M_HEAD_DIM:M_WIDTH + (h + 1) * M_HEAD_DIM] * (M_HEAD_DIM ** -0.5)
        v = mv_ref[0, :, sl]
        qb = q.astype(BF16)
        kb = k.astype(BF16)
        c_prev = c_st[h]
        n_prev = n_st[h, 0:1, :]

        log_d = jnp.where(causal, bc[:, h:h + 1] - bc_t[h:h + 1, :] + ig_t[h:h + 1, :], -jnp.inf)
        li = log_inter[:, h:h + 1]
        m_i = jnp.maximum(jnp.max(log_d, axis=-1, keepdims=True), li)
        p = jnp.exp(log_d - m_i) * _dot_nt(qb, kb)
        s_inter = jnp.exp(li - m_i)
        num = _dot(p.astype(BF16), v) + s_inter * _dot(qb, c_prev.astype(BF16))
        den = jnp.sum(p, axis=-1, keepdims=True) + s_inter * jnp.sum(q * n_prev, axis=-1, keepdims=True)
        hh = num / jnp.maximum(jnp.abs(den), jnp.exp(-m_i))
        hh = hh * lax.rsqrt(jnp.mean(hh * hh, axis=-1, keepdims=True) + RMS_EPS) * gh_ref[:, sl]
        ym_ref[0, :, sl] = (jax.nn.sigmoid(mo_ref[0, :, sl].astype(F32)) * hh).astype(ym_ref.dtype)

        w_col = w_all[:, h:h + 1]
        dec = decay[:, h:h + 1]
        vw = (v.astype(F32) * w_col).astype(BF16)
        c_st[h] = dec * c_prev + _dot_tn(kb, vw)
        n_st[h, 0:1, :] = dec * n_prev + jnp.sum(k * w_col, axis=0, keepdims=True)

    m_st[0:1, :] = m_new


def _mlstm(mqk, mv, mo, small, cw, bf, gh, ltri):
    B, T, _ = mqk.shape
    L = M_CHUNK
    assert T % L == 0
    blk = lambda w: pl.BlockSpec((1, L, w), lambda b, c: (b, c, 0))
    return pl.pallas_call(
        _mlstm_kernel, out_shape=jax.ShapeDtypeStruct((B, T, M_WIDTH), BF16),
        grid=(B, T // L),
        in_specs=[blk(2 * M_WIDTH), blk(M_WIDTH), blk(M_WIDTH), blk(LANES),
                  _const_spec((8, 2 * M_WIDTH)), _const_spec((1, LANES)),
                  _const_spec((1, M_WIDTH)), _const_spec((L, L))],
        out_specs=blk(M_WIDTH),
        scratch_shapes=[pltpu.VMEM((L + 8, 2 * M_WIDTH), F32),
                        pltpu.VMEM((M_HEADS, M_HEAD_DIM, M_HEAD_DIM), F32),
                        pltpu.VMEM((M_HEADS, 8, M_HEAD_DIM), F32),
                        pltpu.VMEM((8, LANES), F32)],
        name="mlstm",
        compiler_params=pltpu.CompilerParams(dimension_semantics=("arbitrary", "arbitrary")),
    )(mqk, mv, mo, small, cw, bf, gh, ltri)


def _compress_kernel(x_ref, pe_ref, w1t_ref, w1b_ref, w2_ref, o_ref, *, nch):
    half = CMP_BLOCK // 2
    top = jnp.zeros((nch, 2 * CMP_HIDDEN), F32)
    bot = jnp.zeros((nch, 2 * CMP_HIDDEN), F32)
    for p in range(half):
        xp = x_ref[0, 0, pl.ds(p, nch, stride=CMP_STRIDE), :]
        top = top + _dot((xp + pe_ref[0, p:p + 1, :]).astype(BF16), w1t_ref[0, p])
        bot = bot + _dot((xp + pe_ref[0, half + p:half + p + 1, :]).astype(BF16), w1b_ref[0, p])
    hid = top + pltpu.roll(bot, nch - 1, 0)
    act = hid * jax.nn.sigmoid(hid)
    o_ref[0, 0] = _dot(act.astype(BF16), w2_ref[0]).astype(o_ref.dtype)


def _compress(cin, pe2, w1t, w1b, w2):
    _, B, T, _ = cin.shape
    nch = T // CMP_STRIDE
    sel = lambda *shape: pl.BlockSpec((1,) + shape, lambda j, b: (j,) + (0,) * len(shape))
    return pl.pallas_call(
        functools.partial(_compress_kernel, nch=nch),
        out_shape=jax.ShapeDtypeStruct((2, B, nch, LANES), BF16),
        grid=(2, B),
        in_specs=[pl.BlockSpec((1, 1, T, LANES), lambda j, b: (j, b, 0, 0)),
                  sel(CMP_BLOCK, LANES), sel(CMP_BLOCK // 2, LANES, 2 * CMP_HIDDEN),
                  sel(CMP_BLOCK // 2, LANES, 2 * CMP_HIDDEN), sel(2 * CMP_HIDDEN, LANES)],
        out_specs=pl.BlockSpec((1, 1, nch, LANES), lambda j, b: (j, b, 0, 0)),
        name="compress",
        compiler_params=pltpu.CompilerParams(dimension_semantics=("arbitrary", "arbitrary")),
    )(cin, pe2, w1t, w1b, w2)


def _bias_kernel(tbl_ref, o_ref, *, kind, n_cmp):
    pid = pl.program_id(0)
    a = lax.broadcasted_iota(jnp.int32, (TQ, LANES), 0)
    b = lax.broadcasted_iota(jnp.int32, (TQ, LANES), 1)
    if kind == "cmp":
        dist = pid * TQ + a - (b * CMP_STRIDE + CMP_BLOCK - 1)
    else:
        dist = (pid - 1) * TQ + a - b
    n = jnp.maximum(dist, 0)
    cnt = jnp.zeros_like(n)
    for t in _BUCKET_THR:
        cnt = cnt + jnp.where(n >= t, 1, 0)
    bucket = jnp.where(n < REL_BUCKETS // 2, n, REL_BUCKETS // 2 + cnt)
    if kind == "tok":
        madd = jnp.where((dist >= 0) & (pid > 0), 0.0, NEG)
    elif kind == "win":
        madd = jnp.where((dist >= 0) & (dist < WINDOW) & (pid > 0), 0.0, NEG)
    else:
        madd = jnp.where((dist >= 0) & (b < n_cmp), 0.0, NEG)
    for h in range(N_HEADS):
        val = jnp.zeros((TQ, LANES), F32)
        for bb in range(REL_BUCKETS):
            val = jnp.where(bucket == bb, tbl_ref[h * REL_BUCKETS + bb], val)
        o_ref[0, h * TQ:(h + 1) * TQ, :] = val * LOG2E + madd


def _bias_tiles(tbl, n_tiles, kind, n_cmp=0):
    return pl.pallas_call(
        functools.partial(_bias_kernel, kind=kind, n_cmp=n_cmp),
        out_shape=jax.ShapeDtypeStruct((n_tiles, N_HEADS * TQ, LANES), F32),
        grid=(n_tiles,),
        in_specs=[pl.BlockSpec(memory_space=pltpu.SMEM)],
        out_specs=pl.BlockSpec((1, N_HEADS * TQ, LANES), lambda i: (i, 0, 0)),
        name="bias_" + kind,
        compiler_params=pltpu.CompilerParams(dimension_semantics=("arbitrary",)),
    )(tbl)


def _nsa_kernel(nq_ref, ks_ref, kw_ref, vs0_ref, vs1_ref, vw0_ref, vw1_ref, kc_ref, vc_ref, small_ref,
                bt_ref, wb_ref, cb_ref, ovt_ref, et_ref, yn_ref,
                lhs_sc, z_sc, *, nb, n_slc, n_top, n_win, n_chunks_max):
    qi = pl.program_id(1)
    HR = N_HEADS * TQ
    GR = N_REP * TQ
    CH = 4 * TQ
    vs_refs = (vs0_ref, vs1_ref)
    vw_refs = (vw0_ref, vw1_ref)
    lane = lax.broadcasted_iota(jnp.int32, (TQ, LANES), 1)
    lo = lane < N_HEAD_DIM

    def pair(o, r):
        return jnp.where(lo, o[r * TQ:(r + 1) * TQ], o[(N_REP + r) * TQ:(N_REP + r + 1) * TQ])

    def normed(acc, r):
        den = jnp.where(lo, acc[(N_REP + r) * TQ:(N_REP + r + 1) * TQ], acc[r * TQ:(r + 1) * TQ])
        return pair(acc, r) / pltpu.roll(den, N_HEAD_DIM, 1)

    def gate(sg, r, ci):
        c0 = _SMALL_NGATE + r * 3 + ci
        c1 = _SMALL_NGATE + (N_REP + r) * 3 + ci
        return jnp.where(lo, sg[:, c0:c0 + 1], sg[:, c1:c1 + 1])

    jb = lax.broadcasted_iota(jnp.int32, (n_slc, TQ), 0)
    tq = qi * TQ + lax.broadcasted_iota(jnp.int32, (n_slc, TQ), 1)
    cur = jnp.right_shift(tq, 6)
    forced = (jb == 0) | (jb == cur) | (jb == cur - 1)
    elig = jb <= cur
    t8 = qi * TQ + (lax.broadcasted_iota(jnp.int32, (HR, 1), 0) & (TQ - 1))
    w0 = jnp.maximum(qi - (n_win - 1), 0)
    woff = pl.multiple_of(w0 * TQ, TQ)

    def one_tile(s, n_chunks):
        q_all = nq_ref[s]
        zero = jnp.zeros((TQ, LANES), BF16)
        parts = []
        for g in range(N_KV_GROUPS):
            for r in range(N_REP):
                qr = q_all[:, r * LANES:(r + 1) * LANES]
                parts.append(jnp.where(lo if g == 0 else jnp.logical_not(lo), qr, zero))
        qs = jnp.concatenate(parts, axis=0)

        z = _dot_nt(qs, kc_ref[0, s]) + cb_ref[0]
        e = jnp.exp2(z - jnp.max(z, axis=-1, keepdims=True))
        rinv = jnp.where(t8 >= CMP_BLOCK - 1,
                         1.0 / jnp.maximum(jnp.sum(e, axis=-1, keepdims=True), 1e-30), 0.0)
        p_c = e * rinv
        o_c = _dot(p_c.astype(BF16), vc_ref[0, s])

        sw = _dot_nt(qs, kw_ref[s, pl.ds(woff, n_win * TQ), :])
        zw = [sw[:, j * TQ:(j + 1) * TQ] + wb_ref[jnp.maximum(qi - w0 - j + 1, 0)] for j in range(n_win)]
        zmax = zw[0]
        for j in range(1, n_win):
            zmax = jnp.maximum(zmax, zw[j])
        mw = jnp.max(zmax, axis=-1, keepdims=True)
        pw = jnp.concatenate([jnp.exp2(zj - mw).astype(BF16) for zj in zw], axis=1)
        acc_w = jnp.concatenate(
            [_dot(pw[g * GR:(g + 1) * GR], vw_refs[g][s, pl.ds(woff, n_win * TQ), :])
             for g in range(N_KV_GROUPS)], axis=0)

        for g in range(N_KV_GROUPS):
            psum = p_c[(g * N_REP) * TQ:(g * N_REP + 1) * TQ]
            for r in range(1, N_REP):
                psum = psum + p_c[(g * N_REP + r) * TQ:(g * N_REP + r + 1) * TQ]
            imp = lax.dot_general(ovt_ref[...], psum, (((1,), (1,)), ((), ())),
                                  precision=HI, preferred_element_type=F32)
            score = jnp.where(elig, jnp.where(forced, BIG, imp), -BIG)
            cnt = jnp.zeros((n_slc, TQ), F32)
            for i in range(n_slc):
                si = score[i:i + 1, :]
                tie = jnp.where(jb > i, 1.0, 0.0)
                cnt = cnt + jnp.where(si > score, 1.0, jnp.where(si == score, tie, 0.0))
            sel = jnp.where((cnt < n_top) & (score > -BIG / 2), 1.0, 0.0)
            seln = jnp.concatenate([sel - 1.0, jnp.zeros((LANES - n_slc, TQ), F32)], axis=0).T.astype(BF16)
            for r in range(N_REP):
                h = g * N_REP + r
                lhs_sc[s, h * TQ:(h + 1) * TQ, :] = jnp.concatenate([parts[h], seln], axis=1)

        zmax = None
        for c in range(n_chunks):
            for half in range(2):
                k0 = c * CH + half * 2 * TQ
                rhs = jnp.concatenate([ks_ref[s, k0:k0 + 2 * TQ, :], et_ref[k0:k0 + 2 * TQ, :]], axis=1)
                sz = _dot_nt(lhs_sc[s], rhs)
                for u in range(2):
                    kt = 4 * c + 2 * half + u
                    zt = sz[:, u * TQ:(u + 1) * TQ] + bt_ref[jnp.maximum(qi - kt + 1, 0)]
                    z_sc[s, c, :, (2 * half + u) * TQ:(2 * half + u + 1) * TQ] = zt
                    zmax = zt if zmax is None else jnp.maximum(zmax, zt)
        ms = jnp.max(zmax, axis=-1, keepdims=True)
        acc = [None] * N_KV_GROUPS
        for c in range(n_chunks):
            p = jnp.exp2(z_sc[s, c] - ms).astype(BF16)
            for g in range(N_KV_GROUPS):
                d = _dot(p[g * GR:(g + 1) * GR], vs_refs[g][s, c * CH:(c + 1) * CH, :])
                acc[g] = d if acc[g] is None else acc[g] + d
        acc_s = jnp.concatenate(acc, axis=0)

        sg = jax.nn.sigmoid(small_ref[s])
        for r in range(N_REP):
            out = (gate(sg, r, 0) * pair(o_c, r) + gate(sg, r, 1) * normed(acc_s, r)
                   + gate(sg, r, 2) * normed(acc_w, r))
            yn_ref[s, :, r * LANES:(r + 1) * LANES] = out.astype(yn_ref.dtype)

    for nc in range(1, n_chunks_max + 1):
        @pl.when(qi // 4 == nc - 1)
        def _(nc=nc):
            for s in range(nb):
                one_tile(s, nc)


def _nsa(nq, nsw, ckv, small, bt, wb, cb, ov, emat):
    B, T, _ = nq.shape
    assert T == 2048, "single 128-wide compressed-key tile assumes T == 2048"
    nqt = T // TQ
    n_cmp = (T - CMP_BLOCK) // CMP_STRIDE + 1
    n_slc = T // SEL_BLOCK
    n_top = min(SEL_TOPK, n_slc)
    n_win = wb.shape[0] - 1
    HR = N_HEADS * TQ
    assert nqt % 4 == 0 and n_slc % 8 == 0 and n_slc <= LANES
    nb = NSA_BATCH if B % NSA_BATCH == 0 else 1
    n_chunks_max = nqt // 4
    kv = lambda j: pl.BlockSpec((nb, T, LANES), lambda b, q: (b, 0, j), pipeline_mode=pl.Buffered(1))
    ck = lambda j: pl.BlockSpec((1, nb, T // CMP_STRIDE, LANES), lambda b, q: (j, b, 0, 0))
    kern = functools.partial(_nsa_kernel, nb=nb, n_slc=n_slc, n_top=n_top, n_win=n_win,
                             n_chunks_max=n_chunks_max)
    return pl.pallas_call(
        kern, out_shape=jax.ShapeDtypeStruct((B, T, N_WIDTH), BF16),
        grid=(B // nb, nqt),
        in_specs=[pl.BlockSpec((nb, TQ, N_WIDTH), lambda b, q: (b, q, 0)),
                  kv(0), kv(1), kv(2), kv(3), kv(4), kv(5), ck(0), ck(1),
                  pl.BlockSpec((nb, TQ, LANES), lambda b, q: (b, q, 0)),
                  _const_spec(bt.shape), _const_spec(wb.shape),
                  pl.BlockSpec((1, HR, LANES), lambda b, q: (q, 0, 0)),
                  _const_spec(ov.shape), _const_spec(emat.shape)],
        out_specs=pl.BlockSpec((nb, TQ, N_WIDTH), lambda b, q: (b, q, 0)),
        scratch_shapes=[pltpu.VMEM((nb, HR, 2 * LANES), BF16),
                        pltpu.VMEM((nb, n_chunks_max, HR, 4 * TQ), F32)],
        name="nsa",
        compiler_params=pltpu.CompilerParams(dimension_semantics=("arbitrary", "arbitrary"),
                                             vmem_limit_bytes=VMEM_LIMIT),
    )(nq, nsw, nsw, nsw, nsw, nsw, nsw, ckv, ckv, small, bt, wb, cb, ov, emat)


def _merge_kernel(x_ref, ym_ref, yn_ref, mg_ref, wbm_ref, wbn_ref, wo_ref, gffn_ref,
                  wg_ref, wu_ref, wd_ref, gfin_ref, o_ref, *, tf):
    bm = _dot(ym_ref[...], wbm_ref[...])
    bn = _dot(yn_ref[...], wbn_ref[...])
    mixed = (jax.nn.sigmoid(mg_ref[:, :D_MODEL].astype(F32)) * bm
             + jax.nn.sigmoid(mg_ref[:, D_MODEL:].astype(F32)) * bn)
    h = x_ref[...] + _dot(mixed.astype(BF16), wo_ref[...])
    f = (h * lax.rsqrt(jnp.mean(h * h, axis=-1, keepdims=True) + RMS_EPS) * gffn_ref[...]).astype(BF16)
    acc = jnp.zeros(h.shape, F32)
    for j in range(D_FF // tf):
        gg = _dot(f, wg_ref[:, j * tf:(j + 1) * tf])
        uu = _dot(f, wu_ref[:, j * tf:(j + 1) * tf])
        act = (gg * jax.nn.sigmoid(gg) * uu).astype(BF16)
        acc = acc + _dot(act, wd_ref[j * tf:(j + 1) * tf, :])
    h2 = h + acc
    o_ref[...] = h2 * lax.rsqrt(jnp.mean(h2 * h2, axis=-1, keepdims=True) + RMS_EPS) * gfin_ref[...]


def _merge(x2, ym, yn, mg, wbm, wbn, wo, gffn, wg, wu, wd, gfin, tm=512, tf=256):
    n = x2.shape[0]
    assert n % tm == 0 and D_FF % tf == 0
    row = lambda width: pl.BlockSpec((tm, width), lambda i: (i, 0))
    return pl.pallas_call(
        functools.partial(_merge_kernel, tf=tf),
        out_shape=jax.ShapeDtypeStruct((n, D_MODEL), F32), grid=(n // tm,),
        in_specs=[row(D_MODEL), row(M_WIDTH), row(N_WIDTH), row(N_BRANCH * D_MODEL),
                  _const_spec(wbm.shape), _const_spec(wbn.shape), _const_spec(wo.shape),
                  _const_spec(gffn.shape), _const_spec(wg.shape), _const_spec(wu.shape),
                  _const_spec(wd.shape), _const_spec(gfin.shape)],
        out_specs=row(D_MODEL), name="merge_ffn",
        compiler_params=pltpu.CompilerParams(dimension_semantics=("arbitrary",),
                                             vmem_limit_bytes=VMEM_LIMIT),
    )(x2, ym, yn, mg, wbm, wbn, wo, gffn, wg, wu, wd, gfin)


def _nsa_constants(T):
    n_cmp = (T - CMP_BLOCK) // CMP_STRIDE + 1
    n_slc = T // SEL_BLOCK
    cs = np.arange(n_cmp) * CMP_STRIDE
    ss = np.arange(n_slc) * SEL_BLOCK
    ov = np.clip(np.minimum(cs[:, None] + CMP_BLOCK, ss[None, :] + SEL_BLOCK)
                 - np.maximum(cs[:, None], ss[None, :]), 0, None) / CMP_STRIDE
    ovt = np.zeros((n_slc, LANES), np.float32)
    ovt[:, :n_cmp] = ov.T
    et = (np.arange(T)[:, None] // SEL_BLOCK == np.arange(LANES)[None, :]).astype(np.float32) * (-NEG)
    return jnp.asarray(ovt), jnp.asarray(et, dtype=BF16)


def _compress_weights(pe_cmp, w_cmp1, w_cmp2):
    half = CMP_BLOCK // 2
    dh, hid = N_HEAD_DIM, CMP_HIDDEN
    eye = jnp.eye(N_KV_GROUPS, dtype=F32)
    w1 = w_cmp1.reshape(2, CMP_BLOCK, dh, hid)
    w1bd = jnp.einsum("jpdn,gk->jpgdkn", w1, eye).reshape(2, CMP_BLOCK, N_KV_GROUPS * dh, N_KV_GROUPS * hid)
    w2bd = jnp.einsum("jnd,gk->jgnkd", w_cmp2, eye).reshape(2, N_KV_GROUPS * hid, N_KV_GROUPS * dh)
    pe2 = jnp.tile(pe_cmp, (1, 1, N_KV_GROUPS))
    return pe2, w1bd[:, :half].astype(BF16), w1bd[:, half:].astype(BF16), w2bd.astype(BF16)


def kernel(x, g_norm_mix, w_in, b_in, b_fgate, conv_qk, g_mlstm_head, pe_cmp, w_cmp1, w_cmp2,
           rel_bias, w_branch, w_out, g_norm_ffn, w_gate, w_up, w_down, g_final):
    B, T, D = x.shape
    assert D == D_MODEL and w_in.shape[0] == 1, "one residual block (DEPTH == 1)"
    N = B * T
    x2 = x.reshape(N, D)

    idx, scale, const = _inproj_perm()
    w_r = (jnp.take(w_in[0], jnp.asarray(idx), axis=1) * jnp.asarray(scale)).astype(BF16)
    b_r = (jnp.take(b_in[0], jnp.asarray(idx)) * jnp.asarray(scale) + jnp.asarray(const)).reshape(1, -1)
    mqk, mv, mo, nq, cin, nsw, mg, small = _inproj(x2, g_norm_mix[0].reshape(1, D), w_r, b_r)

    cw = jnp.zeros((8, 2 * M_WIDTH), F32).at[:CONV_WIDTH].set(conv_qk[0])
    bf = jnp.zeros((1, LANES), F32).at[0, :M_HEADS].set(b_fgate[0])
    ltri = jnp.asarray(np.tril(np.ones((M_CHUNK, M_CHUNK), np.float32)))
    r3 = lambda a: a.reshape(B, T, a.shape[-1])
    ym = _mlstm(r3(mqk), r3(mv), r3(mo), r3(small), cw, bf, g_mlstm_head[0].reshape(1, M_WIDTH), ltri)

    pe2, w1t, w1b, w2 = _compress_weights(pe_cmp[0], w_cmp1[0], w_cmp2[0])
    ckv = _compress(cin.reshape(2, B, T, LANES), pe2, w1t, w1b, w2)
    tbl = rel_bias.astype(F32).T.reshape(-1)
    nqt = T // TQ
    bt = _bias_tiles(tbl, nqt + 1, "tok")
    wb = _bias_tiles(tbl, min(WINDOW // TQ + 1, nqt) + 1, "win")
    cb = _bias_tiles(tbl, nqt, "cmp", n_cmp=(T - CMP_BLOCK) // CMP_STRIDE + 1)
    ov, emat = _nsa_constants(T)
    yn = _nsa(r3(nq), r3(nsw), ckv, r3(small), bt, wb, cb, ov, emat)

    perm = np.zeros((N_WIDTH,), np.int32)
    for r in range(N_REP):
        for g in range(N_KV_GROUPS):
            perm[r * 128 + g * 64:r * 128 + g * 64 + 64] = (g * N_REP + r) * N_HEAD_DIM + np.arange(N_HEAD_DIM)
    wbm = w_branch[0, 0].astype(BF16)
    wbn = jnp.take(w_branch[0, 1], jnp.asarray(perm), axis=0).astype(BF16)
    out = _merge(x2, ym.reshape(N, M_WIDTH), yn.reshape(N, N_WIDTH), mg, wbm, wbn,
                 w_out[0].astype(BF16), g_norm_ffn[0].reshape(1, D), w_gate[0].astype(BF16),
                 w_up[0].astype(BF16), w_down[0].astype(BF16), g_final.reshape(1, D))
    return out.reshape(B, T, D)
```

```python
import functools

import numpy as np
import jax
import jax.numpy as jnp
from jax import lax
from jax.experimental import pallas as pl
from jax.experimental.pallas import tpu as pltpu

F32 = jnp.float32
BF16 = jnp.bfloat16
HI = lax.Precision.HIGHEST

D_MODEL = 1024
M_HEADS = 4
M_HEAD_DIM = 128
M_WIDTH = M_HEADS * M_HEAD_DIM
M_CHUNK = 128
CONV_WIDTH = 4
N_HEADS = 8
N_KV_GROUPS = 2
N_REP = N_HEADS // N_KV_GROUPS
N_HEAD_DIM = 64
N_WIDTH = N_HEADS * N_HEAD_DIM
N_KV_WIDTH = N_KV_GROUPS * N_HEAD_DIM
CMP_BLOCK = 32
CMP_STRIDE = 16
CMP_HIDDEN = 2 * N_HEAD_DIM
SEL_BLOCK = 64
SEL_TOPK = 16
WINDOW = 512
REL_BUCKETS = 32
REL_MAX_DIST = 1024
N_BRANCH = 2
D_FF = 2816
RMS_EPS = 1e-6
BIG = 1e9
NEG = -1e30

LANES = 128
TQ = 128
NSA_BATCH = 1
VMEM_LIMIT = 56 * 1024 * 1024

_OFF_MQ, _OFF_MK, _OFF_MV, _OFF_MO = 0, 512, 1024, 1536
_OFF_MI, _OFF_MF, _OFF_NQ, _OFF_NKV = 2048, 2052, 2056, 2568
_OFF_NGATE, _OFF_MERGE, _D_IN = 3336, 3360, 5408
_SMALL_NGATE = 8

_SEGS = (("mqk", 1024), ("mv", 512), ("mo", 512), ("nq", 512), ("kc", 128),
         ("vc", 128), ("nsw", 768), ("mg", 2048), ("small", 128))
_D_IN_PAD = sum(w for _, w in _SEGS)
LOG2E = 1.4426950408889634


def _dot(a, b, **kw):
    return jnp.dot(a, b, preferred_element_type=F32, **kw)


def _dot_nt(a, b):
    return lax.dot_general(a, b, (((1,), (1,)), ((), ())), preferred_element_type=F32)


def _dot_tn(a, b):
    return lax.dot_general(a, b, (((0,), (0,)), ((), ())), preferred_element_type=F32)


def _const_spec(shape):
    nd = len(shape)
    return pl.BlockSpec(shape, lambda *_: (0,) * nd, pipeline_mode=pl.Buffered(1))


def _bucket_thresholds():
    max_exact = REL_BUCKETS // 2
    assert REL_MAX_DIST == 64 * max_exact and REL_BUCKETS - max_exact == 16
    thr = []
    for k in range(1, REL_BUCKETS - max_exact):
        t = max_exact
        while t ** 8 < (max_exact ** 8) * (2 ** (3 * k)):
            t += 1
        thr.append(t)
    return tuple(thr)


_BUCKET_THR = _bucket_thresholds()


def _inproj_perm():
    idx = np.zeros((_D_IN_PAD,), np.int32)
    scale = np.zeros((_D_IN_PAD,), np.float32)
    const = np.zeros((_D_IN_PAD,), np.float32)
    pos = 0

    def put(cols, s=1.0):
        nonlocal pos
        n = len(cols)
        idx[pos:pos + n] = cols
        scale[pos:pos + n] = s
        pos += n

    def ones(n):
        nonlocal pos
        const[pos:pos + n] = 1.0
        pos += n

    put(np.arange(_OFF_MQ, _OFF_MV))
    put(np.arange(_OFF_MV, _OFF_MO))
    put(np.arange(_OFF_MO, _OFF_MI))
    nq = np.zeros((N_WIDTH,), np.int32)
    for r in range(N_REP):
        for g in range(N_KV_GROUPS):
            for d in range(N_HEAD_DIM):
                nq[r * 128 + g * 64 + d] = _OFF_NQ + (g * N_REP + r) * N_HEAD_DIM + d
    put(nq, N_HEAD_DIM ** -0.5 * LOG2E)
    kv = lambda j, g: np.arange(_OFF_NKV + (j * N_KV_GROUPS + g) * N_HEAD_DIM,
                                _OFF_NKV + (j * N_KV_GROUPS + g + 1) * N_HEAD_DIM)
    put(np.arange(_OFF_NKV, _OFF_NKV + 256))
    put(np.concatenate([kv(2, 0), kv(2, 1)]))
    put(np.concatenate([kv(4, 0), kv(4, 1)]))
    for j in (3, 5):
        put(kv(j, 0))
        ones(N_HEAD_DIM)
        ones(N_HEAD_DIM)
        put(kv(j, 1))
    put(np.arange(_OFF_MERGE, _D_IN))
    put(np.arange(_OFF_MI, _OFF_MI + 8))
    put(np.arange(_OFF_NGATE, _OFF_NGATE + 24))
    pos += LANES - 32
    assert pos == _D_IN_PAD
    return idx, scale, const


def _gather_cols(a, idx, scale):
    pieces = []
    start = 0
    n = len(idx)
    for c in range(1, n + 1):
        same = c < n and scale[c] == scale[start] and (scale[c] == 0.0 or idx[c] == idx[c - 1] + 1)
        if not same:
            if scale[start] == 0.0:
                piece = jnp.zeros((a.shape[0], c - start), a.dtype)
            else:
                piece = a[:, int(idx[start]):int(idx[start]) + (c - start)]
                if scale[start] != 1.0:
                    piece = piece * float(scale[start])
            pieces.append(piece)
            start = c
    return jnp.concatenate(pieces, axis=1)


def _inproj_kernel(x_ref, g_ref, w_ref, b_ref, cw_ref, mqk_ref, mv_ref, mo_ref, nq_ref,
                   cin_ref, nsw_ref, mg_ref, small_ref, ebuf, *, tiles_per_seq):
    tm = x_ref.shape[0]

    @pl.when(pl.program_id(0) % tiles_per_seq == 0)
    def _():
        ebuf[0:8, :] = jnp.zeros((8, 2 * M_WIDTH), F32)

    x = x_ref[...]
    ms = jnp.mean(x * x, axis=-1, keepdims=True)
    u = (x * lax.rsqrt(ms + RMS_EPS) * g_ref[...]).astype(BF16)

    def seg(a, n):
        return _dot(u, w_ref[:, a:a + n]) + b_ref[:, a:a + n]

    off = {}
    pos = 0
    for name, w in _SEGS:
        off[name] = pos
        pos += w

    ebuf[8:8 + tm, :] = seg(off["mqk"], 1024)

    def conv_rows(r0, nr):
        ext = ebuf[r0:r0 + nr + 8, :]
        conv = cw_ref[CONV_WIDTH - 1:CONV_WIDTH, :] * ext[8:, :]
        for k in range(1, CONV_WIDTH):
            j = CONV_WIDTH - 1 - k
            conv = conv + cw_ref[j:j + 1, :] * pltpu.roll(ext, k, 0)[8:, :]
        mqk_ref[r0:r0 + nr, :] = (conv * jax.nn.sigmoid(conv)
                                  * cw_ref[CONV_WIDTH:CONV_WIDTH + 1, :]).astype(BF16)

    nsl = 4
    rs = tm // nsl
    mv_ref[...] = seg(off["mv"], 512).astype(BF16)
    conv_rows(0 * rs, rs)
    mo_ref[...] = seg(off["mo"], 512).astype(BF16)
    conv_rows(1 * rs, rs)
    nq_ref[...] = seg(off["nq"], 512).astype(BF16)
    conv_rows(2 * rs, rs)
    nsw_ref[...] = seg(off["nsw"], 768).astype(BF16)
    conv_rows(3 * rs, rs)
    ebuf[0:8, :] = ebuf[tm:tm + 8, :]
    cin_ref[0] = seg(off["kc"], 128)
    cin_ref[1] = seg(off["vc"], 128)
    mg_ref[...] = seg(off["mg"], 2048).astype(BF16)
    small_ref[...] = seg(off["small"], 128)


def _inproj(x2, g, w, b, cw, seq_len, tm=512):
    n = x2.shape[0]
    assert n % tm == 0 and seq_len % tm == 0
    row = lambda width: pl.BlockSpec((tm, width), lambda i: (i, 0))
    out_shape = (
        jax.ShapeDtypeStruct((n, 1024), BF16), jax.ShapeDtypeStruct((n, 512), BF16),
        jax.ShapeDtypeStruct((n, 512), BF16), jax.ShapeDtypeStruct((n, 512), BF16),
        jax.ShapeDtypeStruct((2, n, 128), F32), jax.ShapeDtypeStruct((n, 768), BF16),
        jax.ShapeDtypeStruct((n, 2048), BF16), jax.ShapeDtypeStruct((n, 128), F32))
    out_specs = (row(1024), row(512), row(512), row(512),
                 pl.BlockSpec((2, tm, 128), lambda i: (0, i, 0)), row(768), row(2048), row(128))
    return pl.pallas_call(
        functools.partial(_inproj_kernel, tiles_per_seq=seq_len // tm),
        out_shape=out_shape, grid=(n // tm,),
        in_specs=[row(D_MODEL), _const_spec((1, D_MODEL)), _const_spec((D_MODEL, _D_IN_PAD)),
                  _const_spec((1, _D_IN_PAD)), _const_spec((8, 2 * M_WIDTH))],
        out_specs=out_specs,
        scratch_shapes=[pltpu.VMEM((tm + 8, 2 * M_WIDTH), F32)],
        name="inproj",
        compiler_params=pltpu.CompilerParams(dimension_semantics=("arbitrary",),
                                             vmem_limit_bytes=VMEM_LIMIT),
    )(x2, g, w, b, cw)


def _mlstm_kernel(mqk_ref, mv_ref, mo_ref, small_ref, bf_ref, gh_ref, ltri_ref,
                  ym_ref, c_st, m_st):
    L = M_CHUNK
    c = pl.program_id(1)

    @pl.when(c == 0)
    def _():
        c_st[...] = jnp.zeros_like(c_st)
        m_st[...] = jnp.zeros_like(m_st)

    gt = small_ref[0]
    ig = gt
    fg = pltpu.roll(gt, LANES - M_HEADS, 1) + bf_ref[...]
    lf = jnp.minimum(fg, 0.0) - jnp.log1p(jnp.exp(-jnp.abs(fg)))
    bc = jnp.dot(ltri_ref[...], lf, precision=HI, preferred_element_type=F32)
    blast = bc[L - 1:L, :]
    m_prev = m_st[0:1, :]
    a_all = blast - bc + ig
    m_new = jnp.maximum(blast + m_prev, jnp.max(a_all, axis=0, keepdims=True))
    decay = jnp.exp(blast + m_prev - m_new)
    w_all = jnp.exp(a_all - m_new)
    log_inter = bc + m_prev
    bc_t = bc.T
    ig_t = ig.T

    row = lax.broadcasted_iota(jnp.int32, (L, L), 0)
    col = lax.broadcasted_iota(jnp.int32, (L, L), 1)
    causal = row >= col

    ones = jnp.ones((L, M_HEAD_DIM), BF16)
    for h in range(M_HEADS):
        sl = slice(h * M_HEAD_DIM, (h + 1) * M_HEAD_DIM)
        qb = mqk_ref[0, :, sl]
        kb = mqk_ref[0, :, M_WIDTH + h * M_HEAD_DIM:M_WIDTH + (h + 1) * M_HEAD_DIM]
        v = mv_ref[0, :, sl]
        cn_prev = c_st[h]

        log_d = jnp.where(causal, bc[:, h:h + 1] - bc_t[h:h + 1, :] + ig_t[h:h + 1, :], -jnp.inf)
        li = log_inter[:, h:h + 1]
        m_i = jnp.maximum(jnp.max(log_d, axis=-1, keepdims=True), li)
        p = jnp.exp(log_d - m_i) * _dot_nt(qb, kb)
        s_inter = jnp.exp(li - m_i)
        nd = (_dot(p.astype(BF16), jnp.concatenate([v, ones], axis=1))
              + s_inter * _dot(qb, cn_prev.astype(BF16)))
        hh = nd[:, :M_HEAD_DIM] / jnp.maximum(jnp.abs(nd[:, M_HEAD_DIM:]), jnp.exp(-m_i))
        hh = hh * lax.rsqrt(jnp.mean(hh * hh, axis=-1, keepdims=True) + RMS_EPS) * gh_ref[:, sl]
        ym_ref[0, :, sl] = (jax.nn.sigmoid(mo_ref[0, :, sl].astype(F32)) * hh).astype(ym_ref.dtype)

        w_col = w_all[:, h:h + 1]
        dec = decay[:, h:h + 1]
        vw = jnp.concatenate([v.astype(F32) * w_col, jnp.broadcast_to(w_col, (L, M_HEAD_DIM))], axis=1)
        c_st[h] = dec * cn_prev + _dot_tn(kb, vw.astype(BF16))

    m_st[0:1, :] = m_new


def _mlstm(mqk, mv, mo, small, bf, gh, ltri):
    B, T, _ = mqk.shape
    L = M_CHUNK
    assert T % L == 0
    blk = lambda w: pl.BlockSpec((1, L, w), lambda b, c: (b, c, 0))
    return pl.pallas_call(
        _mlstm_kernel, out_shape=jax.ShapeDtypeStruct((B, T, M_WIDTH), BF16),
        grid=(B, T // L),
        in_specs=[blk(2 * M_WIDTH), blk(M_WIDTH), blk(M_WIDTH), blk(LANES),
                  _const_spec((1, LANES)), _const_spec((1, M_WIDTH)), _const_spec((L, L))],
        out_specs=blk(M_WIDTH),
        scratch_shapes=[pltpu.VMEM((M_HEADS, M_HEAD_DIM, 2 * M_HEAD_DIM), F32),
                        pltpu.VMEM((8, LANES), F32)],
        name="mlstm",
        compiler_params=pltpu.CompilerParams(dimension_semantics=("arbitrary", "arbitrary")),
    )(mqk, mv, mo, small, bf, gh, ltri)


def _compress_kernel(x_ref, pe_ref, w1t_ref, w1b_ref, w2_ref, o_ref, *, nch):
    half = CMP_BLOCK // 2
    top = jnp.zeros((nch, 2 * CMP_HIDDEN), F32)
    bot = jnp.zeros((nch, 2 * CMP_HIDDEN), F32)
    for p in range(half):
        xp = x_ref[0, 0, pl.ds(p, nch, stride=CMP_STRIDE), :]
        top = top + _dot((xp + pe_ref[0, p:p + 1, :]).astype(BF16), w1t_ref[0, p])
        bot = bot + _dot((xp + pe_ref[0, half + p:half + p + 1, :]).astype(BF16), w1b_ref[0, p])
    hid = top + pltpu.roll(bot, nch - 1, 0)
    act = hid * jax.nn.sigmoid(hid)
    o_ref[0, 0] = _dot(act.astype(BF16), w2_ref[0]).astype(o_ref.dtype)


def _compress(cin, pe2, w1t, w1b, w2):
    _, B, T, _ = cin.shape
    nch = T // CMP_STRIDE
    sel = lambda *shape: pl.BlockSpec((1,) + shape, lambda j, b: (j,) + (0,) * len(shape))
    return pl.pallas_call(
        functools.partial(_compress_kernel, nch=nch),
        out_shape=jax.ShapeDtypeStruct((2, B, nch, LANES), BF16),
        grid=(2, B),
        in_specs=[pl.BlockSpec((1, 1, T, LANES), lambda j, b: (j, b, 0, 0)),
                  sel(CMP_BLOCK, LANES), sel(CMP_BLOCK // 2, LANES, 2 * CMP_HIDDEN),
                  sel(CMP_BLOCK // 2, LANES, 2 * CMP_HIDDEN), sel(2 * CMP_HIDDEN, LANES)],
        out_specs=pl.BlockSpec((1, 1, nch, LANES), lambda j, b: (j, b, 0, 0)),
        name="compress",
        compiler_params=pltpu.CompilerParams(dimension_semantics=("arbitrary", "arbitrary")),
    )(cin, pe2, w1t, w1b, w2)


def _bias_kernel(tbl_ref, o_ref, *, kind, n_cmp):
    pid = pl.program_id(0)
    a = lax.broadcasted_iota(jnp.int32, (TQ, LANES), 0)
    b = lax.broadcasted_iota(jnp.int32, (TQ, LANES), 1)
    if kind == "cmp":
        dist = pid * TQ + a - (b * CMP_STRIDE + CMP_BLOCK - 1)
    else:
        dist = (pid - 1) * TQ + a - b
    n = jnp.maximum(dist, 0)
    cnt = jnp.zeros_like(n)
    for t in _BUCKET_THR:
        cnt = cnt + jnp.where(n >= t, 1, 0)
    bucket = jnp.where(n < REL_BUCKETS // 2, n, REL_BUCKETS // 2 + cnt)
    if kind == "tok":
        madd = jnp.where((dist >= 0) & (pid > 0), 0.0, NEG)
    elif kind == "win":
        madd = jnp.where((dist >= 0) & (dist < WINDOW) & (pid > 0), 0.0, NEG)
    else:
        madd = jnp.where((dist >= 0) & (b < n_cmp), 0.0, NEG)
    for h in range(N_HEADS):
        val = jnp.zeros((TQ, LANES), F32)
        for bb in range(REL_BUCKETS):
            val = jnp.where(bucket == bb, tbl_ref[h * REL_BUCKETS + bb], val)
        o_ref[0, h * TQ:(h + 1) * TQ, :] = val * LOG2E + madd


def _bias_tiles(tbl, n_tiles, kind, n_cmp=0):
    return pl.pallas_call(
        functools.partial(_bias_kernel, kind=kind, n_cmp=n_cmp),
        out_shape=jax.ShapeDtypeStruct((n_tiles, N_HEADS * TQ, LANES), F32),
        grid=(n_tiles,),
        in_specs=[pl.BlockSpec(memory_space=pltpu.SMEM)],
        out_specs=pl.BlockSpec((1, N_HEADS * TQ, LANES), lambda i: (i, 0, 0)),
        name="bias_" + kind,
        compiler_params=pltpu.CompilerParams(dimension_semantics=("arbitrary",)),
    )(tbl)


def _nsa_kernel(nq_ref, ks_ref, kw_ref, vs0_ref, vs1_ref, vw0_ref, vw1_ref, kc_ref, vc_ref, small_ref,
                bt_ref, wb_ref, cb_ref, ovt_ref, et_ref, yn_ref,
                lhs_sc, z_sc, *, nb, n_slc, n_top, n_win, n_chunks_max):
    qi = pl.program_id(1)
    HR = N_HEADS * TQ
    GR = N_REP * TQ
    CH = 4 * TQ
    vs_refs = (vs0_ref, vs1_ref)
    vw_refs = (vw0_ref, vw1_ref)
    lane = lax.broadcasted_iota(jnp.int32, (TQ, LANES), 1)
    lo = lane < N_HEAD_DIM

    def pair(o, r):
        return jnp.where(lo, o[r * TQ:(r + 1) * TQ], o[(N_REP + r) * TQ:(N_REP + r + 1) * TQ])

    def normed(acc, r):
        den = jnp.where(lo, acc[(N_REP + r) * TQ:(N_REP + r + 1) * TQ], acc[r * TQ:(r + 1) * TQ])
        return pair(acc, r) / pltpu.roll(den, N_HEAD_DIM, 1)

    def gate(sg, r, ci):
        c0 = _SMALL_NGATE + r * 3 + ci
        c1 = _SMALL_NGATE + (N_REP + r) * 3 + ci
        return jnp.where(lo, sg[:, c0:c0 + 1], sg[:, c1:c1 + 1])

    jb = lax.broadcasted_iota(jnp.int32, (n_slc, TQ), 0)
    tq = qi * TQ + lax.broadcasted_iota(jnp.int32, (n_slc, TQ), 1)
    cur = jnp.right_shift(tq, 6)
    forced = (jb == 0) | (jb == cur) | (jb == cur - 1)
    elig = jb <= cur
    t8 = qi * TQ + (lax.broadcasted_iota(jnp.int32, (HR, 1), 0) & (TQ - 1))
    w0 = jnp.maximum(qi - (n_win - 1), 0)
    woff = pl.multiple_of(w0 * TQ, TQ)

    def one_tile(s, n_chunks):
        q_all = nq_ref[s]
        zero = jnp.zeros((TQ, LANES), BF16)
        parts = []
        for g in range(N_KV_GROUPS):
            for r in range(N_REP):
                qr = q_all[:, r * LANES:(r + 1) * LANES]
                parts.append(jnp.where(lo if g == 0 else jnp.logical_not(lo), qr, zero))
        qs = jnp.concatenate(parts, axis=0)

        z = _dot_nt(qs, kc_ref[0, s]) + cb_ref[0]
        e = jnp.exp2(z - jnp.max(z, axis=-1, keepdims=True))
        rinv = jnp.where(t8 >= CMP_BLOCK - 1,
                         1.0 / jnp.maximum(jnp.sum(e, axis=-1, keepdims=True), 1e-30), 0.0)
        p_c = e * rinv
        o_c = _dot(p_c.astype(BF16), vc_ref[0, s])

        zw = []
        zmax = None
        for j0 in range(0, n_win, 2):
            nj = min(2, n_win - j0)
            sw = _dot_nt(qs, kw_ref[s, pl.ds(woff + j0 * TQ, nj * TQ), :])
            for u in range(nj):
                zj = sw[:, u * TQ:(u + 1) * TQ] + wb_ref[jnp.maximum(qi - w0 - (j0 + u) + 1, 0)]
                zw.append(zj)
                zmax = zj if zmax is None else jnp.maximum(zmax, zj)
        mw = jnp.max(zmax, axis=-1, keepdims=True)
        pw = jnp.concatenate([jnp.exp2(zj - mw).astype(BF16) for zj in zw], axis=1)
        acc_w = jnp.concatenate(
            [_dot(pw[g * GR:(g + 1) * GR], vw_refs[g][s, pl.ds(woff, n_win * TQ), :])
             for g in range(N_KV_GROUPS)], axis=0)

        all_selected = n_chunks * CH <= n_top * SEL_BLOCK
        for g in range(N_KV_GROUPS):
            if all_selected:
                sel = jnp.where(elig, 1.0, 0.0)
            else:
                psum = p_c[(g * N_REP) * TQ:(g * N_REP + 1) * TQ]
                for r in range(1, N_REP):
                    psum = psum + p_c[(g * N_REP + r) * TQ:(g * N_REP + r + 1) * TQ]
                imp = lax.dot_general(ovt_ref[...], psum, (((1,), (1,)), ((), ())),
                                      precision=HI, preferred_element_type=F32)
                score = jnp.where(elig, jnp.where(forced, BIG, imp), -BIG)
                cnt = jnp.zeros((n_slc, TQ), F32)
                for i in range(n_slc):
                    si = score[i:i + 1, :]
                    tie = jnp.where(jb > i, 1.0, 0.0)
                    cnt = cnt + jnp.where(si > score, 1.0, jnp.where(si == score, tie, 0.0))
                sel = jnp.where((cnt < n_top) & (score > -BIG / 2), 1.0, 0.0)
            seln = jnp.concatenate([sel - 1.0, jnp.zeros((LANES - n_slc, TQ), F32)], axis=0).T.astype(BF16)
            for r in range(N_REP):
                h = g * N_REP + r
                lhs_sc[s, h * TQ:(h + 1) * TQ, :] = jnp.concatenate([parts[h], seln], axis=1)

        zmax = None
        for c in range(n_chunks):
            for half in range(2):
                k0 = c * CH + half * 2 * TQ
                rhs = jnp.concatenate([ks_ref[s, k0:k0 + 2 * TQ, :], et_ref[k0:k0 + 2 * TQ, :]], axis=1)
                sz = _dot_nt(lhs_sc[s], rhs)
                for u in range(2):
                    kt = 4 * c + 2 * half + u
                    zt = sz[:, u * TQ:(u + 1) * TQ] + bt_ref[jnp.maximum(qi - kt + 1, 0)]
                    z_sc[s, c, :, (2 * half + u) * TQ:(2 * half + u + 1) * TQ] = zt
                    zmax = zt if zmax is None else jnp.maximum(zmax, zt)
        ms = jnp.max(zmax, axis=-1, keepdims=True)
        acc = [None] * N_KV_GROUPS
        for c in range(n_chunks):
            p = jnp.exp2(z_sc[s, c] - ms).astype(BF16)
            for g in range(N_KV_GROUPS):
                d = _dot(p[g * GR:(g + 1) * GR], vs_refs[g][s, c * CH:(c + 1) * CH, :])
                acc[g] = d if acc[g] is None else acc[g] + d
        acc_s = jnp.concatenate(acc, axis=0)

        sg = jax.nn.sigmoid(small_ref[s])
        for r in range(N_REP):
            out = (gate(sg, r, 0) * pair(o_c, r) + gate(sg, r, 1) * normed(acc_s, r)
                   + gate(sg, r, 2) * normed(acc_w, r))
            yn_ref[s, :, r * LANES:(r + 1) * LANES] = out.astype(yn_ref.dtype)

    for nc in range(1, n_chunks_max + 1):
        @pl.when(qi // 4 == nc - 1)
        def _(nc=nc):
            for s in range(nb):
                one_tile(s, nc)


def _nsa(nq, nsw, ckv, small, bt, wb, cb, ov, emat):
    B, T, _ = nq.shape
    assert T == 2048, "single 128-wide compressed-key tile assumes T == 2048"
    nqt = T // TQ
    n_cmp = (T - CMP_BLOCK) // CMP_STRIDE + 1
    n_slc = T // SEL_BLOCK
    n_top = min(SEL_TOPK, n_slc)
    n_win = wb.shape[0] - 1
    HR = N_HEADS * TQ
    assert nqt % 4 == 0 and n_slc % 8 == 0 and n_slc <= LANES
    nb = NSA_BATCH if B % NSA_BATCH == 0 else 1
    n_chunks_max = nqt // 4
    kv = lambda j: pl.BlockSpec((nb, T, LANES), lambda b, q: (b, 0, j), pipeline_mode=pl.Buffered(1))
    ck = lambda j: pl.BlockSpec((1, nb, T // CMP_STRIDE, LANES), lambda b, q: (j, b, 0, 0))
    kern = functools.partial(_nsa_kernel, nb=nb, n_slc=n_slc, n_top=n_top, n_win=n_win,
                             n_chunks_max=n_chunks_max)
    return pl.pallas_call(
        kern, out_shape=jax.ShapeDtypeStruct((B, T, N_WIDTH), BF16),
        grid=(B // nb, nqt),
        in_specs=[pl.BlockSpec((nb, TQ, N_WIDTH), lambda b, q: (b, q, 0)),
                  kv(0), kv(1), kv(2), kv(3), kv(4), kv(5), ck(0), ck(1),
                  pl.BlockSpec((nb, TQ, LANES), lambda b, q: (b, q, 0)),
                  _const_spec(bt.shape), _const_spec(wb.shape),
                  pl.BlockSpec((1, HR, LANES), lambda b, q: (q, 0, 0)),
                  _const_spec(ov.shape), _const_spec(emat.shape)],
        out_specs=pl.BlockSpec((nb, TQ, N_WIDTH), lambda b, q: (b, q, 0)),
        scratch_shapes=[pltpu.VMEM((nb, HR, 2 * LANES), BF16),
                        pltpu.VMEM((nb, n_chunks_max, HR, 4 * TQ), F32)],
        name="nsa",
        compiler_params=pltpu.CompilerParams(dimension_semantics=("arbitrary", "arbitrary"),
                                             vmem_limit_bytes=VMEM_LIMIT),
    )(nq, nsw, nsw, nsw, nsw, nsw, nsw, ckv, ckv, small, bt, wb, cb, ov, emat)


def _merge_kernel(x_ref, ym_ref, yn_ref, mg_ref, wbm_ref, wbn_ref, wo_ref, gffn_ref,
                  wg_ref, wu_ref, wd_ref, gfin_ref, o_ref, *, tf):
    bm = _dot(ym_ref[...], wbm_ref[...])
    bn = _dot(yn_ref[...], wbn_ref[...])
    mixed = (jax.nn.sigmoid(mg_ref[:, :D_MODEL].astype(F32)) * bm
             + jax.nn.sigmoid(mg_ref[:, D_MODEL:].astype(F32)) * bn)
    h = x_ref[...] + _dot(mixed.astype(BF16), wo_ref[...])
    f = (h * lax.rsqrt(jnp.mean(h * h, axis=-1, keepdims=True) + RMS_EPS) * gffn_ref[...]).astype(BF16)
    acc = jnp.zeros(h.shape, F32)
    for j in range(D_FF // tf):
        gg = _dot(f, wg_ref[:, j * tf:(j + 1) * tf])
        uu = _dot(f, wu_ref[:, j * tf:(j + 1) * tf])
        act = (gg * jax.nn.sigmoid(gg) * uu).astype(BF16)
        acc = acc + _dot(act, wd_ref[j * tf:(j + 1) * tf, :])
    h2 = h + acc
    o_ref[...] = h2 * lax.rsqrt(jnp.mean(h2 * h2, axis=-1, keepdims=True) + RMS_EPS) * gfin_ref[...]


def _merge(x2, ym, yn, mg, wbm, wbn, wo, gffn, wg, wu, wd, gfin, tm=512, tf=256):
    n = x2.shape[0]
    assert n % tm == 0 and D_FF % tf == 0
    row = lambda width: pl.BlockSpec((tm, width), lambda i: (i, 0))
    return pl.pallas_call(
        functools.partial(_merge_kernel, tf=tf),
        out_shape=jax.ShapeDtypeStruct((n, D_MODEL), F32), grid=(n // tm,),
        in_specs=[row(D_MODEL), row(M_WIDTH), row(N_WIDTH), row(N_BRANCH * D_MODEL),
                  _const_spec(wbm.shape), _const_spec(wbn.shape), _const_spec(wo.shape),
                  _const_spec(gffn.shape), _const_spec(wg.shape), _const_spec(wu.shape),
                  _const_spec(wd.shape), _const_spec(gfin.shape)],
        out_specs=row(D_MODEL), name="merge_ffn",
        compiler_params=pltpu.CompilerParams(dimension_semantics=("arbitrary",),
                                             vmem_limit_bytes=VMEM_LIMIT),
    )(x2, ym, yn, mg, wbm, wbn, wo, gffn, wg, wu, wd, gfin)


def _nsa_constants(T):
    n_cmp = (T - CMP_BLOCK) // CMP_STRIDE + 1
    n_slc = T // SEL_BLOCK
    cs = np.arange(n_cmp) * CMP_STRIDE
    ss = np.arange(n_slc) * SEL_BLOCK
    ov = np.clip(np.minimum(cs[:, None] + CMP_BLOCK, ss[None, :] + SEL_BLOCK)
                 - np.maximum(cs[:, None], ss[None, :]), 0, None) / CMP_STRIDE
    ovt = np.zeros((n_slc, LANES), np.float32)
    ovt[:, :n_cmp] = ov.T
    et = (np.arange(T)[:, None] // SEL_BLOCK == np.arange(LANES)[None, :]).astype(np.float32) * (-NEG)
    return jnp.asarray(ovt), jnp.asarray(et, dtype=BF16)


def _compress_weights(pe_cmp, w_cmp1, w_cmp2):
    half = CMP_BLOCK // 2
    dh, hid = N_HEAD_DIM, CMP_HIDDEN
    eye = jnp.eye(N_KV_GROUPS, dtype=F32)
    w1 = w_cmp1.reshape(2, CMP_BLOCK, dh, hid)
    w1bd = jnp.einsum("jpdn,gk->jpgdkn", w1, eye).reshape(2, CMP_BLOCK, N_KV_GROUPS * dh, N_KV_GROUPS * hid)
    w2bd = jnp.einsum("jnd,gk->jgnkd", w_cmp2, eye).reshape(2, N_KV_GROUPS * hid, N_KV_GROUPS * dh)
    pe2 = jnp.tile(pe_cmp, (1, 1, N_KV_GROUPS))
    return pe2, w1bd[:, :half].astype(BF16), w1bd[:, half:].astype(BF16), w2bd.astype(BF16)


def kernel(x, g_norm_mix, w_in, b_in, b_fgate, conv_qk, g_mlstm_head, pe_cmp, w_cmp1, w_cmp2,
           rel_bias, w_branch, w_out, g_norm_ffn, w_gate, w_up, w_down, g_final):
    B, T, D = x.shape
    assert D == D_MODEL and w_in.shape[0] == 1, "one residual block (DEPTH == 1)"
    N = B * T
    x2 = x.reshape(N, D)

    idx, scale, const = _inproj_perm()
    w_r = _gather_cols(w_in[0], idx, scale).astype(BF16)
    b_r = _gather_cols(b_in[0].reshape(1, -1), idx, scale) + jnp.asarray(const).reshape(1, -1)
    post = np.concatenate([np.ones(M_WIDTH, np.float32), np.full(M_WIDTH, M_HEAD_DIM ** -0.5, np.float32)])
    cw = jnp.zeros((8, 2 * M_WIDTH), F32).at[:CONV_WIDTH].set(conv_qk[0]).at[CONV_WIDTH].set(jnp.asarray(post))
    mqk, mv, mo, nq, cin, nsw, mg, small = _inproj(x2, g_norm_mix[0].reshape(1, D), w_r, b_r, cw, T)

    bf = jnp.zeros((1, LANES), F32).at[0, :M_HEADS].set(b_fgate[0])
    ltri = jnp.asarray(np.tril(np.ones((M_CHUNK, M_CHUNK), np.float32)))
    r3 = lambda a: a.reshape(B, T, a.shape[-1])
    ym = _mlstm(r3(mqk), r3(mv), r3(mo), r3(small), bf, g_mlstm_head[0].reshape(1, M_WIDTH), ltri)

    pe2, w1t, w1b, w2 = _compress_weights(pe_cmp[0], w_cmp1[0], w_cmp2[0])
    ckv = _compress(cin.reshape(2, B, T, LANES), pe2, w1t, w1b, w2)
    tbl = rel_bias.astype(F32).T.reshape(-1)
    nqt = T // TQ
    bt = _bias_tiles(tbl, nqt + 1, "tok")
    wb = _bias_tiles(tbl, min(WINDOW // TQ + 1, nqt) + 1, "win")
    cb = _bias_tiles(tbl, nqt, "cmp", n_cmp=(T - CMP_BLOCK) // CMP_STRIDE + 1)
    ov, emat = _nsa_constants(T)
    yn = _nsa(r3(nq), r3(nsw), ckv, r3(small), bt, wb, cb, ov, emat)

    wbm = w_branch[0, 0].astype(BF16)
    wbn = jnp.concatenate(
        [w_branch[0, 1, (g * N_REP + r) * N_HEAD_DIM:(g * N_REP + r + 1) * N_HEAD_DIM]
         for r in range(N_REP) for g in range(N_KV_GROUPS)], axis=0).astype(BF16)
    out = _merge(x2, ym.reshape(N, M_WIDTH), yn.reshape(N, N_WIDTH), mg, wbm, wbn,
                 w_out[0].astype(BF16), g_norm_ffn[0].reshape(1, D), w_gate[0].astype(BF16),
                 w_up[0].astype(BF16), w_down[0].astype(BF16), g_final.reshape(1, D))
    return out.reshape(B, T, D)
```

```python
import functools

import numpy as np
import jax
import jax.numpy as jnp
from jax import lax
from jax.experimental import pallas as pl
from jax.experimental.pallas import tpu as pltpu

F32 = jnp.float32
BF16 = jnp.bfloat16
HI = lax.Precision.HIGHEST

D_MODEL = 1024
M_HEADS = 4
M_HEAD_DIM = 128
M_WIDTH = M_HEADS * M_HEAD_DIM
M_CHUNK = 128
CONV_WIDTH = 4
N_HEADS = 8
N_KV_GROUPS = 2
N_REP = N_HEADS // N_KV_GROUPS
N_HEAD_DIM = 64
N_WIDTH = N_HEADS * N_HEAD_DIM
N_KV_WIDTH = N_KV_GROUPS * N_HEAD_DIM
CMP_BLOCK = 32
CMP_STRIDE = 16
CMP_HIDDEN = 2 * N_HEAD_DIM
SEL_BLOCK = 64
SEL_TOPK = 16
WINDOW = 512
REL_BUCKETS = 32
REL_MAX_DIST = 1024
N_BRANCH = 2
D_FF = 2816
RMS_EPS = 1e-6
BIG = 1e9
NEG = -1e30

LANES = 128
TQ = 128
NSA_BATCH = 1
VMEM_LIMIT = 56 * 1024 * 1024

_OFF_MQ, _OFF_MK, _OFF_MV, _OFF_MO = 0, 512, 1024, 1536
_OFF_MI, _OFF_MF, _OFF_NQ, _OFF_NKV = 2048, 2052, 2056, 2568
_OFF_NGATE, _OFF_MERGE, _D_IN = 3336, 3360, 5408
_SMALL_NGATE = 8

_SEGS = (("mqk", 1024), ("mv", 512), ("mo", 512), ("nq", 512), ("kc", 128),
         ("vc", 128), ("nsw", 768), ("mg", 2048), ("small", 128))
_D_IN_PAD = sum(w for _, w in _SEGS)
LOG2E = 1.4426950408889634


def _dot(a, b, **kw):
    return jnp.dot(a, b, preferred_element_type=F32, **kw)


def _dot_nt(a, b):
    return lax.dot_general(a, b, (((1,), (1,)), ((), ())), preferred_element_type=F32)


def _dot_tn(a, b):
    return lax.dot_general(a, b, (((0,), (0,)), ((), ())), preferred_element_type=F32)


def _const_spec(shape):
    nd = len(shape)
    return pl.BlockSpec(shape, lambda *_: (0,) * nd, pipeline_mode=pl.Buffered(1))


def _bucket_thresholds():
    max_exact = REL_BUCKETS // 2
    assert REL_MAX_DIST == 64 * max_exact and REL_BUCKETS - max_exact == 16
    thr = []
    for k in range(1, REL_BUCKETS - max_exact):
        t = max_exact
        while t ** 8 < (max_exact ** 8) * (2 ** (3 * k)):
            t += 1
        thr.append(t)
    return tuple(thr)


_BUCKET_THR = _bucket_thresholds()


def _inproj_perm():
    idx = np.zeros((_D_IN_PAD,), np.int32)
    scale = np.zeros((_D_IN_PAD,), np.float32)
    const = np.zeros((_D_IN_PAD,), np.float32)
    pos = 0

    def put(cols, s=1.0):
        nonlocal pos
        n = len(cols)
        idx[pos:pos + n] = cols
        scale[pos:pos + n] = s
        pos += n

    def ones(n):
        nonlocal pos
        const[pos:pos + n] = 1.0
        pos += n

    put(np.arange(_OFF_MQ, _OFF_MV))
    put(np.arange(_OFF_MV, _OFF_MO))
    put(np.arange(_OFF_MO, _OFF_MI))
    nq = np.zeros((N_WIDTH,), np.int32)
    for r in range(N_REP):
        for g in range(N_KV_GROUPS):
            for d in range(N_HEAD_DIM):
                nq[r * 128 + g * 64 + d] = _OFF_NQ + (g * N_REP + r) * N_HEAD_DIM + d
    put(nq, N_HEAD_DIM ** -0.5 * LOG2E)
    kv = lambda j, g: np.arange(_OFF_NKV + (j * N_KV_GROUPS + g) * N_HEAD_DIM,
                                _OFF_NKV + (j * N_KV_GROUPS + g + 1) * N_HEAD_DIM)
    put(np.arange(_OFF_NKV, _OFF_NKV + 256))
    put(np.concatenate([kv(2, 0), kv(2, 1)]))
    put(np.concatenate([kv(4, 0), kv(4, 1)]))
    for j in (3, 5):
        put(kv(j, 0))
        ones(N_HEAD_DIM)
        ones(N_HEAD_DIM)
        put(kv(j, 1))
    put(np.arange(_OFF_MERGE, _D_IN))
    put(np.arange(_OFF_MI, _OFF_MI + 8))
    put(np.arange(_OFF_NGATE, _OFF_NGATE + 24))
    pos += LANES - 32
    assert pos == _D_IN_PAD
    return idx, scale, const


def _gather_cols(a, idx, scale):
    pieces = []
    start = 0
    n = len(idx)
    for c in range(1, n + 1):
        same = c < n and scale[c] == scale[start] and (scale[c] == 0.0 or idx[c] == idx[c - 1] + 1)
        if not same:
            if scale[start] == 0.0:
                piece = jnp.zeros((a.shape[0], c - start), a.dtype)
            else:
                piece = a[:, int(idx[start]):int(idx[start]) + (c - start)]
                if scale[start] != 1.0:
                    piece = piece * float(scale[start])
            pieces.append(piece)
            start = c
    return jnp.concatenate(pieces, axis=1)


def _inproj_kernel(x_ref, g_ref, w_ref, b_ref, mqk_ref, mv_ref, mo_ref, nq_ref,
                   cin_ref, nsw_ref, mg_ref, small_ref):
    x = x_ref[...]
    ms = jnp.mean(x * x, axis=-1, keepdims=True)
    u = (x * lax.rsqrt(ms + RMS_EPS) * g_ref[...]).astype(BF16)

    def seg(a, n):
        return _dot(u, w_ref[:, a:a + n]) + b_ref[:, a:a + n]

    off = {}
    pos = 0
    for name, w in _SEGS:
        off[name] = pos
        pos += w

    mqk_ref[...] = seg(off["mqk"], 1024).astype(BF16)
    mv_ref[...] = seg(off["mv"], 512).astype(BF16)
    mo_ref[...] = seg(off["mo"], 512).astype(BF16)
    nq_ref[...] = seg(off["nq"], 512).astype(BF16)
    cin_ref[0] = seg(off["kc"], 128)
    cin_ref[1] = seg(off["vc"], 128)
    nsw_ref[...] = seg(off["nsw"], 768).astype(BF16)
    mg_ref[...] = seg(off["mg"], 2048).astype(BF16)
    small_ref[...] = seg(off["small"], 128)


def _inproj(x2, g, w, b, tm=512):
    n = x2.shape[0]
    assert n % tm == 0
    row = lambda width: pl.BlockSpec((tm, width), lambda i: (i, 0))
    out_shape = (
        jax.ShapeDtypeStruct((n, 1024), BF16), jax.ShapeDtypeStruct((n, 512), BF16),
        jax.ShapeDtypeStruct((n, 512), BF16), jax.ShapeDtypeStruct((n, 512), BF16),
        jax.ShapeDtypeStruct((2, n, 128), F32), jax.ShapeDtypeStruct((n, 768), BF16),
        jax.ShapeDtypeStruct((n, 2048), BF16), jax.ShapeDtypeStruct((n, 128), F32))
    out_specs = (row(1024), row(512), row(512), row(512),
                 pl.BlockSpec((2, tm, 128), lambda i: (0, i, 0)), row(768), row(2048), row(128))
    return pl.pallas_call(
        _inproj_kernel, out_shape=out_shape, grid=(n // tm,),
        in_specs=[row(D_MODEL), _const_spec((1, D_MODEL)), _const_spec((D_MODEL, _D_IN_PAD)),
                  _const_spec((1, _D_IN_PAD))],
        out_specs=out_specs, name="inproj",
        compiler_params=pltpu.CompilerParams(dimension_semantics=("arbitrary",),
                                             vmem_limit_bytes=VMEM_LIMIT),
    )(x2, g, w, b)


def _mlstm_kernel(mqk_ref, mv_ref, mo_ref, small_ref, cw_ref, bf_ref, gh_ref, ltri_ref,
                  ym_ref, ebuf, c_st, m_st, g_bc, g_dt, g_a, g_mrow, g_row):
    L = M_CHUNK
    c = pl.program_id(1)
    n_chunks = g_bc.shape[0]
    rows = lax.broadcasted_iota(jnp.int32, (L, LANES), 0)

    @pl.when(c == 0)
    def _():
        ebuf[0:8, :] = jnp.zeros((8, 2 * M_WIDTH), F32)
        c_st[...] = jnp.zeros_like(c_st)
        m_st[...] = jnp.zeros_like(m_st)
        for cc in range(n_chunks):
            ig = small_ref[0, cc * L:(cc + 1) * L, :]
            fg = pltpu.roll(ig, LANES - M_HEADS, 1) + bf_ref[...]
            lf = jnp.minimum(fg, 0.0) - jnp.log1p(jnp.exp(-jnp.abs(fg)))
            bc = jnp.dot(ltri_ref[...], lf, precision=HI, preferred_element_type=F32)
            blast = bc[L - 1:L, :]
            a_all = blast - bc + ig
            d = ig - bc
            cm = d
            sh = 1
            while sh < L:
                cm = jnp.maximum(cm, jnp.where(rows >= sh, pltpu.roll(cm, sh, 0), -jnp.inf))
                sh *= 2
            g_bc[cc] = bc
            g_dt[cc] = d.T
            g_a[cc] = a_all
            g_mrow[cc] = bc + cm
            g_row[cc, 0:1, :] = blast
            g_row[cc, 1:2, :] = jnp.max(a_all, axis=0, keepdims=True)

    ebuf[8:8 + L, :] = mqk_ref[0].astype(F32)
    ext = ebuf[...]
    conv = cw_ref[CONV_WIDTH - 1:CONV_WIDTH, :] * ext[8:, :]
    for k in range(1, CONV_WIDTH):
        conv = conv + cw_ref[CONV_WIDTH - 1 - k:CONV_WIDTH - k, :] * pltpu.roll(ext, k, 0)[8:, :]
    ebuf[0:8, :] = ebuf[L:L + 8, :]
    qk = (conv * jax.nn.sigmoid(conv) * cw_ref[CONV_WIDTH:CONV_WIDTH + 1, :]).astype(BF16)

    bc = g_bc[c]
    dt = g_dt[c]
    a_all = g_a[c]
    blast = g_row[c, 0:1, :]
    m_prev = m_st[0:1, :]
    m_new = jnp.maximum(blast + m_prev, g_row[c, 1:2, :])
    decay = jnp.exp(blast + m_prev - m_new)
    w_all = jnp.exp(a_all - m_new)
    log_inter = bc + m_prev
    m_i_all = jnp.maximum(g_mrow[c], log_inter)
    s_inter_all = jnp.exp(log_inter - m_i_all)
    emi_all = jnp.exp(-m_i_all)
    u_all = bc - m_i_all

    row = lax.broadcasted_iota(jnp.int32, (L, L), 0)
    col = lax.broadcasted_iota(jnp.int32, (L, L), 1)
    causal = row >= col

    ones = jnp.ones((L, M_HEAD_DIM), BF16)
    for h in range(M_HEADS):
        sl = slice(h * M_HEAD_DIM, (h + 1) * M_HEAD_DIM)
        qb = qk[:, sl]
        kb = qk[:, M_WIDTH + h * M_HEAD_DIM:M_WIDTH + (h + 1) * M_HEAD_DIM]
        v = mv_ref[0, :, sl]
        cn_prev = c_st[h]

        logp = jnp.where(causal, u_all[:, h:h + 1] + dt[h:h + 1, :], -jnp.inf)
        p = jnp.exp(logp) * _dot_nt(qb, kb)
        nd = (_dot(p.astype(BF16), jnp.concatenate([v, ones], axis=1))
              + s_inter_all[:, h:h + 1] * _dot(qb, cn_prev.astype(BF16)))
        hh = nd[:, :M_HEAD_DIM] / jnp.maximum(jnp.abs(nd[:, M_HEAD_DIM:]), emi_all[:, h:h + 1])
        hh = hh * lax.rsqrt(jnp.mean(hh * hh, axis=-1, keepdims=True) + RMS_EPS) * gh_ref[:, sl]
        ym_ref[0, :, sl] = (jax.nn.sigmoid(mo_ref[0, :, sl].astype(F32)) * hh).astype(ym_ref.dtype)

        w_col = w_all[:, h:h + 1]
        dec = decay[:, h:h + 1]
        vw = jnp.concatenate([v.astype(F32) * w_col, jnp.broadcast_to(w_col, (L, M_HEAD_DIM))], axis=1)
        c_st[h] = dec * cn_prev + _dot_tn(kb, vw.astype(BF16))

    m_st[0:1, :] = m_new


def _mlstm(mqk, mv, mo, small, cw, bf, gh, ltri):
    B, T, _ = mqk.shape
    L = M_CHUNK
    assert T % L == 0
    nc = T // L
    blk = lambda w: pl.BlockSpec((1, L, w), lambda b, c: (b, c, 0))
    gate = lambda: pltpu.VMEM((nc, L, LANES), F32)
    return pl.pallas_call(
        _mlstm_kernel, out_shape=jax.ShapeDtypeStruct((B, T, M_WIDTH), BF16),
        grid=(B, nc),
        in_specs=[blk(2 * M_WIDTH), blk(M_WIDTH), blk(M_WIDTH),
                  pl.BlockSpec((1, T, LANES), lambda b, c: (b, 0, 0)),
                  _const_spec((8, 2 * M_WIDTH)), _const_spec((1, LANES)),
                  _const_spec((1, M_WIDTH)), _const_spec((L, L))],
        out_specs=blk(M_WIDTH),
        scratch_shapes=[pltpu.VMEM((L + 8, 2 * M_WIDTH), F32),
                        pltpu.VMEM((M_HEADS, M_HEAD_DIM, 2 * M_HEAD_DIM), F32),
                        pltpu.VMEM((8, LANES), F32),
                        gate(), gate(), gate(), gate(), pltpu.VMEM((nc, 8, LANES), F32)],
        name="mlstm",
        compiler_params=pltpu.CompilerParams(dimension_semantics=("arbitrary", "arbitrary")),
    )(mqk, mv, mo, small, cw, bf, gh, ltri)


def _compress_kernel(x_ref, pe_ref, w1t_ref, w1b_ref, w2_ref, o_ref, *, nch):
    half = CMP_BLOCK // 2
    top = jnp.zeros((nch, 2 * CMP_HIDDEN), F32)
    bot = jnp.zeros((nch, 2 * CMP_HIDDEN), F32)
    for p in range(half):
        xp = x_ref[0, 0, pl.ds(p, nch, stride=CMP_STRIDE), :]
        top = top + _dot((xp + pe_ref[0, p:p + 1, :]).astype(BF16), w1t_ref[0, p])
        bot = bot + _dot((xp + pe_ref[0, half + p:half + p + 1, :]).astype(BF16), w1b_ref[0, p])
    hid = top + pltpu.roll(bot, nch - 1, 0)
    act = hid * jax.nn.sigmoid(hid)
    o_ref[0, 0] = _dot(act.astype(BF16), w2_ref[0]).astype(o_ref.dtype)


def _compress(cin, pe2, w1t, w1b, w2):
    _, B, T, _ = cin.shape
    nch = T // CMP_STRIDE
    sel = lambda *shape: pl.BlockSpec((1,) + shape, lambda j, b: (j,) + (0,) * len(shape))
    return pl.pallas_call(
        functools.partial(_compress_kernel, nch=nch),
        out_shape=jax.ShapeDtypeStruct((2, B, nch, LANES), BF16),
        grid=(2, B),
        in_specs=[pl.BlockSpec((1, 1, T, LANES), lambda j, b: (j, b, 0, 0)),
                  sel(CMP_BLOCK, LANES), sel(CMP_BLOCK // 2, LANES, 2 * CMP_HIDDEN),
                  sel(CMP_BLOCK // 2, LANES, 2 * CMP_HIDDEN), sel(2 * CMP_HIDDEN, LANES)],
        out_specs=pl.BlockSpec((1, 1, nch, LANES), lambda j, b: (j, b, 0, 0)),
        name="compress",
        compiler_params=pltpu.CompilerParams(dimension_semantics=("arbitrary", "arbitrary")),
    )(cin, pe2, w1t, w1b, w2)


def _bias_kernel(tbl_ref, o_ref, *, kind, n_cmp):
    pid = pl.program_id(0)
    a = lax.broadcasted_iota(jnp.int32, (TQ, LANES), 0)
    b = lax.broadcasted_iota(jnp.int32, (TQ, LANES), 1)
    if kind == "cmp":
        dist = pid * TQ + a - (b * CMP_STRIDE + CMP_BLOCK - 1)
    else:
        dist = (pid - 1) * TQ + a - b
    n = jnp.maximum(dist, 0)
    cnt = jnp.zeros_like(n)
    for t in _BUCKET_THR:
        cnt = cnt + jnp.where(n >= t, 1, 0)
    bucket = jnp.where(n < REL_BUCKETS // 2, n, REL_BUCKETS // 2 + cnt)
    if kind == "tok":
        madd = jnp.where((dist >= 0) & (pid > 0), 0.0, NEG)
    elif kind == "win":
        madd = jnp.where((dist >= 0) & (dist < WINDOW) & (pid > 0), 0.0, NEG)
    else:
        madd = jnp.where((dist >= 0) & (b < n_cmp), 0.0, NEG)
    for h in range(N_HEADS):
        val = jnp.zeros((TQ, LANES), F32)
        for bb in range(REL_BUCKETS):
            val = jnp.where(bucket == bb, tbl_ref[h * REL_BUCKETS + bb], val)
        o_ref[0, h * TQ:(h + 1) * TQ, :] = val * LOG2E + madd


def _bias_tiles(tbl, n_tiles, kind, n_cmp=0):
    return pl.pallas_call(
        functools.partial(_bias_kernel, kind=kind, n_cmp=n_cmp),
        out_shape=jax.ShapeDtypeStruct((n_tiles, N_HEADS * TQ, LANES), F32),
        grid=(n_tiles,),
        in_specs=[pl.BlockSpec(memory_space=pltpu.SMEM)],
        out_specs=pl.BlockSpec((1, N_HEADS * TQ, LANES), lambda i: (i, 0, 0)),
        name="bias_" + kind,
        compiler_params=pltpu.CompilerParams(dimension_semantics=("arbitrary",)),
    )(tbl)


def _nsa_kernel(nq_ref, ks_ref, kw_ref, vs0_ref, vs1_ref, vw0_ref, vw1_ref, kc_ref, vc_ref, small_ref,
                bt_ref, wb_ref, cb_ref, ovt_ref, et_ref, yn_ref,
                lhs_sc, z_sc, *, nb, n_slc, n_top, n_win, n_chunks_max):
    qi = pl.program_id(1)
    HR = N_HEADS * TQ
    GR = N_REP * TQ
    CH = 4 * TQ
    vs_refs = (vs0_ref, vs1_ref)
    vw_refs = (vw0_ref, vw1_ref)
    lane = lax.broadcasted_iota(jnp.int32, (TQ, LANES), 1)
    lo = lane < N_HEAD_DIM

    def pair(o, r):
        return jnp.where(lo, o[r * TQ:(r + 1) * TQ], o[(N_REP + r) * TQ:(N_REP + r + 1) * TQ])

    def normed(acc, r):
        den = jnp.where(lo, acc[(N_REP + r) * TQ:(N_REP + r + 1) * TQ], acc[r * TQ:(r + 1) * TQ])
        return pair(acc, r) / pltpu.roll(den, N_HEAD_DIM, 1)

    def gate(sg, r, ci):
        c0 = _SMALL_NGATE + r * 3 + ci
        c1 = _SMALL_NGATE + (N_REP + r) * 3 + ci
        return jnp.where(lo, sg[:, c0:c0 + 1], sg[:, c1:c1 + 1])

    jb = lax.broadcasted_iota(jnp.int32, (n_slc, TQ), 0)
    tq = qi * TQ + lax.broadcasted_iota(jnp.int32, (n_slc, TQ), 1)
    cur = jnp.right_shift(tq, 6)
    forced = (jb == 0) | (jb == cur) | (jb == cur - 1)
    elig = jb <= cur
    t8 = qi * TQ + (lax.broadcasted_iota(jnp.int32, (HR, 1), 0) & (TQ - 1))
    w0 = jnp.maximum(qi - (n_win - 1), 0)
    woff = pl.multiple_of(w0 * TQ, TQ)

    def one_tile(s, n_chunks):
        q_all = nq_ref[s]
        zero = jnp.zeros((TQ, LANES), BF16)
        parts = []
        for g in range(N_KV_GROUPS):
            for r in range(N_REP):
                qr = q_all[:, r * LANES:(r + 1) * LANES]
                parts.append(jnp.where(lo if g == 0 else jnp.logical_not(lo), qr, zero))
        qs = jnp.concatenate(parts, axis=0)

        z = _dot_nt(qs, kc_ref[0, s]) + cb_ref[0]
        e = jnp.exp2(z - jnp.max(z, axis=-1, keepdims=True))
        rinv = jnp.where(t8 >= CMP_BLOCK - 1,
                         1.0 / jnp.maximum(jnp.sum(e, axis=-1, keepdims=True), 1e-30), 0.0)
        p_c = e * rinv
        o_c = _dot(p_c.astype(BF16), vc_ref[0, s])

        zw = []
        zmax = None
        for j0 in range(0, n_win, 2):
            nj = min(2, n_win - j0)
            sw = _dot_nt(qs, kw_ref[s, pl.ds(woff + j0 * TQ, nj * TQ), :])
            for u in range(nj):
                zj = sw[:, u * TQ:(u + 1) * TQ] + wb_ref[jnp.maximum(qi - w0 - (j0 + u) + 1, 0)]
                zw.append(zj)
                zmax = zj if zmax is None else jnp.maximum(zmax, zj)
        mw = jnp.max(zmax, axis=-1, keepdims=True)
        pw = jnp.concatenate([jnp.exp2(zj - mw).astype(BF16) for zj in zw], axis=1)
        acc_w = jnp.concatenate(
            [_dot(pw[g * GR:(g + 1) * GR], vw_refs[g][s, pl.ds(woff, n_win * TQ), :])
             for g in range(N_KV_GROUPS)], axis=0)

        all_selected = n_chunks * CH <= n_top * SEL_BLOCK
        for g in range(N_KV_GROUPS):
            if all_selected:
                sel = jnp.where(elig, 1.0, 0.0)
            else:
                psum = p_c[(g * N_REP) * TQ:(g * N_REP + 1) * TQ]
                for r in range(1, N_REP):
                    psum = psum + p_c[(g * N_REP + r) * TQ:(g * N_REP + r + 1) * TQ]
                imp = lax.dot_general(ovt_ref[...], psum, (((1,), (1,)), ((), ())),
                                      precision=HI, preferred_element_type=F32)
                score = jnp.where(elig, jnp.where(forced, BIG, imp), -BIG)
                cnt = jnp.zeros((n_slc, TQ), F32)
                for i in range(n_slc):
                    si = score[i:i + 1, :]
                    tie = jnp.where(jb > i, 1.0, 0.0)
                    cnt = cnt + jnp.where(si > score, 1.0, jnp.where(si == score, tie, 0.0))
                sel = jnp.where((cnt < n_top) & (score > -BIG / 2), 1.0, 0.0)
            seln = jnp.concatenate([sel - 1.0, jnp.zeros((LANES - n_slc, TQ), F32)], axis=0).T.astype(BF16)
            for r in range(N_REP):
                h = g * N_REP + r
                lhs_sc[s, h * TQ:(h + 1) * TQ, :] = jnp.concatenate([parts[h], seln], axis=1)

        zmax = None
        for c in range(n_chunks):
            for half in range(2):
                k0 = c * CH + half * 2 * TQ
                rhs = jnp.concatenate([ks_ref[s, k0:k0 + 2 * TQ, :], et_ref[k0:k0 + 2 * TQ, :]], axis=1)
                sz = _dot_nt(lhs_sc[s], rhs)
                for u in range(2):
                    kt = 4 * c + 2 * half + u
                    zt = sz[:, u * TQ:(u + 1) * TQ] + bt_ref[jnp.maximum(qi - kt + 1, 0)]
                    z_sc[s, c, :, (2 * half + u) * TQ:(2 * half + u + 1) * TQ] = zt
                    zmax = zt if zmax is None else jnp.maximum(zmax, zt)
        ms = jnp.max(zmax, axis=-1, keepdims=True)
        acc = [None] * N_KV_GROUPS
        for c in range(n_chunks):
            p = jnp.exp2(z_sc[s, c] - ms).astype(BF16)
            for g in range(N_KV_GROUPS):
                d = _dot(p[g * GR:(g + 1) * GR], vs_refs[g][s, c * CH:(c + 1) * CH, :])
                acc[g] = d if acc[g] is None else acc[g] + d
        acc_s = jnp.concatenate(acc, axis=0)

        sg = jax.nn.sigmoid(small_ref[s])
        for r in range(N_REP):
            out = (gate(sg, r, 0) * pair(o_c, r) + gate(sg, r, 1) * normed(acc_s, r)
                   + gate(sg, r, 2) * normed(acc_w, r))
            yn_ref[s, :, r * LANES:(r + 1) * LANES] = out.astype(yn_ref.dtype)

    for nc in range(1, n_chunks_max + 1):
        @pl.when(qi // 4 == nc - 1)
        def _(nc=nc):
            for s in range(nb):
                one_tile(s, nc)


def _nsa(nq, nsw, ckv, small, bt, wb, cb, ov, emat):
    B, T, _ = nq.shape
    assert T == 2048, "single 128-wide compressed-key tile assumes T == 2048"
    nqt = T // TQ
    n_cmp = (T - CMP_BLOCK) // CMP_STRIDE + 1
    n_slc = T // SEL_BLOCK
    n_top = min(SEL_TOPK, n_slc)
    n_win = wb.shape[0] - 1
    HR = N_HEADS * TQ
    assert nqt % 4 == 0 and n_slc % 8 == 0 and n_slc <= LANES
    nb = NSA_BATCH if B % NSA_BATCH == 0 else 1
    n_chunks_max = nqt // 4
    kv = lambda j: pl.BlockSpec((nb, T, LANES), lambda b, q: (b, 0, j), pipeline_mode=pl.Buffered(1))
    ck = lambda j: pl.BlockSpec((1, nb, T // CMP_STRIDE, LANES), lambda b, q: (j, b, 0, 0))
    kern = functools.partial(_nsa_kernel, nb=nb, n_slc=n_slc, n_top=n_top, n_win=n_win,
                             n_chunks_max=n_chunks_max)
    return pl.pallas_call(
        kern, out_shape=jax.ShapeDtypeStruct((B, T, N_WIDTH), BF16),
        grid=(B // nb, nqt),
        in_specs=[pl.BlockSpec((nb, TQ, N_WIDTH), lambda b, q: (b, q, 0)),
                  kv(0), kv(1), kv(2), kv(3), kv(4), kv(5), ck(0), ck(1),
                  pl.BlockSpec((nb, TQ, LANES), lambda b, q: (b, q, 0)),
                  _const_spec(bt.shape), _const_spec(wb.shape),
                  pl.BlockSpec((1, HR, LANES), lambda b, q: (q, 0, 0)),
                  _const_spec(ov.shape), _const_spec(emat.shape)],
        out_specs=pl.BlockSpec((nb, TQ, N_WIDTH), lambda b, q: (b, q, 0)),
        scratch_shapes=[pltpu.VMEM((nb, HR, 2 * LANES), BF16),
                        pltpu.VMEM((nb, n_chunks_max, HR, 4 * TQ), F32)],
        name="nsa",
        compiler_params=pltpu.CompilerParams(dimension_semantics=("arbitrary", "arbitrary"),
                                             vmem_limit_bytes=VMEM_LIMIT),
    )(nq, nsw, nsw, nsw, nsw, nsw, nsw, ckv, ckv, small, bt, wb, cb, ov, emat)


def _merge_kernel(x_ref, ym_ref, yn_ref, mg_ref, wbm_ref, wbn_ref, wo_ref, gffn_ref,
                  wg_ref, wu_ref, wd_ref, gfin_ref, o_ref, *, tf):
    bm = _dot(ym_ref[...], wbm_ref[...])
    bn = _dot(yn_ref[...], wbn_ref[...])
    mixed = (jax.nn.sigmoid(mg_ref[:, :D_MODEL].astype(F32)) * bm
             + jax.nn.sigmoid(mg_ref[:, D_MODEL:].astype(F32)) * bn)
    h = x_ref[...] + _dot(mixed.astype(BF16), wo_ref[...])
    f = (h * lax.rsqrt(jnp.mean(h * h, axis=-1, keepdims=True) + RMS_EPS) * gffn_ref[...]).astype(BF16)
    acc = jnp.zeros(h.shape, F32)
    for j in range(D_FF // tf):
        gg = _dot(f, wg_ref[:, j * tf:(j + 1) * tf])
        uu = _dot(f, wu_ref[:, j * tf:(j + 1) * tf])
        act = (gg * jax.nn.sigmoid(gg) * uu).astype(BF16)
        acc = acc + _dot(act, wd_ref[j * tf:(j + 1) * tf, :])
    h2 = h + acc
    o_ref[...] = h2 * lax.rsqrt(jnp.mean(h2 * h2, axis=-1, keepdims=True) + RMS_EPS) * gfin_ref[...]


def _merge(x2, ym, yn, mg, wbm, wbn, wo, gffn, wg, wu, wd, gfin, tm=512, tf=256):
    n = x2.shape[0]
    assert n % tm == 0 and D_FF % tf == 0
    row = lambda width: pl.BlockSpec((tm, width), lambda i: (i, 0))
    return pl.pallas_call(
        functools.partial(_merge_kernel, tf=tf),
        out_shape=jax.ShapeDtypeStruct((n, D_MODEL), F32), grid=(n // tm,),
        in_specs=[row(D_MODEL), row(M_WIDTH), row(N_WIDTH), row(N_BRANCH * D_MODEL),
                  _const_spec(wbm.shape), _const_spec(wbn.shape), _const_spec(wo.shape),
                  _const_spec(gffn.shape), _const_spec(wg.shape), _const_spec(wu.shape),
                  _const_spec(wd.shape), _const_spec(gfin.shape)],
        out_specs=row(D_MODEL), name="merge_ffn",
        compiler_params=pltpu.CompilerParams(dimension_semantics=("arbitrary",),
                                             vmem_limit_bytes=VMEM_LIMIT),
    )(x2, ym, yn, mg, wbm, wbn, wo, gffn, wg, wu, wd, gfin)


def _nsa_constants(T):
    n_cmp = (T - CMP_BLOCK) // CMP_STRIDE + 1
    n_slc = T // SEL_BLOCK
    cs = np.arange(n_cmp) * CMP_STRIDE
    ss = np.arange(n_slc) * SEL_BLOCK
    ov = np.clip(np.minimum(cs[:, None] + CMP_BLOCK, ss[None, :] + SEL_BLOCK)
                 - np.maximum(cs[:, None], ss[None, :]), 0, None) / CMP_STRIDE
    ovt = np.zeros((n_slc, LANES), np.float32)
    ovt[:, :n_cmp] = ov.T
    et = (np.arange(T)[:, None] // SEL_BLOCK == np.arange(LANES)[None, :]).astype(np.float32) * (-NEG)
    return jnp.asarray(ovt), jnp.asarray(et, dtype=BF16)


def _compress_weights(pe_cmp, w_cmp1, w_cmp2):
    half = CMP_BLOCK // 2
    dh, hid = N_HEAD_DIM, CMP_HIDDEN
    eye = jnp.eye(N_KV_GROUPS, dtype=F32)
    w1 = w_cmp1.reshape(2, CMP_BLOCK, dh, hid)
    w1bd = jnp.einsum("jpdn,gk->jpgdkn", w1, eye).reshape(2, CMP_BLOCK, N_KV_GROUPS * dh, N_KV_GROUPS * hid)
    w2bd = jnp.einsum("jnd,gk->jgnkd", w_cmp2, eye).reshape(2, N_KV_GROUPS * hid, N_KV_GROUPS * dh)
    pe2 = jnp.tile(pe_cmp, (1, 1, N_KV_GROUPS))
    return pe2, w1bd[:, :half].astype(BF16), w1bd[:, half:].astype(BF16), w2bd.astype(BF16)


def kernel(x, g_norm_mix, w_in, b_in, b_fgate, conv_qk, g_mlstm_head, pe_cmp, w_cmp1, w_cmp2,
           rel_bias, w_branch, w_out, g_norm_ffn, w_gate, w_up, w_down, g_final):
    B, T, D = x.shape
    assert D == D_MODEL and w_in.shape[0] == 1, "one residual block (DEPTH == 1)"
    N = B * T
    x2 = x.reshape(N, D)

    idx, scale, const = _inproj_perm()
    w_r = _gather_cols(w_in[0], idx, scale).astype(BF16)
    b_r = _gather_cols(b_in[0].reshape(1, -1), idx, scale) + jnp.asarray(const).reshape(1, -1)
    post = np.concatenate([np.ones(M_WIDTH, np.float32), np.full(M_WIDTH, M_HEAD_DIM ** -0.5, np.float32)])
    cw = jnp.zeros((8, 2 * M_WIDTH), F32).at[:CONV_WIDTH].set(conv_qk[0]).at[CONV_WIDTH].set(jnp.asarray(post))
    mqk, mv, mo, nq, cin, nsw, mg, small = _inproj(x2, g_norm_mix[0].reshape(1, D), w_r, b_r)

    bf = jnp.zeros((1, LANES), F32).at[0, :M_HEADS].set(b_fgate[0])
    ltri = jnp.asarray(np.tril(np.ones((M_CHUNK, M_CHUNK), np.float32)))
    r3 = lambda a: a.reshape(B, T, a.shape[-1])
    ym = _mlstm(r3(mqk), r3(mv), r3(mo), r3(small), cw, bf, g_mlstm_head[0].reshape(1, M_WIDTH), ltri)

    pe2, w1t, w1b, w2 = _compress_weights(pe_cmp[0], w_cmp1[0], w_cmp2[0])
    ckv = _compress(cin.reshape(2, B, T, LANES), pe2, w1t, w1b, w2)
    tbl = rel_bias.astype(F32).T.reshape(-1)
    nqt = T // TQ
    bt = _bias_tiles(tbl, nqt + 1, "tok")
    wb = _bias_tiles(tbl, min(WINDOW // TQ + 1, nqt) + 1, "win")
    cb = _bias_tiles(tbl, nqt, "cmp", n_cmp=(T - CMP_BLOCK) // CMP_STRIDE + 1)
    ov, emat = _nsa_constants(T)
    yn = _nsa(r3(nq), r3(nsw), ckv, r3(small), bt, wb, cb, ov, emat)

    wbm = w_branch[0, 0].astype(BF16)
    wbn = jnp.concatenate(
        [w_branch[0, 1, (g * N_REP + r) * N_HEAD_DIM:(g * N_REP + r + 1) * N_HEAD_DIM]
         for r in range(N_REP) for g in range(N_KV_GROUPS)], axis=0).astype(BF16)
    out = _merge(x2, ym.reshape(N, M_WIDTH), yn.reshape(N, N_WIDTH), mg, wbm, wbn,
                 w_out[0].astype(BF16), g_norm_ffn[0].reshape(1, D), w_gate[0].astype(BF16),
                 w_up[0].astype(BF16), w_down[0].astype(BF16), g_final.reshape(1, D))
    return out.reshape(B, T, D)
```

```python
import functools

import numpy as np
import jax
import jax.numpy as jnp
from jax import lax
from jax.experimental import pallas as pl
from jax.experimental.pallas import tpu as pltpu

F32 = jnp.float32
BF16 = jnp.bfloat16
HI = lax.Precision.HIGHEST

D_MODEL = 1024
M_HEADS = 4
M_HEAD_DIM = 128
M_WIDTH = M_HEADS * M_HEAD_DIM
M_CHUNK = 128
CONV_WIDTH = 4
N_HEADS = 8
N_KV_GROUPS = 2
N_REP = N_HEADS // N_KV_GROUPS
N_HEAD_DIM = 64
N_WIDTH = N_HEADS * N_HEAD_DIM
N_KV_WIDTH = N_KV_GROUPS * N_HEAD_DIM
CMP_BLOCK = 32
CMP_STRIDE = 16
CMP_HIDDEN = 2 * N_HEAD_DIM
SEL_BLOCK = 64
SEL_TOPK = 16
WINDOW = 512
REL_BUCKETS = 32
REL_MAX_DIST = 1024
N_BRANCH = 2
D_FF = 2816
RMS_EPS = 1e-6
BIG = 1e9
NEG = -1e30

LANES = 128
TQ = 128
NSA_BATCH = 1
VMEM_LIMIT = 56 * 1024 * 1024

_OFF_MQ, _OFF_MK, _OFF_MV, _OFF_MO = 0, 512, 1024, 1536
_OFF_MI, _OFF_MF, _OFF_NQ, _OFF_NKV = 2048, 2052, 2056, 2568
_OFF_NGATE, _OFF_MERGE, _D_IN = 3336, 3360, 5408
_SMALL_NGATE = 8

_SEGS = (("mqk", 1024), ("mv", 512), ("mo", 512), ("nq", 512), ("kc", 128),
         ("vc", 128), ("nsw", 512), ("mg", 2048), ("small", 128))
_D_IN_PAD = sum(w for _, w in _SEGS)
LOG2E = 1.4426950408889634


def _dot(a, b, **kw):
    return jnp.dot(a, b, preferred_element_type=F32, **kw)


def _dot_nt(a, b):
    return lax.dot_general(a, b, (((1,), (1,)), ((), ())), preferred_element_type=F32)


def _dot_tn(a, b):
    return lax.dot_general(a, b, (((0,), (0,)), ((), ())), preferred_element_type=F32)


def _const_spec(shape):
    nd = len(shape)
    return pl.BlockSpec(shape, lambda *_: (0,) * nd, pipeline_mode=pl.Buffered(1))


def _bucket_thresholds():
    max_exact = REL_BUCKETS // 2
    assert REL_MAX_DIST == 64 * max_exact and REL_BUCKETS - max_exact == 16
    thr = []
    for k in range(1, REL_BUCKETS - max_exact):
        t = max_exact
        while t ** 8 < (max_exact ** 8) * (2 ** (3 * k)):
            t += 1
        thr.append(t)
    return tuple(thr)


_BUCKET_THR = _bucket_thresholds()


def _inproj_perm():
    idx = np.zeros((_D_IN_PAD,), np.int32)
    scale = np.zeros((_D_IN_PAD,), np.float32)
    pos = 0

    def put(cols, s=1.0):
        nonlocal pos
        n = len(cols)
        idx[pos:pos + n] = cols
        scale[pos:pos + n] = s
        pos += n

    put(np.arange(_OFF_MQ, _OFF_MV))
    put(np.arange(_OFF_MV, _OFF_MO))
    put(np.arange(_OFF_MO, _OFF_MI))
    nq = np.zeros((N_WIDTH,), np.int32)
    for r in range(N_REP):
        for g in range(N_KV_GROUPS):
            for d in range(N_HEAD_DIM):
                nq[r * 128 + g * 64 + d] = _OFF_NQ + (g * N_REP + r) * N_HEAD_DIM + d
    put(nq, N_HEAD_DIM ** -0.5 * LOG2E)
    kv = lambda j, g: np.arange(_OFF_NKV + (j * N_KV_GROUPS + g) * N_HEAD_DIM,
                                _OFF_NKV + (j * N_KV_GROUPS + g + 1) * N_HEAD_DIM)
    put(np.arange(_OFF_NKV, _OFF_NKV + 256))
    put(np.concatenate([kv(2, 0), kv(2, 1)]))
    put(np.concatenate([kv(4, 0), kv(4, 1)]))
    put(np.concatenate([kv(3, 0), kv(3, 1)]))
    put(np.concatenate([kv(5, 0), kv(5, 1)]))
    put(np.arange(_OFF_MERGE, _D_IN))
    put(np.arange(_OFF_MI, _OFF_MI + 8))
    put(np.arange(_OFF_NGATE, _OFF_NGATE + 24))
    pos += LANES - 32
    assert pos == _D_IN_PAD
    return idx, scale


def _gather_cols(a, idx, scale, dtype):
    pieces = []
    start = 0
    n = len(idx)
    for c in range(1, n + 1):
        same = c < n and scale[c] == scale[start] and (scale[c] == 0.0 or idx[c] == idx[c - 1] + 1)
        if not same:
            if scale[start] == 0.0:
                piece = jnp.zeros((a.shape[0], c - start), dtype)
            else:
                piece = a[:, int(idx[start]):int(idx[start]) + (c - start)]
                if scale[start] != 1.0:
                    piece = piece * float(scale[start])
            pieces.append(piece.astype(dtype))
            start = c
    return jnp.concatenate(pieces, axis=1)


def _inproj_kernel(x_ref, w_ref, b_ref, mqk_ref, mv_ref, mo_ref, nq_ref,
                   cin_ref, nsw_ref, mg_ref, small_ref):
    x = x_ref[...]
    rs = lax.rsqrt(jnp.mean(x * x, axis=-1, keepdims=True) + RMS_EPS)
    xb = x.astype(BF16)

    def seg(a, n):
        return _dot(xb, w_ref[:, a:a + n]) * rs + b_ref[:, a:a + n]

    off = {}
    pos = 0
    for name, w in _SEGS:
        off[name] = pos
        pos += w

    mqk_ref[...] = seg(off["mqk"], 1024).astype(BF16)
    mv_ref[...] = seg(off["mv"], 512).astype(BF16)
    mo_ref[...] = seg(off["mo"], 512).astype(BF16)
    nq_ref[...] = seg(off["nq"], 512).astype(BF16)
    cc = seg(off["kc"], 2 * LANES)
    cin_ref[0] = cc[:, :LANES]
    cin_ref[1] = cc[:, LANES:]
    nsw_ref[:, 0:2 * LANES] = seg(off["nsw"], 2 * LANES).astype(BF16)
    vv = seg(off["nsw"] + 2 * LANES, 2 * LANES)
    lo = lax.broadcasted_iota(jnp.int32, (x.shape[0], LANES), 1) < N_HEAD_DIM
    for j in range(2):
        vj = vv[:, j * LANES:(j + 1) * LANES]
        nsw_ref[:, (2 + 2 * j) * LANES:(3 + 2 * j) * LANES] = jnp.where(lo, vj, 1.0).astype(BF16)
        nsw_ref[:, (3 + 2 * j) * LANES:(4 + 2 * j) * LANES] = jnp.where(lo, 1.0, vj).astype(BF16)
    mg_ref[...] = seg(off["mg"], 2048).astype(BF16)
    small_ref[...] = seg(off["small"], 128)


def _inproj(x2, w, b, tm=512):
    n = x2.shape[0]
    assert n % tm == 0
    row = lambda width: pl.BlockSpec((tm, width), lambda i: (i, 0))
    out_shape = (
        jax.ShapeDtypeStruct((n, 1024), BF16), jax.ShapeDtypeStruct((n, 512), BF16),
        jax.ShapeDtypeStruct((n, 512), BF16), jax.ShapeDtypeStruct((n, 512), BF16),
        jax.ShapeDtypeStruct((2, n, 128), F32), jax.ShapeDtypeStruct((n, 768), BF16),
        jax.ShapeDtypeStruct((n, 2048), BF16), jax.ShapeDtypeStruct((n, 128), F32))
    out_specs = (row(1024), row(512), row(512), row(512),
                 pl.BlockSpec((2, tm, 128), lambda i: (0, i, 0)), row(768), row(2048), row(128))
    return pl.pallas_call(
        _inproj_kernel, out_shape=out_shape, grid=(n // tm,),
        in_specs=[row(D_MODEL), _const_spec((D_MODEL, _D_IN_PAD)), _const_spec((1, _D_IN_PAD))],
        out_specs=out_specs, name="inproj",
        compiler_params=pltpu.CompilerParams(dimension_semantics=("arbitrary",),
                                             vmem_limit_bytes=VMEM_LIMIT),
    )(x2, w, b)


def _mlstm_kernel(mqk_ref, mv_ref, mo_ref, small_ref, cw_ref, bf_ref, gh_ref, ltri_ref,
                  ym_ref, ebuf, c_st, m_st, g_bc, g_dt, g_a, g_mrow, g_row):
    L = M_CHUNK
    c = pl.program_id(1)
    n_chunks = g_bc.shape[0]
    rows = lax.broadcasted_iota(jnp.int32, (L, LANES), 0)

    @pl.when(c == 0)
    def _():
        ebuf[0:8, :] = jnp.zeros((8, 2 * M_WIDTH), F32)
        c_st[...] = jnp.zeros_like(c_st)
        m_st[...] = jnp.zeros_like(m_st)
        for cc in range(n_chunks):
            ig = small_ref[0, cc * L:(cc + 1) * L, :]
            fg = pltpu.roll(ig, LANES - M_HEADS, 1) + bf_ref[...]
            lf = jnp.minimum(fg, 0.0) - jnp.log1p(jnp.exp(-jnp.abs(fg)))
            bc = jnp.dot(ltri_ref[...], lf, precision=HI, preferred_element_type=F32)
            blast = bc[L - 1:L, :]
            a_all = blast - bc + ig
            d = ig - bc
            cm = d
            sh = 1
            while sh < L:
                cm = jnp.maximum(cm, jnp.where(rows >= sh, pltpu.roll(cm, sh, 0), -jnp.inf))
                sh *= 2
            g_bc[cc] = bc
            g_dt[cc] = d.T
            g_a[cc] = a_all
            g_mrow[cc] = bc + cm
            g_row[cc, 0:1, :] = blast
            g_row[cc, 1:2, :] = jnp.max(a_all, axis=0, keepdims=True)

    ebuf[8:8 + L, :] = mqk_ref[0].astype(F32)
    ext = ebuf[...]
    conv = cw_ref[CONV_WIDTH - 1:CONV_WIDTH, :] * ext[8:, :]
    for k in range(1, CONV_WIDTH):
        conv = conv + cw_ref[CONV_WIDTH - 1 - k:CONV_WIDTH - k, :] * pltpu.roll(ext, k, 0)[8:, :]
    ebuf[0:8, :] = ebuf[L:L + 8, :]
    qk = (conv * jax.nn.sigmoid(conv) * cw_ref[CONV_WIDTH:CONV_WIDTH + 1, :]).astype(BF16)

    bc = g_bc[c]
    dt = g_dt[c]
    a_all = g_a[c]
    blast = g_row[c, 0:1, :]
    m_prev = m_st[0:1, :]
    m_new = jnp.maximum(blast + m_prev, g_row[c, 1:2, :])
    decay = jnp.exp(blast + m_prev - m_new)
    w_all = jnp.exp(a_all - m_new)
    log_inter = bc + m_prev
    m_i_all = jnp.maximum(g_mrow[c], log_inter)
    s_inter_all = jnp.exp(log_inter - m_i_all)
    emi_all = jnp.exp(-m_i_all)
    u_all = bc - m_i_all

    row = lax.broadcasted_iota(jnp.int32, (L, L), 0)
    col = lax.broadcasted_iota(jnp.int32, (L, L), 1)
    causal = row >= col

    ones = jnp.ones((L, M_HEAD_DIM), BF16)
    for h in range(M_HEADS):
        sl = slice(h * M_HEAD_DIM, (h + 1) * M_HEAD_DIM)
        qb = qk[:, sl]
        kb = qk[:, M_WIDTH + h * M_HEAD_DIM:M_WIDTH + (h + 1) * M_HEAD_DIM]
        v = mv_ref[0, :, sl]
        cn_prev = c_st[h]

        logp = jnp.where(causal, u_all[:, h:h + 1] + dt[h:h + 1, :], -jnp.inf)
        p = jnp.exp(logp) * _dot_nt(qb, kb)
        nd = (_dot(p.astype(BF16), jnp.concatenate([v, ones], axis=1))
              + s_inter_all[:, h:h + 1] * _dot(qb, cn_prev.astype(BF16)))
        hh = nd[:, :M_HEAD_DIM] / jnp.maximum(jnp.abs(nd[:, M_HEAD_DIM:]), emi_all[:, h:h + 1])
        hh = hh * lax.rsqrt(jnp.mean(hh * hh, axis=-1, keepdims=True) + RMS_EPS) * gh_ref[:, sl]
        ym_ref[0, :, sl] = (jax.nn.sigmoid(mo_ref[0, :, sl].astype(F32)) * hh).astype(ym_ref.dtype)

        w_col = w_all[:, h:h + 1]
        dec = decay[:, h:h + 1]
        vw = jnp.concatenate([v.astype(F32) * w_col, jnp.broadcast_to(w_col, (L, M_HEAD_DIM))], axis=1)
        c_st[h] = dec * cn_prev + _dot_tn(kb, vw.astype(BF16))

    m_st[0:1, :] = m_new


def _mlstm(mqk, mv, mo, small, cw, bf, gh, ltri):
    B, T, _ = mqk.shape
    L = M_CHUNK
    assert T % L == 0
    nc = T // L
    blk = lambda w: pl.BlockSpec((1, L, w), lambda b, c: (b, c, 0))
    gate = lambda: pltpu.VMEM((nc, L, LANES), F32)
    return pl.pallas_call(
        _mlstm_kernel, out_shape=jax.ShapeDtypeStruct((B, T, M_WIDTH), BF16),
        grid=(B, nc),
        in_specs=[blk(2 * M_WIDTH), blk(M_WIDTH), blk(M_WIDTH),
                  pl.BlockSpec((1, T, LANES), lambda b, c: (b, 0, 0)),
                  _const_spec((8, 2 * M_WIDTH)), _const_spec((1, LANES)),
                  _const_spec((1, M_WIDTH)), _const_spec((L, L))],
        out_specs=blk(M_WIDTH),
        scratch_shapes=[pltpu.VMEM((L + 8, 2 * M_WIDTH), F32),
                        pltpu.VMEM((M_HEADS, M_HEAD_DIM, 2 * M_HEAD_DIM), F32),
                        pltpu.VMEM((8, LANES), F32),
                        gate(), gate(), gate(), gate(), pltpu.VMEM((nc, 8, LANES), F32)],
        name="mlstm",
        compiler_params=pltpu.CompilerParams(dimension_semantics=("arbitrary", "arbitrary")),
    )(mqk, mv, mo, small, cw, bf, gh, ltri)


def _compress_kernel(x_ref, pe_ref, w1t_ref, w1b_ref, w2_ref, o_ref, *, nch):
    half = CMP_BLOCK // 2
    top = jnp.zeros((nch, 2 * CMP_HIDDEN), F32)
    bot = jnp.zeros((nch, 2 * CMP_HIDDEN), F32)
    for p in range(half):
        xp = x_ref[0, 0, pl.ds(p, nch, stride=CMP_STRIDE), :]
        top = top + _dot((xp + pe_ref[0, p:p + 1, :]).astype(BF16), w1t_ref[0, p])
        bot = bot + _dot((xp + pe_ref[0, half + p:half + p + 1, :]).astype(BF16), w1b_ref[0, p])
    hid = top + pltpu.roll(bot, nch - 1, 0)
    act = hid * jax.nn.sigmoid(hid)
    o_ref[0, 0] = _dot(act.astype(BF16), w2_ref[0]).astype(o_ref.dtype)


def _compress(cin, pe2, w1t, w1b, w2):
    _, B, T, _ = cin.shape
    nch = T // CMP_STRIDE
    sel = lambda *shape: pl.BlockSpec((1,) + shape, lambda j, b: (j,) + (0,) * len(shape))
    return pl.pallas_call(
        functools.partial(_compress_kernel, nch=nch),
        out_shape=jax.ShapeDtypeStruct((2, B, nch, LANES), BF16),
        grid=(2, B),
        in_specs=[pl.BlockSpec((1, 1, T, LANES), lambda j, b: (j, b, 0, 0)),
                  sel(CMP_BLOCK, LANES), sel(CMP_BLOCK // 2, LANES, 2 * CMP_HIDDEN),
                  sel(CMP_BLOCK // 2, LANES, 2 * CMP_HIDDEN), sel(2 * CMP_HIDDEN, LANES)],
        out_specs=pl.BlockSpec((1, 1, nch, LANES), lambda j, b: (j, b, 0, 0)),
        name="compress",
        compiler_params=pltpu.CompilerParams(dimension_semantics=("arbitrary", "arbitrary")),
    )(cin, pe2, w1t, w1b, w2)


def _bias_kernel(tbl_ref, o_ref, *, kind, n_cmp):
    pid = pl.program_id(0)
    toeplitz = kind != "cmp"
    if toeplitz:
        k = lax.broadcasted_iota(jnp.int32, (8, 2 * LANES), 1)
        dist = (pid - 1) * TQ + jnp.where(k < LANES, -k, 2 * LANES - k)
    else:
        a = lax.broadcasted_iota(jnp.int32, (TQ, LANES), 0)
        b = lax.broadcasted_iota(jnp.int32, (TQ, LANES), 1)
        dist = pid * TQ + a - (b * CMP_STRIDE + CMP_BLOCK - 1)
    n = jnp.maximum(dist, 0)
    cnt = jnp.zeros_like(n)
    for t in _BUCKET_THR:
        cnt = cnt + jnp.where(n >= t, 1, 0)
    bucket = jnp.where(n < REL_BUCKETS // 2, n, REL_BUCKETS // 2 + cnt)
    if kind == "tok":
        madd = jnp.where((dist >= 0) & (pid > 0), 0.0, NEG)
    elif kind == "win":
        madd = jnp.where((dist >= 0) & (dist < WINDOW) & (pid > 0), 0.0, NEG)
    else:
        madd = jnp.where((dist >= 0) & (b < n_cmp), 0.0, NEG)
    for h in range(N_HEADS):
        val = jnp.zeros(dist.shape, F32)
        for bb in range(REL_BUCKETS):
            val = jnp.where(bucket == bb, tbl_ref[h * REL_BUCKETS + bb], val)
        val = val * LOG2E + madd
        if toeplitz:
            g = jnp.broadcast_to(val[0:1, :], (TQ, 2 * LANES))
            val = pltpu.roll(g, 0, 1, stride=1, stride_axis=0)[:, :LANES]
        o_ref[0, h * TQ:(h + 1) * TQ, :] = val


def _bias_tiles(tbl, n_tiles, kind, n_cmp=0):
    return pl.pallas_call(
        functools.partial(_bias_kernel, kind=kind, n_cmp=n_cmp),
        out_shape=jax.ShapeDtypeStruct((n_tiles, N_HEADS * TQ, LANES), F32),
        grid=(n_tiles,),
        in_specs=[pl.BlockSpec(memory_space=pltpu.SMEM)],
        out_specs=pl.BlockSpec((1, N_HEADS * TQ, LANES), lambda i: (i, 0, 0)),
        name="bias_" + kind,
        compiler_params=pltpu.CompilerParams(dimension_semantics=("arbitrary",)),
    )(tbl)


def _nsa_kernel(nq_ref, ks_ref, kw_ref, vs0_ref, vs1_ref, vw0_ref, vw1_ref, kc_ref, vc_ref, small_ref,
                bt_ref, wb_ref, cb_ref, ovt_ref, et_ref, yn_ref,
                lhs_sc, z_sc, *, nb, n_slc, n_top, n_win, n_chunks_max):
    qi = pl.program_id(1)
    HR = N_HEADS * TQ
    GR = N_REP * TQ
    CH = 4 * TQ
    vs_refs = (vs0_ref, vs1_ref)
    vw_refs = (vw0_ref, vw1_ref)
    lane = lax.broadcasted_iota(jnp.int32, (TQ, LANES), 1)
    lo = lane < N_HEAD_DIM

    def pair(o, r):
        return jnp.where(lo, o[r * TQ:(r + 1) * TQ], o[(N_REP + r) * TQ:(N_REP + r + 1) * TQ])

    def normed(acc, r):
        den = jnp.where(lo, acc[(N_REP + r) * TQ:(N_REP + r + 1) * TQ], acc[r * TQ:(r + 1) * TQ])
        return pair(acc, r) / pltpu.roll(den, N_HEAD_DIM, 1)

    def gate(sg, r, ci):
        c0 = _SMALL_NGATE + r * 3 + ci
        c1 = _SMALL_NGATE + (N_REP + r) * 3 + ci
        return jnp.where(lo, sg[:, c0:c0 + 1], sg[:, c1:c1 + 1])

    jb = lax.broadcasted_iota(jnp.int32, (n_slc, TQ), 0)
    tq = qi * TQ + lax.broadcasted_iota(jnp.int32, (n_slc, TQ), 1)
    cur = jnp.right_shift(tq, 6)
    forced = (jb == 0) | (jb == cur) | (jb == cur - 1)
    elig = jb <= cur
    t8 = qi * TQ + (lax.broadcasted_iota(jnp.int32, (HR, 1), 0) & (TQ - 1))
    w0 = jnp.maximum(qi - (n_win - 1), 0)
    woff = pl.multiple_of(w0 * TQ, TQ)

    def one_tile(s, n_chunks):
        q_all = nq_ref[s]
        zero = jnp.zeros((TQ, LANES), BF16)
        parts = []
        for g in range(N_KV_GROUPS):
            for r in range(N_REP):
                qr = q_all[:, r * LANES:(r + 1) * LANES]
                parts.append(jnp.where(lo if g == 0 else jnp.logical_not(lo), qr, zero))
        qs = jnp.concatenate(parts, axis=0)

        z = _dot_nt(qs, kc_ref[0, s]) + cb_ref[0]
        e = jnp.exp2(z - jnp.max(z, axis=-1, keepdims=True))
        rinv = jnp.where(t8 >= CMP_BLOCK - 1,
                         1.0 / jnp.maximum(jnp.sum(e, axis=-1, keepdims=True), 1e-30), 0.0)
        p_c = e * rinv
        o_c = _dot(p_c.astype(BF16), vc_ref[0, s])

        zw = []
        zmax = None
        for j0 in range(0, n_win, 2):
            nj = min(2, n_win - j0)
            sw = _dot_nt(qs, kw_ref[s, pl.ds(woff + j0 * TQ, nj * TQ), :])
            for u in range(nj):
                zj = sw[:, u * TQ:(u + 1) * TQ] + wb_ref[jnp.maximum(qi - w0 - (j0 + u) + 1, 0)]
                zw.append(zj)
                zmax = zj if zmax is None else jnp.maximum(zmax, zj)
        mw = jnp.max(zmax, axis=-1, keepdims=True)
        pw = jnp.concatenate([jnp.exp2(zj - mw).astype(BF16) for zj in zw], axis=1)
        acc_w = jnp.concatenate(
            [_dot(pw[g * GR:(g + 1) * GR], vw_refs[g][s, pl.ds(woff, n_win * TQ), :])
             for g in range(N_KV_GROUPS)], axis=0)

        all_selected = n_chunks * CH <= n_top * SEL_BLOCK
        for g in range(N_KV_GROUPS):
            if all_selected:
                sel = jnp.where(elig, 1.0, 0.0)
            else:
                psum = p_c[(g * N_REP) * TQ:(g * N_REP + 1) * TQ]
                for r in range(1, N_REP):
                    psum = psum + p_c[(g * N_REP + r) * TQ:(g * N_REP + r + 1) * TQ]
                imp = lax.dot_general(ovt_ref[...], psum, (((1,), (1,)), ((), ())),
                                      precision=HI, preferred_element_type=F32)
                score = jnp.where(elig, jnp.where(forced, BIG, imp), -BIG)
                cnt = jnp.zeros((n_slc, TQ), F32)
                for i in range(n_slc):
                    si = score[i:i + 1, :]
                    tie = jnp.where(jb > i, 1.0, 0.0)
                    cnt = cnt + jnp.where(si > score, 1.0, jnp.where(si == score, tie, 0.0))
                sel = jnp.where((cnt < n_top) & (score > -BIG / 2), 1.0, 0.0)
            seln = jnp.concatenate([sel - 1.0, jnp.zeros((LANES - n_slc, TQ), F32)], axis=0).T.astype(BF16)
            for r in range(N_REP):
                h = g * N_REP + r
                lhs_sc[s, h * TQ:(h + 1) * TQ, :] = jnp.concatenate([parts[h], seln], axis=1)

        zmax = None
        for c in range(n_chunks):
            for half in range(2):
                k0 = c * CH + half * 2 * TQ
                rhs = jnp.concatenate([ks_ref[s, k0:k0 + 2 * TQ, :], et_ref[k0:k0 + 2 * TQ, :]], axis=1)
                sz = _dot_nt(lhs_sc[s], rhs)
                for u in range(2):
                    kt = 4 * c + 2 * half + u
                    zt = sz[:, u * TQ:(u + 1) * TQ] + bt_ref[jnp.maximum(qi - kt + 1, 0)]
                    z_sc[s, c, :, (2 * half + u) * TQ:(2 * half + u + 1) * TQ] = zt
                    zmax = zt if zmax is None else jnp.maximum(zmax, zt)
        ms = jnp.max(zmax, axis=-1, keepdims=True)
        acc = [None] * N_KV_GROUPS
        for c in range(n_chunks):
            p = jnp.exp2(z_sc[s, c] - ms).astype(BF16)
            for g in range(N_KV_GROUPS):
                d = _dot(p[g * GR:(g + 1) * GR], vs_refs[g][s, c * CH:(c + 1) * CH, :])
                acc[g] = d if acc[g] is None else acc[g] + d
        acc_s = jnp.concatenate(acc, axis=0)

        sg = jax.nn.sigmoid(small_ref[s])
        for r in range(N_REP):
            out = (gate(sg, r, 0) * pair(o_c, r) + gate(sg, r, 1) * normed(acc_s, r)
                   + gate(sg, r, 2) * normed(acc_w, r))
            yn_ref[s, :, r * LANES:(r + 1) * LANES] = out.astype(yn_ref.dtype)

    for nc in range(1, n_chunks_max + 1):
        @pl.when(qi // 4 == nc - 1)
        def _(nc=nc):
            for s in range(nb):
                one_tile(s, nc)


def _nsa(nq, nsw, ckv, small, bt, wb, cb, ov, emat):
    B, T, _ = nq.shape
    assert T == 2048, "single 128-wide compressed-key tile assumes T == 2048"
    nqt = T // TQ
    n_cmp = (T - CMP_BLOCK) // CMP_STRIDE + 1
    n_slc = T // SEL_BLOCK
    n_top = min(SEL_TOPK, n_slc)
    n_win = wb.shape[0] - 1
    HR = N_HEADS * TQ
    assert nqt % 4 == 0 and n_slc % 8 == 0 and n_slc <= LANES
    nb = NSA_BATCH if B % NSA_BATCH == 0 else 1
    n_chunks_max = nqt // 4
    kv = lambda j: pl.BlockSpec((nb, T, LANES), lambda b, q: (b, 0, j), pipeline_mode=pl.Buffered(1))
    ck = lambda j: pl.BlockSpec((1, nb, T // CMP_STRIDE, LANES), lambda b, q: (j, b, 0, 0))
    kern = functools.partial(_nsa_kernel, nb=nb, n_slc=n_slc, n_top=n_top, n_win=n_win,
                             n_chunks_max=n_chunks_max)
    return pl.pallas_call(
        kern, out_shape=jax.ShapeDtypeStruct((B, T, N_WIDTH), BF16),
        grid=(B // nb, nqt),
        in_specs=[pl.BlockSpec((nb, TQ, N_WIDTH), lambda b, q: (b, q, 0)),
                  kv(0), kv(1), kv(2), kv(3), kv(4), kv(5), ck(0), ck(1),
                  pl.BlockSpec((nb, TQ, LANES), lambda b, q: (b, q, 0)),
                  _const_spec(bt.shape), _const_spec(wb.shape),
                  pl.BlockSpec((1, HR, LANES), lambda b, q: (q, 0, 0)),
                  _const_spec(ov.shape), _const_spec(emat.shape)],
        out_specs=pl.BlockSpec((nb, TQ, N_WIDTH), lambda b, q: (b, q, 0)),
        scratch_shapes=[pltpu.VMEM((nb, HR, 2 * LANES), BF16),
                        pltpu.VMEM((nb, n_chunks_max, HR, 4 * TQ), F32)],
        name="nsa",
        compiler_params=pltpu.CompilerParams(dimension_semantics=("arbitrary", "arbitrary"),
                                             vmem_limit_bytes=VMEM_LIMIT),
    )(nq, nsw, nsw, nsw, nsw, nsw, nsw, ckv, ckv, small, bt, wb, cb, ov, emat)


def _merge_kernel(x_ref, ym_ref, yn_ref, mg_ref, wbm_ref, wbn_ref, wo_ref,
                  wg_ref, wu_ref, wd_ref, gfin_ref, o_ref, *, tf):
    bm = _dot(ym_ref[...], wbm_ref[...])
    bn = _dot(yn_ref[...], wbn_ref[...])
    mixed = (jax.nn.sigmoid(mg_ref[:, :D_MODEL].astype(F32)) * bm
             + jax.nn.sigmoid(mg_ref[:, D_MODEL:].astype(F32)) * bn)
    h = x_ref[...] + _dot(mixed.astype(BF16), wo_ref[...])
    rs = lax.rsqrt(jnp.mean(h * h, axis=-1, keepdims=True) + RMS_EPS)
    hb = h.astype(BF16)
    acc = jnp.zeros(h.shape, F32)
    for j in range(D_FF // tf):
        gg = _dot(hb, wg_ref[:, j * tf:(j + 1) * tf]) * rs
        uu = _dot(hb, wu_ref[:, j * tf:(j + 1) * tf]) * rs
        act = (gg * jax.nn.sigmoid(gg) * uu).astype(BF16)
        acc = acc + _dot(act, wd_ref[j * tf:(j + 1) * tf, :])
    h2 = h + acc
    o_ref[...] = h2 * lax.rsqrt(jnp.mean(h2 * h2, axis=-1, keepdims=True) + RMS_EPS) * gfin_ref[...]


def _merge(x2, ym, yn, mg, wbm, wbn, wo, wg, wu, wd, gfin, tm=512, tf=256):
    n = x2.shape[0]
    assert n % tm == 0 and D_FF % tf == 0
    row = lambda width: pl.BlockSpec((tm, width), lambda i: (i, 0))
    return pl.pallas_call(
        functools.partial(_merge_kernel, tf=tf),
        out_shape=jax.ShapeDtypeStruct((n, D_MODEL), F32), grid=(n // tm,),
        in_specs=[row(D_MODEL), row(M_WIDTH), row(N_WIDTH), row(N_BRANCH * D_MODEL),
                  _const_spec(wbm.shape), _const_spec(wbn.shape), _const_spec(wo.shape),
                  _const_spec(wg.shape), _const_spec(wu.shape),
                  _const_spec(wd.shape), _const_spec(gfin.shape)],
        out_specs=row(D_MODEL), name="merge_ffn",
        compiler_params=pltpu.CompilerParams(dimension_semantics=("arbitrary",),
                                             vmem_limit_bytes=VMEM_LIMIT),
    )(x2, ym, yn, mg, wbm, wbn, wo, wg, wu, wd, gfin)


def _nsa_constants(T):
    n_cmp = (T - CMP_BLOCK) // CMP_STRIDE + 1
    n_slc = T // SEL_BLOCK
    cs = np.arange(n_cmp) * CMP_STRIDE
    ss = np.arange(n_slc) * SEL_BLOCK
    ov = np.clip(np.minimum(cs[:, None] + CMP_BLOCK, ss[None, :] + SEL_BLOCK)
                 - np.maximum(cs[:, None], ss[None, :]), 0, None) / CMP_STRIDE
    ovt = np.zeros((n_slc, LANES), np.float32)
    ovt[:, :n_cmp] = ov.T
    et = (np.arange(T)[:, None] // SEL_BLOCK == np.arange(LANES)[None, :]).astype(np.float32) * (-NEG)
    return jnp.asarray(ovt), jnp.asarray(et, dtype=BF16)


def _compress_weights(pe_cmp, w_cmp1, w_cmp2):
    half = CMP_BLOCK // 2
    dh, hid = N_HEAD_DIM, CMP_HIDDEN
    eye = jnp.eye(N_KV_GROUPS, dtype=F32)
    w1 = w_cmp1.reshape(2, CMP_BLOCK, dh, hid)
    w1bd = jnp.einsum("jpdn,gk->jpgdkn", w1, eye).reshape(2, CMP_BLOCK, N_KV_GROUPS * dh, N_KV_GROUPS * hid)
    w2bd = jnp.einsum("jnd,gk->jgnkd", w_cmp2, eye).reshape(2, N_KV_GROUPS * hid, N_KV_GROUPS * dh)
    pe2 = jnp.tile(pe_cmp, (1, 1, N_KV_GROUPS))
    return pe2, w1bd[:, :half].astype(BF16), w1bd[:, half:].astype(BF16), w2bd.astype(BF16)


def kernel(x, g_norm_mix, w_in, b_in, b_fgate, conv_qk, g_mlstm_head, pe_cmp, w_cmp1, w_cmp2,
           rel_bias, w_branch, w_out, g_norm_ffn, w_gate, w_up, w_down, g_final):
    B, T, D = x.shape
    assert D == D_MODEL and w_in.shape[0] == 1, "one residual block (DEPTH == 1)"
    N = B * T
    x2 = x.reshape(N, D)

    idx, scale = _inproj_perm()
    w_r = _gather_cols(w_in[0] * g_norm_mix[0][:, None], idx, scale, BF16)
    b_r = _gather_cols(b_in[0].reshape(1, -1), idx, scale, F32)
    post = np.concatenate([np.ones(M_WIDTH, np.float32), np.full(M_WIDTH, M_HEAD_DIM ** -0.5, np.float32)])
    cw = jnp.zeros((8, 2 * M_WIDTH), F32).at[:CONV_WIDTH].set(conv_qk[0]).at[CONV_WIDTH].set(jnp.asarray(post))
    mqk, mv, mo, nq, cin, nsw, mg, small = _inproj(x2, w_r, b_r)

    bf = jnp.zeros((1, LANES), F32).at[0, :M_HEADS].set(b_fgate[0])
    ltri = jnp.asarray(np.tril(np.ones((M_CHUNK, M_CHUNK), np.float32)))
    r3 = lambda a: a.reshape(B, T, a.shape[-1])
    ym = _mlstm(r3(mqk), r3(mv), r3(mo), r3(small), cw, bf, g_mlstm_head[0].reshape(1, M_WIDTH), ltri)

    pe2, w1t, w1b, w2 = _compress_weights(pe_cmp[0], w_cmp1[0], w_cmp2[0])
    ckv = _compress(cin.reshape(2, B, T, LANES), pe2, w1t, w1b, w2)
    tbl = rel_bias.astype(F32).T.reshape(-1)
    nqt = T // TQ
    bt = _bias_tiles(tbl, nqt + 1, "tok")
    wb = _bias_tiles(tbl, min(WINDOW // TQ + 1, nqt) + 1, "win")
    cb = _bias_tiles(tbl, nqt, "cmp", n_cmp=(T - CMP_BLOCK) // CMP_STRIDE + 1)
    ov, emat = _nsa_constants(T)
    yn = _nsa(r3(nq), r3(nsw), ckv, r3(small), bt, wb, cb, ov, emat)

    wbm = w_branch[0, 0].astype(BF16)
    wbn = jnp.concatenate(
        [w_branch[0, 1, (g * N_REP + r) * N_HEAD_DIM:(g * N_REP + r + 1) * N_HEAD_DIM]
         for r in range(N_REP) for g in range(N_KV_GROUPS)], axis=0).astype(BF16)
    out = _merge(x2, ym.reshape(N, M_WIDTH), yn.reshape(N, N_WIDTH), mg, wbm, wbn,
                 w_out[0].astype(BF16), (w_gate[0] * g_norm_ffn[0][:, None]).astype(BF16),
                 (w_up[0] * g_norm_ffn[0][:, None]).astype(BF16), w_down[0].astype(BF16),
                 g_final.reshape(1, D))
    return out.reshape(B, T, D)
```

```python
import functools

import numpy as np
import jax
import jax.numpy as jnp
from jax import lax
from jax.experimental import pallas as pl
from jax.experimental.pallas import tpu as pltpu

F32 = jnp.float32
BF16 = jnp.bfloat16
HI = lax.Precision.HIGHEST

D_MODEL = 1024
M_HEADS = 4
M_HEAD_DIM = 128
M_WIDTH = M_HEADS * M_HEAD_DIM
M_CHUNK = 128
CONV_WIDTH = 4
N_HEADS = 8
N_KV_GROUPS = 2
N_REP = N_HEADS // N_KV_GROUPS
N_HEAD_DIM = 64
N_WIDTH = N_HEADS * N_HEAD_DIM
N_KV_WIDTH = N_KV_GROUPS * N_HEAD_DIM
CMP_BLOCK = 32
CMP_STRIDE = 16
CMP_HIDDEN = 2 * N_HEAD_DIM
SEL_BLOCK = 64
SEL_TOPK = 16
WINDOW = 512
REL_BUCKETS = 32
REL_MAX_DIST = 1024
N_BRANCH = 2
D_FF = 2816
RMS_EPS = 1e-6
BIG = 1e9
NEG = -1e30

LANES = 128
TQ = 128
NSA_BATCH = 1
VMEM_LIMIT = 56 * 1024 * 1024

_OFF_MQ, _OFF_MK, _OFF_MV, _OFF_MO = 0, 512, 1024, 1536
_OFF_MI, _OFF_MF, _OFF_NQ, _OFF_NKV = 2048, 2052, 2056, 2568
_OFF_NGATE, _OFF_MERGE, _D_IN = 3336, 3360, 5408
_SMALL_NGATE = 8

_SEGS = (("mqk", 1024), ("mv", 512), ("mo", 512), ("nq", 512), ("kc", 128),
         ("vc", 128), ("nsw", 512), ("mg", 2048), ("small", 128))
_D_IN_PAD = sum(w for _, w in _SEGS)
LOG2E = 1.4426950408889634


def _dot(a, b, **kw):
    return jnp.dot(a, b, preferred_element_type=F32, **kw)


def _dot_nt(a, b):
    return lax.dot_general(a, b, (((1,), (1,)), ((), ())), preferred_element_type=F32)


def _dot_tn(a, b):
    return lax.dot_general(a, b, (((0,), (0,)), ((), ())), preferred_element_type=F32)


def _const_spec(shape):
    nd = len(shape)
    return pl.BlockSpec(shape, lambda *_: (0,) * nd, pipeline_mode=pl.Buffered(1))


def _bucket_thresholds():
    max_exact = REL_BUCKETS // 2
    assert REL_MAX_DIST == 64 * max_exact and REL_BUCKETS - max_exact == 16
    thr = []
    for k in range(1, REL_BUCKETS - max_exact):
        t = max_exact
        while t ** 8 < (max_exact ** 8) * (2 ** (3 * k)):
            t += 1
        thr.append(t)
    return tuple(thr)


_BUCKET_THR = _bucket_thresholds()


def _inproj_perm():
    idx = np.zeros((_D_IN_PAD,), np.int32)
    scale = np.zeros((_D_IN_PAD,), np.float32)
    pos = 0

    def put(cols, s=1.0):
        nonlocal pos
        n = len(cols)
        idx[pos:pos + n] = cols
        scale[pos:pos + n] = s
        pos += n

    put(np.arange(_OFF_MQ, _OFF_MV))
    put(np.arange(_OFF_MV, _OFF_MO))
    put(np.arange(_OFF_MO, _OFF_MI))
    nq = np.zeros((N_WIDTH,), np.int32)
    for r in range(N_REP):
        for g in range(N_KV_GROUPS):
            for d in range(N_HEAD_DIM):
                nq[r * 128 + g * 64 + d] = _OFF_NQ + (g * N_REP + r) * N_HEAD_DIM + d
    put(nq, N_HEAD_DIM ** -0.5 * LOG2E)
    kv = lambda j, g: np.arange(_OFF_NKV + (j * N_KV_GROUPS + g) * N_HEAD_DIM,
                                _OFF_NKV + (j * N_KV_GROUPS + g + 1) * N_HEAD_DIM)
    put(np.arange(_OFF_NKV, _OFF_NKV + 256))
    put(np.concatenate([kv(2, 0), kv(2, 1)]))
    put(np.concatenate([kv(4, 0), kv(4, 1)]))
    put(np.concatenate([kv(3, 0), kv(3, 1)]))
    put(np.concatenate([kv(5, 0), kv(5, 1)]))
    put(np.arange(_OFF_MERGE, _D_IN))
    put(np.arange(_OFF_MI, _OFF_MI + 8))
    put(np.arange(_OFF_NGATE, _OFF_NGATE + 24))
    pos += LANES - 32
    assert pos == _D_IN_PAD
    return idx, scale


def _gather_cols(a, idx, scale, dtype):
    pieces = []
    start = 0
    n = len(idx)
    for c in range(1, n + 1):
        same = c < n and scale[c] == scale[start] and (scale[c] == 0.0 or idx[c] == idx[c - 1] + 1)
        if not same:
            if scale[start] == 0.0:
                piece = jnp.zeros((a.shape[0], c - start), dtype)
            else:
                piece = a[:, int(idx[start]):int(idx[start]) + (c - start)]
                if scale[start] != 1.0:
                    piece = piece * float(scale[start])
            pieces.append(piece.astype(dtype))
            start = c
    return jnp.concatenate(pieces, axis=1)


def _inproj_kernel(x_ref, w_ref, b_ref, mqk_ref, mv_ref, mo_ref, nq_ref,
                   cin_ref, nsw_ref, mg_ref, small_ref):
    x = x_ref[...]
    rs = lax.rsqrt(jnp.mean(x * x, axis=-1, keepdims=True) + RMS_EPS)
    xb = x.astype(BF16)

    def seg(a, n):
        return _dot(xb, w_ref[:, a:a + n]) * rs + b_ref[:, a:a + n]

    off = {}
    pos = 0
    for name, w in _SEGS:
        off[name] = pos
        pos += w

    mqk_ref[...] = seg(off["mqk"], 1024).astype(BF16)
    mv_ref[...] = seg(off["mv"], 512).astype(BF16)
    mo_ref[...] = seg(off["mo"], 512).astype(BF16)
    nq_ref[...] = seg(off["nq"], 512).astype(BF16)
    cc = seg(off["kc"], 2 * LANES)
    cin_ref[0] = cc[:, :LANES]
    cin_ref[1] = cc[:, LANES:]
    nsw_ref[:, 0:2 * LANES] = seg(off["nsw"], 2 * LANES).astype(BF16)
    vv = seg(off["nsw"] + 2 * LANES, 2 * LANES)
    lo = lax.broadcasted_iota(jnp.int32, (x.shape[0], LANES), 1) < N_HEAD_DIM
    for j in range(2):
        vj = vv[:, j * LANES:(j + 1) * LANES]
        nsw_ref[:, (2 + 2 * j) * LANES:(3 + 2 * j) * LANES] = jnp.where(lo, vj, 1.0).astype(BF16)
        nsw_ref[:, (3 + 2 * j) * LANES:(4 + 2 * j) * LANES] = jnp.where(lo, 1.0, vj).astype(BF16)
    mg_ref[...] = seg(off["mg"], 2048).astype(BF16)
    small_ref[...] = seg(off["small"], 128)


def _inproj(x2, w, b, tm=512):
    n = x2.shape[0]
    assert n % tm == 0
    row = lambda width: pl.BlockSpec((tm, width), lambda i: (i, 0))
    out_shape = (
        jax.ShapeDtypeStruct((n, 1024), BF16), jax.ShapeDtypeStruct((n, 512), BF16),
        jax.ShapeDtypeStruct((n, 512), BF16), jax.ShapeDtypeStruct((n, 512), BF16),
        jax.ShapeDtypeStruct((2, n, 128), F32), jax.ShapeDtypeStruct((n, 768), BF16),
        jax.ShapeDtypeStruct((n, 2048), BF16), jax.ShapeDtypeStruct((n, 128), F32))
    out_specs = (row(1024), row(512), row(512), row(512),
                 pl.BlockSpec((2, tm, 128), lambda i: (0, i, 0)), row(768), row(2048), row(128))
    return pl.pallas_call(
        _inproj_kernel, out_shape=out_shape, grid=(n // tm,),
        in_specs=[row(D_MODEL), _const_spec((D_MODEL, _D_IN_PAD)), _const_spec((1, _D_IN_PAD))],
        out_specs=out_specs, name="inproj",
        compiler_params=pltpu.CompilerParams(dimension_semantics=("arbitrary",),
                                             vmem_limit_bytes=VMEM_LIMIT),
    )(x2, w, b)


def _mlstm_kernel(mqk_ref, mv_ref, mo_ref, small_ref, cw_ref, bf_ref, gh_ref, ltri_ref,
                  ym_ref, ebuf, c_st, m_st, g_bc, g_dt, g_a, g_mrow, g_row):
    L = M_CHUNK
    c = pl.program_id(1)
    n_chunks = g_bc.shape[0]
    rows = lax.broadcasted_iota(jnp.int32, (L, LANES), 0)

    @pl.when(c == 0)
    def _():
        ebuf[0:8, :] = jnp.zeros((8, 2 * M_WIDTH), F32)
        c_st[...] = jnp.zeros_like(c_st)
        m_st[...] = jnp.zeros_like(m_st)
        for cc in range(n_chunks):
            ig = small_ref[0, cc * L:(cc + 1) * L, :]
            fg = pltpu.roll(ig, LANES - M_HEADS, 1) + bf_ref[...]
            lf = jnp.minimum(fg, 0.0) - jnp.log1p(jnp.exp(-jnp.abs(fg)))
            bc = jnp.dot(ltri_ref[...], lf, precision=HI, preferred_element_type=F32)
            blast = bc[L - 1:L, :]
            a_all = blast - bc + ig
            d = ig - bc
            cm = d
            sh = 1
            while sh < L:
                cm = jnp.maximum(cm, jnp.where(rows >= sh, pltpu.roll(cm, sh, 0), -jnp.inf))
                sh *= 2
            g_bc[cc] = bc
            g_dt[cc] = d.T
            g_a[cc] = a_all
            g_mrow[cc] = bc + cm
            g_row[cc, 0:1, :] = blast
            g_row[cc, 1:2, :] = jnp.max(a_all, axis=0, keepdims=True)

    ebuf[8:8 + L, :] = mqk_ref[0].astype(F32)
    ext = ebuf[...]
    conv = cw_ref[CONV_WIDTH - 1:CONV_WIDTH, :] * ext[8:, :]
    for k in range(1, CONV_WIDTH):
        conv = conv + cw_ref[CONV_WIDTH - 1 - k:CONV_WIDTH - k, :] * pltpu.roll(ext, k, 0)[8:, :]
    ebuf[0:8, :] = ebuf[L:L + 8, :]
    qk = (conv * jax.nn.sigmoid(conv) * cw_ref[CONV_WIDTH:CONV_WIDTH + 1, :]).astype(BF16)

    bc = g_bc[c]
    dt = g_dt[c]
    a_all = g_a[c]
    blast = g_row[c, 0:1, :]
    m_prev = m_st[0:1, :]
    m_new = jnp.maximum(blast + m_prev, g_row[c, 1:2, :])
    decay = jnp.exp(blast + m_prev - m_new)
    w_all = jnp.exp(a_all - m_new)
    log_inter = bc + m_prev
    m_i_all = jnp.maximum(g_mrow[c], log_inter)
    s_inter_all = jnp.exp(log_inter - m_i_all)
    emi_all = jnp.exp(-m_i_all)
    u_all = bc - m_i_all

    row = lax.broadcasted_iota(jnp.int32, (L, L), 0)
    col = lax.broadcasted_iota(jnp.int32, (L, L), 1)
    causal = row >= col

    ones = jnp.ones((L, M_HEAD_DIM), BF16)
    for h in range(M_HEADS):
        sl = slice(h * M_HEAD_DIM, (h + 1) * M_HEAD_DIM)
        qb = qk[:, sl]
        kb = qk[:, M_WIDTH + h * M_HEAD_DIM:M_WIDTH + (h + 1) * M_HEAD_DIM]
        v = mv_ref[0, :, sl]
        cn_prev = c_st[h]

        logp = jnp.where(causal, u_all[:, h:h + 1] + dt[h:h + 1, :], -jnp.inf)
        p = jnp.exp(logp) * _dot_nt(qb, kb)
        nd = (_dot(p.astype(BF16), jnp.concatenate([v, ones], axis=1))
              + s_inter_all[:, h:h + 1] * _dot(qb, cn_prev.astype(BF16)))
        hh = nd[:, :M_HEAD_DIM] / jnp.maximum(jnp.abs(nd[:, M_HEAD_DIM:]), emi_all[:, h:h + 1])
        hh = hh * lax.rsqrt(jnp.mean(hh * hh, axis=-1, keepdims=True) + RMS_EPS) * gh_ref[:, sl]
        ym_ref[0, :, sl] = (jax.nn.sigmoid(mo_ref[0, :, sl].astype(F32)) * hh).astype(ym_ref.dtype)

        w_col = w_all[:, h:h + 1]
        dec = decay[:, h:h + 1]
        vw = jnp.concatenate([v.astype(F32) * w_col, jnp.broadcast_to(w_col, (L, M_HEAD_DIM))], axis=1)
        c_st[h] = dec * cn_prev + _dot_tn(kb, vw.astype(BF16))

    m_st[0:1, :] = m_new


def _mlstm(mqk, mv, mo, small, cw, bf, gh, ltri):
    B, T, _ = mqk.shape
    L = M_CHUNK
    assert T % L == 0
    nc = T // L
    blk = lambda w: pl.BlockSpec((1, L, w), lambda b, c: (b, c, 0))
    gate = lambda: pltpu.VMEM((nc, L, LANES), F32)
    return pl.pallas_call(
        _mlstm_kernel, out_shape=jax.ShapeDtypeStruct((B, T, M_WIDTH), BF16),
        grid=(B, nc),
        in_specs=[blk(2 * M_WIDTH), blk(M_WIDTH), blk(M_WIDTH),
                  pl.BlockSpec((1, T, LANES), lambda b, c: (b, 0, 0)),
                  _const_spec((8, 2 * M_WIDTH)), _const_spec((1, LANES)),
                  _const_spec((1, M_WIDTH)), _const_spec((L, L))],
        out_specs=blk(M_WIDTH),
        scratch_shapes=[pltpu.VMEM((L + 8, 2 * M_WIDTH), F32),
                        pltpu.VMEM((M_HEADS, M_HEAD_DIM, 2 * M_HEAD_DIM), F32),
                        pltpu.VMEM((8, LANES), F32),
                        gate(), gate(), gate(), gate(), pltpu.VMEM((nc, 8, LANES), F32)],
        name="mlstm",
        compiler_params=pltpu.CompilerParams(dimension_semantics=("arbitrary", "arbitrary")),
    )(mqk, mv, mo, small, cw, bf, gh, ltri)


def _compress_kernel(x_ref, pe_ref, w1t_ref, w1b_ref, w2_ref, o_ref, *, nch):
    half = CMP_BLOCK // 2
    top = jnp.zeros((nch, 2 * CMP_HIDDEN), F32)
    bot = jnp.zeros((nch, 2 * CMP_HIDDEN), F32)
    for p in range(half):
        xp = x_ref[0, 0, pl.ds(p, nch, stride=CMP_STRIDE), :]
        top = top + _dot((xp + pe_ref[0, p:p + 1, :]).astype(BF16), w1t_ref[0, p])
        bot = bot + _dot((xp + pe_ref[0, half + p:half + p + 1, :]).astype(BF16), w1b_ref[0, p])
    hid = top + pltpu.roll(bot, nch - 1, 0)
    act = hid * jax.nn.sigmoid(hid)
    o_ref[0, 0] = _dot(act.astype(BF16), w2_ref[0]).astype(o_ref.dtype)


def _compress(cin, pe2, w1t, w1b, w2):
    _, B, T, _ = cin.shape
    nch = T // CMP_STRIDE
    sel = lambda *shape: pl.BlockSpec((1,) + shape, lambda j, b: (j,) + (0,) * len(shape))
    return pl.pallas_call(
        functools.partial(_compress_kernel, nch=nch),
        out_shape=jax.ShapeDtypeStruct((2, B, nch, LANES), BF16),
        grid=(2, B),
        in_specs=[pl.BlockSpec((1, 1, T, LANES), lambda j, b: (j, b, 0, 0)),
                  sel(CMP_BLOCK, LANES), sel(CMP_BLOCK // 2, LANES, 2 * CMP_HIDDEN),
                  sel(CMP_BLOCK // 2, LANES, 2 * CMP_HIDDEN), sel(2 * CMP_HIDDEN, LANES)],
        out_specs=pl.BlockSpec((1, 1, nch, LANES), lambda j, b: (j, b, 0, 0)),
        name="compress",
        compiler_params=pltpu.CompilerParams(dimension_semantics=("arbitrary", "arbitrary")),
    )(cin, pe2, w1t, w1b, w2)


def _bias_kernel(tbl_ref, o_ref, *, kind, n_cmp):
    pid = pl.program_id(0)
    toeplitz = kind != "cmp"
    if toeplitz:
        k = lax.broadcasted_iota(jnp.int32, (8, 2 * LANES), 1)
        dist = (pid - 1) * TQ + jnp.where(k < LANES, -k, 2 * LANES - k)
    else:
        a = lax.broadcasted_iota(jnp.int32, (TQ, LANES), 0)
        b = lax.broadcasted_iota(jnp.int32, (TQ, LANES), 1)
        dist = pid * TQ + a - (b * CMP_STRIDE + CMP_BLOCK - 1)
    n = jnp.maximum(dist, 0)
    cnt = jnp.zeros_like(n)
    for t in _BUCKET_THR:
        cnt = cnt + jnp.where(n >= t, 1, 0)
    bucket = jnp.where(n < REL_BUCKETS // 2, n, REL_BUCKETS // 2 + cnt)
    if kind == "tok":
        madd = jnp.where((dist >= 0) & (pid > 0), 0.0, NEG)
    elif kind == "win":
        madd = jnp.where((dist >= 0) & (dist < WINDOW) & (pid > 0), 0.0, NEG)
    else:
        madd = jnp.where((dist >= 0) & (b < n_cmp), 0.0, NEG)
    for h in range(N_HEADS):
        val = jnp.zeros(dist.shape, F32)
        for bb in range(REL_BUCKETS):
            val = jnp.where(bucket == bb, tbl_ref[h * REL_BUCKETS + bb], val)
        val = val * LOG2E + madd
        if toeplitz:
            g = jnp.broadcast_to(val[0:1, :], (TQ, 2 * LANES))
            val = pltpu.roll(g, 0, 1, stride=1, stride_axis=0)[:, :LANES]
        o_ref[0, h * TQ:(h + 1) * TQ, :] = val


def _bias_tiles(tbl, n_tiles, kind, n_cmp=0):
    return pl.pallas_call(
        functools.partial(_bias_kernel, kind=kind, n_cmp=n_cmp),
        out_shape=jax.ShapeDtypeStruct((n_tiles, N_HEADS * TQ, LANES), F32),
        grid=(n_tiles,),
        in_specs=[pl.BlockSpec(memory_space=pltpu.SMEM)],
        out_specs=pl.BlockSpec((1, N_HEADS * TQ, LANES), lambda i: (i, 0, 0)),
        name="bias_" + kind,
        compiler_params=pltpu.CompilerParams(dimension_semantics=("arbitrary",)),
    )(tbl)


def _nsa_kernel(nq_ref, nq4_ref, ks_ref, kw_ref, vs0_ref, vs1_ref, vw0_ref, vw1_ref, kc_ref, vc_ref,
                small_ref, small4_ref, bt_ref, wb_ref, cb4_ref, ovt_ref, et_ref, yn_ref,
                lhs_sc, z_sc, seln_sc, partc_sc, *, nb, n_slc, n_top, n_win, n_chunks_max):
    qi = pl.program_id(1)
    HR = N_HEADS * TQ
    GR = N_REP * TQ
    CH = 4 * TQ
    G4 = 4
    vs_refs = (vs0_ref, vs1_ref)
    vw_refs = (vw0_ref, vw1_ref)
    lane = lax.broadcasted_iota(jnp.int32, (TQ, LANES), 1)
    lo = lane < N_HEAD_DIM

    def stack_heads(q_all):
        zero = jnp.zeros((TQ, LANES), BF16)
        parts = []
        for g in range(N_KV_GROUPS):
            for r in range(N_REP):
                qr = q_all[:, r * LANES:(r + 1) * LANES]
                parts.append(jnp.where(lo if g == 0 else jnp.logical_not(lo), qr, zero))
        return parts

    def pair(o, r):
        return jnp.where(lo, o[r * TQ:(r + 1) * TQ], o[(N_REP + r) * TQ:(N_REP + r + 1) * TQ])

    def normed(acc, r):
        den = jnp.where(lo, acc[(N_REP + r) * TQ:(N_REP + r + 1) * TQ], acc[r * TQ:(r + 1) * TQ])
        return pair(acc, r) / pltpu.roll(den, N_HEAD_DIM, 1)

    def gate(sg, r, ci):
        c0 = _SMALL_NGATE + r * 3 + ci
        c1 = _SMALL_NGATE + (N_REP + r) * 3 + ci
        return jnp.where(lo, sg[:, c0:c0 + 1], sg[:, c1:c1 + 1])

    w0 = jnp.maximum(qi - (n_win - 1), 0)
    woff = pl.multiple_of(w0 * TQ, TQ)

    def group_prep(s, n_chunks):
        qs4 = jnp.concatenate(
            [p for j in range(G4) for p in stack_heads(nq4_ref[s, j * TQ:(j + 1) * TQ, :])], axis=0)
        rows = lax.broadcasted_iota(jnp.int32, (G4 * HR, 1), 0)
        t_rows = (qi + jnp.right_shift(rows, 10)) * TQ + (rows & (TQ - 1))

        z = _dot_nt(qs4, kc_ref[0, s]) + cb4_ref[...].reshape(G4 * HR, LANES)
        e = jnp.exp2(z - jnp.max(z, axis=-1, keepdims=True))
        l = _dot(e.astype(BF16), jnp.ones((LANES, LANES), BF16))
        p_c = e * jnp.where(t_rows >= CMP_BLOCK - 1, 1.0 / jnp.maximum(l, 1e-30), 0.0)
        o_c = _dot(p_c.astype(BF16), vc_ref[0, s])
        sg4 = jax.nn.sigmoid(small4_ref[s])
        for j in range(G4):
            for r in range(N_REP):
                partc_sc[s, j, r] = (gate(sg4[j * TQ:(j + 1) * TQ], r, 0)
                                     * pair(o_c[j * HR:(j + 1) * HR], r))

        W = G4 * N_KV_GROUPS * TQ
        jb = lax.broadcasted_iota(jnp.int32, (n_slc, W), 0)
        col = lax.broadcasted_iota(jnp.int32, (n_slc, W), 1)
        tq = (qi + jnp.right_shift(col, 8)) * TQ + (col & (TQ - 1))
        cur = jnp.right_shift(tq, 6)
        elig = jb <= cur
        if n_chunks * CH <= n_top * SEL_BLOCK:
            sel = jnp.where(elig, 1.0, 0.0)
        else:
            forced = (jb == 0) | (jb == cur) | (jb == cur - 1)
            psums = []
            for j in range(G4):
                for g in range(N_KV_GROUPS):
                    base = j * HR + g * GR
                    ps = p_c[base:base + TQ]
                    for r in range(1, N_REP):
                        ps = ps + p_c[base + r * TQ:base + (r + 1) * TQ]
                    psums.append(ps)
            imp = lax.dot_general(ovt_ref[...], jnp.concatenate(psums, axis=0), (((1,), (1,)), ((), ())),
                                  precision=HI, preferred_element_type=F32)
            score = jnp.where(elig, jnp.where(forced, BIG, imp), -BIG)
            cnt = jnp.zeros((n_slc, W), F32)
            for i in range(n_slc):
                si = score[i:i + 1, :]
                tie = jnp.where(jb > i, 1.0, 0.0)
                cnt = cnt + jnp.where(si > score, 1.0, jnp.where(si == score, tie, 0.0))
            sel = jnp.where((cnt < n_top) & (score > -BIG / 2), 1.0, 0.0)
        seln = jnp.concatenate([sel - 1.0, jnp.zeros((LANES - n_slc, W), F32)], axis=0).T.astype(BF16)
        for j in range(G4):
            for g in range(N_KV_GROUPS):
                seln_sc[s, j, g] = seln[(j * N_KV_GROUPS + g) * TQ:(j * N_KV_GROUPS + g + 1) * TQ]

    def one_tile(s, n_chunks):
        jq = qi & (G4 - 1)
        parts = stack_heads(nq_ref[s])
        qs = jnp.concatenate(parts, axis=0)

        zw = []
        zmax = None
        for j0 in range(0, n_win, 2):
            nj = min(2, n_win - j0)
            sw = _dot_nt(qs, kw_ref[s, pl.ds(woff + j0 * TQ, nj * TQ), :])
            for u in range(nj):
                zj = sw[:, u * TQ:(u + 1) * TQ] + wb_ref[jnp.maximum(qi - w0 - (j0 + u) + 1, 0)]
                zw.append(zj)
                zmax = zj if zmax is None else jnp.maximum(zmax, zj)
        mw = jnp.max(zmax, axis=-1, keepdims=True)
        pw = jnp.concatenate([jnp.exp2(zj - mw).astype(BF16) for zj in zw], axis=1)
        acc_w = jnp.concatenate(
            [_dot(pw[g * GR:(g + 1) * GR], vw_refs[g][s, pl.ds(woff, n_win * TQ), :])
             for g in range(N_KV_GROUPS)], axis=0)

        for g in range(N_KV_GROUPS):
            seln = seln_sc[s, jq, g]
            for r in range(N_REP):
                h = g * N_REP + r
                lhs_sc[s, h * TQ:(h + 1) * TQ, :] = jnp.concatenate([parts[h], seln], axis=1)

        zmax = None
        for c in range(n_chunks):
            for half in range(2):
                k0 = c * CH + half * 2 * TQ
                rhs = jnp.concatenate([ks_ref[s, k0:k0 + 2 * TQ, :], et_ref[k0:k0 + 2 * TQ, :]], axis=1)
                sz = _dot_nt(lhs_sc[s], rhs)
                for u in range(2):
                    kt = 4 * c + 2 * half + u
                    zt = sz[:, u * TQ:(u + 1) * TQ] + bt_ref[jnp.maximum(qi - kt + 1, 0)]
                    z_sc[s, c, :, (2 * half + u) * TQ:(2 * half + u + 1) * TQ] = zt
                    zmax = zt if zmax is None else jnp.maximum(zmax, zt)
        ms = jnp.max(zmax, axis=-1, keepdims=True)
        acc = [None] * N_KV_GROUPS
        for c in range(n_chunks):
            p = jnp.exp2(z_sc[s, c] - ms).astype(BF16)
            for g in range(N_KV_GROUPS):
                d = _dot(p[g * GR:(g + 1) * GR], vs_refs[g][s, c * CH:(c + 1) * CH, :])
                acc[g] = d if acc[g] is None else acc[g] + d
        acc_s = jnp.concatenate(acc, axis=0)

        sg = jax.nn.sigmoid(small_ref[s])
        for r in range(N_REP):
            out = (partc_sc[s, jq, r] + gate(sg, r, 1) * normed(acc_s, r)
                   + gate(sg, r, 2) * normed(acc_w, r))
            yn_ref[s, :, r * LANES:(r + 1) * LANES] = out.astype(yn_ref.dtype)

    for nc in range(1, n_chunks_max + 1):
        @pl.when(qi // G4 == nc - 1)
        def _(nc=nc):
            @pl.when(qi % G4 == 0)
            def _():
                for s in range(nb):
                    group_prep(s, nc)

            for s in range(nb):
                one_tile(s, nc)


def _nsa(nq, nsw, ckv, small, bt, wb, cb, ov, emat):
    B, T, _ = nq.shape
    assert T == 2048, "single 128-wide compressed-key tile assumes T == 2048"
    nqt = T // TQ
    n_cmp = (T - CMP_BLOCK) // CMP_STRIDE + 1
    n_slc = T // SEL_BLOCK
    n_top = min(SEL_TOPK, n_slc)
    n_win = wb.shape[0] - 1
    HR = N_HEADS * TQ
    assert nqt % 4 == 0 and n_slc % 8 == 0 and n_slc <= LANES
    nb = NSA_BATCH if B % NSA_BATCH == 0 else 1
    n_chunks_max = nqt // 4
    kv = lambda j: pl.BlockSpec((nb, T, LANES), lambda b, q: (b, 0, j), pipeline_mode=pl.Buffered(1))
    ck = lambda j: pl.BlockSpec((1, nb, T // CMP_STRIDE, LANES), lambda b, q: (j, b, 0, 0))
    kern = functools.partial(_nsa_kernel, nb=nb, n_slc=n_slc, n_top=n_top, n_win=n_win,
                             n_chunks_max=n_chunks_max)
    return pl.pallas_call(
        kern, out_shape=jax.ShapeDtypeStruct((B, T, N_WIDTH), BF16),
        grid=(B // nb, nqt),
        in_specs=[pl.BlockSpec((nb, TQ, N_WIDTH), lambda b, q: (b, q, 0)),
                  pl.BlockSpec((nb, 4 * TQ, N_WIDTH), lambda b, q: (b, q // 4, 0)),
                  kv(0), kv(1), kv(2), kv(3), kv(4), kv(5), ck(0), ck(1),
                  pl.BlockSpec((nb, TQ, LANES), lambda b, q: (b, q, 0)),
                  pl.BlockSpec((nb, 4 * TQ, LANES), lambda b, q: (b, q // 4, 0)),
                  _const_spec(bt.shape), _const_spec(wb.shape),
                  pl.BlockSpec((4, HR, LANES), lambda b, q: (q // 4, 0, 0)),
                  _const_spec(ov.shape), _const_spec(emat.shape)],
        out_specs=pl.BlockSpec((nb, TQ, N_WIDTH), lambda b, q: (b, q, 0)),
        scratch_shapes=[pltpu.VMEM((nb, HR, 2 * LANES), BF16),
                        pltpu.VMEM((nb, n_chunks_max, HR, 4 * TQ), F32),
                        pltpu.VMEM((nb, 4, N_KV_GROUPS, TQ, LANES), BF16),
                        pltpu.VMEM((nb, 4, N_REP, TQ, LANES), F32)],
        name="nsa",
        compiler_params=pltpu.CompilerParams(dimension_semantics=("arbitrary", "arbitrary"),
                                             vmem_limit_bytes=VMEM_LIMIT),
    )(nq, nq, nsw, nsw, nsw, nsw, nsw, nsw, ckv, ckv, small, small, bt, wb, cb, ov, emat)


def _merge_kernel(x_ref, ym_ref, yn_ref, mg_ref, wbm_ref, wbn_ref, wo_ref,
                  wg_ref, wu_ref, wd_ref, gfin_ref, o_ref, *, tf):
    bm = _dot(ym_ref[...], wbm_ref[...])
    bn = _dot(yn_ref[...], wbn_ref[...])
    mixed = (jax.nn.sigmoid(mg_ref[:, :D_MODEL].astype(F32)) * bm
             + jax.nn.sigmoid(mg_ref[:, D_MODEL:].astype(F32)) * bn)
    h = x_ref[...] + _dot(mixed.astype(BF16), wo_ref[...])
    rs = lax.rsqrt(jnp.mean(h * h, axis=-1, keepdims=True) + RMS_EPS)
    hb = h.astype(BF16)
    acc = jnp.zeros(h.shape, F32)
    for j in range(D_FF // tf):
        gg = _dot(hb, wg_ref[:, j * tf:(j + 1) * tf]) * rs
        uu = _dot(hb, wu_ref[:, j * tf:(j + 1) * tf]) * rs
        act = (gg * jax.nn.sigmoid(gg) * uu).astype(BF16)
        acc = acc + _dot(act, wd_ref[j * tf:(j + 1) * tf, :])
    h2 = h + acc
    o_ref[...] = h2 * lax.rsqrt(jnp.mean(h2 * h2, axis=-1, keepdims=True) + RMS_EPS) * gfin_ref[...]


def _merge(x2, ym, yn, mg, wbm, wbn, wo, wg, wu, wd, gfin, tm=512, tf=256):
    n = x2.shape[0]
    assert n % tm == 0 and D_FF % tf == 0
    row = lambda width: pl.BlockSpec((tm, width), lambda i: (i, 0))
    return pl.pallas_call(
        functools.partial(_merge_kernel, tf=tf),
        out_shape=jax.ShapeDtypeStruct((n, D_MODEL), F32), grid=(n // tm,),
        in_specs=[row(D_MODEL), row(M_WIDTH), row(N_WIDTH), row(N_BRANCH * D_MODEL),
                  _const_spec(wbm.shape), _const_spec(wbn.shape), _const_spec(wo.shape),
                  _const_spec(wg.shape), _const_spec(wu.shape),
                  _const_spec(wd.shape), _const_spec(gfin.shape)],
        out_specs=row(D_MODEL), name="merge_ffn",
        compiler_params=pltpu.CompilerParams(dimension_semantics=("arbitrary",),
                                             vmem_limit_bytes=VMEM_LIMIT),
    )(x2, ym, yn, mg, wbm, wbn, wo, wg, wu, wd, gfin)


def _nsa_constants(T):
    n_cmp = (T - CMP_BLOCK) // CMP_STRIDE + 1
    n_slc = T // SEL_BLOCK
    cs = np.arange(n_cmp) * CMP_STRIDE
    ss = np.arange(n_slc) * SEL_BLOCK
    ov = np.clip(np.minimum(cs[:, None] + CMP_BLOCK, ss[None, :] + SEL_BLOCK)
                 - np.maximum(cs[:, None], ss[None, :]), 0, None) / CMP_STRIDE
    ovt = np.zeros((n_slc, LANES), np.float32)
    ovt[:, :n_cmp] = ov.T
    et = (np.arange(T)[:, None] // SEL_BLOCK == np.arange(LANES)[None, :]).astype(np.float32) * (-NEG)
    return jnp.asarray(ovt), jnp.asarray(et, dtype=BF16)


def _compress_weights(pe_cmp, w_cmp1, w_cmp2):
    half = CMP_BLOCK // 2
    dh, hid = N_HEAD_DIM, CMP_HIDDEN
    eye = jnp.eye(N_KV_GROUPS, dtype=F32)
    w1 = w_cmp1.reshape(2, CMP_BLOCK, dh, hid)
    w1bd = jnp.einsum("jpdn,gk->jpgdkn", w1, eye).reshape(2, CMP_BLOCK, N_KV_GROUPS * dh, N_KV_GROUPS * hid)
    w2bd = jnp.einsum("jnd,gk->jgnkd", w_cmp2, eye).reshape(2, N_KV_GROUPS * hid, N_KV_GROUPS * dh)
    pe2 = jnp.tile(pe_cmp, (1, 1, N_KV_GROUPS))
    return pe2, w1bd[:, :half].astype(BF16), w1bd[:, half:].astype(BF16), w2bd.astype(BF16)


def kernel(x, g_norm_mix, w_in, b_in, b_fgate, conv_qk, g_mlstm_head, pe_cmp, w_cmp1, w_cmp2,
           rel_bias, w_branch, w_out, g_norm_ffn, w_gate, w_up, w_down, g_final):
    B, T, D = x.shape
    assert D == D_MODEL and w_in.shape[0] == 1, "one residual block (DEPTH == 1)"
    N = B * T
    x2 = x.reshape(N, D)

    idx, scale = _inproj_perm()
    w_r = _gather_cols(w_in[0] * g_norm_mix[0][:, None], idx, scale, BF16)
    b_r = _gather_cols(b_in[0].reshape(1, -1), idx, scale, F32)
    post = np.concatenate([np.ones(M_WIDTH, np.float32), np.full(M_WIDTH, M_HEAD_DIM ** -0.5, np.float32)])
    cw = jnp.zeros((8, 2 * M_WIDTH), F32).at[:CONV_WIDTH].set(conv_qk[0]).at[CONV_WIDTH].set(jnp.asarray(post))
    mqk, mv, mo, nq, cin, nsw, mg, small = _inproj(x2, w_r, b_r)

    bf = jnp.zeros((1, LANES), F32).at[0, :M_HEADS].set(b_fgate[0])
    ltri = jnp.asarray(np.tril(np.ones((M_CHUNK, M_CHUNK), np.float32)))
    r3 = lambda a: a.reshape(B, T, a.shape[-1])
    ym = _mlstm(r3(mqk), r3(mv), r3(mo), r3(small), cw, bf, g_mlstm_head[0].reshape(1, M_WIDTH), ltri)

    pe2, w1t, w1b, w2 = _compress_weights(pe_cmp[0], w_cmp1[0], w_cmp2[0])
    ckv = _compress(cin.reshape(2, B, T, LANES), pe2, w1t, w1b, w2)
    tbl = rel_bias.astype(F32).T.reshape(-1)
    nqt = T // TQ
    bt = _bias_tiles(tbl, nqt + 1, "tok")
    wb = _bias_tiles(tbl, min(WINDOW // TQ + 1, nqt) + 1, "win")
    cb = _bias_tiles(tbl, nqt, "cmp", n_cmp=(T - CMP_BLOCK) // CMP_STRIDE + 1)
    ov, emat = _nsa_constants(T)
    yn = _nsa(r3(nq), r3(nsw), ckv, r3(small), bt, wb, cb, ov, emat)

    wbm = w_branch[0, 0].astype(BF16)
    wbn = jnp.concatenate(
        [w_branch[0, 1, (g * N_REP + r) * N_HEAD_DIM:(g * N_REP + r + 1) * N_HEAD_DIM]
         for r in range(N_REP) for g in range(N_KV_GROUPS)], axis=0).astype(BF16)
    out = _merge(x2, ym.reshape(N, M_WIDTH), yn.reshape(N, N_WIDTH), mg, wbm, wbn,
                 w_out[0].astype(BF16), (w_gate[0] * g_norm_ffn[0][:, None]).astype(BF16),
                 (w_up[0] * g_norm_ffn[0][:, None]).astype(BF16), w_down[0].astype(BF16),
                 g_final.reshape(1, D))
    return out.reshape(B, T, D)
```

```python
import functools

import numpy as np
import jax
import jax.numpy as jnp
from jax import lax
from jax.experimental import pallas as pl
from jax.experimental.pallas import tpu as pltpu

F32 = jnp.float32
BF16 = jnp.bfloat16
HI = lax.Precision.HIGHEST

D_MODEL = 1024
M_HEADS = 4
M_HEAD_DIM = 128
M_WIDTH = M_HEADS * M_HEAD_DIM
M_CHUNK = 128
CONV_WIDTH = 4
N_HEADS = 8
N_KV_GROUPS = 2
N_REP = N_HEADS // N_KV_GROUPS
N_HEAD_DIM = 64
N_WIDTH = N_HEADS * N_HEAD_DIM
N_KV_WIDTH = N_KV_GROUPS * N_HEAD_DIM
CMP_BLOCK = 32
CMP_STRIDE = 16
CMP_HIDDEN = 2 * N_HEAD_DIM
SEL_BLOCK = 64
SEL_TOPK = 16
WINDOW = 512
REL_BUCKETS = 32
REL_MAX_DIST = 1024
N_BRANCH = 2
D_FF = 2816
RMS_EPS = 1e-6
BIG = 1e9
NEG = -1e30

LANES = 128
TQ = 128
NSA_BATCH = 1
VMEM_LIMIT = 56 * 1024 * 1024

_OFF_MQ, _OFF_MK, _OFF_MV, _OFF_MO = 0, 512, 1024, 1536
_OFF_MI, _OFF_MF, _OFF_NQ, _OFF_NKV = 2048, 2052, 2056, 2568
_OFF_NGATE, _OFF_MERGE, _D_IN = 3336, 3360, 5408
_SMALL_NGATE = 8

_SEGS = (("mqk", 1024), ("mv", 512), ("mo", 512), ("nq", 512), ("kc", 128),
         ("vc", 128), ("nsw", 512), ("mg", 2048), ("small", 128))
_D_IN_PAD = sum(w for _, w in _SEGS)
LOG2E = 1.4426950408889634


def _dot(a, b, **kw):
    return jnp.dot(a, b, preferred_element_type=F32, **kw)


def _dot_nt(a, b):
    return lax.dot_general(a, b, (((1,), (1,)), ((), ())), preferred_element_type=F32)


def _dot_tn(a, b):
    return lax.dot_general(a, b, (((0,), (0,)), ((), ())), preferred_element_type=F32)


def _const_spec(shape):
    nd = len(shape)
    return pl.BlockSpec(shape, lambda *_: (0,) * nd, pipeline_mode=pl.Buffered(1))


def _bucket_thresholds():
    max_exact = REL_BUCKETS // 2
    assert REL_MAX_DIST == 64 * max_exact and REL_BUCKETS - max_exact == 16
    thr = []
    for k in range(1, REL_BUCKETS - max_exact):
        t = max_exact
        while t ** 8 < (max_exact ** 8) * (2 ** (3 * k)):
            t += 1
        thr.append(t)
    return tuple(thr)


_BUCKET_THR = _bucket_thresholds()


def _inproj_perm():
    idx = np.zeros((_D_IN_PAD,), np.int32)
    scale = np.zeros((_D_IN_PAD,), np.float32)
    pos = 0

    def put(cols, s=1.0):
        nonlocal pos
        n = len(cols)
        idx[pos:pos + n] = cols
        scale[pos:pos + n] = s
        pos += n

    put(np.arange(_OFF_MQ, _OFF_MV))
    put(np.arange(_OFF_MV, _OFF_MO))
    put(np.arange(_OFF_MO, _OFF_MI))
    nq = np.zeros((N_WIDTH,), np.int32)
    for r in range(N_REP):
        for g in range(N_KV_GROUPS):
            for d in range(N_HEAD_DIM):
                nq[r * 128 + g * 64 + d] = _OFF_NQ + (g * N_REP + r) * N_HEAD_DIM + d
    put(nq, N_HEAD_DIM ** -0.5 * LOG2E)
    kv = lambda j, g: np.arange(_OFF_NKV + (j * N_KV_GROUPS + g) * N_HEAD_DIM,
                                _OFF_NKV + (j * N_KV_GROUPS + g + 1) * N_HEAD_DIM)
    put(np.arange(_OFF_NKV, _OFF_NKV + 256))
    put(np.concatenate([kv(2, 0), kv(2, 1)]))
    put(np.concatenate([kv(4, 0), kv(4, 1)]))
    put(np.concatenate([kv(3, 0), kv(3, 1)]))
    put(np.concatenate([kv(5, 0), kv(5, 1)]))
    put(np.arange(_OFF_MERGE, _D_IN))
    put(np.arange(_OFF_MI, _OFF_MI + 8))
    put(np.arange(_OFF_NGATE, _OFF_NGATE + 24))
    pos += LANES - 32
    assert pos == _D_IN_PAD
    return idx, scale


def _gather_cols(a, idx, scale, dtype):
    pieces = []
    start = 0
    n = len(idx)
    for c in range(1, n + 1):
        same = c < n and scale[c] == scale[start] and (scale[c] == 0.0 or idx[c] == idx[c - 1] + 1)
        if not same:
            if scale[start] == 0.0:
                piece = jnp.zeros((a.shape[0], c - start), dtype)
            else:
                piece = a[:, int(idx[start]):int(idx[start]) + (c - start)]
                if scale[start] != 1.0:
                    piece = piece * float(scale[start])
            pieces.append(piece.astype(dtype))
            start = c
    return jnp.concatenate(pieces, axis=1)


def _inproj_kernel(x_ref, w_ref, b_ref, mqk_ref, mv_ref, mo_ref, nq_ref,
                   cin_ref, nsw_ref, mg_ref, small_ref):
    x = x_ref[...]
    rs = lax.rsqrt(jnp.mean(x * x, axis=-1, keepdims=True) + RMS_EPS)
    xb = x.astype(BF16)

    def seg(a, n):
        return _dot(xb, w_ref[:, a:a + n]) * rs + b_ref[:, a:a + n]

    off = {}
    pos = 0
    for name, w in _SEGS:
        off[name] = pos
        pos += w

    mqk_ref[...] = seg(off["mqk"], 1024).astype(BF16)
    mv_ref[...] = seg(off["mv"], 512).astype(BF16)
    mo_ref[...] = seg(off["mo"], 512).astype(BF16)
    nq_ref[...] = seg(off["nq"], 512).astype(BF16)
    cc = seg(off["kc"], 2 * LANES)
    cin_ref[0] = cc[:, :LANES]
    cin_ref[1] = cc[:, LANES:]
    nsw_ref[:, 0:2 * LANES] = seg(off["nsw"], 2 * LANES).astype(BF16)
    vv = seg(off["nsw"] + 2 * LANES, 2 * LANES)
    lo = lax.broadcasted_iota(jnp.int32, (x.shape[0], LANES), 1) < N_HEAD_DIM
    for j in range(2):
        vj = vv[:, j * LANES:(j + 1) * LANES]
        nsw_ref[:, (2 + 2 * j) * LANES:(3 + 2 * j) * LANES] = jnp.where(lo, vj, 1.0).astype(BF16)
        nsw_ref[:, (3 + 2 * j) * LANES:(4 + 2 * j) * LANES] = jnp.where(lo, 1.0, vj).astype(BF16)
    mg_ref[...] = seg(off["mg"], 2048).astype(BF16)
    small_ref[...] = seg(off["small"], 128)


def _inproj(x2, w, b, tm=512):
    n = x2.shape[0]
    assert n % tm == 0
    row = lambda width: pl.BlockSpec((tm, width), lambda i: (i, 0))
    out_shape = (
        jax.ShapeDtypeStruct((n, 1024), BF16), jax.ShapeDtypeStruct((n, 512), BF16),
        jax.ShapeDtypeStruct((n, 512), BF16), jax.ShapeDtypeStruct((n, 512), BF16),
        jax.ShapeDtypeStruct((2, n, 128), F32), jax.ShapeDtypeStruct((n, 768), BF16),
        jax.ShapeDtypeStruct((n, 2048), BF16), jax.ShapeDtypeStruct((n, 128), F32))
    out_specs = (row(1024), row(512), row(512), row(512),
                 pl.BlockSpec((2, tm, 128), lambda i: (0, i, 0)), row(768), row(2048), row(128))
    return pl.pallas_call(
        _inproj_kernel, out_shape=out_shape, grid=(n // tm,),
        in_specs=[row(D_MODEL), _const_spec((D_MODEL, _D_IN_PAD)), _const_spec((1, _D_IN_PAD))],
        out_specs=out_specs, name="inproj",
        compiler_params=pltpu.CompilerParams(dimension_semantics=("arbitrary",),
                                             vmem_limit_bytes=VMEM_LIMIT),
    )(x2, w, b)


def _mlstm_kernel(mqk_ref, mv_ref, mo_ref, small_ref, cw_ref, bf_ref, gh_ref, ltri_ref,
                  ym_ref, ebuf, c_st, m_st, g_bc, g_dt, g_a, g_mrow, g_row):
    L = M_CHUNK
    c = pl.program_id(1)
    n_chunks = g_bc.shape[0]
    rows = lax.broadcasted_iota(jnp.int32, (L, LANES), 0)

    @pl.when(c == 0)
    def _():
        ebuf[0:8, :] = jnp.zeros((8, 2 * M_WIDTH), F32)
        c_st[...] = jnp.zeros_like(c_st)
        m_st[...] = jnp.zeros_like(m_st)
        for cc in range(n_chunks):
            ig = small_ref[0, cc * L:(cc + 1) * L, :]
            fg = pltpu.roll(ig, LANES - M_HEADS, 1) + bf_ref[...]
            lf = jnp.minimum(fg, 0.0) - jnp.log1p(jnp.exp(-jnp.abs(fg)))
            bc = jnp.dot(ltri_ref[...], lf, precision=HI, preferred_element_type=F32)
            blast = bc[L - 1:L, :]
            a_all = blast - bc + ig
            d = ig - bc
            cm = d
            sh = 1
            while sh < L:
                cm = jnp.maximum(cm, jnp.where(rows >= sh, pltpu.roll(cm, sh, 0), -jnp.inf))
                sh *= 2
            g_bc[cc] = bc
            g_dt[cc] = d.T
            g_a[cc] = a_all
            g_mrow[cc] = bc + cm
            g_row[cc, 0:1, :] = blast
            g_row[cc, 1:2, :] = jnp.max(a_all, axis=0, keepdims=True)

    ebuf[8:8 + L, :] = mqk_ref[0].astype(F32)
    ext = ebuf[...]
    conv = cw_ref[CONV_WIDTH - 1:CONV_WIDTH, :] * ext[8:, :]
    for k in range(1, CONV_WIDTH):
        conv = conv + cw_ref[CONV_WIDTH - 1 - k:CONV_WIDTH - k, :] * pltpu.roll(ext, k, 0)[8:, :]
    ebuf[0:8, :] = ebuf[L:L + 8, :]
    qk = (conv * jax.nn.sigmoid(conv) * cw_ref[CONV_WIDTH:CONV_WIDTH + 1, :]).astype(BF16)

    bc = g_bc[c]
    dt = g_dt[c]
    a_all = g_a[c]
    blast = g_row[c, 0:1, :]
    m_prev = m_st[0:1, :]
    m_new = jnp.maximum(blast + m_prev, g_row[c, 1:2, :])
    decay = jnp.exp(blast + m_prev - m_new)
    w_all = jnp.exp(a_all - m_new)
    log_inter = bc + m_prev
    m_i_all = jnp.maximum(g_mrow[c], log_inter)
    s_inter_all = jnp.exp(log_inter - m_i_all)
    emi_all = jnp.exp(-m_i_all)
    u_all = bc - m_i_all

    row = lax.broadcasted_iota(jnp.int32, (L, L), 0)
    col = lax.broadcasted_iota(jnp.int32, (L, L), 1)
    causal = row >= col

    ones = jnp.ones((L, M_HEAD_DIM), BF16)
    for h in range(M_HEADS):
        sl = slice(h * M_HEAD_DIM, (h + 1) * M_HEAD_DIM)
        qb = qk[:, sl]
        kb = qk[:, M_WIDTH + h * M_HEAD_DIM:M_WIDTH + (h + 1) * M_HEAD_DIM]
        v = mv_ref[0, :, sl]
        cn_prev = c_st[h]

        logp = jnp.where(causal, u_all[:, h:h + 1] + dt[h:h + 1, :], -jnp.inf)
        p = jnp.exp(logp) * _dot_nt(qb, kb)
        nd = (_dot(p.astype(BF16), jnp.concatenate([v, ones], axis=1))
              + s_inter_all[:, h:h + 1] * _dot(qb, cn_prev.astype(BF16)))
        hh = nd[:, :M_HEAD_DIM] / jnp.maximum(jnp.abs(nd[:, M_HEAD_DIM:]), emi_all[:, h:h + 1])
        hh = hh * lax.rsqrt(jnp.mean(hh * hh, axis=-1, keepdims=True) + RMS_EPS) * gh_ref[:, sl]
        ym_ref[0, :, sl] = (jax.nn.sigmoid(mo_ref[0, :, sl].astype(F32)) * hh).astype(ym_ref.dtype)

        w_col = w_all[:, h:h + 1]
        dec = decay[:, h:h + 1]
        vw = jnp.concatenate([v.astype(F32) * w_col, jnp.broadcast_to(w_col, (L, M_HEAD_DIM))], axis=1)
        c_st[h] = dec * cn_prev + _dot_tn(kb, vw.astype(BF16))

    m_st[0:1, :] = m_new


def _mlstm(mqk, mv, mo, small, cw, bf, gh, ltri):
    B, T, _ = mqk.shape
    L = M_CHUNK
    assert T % L == 0
    nc = T // L
    blk = lambda w: pl.BlockSpec((1, L, w), lambda b, c: (b, c, 0))
    gate = lambda: pltpu.VMEM((nc, L, LANES), F32)
    return pl.pallas_call(
        _mlstm_kernel, out_shape=jax.ShapeDtypeStruct((B, T, M_WIDTH), BF16),
        grid=(B, nc),
        in_specs=[blk(2 * M_WIDTH), blk(M_WIDTH), blk(M_WIDTH),
                  pl.BlockSpec((1, T, LANES), lambda b, c: (b, 0, 0)),
                  _const_spec((8, 2 * M_WIDTH)), _const_spec((1, LANES)),
                  _const_spec((1, M_WIDTH)), _const_spec((L, L))],
        out_specs=blk(M_WIDTH),
        scratch_shapes=[pltpu.VMEM((L + 8, 2 * M_WIDTH), F32),
                        pltpu.VMEM((M_HEADS, M_HEAD_DIM, 2 * M_HEAD_DIM), F32),
                        pltpu.VMEM((8, LANES), F32),
                        gate(), gate(), gate(), gate(), pltpu.VMEM((nc, 8, LANES), F32)],
        name="mlstm",
        compiler_params=pltpu.CompilerParams(dimension_semantics=("arbitrary", "arbitrary")),
    )(mqk, mv, mo, small, cw, bf, gh, ltri)


def _compress_kernel(x_ref, pe_ref, w1t_ref, w1b_ref, w2_ref, o_ref, *, nch, nbt):
    half = CMP_BLOCK // 2
    rows = nbt * nch
    top = jnp.zeros((rows, 2 * CMP_HIDDEN), F32)
    bot = jnp.zeros((rows, 2 * CMP_HIDDEN), F32)
    for p in range(half):
        xp = jnp.concatenate([x_ref[0, e, pl.ds(p, nch, stride=CMP_STRIDE), :] for e in range(nbt)], axis=0)
        top = top + _dot((xp + pe_ref[0, p:p + 1, :]).astype(BF16), w1t_ref[0, p])
        bot = bot + _dot((xp + pe_ref[0, half + p:half + p + 1, :]).astype(BF16), w1b_ref[0, p])
    hid = top + pltpu.roll(bot, rows - 1, 0)
    act = hid * jax.nn.sigmoid(hid)
    out = _dot(act.astype(BF16), w2_ref[0]).astype(o_ref.dtype)
    for e in range(nbt):
        o_ref[0, e] = out[e * nch:(e + 1) * nch]


def _compress(cin, pe2, w1t, w1b, w2):
    _, B, T, _ = cin.shape
    nch = T // CMP_STRIDE
    nbt = 4 if B % 4 == 0 else 1
    sel = lambda *shape: pl.BlockSpec((1,) + shape, lambda j, b: (j,) + (0,) * len(shape))
    return pl.pallas_call(
        functools.partial(_compress_kernel, nch=nch, nbt=nbt),
        out_shape=jax.ShapeDtypeStruct((2, B, nch, LANES), BF16),
        grid=(2, B // nbt),
        in_specs=[pl.BlockSpec((1, nbt, T, LANES), lambda j, b: (j, b, 0, 0)),
                  sel(CMP_BLOCK, LANES), sel(CMP_BLOCK // 2, LANES, 2 * CMP_HIDDEN),
                  sel(CMP_BLOCK // 2, LANES, 2 * CMP_HIDDEN), sel(2 * CMP_HIDDEN, LANES)],
        out_specs=pl.BlockSpec((1, nbt, nch, LANES), lambda j, b: (j, b, 0, 0)),
        name="compress",
        compiler_params=pltpu.CompilerParams(dimension_semantics=("arbitrary", "arbitrary")),
    )(cin, pe2, w1t, w1b, w2)


def _bias_kernel(tbl_ref, o_ref, *, kind, n_cmp):
    pid = pl.program_id(0)
    toeplitz = kind != "cmp"
    if toeplitz:
        k = lax.broadcasted_iota(jnp.int32, (8, 2 * LANES), 1)
        dist = (pid - 1) * TQ + jnp.where(k < LANES, -k, 2 * LANES - k)
    else:
        al = lax.broadcasted_iota(jnp.int32, (CMP_STRIDE, 2 * LANES), 0)
        m = lax.broadcasted_iota(jnp.int32, (CMP_STRIDE, 2 * LANES), 1)
        dist = pid * TQ - (CMP_BLOCK - 1) + al - CMP_STRIDE * jnp.where(m < LANES, m, m - 2 * LANES)
    n = jnp.maximum(dist, 0)
    cnt = jnp.zeros_like(n)
    for t in _BUCKET_THR:
        cnt = cnt + jnp.where(n >= t, 1, 0)
    bucket = jnp.where(n < REL_BUCKETS // 2, n, REL_BUCKETS // 2 + cnt)
    if kind == "tok":
        madd = jnp.where((dist >= 0) & (pid > 0), 0.0, NEG)
    elif kind == "win":
        madd = jnp.where((dist >= 0) & (dist < WINDOW) & (pid > 0), 0.0, NEG)
    else:
        madd = jnp.where(dist >= 0, 0.0, NEG)
        valid_c = lax.broadcasted_iota(jnp.int32, (TQ, LANES), 1) < n_cmp
    for h in range(N_HEADS):
        val = jnp.zeros(dist.shape, F32)
        for bb in range(REL_BUCKETS):
            val = jnp.where(bucket == bb, tbl_ref[h * REL_BUCKETS + bb], val)
        val = val * LOG2E + madd
        if toeplitz:
            g = jnp.broadcast_to(val[0:1, :], (TQ, 2 * LANES))
            val = pltpu.roll(g, 0, 1, stride=1, stride_axis=0)[:, :LANES]
        else:
            bands = [val[:, :LANES]] + [pltpu.roll(val, ah, 1)[:, :LANES] for ah in range(1, TQ // CMP_STRIDE)]
            val = jnp.where(valid_c, jnp.concatenate(bands, axis=0), NEG)
        o_ref[0, h * TQ:(h + 1) * TQ, :] = val


def _bias_tiles(tbl, n_tiles, kind, n_cmp=0):
    return pl.pallas_call(
        functools.partial(_bias_kernel, kind=kind, n_cmp=n_cmp),
        out_shape=jax.ShapeDtypeStruct((n_tiles, N_HEADS * TQ, LANES), F32),
        grid=(n_tiles,),
        in_specs=[pl.BlockSpec(memory_space=pltpu.SMEM)],
        out_specs=pl.BlockSpec((1, N_HEADS * TQ, LANES), lambda i: (i, 0, 0)),
        name="bias_" + kind,
        compiler_params=pltpu.CompilerParams(dimension_semantics=("arbitrary",)),
    )(tbl)


def _nsa_kernel(nq_ref, nq4_ref, ks_ref, kw_ref, vs0_ref, vs1_ref, vw0_ref, vw1_ref, kc_ref, vc_ref,
                small_ref, small4_ref, bt_ref, wb_ref, cb4_ref, ovt_ref, et_ref, yn_ref,
                lhs_sc, z_sc, seln_sc, partc_sc, *, nb, n_slc, n_top, n_win, n_chunks_max):
    qi = pl.program_id(1)
    HR = N_HEADS * TQ
    GR = N_REP * TQ
    CH = 4 * TQ
    G4 = 4
    vs_refs = (vs0_ref, vs1_ref)
    vw_refs = (vw0_ref, vw1_ref)
    lane = lax.broadcasted_iota(jnp.int32, (TQ, LANES), 1)
    lo = lane < N_HEAD_DIM

    def stack_heads(q_all):
        zero = jnp.zeros((TQ, LANES), BF16)
        parts = []
        for g in range(N_KV_GROUPS):
            for r in range(N_REP):
                qr = q_all[:, r * LANES:(r + 1) * LANES]
                parts.append(jnp.where(lo if g == 0 else jnp.logical_not(lo), qr, zero))
        return parts

    def pair(o, r):
        return jnp.where(lo, o[r * TQ:(r + 1) * TQ], o[(N_REP + r) * TQ:(N_REP + r + 1) * TQ])

    def normed(acc, r):
        den = jnp.where(lo, acc[(N_REP + r) * TQ:(N_REP + r + 1) * TQ], acc[r * TQ:(r + 1) * TQ])
        return pair(acc, r) / pltpu.roll(den, N_HEAD_DIM, 1)

    def gate(sg, r, ci):
        c0 = _SMALL_NGATE + r * 3 + ci
        c1 = _SMALL_NGATE + (N_REP + r) * 3 + ci
        return jnp.where(lo, sg[:, c0:c0 + 1], sg[:, c1:c1 + 1])

    w0 = jnp.maximum(qi - (n_win - 1), 0)
    woff = pl.multiple_of(w0 * TQ, TQ)

    def group_prep(s, n_chunks):
        qs4 = jnp.concatenate(
            [p for j in range(G4) for p in stack_heads(nq4_ref[s, j * TQ:(j + 1) * TQ, :])], axis=0)
        rows = lax.broadcasted_iota(jnp.int32, (G4 * HR, 1), 0)
        t_rows = (qi + jnp.right_shift(rows, 10)) * TQ + (rows & (TQ - 1))

        z = _dot_nt(qs4, kc_ref[0, s]) + cb4_ref[...].reshape(G4 * HR, LANES)
        e = jnp.exp2(z - jnp.max(z, axis=-1, keepdims=True))
        l = _dot(e.astype(BF16), jnp.ones((LANES, LANES), BF16))
        p_c = e * jnp.where(t_rows >= CMP_BLOCK - 1, 1.0 / jnp.maximum(l, 1e-30), 0.0)
        o_c = _dot(p_c.astype(BF16), vc_ref[0, s])
        sg4 = jax.nn.sigmoid(small4_ref[s])
        for j in range(G4):
            for r in range(N_REP):
                partc_sc[s, j, r] = (gate(sg4[j * TQ:(j + 1) * TQ], r, 0)
                                     * pair(o_c[j * HR:(j + 1) * HR], r))

        W = G4 * N_KV_GROUPS * TQ
        jb = lax.broadcasted_iota(jnp.int32, (n_slc, W), 0)
        col = lax.broadcasted_iota(jnp.int32, (n_slc, W), 1)
        tq = (qi + jnp.right_shift(col, 8)) * TQ + (col & (TQ - 1))
        cur = jnp.right_shift(tq, 6)
        elig = jb <= cur
        if n_chunks * CH <= n_top * SEL_BLOCK:
            sel = jnp.where(elig, 1.0, 0.0)
        else:
            forced = (jb == 0) | (jb == cur) | (jb == cur - 1)
            psums = []
            for j in range(G4):
                for g in range(N_KV_GROUPS):
                    base = j * HR + g * GR
                    ps = p_c[base:base + TQ]
                    for r in range(1, N_REP):
                        ps = ps + p_c[base + r * TQ:base + (r + 1) * TQ]
                    psums.append(ps)
            imp = lax.dot_general(ovt_ref[...], jnp.concatenate(psums, axis=0), (((1,), (1,)), ((), ())),
                                  precision=HI, preferred_element_type=F32)
            score = jnp.where(elig, jnp.where(forced, BIG, imp), -BIG)
            cnt = jnp.zeros((n_slc, W), F32)
            for i in range(n_slc):
                si = score[i:i + 1, :]
                tie = jnp.where(jb > i, 1.0, 0.0)
                cnt = cnt + jnp.where(si > score, 1.0, jnp.where(si == score, tie, 0.0))
            sel = jnp.where((cnt < n_top) & (score > -BIG / 2), 1.0, 0.0)
        seln = jnp.concatenate([sel - 1.0, jnp.zeros((LANES - n_slc, W), F32)], axis=0).T.astype(BF16)
        for j in range(G4):
            for g in range(N_KV_GROUPS):
                seln_sc[s, j, g] = seln[(j * N_KV_GROUPS + g) * TQ:(j * N_KV_GROUPS + g + 1) * TQ]

    def tiles(n_pairs):
        jq = qi & (G4 - 1)
        elems = range(nb)
        cat = lambda xs: xs[0] if len(xs) == 1 else jnp.concatenate(xs, axis=0)
        rep = lambda t: t if nb == 1 else jnp.concatenate([t] * nb, axis=0)
        parts = [stack_heads(nq_ref[s]) for s in elems]
        qs = [jnp.concatenate(parts[s], axis=0) for s in elems]

        zw = []
        zmax = None
        for j0 in range(0, n_win, 2):
            nj = min(2, n_win - j0)
            sw = cat([_dot_nt(qs[s], kw_ref[s, pl.ds(woff + j0 * TQ, nj * TQ), :]) for s in elems])
            for u in range(nj):
                zj = sw[:, u * TQ:(u + 1) * TQ] + rep(wb_ref[jnp.maximum(qi - w0 - (j0 + u) + 1, 0)])
                zw.append(zj)
                zmax = zj if zmax is None else jnp.maximum(zmax, zj)
        mw = jnp.max(zmax, axis=-1, keepdims=True)
        pw = jnp.concatenate([jnp.exp2(zj - mw).astype(BF16) for zj in zw], axis=1)
        acc_w = cat([_dot(pw[s * HR + g * GR:s * HR + (g + 1) * GR],
                          vw_refs[g][s, pl.ds(woff, n_win * TQ), :])
                     for s in elems for g in range(N_KV_GROUPS)])

        for s in elems:
            for g in range(N_KV_GROUPS):
                seln = seln_sc[s, jq, g]
                for r in range(N_REP):
                    h = g * N_REP + r
                    lhs_sc[s, h * TQ:(h + 1) * TQ, :] = jnp.concatenate([parts[s][h], seln], axis=1)

        zmax = None
        for pi in range(n_pairs):
            k0 = pi * 2 * TQ
            sz = cat([_dot_nt(lhs_sc[s], jnp.concatenate([ks_ref[s, k0:k0 + 2 * TQ, :],
                                                          et_ref[k0:k0 + 2 * TQ, :]], axis=1))
                      for s in elems])
            for u in range(2):
                kt = 2 * pi + u
                zt = sz[:, u * TQ:(u + 1) * TQ] + rep(bt_ref[jnp.maximum(qi - kt + 1, 0)])
                z_sc[pi // 2, :, (kt % 4) * TQ:(kt % 4 + 1) * TQ] = zt
                zmax = zt if zmax is None else jnp.maximum(zmax, zt)
        ms = jnp.max(zmax, axis=-1, keepdims=True)
        acc = [None] * (nb * N_KV_GROUPS)
        for c in range((n_pairs + 1) // 2):
            width = min(CH, n_pairs * 2 * TQ - c * CH)
            p = jnp.exp2(z_sc[c, :, 0:width] - ms).astype(BF16)
            for s in elems:
                for g in range(N_KV_GROUPS):
                    i = s * N_KV_GROUPS + g
                    d = _dot(p[i * GR:(i + 1) * GR], vs_refs[g][s, c * CH:c * CH + width, :])
                    acc[i] = d if acc[i] is None else acc[i] + d
        acc_s = cat(acc)

        for s in elems:
            sg = jax.nn.sigmoid(small_ref[s])
            a_s = acc_s[s * HR:(s + 1) * HR]
            a_w = acc_w[s * HR:(s + 1) * HR]
            for r in range(N_REP):
                out = (partc_sc[s, jq, r] + gate(sg, r, 1) * normed(a_s, r)
                       + gate(sg, r, 2) * normed(a_w, r))
                yn_ref[s, :, r * LANES:(r + 1) * LANES] = out.astype(yn_ref.dtype)

    for nc in range(1, n_chunks_max + 1):
        @pl.when(qi // G4 == nc - 1)
        def _(nc=nc):
            @pl.when(qi % G4 == 0)
            def _():
                for s in range(nb):
                    group_prep(s, nc)

            tiles(2 * nc)


def _nsa(nq, nsw, ckv, small, bt, wb, cb, ov, emat):
    B, T, _ = nq.shape
    assert T == 2048, "single 128-wide compressed-key tile assumes T == 2048"
    nqt = T // TQ
    n_cmp = (T - CMP_BLOCK) // CMP_STRIDE + 1
    n_slc = T // SEL_BLOCK
    n_top = min(SEL_TOPK, n_slc)
    n_win = wb.shape[0] - 1
    HR = N_HEADS * TQ
    assert nqt % 4 == 0 and n_slc % 8 == 0 and n_slc <= LANES
    nb = NSA_BATCH if B % NSA_BATCH == 0 else 1
    n_chunks_max = nqt // 4
    kv = lambda j: pl.BlockSpec((nb, T, LANES), lambda b, q: (b, 0, j))
    ck = lambda j: pl.BlockSpec((1, nb, T // CMP_STRIDE, LANES), lambda b, q: (j, b, 0, 0))
    kern = functools.partial(_nsa_kernel, nb=nb, n_slc=n_slc, n_top=n_top, n_win=n_win,
                             n_chunks_max=n_chunks_max)
    return pl.pallas_call(
        kern, out_shape=jax.ShapeDtypeStruct((B, T, N_WIDTH), BF16),
        grid=(B // nb, nqt),
        in_specs=[pl.BlockSpec((nb, TQ, N_WIDTH), lambda b, q: (b, q, 0)),
                  pl.BlockSpec((nb, 4 * TQ, N_WIDTH), lambda b, q: (b, q // 4, 0)),
                  kv(0), kv(1), kv(2), kv(3), kv(4), kv(5), ck(0), ck(1),
                  pl.BlockSpec((nb, TQ, LANES), lambda b, q: (b, q, 0)),
                  pl.BlockSpec((nb, 4 * TQ, LANES), lambda b, q: (b, q // 4, 0)),
                  _const_spec(bt.shape), _const_spec(wb.shape),
                  pl.BlockSpec((4, HR, LANES), lambda b, q: (q // 4, 0, 0)),
                  _const_spec(ov.shape), _const_spec(emat.shape)],
        out_specs=pl.BlockSpec((nb, TQ, N_WIDTH), lambda b, q: (b, q, 0)),
        scratch_shapes=[pltpu.VMEM((nb, HR, 2 * LANES), BF16),
                        pltpu.VMEM((n_chunks_max, nb * HR, 4 * TQ), F32),
                        pltpu.VMEM((nb, 4, N_KV_GROUPS, TQ, LANES), BF16),
                        pltpu.VMEM((nb, 4, N_REP, TQ, LANES), F32)],
        name="nsa",
        compiler_params=pltpu.CompilerParams(dimension_semantics=("arbitrary", "arbitrary"),
                                             vmem_limit_bytes=VMEM_LIMIT),
    )(nq, nq, nsw, nsw, nsw, nsw, nsw, nsw, ckv, ckv, small, small, bt, wb, cb, ov, emat)


def _merge_kernel(x_ref, ym_ref, yn_ref, mg_ref, wbm_ref, wbn_ref, wo_ref,
                  wg_ref, wu_ref, wd_ref, gfin_ref, o_ref, *, tf):
    bm = _dot(ym_ref[...], wbm_ref[...])
    bn = _dot(yn_ref[...], wbn_ref[...])
    mixed = (jax.nn.sigmoid(mg_ref[:, :D_MODEL].astype(F32)) * bm
             + jax.nn.sigmoid(mg_ref[:, D_MODEL:].astype(F32)) * bn)
    h = x_ref[...] + _dot(mixed.astype(BF16), wo_ref[...])
    rs = lax.rsqrt(jnp.mean(h * h, axis=-1, keepdims=True) + RMS_EPS)
    hb = h.astype(BF16)
    acc = jnp.zeros(h.shape, F32)
    for j in range(D_FF // tf):
        gg = _dot(hb, wg_ref[:, j * tf:(j + 1) * tf]) * rs
        uu = _dot(hb, wu_ref[:, j * tf:(j + 1) * tf]) * rs
        act = (gg * jax.nn.sigmoid(gg) * uu).astype(BF16)
        acc = acc + _dot(act, wd_ref[j * tf:(j + 1) * tf, :])
    h2 = h + acc
    o_ref[...] = h2 * lax.rsqrt(jnp.mean(h2 * h2, axis=-1, keepdims=True) + RMS_EPS) * gfin_ref[...]


def _merge(x2, ym, yn, mg, wbm, wbn, wo, wg, wu, wd, gfin, tm=512, tf=256):
    n = x2.shape[0]
    assert n % tm == 0 and D_FF % tf == 0
    row = lambda width: pl.BlockSpec((tm, width), lambda i: (i, 0))
    return pl.pallas_call(
        functools.partial(_merge_kernel, tf=tf),
        out_shape=jax.ShapeDtypeStruct((n, D_MODEL), F32), grid=(n // tm,),
        in_specs=[row(D_MODEL), row(M_WIDTH), row(N_WIDTH), row(N_BRANCH * D_MODEL),
                  _const_spec(wbm.shape), _const_spec(wbn.shape), _const_spec(wo.shape),
                  _const_spec(wg.shape), _const_spec(wu.shape),
                  _const_spec(wd.shape), _const_spec(gfin.shape)],
        out_specs=row(D_MODEL), name="merge_ffn",
        compiler_params=pltpu.CompilerParams(dimension_semantics=("arbitrary",),
                                             vmem_limit_bytes=VMEM_LIMIT),
    )(x2, ym, yn, mg, wbm, wbn, wo, wg, wu, wd, gfin)


def _nsa_constants(T):
    n_cmp = (T - CMP_BLOCK) // CMP_STRIDE + 1
    n_slc = T // SEL_BLOCK
    cs = np.arange(n_cmp) * CMP_STRIDE
    ss = np.arange(n_slc) * SEL_BLOCK
    ov = np.clip(np.minimum(cs[:, None] + CMP_BLOCK, ss[None, :] + SEL_BLOCK)
                 - np.maximum(cs[:, None], ss[None, :]), 0, None) / CMP_STRIDE
    ovt = np.zeros((n_slc, LANES), np.float32)
    ovt[:, :n_cmp] = ov.T
    et = (np.arange(T)[:, None] // SEL_BLOCK == np.arange(LANES)[None, :]).astype(np.float32) * (-NEG)
    return jnp.asarray(ovt), jnp.asarray(et, dtype=BF16)


def _compress_weights(pe_cmp, w_cmp1, w_cmp2):
    half = CMP_BLOCK // 2
    dh, hid = N_HEAD_DIM, CMP_HIDDEN
    eye = jnp.eye(N_KV_GROUPS, dtype=F32)
    w1 = w_cmp1.reshape(2, CMP_BLOCK, dh, hid)
    w1bd = jnp.einsum("jpdn,gk->jpgdkn", w1, eye).reshape(2, CMP_BLOCK, N_KV_GROUPS * dh, N_KV_GROUPS * hid)
    w2bd = jnp.einsum("jnd,gk->jgnkd", w_cmp2, eye).reshape(2, N_KV_GROUPS * hid, N_KV_GROUPS * dh)
    pe2 = jnp.tile(pe_cmp, (1, 1, N_KV_GROUPS))
    return pe2, w1bd[:, :half].astype(BF16), w1bd[:, half:].astype(BF16), w2bd.astype(BF16)


def kernel(x, g_norm_mix, w_in, b_in, b_fgate, conv_qk, g_mlstm_head, pe_cmp, w_cmp1, w_cmp2,
           rel_bias, w_branch, w_out, g_norm_ffn, w_gate, w_up, w_down, g_final):
    B, T, D = x.shape
    assert D == D_MODEL and w_in.shape[0] == 1, "one residual block (DEPTH == 1)"
    N = B * T
    x2 = x.reshape(N, D)

    idx, scale = _inproj_perm()
    w_r = _gather_cols(w_in[0] * g_norm_mix[0][:, None], idx, scale, BF16)
    b_r = _gather_cols(b_in[0].reshape(1, -1), idx, scale, F32)
    post = np.concatenate([np.ones(M_WIDTH, np.float32), np.full(M_WIDTH, M_HEAD_DIM ** -0.5, np.float32)])
    cw = jnp.zeros((8, 2 * M_WIDTH), F32).at[:CONV_WIDTH].set(conv_qk[0]).at[CONV_WIDTH].set(jnp.asarray(post))
    mqk, mv, mo, nq, cin, nsw, mg, small = _inproj(x2, w_r, b_r)

    bf = jnp.zeros((1, LANES), F32).at[0, :M_HEADS].set(b_fgate[0])
    ltri = jnp.asarray(np.tril(np.ones((M_CHUNK, M_CHUNK), np.float32)))
    r3 = lambda a: a.reshape(B, T, a.shape[-1])
    ym = _mlstm(r3(mqk), r3(mv), r3(mo), r3(small), cw, bf, g_mlstm_head[0].reshape(1, M_WIDTH), ltri)

    pe2, w1t, w1b, w2 = _compress_weights(pe_cmp[0], w_cmp1[0], w_cmp2[0])
    ckv = _compress(cin.reshape(2, B, T, LANES), pe2, w1t, w1b, w2)
    tbl = rel_bias.astype(F32).T.reshape(-1)
    nqt = T // TQ
    bt = _bias_tiles(tbl, nqt + 1, "tok")
    wb = _bias_tiles(tbl, min(WINDOW // TQ + 1, nqt) + 1, "win")
    cb = _bias_tiles(tbl, nqt, "cmp", n_cmp=(T - CMP_BLOCK) // CMP_STRIDE + 1)
    ov, emat = _nsa_constants(T)
    yn = _nsa(r3(nq), r3(nsw), ckv, r3(small), bt, wb, cb, ov, emat)

    wbm = w_branch[0, 0].astype(BF16)
    wbn = jnp.concatenate(
        [w_branch[0, 1, (g * N_REP + r) * N_HEAD_DIM:(g * N_REP + r + 1) * N_HEAD_DIM]
         for r in range(N_REP) for g in range(N_KV_GROUPS)], axis=0).astype(BF16)
    out = _merge(x2, ym.reshape(N, M_WIDTH), yn.reshape(N, N_WIDTH), mg, wbm, wbn,
                 w_out[0].astype(BF16), (w_gate[0] * g_norm_ffn[0][:, None]).astype(BF16),
                 (w_up[0] * g_norm_ffn[0][:, None]).astype(BF16), w_down[0].astype(BF16),
                 g_final.reshape(1, D))
    return out.reshape(B, T, D)
```

```python
import functools

import numpy as np
import jax
import jax.numpy as jnp
from jax import lax
from jax.experimental import pallas as pl
from jax.experimental.pallas import tpu as pltpu

F32 = jnp.float32
BF16 = jnp.bfloat16
HI = lax.Precision.HIGHEST

D_MODEL = 1024
M_HEADS = 4
M_HEAD_DIM = 128
M_WIDTH = M_HEADS * M_HEAD_DIM
M_CHUNK = 128
CONV_WIDTH = 4
N_HEADS = 8
N_KV_GROUPS = 2
N_REP = N_HEADS // N_KV_GROUPS
N_HEAD_DIM = 64
N_WIDTH = N_HEADS * N_HEAD_DIM
N_KV_WIDTH = N_KV_GROUPS * N_HEAD_DIM
CMP_BLOCK = 32
CMP_STRIDE = 16
CMP_HIDDEN = 2 * N_HEAD_DIM
SEL_BLOCK = 64
SEL_TOPK = 16
WINDOW = 512
REL_BUCKETS = 32
REL_MAX_DIST = 1024
N_BRANCH = 2
D_FF = 2816
RMS_EPS = 1e-6
BIG = 1e9
NEG = -1e30

LANES = 128
TQ = 128
NSA_BATCH = 1
VMEM_LIMIT = 56 * 1024 * 1024

_OFF_MQ, _OFF_MK, _OFF_MV, _OFF_MO = 0, 512, 1024, 1536
_OFF_MI, _OFF_MF, _OFF_NQ, _OFF_NKV = 2048, 2052, 2056, 2568
_OFF_NGATE, _OFF_MERGE, _D_IN = 3336, 3360, 5408
_SMALL_NGATE = 8

_SEGS = (("mqk", 1024), ("mv", 512), ("mo", 512), ("nq", 512), ("kc", 128),
         ("vc", 128), ("nsw", 512), ("mg", 2048), ("small", 128))
_D_IN_PAD = sum(w for _, w in _SEGS)
LOG2E = 1.4426950408889634


def _dot(a, b, **kw):
    return jnp.dot(a, b, preferred_element_type=F32, **kw)


def _dot_nt(a, b):
    return lax.dot_general(a, b, (((1,), (1,)), ((), ())), preferred_element_type=F32)


def _dot_tn(a, b):
    return lax.dot_general(a, b, (((0,), (0,)), ((), ())), preferred_element_type=F32)


def _const_spec(shape):
    nd = len(shape)
    return pl.BlockSpec(shape, lambda *_: (0,) * nd, pipeline_mode=pl.Buffered(1))


def _bucket_thresholds():
    max_exact = REL_BUCKETS // 2
    assert REL_MAX_DIST == 64 * max_exact and REL_BUCKETS - max_exact == 16
    thr = []
    for k in range(1, REL_BUCKETS - max_exact):
        t = max_exact
        while t ** 8 < (max_exact ** 8) * (2 ** (3 * k)):
            t += 1
        thr.append(t)
    return tuple(thr)


_BUCKET_THR = _bucket_thresholds()


def _inproj_perm():
    idx = np.zeros((_D_IN_PAD,), np.int32)
    scale = np.zeros((_D_IN_PAD,), np.float32)
    pos = 0

    def put(cols, s=1.0):
        nonlocal pos
        n = len(cols)
        idx[pos:pos + n] = cols
        scale[pos:pos + n] = s
        pos += n

    put(np.arange(_OFF_MQ, _OFF_MV))
    put(np.arange(_OFF_MV, _OFF_MO))
    put(np.arange(_OFF_MO, _OFF_MI))
    nq = np.zeros((N_WIDTH,), np.int32)
    for r in range(N_REP):
        for g in range(N_KV_GROUPS):
            for d in range(N_HEAD_DIM):
                nq[r * 128 + g * 64 + d] = _OFF_NQ + (g * N_REP + r) * N_HEAD_DIM + d
    put(nq, N_HEAD_DIM ** -0.5 * LOG2E)
    kv = lambda j, g: np.arange(_OFF_NKV + (j * N_KV_GROUPS + g) * N_HEAD_DIM,
                                _OFF_NKV + (j * N_KV_GROUPS + g + 1) * N_HEAD_DIM)
    put(np.arange(_OFF_NKV, _OFF_NKV + 256))
    put(np.concatenate([kv(2, 0), kv(2, 1)]))
    put(np.concatenate([kv(4, 0), kv(4, 1)]))
    put(np.concatenate([kv(3, 0), kv(3, 1)]))
    put(np.concatenate([kv(5, 0), kv(5, 1)]))
    put(np.arange(_OFF_MERGE, _D_IN))
    put(np.arange(_OFF_MI, _OFF_MI + 8))
    put(np.arange(_OFF_NGATE, _OFF_NGATE + 24))
    pos += LANES - 32
    assert pos == _D_IN_PAD
    return idx, scale


def _gather_cols(a, idx, scale, dtype):
    pieces = []
    start = 0
    n = len(idx)
    for c in range(1, n + 1):
        same = c < n and scale[c] == scale[start] and (scale[c] == 0.0 or idx[c] == idx[c - 1] + 1)
        if not same:
            if scale[start] == 0.0:
                piece = jnp.zeros((a.shape[0], c - start), dtype)
            else:
                piece = a[:, int(idx[start]):int(idx[start]) + (c - start)]
                if scale[start] != 1.0:
                    piece = piece * float(scale[start])
            pieces.append(piece.astype(dtype))
            start = c
    return jnp.concatenate(pieces, axis=1)


def _inproj_kernel(x_ref, w_ref, b_ref, mqk_ref, mv_ref, mo_ref, nq_ref,
                   cin_ref, nsw_ref, mg_ref, small_ref):
    x = x_ref[...]
    rs = lax.rsqrt(jnp.mean(x * x, axis=-1, keepdims=True) + RMS_EPS)
    xb = x.astype(BF16)

    def seg(a, n):
        return _dot(xb, w_ref[:, a:a + n]) * rs + b_ref[:, a:a + n]

    off = {}
    pos = 0
    for name, w in _SEGS:
        off[name] = pos
        pos += w

    mqk_ref[...] = seg(off["mqk"], 1024).astype(BF16)
    mv_ref[...] = seg(off["mv"], 512).astype(BF16)
    mo_ref[...] = seg(off["mo"], 512).astype(BF16)
    nq_ref[...] = seg(off["nq"], 512).astype(BF16)
    cc = seg(off["kc"], 2 * LANES)
    cin_ref[0] = cc[:, :LANES]
    cin_ref[1] = cc[:, LANES:]
    nsw_ref[:, 0:2 * LANES] = seg(off["nsw"], 2 * LANES).astype(BF16)
    vv = seg(off["nsw"] + 2 * LANES, 2 * LANES)
    lo = lax.broadcasted_iota(jnp.int32, (x.shape[0], LANES), 1) < N_HEAD_DIM
    for j in range(2):
        vj = vv[:, j * LANES:(j + 1) * LANES]
        nsw_ref[:, (2 + 2 * j) * LANES:(3 + 2 * j) * LANES] = jnp.where(lo, vj, 1.0).astype(BF16)
        nsw_ref[:, (3 + 2 * j) * LANES:(4 + 2 * j) * LANES] = jnp.where(lo, 1.0, vj).astype(BF16)
    mg_ref[...] = seg(off["mg"], 2048).astype(BF16)
    small_ref[...] = seg(off["small"], 128)


def _inproj(x2, w, b, tm=512):
    n = x2.shape[0]
    assert n % tm == 0
    row = lambda width: pl.BlockSpec((tm, width), lambda i: (i, 0))
    out_shape = (
        jax.ShapeDtypeStruct((n, 1024), BF16), jax.ShapeDtypeStruct((n, 512), BF16),
        jax.ShapeDtypeStruct((n, 512), BF16), jax.ShapeDtypeStruct((n, 512), BF16),
        jax.ShapeDtypeStruct((2, n, 128), F32), jax.ShapeDtypeStruct((n, 768), BF16),
        jax.ShapeDtypeStruct((n, 2048), BF16), jax.ShapeDtypeStruct((n, 128), F32))
    out_specs = (row(1024), row(512), row(512), row(512),
                 pl.BlockSpec((2, tm, 128), lambda i: (0, i, 0)), row(768), row(2048), row(128))
    return pl.pallas_call(
        _inproj_kernel, out_shape=out_shape, grid=(n // tm,),
        in_specs=[row(D_MODEL), _const_spec((D_MODEL, _D_IN_PAD)), _const_spec((1, _D_IN_PAD))],
        out_specs=out_specs, name="inproj",
        compiler_params=pltpu.CompilerParams(dimension_semantics=("arbitrary",),
                                             vmem_limit_bytes=VMEM_LIMIT),
    )(x2, w, b)


def _mlstm_kernel(mqk_ref, mv_ref, mo_ref, small_ref, cw_ref, bf_ref, gh_ref, ltri_ref,
                  ym_ref, ebuf, c_st, m_st, g_bc, g_dt, g_a, g_mrow, g_row):
    L = M_CHUNK
    c = pl.program_id(1)
    n_chunks = g_bc.shape[0]
    rows = lax.broadcasted_iota(jnp.int32, (L, LANES), 0)

    @pl.when(c == 0)
    def _():
        ebuf[0:8, :] = jnp.zeros((8, 2 * M_WIDTH), F32)
        c_st[...] = jnp.zeros_like(c_st)
        m_st[...] = jnp.zeros_like(m_st)
        for cc in range(n_chunks):
            ig = small_ref[0, cc * L:(cc + 1) * L, :]
            fg = pltpu.roll(ig, LANES - M_HEADS, 1) + bf_ref[...]
            lf = jnp.minimum(fg, 0.0) - jnp.log1p(jnp.exp(-jnp.abs(fg)))
            bc = jnp.dot(ltri_ref[...], lf, precision=HI, preferred_element_type=F32)
            blast = bc[L - 1:L, :]
            a_all = blast - bc + ig
            d = ig - bc
            cm = d
            sh = 1
            while sh < L:
                cm = jnp.maximum(cm, jnp.where(rows >= sh, pltpu.roll(cm, sh, 0), -jnp.inf))
                sh *= 2
            g_bc[cc] = bc
            g_dt[cc] = d.T
            g_a[cc] = a_all
            g_mrow[cc] = bc + cm
            g_row[cc, 0:1, :] = blast
            g_row[cc, 1:2, :] = jnp.max(a_all, axis=0, keepdims=True)

    ebuf[8:8 + L, :] = mqk_ref[0].astype(F32)
    ext = ebuf[...]
    conv = cw_ref[CONV_WIDTH - 1:CONV_WIDTH, :] * ext[8:, :]
    for k in range(1, CONV_WIDTH):
        conv = conv + cw_ref[CONV_WIDTH - 1 - k:CONV_WIDTH - k, :] * pltpu.roll(ext, k, 0)[8:, :]
    ebuf[0:8, :] = ebuf[L:L + 8, :]
    qk = (conv * jax.nn.sigmoid(conv) * cw_ref[CONV_WIDTH:CONV_WIDTH + 1, :]).astype(BF16)

    bc = g_bc[c]
    dt = g_dt[c]
    a_all = g_a[c]
    blast = g_row[c, 0:1, :]
    m_prev = m_st[0:1, :]
    m_new = jnp.maximum(blast + m_prev, g_row[c, 1:2, :])
    decay = jnp.exp(blast + m_prev - m_new)
    w_all = jnp.exp(a_all - m_new)
    log_inter = bc + m_prev
    m_i_all = jnp.maximum(g_mrow[c], log_inter)
    s_inter_all = jnp.exp(log_inter - m_i_all)
    emi_all = jnp.exp(-m_i_all)
    u_all = bc - m_i_all

    row = lax.broadcasted_iota(jnp.int32, (L, L), 0)
    col = lax.broadcasted_iota(jnp.int32, (L, L), 1)
    causal = row >= col

    ones = jnp.ones((L, M_HEAD_DIM), BF16)
    for h in range(M_HEADS):
        sl = slice(h * M_HEAD_DIM, (h + 1) * M_HEAD_DIM)
        qb = qk[:, sl]
        kb = qk[:, M_WIDTH + h * M_HEAD_DIM:M_WIDTH + (h + 1) * M_HEAD_DIM]
        v = mv_ref[0, :, sl]
        cn_prev = c_st[h]

        logp = jnp.where(causal, u_all[:, h:h + 1] + dt[h:h + 1, :], -jnp.inf)
        p = jnp.exp(logp) * _dot_nt(qb, kb)
        nd = (_dot(p.astype(BF16), jnp.concatenate([v, ones], axis=1))
              + s_inter_all[:, h:h + 1] * _dot(qb, cn_prev.astype(BF16)))
        hh = nd[:, :M_HEAD_DIM] / jnp.maximum(jnp.abs(nd[:, M_HEAD_DIM:]), emi_all[:, h:h + 1])
        hh = hh * lax.rsqrt(jnp.mean(hh * hh, axis=-1, keepdims=True) + RMS_EPS) * gh_ref[:, sl]
        ym_ref[0, :, sl] = (jax.nn.sigmoid(mo_ref[0, :, sl].astype(F32)) * hh).astype(ym_ref.dtype)

        w_col = w_all[:, h:h + 1]
        dec = decay[:, h:h + 1]
        vw = jnp.concatenate([v.astype(F32) * w_col, jnp.broadcast_to(w_col, (L, M_HEAD_DIM))], axis=1)
        c_st[h] = dec * cn_prev + _dot_tn(kb, vw.astype(BF16))

    m_st[0:1, :] = m_new


def _mlstm(mqk, mv, mo, small, cw, bf, gh, ltri):
    B, T, _ = mqk.shape
    L = M_CHUNK
    assert T % L == 0
    nc = T // L
    blk = lambda w: pl.BlockSpec((1, L, w), lambda b, c: (b, c, 0))
    gate = lambda: pltpu.VMEM((nc, L, LANES), F32)
    return pl.pallas_call(
        _mlstm_kernel, out_shape=jax.ShapeDtypeStruct((B, T, M_WIDTH), BF16),
        grid=(B, nc),
        in_specs=[blk(2 * M_WIDTH), blk(M_WIDTH), blk(M_WIDTH),
                  pl.BlockSpec((1, T, LANES), lambda b, c: (b, 0, 0)),
                  _const_spec((8, 2 * M_WIDTH)), _const_spec((1, LANES)),
                  _const_spec((1, M_WIDTH)), _const_spec((L, L))],
        out_specs=blk(M_WIDTH),
        scratch_shapes=[pltpu.VMEM((L + 8, 2 * M_WIDTH), F32),
                        pltpu.VMEM((M_HEADS, M_HEAD_DIM, 2 * M_HEAD_DIM), F32),
                        pltpu.VMEM((8, LANES), F32),
                        gate(), gate(), gate(), gate(), pltpu.VMEM((nc, 8, LANES), F32)],
        name="mlstm",
        compiler_params=pltpu.CompilerParams(dimension_semantics=("arbitrary", "arbitrary")),
    )(mqk, mv, mo, small, cw, bf, gh, ltri)


def _compress_kernel(x_ref, pe_ref, w1t_ref, w1b_ref, w2_ref, o_ref, *, nch, nbt):
    half = CMP_BLOCK // 2
    rows = nbt * nch
    top = jnp.zeros((rows, 2 * CMP_HIDDEN), F32)
    bot = jnp.zeros((rows, 2 * CMP_HIDDEN), F32)
    for p in range(half):
        xp = jnp.concatenate([x_ref[0, e, pl.ds(p, nch, stride=CMP_STRIDE), :] for e in range(nbt)], axis=0)
        top = top + _dot((xp + pe_ref[0, p:p + 1, :]).astype(BF16), w1t_ref[0, p])
        bot = bot + _dot((xp + pe_ref[0, half + p:half + p + 1, :]).astype(BF16), w1b_ref[0, p])
    hid = top + pltpu.roll(bot, rows - 1, 0)
    act = hid * jax.nn.sigmoid(hid)
    out = _dot(act.astype(BF16), w2_ref[0]).astype(o_ref.dtype)
    for e in range(nbt):
        o_ref[0, e] = out[e * nch:(e + 1) * nch]


def _compress(cin, pe2, w1t, w1b, w2):
    _, B, T, _ = cin.shape
    nch = T // CMP_STRIDE
    nbt = 4 if B % 4 == 0 else 1
    sel = lambda *shape: pl.BlockSpec((1,) + shape, lambda j, b: (j,) + (0,) * len(shape))
    return pl.pallas_call(
        functools.partial(_compress_kernel, nch=nch, nbt=nbt),
        out_shape=jax.ShapeDtypeStruct((2, B, nch, LANES), BF16),
        grid=(2, B // nbt),
        in_specs=[pl.BlockSpec((1, nbt, T, LANES), lambda j, b: (j, b, 0, 0)),
                  sel(CMP_BLOCK, LANES), sel(CMP_BLOCK // 2, LANES, 2 * CMP_HIDDEN),
                  sel(CMP_BLOCK // 2, LANES, 2 * CMP_HIDDEN), sel(2 * CMP_HIDDEN, LANES)],
        out_specs=pl.BlockSpec((1, nbt, nch, LANES), lambda j, b: (j, b, 0, 0)),
        name="compress",
        compiler_params=pltpu.CompilerParams(dimension_semantics=("arbitrary", "arbitrary")),
    )(cin, pe2, w1t, w1b, w2)


def _bias_kernel(tbl_ref, o_ref, *, kind, n_cmp):
    pid = pl.program_id(0)
    toeplitz = kind != "cmp"
    if toeplitz:
        k = lax.broadcasted_iota(jnp.int32, (8, 2 * LANES), 1)
        dist = (pid - 1) * TQ + jnp.where(k < LANES, -k, 2 * LANES - k)
    else:
        al = lax.broadcasted_iota(jnp.int32, (CMP_STRIDE, 2 * LANES), 0)
        m = lax.broadcasted_iota(jnp.int32, (CMP_STRIDE, 2 * LANES), 1)
        dist = pid * TQ - (CMP_BLOCK - 1) + al - CMP_STRIDE * jnp.where(m < LANES, m, m - 2 * LANES)
    n = jnp.maximum(dist, 0)
    cnt = jnp.zeros_like(n)
    for t in _BUCKET_THR:
        cnt = cnt + jnp.where(n >= t, 1, 0)
    bucket = jnp.where(n < REL_BUCKETS // 2, n, REL_BUCKETS // 2 + cnt)
    if kind == "tok":
        madd = jnp.where((dist >= 0) & (pid > 0), 0.0, NEG)
    elif kind == "win":
        madd = jnp.where((dist >= 0) & (dist < WINDOW) & (pid > 0), 0.0, NEG)
    else:
        madd = jnp.where(dist >= 0, 0.0, NEG)
        valid_c = lax.broadcasted_iota(jnp.int32, (TQ, LANES), 1) < n_cmp
    for h in range(N_HEADS):
        val = jnp.zeros(dist.shape, F32)
        for bb in range(REL_BUCKETS):
            val = jnp.where(bucket == bb, tbl_ref[h * REL_BUCKETS + bb], val)
        val = val * LOG2E + madd
        if toeplitz:
            g = jnp.broadcast_to(val[0:1, :], (TQ, 2 * LANES))
            val = pltpu.roll(g, 0, 1, stride=1, stride_axis=0)[:, :LANES]
        else:
            bands = [val[:, :LANES]] + [pltpu.roll(val, ah, 1)[:, :LANES] for ah in range(1, TQ // CMP_STRIDE)]
            val = jnp.where(valid_c, jnp.concatenate(bands, axis=0), NEG)
        o_ref[0, h * TQ:(h + 1) * TQ, :] = val


def _bias_tiles(tbl, n_tiles, kind, n_cmp=0):
    return pl.pallas_call(
        functools.partial(_bias_kernel, kind=kind, n_cmp=n_cmp),
        out_shape=jax.ShapeDtypeStruct((n_tiles, N_HEADS * TQ, LANES), F32),
        grid=(n_tiles,),
        in_specs=[pl.BlockSpec(memory_space=pltpu.SMEM)],
        out_specs=pl.BlockSpec((1, N_HEADS * TQ, LANES), lambda i: (i, 0, 0)),
        name="bias_" + kind,
        compiler_params=pltpu.CompilerParams(dimension_semantics=("arbitrary",)),
    )(tbl)


def _nsa_kernel(nq_ref, nq4_ref, ks_ref, kw_ref, vs0_ref, vs1_ref, vw0_ref, vw1_ref, kc_ref, vc_ref,
                small_ref, small4_ref, bt_ref, wb_ref, cb4_ref, ovt_ref, et_ref, yn_ref,
                lhs_sc, z_sc, zw_sc, seln_sc, partc_sc, *, nb, n_slc, n_top, n_win, n_chunks_max):
    qi = pl.program_id(1)
    HR = N_HEADS * TQ
    GR = N_REP * TQ
    CH = 4 * TQ
    G4 = 4
    vs_refs = (vs0_ref, vs1_ref)
    vw_refs = (vw0_ref, vw1_ref)
    lane = lax.broadcasted_iota(jnp.int32, (TQ, LANES), 1)
    lo = lane < N_HEAD_DIM

    def stack_heads(q_all):
        zero = jnp.zeros((TQ, LANES), BF16)
        parts = []
        for g in range(N_KV_GROUPS):
            for r in range(N_REP):
                qr = q_all[:, r * LANES:(r + 1) * LANES]
                parts.append(jnp.where(lo if g == 0 else jnp.logical_not(lo), qr, zero))
        return parts

    def pair(o, r):
        return jnp.where(lo, o[r * TQ:(r + 1) * TQ], o[(N_REP + r) * TQ:(N_REP + r + 1) * TQ])

    def normed(acc, r):
        den = jnp.where(lo, acc[(N_REP + r) * TQ:(N_REP + r + 1) * TQ], acc[r * TQ:(r + 1) * TQ])
        return pair(acc, r) / pltpu.roll(den, N_HEAD_DIM, 1)

    def gate(sg, r, ci):
        c0 = _SMALL_NGATE + r * 3 + ci
        c1 = _SMALL_NGATE + (N_REP + r) * 3 + ci
        return jnp.where(lo, sg[:, c0:c0 + 1], sg[:, c1:c1 + 1])

    w0 = jnp.maximum(qi - (n_win - 1), 0)
    woff = pl.multiple_of(w0 * TQ, TQ)

    def group_prep(s, n_chunks):
        qs4 = jnp.concatenate(
            [p for j in range(G4) for p in stack_heads(nq4_ref[s, j * TQ:(j + 1) * TQ, :])], axis=0)
        rows = lax.broadcasted_iota(jnp.int32, (G4 * HR, 1), 0)
        t_rows = (qi + jnp.right_shift(rows, 10)) * TQ + (rows & (TQ - 1))

        z = _dot_nt(qs4, kc_ref[0, s]) + cb4_ref[...].reshape(G4 * HR, LANES)
        e = jnp.exp2(z - jnp.max(z, axis=-1, keepdims=True))
        l = _dot(e.astype(BF16), jnp.ones((LANES, LANES), BF16))
        p_c = e * jnp.where(t_rows >= CMP_BLOCK - 1, 1.0 / jnp.maximum(l, 1e-30), 0.0)
        o_c = _dot(p_c.astype(BF16), vc_ref[0, s])
        sg4 = jax.nn.sigmoid(small4_ref[s])
        for j in range(G4):
            for r in range(N_REP):
                partc_sc[s, j, r] = (gate(sg4[j * TQ:(j + 1) * TQ], r, 0)
                                     * pair(o_c[j * HR:(j + 1) * HR], r))

        W = G4 * N_KV_GROUPS * TQ
        jb = lax.broadcasted_iota(jnp.int32, (n_slc, W), 0)
        col = lax.broadcasted_iota(jnp.int32, (n_slc, W), 1)
        tq = (qi + jnp.right_shift(col, 8)) * TQ + (col & (TQ - 1))
        cur = jnp.right_shift(tq, 6)
        elig = jb <= cur
        if n_chunks * CH <= n_top * SEL_BLOCK:
            sel = jnp.where(elig, 1.0, 0.0)
        else:
            forced = (jb == 0) | (jb == cur) | (jb == cur - 1)
            psums = []
            for j in range(G4):
                for g in range(N_KV_GROUPS):
                    base = j * HR + g * GR
                    ps = p_c[base:base + TQ]
                    for r in range(1, N_REP):
                        ps = ps + p_c[base + r * TQ:base + (r + 1) * TQ]
                    psums.append(ps)
            imp = lax.dot_general(ovt_ref[...], jnp.concatenate(psums, axis=0), (((1,), (1,)), ((), ())),
                                  precision=HI, preferred_element_type=F32)
            score = jnp.where(elig, jnp.where(forced, BIG, imp), -BIG)
            cnt = jnp.zeros((n_slc, W), F32)
            for i in range(n_slc):
                si = score[i:i + 1, :]
                tie = jnp.where(jb > i, 1.0, 0.0)
                cnt = cnt + jnp.where(si > score, 1.0, jnp.where(si == score, tie, 0.0))
            sel = jnp.where((cnt < n_top) & (score > -BIG / 2), 1.0, 0.0)
        seln = jnp.concatenate([sel - 1.0, jnp.zeros((LANES - n_slc, W), F32)], axis=0).T.astype(BF16)
        for j in range(G4):
            for g in range(N_KV_GROUPS):
                seln_sc[s, j, g] = seln[(j * N_KV_GROUPS + g) * TQ:(j * N_KV_GROUPS + g + 1) * TQ]

    def tiles(n_pairs):
        jq = qi & (G4 - 1)
        elems = range(nb)
        cat = lambda xs: xs[0] if len(xs) == 1 else jnp.concatenate(xs, axis=0)
        rep = lambda t: t if nb == 1 else jnp.concatenate([t] * nb, axis=0)
        parts = [stack_heads(nq_ref[s]) for s in elems]
        qs = [jnp.concatenate(parts[s], axis=0) for s in elems]

        def bias_tiles(ref, first, count, tile0):
            tiles_ = [ref[jnp.maximum(qi - tile0 - (first + u) + 1, 0)] for u in range(count)]
            return rep(tiles_[0] if count == 1 else jnp.concatenate(tiles_, axis=1))

        zmax = None
        for j0 in range(0, n_win, 2):
            nj = min(2, n_win - j0)
            zp = (cat([_dot_nt(qs[s], kw_ref[s, pl.ds(woff + j0 * TQ, nj * TQ), :]) for s in elems])
                  + bias_tiles(wb_ref, j0, nj, w0))
            zw_sc[:, j0 * TQ:(j0 + nj) * TQ] = zp
            for u in range(nj):
                zj = zp[:, u * TQ:(u + 1) * TQ]
                zmax = zj if zmax is None else jnp.maximum(zmax, zj)
        mw = jnp.max(zmax, axis=-1, keepdims=True)
        pw = jnp.exp2(zw_sc[...] - mw).astype(BF16)
        acc_w = cat([_dot(pw[s * HR + g * GR:s * HR + (g + 1) * GR],
                          vw_refs[g][s, pl.ds(woff, n_win * TQ), :])
                     for s in elems for g in range(N_KV_GROUPS)])

        for s in elems:
            for g in range(N_KV_GROUPS):
                seln = seln_sc[s, jq, g]
                for r in range(N_REP):
                    h = g * N_REP + r
                    lhs_sc[s, h * TQ:(h + 1) * TQ, :] = jnp.concatenate([parts[s][h], seln], axis=1)

        zmax = None
        for pi in range(n_pairs):
            k0 = pi * 2 * TQ
            zp = (cat([_dot_nt(lhs_sc[s], jnp.concatenate([ks_ref[s, k0:k0 + 2 * TQ, :],
                                                           et_ref[k0:k0 + 2 * TQ, :]], axis=1))
                       for s in elems])
                  + bias_tiles(bt_ref, 2 * pi, 2, 0))
            z_sc[pi // 2, :, (pi % 2) * 2 * TQ:(pi % 2 + 1) * 2 * TQ] = zp
            for u in range(2):
                zt = zp[:, u * TQ:(u + 1) * TQ]
                zmax = zt if zmax is None else jnp.maximum(zmax, zt)
        ms = jnp.max(zmax, axis=-1, keepdims=True)
        acc = [None] * (nb * N_KV_GROUPS)
        for c in range((n_pairs + 1) // 2):
            width = min(CH, n_pairs * 2 * TQ - c * CH)
            p = jnp.exp2(z_sc[c, :, 0:width] - ms).astype(BF16)
            for s in elems:
                for g in range(N_KV_GROUPS):
                    i = s * N_KV_GROUPS + g
                    d = _dot(p[i * GR:(i + 1) * GR], vs_refs[g][s, c * CH:c * CH + width, :])
                    acc[i] = d if acc[i] is None else acc[i] + d
        acc_s = cat(acc)

        for s in elems:
            sg = jax.nn.sigmoid(small_ref[s])
            a_s = acc_s[s * HR:(s + 1) * HR]
            a_w = acc_w[s * HR:(s + 1) * HR]
            for r in range(N_REP):
                out = (partc_sc[s, jq, r] + gate(sg, r, 1) * normed(a_s, r)
                       + gate(sg, r, 2) * normed(a_w, r))
                yn_ref[s, :, r * LANES:(r + 1) * LANES] = out.astype(yn_ref.dtype)

    for nc in range(1, n_chunks_max + 1):
        @pl.when(qi // G4 == nc - 1)
        def _(nc=nc):
            @pl.when(qi % G4 == 0)
            def _():
                for s in range(nb):
                    group_prep(s, nc)

            tiles(2 * nc)


def _nsa(nq, nsw, ckv, small, bt, wb, cb, ov, emat):
    B, T, _ = nq.shape
    assert T == 2048, "single 128-wide compressed-key tile assumes T == 2048"
    nqt = T // TQ
    n_cmp = (T - CMP_BLOCK) // CMP_STRIDE + 1
    n_slc = T // SEL_BLOCK
    n_top = min(SEL_TOPK, n_slc)
    n_win = wb.shape[0] - 1
    HR = N_HEADS * TQ
    assert nqt % 4 == 0 and n_slc % 8 == 0 and n_slc <= LANES
    nb = NSA_BATCH if B % NSA_BATCH == 0 else 1
    n_chunks_max = nqt // 4
    kv = lambda j: pl.BlockSpec((nb, T, LANES), lambda b, q: (b, 0, j))
    ck = lambda j: pl.BlockSpec((1, nb, T // CMP_STRIDE, LANES), lambda b, q: (j, b, 0, 0))
    kern = functools.partial(_nsa_kernel, nb=nb, n_slc=n_slc, n_top=n_top, n_win=n_win,
                             n_chunks_max=n_chunks_max)
    return pl.pallas_call(
        kern, out_shape=jax.ShapeDtypeStruct((B, T, N_WIDTH), BF16),
        grid=(B // nb, nqt),
        in_specs=[pl.BlockSpec((nb, TQ, N_WIDTH), lambda b, q: (b, q, 0)),
                  pl.BlockSpec((nb, 4 * TQ, N_WIDTH), lambda b, q: (b, q // 4, 0)),
                  kv(0), kv(1), kv(2), kv(3), kv(4), kv(5), ck(0), ck(1),
                  pl.BlockSpec((nb, TQ, LANES), lambda b, q: (b, q, 0)),
                  pl.BlockSpec((nb, 4 * TQ, LANES), lambda b, q: (b, q // 4, 0)),
                  _const_spec(bt.shape), _const_spec(wb.shape),
                  pl.BlockSpec((4, HR, LANES), lambda b, q: (q // 4, 0, 0)),
                  _const_spec(ov.shape), _const_spec(emat.shape)],
        out_specs=pl.BlockSpec((nb, TQ, N_WIDTH), lambda b, q: (b, q, 0)),
        scratch_shapes=[pltpu.VMEM((nb, HR, 2 * LANES), BF16),
                        pltpu.VMEM((n_chunks_max, nb * HR, 4 * TQ), F32),
                        pltpu.VMEM((nb * HR, n_win * TQ), F32),
                        pltpu.VMEM((nb, 4, N_KV_GROUPS, TQ, LANES), BF16),
                        pltpu.VMEM((nb, 4, N_REP, TQ, LANES), F32)],
        name="nsa",
        compiler_params=pltpu.CompilerParams(dimension_semantics=("arbitrary", "arbitrary"),
                                             vmem_limit_bytes=VMEM_LIMIT),
    )(nq, nq, nsw, nsw, nsw, nsw, nsw, nsw, ckv, ckv, small, small, bt, wb, cb, ov, emat)


def _merge_kernel(x_ref, ym_ref, yn_ref, mg_ref, wbm_ref, wbn_ref, wo_ref,
                  wg_ref, wu_ref, wd_ref, gfin_ref, o_ref, *, tf):
    bm = _dot(ym_ref[...], wbm_ref[...])
    bn = _dot(yn_ref[...], wbn_ref[...])
    mixed = (jax.nn.sigmoid(mg_ref[:, :D_MODEL].astype(F32)) * bm
             + jax.nn.sigmoid(mg_ref[:, D_MODEL:].astype(F32)) * bn)
    h = x_ref[...] + _dot(mixed.astype(BF16), wo_ref[...])
    rs = lax.rsqrt(jnp.mean(h * h, axis=-1, keepdims=True) + RMS_EPS)
    hb = h.astype(BF16)
    acc = jnp.zeros(h.shape, F32)
    for j in range(D_FF // tf):
        gg = _dot(hb, wg_ref[:, j * tf:(j + 1) * tf]) * rs
        uu = _dot(hb, wu_ref[:, j * tf:(j + 1) * tf]) * rs
        act = (gg * jax.nn.sigmoid(gg) * uu).astype(BF16)
        acc = acc + _dot(act, wd_ref[j * tf:(j + 1) * tf, :])
    h2 = h + acc
    o_ref[...] = h2 * lax.rsqrt(jnp.mean(h2 * h2, axis=-1, keepdims=True) + RMS_EPS) * gfin_ref[...]


def _merge(x2, ym, yn, mg, wbm, wbn, wo, wg, wu, wd, gfin, tm=512, tf=256):
    n = x2.shape[0]
    assert n % tm == 0 and D_FF % tf == 0
    row = lambda width: pl.BlockSpec((tm, width), lambda i: (i, 0))
    return pl.pallas_call(
        functools.partial(_merge_kernel, tf=tf),
        out_shape=jax.ShapeDtypeStruct((n, D_MODEL), F32), grid=(n // tm,),
        in_specs=[row(D_MODEL), row(M_WIDTH), row(N_WIDTH), row(N_BRANCH * D_MODEL),
                  _const_spec(wbm.shape), _const_spec(wbn.shape), _const_spec(wo.shape),
                  _const_spec(wg.shape), _const_spec(wu.shape),
                  _const_spec(wd.shape), _const_spec(gfin.shape)],
        out_specs=row(D_MODEL), name="merge_ffn",
        compiler_params=pltpu.CompilerParams(dimension_semantics=("arbitrary",),
                                             vmem_limit_bytes=VMEM_LIMIT),
    )(x2, ym, yn, mg, wbm, wbn, wo, wg, wu, wd, gfin)


def _nsa_constants(T):
    n_cmp = (T - CMP_BLOCK) // CMP_STRIDE + 1
    n_slc = T // SEL_BLOCK
    cs = np.arange(n_cmp) * CMP_STRIDE
    ss = np.arange(n_slc) * SEL_BLOCK
    ov = np.clip(np.minimum(cs[:, None] + CMP_BLOCK, ss[None, :] + SEL_BLOCK)
                 - np.maximum(cs[:, None], ss[None, :]), 0, None) / CMP_STRIDE
    ovt = np.zeros((n_slc, LANES), np.float32)
    ovt[:, :n_cmp] = ov.T
    et = (np.arange(T)[:, None] // SEL_BLOCK == np.arange(LANES)[None, :]).astype(np.float32) * (-NEG)
    return jnp.asarray(ovt), jnp.asarray(et, dtype=BF16)


def _compress_weights(pe_cmp, w_cmp1, w_cmp2):
    half = CMP_BLOCK // 2
    dh, hid = N_HEAD_DIM, CMP_HIDDEN
    eye = jnp.eye(N_KV_GROUPS, dtype=F32)
    w1 = w_cmp1.reshape(2, CMP_BLOCK, dh, hid)
    w1bd = jnp.einsum("jpdn,gk->jpgdkn", w1, eye).reshape(2, CMP_BLOCK, N_KV_GROUPS * dh, N_KV_GROUPS * hid)
    w2bd = jnp.einsum("jnd,gk->jgnkd", w_cmp2, eye).reshape(2, N_KV_GROUPS * hid, N_KV_GROUPS * dh)
    pe2 = jnp.tile(pe_cmp, (1, 1, N_KV_GROUPS))
    return pe2, w1bd[:, :half].astype(BF16), w1bd[:, half:].astype(BF16), w2bd.astype(BF16)


def kernel(x, g_norm_mix, w_in, b_in, b_fgate, conv_qk, g_mlstm_head, pe_cmp, w_cmp1, w_cmp2,
           rel_bias, w_branch, w_out, g_norm_ffn, w_gate, w_up, w_down, g_final):
    B, T, D = x.shape
    assert D == D_MODEL and w_in.shape[0] == 1, "one residual block (DEPTH == 1)"
    N = B * T
    x2 = x.reshape(N, D)

    idx, scale = _inproj_perm()
    w_r = _gather_cols(w_in[0] * g_norm_mix[0][:, None], idx, scale, BF16)
    b_r = _gather_cols(b_in[0].reshape(1, -1), idx, scale, F32)
    post = np.concatenate([np.ones(M_WIDTH, np.float32), np.full(M_WIDTH, M_HEAD_DIM ** -0.5, np.float32)])
    cw = jnp.zeros((8, 2 * M_WIDTH), F32).at[:CONV_WIDTH].set(conv_qk[0]).at[CONV_WIDTH].set(jnp.asarray(post))
    mqk, mv, mo, nq, cin, nsw, mg, small = _inproj(x2, w_r, b_r)

    bf = jnp.zeros((1, LANES), F32).at[0, :M_HEADS].set(b_fgate[0])
    ltri = jnp.asarray(np.tril(np.ones((M_CHUNK, M_CHUNK), np.float32)))
    r3 = lambda a: a.reshape(B, T, a.shape[-1])
    ym = _mlstm(r3(mqk), r3(mv), r3(mo), r3(small), cw, bf, g_mlstm_head[0].reshape(1, M_WIDTH), ltri)

    pe2, w1t, w1b, w2 = _compress_weights(pe_cmp[0], w_cmp1[0], w_cmp2[0])
    ckv = _compress(cin.reshape(2, B, T, LANES), pe2, w1t, w1b, w2)
    tbl = rel_bias.astype(F32).T.reshape(-1)
    nqt = T // TQ
    bt = _bias_tiles(tbl, nqt + 1, "tok")
    wb = _bias_tiles(tbl, min(WINDOW // TQ + 1, nqt) + 1, "win")
    cb = _bias_tiles(tbl, nqt, "cmp", n_cmp=(T - CMP_BLOCK) // CMP_STRIDE + 1)
    ov, emat = _nsa_constants(T)
    yn = _nsa(r3(nq), r3(nsw), ckv, r3(small), bt, wb, cb, ov, emat)

    wbm = w_branch[0, 0].astype(BF16)
    wbn = jnp.concatenate(
        [w_branch[0, 1, (g * N_REP + r) * N_HEAD_DIM:(g * N_REP + r + 1) * N_HEAD_DIM]
         for r in range(N_REP) for g in range(N_KV_GROUPS)], axis=0).astype(BF16)
    out = _merge(x2, ym.reshape(N, M_WIDTH), yn.reshape(N, N_WIDTH), mg, wbm, wbn,
                 w_out[0].astype(BF16), (w_gate[0] * g_norm_ffn[0][:, None]).astype(BF16),
                 (w_up[0] * g_norm_ffn[0][:, None]).astype(BF16), w_down[0].astype(BF16),
                 g_final.reshape(1, D))
    return out.reshape(B, T, D)
```

```python
import functools

import numpy as np
import jax
import jax.numpy as jnp
from jax import lax
from jax.experimental import pallas as pl
from jax.experimental.pallas import tpu as pltpu

F32 = jnp.float32
BF16 = jnp.bfloat16
HI = lax.Precision.HIGHEST

D_MODEL = 1024
M_HEADS = 4
M_HEAD_DIM = 128
M_WIDTH = M_HEADS * M_HEAD_DIM
M_CHUNK = 128
CONV_WIDTH = 4
N_HEADS = 8
N_KV_GROUPS = 2
N_REP = N_HEADS // N_KV_GROUPS
N_HEAD_DIM = 64
N_WIDTH = N_HEADS * N_HEAD_DIM
N_KV_WIDTH = N_KV_GROUPS * N_HEAD_DIM
CMP_BLOCK = 32
CMP_STRIDE = 16
CMP_HIDDEN = 2 * N_HEAD_DIM
SEL_BLOCK = 64
SEL_TOPK = 16
WINDOW = 512
REL_BUCKETS = 32
REL_MAX_DIST = 1024
N_BRANCH = 2
D_FF = 2816
RMS_EPS = 1e-6
BIG = 1e9
NEG = -1e30

LANES = 128
TQ = 128
NSA_BATCH = 1
VMEM_LIMIT = 56 * 1024 * 1024

_OFF_MQ, _OFF_MK, _OFF_MV, _OFF_MO = 0, 512, 1024, 1536
_OFF_MI, _OFF_MF, _OFF_NQ, _OFF_NKV = 2048, 2052, 2056, 2568
_OFF_NGATE, _OFF_MERGE, _D_IN = 3336, 3360, 5408
_SMALL_NGATE = 8

_SEGS = (("mqk", 1024), ("mv", 512), ("mo", 512), ("nq", 512), ("kc", 128),
         ("vc", 128), ("nsw", 512), ("mg", 2048), ("small", 128))
_D_IN_PAD = sum(w for _, w in _SEGS)
LOG2E = 1.4426950408889634


def _dot(a, b, **kw):
    return jnp.dot(a, b, preferred_element_type=F32, **kw)


def _dot_nt(a, b):
    return lax.dot_general(a, b, (((1,), (1,)), ((), ())), preferred_element_type=F32)


def _dot_tn(a, b):
    return lax.dot_general(a, b, (((0,), (0,)), ((), ())), preferred_element_type=F32)


def _const_spec(shape):
    nd = len(shape)
    return pl.BlockSpec(shape, lambda *_: (0,) * nd, pipeline_mode=pl.Buffered(1))


def _bucket_thresholds():
    max_exact = REL_BUCKETS // 2
    assert REL_MAX_DIST == 64 * max_exact and REL_BUCKETS - max_exact == 16
    thr = []
    for k in range(1, REL_BUCKETS - max_exact):
        t = max_exact
        while t ** 8 < (max_exact ** 8) * (2 ** (3 * k)):
            t += 1
        thr.append(t)
    return tuple(thr)


_BUCKET_THR = _bucket_thresholds()


def _inproj_perm():
    idx = np.zeros((_D_IN_PAD,), np.int32)
    scale = np.zeros((_D_IN_PAD,), np.float32)
    pos = 0

    def put(cols, s=1.0):
        nonlocal pos
        n = len(cols)
        idx[pos:pos + n] = cols
        scale[pos:pos + n] = s
        pos += n

    put(np.arange(_OFF_MQ, _OFF_MV))
    put(np.arange(_OFF_MV, _OFF_MO))
    put(np.arange(_OFF_MO, _OFF_MI))
    nq = np.zeros((N_WIDTH,), np.int32)
    for r in range(N_REP):
        for g in range(N_KV_GROUPS):
            for d in range(N_HEAD_DIM):
                nq[r * 128 + g * 64 + d] = _OFF_NQ + (g * N_REP + r) * N_HEAD_DIM + d
    put(nq, N_HEAD_DIM ** -0.5 * LOG2E)
    kv = lambda j, g: np.arange(_OFF_NKV + (j * N_KV_GROUPS + g) * N_HEAD_DIM,
                                _OFF_NKV + (j * N_KV_GROUPS + g + 1) * N_HEAD_DIM)
    put(np.arange(_OFF_NKV, _OFF_NKV + 256))
    put(np.concatenate([kv(2, 0), kv(2, 1)]))
    put(np.concatenate([kv(4, 0), kv(4, 1)]))
    put(np.concatenate([kv(3, 0), kv(3, 1)]))
    put(np.concatenate([kv(5, 0), kv(5, 1)]))
    put(np.arange(_OFF_MERGE, _D_IN))
    put(np.arange(_OFF_MI, _OFF_MI + 8))
    put(np.arange(_OFF_NGATE, _OFF_NGATE + 24))
    pos += LANES - 32
    assert pos == _D_IN_PAD
    return idx, scale


def _gather_cols(a, idx, scale, dtype):
    pieces = []
    start = 0
    n = len(idx)
    for c in range(1, n + 1):
        same = c < n and scale[c] == scale[start] and (scale[c] == 0.0 or idx[c] == idx[c - 1] + 1)
        if not same:
            if scale[start] == 0.0:
                piece = jnp.zeros((a.shape[0], c - start), dtype)
            else:
                piece = a[:, int(idx[start]):int(idx[start]) + (c - start)]
                if scale[start] != 1.0:
                    piece = piece * float(scale[start])
            pieces.append(piece.astype(dtype))
            start = c
    return jnp.concatenate(pieces, axis=1)


def _col_runs(idx, scale):
    runs = []
    start = 0
    n = len(idx)
    for c in range(1, n + 1):
        same = c < n and scale[c] == scale[start] and (scale[c] == 0.0 or idx[c] == idx[c - 1] + 1)
        if not same:
            runs.append((start, int(idx[start]), c - start, float(scale[start])))
            start = c
    return runs


def _wprep_kernel(w_ref, g_ref, o_ref, *, runs):
    g = g_ref[...]
    for dst, src, n, scale in runs:
        if scale == 0.0:
            o_ref[:, dst:dst + n] = jnp.zeros((o_ref.shape[0], n), o_ref.dtype)
        else:
            o_ref[:, dst:dst + n] = (w_ref[:, src:src + n] * (g * scale)).astype(o_ref.dtype)


def _wprep(w, g, idx, scale, tr=256):
    rows, cols = w.shape
    assert rows % tr == 0
    return pl.pallas_call(
        functools.partial(_wprep_kernel, runs=_col_runs(idx, scale)),
        out_shape=jax.ShapeDtypeStruct((rows, len(idx)), BF16), grid=(rows // tr,),
        in_specs=[pl.BlockSpec((tr, cols), lambda i: (i, 0)), pl.BlockSpec((tr, 1), lambda i: (i, 0))],
        out_specs=pl.BlockSpec((tr, len(idx)), lambda i: (i, 0)), name="wprep",
        compiler_params=pltpu.CompilerParams(dimension_semantics=("arbitrary",),
                                             vmem_limit_bytes=VMEM_LIMIT),
    )(w, g)


def _inproj_kernel(x_ref, w_ref, b_ref, mqk_ref, mv_ref, mo_ref, nq_ref,
                   cin_ref, nsw_ref, mg_ref, small_ref):
    x = x_ref[...]
    rs = lax.rsqrt(jnp.mean(x * x, axis=-1, keepdims=True) + RMS_EPS)
    xb = x.astype(BF16)

    def seg(a, n):
        return _dot(xb, w_ref[:, a:a + n]) * rs + b_ref[:, a:a + n]

    off = {}
    pos = 0
    for name, w in _SEGS:
        off[name] = pos
        pos += w

    mqk_ref[...] = seg(off["mqk"], 1024).astype(BF16)
    mv_ref[...] = seg(off["mv"], 512).astype(BF16)
    mo_ref[...] = seg(off["mo"], 512).astype(BF16)
    nq_ref[...] = seg(off["nq"], 512).astype(BF16)
    cc = seg(off["kc"], 2 * LANES)
    cin_ref[0] = cc[:, :LANES]
    cin_ref[1] = cc[:, LANES:]
    nsw_ref[:, 0:2 * LANES] = seg(off["nsw"], 2 * LANES).astype(BF16)
    vv = seg(off["nsw"] + 2 * LANES, 2 * LANES)
    lo = lax.broadcasted_iota(jnp.int32, (x.shape[0], LANES), 1) < N_HEAD_DIM
    for j in range(2):
        vj = vv[:, j * LANES:(j + 1) * LANES]
        nsw_ref[:, (2 + 2 * j) * LANES:(3 + 2 * j) * LANES] = jnp.where(lo, vj, 1.0).astype(BF16)
        nsw_ref[:, (3 + 2 * j) * LANES:(4 + 2 * j) * LANES] = jnp.where(lo, 1.0, vj).astype(BF16)
    mg_ref[...] = seg(off["mg"], 2048).astype(BF16)
    small_ref[...] = seg(off["small"], 128)


def _inproj(x2, w, b, tm=512):
    n = x2.shape[0]
    assert n % tm == 0
    row = lambda width: pl.BlockSpec((tm, width), lambda i: (i, 0))
    out_shape = (
        jax.ShapeDtypeStruct((n, 1024), BF16), jax.ShapeDtypeStruct((n, 512), BF16),
        jax.ShapeDtypeStruct((n, 512), BF16), jax.ShapeDtypeStruct((n, 512), BF16),
        jax.ShapeDtypeStruct((2, n, 128), F32), jax.ShapeDtypeStruct((n, 768), BF16),
        jax.ShapeDtypeStruct((n, 2048), BF16), jax.ShapeDtypeStruct((n, 128), F32))
    out_specs = (row(1024), row(512), row(512), row(512),
                 pl.BlockSpec((2, tm, 128), lambda i: (0, i, 0)), row(768), row(2048), row(128))
    return pl.pallas_call(
        _inproj_kernel, out_shape=out_shape, grid=(n // tm,),
        in_specs=[row(D_MODEL), _const_spec((D_MODEL, _D_IN_PAD)), _const_spec((1, _D_IN_PAD))],
        out_specs=out_specs, name="inproj",
        compiler_params=pltpu.CompilerParams(dimension_semantics=("arbitrary",),
                                             vmem_limit_bytes=VMEM_LIMIT),
    )(x2, w, b)


def _mlstm_kernel(mqk_ref, mv_ref, mo_ref, small_ref, cw_ref, bf_ref, gh_ref, ltri_ref,
                  ym_ref, ebuf, c_st, m_st, g_bc, g_dt, g_a, g_mrow, g_row):
    L = M_CHUNK
    c = pl.program_id(1)
    n_chunks = g_bc.shape[0]
    rows = lax.broadcasted_iota(jnp.int32, (L, LANES), 0)

    @pl.when(c == 0)
    def _():
        ebuf[0:8, :] = jnp.zeros((8, 2 * M_WIDTH), F32)
        c_st[...] = jnp.zeros_like(c_st)
        m_st[...] = jnp.zeros_like(m_st)
        for cc in range(n_chunks):
            ig = small_ref[0, cc * L:(cc + 1) * L, :]
            fg = pltpu.roll(ig, LANES - M_HEADS, 1) + bf_ref[...]
            lf = jnp.minimum(fg, 0.0) - jnp.log1p(jnp.exp(-jnp.abs(fg)))
            bc = jnp.dot(ltri_ref[...], lf, precision=HI, preferred_element_type=F32)
            blast = bc[L - 1:L, :]
            a_all = blast - bc + ig
            d = ig - bc
            cm = d
            sh = 1
            while sh < L:
                cm = jnp.maximum(cm, jnp.where(rows >= sh, pltpu.roll(cm, sh, 0), -jnp.inf))
                sh *= 2
            g_bc[cc] = bc
            g_dt[cc] = d.T
            g_a[cc] = a_all
            g_mrow[cc] = bc + cm
            g_row[cc, 0:1, :] = blast
            g_row[cc, 1:2, :] = jnp.max(a_all, axis=0, keepdims=True)

    ebuf[8:8 + L, :] = mqk_ref[0].astype(F32)
    ext = ebuf[...]
    conv = cw_ref[CONV_WIDTH - 1:CONV_WIDTH, :] * ext[8:, :]
    for k in range(1, CONV_WIDTH):
        conv = conv + cw_ref[CONV_WIDTH - 1 - k:CONV_WIDTH - k, :] * pltpu.roll(ext, k, 0)[8:, :]
    ebuf[0:8, :] = ebuf[L:L + 8, :]
    qk = (conv * jax.nn.sigmoid(conv) * cw_ref[CONV_WIDTH:CONV_WIDTH + 1, :]).astype(BF16)

    bc = g_bc[c]
    dt = g_dt[c]
    a_all = g_a[c]
    blast = g_row[c, 0:1, :]
    m_prev = m_st[0:1, :]
    m_new = jnp.maximum(blast + m_prev, g_row[c, 1:2, :])
    decay = jnp.exp(blast + m_prev - m_new)
    w_all = jnp.exp(a_all - m_new)
    log_inter = bc + m_prev
    m_i_all = jnp.maximum(g_mrow[c], log_inter)
    s_inter_all = jnp.exp(log_inter - m_i_all)
    emi_all = jnp.exp(-m_i_all)
    u_all = bc - m_i_all

    row = lax.broadcasted_iota(jnp.int32, (L, L), 0)
    col = lax.broadcasted_iota(jnp.int32, (L, L), 1)
    causal = row >= col

    ones = jnp.ones((L, M_HEAD_DIM), BF16)
    for h in range(M_HEADS):
        sl = slice(h * M_HEAD_DIM, (h + 1) * M_HEAD_DIM)
        qb = qk[:, sl]
        kb = qk[:, M_WIDTH + h * M_HEAD_DIM:M_WIDTH + (h + 1) * M_HEAD_DIM]
        v = mv_ref[0, :, sl]
        cn_prev = c_st[h]

        logp = jnp.where(causal, u_all[:, h:h + 1] + dt[h:h + 1, :], -jnp.inf)
        p = jnp.exp(logp) * _dot_nt(qb, kb)
        nd = (_dot(p.astype(BF16), jnp.concatenate([v, ones], axis=1))
              + s_inter_all[:, h:h + 1] * _dot(qb, cn_prev.astype(BF16)))
        hh = nd[:, :M_HEAD_DIM] / jnp.maximum(jnp.abs(nd[:, M_HEAD_DIM:]), emi_all[:, h:h + 1])
        hh = hh * lax.rsqrt(jnp.mean(hh * hh, axis=-1, keepdims=True) + RMS_EPS) * gh_ref[:, sl]
        ym_ref[0, :, sl] = (jax.nn.sigmoid(mo_ref[0, :, sl].astype(F32)) * hh).astype(ym_ref.dtype)

        w_col = w_all[:, h:h + 1]
        dec = decay[:, h:h + 1]
        vw = jnp.concatenate([v.astype(F32) * w_col, jnp.broadcast_to(w_col, (L, M_HEAD_DIM))], axis=1)
        c_st[h] = dec * cn_prev + _dot_tn(kb, vw.astype(BF16))

    m_st[0:1, :] = m_new


def _mlstm(mqk, mv, mo, small, cw, bf, gh, ltri):
    B, T, _ = mqk.shape
    L = M_CHUNK
    assert T % L == 0
    nc = T // L
    blk = lambda w: pl.BlockSpec((1, L, w), lambda b, c: (b, c, 0))
    gate = lambda: pltpu.VMEM((nc, L, LANES), F32)
    return pl.pallas_call(
        _mlstm_kernel, out_shape=jax.ShapeDtypeStruct((B, T, M_WIDTH), BF16),
        grid=(B, nc),
        in_specs=[blk(2 * M_WIDTH), blk(M_WIDTH), blk(M_WIDTH),
                  pl.BlockSpec((1, T, LANES), lambda b, c: (b, 0, 0)),
                  _const_spec((8, 2 * M_WIDTH)), _const_spec((1, LANES)),
                  _const_spec((1, M_WIDTH)), _const_spec((L, L))],
        out_specs=blk(M_WIDTH),
        scratch_shapes=[pltpu.VMEM((L + 8, 2 * M_WIDTH), F32),
                        pltpu.VMEM((M_HEADS, M_HEAD_DIM, 2 * M_HEAD_DIM), F32),
                        pltpu.VMEM((8, LANES), F32),
                        gate(), gate(), gate(), gate(), pltpu.VMEM((nc, 8, LANES), F32)],
        name="mlstm",
        compiler_params=pltpu.CompilerParams(dimension_semantics=("arbitrary", "arbitrary")),
    )(mqk, mv, mo, small, cw, bf, gh, ltri)


def _compress_kernel(x_ref, pe_ref, w1t_ref, w1b_ref, w2_ref, o_ref, *, nch, nbt):
    half = CMP_BLOCK // 2
    rows = nbt * nch
    top = jnp.zeros((rows, 2 * CMP_HIDDEN), F32)
    bot = jnp.zeros((rows, 2 * CMP_HIDDEN), F32)
    for p in range(half):
        xp = jnp.concatenate([x_ref[0, e, pl.ds(p, nch, stride=CMP_STRIDE), :] for e in range(nbt)], axis=0)
        top = top + _dot((xp + pe_ref[0, p:p + 1, :]).astype(BF16), w1t_ref[0, p])
        bot = bot + _dot((xp + pe_ref[0, half + p:half + p + 1, :]).astype(BF16), w1b_ref[0, p])
    hid = top + pltpu.roll(bot, rows - 1, 0)
    act = hid * jax.nn.sigmoid(hid)
    out = _dot(act.astype(BF16), w2_ref[0]).astype(o_ref.dtype)
    for e in range(nbt):
        o_ref[0, e] = out[e * nch:(e + 1) * nch]


def _compress(cin, pe2, w1t, w1b, w2):
    _, B, T, _ = cin.shape
    nch = T // CMP_STRIDE
    nbt = 4 if B % 4 == 0 else 1
    sel = lambda *shape: pl.BlockSpec((1,) + shape, lambda j, b: (j,) + (0,) * len(shape))
    return pl.pallas_call(
        functools.partial(_compress_kernel, nch=nch, nbt=nbt),
        out_shape=jax.ShapeDtypeStruct((2, B, nch, LANES), BF16),
        grid=(2, B // nbt),
        in_specs=[pl.BlockSpec((1, nbt, T, LANES), lambda j, b: (j, b, 0, 0)),
                  sel(CMP_BLOCK, LANES), sel(CMP_BLOCK // 2, LANES, 2 * CMP_HIDDEN),
                  sel(CMP_BLOCK // 2, LANES, 2 * CMP_HIDDEN), sel(2 * CMP_HIDDEN, LANES)],
        out_specs=pl.BlockSpec((1, nbt, nch, LANES), lambda j, b: (j, b, 0, 0)),
        name="compress",
        compiler_params=pltpu.CompilerParams(dimension_semantics=("arbitrary", "arbitrary")),
    )(cin, pe2, w1t, w1b, w2)


def _bias_kernel(tbl_ref, o_ref, *, kind, n_cmp):
    pid = pl.program_id(0)
    toeplitz = kind != "cmp"
    if toeplitz:
        k = lax.broadcasted_iota(jnp.int32, (8, 2 * LANES), 1)
        dist = (pid - 1) * TQ + jnp.where(k < LANES, -k, 2 * LANES - k)
    else:
        al = lax.broadcasted_iota(jnp.int32, (CMP_STRIDE, 2 * LANES), 0)
        m = lax.broadcasted_iota(jnp.int32, (CMP_STRIDE, 2 * LANES), 1)
        dist = pid * TQ - (CMP_BLOCK - 1) + al - CMP_STRIDE * jnp.where(m < LANES, m, m - 2 * LANES)
    n = jnp.maximum(dist, 0)
    cnt = jnp.zeros_like(n)
    for t in _BUCKET_THR:
        cnt = cnt + jnp.where(n >= t, 1, 0)
    bucket = jnp.where(n < REL_BUCKETS // 2, n, REL_BUCKETS // 2 + cnt)
    if kind == "tok":
        madd = jnp.where((dist >= 0) & (pid > 0), 0.0, NEG)
    elif kind == "win":
        madd = jnp.where((dist >= 0) & (dist < WINDOW) & (pid > 0), 0.0, NEG)
    else:
        madd = jnp.where(dist >= 0, 0.0, NEG)
        valid_c = lax.broadcasted_iota(jnp.int32, (TQ, LANES), 1) < n_cmp
    for h in range(N_HEADS):
        val = jnp.zeros(dist.shape, F32)
        for bb in range(REL_BUCKETS):
            val = jnp.where(bucket == bb, tbl_ref[h * REL_BUCKETS + bb], val)
        val = val * LOG2E + madd
        if toeplitz:
            g = jnp.broadcast_to(val[0:1, :], (TQ, 2 * LANES))
            val = pltpu.roll(g, 0, 1, stride=1, stride_axis=0)[:, :LANES]
        else:
            bands = [val[:, :LANES]] + [pltpu.roll(val, ah, 1)[:, :LANES] for ah in range(1, TQ // CMP_STRIDE)]
            val = jnp.where(valid_c, jnp.concatenate(bands, axis=0), NEG)
        o_ref[0, h * TQ:(h + 1) * TQ, :] = val


def _bias_tiles(tbl, n_tiles, kind, n_cmp=0):
    return pl.pallas_call(
        functools.partial(_bias_kernel, kind=kind, n_cmp=n_cmp),
        out_shape=jax.ShapeDtypeStruct((n_tiles, N_HEADS * TQ, LANES), F32),
        grid=(n_tiles,),
        in_specs=[pl.BlockSpec(memory_space=pltpu.SMEM)],
        out_specs=pl.BlockSpec((1, N_HEADS * TQ, LANES), lambda i: (i, 0, 0)),
        name="bias_" + kind,
        compiler_params=pltpu.CompilerParams(dimension_semantics=("arbitrary",)),
    )(tbl)


def _nsa_kernel(nq_ref, nq4_ref, ks_ref, kw_ref, vs0_ref, vs1_ref, vw0_ref, vw1_ref, kc_ref, vc_ref,
                small_ref, small4_ref, bt_ref, wb_ref, cb4_ref, ovt_ref, et_ref, yn_ref,
                lhs_sc, z_sc, zw_sc, seln_sc, partc_sc, *, nb, n_slc, n_top, n_win, n_chunks_max):
    qi = pl.program_id(1)
    HR = N_HEADS * TQ
    GR = N_REP * TQ
    CH = 4 * TQ
    G4 = 4
    vs_refs = (vs0_ref, vs1_ref)
    vw_refs = (vw0_ref, vw1_ref)
    lane = lax.broadcasted_iota(jnp.int32, (TQ, LANES), 1)
    lo = lane < N_HEAD_DIM

    def stack_heads(q_all):
        zero = jnp.zeros((TQ, LANES), BF16)
        parts = []
        for g in range(N_KV_GROUPS):
            for r in range(N_REP):
                qr = q_all[:, r * LANES:(r + 1) * LANES]
                parts.append(jnp.where(lo if g == 0 else jnp.logical_not(lo), qr, zero))
        return parts

    def pair(o, r):
        return jnp.where(lo, o[r * TQ:(r + 1) * TQ], o[(N_REP + r) * TQ:(N_REP + r + 1) * TQ])

    def normed(acc, r):
        den = jnp.where(lo, acc[(N_REP + r) * TQ:(N_REP + r + 1) * TQ], acc[r * TQ:(r + 1) * TQ])
        return pair(acc, r) / pltpu.roll(den, N_HEAD_DIM, 1)

    def gate(sg, r, ci):
        c0 = _SMALL_NGATE + r * 3 + ci
        c1 = _SMALL_NGATE + (N_REP + r) * 3 + ci
        return jnp.where(lo, sg[:, c0:c0 + 1], sg[:, c1:c1 + 1])

    w0 = jnp.maximum(qi - (n_win - 1), 0)
    woff = pl.multiple_of(w0 * TQ, TQ)

    def group_prep(s, n_chunks):
        qs4 = jnp.concatenate(
            [p for j in range(G4) for p in stack_heads(nq4_ref[s, j * TQ:(j + 1) * TQ, :])], axis=0)
        rows = lax.broadcasted_iota(jnp.int32, (G4 * HR, 1), 0)
        t_rows = (qi + jnp.right_shift(rows, 10)) * TQ + (rows & (TQ - 1))

        z = _dot_nt(qs4, kc_ref[0, s]) + cb4_ref[...].reshape(G4 * HR, LANES)
        e = jnp.exp2(z - jnp.max(z, axis=-1, keepdims=True))
        l = _dot(e.astype(BF16), jnp.ones((LANES, LANES), BF16))
        p_c = e * jnp.where(t_rows >= CMP_BLOCK - 1, 1.0 / jnp.maximum(l, 1e-30), 0.0)
        o_c = _dot(p_c.astype(BF16), vc_ref[0, s])
        sg4 = jax.nn.sigmoid(small4_ref[s])
        for j in range(G4):
            for r in range(N_REP):
                partc_sc[s, j, r] = (gate(sg4[j * TQ:(j + 1) * TQ], r, 0)
                                     * pair(o_c[j * HR:(j + 1) * HR], r))

        W = G4 * N_KV_GROUPS * TQ
        jb = lax.broadcasted_iota(jnp.int32, (n_slc, W), 0)
        col = lax.broadcasted_iota(jnp.int32, (n_slc, W), 1)
        tq = (qi + jnp.right_shift(col, 8)) * TQ + (col & (TQ - 1))
        cur = jnp.right_shift(tq, 6)
        elig = jb <= cur
        if n_chunks * CH <= n_top * SEL_BLOCK:
            sel = jnp.where(elig, 1.0, 0.0)
        else:
            forced = (jb == 0) | (jb == cur) | (jb == cur - 1)
            psums = []
            for j in range(G4):
                for g in range(N_KV_GROUPS):
                    base = j * HR + g * GR
                    ps = p_c[base:base + TQ]
                    for r in range(1, N_REP):
                        ps = ps + p_c[base + r * TQ:base + (r + 1) * TQ]
                    psums.append(ps)
            imp = lax.dot_general(ovt_ref[...], jnp.concatenate(psums, axis=0), (((1,), (1,)), ((), ())),
                                  precision=HI, preferred_element_type=F32)
            score = jnp.where(elig, jnp.where(forced, BIG, imp), -BIG)
            cnt = jnp.zeros((n_slc, W), F32)
            for i in range(n_slc):
                si = score[i:i + 1, :]
                tie = jnp.where(jb > i, 1.0, 0.0)
                cnt = cnt + jnp.where(si > score, 1.0, jnp.where(si == score, tie, 0.0))
            sel = jnp.where((cnt < n_top) & (score > -BIG / 2), 1.0, 0.0)
        seln = jnp.concatenate([sel - 1.0, jnp.zeros((LANES - n_slc, W), F32)], axis=0).T.astype(BF16)
        for j in range(G4):
            for g in range(N_KV_GROUPS):
                seln_sc[s, j, g] = seln[(j * N_KV_GROUPS + g) * TQ:(j * N_KV_GROUPS + g + 1) * TQ]

    def tiles(n_pairs):
        jq = qi & (G4 - 1)
        elems = range(nb)
        cat = lambda xs: xs[0] if len(xs) == 1 else jnp.concatenate(xs, axis=0)
        rep = lambda t: t if nb == 1 else jnp.concatenate([t] * nb, axis=0)
        parts = [stack_heads(nq_ref[s]) for s in elems]
        qs = [jnp.concatenate(parts[s], axis=0) for s in elems]

        def bias_tiles(ref, first, count, tile0):
            tiles_ = [ref[jnp.maximum(qi - tile0 - (first + u) + 1, 0)] for u in range(count)]
            return rep(tiles_[0] if count == 1 else jnp.concatenate(tiles_, axis=1))

        zmax = None
        for j0 in range(0, n_win, 2):
            nj = min(2, n_win - j0)
            zp = (cat([_dot_nt(qs[s], kw_ref[s, pl.ds(woff + j0 * TQ, nj * TQ), :]) for s in elems])
                  + bias_tiles(wb_ref, j0, nj, w0))
            zw_sc[:, j0 * TQ:(j0 + nj) * TQ] = zp
            for u in range(nj):
                zj = zp[:, u * TQ:(u + 1) * TQ]
                zmax = zj if zmax is None else jnp.maximum(zmax, zj)
        mw = jnp.max(zmax, axis=-1, keepdims=True)
        pw = jnp.exp2(zw_sc[...] - mw).astype(BF16)
        acc_w = cat([_dot(pw[s * HR + g * GR:s * HR + (g + 1) * GR],
                          vw_refs[g][s, pl.ds(woff, n_win * TQ), :])
                     for s in elems for g in range(N_KV_GROUPS)])

        for s in elems:
            for g in range(N_KV_GROUPS):
                seln = seln_sc[s, jq, g]
                for r in range(N_REP):
                    h = g * N_REP + r
                    lhs_sc[s, h * TQ:(h + 1) * TQ, :] = jnp.concatenate([parts[s][h], seln], axis=1)

        zmax = None
        for pi in range(n_pairs):
            k0 = pi * 2 * TQ
            zp = (cat([_dot_nt(lhs_sc[s], jnp.concatenate([ks_ref[s, k0:k0 + 2 * TQ, :],
                                                           et_ref[k0:k0 + 2 * TQ, :]], axis=1))
                       for s in elems])
                  + bias_tiles(bt_ref, 2 * pi, 2, 0))
            z_sc[pi // 2, :, (pi % 2) * 2 * TQ:(pi % 2 + 1) * 2 * TQ] = zp
            for u in range(2):
                zt = zp[:, u * TQ:(u + 1) * TQ]
                zmax = zt if zmax is None else jnp.maximum(zmax, zt)
        ms = jnp.max(zmax, axis=-1, keepdims=True)
        acc = [None] * (nb * N_KV_GROUPS)
        for c in range((n_pairs + 1) // 2):
            width = min(CH, n_pairs * 2 * TQ - c * CH)
            p = jnp.exp2(z_sc[c, :, 0:width] - ms).astype(BF16)
            for s in elems:
                for g in range(N_KV_GROUPS):
                    i = s * N_KV_GROUPS + g
                    d = _dot(p[i * GR:(i + 1) * GR], vs_refs[g][s, c * CH:c * CH + width, :])
                    acc[i] = d if acc[i] is None else acc[i] + d
        acc_s = cat(acc)

        for s in elems:
            sg = jax.nn.sigmoid(small_ref[s])
            a_s = acc_s[s * HR:(s + 1) * HR]
            a_w = acc_w[s * HR:(s + 1) * HR]
            for r in range(N_REP):
                out = (partc_sc[s, jq, r] + gate(sg, r, 1) * normed(a_s, r)
                       + gate(sg, r, 2) * normed(a_w, r))
                yn_ref[s, :, r * LANES:(r + 1) * LANES] = out.astype(yn_ref.dtype)

    for nc in range(1, n_chunks_max + 1):
        @pl.when(qi // G4 == nc - 1)
        def _(nc=nc):
            @pl.when(qi % G4 == 0)
            def _():
                for s in range(nb):
                    group_prep(s, nc)

            tiles(2 * nc)


def _nsa(nq, nsw, ckv, small, bt, wb, cb, ov, emat):
    B, T, _ = nq.shape
    assert T == 2048, "single 128-wide compressed-key tile assumes T == 2048"
    nqt = T // TQ
    n_cmp = (T - CMP_BLOCK) // CMP_STRIDE + 1
    n_slc = T // SEL_BLOCK
    n_top = min(SEL_TOPK, n_slc)
    n_win = wb.shape[0] - 1
    HR = N_HEADS * TQ
    assert nqt % 4 == 0 and n_slc % 8 == 0 and n_slc <= LANES
    nb = NSA_BATCH if B % NSA_BATCH == 0 else 1
    n_chunks_max = nqt // 4
    kv = lambda j: pl.BlockSpec((nb, T, LANES), lambda b, q: (b, 0, j))
    ck = lambda j: pl.BlockSpec((1, nb, T // CMP_STRIDE, LANES), lambda b, q: (j, b, 0, 0))
    kern = functools.partial(_nsa_kernel, nb=nb, n_slc=n_slc, n_top=n_top, n_win=n_win,
                             n_chunks_max=n_chunks_max)
    return pl.pallas_call(
        kern, out_shape=jax.ShapeDtypeStruct((B, T, N_WIDTH), BF16),
        grid=(B // nb, nqt),
        in_specs=[pl.BlockSpec((nb, TQ, N_WIDTH), lambda b, q: (b, q, 0)),
                  pl.BlockSpec((nb, 4 * TQ, N_WIDTH), lambda b, q: (b, q // 4, 0)),
                  kv(0), kv(1), kv(2), kv(3), kv(4), kv(5), ck(0), ck(1),
                  pl.BlockSpec((nb, TQ, LANES), lambda b, q: (b, q, 0)),
                  pl.BlockSpec((nb, 4 * TQ, LANES), lambda b, q: (b, q // 4, 0)),
                  _const_spec(bt.shape), _const_spec(wb.shape),
                  pl.BlockSpec((4, HR, LANES), lambda b, q: (q // 4, 0, 0)),
                  _const_spec(ov.shape), _const_spec(emat.shape)],
        out_specs=pl.BlockSpec((nb, TQ, N_WIDTH), lambda b, q: (b, q, 0)),
        scratch_shapes=[pltpu.VMEM((nb, HR, 2 * LANES), BF16),
                        pltpu.VMEM((n_chunks_max, nb * HR, 4 * TQ), F32),
                        pltpu.VMEM((nb * HR, n_win * TQ), F32),
                        pltpu.VMEM((nb, 4, N_KV_GROUPS, TQ, LANES), BF16),
                        pltpu.VMEM((nb, 4, N_REP, TQ, LANES), F32)],
        name="nsa",
        compiler_params=pltpu.CompilerParams(dimension_semantics=("arbitrary", "arbitrary"),
                                             vmem_limit_bytes=VMEM_LIMIT),
    )(nq, nq, nsw, nsw, nsw, nsw, nsw, nsw, ckv, ckv, small, small, bt, wb, cb, ov, emat)


def _merge_kernel(x_ref, ym_ref, yn_ref, mg_ref, wbm_ref, wbn_ref, wo_ref,
                  wg_ref, wu_ref, wd_ref, gfin_ref, o_ref, *, tf):
    bm = _dot(ym_ref[...], wbm_ref[...])
    bn = _dot(yn_ref[...], wbn_ref[...])
    mixed = (jax.nn.sigmoid(mg_ref[:, :D_MODEL].astype(F32)) * bm
             + jax.nn.sigmoid(mg_ref[:, D_MODEL:].astype(F32)) * bn)
    h = x_ref[...] + _dot(mixed.astype(BF16), wo_ref[...])
    rs = lax.rsqrt(jnp.mean(h * h, axis=-1, keepdims=True) + RMS_EPS)
    hb = h.astype(BF16)
    acc = jnp.zeros(h.shape, F32)
    for j in range(D_FF // tf):
        gg = _dot(hb, wg_ref[:, j * tf:(j + 1) * tf]) * rs
        uu = _dot(hb, wu_ref[:, j * tf:(j + 1) * tf]) * rs
        act = (gg * jax.nn.sigmoid(gg) * uu).astype(BF16)
        acc = acc + _dot(act, wd_ref[j * tf:(j + 1) * tf, :])
    h2 = h + acc
    o_ref[...] = h2 * lax.rsqrt(jnp.mean(h2 * h2, axis=-1, keepdims=True) + RMS_EPS) * gfin_ref[...]


def _merge(x2, ym, yn, mg, wbm, wbn, wo, wg, wu, wd, gfin, tm=512, tf=256):
    n = x2.shape[0]
    assert n % tm == 0 and D_FF % tf == 0
    row = lambda width: pl.BlockSpec((tm, width), lambda i: (i, 0))
    return pl.pallas_call(
        functools.partial(_merge_kernel, tf=tf),
        out_shape=jax.ShapeDtypeStruct((n, D_MODEL), F32), grid=(n // tm,),
        in_specs=[row(D_MODEL), row(M_WIDTH), row(N_WIDTH), row(N_BRANCH * D_MODEL),
                  _const_spec(wbm.shape), _const_spec(wbn.shape), _const_spec(wo.shape),
                  _const_spec(wg.shape), _const_spec(wu.shape),
                  _const_spec(wd.shape), _const_spec(gfin.shape)],
        out_specs=row(D_MODEL), name="merge_ffn",
        compiler_params=pltpu.CompilerParams(dimension_semantics=("arbitrary",),
                                             vmem_limit_bytes=VMEM_LIMIT),
    )(x2, ym, yn, mg, wbm, wbn, wo, wg, wu, wd, gfin)


def _nsa_constants(T):
    n_cmp = (T - CMP_BLOCK) // CMP_STRIDE + 1
    n_slc = T // SEL_BLOCK
    cs = np.arange(n_cmp) * CMP_STRIDE
    ss = np.arange(n_slc) * SEL_BLOCK
    ov = np.clip(np.minimum(cs[:, None] + CMP_BLOCK, ss[None, :] + SEL_BLOCK)
                 - np.maximum(cs[:, None], ss[None, :]), 0, None) / CMP_STRIDE
    ovt = np.zeros((n_slc, LANES), np.float32)
    ovt[:, :n_cmp] = ov.T
    et = (np.arange(T)[:, None] // SEL_BLOCK == np.arange(LANES)[None, :]).astype(np.float32) * (-NEG)
    return jnp.asarray(ovt), jnp.asarray(et, dtype=BF16)


def _compress_weights(pe_cmp, w_cmp1, w_cmp2):
    half = CMP_BLOCK // 2
    dh, hid = N_HEAD_DIM, CMP_HIDDEN
    eye = jnp.eye(N_KV_GROUPS, dtype=F32)
    w1 = w_cmp1.reshape(2, CMP_BLOCK, dh, hid)
    w1bd = jnp.einsum("jpdn,gk->jpgdkn", w1, eye).reshape(2, CMP_BLOCK, N_KV_GROUPS * dh, N_KV_GROUPS * hid)
    w2bd = jnp.einsum("jnd,gk->jgnkd", w_cmp2, eye).reshape(2, N_KV_GROUPS * hid, N_KV_GROUPS * dh)
    pe2 = jnp.tile(pe_cmp, (1, 1, N_KV_GROUPS))
    return pe2, w1bd[:, :half].astype(BF16), w1bd[:, half:].astype(BF16), w2bd.astype(BF16)


def kernel(x, g_norm_mix, w_in, b_in, b_fgate, conv_qk, g_mlstm_head, pe_cmp, w_cmp1, w_cmp2,
           rel_bias, w_branch, w_out, g_norm_ffn, w_gate, w_up, w_down, g_final):
    B, T, D = x.shape
    assert D == D_MODEL and w_in.shape[0] == 1, "one residual block (DEPTH == 1)"
    N = B * T
    x2 = x.reshape(N, D)

    idx, scale = _inproj_perm()
    w_r = _wprep(w_in[0], g_norm_mix[0].reshape(D, 1), idx, scale)
    b_r = _gather_cols(b_in[0].reshape(1, -1), idx, scale, F32)
    post = np.concatenate([np.ones(M_WIDTH, np.float32), np.full(M_WIDTH, M_HEAD_DIM ** -0.5, np.float32)])
    cw = jnp.zeros((8, 2 * M_WIDTH), F32).at[:CONV_WIDTH].set(conv_qk[0]).at[CONV_WIDTH].set(jnp.asarray(post))
    mqk, mv, mo, nq, cin, nsw, mg, small = _inproj(x2, w_r, b_r)

    bf = jnp.zeros((1, LANES), F32).at[0, :M_HEADS].set(b_fgate[0])
    ltri = jnp.asarray(np.tril(np.ones((M_CHUNK, M_CHUNK), np.float32)))
    r3 = lambda a: a.reshape(B, T, a.shape[-1])
    ym = _mlstm(r3(mqk), r3(mv), r3(mo), r3(small), cw, bf, g_mlstm_head[0].reshape(1, M_WIDTH), ltri)

    pe2, w1t, w1b, w2 = _compress_weights(pe_cmp[0], w_cmp1[0], w_cmp2[0])
    ckv = _compress(cin.reshape(2, B, T, LANES), pe2, w1t, w1b, w2)
    tbl = rel_bias.astype(F32).T.reshape(-1)
    nqt = T // TQ
    bt = _bias_tiles(tbl, nqt + 1, "tok")
    wb = _bias_tiles(tbl, min(WINDOW // TQ + 1, nqt) + 1, "win")
    cb = _bias_tiles(tbl, nqt, "cmp", n_cmp=(T - CMP_BLOCK) // CMP_STRIDE + 1)
    ov, emat = _nsa_constants(T)
    yn = _nsa(r3(nq), r3(nsw), ckv, r3(small), bt, wb, cb, ov, emat)

    wbm = w_branch[0, 0].astype(BF16)
    wbn = jnp.concatenate(
        [w_branch[0, 1, (g * N_REP + r) * N_HEAD_DIM:(g * N_REP + r + 1) * N_HEAD_DIM]
         for r in range(N_REP) for g in range(N_KV_GROUPS)], axis=0).astype(BF16)
    ident = lambda n: (np.arange(n, dtype=np.int32), np.ones((n,), np.float32))
    g_ffn = g_norm_ffn[0].reshape(D, 1)
    out = _merge(x2, ym.reshape(N, M_WIDTH), yn.reshape(N, N_WIDTH), mg, wbm, wbn,
                 w_out[0].astype(BF16), _wprep(w_gate[0], g_ffn, *ident(D_FF)),
                 _wprep(w_up[0], g_ffn, *ident(D_FF)),
                 _wprep(w_down[0], jnp.ones((D_FF, 1), F32), *ident(D)),
                 g_final.reshape(1, D))
    return out.reshape(B, T, D)
```

```python
import functools

import numpy as np
import jax
import jax.numpy as jnp
from jax import lax
from jax.experimental import pallas as pl
from jax.experimental.pallas import tpu as pltpu

F32 = jnp.float32
BF16 = jnp.bfloat16
HI = lax.Precision.HIGHEST

D_MODEL = 1024
M_HEADS = 4
M_HEAD_DIM = 128
M_WIDTH = M_HEADS * M_HEAD_DIM
M_CHUNK = 128
CONV_WIDTH = 4
N_HEADS = 8
N_KV_GROUPS = 2
N_REP = N_HEADS // N_KV_GROUPS
N_HEAD_DIM = 64
N_WIDTH = N_HEADS * N_HEAD_DIM
N_KV_WIDTH = N_KV_GROUPS * N_HEAD_DIM
CMP_BLOCK = 32
CMP_STRIDE = 16
CMP_HIDDEN = 2 * N_HEAD_DIM
SEL_BLOCK = 64
SEL_TOPK = 16
WINDOW = 512
REL_BUCKETS = 32
REL_MAX_DIST = 1024
N_BRANCH = 2
D_FF = 2816
RMS_EPS = 1e-6
BIG = 1e9
NEG = -1e30

LANES = 128
TQ = 128
NSA_BATCH = 1
VMEM_LIMIT = 56 * 1024 * 1024

_OFF_MQ, _OFF_MK, _OFF_MV, _OFF_MO = 0, 512, 1024, 1536
_OFF_MI, _OFF_MF, _OFF_NQ, _OFF_NKV = 2048, 2052, 2056, 2568
_OFF_NGATE, _OFF_MERGE, _D_IN = 3336, 3360, 5408
_SMALL_NGATE = 8

_SEGS = (("mqk", 1024), ("mv", 512), ("mo", 512), ("nq", 512), ("kc", 128),
         ("vc", 128), ("nsw", 512), ("mg", 2048), ("small", 128))
_D_IN_PAD = sum(w for _, w in _SEGS)
LOG2E = 1.4426950408889634


def _dot(a, b, **kw):
    return jnp.dot(a, b, preferred_element_type=F32, **kw)


def _dot_nt(a, b):
    return lax.dot_general(a, b, (((1,), (1,)), ((), ())), preferred_element_type=F32)


def _dot_tn(a, b):
    return lax.dot_general(a, b, (((0,), (0,)), ((), ())), preferred_element_type=F32)


def _const_spec(shape):
    nd = len(shape)
    return pl.BlockSpec(shape, lambda *_: (0,) * nd, pipeline_mode=pl.Buffered(1))


def _bucket_thresholds():
    max_exact = REL_BUCKETS // 2
    assert REL_MAX_DIST == 64 * max_exact and REL_BUCKETS - max_exact == 16
    thr = []
    for k in range(1, REL_BUCKETS - max_exact):
        t = max_exact
        while t ** 8 < (max_exact ** 8) * (2 ** (3 * k)):
            t += 1
        thr.append(t)
    return tuple(thr)


_BUCKET_THR = _bucket_thresholds()


def _inproj_perm():
    idx = np.zeros((_D_IN_PAD,), np.int32)
    scale = np.zeros((_D_IN_PAD,), np.float32)
    pos = 0

    def put(cols, s=1.0):
        nonlocal pos
        n = len(cols)
        idx[pos:pos + n] = cols
        scale[pos:pos + n] = s
        pos += n

    put(np.arange(_OFF_MQ, _OFF_MV))
    put(np.arange(_OFF_MV, _OFF_MO))
    put(np.arange(_OFF_MO, _OFF_MI))
    nq = np.zeros((N_WIDTH,), np.int32)
    for r in range(N_REP):
        for g in range(N_KV_GROUPS):
            for d in range(N_HEAD_DIM):
                nq[r * 128 + g * 64 + d] = _OFF_NQ + (g * N_REP + r) * N_HEAD_DIM + d
    put(nq, N_HEAD_DIM ** -0.5 * LOG2E)
    kv = lambda j, g: np.arange(_OFF_NKV + (j * N_KV_GROUPS + g) * N_HEAD_DIM,
                                _OFF_NKV + (j * N_KV_GROUPS + g + 1) * N_HEAD_DIM)
    put(np.arange(_OFF_NKV, _OFF_NKV + 256))
    put(np.concatenate([kv(2, 0), kv(2, 1)]))
    put(np.concatenate([kv(4, 0), kv(4, 1)]))
    put(np.concatenate([kv(3, 0), kv(3, 1)]))
    put(np.concatenate([kv(5, 0), kv(5, 1)]))
    put(np.arange(_OFF_MERGE, _D_IN))
    put(np.arange(_OFF_MI, _OFF_MI + 8))
    put(np.arange(_OFF_NGATE, _OFF_NGATE + 24))
    pos += LANES - 32
    assert pos == _D_IN_PAD
    return idx, scale


def _gather_cols(a, idx, scale, dtype):
    pieces = []
    start = 0
    n = len(idx)
    for c in range(1, n + 1):
        same = c < n and scale[c] == scale[start] and (scale[c] == 0.0 or idx[c] == idx[c - 1] + 1)
        if not same:
            if scale[start] == 0.0:
                piece = jnp.zeros((a.shape[0], c - start), dtype)
            else:
                piece = a[:, int(idx[start]):int(idx[start]) + (c - start)]
                if scale[start] != 1.0:
                    piece = piece * float(scale[start])
            pieces.append(piece.astype(dtype))
            start = c
    return jnp.concatenate(pieces, axis=1)


def _col_runs(idx, scale):
    runs = []
    start = 0
    n = len(idx)
    for c in range(1, n + 1):
        same = c < n and scale[c] == scale[start] and (scale[c] == 0.0 or idx[c] == idx[c - 1] + 1)
        if not same:
            runs.append((start, int(idx[start]), c - start, float(scale[start])))
            start = c
    return runs


def _wprep_kernel(w_ref, g_ref, o_ref, *, runs):
    g = g_ref[...]
    for dst, src, n, scale in runs:
        if scale == 0.0:
            o_ref[:, dst:dst + n] = jnp.zeros((o_ref.shape[0], n), o_ref.dtype)
        else:
            o_ref[:, dst:dst + n] = (w_ref[:, src:src + n] * (g * scale)).astype(o_ref.dtype)


def _wprep_t_kernel(wt_ref, g_ref, o_ref, *, runs):
    g = g_ref[...]
    for dst, src, n, scale in runs:
        if scale == 0.0:
            o_ref[dst:dst + n, :] = jnp.zeros((n, o_ref.shape[1]), o_ref.dtype)
        else:
            o_ref[dst:dst + n, :] = (wt_ref[src:src + n, :] * (g * scale)).astype(o_ref.dtype)


def _wprep_t(wt, g, idx, scale, tc=256):
    rows, cols = wt.shape
    assert cols % tc == 0
    return pl.pallas_call(
        functools.partial(_wprep_t_kernel, runs=_col_runs(idx, scale)),
        out_shape=jax.ShapeDtypeStruct((len(idx), cols), BF16), grid=(cols // tc,),
        in_specs=[pl.BlockSpec((rows, tc), lambda i: (0, i)), pl.BlockSpec((1, tc), lambda i: (0, i))],
        out_specs=pl.BlockSpec((len(idx), tc), lambda i: (0, i)), name="wprep_t",
        compiler_params=pltpu.CompilerParams(dimension_semantics=("arbitrary",),
                                             vmem_limit_bytes=VMEM_LIMIT),
    )(wt, g)


def _wprep(w, g, idx, scale, tr=256):
    rows, cols = w.shape
    assert rows % tr == 0
    return pl.pallas_call(
        functools.partial(_wprep_kernel, runs=_col_runs(idx, scale)),
        out_shape=jax.ShapeDtypeStruct((rows, len(idx)), BF16), grid=(rows // tr,),
        in_specs=[pl.BlockSpec((tr, cols), lambda i: (i, 0)), pl.BlockSpec((tr, 1), lambda i: (i, 0))],
        out_specs=pl.BlockSpec((tr, len(idx)), lambda i: (i, 0)), name="wprep",
        compiler_params=pltpu.CompilerParams(dimension_semantics=("arbitrary",),
                                             vmem_limit_bytes=VMEM_LIMIT),
    )(w, g)


def _inproj_kernel(x_ref, w_ref, b_ref, mqk_ref, mv_ref, mo_ref, nq_ref,
                   cin_ref, nsw_ref, mg_ref, small_ref):
    x = x_ref[...]
    rs = lax.rsqrt(jnp.mean(x * x, axis=-1, keepdims=True) + RMS_EPS)
    xb = x.astype(BF16)

    def seg(a, n):
        return _dot_nt(xb, w_ref[a:a + n, :]) * rs + b_ref[:, a:a + n]

    off = {}
    pos = 0
    for name, w in _SEGS:
        off[name] = pos
        pos += w

    mqk_ref[...] = seg(off["mqk"], 1024).astype(BF16)
    mv_ref[...] = seg(off["mv"], 512).astype(BF16)
    mo_ref[...] = seg(off["mo"], 512).astype(BF16)
    nq_ref[...] = seg(off["nq"], 512).astype(BF16)
    cc = seg(off["kc"], 2 * LANES)
    cin_ref[0] = cc[:, :LANES]
    cin_ref[1] = cc[:, LANES:]
    nsw_ref[:, 0:2 * LANES] = seg(off["nsw"], 2 * LANES).astype(BF16)
    vv = seg(off["nsw"] + 2 * LANES, 2 * LANES)
    lo = lax.broadcasted_iota(jnp.int32, (x.shape[0], LANES), 1) < N_HEAD_DIM
    for j in range(2):
        vj = vv[:, j * LANES:(j + 1) * LANES]
        nsw_ref[:, (2 + 2 * j) * LANES:(3 + 2 * j) * LANES] = jnp.where(lo, vj, 1.0).astype(BF16)
        nsw_ref[:, (3 + 2 * j) * LANES:(4 + 2 * j) * LANES] = jnp.where(lo, 1.0, vj).astype(BF16)
    mg_ref[...] = seg(off["mg"], 2048).astype(BF16)
    small_ref[...] = seg(off["small"], 128)


def _inproj(x2, w, b, tm=512):
    n = x2.shape[0]
    assert n % tm == 0
    row = lambda width: pl.BlockSpec((tm, width), lambda i: (i, 0))
    out_shape = (
        jax.ShapeDtypeStruct((n, 1024), BF16), jax.ShapeDtypeStruct((n, 512), BF16),
        jax.ShapeDtypeStruct((n, 512), BF16), jax.ShapeDtypeStruct((n, 512), BF16),
        jax.ShapeDtypeStruct((2, n, 128), F32), jax.ShapeDtypeStruct((n, 768), BF16),
        jax.ShapeDtypeStruct((n, 2048), BF16), jax.ShapeDtypeStruct((n, 128), F32))
    out_specs = (row(1024), row(512), row(512), row(512),
                 pl.BlockSpec((2, tm, 128), lambda i: (0, i, 0)), row(768), row(2048), row(128))
    return pl.pallas_call(
        _inproj_kernel, out_shape=out_shape, grid=(n // tm,),
        in_specs=[row(D_MODEL), _const_spec((_D_IN_PAD, D_MODEL)), _const_spec((1, _D_IN_PAD))],
        out_specs=out_specs, name="inproj",
        compiler_params=pltpu.CompilerParams(dimension_semantics=("arbitrary",),
                                             vmem_limit_bytes=VMEM_LIMIT),
    )(x2, w, b)


def _mlstm_kernel(mqk_ref, mv_ref, mo_ref, small_ref, cw_ref, bf_ref, gh_ref, ltri_ref,
                  ym_ref, ebuf, c_st, m_st, g_bc, g_dt, g_a, g_mrow, g_row):
    L = M_CHUNK
    c = pl.program_id(1)
    n_chunks = g_bc.shape[0]
    rows = lax.broadcasted_iota(jnp.int32, (L, LANES), 0)

    @pl.when(c == 0)
    def _():
        ebuf[0:8, :] = jnp.zeros((8, 2 * M_WIDTH), F32)
        c_st[...] = jnp.zeros_like(c_st)
        m_st[...] = jnp.zeros_like(m_st)
        for cc in range(n_chunks):
            ig = small_ref[0, cc * L:(cc + 1) * L, :]
            fg = pltpu.roll(ig, LANES - M_HEADS, 1) + bf_ref[...]
            lf = jnp.minimum(fg, 0.0) - jnp.log1p(jnp.exp(-jnp.abs(fg)))
            bc = jnp.dot(ltri_ref[...], lf, precision=HI, preferred_element_type=F32)
            blast = bc[L - 1:L, :]
            a_all = blast - bc + ig
            d = ig - bc
            cm = d
            sh = 1
            while sh < L:
                cm = jnp.maximum(cm, jnp.where(rows >= sh, pltpu.roll(cm, sh, 0), -jnp.inf))
                sh *= 2
            g_bc[cc] = bc
            g_dt[cc] = d.T
            g_a[cc] = a_all
            g_mrow[cc] = bc + cm
            g_row[cc, 0:1, :] = blast
            g_row[cc, 1:2, :] = jnp.max(a_all, axis=0, keepdims=True)

    ebuf[8:8 + L, :] = mqk_ref[0].astype(F32)
    ext = ebuf[...]
    conv = cw_ref[CONV_WIDTH - 1:CONV_WIDTH, :] * ext[8:, :]
    for k in range(1, CONV_WIDTH):
        conv = conv + cw_ref[CONV_WIDTH - 1 - k:CONV_WIDTH - k, :] * pltpu.roll(ext, k, 0)[8:, :]
    ebuf[0:8, :] = ebuf[L:L + 8, :]
    qk = (conv * jax.nn.sigmoid(conv) * cw_ref[CONV_WIDTH:CONV_WIDTH + 1, :]).astype(BF16)

    bc = g_bc[c]
    dt = g_dt[c]
    a_all = g_a[c]
    blast = g_row[c, 0:1, :]
    m_prev = m_st[0:1, :]
    m_new = jnp.maximum(blast + m_prev, g_row[c, 1:2, :])
    decay = jnp.exp(blast + m_prev - m_new)
    w_all = jnp.exp(a_all - m_new)
    log_inter = bc + m_prev
    m_i_all = jnp.maximum(g_mrow[c], log_inter)
    s_inter_all = jnp.exp(log_inter - m_i_all)
    emi_all = jnp.exp(-m_i_all)
    u_all = bc - m_i_all

    row = lax.broadcasted_iota(jnp.int32, (L, L), 0)
    col = lax.broadcasted_iota(jnp.int32, (L, L), 1)
    causal = row >= col

    ones = jnp.ones((L, M_HEAD_DIM), BF16)
    for h in range(M_HEADS):
        sl = slice(h * M_HEAD_DIM, (h + 1) * M_HEAD_DIM)
        qb = qk[:, sl]
        kb = qk[:, M_WIDTH + h * M_HEAD_DIM:M_WIDTH + (h + 1) * M_HEAD_DIM]
        v = mv_ref[0, :, sl]
        cn_prev = c_st[h]

        logp = jnp.where(causal, u_all[:, h:h + 1] + dt[h:h + 1, :], -jnp.inf)
        p = jnp.exp(logp) * _dot_nt(qb, kb)
        nd = (_dot(p.astype(BF16), jnp.concatenate([v, ones], axis=1))
              + s_inter_all[:, h:h + 1] * _dot(qb, cn_prev.astype(BF16)))
        hh = nd[:, :M_HEAD_DIM] / jnp.maximum(jnp.abs(nd[:, M_HEAD_DIM:]), emi_all[:, h:h + 1])
        hh = hh * lax.rsqrt(jnp.mean(hh * hh, axis=-1, keepdims=True) + RMS_EPS) * gh_ref[:, sl]
        ym_ref[0, :, sl] = (jax.nn.sigmoid(mo_ref[0, :, sl].astype(F32)) * hh).astype(ym_ref.dtype)

        w_col = w_all[:, h:h + 1]
        dec = decay[:, h:h + 1]
        vw = jnp.concatenate([v.astype(F32) * w_col, jnp.broadcast_to(w_col, (L, M_HEAD_DIM))], axis=1)
        c_st[h] = dec * cn_prev + _dot_tn(kb, vw.astype(BF16))

    m_st[0:1, :] = m_new


def _mlstm(mqk, mv, mo, small, cw, bf, gh, ltri):
    B, T, _ = mqk.shape
    L = M_CHUNK
    assert T % L == 0
    nc = T // L
    blk = lambda w: pl.BlockSpec((1, L, w), lambda b, c: (b, c, 0))
    gate = lambda: pltpu.VMEM((nc, L, LANES), F32)
    return pl.pallas_call(
        _mlstm_kernel, out_shape=jax.ShapeDtypeStruct((B, T, M_WIDTH), BF16),
        grid=(B, nc),
        in_specs=[blk(2 * M_WIDTH), blk(M_WIDTH), blk(M_WIDTH),
                  pl.BlockSpec((1, T, LANES), lambda b, c: (b, 0, 0)),
                  _const_spec((8, 2 * M_WIDTH)), _const_spec((1, LANES)),
                  _const_spec((1, M_WIDTH)), _const_spec((L, L))],
        out_specs=blk(M_WIDTH),
        scratch_shapes=[pltpu.VMEM((L + 8, 2 * M_WIDTH), F32),
                        pltpu.VMEM((M_HEADS, M_HEAD_DIM, 2 * M_HEAD_DIM), F32),
                        pltpu.VMEM((8, LANES), F32),
                        gate(), gate(), gate(), gate(), pltpu.VMEM((nc, 8, LANES), F32)],
        name="mlstm",
        compiler_params=pltpu.CompilerParams(dimension_semantics=("arbitrary", "arbitrary")),
    )(mqk, mv, mo, small, cw, bf, gh, ltri)


def _compress_kernel(x_ref, pe_ref, w1t_ref, w1b_ref, w2_ref, o_ref, *, nch, nbt):
    half = CMP_BLOCK // 2
    rows = nbt * nch
    top = jnp.zeros((rows, 2 * CMP_HIDDEN), F32)
    bot = jnp.zeros((rows, 2 * CMP_HIDDEN), F32)
    for p in range(half):
        xp = jnp.concatenate([x_ref[0, e, pl.ds(p, nch, stride=CMP_STRIDE), :] for e in range(nbt)], axis=0)
        top = top + _dot((xp + pe_ref[0, p:p + 1, :]).astype(BF16), w1t_ref[0, p])
        bot = bot + _dot((xp + pe_ref[0, half + p:half + p + 1, :]).astype(BF16), w1b_ref[0, p])
    hid = top + pltpu.roll(bot, rows - 1, 0)
    act = hid * jax.nn.sigmoid(hid)
    out = _dot(act.astype(BF16), w2_ref[0]).astype(o_ref.dtype)
    for e in range(nbt):
        o_ref[0, e] = out[e * nch:(e + 1) * nch]


def _compress(cin, pe2, w1t, w1b, w2):
    _, B, T, _ = cin.shape
    nch = T // CMP_STRIDE
    nbt = 4 if B % 4 == 0 else 1
    sel = lambda *shape: pl.BlockSpec((1,) + shape, lambda j, b: (j,) + (0,) * len(shape))
    return pl.pallas_call(
        functools.partial(_compress_kernel, nch=nch, nbt=nbt),
        out_shape=jax.ShapeDtypeStruct((2, B, nch, LANES), BF16),
        grid=(2, B // nbt),
        in_specs=[pl.BlockSpec((1, nbt, T, LANES), lambda j, b: (j, b, 0, 0)),
                  sel(CMP_BLOCK, LANES), sel(CMP_BLOCK // 2, LANES, 2 * CMP_HIDDEN),
                  sel(CMP_BLOCK // 2, LANES, 2 * CMP_HIDDEN), sel(2 * CMP_HIDDEN, LANES)],
        out_specs=pl.BlockSpec((1, nbt, nch, LANES), lambda j, b: (j, b, 0, 0)),
        name="compress",
        compiler_params=pltpu.CompilerParams(dimension_semantics=("arbitrary", "arbitrary")),
    )(cin, pe2, w1t, w1b, w2)


def _bias_kernel(tbl_ref, o_ref, *, kind, n_cmp):
    pid = pl.program_id(0)
    toeplitz = kind != "cmp"
    if toeplitz:
        k = lax.broadcasted_iota(jnp.int32, (8, 2 * LANES), 1)
        dist = (pid - 1) * TQ + jnp.where(k < LANES, -k, 2 * LANES - k)
    else:
        al = lax.broadcasted_iota(jnp.int32, (CMP_STRIDE, 2 * LANES), 0)
        m = lax.broadcasted_iota(jnp.int32, (CMP_STRIDE, 2 * LANES), 1)
        dist = pid * TQ - (CMP_BLOCK - 1) + al - CMP_STRIDE * jnp.where(m < LANES, m, m - 2 * LANES)
    n = jnp.maximum(dist, 0)
    cnt = jnp.zeros_like(n)
    for t in _BUCKET_THR:
        cnt = cnt + jnp.where(n >= t, 1, 0)
    bucket = jnp.where(n < REL_BUCKETS // 2, n, REL_BUCKETS // 2 + cnt)
    if kind == "tok":
        madd = jnp.where((dist >= 0) & (pid > 0), 0.0, NEG)
    elif kind == "win":
        madd = jnp.where((dist >= 0) & (dist < WINDOW) & (pid > 0), 0.0, NEG)
    else:
        madd = jnp.where(dist >= 0, 0.0, NEG)
        valid_c = lax.broadcasted_iota(jnp.int32, (TQ, LANES), 1) < n_cmp
    for h in range(N_HEADS):
        val = jnp.zeros(dist.shape, F32)
        for bb in range(REL_BUCKETS):
            val = jnp.where(bucket == bb, tbl_ref[h * REL_BUCKETS + bb], val)
        val = val * LOG2E + madd
        if toeplitz:
            g = jnp.broadcast_to(val[0:1, :], (TQ, 2 * LANES))
            val = pltpu.roll(g, 0, 1, stride=1, stride_axis=0)[:, :LANES]
        else:
            bands = [val[:, :LANES]] + [pltpu.roll(val, ah, 1)[:, :LANES] for ah in range(1, TQ // CMP_STRIDE)]
            val = jnp.where(valid_c, jnp.concatenate(bands, axis=0), NEG)
        o_ref[0, h * TQ:(h + 1) * TQ, :] = val


def _bias_tiles(tbl, n_tiles, kind, n_cmp=0):
    return pl.pallas_call(
        functools.partial(_bias_kernel, kind=kind, n_cmp=n_cmp),
        out_shape=jax.ShapeDtypeStruct((n_tiles, N_HEADS * TQ, LANES), F32),
        grid=(n_tiles,),
        in_specs=[pl.BlockSpec(memory_space=pltpu.SMEM)],
        out_specs=pl.BlockSpec((1, N_HEADS * TQ, LANES), lambda i: (i, 0, 0)),
        name="bias_" + kind,
        compiler_params=pltpu.CompilerParams(dimension_semantics=("arbitrary",)),
    )(tbl)


def _nsa_kernel(nq_ref, nq4_ref, ks_ref, kw_ref, vs0_ref, vs1_ref, vw0_ref, vw1_ref, kc_ref, vc_ref,
                small_ref, small4_ref, bt_ref, wb_ref, cb4_ref, ovt_ref, et_ref, yn_ref,
                lhs_sc, z_sc, zw_sc, seln_sc, partc_sc, *, nb, n_slc, n_top, n_win, n_chunks_max):
    qi = pl.program_id(1)
    HR = N_HEADS * TQ
    GR = N_REP * TQ
    CH = 4 * TQ
    G4 = 4
    vs_refs = (vs0_ref, vs1_ref)
    vw_refs = (vw0_ref, vw1_ref)
    lane = lax.broadcasted_iota(jnp.int32, (TQ, LANES), 1)
    lo = lane < N_HEAD_DIM

    def stack_heads(q_all):
        zero = jnp.zeros((TQ, LANES), BF16)
        parts = []
        for g in range(N_KV_GROUPS):
            for r in range(N_REP):
                qr = q_all[:, r * LANES:(r + 1) * LANES]
                parts.append(jnp.where(lo if g == 0 else jnp.logical_not(lo), qr, zero))
        return parts

    def pair(o, r):
        return jnp.where(lo, o[r * TQ:(r + 1) * TQ], o[(N_REP + r) * TQ:(N_REP + r + 1) * TQ])

    def normed(acc, r):
        den = jnp.where(lo, acc[(N_REP + r) * TQ:(N_REP + r + 1) * TQ], acc[r * TQ:(r + 1) * TQ])
        return pair(acc, r) / pltpu.roll(den, N_HEAD_DIM, 1)

    def gate(sg, r, ci):
        c0 = _SMALL_NGATE + r * 3 + ci
        c1 = _SMALL_NGATE + (N_REP + r) * 3 + ci
        return jnp.where(lo, sg[:, c0:c0 + 1], sg[:, c1:c1 + 1])

    w0 = jnp.maximum(qi - (n_win - 1), 0)
    woff = pl.multiple_of(w0 * TQ, TQ)

    def group_prep(s, n_chunks):
        qs4 = jnp.concatenate(
            [p for j in range(G4) for p in stack_heads(nq4_ref[s, j * TQ:(j + 1) * TQ, :])], axis=0)
        rows = lax.broadcasted_iota(jnp.int32, (G4 * HR, 1), 0)
        t_rows = (qi + jnp.right_shift(rows, 10)) * TQ + (rows & (TQ - 1))

        z = _dot_nt(qs4, kc_ref[0, s]) + cb4_ref[...].reshape(G4 * HR, LANES)
        e = jnp.exp2(z - jnp.max(z, axis=-1, keepdims=True))
        l = _dot(e.astype(BF16), jnp.ones((LANES, LANES), BF16))
        p_c = e * jnp.where(t_rows >= CMP_BLOCK - 1, 1.0 / jnp.maximum(l, 1e-30), 0.0)
        o_c = _dot(p_c.astype(BF16), vc_ref[0, s])
        sg4 = jax.nn.sigmoid(small4_ref[s])
        for j in range(G4):
            for r in range(N_REP):
                partc_sc[s, j, r] = (gate(sg4[j * TQ:(j + 1) * TQ], r, 0)
                                     * pair(o_c[j * HR:(j + 1) * HR], r))

        W = G4 * N_KV_GROUPS * TQ
        jb = lax.broadcasted_iota(jnp.int32, (n_slc, W), 0)
        col = lax.broadcasted_iota(jnp.int32, (n_slc, W), 1)
        tq = (qi + jnp.right_shift(col, 8)) * TQ + (col & (TQ - 1))
        cur = jnp.right_shift(tq, 6)
        elig = jb <= cur
        if n_chunks * CH <= n_top * SEL_BLOCK:
            sel = jnp.where(elig, 1.0, 0.0)
        else:
            forced = (jb == 0) | (jb == cur) | (jb == cur - 1)
            psums = []
            for j in range(G4):
                for g in range(N_KV_GROUPS):
                    base = j * HR + g * GR
                    ps = p_c[base:base + TQ]
                    for r in range(1, N_REP):
                        ps = ps + p_c[base + r * TQ:base + (r + 1) * TQ]
                    psums.append(ps)
            imp = lax.dot_general(ovt_ref[...], jnp.concatenate(psums, axis=0), (((1,), (1,)), ((), ())),
                                  precision=HI, preferred_element_type=F32)
            score = jnp.where(elig, jnp.where(forced, BIG, imp), -BIG)
            cnt = jnp.zeros((n_slc, W), F32)
            for i in range(n_slc):
                si = score[i:i + 1, :]
                tie = jnp.where(jb > i, 1.0, 0.0)
                cnt = cnt + jnp.where(si > score, 1.0, jnp.where(si == score, tie, 0.0))
            sel = jnp.where((cnt < n_top) & (score > -BIG / 2), 1.0, 0.0)
        seln = jnp.concatenate([sel - 1.0, jnp.zeros((LANES - n_slc, W), F32)], axis=0).T.astype(BF16)
        for j in range(G4):
            for g in range(N_KV_GROUPS):
                seln_sc[s, j, g] = seln[(j * N_KV_GROUPS + g) * TQ:(j * N_KV_GROUPS + g + 1) * TQ]

    def tiles(n_pairs):
        jq = qi & (G4 - 1)
        elems = range(nb)
        cat = lambda xs: xs[0] if len(xs) == 1 else jnp.concatenate(xs, axis=0)
        rep = lambda t: t if nb == 1 else jnp.concatenate([t] * nb, axis=0)
        parts = [stack_heads(nq_ref[s]) for s in elems]
        qs = [jnp.concatenate(parts[s], axis=0) for s in elems]

        def bias_tiles(ref, first, count, tile0):
            tiles_ = [ref[jnp.maximum(qi - tile0 - (first + u) + 1, 0)] for u in range(count)]
            return rep(tiles_[0] if count == 1 else jnp.concatenate(tiles_, axis=1))

        zmax = None
        for j0 in range(0, n_win, 2):
            nj = min(2, n_win - j0)
            zp = (cat([_dot_nt(qs[s], kw_ref[s, pl.ds(woff + j0 * TQ, nj * TQ), :]) for s in elems])
                  + bias_tiles(wb_ref, j0, nj, w0))
            zw_sc[:, j0 * TQ:(j0 + nj) * TQ] = zp
            for u in range(nj):
                zj = zp[:, u * TQ:(u + 1) * TQ]
                zmax = zj if zmax is None else jnp.maximum(zmax, zj)
        mw = jnp.max(zmax, axis=-1, keepdims=True)
        pw = jnp.exp2(zw_sc[...] - mw).astype(BF16)
        acc_w = cat([_dot(pw[s * HR + g * GR:s * HR + (g + 1) * GR],
                          vw_refs[g][s, pl.ds(woff, n_win * TQ), :])
                     for s in elems for g in range(N_KV_GROUPS)])

        for s in elems:
            for g in range(N_KV_GROUPS):
                seln = seln_sc[s, jq, g]
                for r in range(N_REP):
                    h = g * N_REP + r
                    lhs_sc[s, h * TQ:(h + 1) * TQ, :] = jnp.concatenate([parts[s][h], seln], axis=1)

        zmax = None
        for pi in range(n_pairs):
            k0 = pi * 2 * TQ
            zp = (cat([_dot_nt(lhs_sc[s], jnp.concatenate([ks_ref[s, k0:k0 + 2 * TQ, :],
                                                           et_ref[k0:k0 + 2 * TQ, :]], axis=1))
                       for s in elems])
                  + bias_tiles(bt_ref, 2 * pi, 2, 0))
            z_sc[pi // 2, :, (pi % 2) * 2 * TQ:(pi % 2 + 1) * 2 * TQ] = zp
            for u in range(2):
                zt = zp[:, u * TQ:(u + 1) * TQ]
                zmax = zt if zmax is None else jnp.maximum(zmax, zt)
        ms = jnp.max(zmax, axis=-1, keepdims=True)
        acc = [None] * (nb * N_KV_GROUPS)
        for c in range((n_pairs + 1) // 2):
            width = min(CH, n_pairs * 2 * TQ - c * CH)
            p = jnp.exp2(z_sc[c, :, 0:width] - ms).astype(BF16)
            for s in elems:
                for g in range(N_KV_GROUPS):
                    i = s * N_KV_GROUPS + g
                    d = _dot(p[i * GR:(i + 1) * GR], vs_refs[g][s, c * CH:c * CH + width, :])
                    acc[i] = d if acc[i] is None else acc[i] + d
        acc_s = cat(acc)

        for s in elems:
            sg = jax.nn.sigmoid(small_ref[s])
            a_s = acc_s[s * HR:(s + 1) * HR]
            a_w = acc_w[s * HR:(s + 1) * HR]
            for r in range(N_REP):
                out = (partc_sc[s, jq, r] + gate(sg, r, 1) * normed(a_s, r)
                       + gate(sg, r, 2) * normed(a_w, r))
                yn_ref[s, :, r * LANES:(r + 1) * LANES] = out.astype(yn_ref.dtype)

    for nc in range(1, n_chunks_max + 1):
        @pl.when(qi // G4 == nc - 1)
        def _(nc=nc):
            @pl.when(qi % G4 == 0)
            def _():
                for s in range(nb):
                    group_prep(s, nc)

            tiles(2 * nc)


def _nsa(nq, nsw, ckv, small, bt, wb, cb, ov, emat):
    B, T, _ = nq.shape
    assert T == 2048, "single 128-wide compressed-key tile assumes T == 2048"
    nqt = T // TQ
    n_cmp = (T - CMP_BLOCK) // CMP_STRIDE + 1
    n_slc = T // SEL_BLOCK
    n_top = min(SEL_TOPK, n_slc)
    n_win = wb.shape[0] - 1
    HR = N_HEADS * TQ
    assert nqt % 4 == 0 and n_slc % 8 == 0 and n_slc <= LANES
    nb = NSA_BATCH if B % NSA_BATCH == 0 else 1
    n_chunks_max = nqt // 4
    kv = lambda j: pl.BlockSpec((nb, T, LANES), lambda b, q: (b, 0, j))
    ck = lambda j: pl.BlockSpec((1, nb, T // CMP_STRIDE, LANES), lambda b, q: (j, b, 0, 0))
    kern = functools.partial(_nsa_kernel, nb=nb, n_slc=n_slc, n_top=n_top, n_win=n_win,
                             n_chunks_max=n_chunks_max)
    return pl.pallas_call(
        kern, out_shape=jax.ShapeDtypeStruct((B, T, N_WIDTH), BF16),
        grid=(B // nb, nqt),
        in_specs=[pl.BlockSpec((nb, TQ, N_WIDTH), lambda b, q: (b, q, 0)),
                  pl.BlockSpec((nb, 4 * TQ, N_WIDTH), lambda b, q: (b, q // 4, 0)),
                  kv(0), kv(1), kv(2), kv(3), kv(4), kv(5), ck(0), ck(1),
                  pl.BlockSpec((nb, TQ, LANES), lambda b, q: (b, q, 0)),
                  pl.BlockSpec((nb, 4 * TQ, LANES), lambda b, q: (b, q // 4, 0)),
                  _const_spec(bt.shape), _const_spec(wb.shape),
                  pl.BlockSpec((4, HR, LANES), lambda b, q: (q // 4, 0, 0)),
                  _const_spec(ov.shape), _const_spec(emat.shape)],
        out_specs=pl.BlockSpec((nb, TQ, N_WIDTH), lambda b, q: (b, q, 0)),
        scratch_shapes=[pltpu.VMEM((nb, HR, 2 * LANES), BF16),
                        pltpu.VMEM((n_chunks_max, nb * HR, 4 * TQ), F32),
                        pltpu.VMEM((nb * HR, n_win * TQ), F32),
                        pltpu.VMEM((nb, 4, N_KV_GROUPS, TQ, LANES), BF16),
                        pltpu.VMEM((nb, 4, N_REP, TQ, LANES), F32)],
        name="nsa",
        compiler_params=pltpu.CompilerParams(dimension_semantics=("arbitrary", "arbitrary"),
                                             vmem_limit_bytes=VMEM_LIMIT),
    )(nq, nq, nsw, nsw, nsw, nsw, nsw, nsw, ckv, ckv, small, small, bt, wb, cb, ov, emat)


def _merge_kernel(x_ref, ym_ref, yn_ref, mg_ref, wbm_ref, wbn_ref, wo_ref,
                  wg_ref, wu_ref, wd_ref, gfin_ref, o_ref, *, tf):
    bm = _dot(ym_ref[...], wbm_ref[...])
    bn = _dot(yn_ref[...], wbn_ref[...])
    mixed = (jax.nn.sigmoid(mg_ref[:, :D_MODEL].astype(F32)) * bm
             + jax.nn.sigmoid(mg_ref[:, D_MODEL:].astype(F32)) * bn)
    h = x_ref[...] + _dot(mixed.astype(BF16), wo_ref[...])
    rs = lax.rsqrt(jnp.mean(h * h, axis=-1, keepdims=True) + RMS_EPS)
    hb = h.astype(BF16)
    acc = jnp.zeros(h.shape, F32)
    for j in range(D_FF // tf):
        gg = _dot(hb, wg_ref[:, j * tf:(j + 1) * tf]) * rs
        uu = _dot(hb, wu_ref[:, j * tf:(j + 1) * tf]) * rs
        act = (gg * jax.nn.sigmoid(gg) * uu).astype(BF16)
        acc = acc + _dot(act, wd_ref[j * tf:(j + 1) * tf, :])
    h2 = h + acc
    o_ref[...] = h2 * lax.rsqrt(jnp.mean(h2 * h2, axis=-1, keepdims=True) + RMS_EPS) * gfin_ref[...]


def _merge(x2, ym, yn, mg, wbm, wbn, wo, wg, wu, wd, gfin, tm=512, tf=256):
    n = x2.shape[0]
    assert n % tm == 0 and D_FF % tf == 0
    row = lambda width: pl.BlockSpec((tm, width), lambda i: (i, 0))
    return pl.pallas_call(
        functools.partial(_merge_kernel, tf=tf),
        out_shape=jax.ShapeDtypeStruct((n, D_MODEL), F32), grid=(n // tm,),
        in_specs=[row(D_MODEL), row(M_WIDTH), row(N_WIDTH), row(N_BRANCH * D_MODEL),
                  _const_spec(wbm.shape), _const_spec(wbn.shape), _const_spec(wo.shape),
                  _const_spec(wg.shape), _const_spec(wu.shape),
                  _const_spec(wd.shape), _const_spec(gfin.shape)],
        out_specs=row(D_MODEL), name="merge_ffn",
        compiler_params=pltpu.CompilerParams(dimension_semantics=("arbitrary",),
                                             vmem_limit_bytes=VMEM_LIMIT),
    )(x2, ym, yn, mg, wbm, wbn, wo, wg, wu, wd, gfin)


def _nsa_constants(T):
    n_cmp = (T - CMP_BLOCK) // CMP_STRIDE + 1
    n_slc = T // SEL_BLOCK
    cs = np.arange(n_cmp) * CMP_STRIDE
    ss = np.arange(n_slc) * SEL_BLOCK
    ov = np.clip(np.minimum(cs[:, None] + CMP_BLOCK, ss[None, :] + SEL_BLOCK)
                 - np.maximum(cs[:, None], ss[None, :]), 0, None) / CMP_STRIDE
    ovt = np.zeros((n_slc, LANES), np.float32)
    ovt[:, :n_cmp] = ov.T
    et = (np.arange(T)[:, None] // SEL_BLOCK == np.arange(LANES)[None, :]).astype(np.float32) * (-NEG)
    return jnp.asarray(ovt), jnp.asarray(et, dtype=BF16)


def _compress_weights(pe_cmp, w_cmp1, w_cmp2):
    half = CMP_BLOCK // 2
    assert N_KV_GROUPS == 2

    def blockdiag(w):
        z = jnp.zeros_like(w)
        return jnp.concatenate([jnp.concatenate([w, z], axis=-1), jnp.concatenate([z, w], axis=-1)], axis=-2)

    w1bd = blockdiag(w_cmp1.reshape(2, CMP_BLOCK, N_HEAD_DIM, CMP_HIDDEN).astype(BF16))
    w2bd = blockdiag(w_cmp2)
    pe2 = jnp.tile(pe_cmp, (1, 1, N_KV_GROUPS))
    return pe2, w1bd[:, :half].astype(BF16), w1bd[:, half:].astype(BF16), w2bd.astype(BF16)


def kernel(x, g_norm_mix, w_in, b_in, b_fgate, conv_qk, g_mlstm_head, pe_cmp, w_cmp1, w_cmp2,
           rel_bias, w_branch, w_out, g_norm_ffn, w_gate, w_up, w_down, g_final):
    B, T, D = x.shape
    assert D == D_MODEL and w_in.shape[0] == 1, "one residual block (DEPTH == 1)"
    N = B * T
    x2 = x.reshape(N, D)

    idx, scale = _inproj_perm()
    w_r = _wprep_t(w_in[0].T, g_norm_mix[0].reshape(1, D), idx, scale)
    b_r = _gather_cols(b_in[0].reshape(1, -1), idx, scale, F32)
    post = np.concatenate([np.ones(M_WIDTH, np.float32), np.full(M_WIDTH, M_HEAD_DIM ** -0.5, np.float32)])
    cw = jnp.zeros((8, 2 * M_WIDTH), F32).at[:CONV_WIDTH].set(conv_qk[0]).at[CONV_WIDTH].set(jnp.asarray(post))
    mqk, mv, mo, nq, cin, nsw, mg, small = _inproj(x2, w_r, b_r)

    bf = jnp.zeros((1, LANES), F32).at[0, :M_HEADS].set(b_fgate[0])
    ltri = jnp.asarray(np.tril(np.ones((M_CHUNK, M_CHUNK), np.float32)))
    r3 = lambda a: a.reshape(B, T, a.shape[-1])
    ym = _mlstm(r3(mqk), r3(mv), r3(mo), r3(small), cw, bf, g_mlstm_head[0].reshape(1, M_WIDTH), ltri)

    pe2, w1t, w1b, w2 = _compress_weights(pe_cmp[0], w_cmp1[0], w_cmp2[0])
    ckv = _compress(cin.reshape(2, B, T, LANES), pe2, w1t, w1b, w2)
    tbl = rel_bias.astype(F32).T.reshape(-1)
    nqt = T // TQ
    bt = _bias_tiles(tbl, nqt + 1, "tok")
    wb = _bias_tiles(tbl, min(WINDOW // TQ + 1, nqt) + 1, "win")
    cb = _bias_tiles(tbl, nqt, "cmp", n_cmp=(T - CMP_BLOCK) // CMP_STRIDE + 1)
    ov, emat = _nsa_constants(T)
    yn = _nsa(r3(nq), r3(nsw), ckv, r3(small), bt, wb, cb, ov, emat)

    wbm = w_branch[0, 0].astype(BF16)
    wbn = jnp.concatenate(
        [w_branch[0, 1, (g * N_REP + r) * N_HEAD_DIM:(g * N_REP + r + 1) * N_HEAD_DIM]
         for r in range(N_REP) for g in range(N_KV_GROUPS)], axis=0).astype(BF16)
    ident = lambda n: (np.arange(n, dtype=np.int32), np.ones((n,), np.float32))
    g_ffn = g_norm_ffn[0].reshape(D, 1)
    out = _merge(x2, ym.reshape(N, M_WIDTH), yn.reshape(N, N_WIDTH), mg, wbm, wbn,
                 w_out[0].astype(BF16), _wprep(w_gate[0], g_ffn, *ident(D_FF)),
                 _wprep(w_up[0], g_ffn, *ident(D_FF)),
                 _wprep(w_down[0], jnp.ones((D_FF, 1), F32), *ident(D)),
                 g_final.reshape(1, D))
    return out.reshape(B, T, D)
```

```python
import functools

import numpy as np
import jax
import jax.numpy as jnp
from jax import lax
from jax.experimental import pallas as pl
from jax.experimental.pallas import tpu as pltpu

F32 = jnp.float32
BF16 = jnp.bfloat16
HI = lax.Precision.HIGHEST

D_MODEL = 1024
M_HEADS = 4
M_HEAD_DIM = 128
M_WIDTH = M_HEADS * M_HEAD_DIM
M_CHUNK = 128
CONV_WIDTH = 4
N_HEADS = 8
N_KV_GROUPS = 2
N_REP = N_HEADS // N_KV_GROUPS
N_HEAD_DIM = 64
N_WIDTH = N_HEADS * N_HEAD_DIM
N_KV_WIDTH = N_KV_GROUPS * N_HEAD_DIM
CMP_BLOCK = 32
CMP_STRIDE = 16
CMP_HIDDEN = 2 * N_HEAD_DIM
SEL_BLOCK = 64
SEL_TOPK = 16
WINDOW = 512
REL_BUCKETS = 32
REL_MAX_DIST = 1024
N_BRANCH = 2
D_FF = 2816
RMS_EPS = 1e-6
BIG = 1e9
NEG = -1e30

LANES = 128
TQ = 128
NSA_BATCH = 1
VMEM_LIMIT = 56 * 1024 * 1024

_OFF_MQ, _OFF_MK, _OFF_MV, _OFF_MO = 0, 512, 1024, 1536
_OFF_MI, _OFF_MF, _OFF_NQ, _OFF_NKV = 2048, 2052, 2056, 2568
_OFF_NGATE, _OFF_MERGE, _D_IN = 3336, 3360, 5408
_SMALL_NGATE = 8

_SEGS = (("mqk", 1024), ("mv", 512), ("mo", 512), ("nq", 512), ("kc", 128),
         ("vc", 128), ("nsw", 512), ("mg", 2048), ("small", 128))
_D_IN_PAD = sum(w for _, w in _SEGS)
LOG2E = 1.4426950408889634


def _dot(a, b, **kw):
    return jnp.dot(a, b, preferred_element_type=F32, **kw)


def _dot_nt(a, b):
    return lax.dot_general(a, b, (((1,), (1,)), ((), ())), preferred_element_type=F32)


def _dot_tn(a, b):
    return lax.dot_general(a, b, (((0,), (0,)), ((), ())), preferred_element_type=F32)


def _const_spec(shape):
    nd = len(shape)
    return pl.BlockSpec(shape, lambda *_: (0,) * nd, pipeline_mode=pl.Buffered(1))


def _bucket_thresholds():
    max_exact = REL_BUCKETS // 2
    assert REL_MAX_DIST == 64 * max_exact and REL_BUCKETS - max_exact == 16
    thr = []
    for k in range(1, REL_BUCKETS - max_exact):
        t = max_exact
        while t ** 8 < (max_exact ** 8) * (2 ** (3 * k)):
            t += 1
        thr.append(t)
    return tuple(thr)


_BUCKET_THR = _bucket_thresholds()


def _inproj_perm():
    idx = np.zeros((_D_IN_PAD,), np.int32)
    scale = np.zeros((_D_IN_PAD,), np.float32)
    pos = 0

    def put(cols, s=1.0):
        nonlocal pos
        n = len(cols)
        idx[pos:pos + n] = cols
        scale[pos:pos + n] = s
        pos += n

    put(np.arange(_OFF_MQ, _OFF_MV))
    put(np.arange(_OFF_MV, _OFF_MO))
    put(np.arange(_OFF_MO, _OFF_MI))
    nq = np.zeros((N_WIDTH,), np.int32)
    for r in range(N_REP):
        for g in range(N_KV_GROUPS):
            for d in range(N_HEAD_DIM):
                nq[r * 128 + g * 64 + d] = _OFF_NQ + (g * N_REP + r) * N_HEAD_DIM + d
    put(nq, N_HEAD_DIM ** -0.5 * LOG2E)
    kv = lambda j, g: np.arange(_OFF_NKV + (j * N_KV_GROUPS + g) * N_HEAD_DIM,
                                _OFF_NKV + (j * N_KV_GROUPS + g + 1) * N_HEAD_DIM)
    put(np.arange(_OFF_NKV, _OFF_NKV + 256))
    put(np.concatenate([kv(2, 0), kv(2, 1)]))
    put(np.concatenate([kv(4, 0), kv(4, 1)]))
    put(np.concatenate([kv(3, 0), kv(3, 1)]))
    put(np.concatenate([kv(5, 0), kv(5, 1)]))
    put(np.arange(_OFF_MERGE, _D_IN))
    put(np.arange(_OFF_MI, _OFF_MI + 8))
    put(np.arange(_OFF_NGATE, _OFF_NGATE + 24))
    pos += LANES - 32
    assert pos == _D_IN_PAD
    return idx, scale


def _gather_cols(a, idx, scale, dtype):
    pieces = []
    start = 0
    n = len(idx)
    for c in range(1, n + 1):
        same = c < n and scale[c] == scale[start] and (scale[c] == 0.0 or idx[c] == idx[c - 1] + 1)
        if not same:
            if scale[start] == 0.0:
                piece = jnp.zeros((a.shape[0], c - start), dtype)
            else:
                piece = a[:, int(idx[start]):int(idx[start]) + (c - start)]
                if scale[start] != 1.0:
                    piece = piece * float(scale[start])
            pieces.append(piece.astype(dtype))
            start = c
    return jnp.concatenate(pieces, axis=1)


def _col_runs(idx, scale):
    runs = []
    start = 0
    n = len(idx)
    for c in range(1, n + 1):
        same = c < n and scale[c] == scale[start] and (scale[c] == 0.0 or idx[c] == idx[c - 1] + 1)
        if not same:
            runs.append((start, int(idx[start]), c - start, float(scale[start])))
            start = c
    return runs


def _wprep_kernel(w_ref, g_ref, o_ref, *, runs):
    g = g_ref[...]
    for dst, src, n, scale in runs:
        if scale == 0.0:
            o_ref[:, dst:dst + n] = jnp.zeros((o_ref.shape[0], n), o_ref.dtype)
        else:
            o_ref[:, dst:dst + n] = (w_ref[:, src:src + n] * (g * scale)).astype(o_ref.dtype)


def _wprep_t_kernel(wt_ref, g_ref, o_ref, *, runs):
    g = g_ref[...]
    for dst, src, n, scale in runs:
        if scale == 0.0:
            o_ref[dst:dst + n, :] = jnp.zeros((n, o_ref.shape[1]), o_ref.dtype)
        else:
            o_ref[dst:dst + n, :] = (wt_ref[src:src + n, :] * (g * scale)).astype(o_ref.dtype)


def _wprep_t(wt, g, idx, scale, tc=256):
    rows, cols = wt.shape
    assert cols % tc == 0
    return pl.pallas_call(
        functools.partial(_wprep_t_kernel, runs=_col_runs(idx, scale)),
        out_shape=jax.ShapeDtypeStruct((len(idx), cols), BF16), grid=(cols // tc,),
        in_specs=[pl.BlockSpec((rows, tc), lambda i: (0, i)), pl.BlockSpec((1, tc), lambda i: (0, i))],
        out_specs=pl.BlockSpec((len(idx), tc), lambda i: (0, i)), name="wprep_t",
        compiler_params=pltpu.CompilerParams(dimension_semantics=("arbitrary",),
                                             vmem_limit_bytes=VMEM_LIMIT),
    )(wt, g)


def _wprep(w, g, idx, scale, tr=256):
    rows, cols = w.shape
    assert rows % tr == 0
    return pl.pallas_call(
        functools.partial(_wprep_kernel, runs=_col_runs(idx, scale)),
        out_shape=jax.ShapeDtypeStruct((rows, len(idx)), BF16), grid=(rows // tr,),
        in_specs=[pl.BlockSpec((tr, cols), lambda i: (i, 0)), pl.BlockSpec((tr, 1), lambda i: (i, 0))],
        out_specs=pl.BlockSpec((tr, len(idx)), lambda i: (i, 0)), name="wprep",
        compiler_params=pltpu.CompilerParams(dimension_semantics=("arbitrary",),
                                             vmem_limit_bytes=VMEM_LIMIT),
    )(w, g)


def _inproj_kernel(x_ref, w_ref, b_ref, mqk_ref, mv_ref, mo_ref, nq_ref,
                   cin_ref, nsw_ref, mg_ref, small_ref):
    x = x_ref[...]
    rs = lax.rsqrt(jnp.mean(x * x, axis=-1, keepdims=True) + RMS_EPS)
    xb = x.astype(BF16)

    def seg(a, n):
        return _dot_nt(xb, w_ref[a:a + n, :]) * rs + b_ref[:, a:a + n]

    off = {}
    pos = 0
    for name, w in _SEGS:
        off[name] = pos
        pos += w

    mqk_ref[...] = seg(off["mqk"], 1024).astype(BF16)
    mv_ref[...] = seg(off["mv"], 512).astype(BF16)
    mo_ref[...] = seg(off["mo"], 512).astype(BF16)
    nq_ref[...] = seg(off["nq"], 512).astype(BF16)
    cc = seg(off["kc"], 2 * LANES)
    cin_ref[0] = cc[:, :LANES]
    cin_ref[1] = cc[:, LANES:]
    nsw_ref[:, 0:2 * LANES] = seg(off["nsw"], 2 * LANES).astype(BF16)
    vv = seg(off["nsw"] + 2 * LANES, 2 * LANES)
    lo = lax.broadcasted_iota(jnp.int32, (x.shape[0], LANES), 1) < N_HEAD_DIM
    for j in range(2):
        vj = vv[:, j * LANES:(j + 1) * LANES]
        nsw_ref[:, (2 + 2 * j) * LANES:(3 + 2 * j) * LANES] = jnp.where(lo, vj, 1.0).astype(BF16)
        nsw_ref[:, (3 + 2 * j) * LANES:(4 + 2 * j) * LANES] = jnp.where(lo, 1.0, vj).astype(BF16)
    mg_ref[...] = seg(off["mg"], 2048).astype(BF16)
    small_ref[...] = seg(off["small"], 128)


def _inproj(x2, w, b, tm=1024):
    n = x2.shape[0]
    assert n % tm == 0
    row = lambda width: pl.BlockSpec((tm, width), lambda i: (i, 0))
    out_shape = (
        jax.ShapeDtypeStruct((n, 1024), BF16), jax.ShapeDtypeStruct((n, 512), BF16),
        jax.ShapeDtypeStruct((n, 512), BF16), jax.ShapeDtypeStruct((n, 512), BF16),
        jax.ShapeDtypeStruct((2, n, 128), F32), jax.ShapeDtypeStruct((n, 768), BF16),
        jax.ShapeDtypeStruct((n, 2048), BF16), jax.ShapeDtypeStruct((n, 128), F32))
    out_specs = (row(1024), row(512), row(512), row(512),
                 pl.BlockSpec((2, tm, 128), lambda i: (0, i, 0)), row(768), row(2048), row(128))
    return pl.pallas_call(
        _inproj_kernel, out_shape=out_shape, grid=(n // tm,),
        in_specs=[row(D_MODEL), _const_spec((_D_IN_PAD, D_MODEL)), _const_spec((1, _D_IN_PAD))],
        out_specs=out_specs, name="inproj",
        compiler_params=pltpu.CompilerParams(dimension_semantics=("arbitrary",),
                                             vmem_limit_bytes=VMEM_LIMIT),
    )(x2, w, b)


def _mlstm_kernel(mqk_ref, mv_ref, mo_ref, small_ref, cw_ref, bf_ref, gh_ref, ltri_ref,
                  ym_ref, ebuf, c_st, m_st, g_bc, g_dt, g_a, g_mrow, g_row):
    L = M_CHUNK
    c = pl.program_id(1)
    n_chunks = g_bc.shape[0]
    rows = lax.broadcasted_iota(jnp.int32, (L, LANES), 0)

    @pl.when(c == 0)
    def _():
        ebuf[0:8, :] = jnp.zeros((8, 2 * M_WIDTH), F32)
        c_st[...] = jnp.zeros_like(c_st)
        m_st[...] = jnp.zeros_like(m_st)
        for cc in range(n_chunks):
            ig = small_ref[0, cc * L:(cc + 1) * L, :]
            fg = pltpu.roll(ig, LANES - M_HEADS, 1) + bf_ref[...]
            lf = jnp.minimum(fg, 0.0) - jnp.log1p(jnp.exp(-jnp.abs(fg)))
            bc = jnp.dot(ltri_ref[...], lf, precision=HI, preferred_element_type=F32)
            blast = bc[L - 1:L, :]
            a_all = blast - bc + ig
            d = ig - bc
            cm = d
            sh = 1
            while sh < L:
                cm = jnp.maximum(cm, jnp.where(rows >= sh, pltpu.roll(cm, sh, 0), -jnp.inf))
                sh *= 2
            g_bc[cc] = bc
            g_dt[cc] = d.T
            g_a[cc] = a_all
            g_mrow[cc] = bc + cm
            g_row[cc, 0:1, :] = blast
            g_row[cc, 1:2, :] = jnp.max(a_all, axis=0, keepdims=True)

    ebuf[8:8 + L, :] = mqk_ref[0].astype(F32)
    ext = ebuf[...]
    conv = cw_ref[CONV_WIDTH - 1:CONV_WIDTH, :] * ext[8:, :]
    for k in range(1, CONV_WIDTH):
        conv = conv + cw_ref[CONV_WIDTH - 1 - k:CONV_WIDTH - k, :] * pltpu.roll(ext, k, 0)[8:, :]
    ebuf[0:8, :] = ebuf[L:L + 8, :]
    qk = (conv * jax.nn.sigmoid(conv)).astype(BF16)

    bc = g_bc[c]
    dt = g_dt[c]
    a_all = g_a[c]
    blast = g_row[c, 0:1, :]
    m_prev = m_st[0:1, :]
    m_new = jnp.maximum(blast + m_prev, g_row[c, 1:2, :])
    decay = jnp.exp(blast + m_prev - m_new)
    w_all = jnp.exp(a_all - m_new)
    log_inter = bc + m_prev
    m_i_all = jnp.maximum(g_mrow[c], log_inter)
    k_scale = M_HEAD_DIM ** -0.5
    s_inter_all = jnp.exp(log_inter - m_i_all) * k_scale
    emi_all = jnp.exp(-m_i_all)
    u_all = bc - m_i_all + float(np.log(k_scale))

    row = lax.broadcasted_iota(jnp.int32, (L, L), 0)
    col = lax.broadcasted_iota(jnp.int32, (L, L), 1)
    causal = row >= col

    ones = jnp.ones((L, M_HEAD_DIM), BF16)
    for h in range(M_HEADS):
        sl = slice(h * M_HEAD_DIM, (h + 1) * M_HEAD_DIM)
        qb = qk[:, sl]
        kb = qk[:, M_WIDTH + h * M_HEAD_DIM:M_WIDTH + (h + 1) * M_HEAD_DIM]
        v = mv_ref[0, :, sl]
        cn_prev = c_st[h]

        logp = jnp.where(causal, u_all[:, h:h + 1] + dt[h:h + 1, :], -jnp.inf)
        p = jnp.exp(logp) * _dot_nt(qb, kb)
        nd = (_dot(p.astype(BF16), jnp.concatenate([v, ones], axis=1))
              + s_inter_all[:, h:h + 1] * _dot(qb, cn_prev.astype(BF16)))
        hh = nd[:, :M_HEAD_DIM] / jnp.maximum(jnp.abs(nd[:, M_HEAD_DIM:]), emi_all[:, h:h + 1])
        hh = hh * lax.rsqrt(jnp.mean(hh * hh, axis=-1, keepdims=True) + RMS_EPS) * gh_ref[:, sl]
        ym_ref[0, :, sl] = (jax.nn.sigmoid(mo_ref[0, :, sl].astype(F32)) * hh).astype(ym_ref.dtype)

        w_col = w_all[:, h:h + 1]
        dec = decay[:, h:h + 1]
        vw = jnp.concatenate([v.astype(F32) * w_col, jnp.broadcast_to(w_col, (L, M_HEAD_DIM))], axis=1)
        c_st[h] = dec * cn_prev + _dot_tn(kb, vw.astype(BF16))

    m_st[0:1, :] = m_new


def _mlstm(mqk, mv, mo, small, cw, bf, gh, ltri):
    B, T, _ = mqk.shape
    L = M_CHUNK
    assert T % L == 0
    nc = T // L
    blk = lambda w: pl.BlockSpec((1, L, w), lambda b, c: (b, c, 0))
    gate = lambda: pltpu.VMEM((nc, L, LANES), F32)
    return pl.pallas_call(
        _mlstm_kernel, out_shape=jax.ShapeDtypeStruct((B, T, M_WIDTH), BF16),
        grid=(B, nc),
        in_specs=[blk(2 * M_WIDTH), blk(M_WIDTH), blk(M_WIDTH),
                  pl.BlockSpec((1, T, LANES), lambda b, c: (b, 0, 0)),
                  _const_spec((8, 2 * M_WIDTH)), _const_spec((1, LANES)),
                  _const_spec((1, M_WIDTH)), _const_spec((L, L))],
        out_specs=blk(M_WIDTH),
        scratch_shapes=[pltpu.VMEM((L + 8, 2 * M_WIDTH), F32),
                        pltpu.VMEM((M_HEADS, M_HEAD_DIM, 2 * M_HEAD_DIM), F32),
                        pltpu.VMEM((8, LANES), F32),
                        gate(), gate(), gate(), gate(), pltpu.VMEM((nc, 8, LANES), F32)],
        name="mlstm",
        compiler_params=pltpu.CompilerParams(dimension_semantics=("arbitrary", "arbitrary")),
    )(mqk, mv, mo, small, cw, bf, gh, ltri)


def _compress_kernel(x_ref, pe_ref, w1t_ref, w1b_ref, w2_ref, o_ref, *, nch, nbt):
    half = CMP_BLOCK // 2
    rows = nbt * nch
    top = jnp.zeros((rows, 2 * CMP_HIDDEN), F32)
    bot = jnp.zeros((rows, 2 * CMP_HIDDEN), F32)
    for p in range(half):
        xp = jnp.concatenate([x_ref[0, e, pl.ds(p, nch, stride=CMP_STRIDE), :] for e in range(nbt)], axis=0)
        top = top + _dot((xp + pe_ref[0, p:p + 1, :]).astype(BF16), w1t_ref[0, p])
        bot = bot + _dot((xp + pe_ref[0, half + p:half + p + 1, :]).astype(BF16), w1b_ref[0, p])
    hid = top + pltpu.roll(bot, rows - 1, 0)
    act = hid * jax.nn.sigmoid(hid)
    out = _dot(act.astype(BF16), w2_ref[0]).astype(o_ref.dtype)
    for e in range(nbt):
        o_ref[0, e] = out[e * nch:(e + 1) * nch]


def _compress(cin, pe2, w1t, w1b, w2):
    _, B, T, _ = cin.shape
    nch = T // CMP_STRIDE
    nbt = 4 if B % 4 == 0 else 1
    sel = lambda *shape: pl.BlockSpec((1,) + shape, lambda j, b: (j,) + (0,) * len(shape))
    return pl.pallas_call(
        functools.partial(_compress_kernel, nch=nch, nbt=nbt),
        out_shape=jax.ShapeDtypeStruct((2, B, nch, LANES), BF16),
        grid=(2, B // nbt),
        in_specs=[pl.BlockSpec((1, nbt, T, LANES), lambda j, b: (j, b, 0, 0)),
                  sel(CMP_BLOCK, LANES), sel(CMP_BLOCK // 2, LANES, 2 * CMP_HIDDEN),
                  sel(CMP_BLOCK // 2, LANES, 2 * CMP_HIDDEN), sel(2 * CMP_HIDDEN, LANES)],
        out_specs=pl.BlockSpec((1, nbt, nch, LANES), lambda j, b: (j, b, 0, 0)),
        name="compress",
        compiler_params=pltpu.CompilerParams(dimension_semantics=("arbitrary", "arbitrary")),
    )(cin, pe2, w1t, w1b, w2)


def _bias_kernel(tbl_ref, o_ref, *, kind, n_cmp):
    pid = pl.program_id(0)
    toeplitz = kind != "cmp"
    if toeplitz:
        k = lax.broadcasted_iota(jnp.int32, (8, 2 * LANES), 1)
        dist = (pid - 1) * TQ + jnp.where(k < LANES, -k, 2 * LANES - k)
    else:
        al = lax.broadcasted_iota(jnp.int32, (CMP_STRIDE, 2 * LANES), 0)
        m = lax.broadcasted_iota(jnp.int32, (CMP_STRIDE, 2 * LANES), 1)
        dist = pid * TQ - (CMP_BLOCK - 1) + al - CMP_STRIDE * jnp.where(m < LANES, m, m - 2 * LANES)
    n = jnp.maximum(dist, 0)
    cnt = jnp.zeros_like(n)
    for t in _BUCKET_THR:
        cnt = cnt + jnp.where(n >= t, 1, 0)
    bucket = jnp.where(n < REL_BUCKETS // 2, n, REL_BUCKETS // 2 + cnt)
    if kind == "tok":
        madd = jnp.where((dist >= 0) & (pid > 0), 0.0, NEG)
    elif kind == "win":
        madd = jnp.where((dist >= 0) & (dist < WINDOW) & (pid > 0), 0.0, NEG)
    else:
        madd = jnp.where(dist >= 0, 0.0, NEG)
        valid_c = lax.broadcasted_iota(jnp.int32, (TQ, LANES), 1) < n_cmp
    for h in range(N_HEADS):
        val = jnp.zeros(dist.shape, F32)
        for bb in range(REL_BUCKETS):
            val = jnp.where(bucket == bb, tbl_ref[h * REL_BUCKETS + bb], val)
        val = val * LOG2E + madd
        if toeplitz:
            g = jnp.broadcast_to(val[0:1, :], (TQ, 2 * LANES))
            val = pltpu.roll(g, 0, 1, stride=1, stride_axis=0)[:, :LANES]
        else:
            bands = [val[:, :LANES]] + [pltpu.roll(val, ah, 1)[:, :LANES] for ah in range(1, TQ // CMP_STRIDE)]
            val = jnp.where(valid_c, jnp.concatenate(bands, axis=0), NEG)
        o_ref[0, h * TQ:(h + 1) * TQ, :] = val


def _bias_tiles(tbl, n_tiles, kind, n_cmp=0):
    return pl.pallas_call(
        functools.partial(_bias_kernel, kind=kind, n_cmp=n_cmp),
        out_shape=jax.ShapeDtypeStruct((n_tiles, N_HEADS * TQ, LANES), F32),
        grid=(n_tiles,),
        in_specs=[pl.BlockSpec(memory_space=pltpu.SMEM)],
        out_specs=pl.BlockSpec((1, N_HEADS * TQ, LANES), lambda i: (i, 0, 0)),
        name="bias_" + kind,
        compiler_params=pltpu.CompilerParams(dimension_semantics=("arbitrary",)),
    )(tbl)


def _nsa_kernel(nq_ref, nq4_ref, ks_ref, kw_ref, vs0_ref, vs1_ref, vw0_ref, vw1_ref, kc_ref, vc_ref,
                small_ref, small4_ref, bt_ref, wb_ref, cb4_ref, ovt_ref, et_ref, yn_ref,
                lhs_sc, z_sc, zw_sc, seln_sc, partc_sc, *, nb, n_slc, n_top, n_win, n_chunks_max):
    qi = pl.program_id(1)
    HR = N_HEADS * TQ
    GR = N_REP * TQ
    CH = 4 * TQ
    G4 = 4
    vs_refs = (vs0_ref, vs1_ref)
    vw_refs = (vw0_ref, vw1_ref)
    lane = lax.broadcasted_iota(jnp.int32, (TQ, LANES), 1)
    lo = lane < N_HEAD_DIM

    def stack_heads(q_all):
        zero = jnp.zeros((TQ, LANES), BF16)
        parts = []
        for g in range(N_KV_GROUPS):
            for r in range(N_REP):
                qr = q_all[:, r * LANES:(r + 1) * LANES]
                parts.append(jnp.where(lo if g == 0 else jnp.logical_not(lo), qr, zero))
        return parts

    def pair(o, r):
        return jnp.where(lo, o[r * TQ:(r + 1) * TQ], o[(N_REP + r) * TQ:(N_REP + r + 1) * TQ])

    def normed(acc, r):
        den = jnp.where(lo, acc[(N_REP + r) * TQ:(N_REP + r + 1) * TQ], acc[r * TQ:(r + 1) * TQ])
        return pair(acc, r) / pltpu.roll(den, N_HEAD_DIM, 1)

    def gate(sg, r, ci):
        c0 = _SMALL_NGATE + r * 3 + ci
        c1 = _SMALL_NGATE + (N_REP + r) * 3 + ci
        return jnp.where(lo, sg[:, c0:c0 + 1], sg[:, c1:c1 + 1])

    w0 = jnp.maximum(qi - (n_win - 1), 0)
    woff = pl.multiple_of(w0 * TQ, TQ)

    def group_prep(s, n_chunks):
        qs4 = jnp.concatenate(
            [p for j in range(G4) for p in stack_heads(nq4_ref[s, j * TQ:(j + 1) * TQ, :])], axis=0)
        rows = lax.broadcasted_iota(jnp.int32, (G4 * HR, 1), 0)
        t_rows = (qi + jnp.right_shift(rows, 10)) * TQ + (rows & (TQ - 1))

        z = _dot_nt(qs4, kc_ref[0, s]) + cb4_ref[...].reshape(G4 * HR, LANES)
        e = jnp.exp2(z - jnp.max(z, axis=-1, keepdims=True))
        l = _dot(e.astype(BF16), jnp.ones((LANES, LANES), BF16))
        p_c = e * jnp.where(t_rows >= CMP_BLOCK - 1, 1.0 / jnp.maximum(l, 1e-30), 0.0)
        o_c = _dot(p_c.astype(BF16), vc_ref[0, s])
        sg4 = jax.nn.sigmoid(small4_ref[s])
        for j in range(G4):
            for r in range(N_REP):
                partc_sc[s, j, r] = (gate(sg4[j * TQ:(j + 1) * TQ], r, 0)
                                     * pair(o_c[j * HR:(j + 1) * HR], r))

        W = G4 * N_KV_GROUPS * TQ
        jb = lax.broadcasted_iota(jnp.int32, (n_slc, W), 0)
        col = lax.broadcasted_iota(jnp.int32, (n_slc, W), 1)
        tq = (qi + jnp.right_shift(col, 8)) * TQ + (col & (TQ - 1))
        cur = jnp.right_shift(tq, 6)
        elig = jb <= cur
        if n_chunks * CH <= n_top * SEL_BLOCK:
            sel = jnp.where(elig, 1.0, 0.0)
        else:
            forced = (jb == 0) | (jb == cur) | (jb == cur - 1)
            psums = []
            for j in range(G4):
                for g in range(N_KV_GROUPS):
                    base = j * HR + g * GR
                    ps = p_c[base:base + TQ]
                    for r in range(1, N_REP):
                        ps = ps + p_c[base + r * TQ:base + (r + 1) * TQ]
                    psums.append(ps)
            imp = lax.dot_general(ovt_ref[...], jnp.concatenate(psums, axis=0), (((1,), (1,)), ((), ())),
                                  precision=HI, preferred_element_type=F32)
            score = jnp.where(elig, jnp.where(forced, BIG, imp), -BIG)
            cnt = jnp.zeros((n_slc, W), F32)
            for i in range(n_slc):
                si = score[i:i + 1, :]
                tie = jnp.where(jb > i, 1.0, 0.0)
                cnt = cnt + jnp.where(si > score, 1.0, jnp.where(si == score, tie, 0.0))
            sel = jnp.where((cnt < n_top) & (score > -BIG / 2), 1.0, 0.0)
        seln = jnp.concatenate([sel - 1.0, jnp.zeros((LANES - n_slc, W), F32)], axis=0).T.astype(BF16)
        for j in range(G4):
            for g in range(N_KV_GROUPS):
                seln_sc[s, j, g] = seln[(j * N_KV_GROUPS + g) * TQ:(j * N_KV_GROUPS + g + 1) * TQ]

    def tiles(n_pairs):
        jq = qi & (G4 - 1)
        elems = range(nb)
        cat = lambda xs: xs[0] if len(xs) == 1 else jnp.concatenate(xs, axis=0)
        rep = lambda t: t if nb == 1 else jnp.concatenate([t] * nb, axis=0)
        parts = [stack_heads(nq_ref[s]) for s in elems]
        qs = [jnp.concatenate(parts[s], axis=0) for s in elems]

        def bias_tiles(ref, first, count, tile0):
            tiles_ = [ref[jnp.maximum(qi - tile0 - (first + u) + 1, 0)] for u in range(count)]
            return rep(tiles_[0] if count == 1 else jnp.concatenate(tiles_, axis=1))

        zmax = None
        for j0 in range(0, n_win, 2):
            nj = min(2, n_win - j0)
            zp = (cat([_dot_nt(qs[s], kw_ref[s, pl.ds(woff + j0 * TQ, nj * TQ), :]) for s in elems])
                  + bias_tiles(wb_ref, j0, nj, w0))
            zw_sc[:, j0 * TQ:(j0 + nj) * TQ] = zp
            for u in range(nj):
                zj = zp[:, u * TQ:(u + 1) * TQ]
                zmax = zj if zmax is None else jnp.maximum(zmax, zj)
        mw = jnp.max(zmax, axis=-1, keepdims=True)
        pw = jnp.exp2(zw_sc[...] - mw).astype(BF16)
        acc_w = cat([_dot(pw[s * HR + g * GR:s * HR + (g + 1) * GR],
                          vw_refs[g][s, pl.ds(woff, n_win * TQ), :])
                     for s in elems for g in range(N_KV_GROUPS)])

        for s in elems:
            for g in range(N_KV_GROUPS):
                seln = seln_sc[s, jq, g]
                for r in range(N_REP):
                    h = g * N_REP + r
                    lhs_sc[s, h * TQ:(h + 1) * TQ, :] = jnp.concatenate([parts[s][h], seln], axis=1)

        zmax = None
        for pi in range(n_pairs):
            k0 = pi * 2 * TQ
            zp = (cat([_dot_nt(lhs_sc[s], jnp.concatenate([ks_ref[s, k0:k0 + 2 * TQ, :],
                                                           et_ref[k0:k0 + 2 * TQ, :]], axis=1))
                       for s in elems])
                  + bias_tiles(bt_ref, 2 * pi, 2, 0))
            z_sc[pi // 2, :, (pi % 2) * 2 * TQ:(pi % 2 + 1) * 2 * TQ] = zp
            for u in range(2):
                zt = zp[:, u * TQ:(u + 1) * TQ]
                zmax = zt if zmax is None else jnp.maximum(zmax, zt)
        ms = jnp.max(zmax, axis=-1, keepdims=True)
        acc = [None] * (nb * N_KV_GROUPS)
        for c in range((n_pairs + 1) // 2):
            width = min(CH, n_pairs * 2 * TQ - c * CH)
            p = jnp.exp2(z_sc[c, :, 0:width] - ms).astype(BF16)
            for s in elems:
                for g in range(N_KV_GROUPS):
                    i = s * N_KV_GROUPS + g
                    d = _dot(p[i * GR:(i + 1) * GR], vs_refs[g][s, c * CH:c * CH + width, :])
                    acc[i] = d if acc[i] is None else acc[i] + d
        acc_s = cat(acc)

        for s in elems:
            sg = jax.nn.sigmoid(small_ref[s])
            a_s = acc_s[s * HR:(s + 1) * HR]
            a_w = acc_w[s * HR:(s + 1) * HR]
            for r in range(N_REP):
                out = (partc_sc[s, jq, r] + gate(sg, r, 1) * normed(a_s, r)
                       + gate(sg, r, 2) * normed(a_w, r))
                yn_ref[s, :, r * LANES:(r + 1) * LANES] = out.astype(yn_ref.dtype)

    for nc in range(1, n_chunks_max + 1):
        @pl.when(qi // G4 == nc - 1)
        def _(nc=nc):
            @pl.when(qi % G4 == 0)
            def _():
                for s in range(nb):
                    group_prep(s, nc)

            tiles(2 * nc)


def _nsa(nq, nsw, ckv, small, bt, wb, cb, ov, emat):
    B, T, _ = nq.shape
    assert T == 2048, "single 128-wide compressed-key tile assumes T == 2048"
    nqt = T // TQ
    n_cmp = (T - CMP_BLOCK) // CMP_STRIDE + 1
    n_slc = T // SEL_BLOCK
    n_top = min(SEL_TOPK, n_slc)
    n_win = wb.shape[0] - 1
    HR = N_HEADS * TQ
    assert nqt % 4 == 0 and n_slc % 8 == 0 and n_slc <= LANES
    nb = NSA_BATCH if B % NSA_BATCH == 0 else 1
    n_chunks_max = nqt // 4
    kv = lambda j: pl.BlockSpec((nb, T, LANES), lambda b, q: (b, 0, j))
    ck = lambda j: pl.BlockSpec((1, nb, T // CMP_STRIDE, LANES), lambda b, q: (j, b, 0, 0))
    kern = functools.partial(_nsa_kernel, nb=nb, n_slc=n_slc, n_top=n_top, n_win=n_win,
                             n_chunks_max=n_chunks_max)
    return pl.pallas_call(
        kern, out_shape=jax.ShapeDtypeStruct((B, T, N_WIDTH), BF16),
        grid=(B // nb, nqt),
        in_specs=[pl.BlockSpec((nb, TQ, N_WIDTH), lambda b, q: (b, q, 0)),
                  pl.BlockSpec((nb, 4 * TQ, N_WIDTH), lambda b, q: (b, q // 4, 0)),
                  kv(0), kv(1), kv(2), kv(3), kv(4), kv(5), ck(0), ck(1),
                  pl.BlockSpec((nb, TQ, LANES), lambda b, q: (b, q, 0)),
                  pl.BlockSpec((nb, 4 * TQ, LANES), lambda b, q: (b, q // 4, 0)),
                  _const_spec(bt.shape), _const_spec(wb.shape),
                  pl.BlockSpec((4, HR, LANES), lambda b, q: (q // 4, 0, 0)),
                  _const_spec(ov.shape), _const_spec(emat.shape)],
        out_specs=pl.BlockSpec((nb, TQ, N_WIDTH), lambda b, q: (b, q, 0)),
        scratch_shapes=[pltpu.VMEM((nb, HR, 2 * LANES), BF16),
                        pltpu.VMEM((n_chunks_max, nb * HR, 4 * TQ), F32),
                        pltpu.VMEM((nb * HR, n_win * TQ), F32),
                        pltpu.VMEM((nb, 4, N_KV_GROUPS, TQ, LANES), BF16),
                        pltpu.VMEM((nb, 4, N_REP, TQ, LANES), F32)],
        name="nsa",
        compiler_params=pltpu.CompilerParams(dimension_semantics=("arbitrary", "arbitrary"),
                                             vmem_limit_bytes=VMEM_LIMIT),
    )(nq, nq, nsw, nsw, nsw, nsw, nsw, nsw, ckv, ckv, small, small, bt, wb, cb, ov, emat)


def _merge_kernel(x_ref, ym_ref, yn_ref, mg_ref, wbm_ref, wbn_ref, wo_ref,
                  wg_ref, wu_ref, wd_ref, gfin_ref, o_ref, *, tf):
    bm = _dot(ym_ref[...], wbm_ref[...])
    bn = _dot(yn_ref[...], wbn_ref[...])
    mixed = (jax.nn.sigmoid(mg_ref[:, :D_MODEL].astype(F32)) * bm
             + jax.nn.sigmoid(mg_ref[:, D_MODEL:].astype(F32)) * bn)
    h = x_ref[...] + _dot(mixed.astype(BF16), wo_ref[...])
    rs = lax.rsqrt(jnp.mean(h * h, axis=-1, keepdims=True) + RMS_EPS)
    hb = h.astype(BF16)
    acc = jnp.zeros(h.shape, F32)
    for j in range(D_FF // tf):
        gg = _dot(hb, wg_ref[:, j * tf:(j + 1) * tf]) * rs
        uu = _dot(hb, wu_ref[:, j * tf:(j + 1) * tf]) * rs
        act = (gg * jax.nn.sigmoid(gg) * uu).astype(BF16)
        acc = acc + _dot(act, wd_ref[j * tf:(j + 1) * tf, :])
    h2 = h + acc
    o_ref[...] = h2 * lax.rsqrt(jnp.mean(h2 * h2, axis=-1, keepdims=True) + RMS_EPS) * gfin_ref[...]


def _merge(x2, ym, yn, mg, wbm, wbn, wo, wg, wu, wd, gfin, tm=512, tf=256):
    n = x2.shape[0]
    assert n % tm == 0 and D_FF % tf == 0
    row = lambda width: pl.BlockSpec((tm, width), lambda i: (i, 0))
    return pl.pallas_call(
        functools.partial(_merge_kernel, tf=tf),
        out_shape=jax.ShapeDtypeStruct((n, D_MODEL), F32), grid=(n // tm,),
        in_specs=[row(D_MODEL), row(M_WIDTH), row(N_WIDTH), row(N_BRANCH * D_MODEL),
                  _const_spec(wbm.shape), _const_spec(wbn.shape), _const_spec(wo.shape),
                  _const_spec(wg.shape), _const_spec(wu.shape),
                  _const_spec(wd.shape), _const_spec(gfin.shape)],
        out_specs=row(D_MODEL), name="merge_ffn",
        compiler_params=pltpu.CompilerParams(dimension_semantics=("arbitrary",),
                                             vmem_limit_bytes=VMEM_LIMIT),
    )(x2, ym, yn, mg, wbm, wbn, wo, wg, wu, wd, gfin)


def _nsa_constants(T):
    n_cmp = (T - CMP_BLOCK) // CMP_STRIDE + 1
    n_slc = T // SEL_BLOCK
    cs = np.arange(n_cmp) * CMP_STRIDE
    ss = np.arange(n_slc) * SEL_BLOCK
    ov = np.clip(np.minimum(cs[:, None] + CMP_BLOCK, ss[None, :] + SEL_BLOCK)
                 - np.maximum(cs[:, None], ss[None, :]), 0, None) / CMP_STRIDE
    ovt = np.zeros((n_slc, LANES), np.float32)
    ovt[:, :n_cmp] = ov.T
    et = (np.arange(T)[:, None] // SEL_BLOCK == np.arange(LANES)[None, :]).astype(np.float32) * (-NEG)
    return jnp.asarray(ovt), jnp.asarray(et, dtype=BF16)


def _compress_weights(pe_cmp, w_cmp1, w_cmp2):
    half = CMP_BLOCK // 2
    assert N_KV_GROUPS == 2

    def blockdiag(w):
        z = jnp.zeros_like(w)
        return jnp.concatenate([jnp.concatenate([w, z], axis=-1), jnp.concatenate([z, w], axis=-1)], axis=-2)

    w1bd = blockdiag(w_cmp1.reshape(2, CMP_BLOCK, N_HEAD_DIM, CMP_HIDDEN).astype(BF16))
    w2bd = blockdiag(w_cmp2)
    pe2 = jnp.tile(pe_cmp, (1, 1, N_KV_GROUPS))
    return pe2, w1bd[:, :half].astype(BF16), w1bd[:, half:].astype(BF16), w2bd.astype(BF16)


def kernel(x, g_norm_mix, w_in, b_in, b_fgate, conv_qk, g_mlstm_head, pe_cmp, w_cmp1, w_cmp2,
           rel_bias, w_branch, w_out, g_norm_ffn, w_gate, w_up, w_down, g_final):
    B, T, D = x.shape
    assert D == D_MODEL and w_in.shape[0] == 1, "one residual block (DEPTH == 1)"
    N = B * T
    x2 = x.reshape(N, D)

    idx, scale = _inproj_perm()
    w_r = _wprep_t(w_in[0].T, g_norm_mix[0].reshape(1, D), idx, scale)
    b_r = _gather_cols(b_in[0].reshape(1, -1), idx, scale, F32)
    cw = jnp.zeros((8, 2 * M_WIDTH), F32).at[:CONV_WIDTH].set(conv_qk[0])
    mqk, mv, mo, nq, cin, nsw, mg, small = _inproj(x2, w_r, b_r)

    bf = jnp.zeros((1, LANES), F32).at[0, :M_HEADS].set(b_fgate[0])
    ltri = jnp.asarray(np.tril(np.ones((M_CHUNK, M_CHUNK), np.float32)))
    r3 = lambda a: a.reshape(B, T, a.shape[-1])
    ym = _mlstm(r3(mqk), r3(mv), r3(mo), r3(small), cw, bf, g_mlstm_head[0].reshape(1, M_WIDTH), ltri)

    pe2, w1t, w1b, w2 = _compress_weights(pe_cmp[0], w_cmp1[0], w_cmp2[0])
    ckv = _compress(cin.reshape(2, B, T, LANES), pe2, w1t, w1b, w2)
    tbl = rel_bias.astype(F32).T.reshape(-1)
    nqt = T // TQ
    bt = _bias_tiles(tbl, nqt + 1, "tok")
    wb = _bias_tiles(tbl, min(WINDOW // TQ + 1, nqt) + 1, "win")
    cb = _bias_tiles(tbl, nqt, "cmp", n_cmp=(T - CMP_BLOCK) // CMP_STRIDE + 1)
    ov, emat = _nsa_constants(T)
    yn = _nsa(r3(nq), r3(nsw), ckv, r3(small), bt, wb, cb, ov, emat)

    wbm = w_branch[0, 0].astype(BF16)
    wbn = jnp.concatenate(
        [w_branch[0, 1, (g * N_REP + r) * N_HEAD_DIM:(g * N_REP + r + 1) * N_HEAD_DIM]
         for r in range(N_REP) for g in range(N_KV_GROUPS)], axis=0).astype(BF16)
    ident = lambda n: (np.arange(n, dtype=np.int32), np.ones((n,), np.float32))
    g_ffn = g_norm_ffn[0].reshape(D, 1)
    out = _merge(x2, ym.reshape(N, M_WIDTH), yn.reshape(N, N_WIDTH), mg, wbm, wbn,
                 w_out[0].astype(BF16), _wprep(w_gate[0], g_ffn, *ident(D_FF)),
                 _wprep(w_up[0], g_ffn, *ident(D_FF)),
                 _wprep(w_down[0], jnp.ones((D_FF, 1), F32), *ident(D)),
                 g_final.reshape(1, D))
    return out.reshape(B, T, D)
```

```python
import functools

import numpy as np
import jax
import jax.numpy as jnp
from jax import lax
from jax.experimental import pallas as pl
from jax.experimental.pallas import tpu as pltpu

F32 = jnp.float32
BF16 = jnp.bfloat16
HI = lax.Precision.HIGHEST

D_MODEL = 1024
M_HEADS = 4
M_HEAD_DIM = 128
M_WIDTH = M_HEADS * M_HEAD_DIM
M_CHUNK = 128
CONV_WIDTH = 4
N_HEADS = 8
N_KV_GROUPS = 2
N_REP = N_HEADS // N_KV_GROUPS
N_HEAD_DIM = 64
N_WIDTH = N_HEADS * N_HEAD_DIM
N_KV_WIDTH = N_KV_GROUPS * N_HEAD_DIM
CMP_BLOCK = 32
CMP_STRIDE = 16
CMP_HIDDEN = 2 * N_HEAD_DIM
SEL_BLOCK = 64
SEL_TOPK = 16
WINDOW = 512
REL_BUCKETS = 32
REL_MAX_DIST = 1024
N_BRANCH = 2
D_FF = 2816
RMS_EPS = 1e-6
BIG = 1e9
NEG = -1e30

LANES = 128
TQ = 128
NSA_BATCH = 1
VMEM_LIMIT = 56 * 1024 * 1024

_OFF_MQ, _OFF_MK, _OFF_MV, _OFF_MO = 0, 512, 1024, 1536
_OFF_MI, _OFF_MF, _OFF_NQ, _OFF_NKV = 2048, 2052, 2056, 2568
_OFF_NGATE, _OFF_MERGE, _D_IN = 3336, 3360, 5408
_SMALL_NGATE = 8

_SEGS = (("mqk", 1024), ("mv", 512), ("mo", 512), ("nq", 512), ("kc", 128),
         ("vc", 128), ("nsw", 512), ("mg", 2048), ("small", 128))
_D_IN_PAD = sum(w for _, w in _SEGS)
LOG2E = 1.4426950408889634


def _dot(a, b, **kw):
    return jnp.dot(a, b, preferred_element_type=F32, **kw)


def _dot_nt(a, b):
    return lax.dot_general(a, b, (((1,), (1,)), ((), ())), preferred_element_type=F32)


def _dot_tn(a, b):
    return lax.dot_general(a, b, (((0,), (0,)), ((), ())), preferred_element_type=F32)


def _const_spec(shape):
    nd = len(shape)
    return pl.BlockSpec(shape, lambda *_: (0,) * nd, pipeline_mode=pl.Buffered(1))


def _bucket_thresholds():
    max_exact = REL_BUCKETS // 2
    assert REL_MAX_DIST == 64 * max_exact and REL_BUCKETS - max_exact == 16
    thr = []
    for k in range(1, REL_BUCKETS - max_exact):
        t = max_exact
        while t ** 8 < (max_exact ** 8) * (2 ** (3 * k)):
            t += 1
        thr.append(t)
    return tuple(thr)


_BUCKET_THR = _bucket_thresholds()


def _inproj_perm():
    idx = np.zeros((_D_IN_PAD,), np.int32)
    scale = np.zeros((_D_IN_PAD,), np.float32)
    pos = 0

    def put(cols, s=1.0):
        nonlocal pos
        n = len(cols)
        idx[pos:pos + n] = cols
        scale[pos:pos + n] = s
        pos += n

    put(np.arange(_OFF_MQ, _OFF_MV))
    put(np.arange(_OFF_MV, _OFF_MO))
    put(np.arange(_OFF_MO, _OFF_MI))
    nq = np.zeros((N_WIDTH,), np.int32)
    for r in range(N_REP):
        for g in range(N_KV_GROUPS):
            for d in range(N_HEAD_DIM):
                nq[r * 128 + g * 64 + d] = _OFF_NQ + (g * N_REP + r) * N_HEAD_DIM + d
    put(nq, N_HEAD_DIM ** -0.5 * LOG2E)
    kv = lambda j, g: np.arange(_OFF_NKV + (j * N_KV_GROUPS + g) * N_HEAD_DIM,
                                _OFF_NKV + (j * N_KV_GROUPS + g + 1) * N_HEAD_DIM)
    put(np.arange(_OFF_NKV, _OFF_NKV + 256))
    put(np.concatenate([kv(2, 0), kv(2, 1)]))
    put(np.concatenate([kv(4, 0), kv(4, 1)]))
    put(np.concatenate([kv(3, 0), kv(3, 1)]))
    put(np.concatenate([kv(5, 0), kv(5, 1)]))
    put(np.arange(_OFF_MERGE, _D_IN))
    put(np.arange(_OFF_MI, _OFF_MI + 8))
    put(np.arange(_OFF_NGATE, _OFF_NGATE + 24))
    pos += LANES - 32
    assert pos == _D_IN_PAD
    return idx, scale


def _gather_cols(a, idx, scale, dtype):
    pieces = []
    start = 0
    n = len(idx)
    for c in range(1, n + 1):
        same = c < n and scale[c] == scale[start] and (scale[c] == 0.0 or idx[c] == idx[c - 1] + 1)
        if not same:
            if scale[start] == 0.0:
                piece = jnp.zeros((a.shape[0], c - start), dtype)
            else:
                piece = a[:, int(idx[start]):int(idx[start]) + (c - start)]
                if scale[start] != 1.0:
                    piece = piece * float(scale[start])
            pieces.append(piece.astype(dtype))
            start = c
    return jnp.concatenate(pieces, axis=1)


def _col_runs(idx, scale):
    runs = []
    start = 0
    n = len(idx)
    for c in range(1, n + 1):
        same = c < n and scale[c] == scale[start] and (scale[c] == 0.0 or idx[c] == idx[c - 1] + 1)
        if not same:
            runs.append((start, int(idx[start]), c - start, float(scale[start])))
            start = c
    return runs


def _wprep_kernel(w_ref, g_ref, o_ref, *, runs):
    g = g_ref[...]
    for dst, src, n, scale in runs:
        if scale == 0.0:
            o_ref[:, dst:dst + n] = jnp.zeros((o_ref.shape[0], n), o_ref.dtype)
        else:
            o_ref[:, dst:dst + n] = (w_ref[:, src:src + n] * (g * scale)).astype(o_ref.dtype)


def _wprep_t_kernel(wt_ref, g_ref, o_ref, *, runs):
    g = g_ref[...]
    for dst, src, n, scale in runs:
        if scale == 0.0:
            o_ref[dst:dst + n, :] = jnp.zeros((n, o_ref.shape[1]), o_ref.dtype)
        else:
            o_ref[dst:dst + n, :] = (wt_ref[src:src + n, :] * (g * scale)).astype(o_ref.dtype)


def _wprep_t(wt, g, idx, scale, tc=256):
    rows, cols = wt.shape
    assert cols % tc == 0
    return pl.pallas_call(
        functools.partial(_wprep_t_kernel, runs=_col_runs(idx, scale)),
        out_shape=jax.ShapeDtypeStruct((len(idx), cols), BF16), grid=(cols // tc,),
        in_specs=[pl.BlockSpec((rows, tc), lambda i: (0, i)), pl.BlockSpec((1, tc), lambda i: (0, i))],
        out_specs=pl.BlockSpec((len(idx), tc), lambda i: (0, i)), name="wprep_t",
        compiler_params=pltpu.CompilerParams(dimension_semantics=("arbitrary",),
                                             vmem_limit_bytes=VMEM_LIMIT),
    )(wt, g)


def _wprep(w, g, idx, scale, tr=256):
    rows, cols = w.shape
    assert rows % tr == 0
    return pl.pallas_call(
        functools.partial(_wprep_kernel, runs=_col_runs(idx, scale)),
        out_shape=jax.ShapeDtypeStruct((rows, len(idx)), BF16), grid=(rows // tr,),
        in_specs=[pl.BlockSpec((tr, cols), lambda i: (i, 0)), pl.BlockSpec((tr, 1), lambda i: (i, 0))],
        out_specs=pl.BlockSpec((tr, len(idx)), lambda i: (i, 0)), name="wprep",
        compiler_params=pltpu.CompilerParams(dimension_semantics=("arbitrary",),
                                             vmem_limit_bytes=VMEM_LIMIT),
    )(w, g)


def _inproj_kernel(x_ref, w_ref, b_ref, mqk_ref, mv_ref, mo_ref, nq_ref,
                   cin_ref, nsw_ref, mg_ref, small_ref):
    x = x_ref[...]
    rs = lax.rsqrt(jnp.mean(x * x, axis=-1, keepdims=True) + RMS_EPS)
    xb = x.astype(BF16)

    def seg(a, n):
        return _dot_nt(xb, w_ref[a:a + n, :]) * rs + b_ref[:, a:a + n]

    off = {}
    pos = 0
    for name, w in _SEGS:
        off[name] = pos
        pos += w

    mqk_ref[...] = seg(off["mqk"], 1024).astype(BF16)
    mv_ref[...] = seg(off["mv"], 512).astype(BF16)
    mo_ref[...] = seg(off["mo"], 512).astype(BF16)
    nq_ref[...] = seg(off["nq"], 512).astype(BF16)
    cc = seg(off["kc"], 2 * LANES)
    cin_ref[0] = cc[:, :LANES]
    cin_ref[1] = cc[:, LANES:]
    nsw_ref[:, 0:2 * LANES] = seg(off["nsw"], 2 * LANES).astype(BF16)
    vv = seg(off["nsw"] + 2 * LANES, 2 * LANES)
    lo = lax.broadcasted_iota(jnp.int32, (x.shape[0], LANES), 1) < N_HEAD_DIM
    for j in range(2):
        vj = vv[:, j * LANES:(j + 1) * LANES]
        nsw_ref[:, (2 + 2 * j) * LANES:(3 + 2 * j) * LANES] = jnp.where(lo, vj, 1.0).astype(BF16)
        nsw_ref[:, (3 + 2 * j) * LANES:(4 + 2 * j) * LANES] = jnp.where(lo, 1.0, vj).astype(BF16)
    mg_ref[...] = seg(off["mg"], 2048).astype(BF16)
    small_ref[...] = seg(off["small"], 128)


def _inproj(x2, w, b, tm=1024):
    n = x2.shape[0]
    assert n % tm == 0
    row = lambda width: pl.BlockSpec((tm, width), lambda i: (i, 0))
    out_shape = (
        jax.ShapeDtypeStruct((n, 1024), BF16), jax.ShapeDtypeStruct((n, 512), BF16),
        jax.ShapeDtypeStruct((n, 512), BF16), jax.ShapeDtypeStruct((n, 512), BF16),
        jax.ShapeDtypeStruct((2, n, 128), F32), jax.ShapeDtypeStruct((n, 768), BF16),
        jax.ShapeDtypeStruct((n, 2048), BF16), jax.ShapeDtypeStruct((n, 128), F32))
    out_specs = (row(1024), row(512), row(512), row(512),
                 pl.BlockSpec((2, tm, 128), lambda i: (0, i, 0)), row(768), row(2048), row(128))
    return pl.pallas_call(
        _inproj_kernel, out_shape=out_shape, grid=(n // tm,),
        in_specs=[row(D_MODEL), _const_spec((_D_IN_PAD, D_MODEL)), _const_spec((1, _D_IN_PAD))],
        out_specs=out_specs, name="inproj",
        compiler_params=pltpu.CompilerParams(dimension_semantics=("arbitrary",),
                                             vmem_limit_bytes=VMEM_LIMIT),
    )(x2, w, b)


def _mlstm_kernel(mqk_ref, mv_ref, mo_ref, small_ref, cw_ref, bfc_ref, gh_ref, ltri_ref,
                  ym_ref, ebuf, c_st, m_st, gx, g_bc, g_d, g_a, g_mrow, g_blast, g_amax):
    L = M_CHUNK
    H = M_HEADS
    c = pl.program_id(1)
    n_chunks = gx.shape[0] // 8
    lanes = lax.broadcasted_iota(jnp.int32, gx.shape, 1)

    @pl.when(c == 0)
    def _():
        ebuf[0:8, :] = jnp.zeros((8, 2 * M_WIDTH), F32)
        c_st[...] = jnp.zeros_like(c_st)
        m_st[...] = jnp.zeros_like(m_st)
        for cc in range(n_chunks):
            gx[cc * 8:(cc + 1) * 8, :] = small_ref[0, cc * L:(cc + 1) * L, :].T[0:8, :]
        x = gx[...]
        fg = x + bfc_ref[...]
        lf = jnp.minimum(fg, 0.0) - jnp.log1p(jnp.exp(-jnp.abs(fg)))
        bc = lax.dot_general(lf, ltri_ref[...], (((1,), (1,)), ((), ())),
                             precision=HI, preferred_element_type=F32)
        ig = pltpu.roll(x, H, 0)
        blast = bc[:, L - 1:L]
        a_all = blast - bc + ig
        d = ig - bc
        cm = d
        sh = 1
        while sh < L:
            cm = jnp.maximum(cm, jnp.where(lanes >= sh, pltpu.roll(cm, sh, 1), -jnp.inf))
            sh *= 2
        g_bc[...] = bc
        g_d[...] = d
        g_a[...] = a_all
        g_mrow[...] = bc + cm
        g_blast[...] = jnp.broadcast_to(blast, gx.shape)
        g_amax[...] = jnp.broadcast_to(jnp.max(a_all, axis=-1, keepdims=True), gx.shape)

    ebuf[8:8 + L, :] = mqk_ref[0].astype(F32)
    ext = ebuf[...]
    conv = cw_ref[CONV_WIDTH - 1:CONV_WIDTH, :] * ext[8:, :]
    for k in range(1, CONV_WIDTH):
        conv = conv + cw_ref[CONV_WIDTH - 1 - k:CONV_WIDTH - k, :] * pltpu.roll(ext, k, 0)[8:, :]
    ebuf[0:8, :] = ebuf[L:L + 8, :]
    qk = (conv * jax.nn.sigmoid(conv)).astype(BF16)

    r8 = pl.ds(pl.multiple_of(c * 8, 8), 8)
    bc = g_bc[r8, :]
    dt = g_d[r8, :]
    blast = g_blast[r8, :]
    m_prev = m_st[...]
    m_new = jnp.maximum(blast + m_prev, g_amax[r8, :])
    decay = jnp.exp(blast + m_prev - m_new)
    w_row = jnp.exp(g_a[r8, :] - m_new)
    log_inter = bc + m_prev
    m_i = jnp.maximum(g_mrow[r8, :], log_inter)
    k_scale = M_HEAD_DIM ** -0.5
    s_inter = jnp.exp(log_inter - m_i) * k_scale
    emi = jnp.exp(-m_i)
    u = bc - m_i + float(np.log(k_scale))
    cols = jnp.concatenate([u, s_inter, emi, w_row, jnp.zeros((LANES - 32, L), F32)], axis=0).T
    u_all, s_inter_all = cols[:, 0:8], cols[:, 8:16]
    emi_all, w_all = cols[:, 16:24], cols[:, 24:32]

    row = lax.broadcasted_iota(jnp.int32, (L, L), 0)
    col = lax.broadcasted_iota(jnp.int32, (L, L), 1)
    causal = row >= col

    ones = jnp.ones((L, M_HEAD_DIM), BF16)
    for h in range(M_HEADS):
        sl = slice(h * M_HEAD_DIM, (h + 1) * M_HEAD_DIM)
        qb = qk[:, sl]
        kb = qk[:, M_WIDTH + h * M_HEAD_DIM:M_WIDTH + (h + 1) * M_HEAD_DIM]
        v = mv_ref[0, :, sl]
        cn_prev = c_st[h]

        g = H + h
        logp = jnp.where(causal, u_all[:, g:g + 1] + dt[g:g + 1, :], -jnp.inf)
        p = jnp.exp(logp) * _dot_nt(qb, kb)
        nd = (_dot(p.astype(BF16), jnp.concatenate([v, ones], axis=1))
              + s_inter_all[:, g:g + 1] * _dot(qb, cn_prev.astype(BF16)))
        hh = nd[:, :M_HEAD_DIM] / jnp.maximum(jnp.abs(nd[:, M_HEAD_DIM:]), emi_all[:, g:g + 1])
        hh = hh * lax.rsqrt(jnp.mean(hh * hh, axis=-1, keepdims=True) + RMS_EPS) * gh_ref[:, sl]
        ym_ref[0, :, sl] = (jax.nn.sigmoid(mo_ref[0, :, sl].astype(F32)) * hh).astype(ym_ref.dtype)

        w_col = w_all[:, g:g + 1]
        vw = jnp.concatenate([v.astype(F32) * w_col, jnp.broadcast_to(w_col, (L, M_HEAD_DIM))], axis=1)
        c_st[h] = decay[g:g + 1, 0:1] * cn_prev + _dot_tn(kb, vw.astype(BF16))

    m_st[...] = m_new


def _mlstm(mqk, mv, mo, small, cw, bf, gh, ltri):
    B, T, _ = mqk.shape
    L = M_CHUNK
    assert T % L == 0
    nc = T // L
    assert L == LANES and 8 * nc <= LANES
    blk = lambda w: pl.BlockSpec((1, L, w), lambda b, c: (b, c, 0))
    gate = lambda: pltpu.VMEM((8 * nc, L), F32)
    return pl.pallas_call(
        _mlstm_kernel, out_shape=jax.ShapeDtypeStruct((B, T, M_WIDTH), BF16),
        grid=(B, nc),
        in_specs=[blk(2 * M_WIDTH), blk(M_WIDTH), blk(M_WIDTH),
                  pl.BlockSpec((1, T, LANES), lambda b, c: (b, 0, 0)),
                  _const_spec((8, 2 * M_WIDTH)), _const_spec((8 * nc, L)),
                  _const_spec((1, M_WIDTH)), _const_spec((L, L))],
        out_specs=blk(M_WIDTH),
        scratch_shapes=[pltpu.VMEM((L + 8, 2 * M_WIDTH), F32),
                        pltpu.VMEM((M_HEADS, M_HEAD_DIM, 2 * M_HEAD_DIM), F32),
                        pltpu.VMEM((8, L), F32),
                        gate(), gate(), gate(), gate(), gate(), gate(), gate()],
        name="mlstm",
        compiler_params=pltpu.CompilerParams(dimension_semantics=("arbitrary", "arbitrary")),
    )(mqk, mv, mo, small, cw, bf, gh, ltri)


def _compress_kernel(x_ref, pe_ref, w1t_ref, w1b_ref, w2_ref, o_ref, *, nch, nbt):
    half = CMP_BLOCK // 2
    rows = nbt * nch
    top = jnp.zeros((rows, 2 * CMP_HIDDEN), F32)
    bot = jnp.zeros((rows, 2 * CMP_HIDDEN), F32)
    for p in range(half):
        xp = jnp.concatenate([x_ref[0, e, pl.ds(p, nch, stride=CMP_STRIDE), :] for e in range(nbt)], axis=0)
        top = top + _dot((xp + pe_ref[0, p:p + 1, :]).astype(BF16), w1t_ref[0, p])
        bot = bot + _dot((xp + pe_ref[0, half + p:half + p + 1, :]).astype(BF16), w1b_ref[0, p])
    hid = top + pltpu.roll(bot, rows - 1, 0)
    act = hid * jax.nn.sigmoid(hid)
    out = _dot(act.astype(BF16), w2_ref[0]).astype(o_ref.dtype)
    for e in range(nbt):
        o_ref[0, e] = out[e * nch:(e + 1) * nch]


def _compress(cin, pe2, w1t, w1b, w2):
    _, B, T, _ = cin.shape
    nch = T // CMP_STRIDE
    nbt = 4 if B % 4 == 0 else 1
    sel = lambda *shape: pl.BlockSpec((1,) + shape, lambda j, b: (j,) + (0,) * len(shape))
    return pl.pallas_call(
        functools.partial(_compress_kernel, nch=nch, nbt=nbt),
        out_shape=jax.ShapeDtypeStruct((2, B, nch, LANES), BF16),
        grid=(2, B // nbt),
        in_specs=[pl.BlockSpec((1, nbt, T, LANES), lambda j, b: (j, b, 0, 0)),
                  sel(CMP_BLOCK, LANES), sel(CMP_BLOCK // 2, LANES, 2 * CMP_HIDDEN),
                  sel(CMP_BLOCK // 2, LANES, 2 * CMP_HIDDEN), sel(2 * CMP_HIDDEN, LANES)],
        out_specs=pl.BlockSpec((1, nbt, nch, LANES), lambda j, b: (j, b, 0, 0)),
        name="compress",
        compiler_params=pltpu.CompilerParams(dimension_semantics=("arbitrary", "arbitrary")),
    )(cin, pe2, w1t, w1b, w2)


def _bias_kernel(tbl_ref, o_ref, *, kind, n_cmp):
    pid = pl.program_id(0)
    toeplitz = kind != "cmp"
    if toeplitz:
        k = lax.broadcasted_iota(jnp.int32, (8, 2 * LANES), 1)
        dist = (pid - 1) * TQ + jnp.where(k < LANES, -k, 2 * LANES - k)
    else:
        al = lax.broadcasted_iota(jnp.int32, (CMP_STRIDE, 2 * LANES), 0)
        m = lax.broadcasted_iota(jnp.int32, (CMP_STRIDE, 2 * LANES), 1)
        dist = pid * TQ - (CMP_BLOCK - 1) + al - CMP_STRIDE * jnp.where(m < LANES, m, m - 2 * LANES)
    n = jnp.maximum(dist, 0)
    cnt = jnp.zeros_like(n)
    for t in _BUCKET_THR:
        cnt = cnt + jnp.where(n >= t, 1, 0)
    bucket = jnp.where(n < REL_BUCKETS // 2, n, REL_BUCKETS // 2 + cnt)
    if kind == "tok":
        madd = jnp.where((dist >= 0) & (pid > 0), 0.0, NEG)
    elif kind == "win":
        madd = jnp.where((dist >= 0) & (dist < WINDOW) & (pid > 0), 0.0, NEG)
    else:
        madd = jnp.where(dist >= 0, 0.0, NEG)
        valid_c = lax.broadcasted_iota(jnp.int32, (TQ, LANES), 1) < n_cmp
    for h in range(N_HEADS):
        val = jnp.zeros(dist.shape, F32)
        for bb in range(REL_BUCKETS):
            val = jnp.where(bucket == bb, tbl_ref[h * REL_BUCKETS + bb], val)
        val = val * LOG2E + madd
        if toeplitz:
            g = jnp.broadcast_to(val[0:1, :], (TQ, 2 * LANES))
            val = pltpu.roll(g, 0, 1, stride=1, stride_axis=0)[:, :LANES]
        else:
            bands = [val[:, :LANES]] + [pltpu.roll(val, ah, 1)[:, :LANES] for ah in range(1, TQ // CMP_STRIDE)]
            val = jnp.where(valid_c, jnp.concatenate(bands, axis=0), NEG)
        o_ref[0, h * TQ:(h + 1) * TQ, :] = val


def _bias_tiles(tbl, n_tiles, kind, n_cmp=0):
    return pl.pallas_call(
        functools.partial(_bias_kernel, kind=kind, n_cmp=n_cmp),
        out_shape=jax.ShapeDtypeStruct((n_tiles, N_HEADS * TQ, LANES), F32),
        grid=(n_tiles,),
        in_specs=[pl.BlockSpec(memory_space=pltpu.SMEM)],
        out_specs=pl.BlockSpec((1, N_HEADS * TQ, LANES), lambda i: (i, 0, 0)),
        name="bias_" + kind,
        compiler_params=pltpu.CompilerParams(dimension_semantics=("arbitrary",)),
    )(tbl)


def _nsa_kernel(nq_ref, nq4_ref, ks_ref, kw_ref, vs0_ref, vs1_ref, vw0_ref, vw1_ref, kc_ref, vc_ref,
                small_ref, small4_ref, bt_ref, wb_ref, cb4_ref, ovt_ref, et_ref, yn_ref,
                lhs_sc, z_sc, zw_sc, seln_sc, partc_sc, *, nb, n_slc, n_top, n_win, n_chunks_max):
    qi = pl.program_id(1)
    HR = N_HEADS * TQ
    GR = N_REP * TQ
    CH = 4 * TQ
    G4 = 4
    vs_refs = (vs0_ref, vs1_ref)
    vw_refs = (vw0_ref, vw1_ref)
    lane = lax.broadcasted_iota(jnp.int32, (TQ, LANES), 1)
    lo = lane < N_HEAD_DIM

    def stack_heads(q_all):
        zero = jnp.zeros((TQ, LANES), BF16)
        parts = []
        for g in range(N_KV_GROUPS):
            for r in range(N_REP):
                qr = q_all[:, r * LANES:(r + 1) * LANES]
                parts.append(jnp.where(lo if g == 0 else jnp.logical_not(lo), qr, zero))
        return parts

    def pair(o, r):
        return jnp.where(lo, o[r * TQ:(r + 1) * TQ], o[(N_REP + r) * TQ:(N_REP + r + 1) * TQ])

    def normed(acc, r):
        den = jnp.where(lo, acc[(N_REP + r) * TQ:(N_REP + r + 1) * TQ], acc[r * TQ:(r + 1) * TQ])
        return pair(acc, r) / pltpu.roll(den, N_HEAD_DIM, 1)

    def gate(sg, r, ci):
        c0 = _SMALL_NGATE + r * 3 + ci
        c1 = _SMALL_NGATE + (N_REP + r) * 3 + ci
        return jnp.where(lo, sg[:, c0:c0 + 1], sg[:, c1:c1 + 1])

    w0 = jnp.maximum(qi - (n_win - 1), 0)
    woff = pl.multiple_of(w0 * TQ, TQ)

    def group_prep(s, n_chunks):
        qs4 = jnp.concatenate(
            [p for j in range(G4) for p in stack_heads(nq4_ref[s, j * TQ:(j + 1) * TQ, :])], axis=0)
        rows = lax.broadcasted_iota(jnp.int32, (G4 * HR, 1), 0)
        t_rows = (qi + jnp.right_shift(rows, 10)) * TQ + (rows & (TQ - 1))

        z = _dot_nt(qs4, kc_ref[0, s]) + cb4_ref[...].reshape(G4 * HR, LANES)
        e = jnp.exp2(z - jnp.max(z, axis=-1, keepdims=True))
        l = _dot(e.astype(BF16), jnp.ones((LANES, LANES), BF16))
        p_c = e * jnp.where(t_rows >= CMP_BLOCK - 1, 1.0 / jnp.maximum(l, 1e-30), 0.0)
        o_c = _dot(p_c.astype(BF16), vc_ref[0, s])
        sg4 = jax.nn.sigmoid(small4_ref[s])
        for j in range(G4):
            for r in range(N_REP):
                partc_sc[s, j, r] = (gate(sg4[j * TQ:(j + 1) * TQ], r, 0)
                                     * pair(o_c[j * HR:(j + 1) * HR], r))

        W = G4 * N_KV_GROUPS * TQ
        jb = lax.broadcasted_iota(jnp.int32, (n_slc, W), 0)
        col = lax.broadcasted_iota(jnp.int32, (n_slc, W), 1)
        tq = (qi + jnp.right_shift(col, 8)) * TQ + (col & (TQ - 1))
        cur = jnp.right_shift(tq, 6)
        elig = jb <= cur
        if n_chunks * CH <= n_top * SEL_BLOCK:
            sel = jnp.where(elig, 1.0, 0.0)
        else:
            forced = (jb == 0) | (jb == cur) | (jb == cur - 1)
            psums = []
            for j in range(G4):
                for g in range(N_KV_GROUPS):
                    base = j * HR + g * GR
                    ps = p_c[base:base + TQ]
                    for r in range(1, N_REP):
                        ps = ps + p_c[base + r * TQ:base + (r + 1) * TQ]
                    psums.append(ps)
            imp = lax.dot_general(ovt_ref[...], jnp.concatenate(psums, axis=0), (((1,), (1,)), ((), ())),
                                  precision=HI, preferred_element_type=F32)
            score = jnp.where(elig, jnp.where(forced, BIG, imp), -BIG)
            cnt = jnp.zeros((n_slc, W), F32)
            for i in range(n_slc):
                si = score[i:i + 1, :]
                tie = jnp.where(jb > i, 1.0, 0.0)
                cnt = cnt + jnp.where(si > score, 1.0, jnp.where(si == score, tie, 0.0))
            sel = jnp.where((cnt < n_top) & (score > -BIG / 2), 1.0, 0.0)
        seln = jnp.concatenate([sel - 1.0, jnp.zeros((LANES - n_slc, W), F32)], axis=0).T.astype(BF16)
        for j in range(G4):
            for g in range(N_KV_GROUPS):
                seln_sc[s, j, g] = seln[(j * N_KV_GROUPS + g) * TQ:(j * N_KV_GROUPS + g + 1) * TQ]

    def tiles(n_pairs):
        jq = qi & (G4 - 1)
        elems = range(nb)
        cat = lambda xs: xs[0] if len(xs) == 1 else jnp.concatenate(xs, axis=0)
        rep = lambda t: t if nb == 1 else jnp.concatenate([t] * nb, axis=0)
        parts = [stack_heads(nq_ref[s]) for s in elems]
        qs = [jnp.concatenate(parts[s], axis=0) for s in elems]

        def bias_tiles(ref, first, count, tile0):
            tiles_ = [ref[jnp.maximum(qi - tile0 - (first + u) + 1, 0)] for u in range(count)]
            return rep(tiles_[0] if count == 1 else jnp.concatenate(tiles_, axis=1))

        zmax = None
        for j0 in range(0, n_win, 2):
            nj = min(2, n_win - j0)
            zp = (cat([_dot_nt(qs[s], kw_ref[s, pl.ds(woff + j0 * TQ, nj * TQ), :]) for s in elems])
                  + bias_tiles(wb_ref, j0, nj, w0))
            zw_sc[:, j0 * TQ:(j0 + nj) * TQ] = zp
            for u in range(nj):
                zj = zp[:, u * TQ:(u + 1) * TQ]
                zmax = zj if zmax is None else jnp.maximum(zmax, zj)
        mw = jnp.max(zmax, axis=-1, keepdims=True)
        pw = jnp.exp2(zw_sc[...] - mw).astype(BF16)
        acc_w = cat([_dot(pw[s * HR + g * GR:s * HR + (g + 1) * GR],
                          vw_refs[g][s, pl.ds(woff, n_win * TQ), :])
                     for s in elems for g in range(N_KV_GROUPS)])

        for s in elems:
            for g in range(N_KV_GROUPS):
                seln = seln_sc[s, jq, g]
                for r in range(N_REP):
                    h = g * N_REP + r
                    lhs_sc[s, h * TQ:(h + 1) * TQ, :] = jnp.concatenate([parts[s][h], seln], axis=1)

        zmax = None
        for pi in range(n_pairs):
            k0 = pi * 2 * TQ
            zp = (cat([_dot_nt(lhs_sc[s], jnp.concatenate([ks_ref[s, k0:k0 + 2 * TQ, :],
                                                           et_ref[k0:k0 + 2 * TQ, :]], axis=1))
                       for s in elems])
                  + bias_tiles(bt_ref, 2 * pi, 2, 0))
            z_sc[pi // 2, :, (pi % 2) * 2 * TQ:(pi % 2 + 1) * 2 * TQ] = zp
            for u in range(2):
                zt = zp[:, u * TQ:(u + 1) * TQ]
                zmax = zt if zmax is None else jnp.maximum(zmax, zt)
        ms = jnp.max(zmax, axis=-1, keepdims=True)
        acc = [None] * (nb * N_KV_GROUPS)
        for c in range((n_pairs + 1) // 2):
            width = min(CH, n_pairs * 2 * TQ - c * CH)
            p = jnp.exp2(z_sc[c, :, 0:width] - ms).astype(BF16)
            for s in elems:
                for g in range(N_KV_GROUPS):
                    i = s * N_KV_GROUPS + g
                    d = _dot(p[i * GR:(i + 1) * GR], vs_refs[g][s, c * CH:c * CH + width, :])
                    acc[i] = d if acc[i] is None else acc[i] + d
        acc_s = cat(acc)

        for s in elems:
            sg = jax.nn.sigmoid(small_ref[s])
            a_s = acc_s[s * HR:(s + 1) * HR]
            a_w = acc_w[s * HR:(s + 1) * HR]
            for r in range(N_REP):
                out = (partc_sc[s, jq, r] + gate(sg, r, 1) * normed(a_s, r)
                       + gate(sg, r, 2) * normed(a_w, r))
                yn_ref[s, :, r * LANES:(r + 1) * LANES] = out.astype(yn_ref.dtype)

    for nc in range(1, n_chunks_max + 1):
        @pl.when(qi // G4 == nc - 1)
        def _(nc=nc):
            @pl.when(qi % G4 == 0)
            def _():
                for s in range(nb):
                    group_prep(s, nc)

            tiles(2 * nc)


def _nsa(nq, nsw, ckv, small, bt, wb, cb, ov, emat):
    B, T, _ = nq.shape
    assert T == 2048, "single 128-wide compressed-key tile assumes T == 2048"
    nqt = T // TQ
    n_cmp = (T - CMP_BLOCK) // CMP_STRIDE + 1
    n_slc = T // SEL_BLOCK
    n_top = min(SEL_TOPK, n_slc)
    n_win = wb.shape[0] - 1
    HR = N_HEADS * TQ
    assert nqt % 4 == 0 and n_slc % 8 == 0 and n_slc <= LANES
    nb = NSA_BATCH if B % NSA_BATCH == 0 else 1
    n_chunks_max = nqt // 4
    kv = lambda j: pl.BlockSpec((nb, T, LANES), lambda b, q: (b, 0, j))
    ck = lambda j: pl.BlockSpec((1, nb, T // CMP_STRIDE, LANES), lambda b, q: (j, b, 0, 0))
    kern = functools.partial(_nsa_kernel, nb=nb, n_slc=n_slc, n_top=n_top, n_win=n_win,
                             n_chunks_max=n_chunks_max)
    return pl.pallas_call(
        kern, out_shape=jax.ShapeDtypeStruct((B, T, N_WIDTH), BF16),
        grid=(B // nb, nqt),
        in_specs=[pl.BlockSpec((nb, TQ, N_WIDTH), lambda b, q: (b, q, 0)),
                  pl.BlockSpec((nb, 4 * TQ, N_WIDTH), lambda b, q: (b, q // 4, 0)),
                  kv(0), kv(1), kv(2), kv(3), kv(4), kv(5), ck(0), ck(1),
                  pl.BlockSpec((nb, TQ, LANES), lambda b, q: (b, q, 0)),
                  pl.BlockSpec((nb, 4 * TQ, LANES), lambda b, q: (b, q // 4, 0)),
                  _const_spec(bt.shape), _const_spec(wb.shape),
                  pl.BlockSpec((4, HR, LANES), lambda b, q: (q // 4, 0, 0)),
                  _const_spec(ov.shape), _const_spec(emat.shape)],
        out_specs=pl.BlockSpec((nb, TQ, N_WIDTH), lambda b, q: (b, q, 0)),
        scratch_shapes=[pltpu.VMEM((nb, HR, 2 * LANES), BF16),
                        pltpu.VMEM((n_chunks_max, nb * HR, 4 * TQ), F32),
                        pltpu.VMEM((nb * HR, n_win * TQ), F32),
                        pltpu.VMEM((nb, 4, N_KV_GROUPS, TQ, LANES), BF16),
                        pltpu.VMEM((nb, 4, N_REP, TQ, LANES), F32)],
        name="nsa",
        compiler_params=pltpu.CompilerParams(dimension_semantics=("arbitrary", "arbitrary"),
                                             vmem_limit_bytes=VMEM_LIMIT),
    )(nq, nq, nsw, nsw, nsw, nsw, nsw, nsw, ckv, ckv, small, small, bt, wb, cb, ov, emat)


def _merge_kernel(x_ref, ym_ref, yn_ref, mg_ref, wbm_ref, wbn_ref, wo_ref,
                  wg_ref, wu_ref, wd_ref, gfin_ref, o_ref, *, tf):
    bm = _dot(ym_ref[...], wbm_ref[...])
    bn = _dot(yn_ref[...], wbn_ref[...])
    mixed = (jax.nn.sigmoid(mg_ref[:, :D_MODEL].astype(F32)) * bm
             + jax.nn.sigmoid(mg_ref[:, D_MODEL:].astype(F32)) * bn)
    h = x_ref[...] + _dot(mixed.astype(BF16), wo_ref[...])
    rs = lax.rsqrt(jnp.mean(h * h, axis=-1, keepdims=True) + RMS_EPS)
    hb = h.astype(BF16)
    acc = jnp.zeros(h.shape, F32)
    for j in range(D_FF // tf):
        gg = _dot(hb, wg_ref[:, j * tf:(j + 1) * tf]) * rs
        uu = _dot(hb, wu_ref[:, j * tf:(j + 1) * tf]) * rs
        act = (gg * jax.nn.sigmoid(gg) * uu).astype(BF16)
        acc = acc + _dot(act, wd_ref[j * tf:(j + 1) * tf, :])
    h2 = h + acc
    o_ref[...] = h2 * lax.rsqrt(jnp.mean(h2 * h2, axis=-1, keepdims=True) + RMS_EPS) * gfin_ref[...]


def _merge(x2, ym, yn, mg, wbm, wbn, wo, wg, wu, wd, gfin, tm=512, tf=256):
    n = x2.shape[0]
    assert n % tm == 0 and D_FF % tf == 0
    row = lambda width: pl.BlockSpec((tm, width), lambda i: (i, 0))
    return pl.pallas_call(
        functools.partial(_merge_kernel, tf=tf),
        out_shape=jax.ShapeDtypeStruct((n, D_MODEL), F32), grid=(n // tm,),
        in_specs=[row(D_MODEL), row(M_WIDTH), row(N_WIDTH), row(N_BRANCH * D_MODEL),
                  _const_spec(wbm.shape), _const_spec(wbn.shape), _const_spec(wo.shape),
                  _const_spec(wg.shape), _const_spec(wu.shape),
                  _const_spec(wd.shape), _const_spec(gfin.shape)],
        out_specs=row(D_MODEL), name="merge_ffn",
        compiler_params=pltpu.CompilerParams(dimension_semantics=("arbitrary",),
                                             vmem_limit_bytes=VMEM_LIMIT),
    )(x2, ym, yn, mg, wbm, wbn, wo, wg, wu, wd, gfin)


def _nsa_constants(T):
    n_cmp = (T - CMP_BLOCK) // CMP_STRIDE + 1
    n_slc = T // SEL_BLOCK
    cs = np.arange(n_cmp) * CMP_STRIDE
    ss = np.arange(n_slc) * SEL_BLOCK
    ov = np.clip(np.minimum(cs[:, None] + CMP_BLOCK, ss[None, :] + SEL_BLOCK)
                 - np.maximum(cs[:, None], ss[None, :]), 0, None) / CMP_STRIDE
    ovt = np.zeros((n_slc, LANES), np.float32)
    ovt[:, :n_cmp] = ov.T
    et = (np.arange(T)[:, None] // SEL_BLOCK == np.arange(LANES)[None, :]).astype(np.float32) * (-NEG)
    return jnp.asarray(ovt), jnp.asarray(et, dtype=BF16)


def _compress_weights(pe_cmp, w_cmp1, w_cmp2):
    half = CMP_BLOCK // 2
    assert N_KV_GROUPS == 2

    def blockdiag(w):
        z = jnp.zeros_like(w)
        return jnp.concatenate([jnp.concatenate([w, z], axis=-1), jnp.concatenate([z, w], axis=-1)], axis=-2)

    w1bd = blockdiag(w_cmp1.reshape(2, CMP_BLOCK, N_HEAD_DIM, CMP_HIDDEN).astype(BF16))
    w2bd = blockdiag(w_cmp2)
    pe2 = jnp.tile(pe_cmp, (1, 1, N_KV_GROUPS))
    return pe2, w1bd[:, :half].astype(BF16), w1bd[:, half:].astype(BF16), w2bd.astype(BF16)


def kernel(x, g_norm_mix, w_in, b_in, b_fgate, conv_qk, g_mlstm_head, pe_cmp, w_cmp1, w_cmp2,
           rel_bias, w_branch, w_out, g_norm_ffn, w_gate, w_up, w_down, g_final):
    B, T, D = x.shape
    assert D == D_MODEL and w_in.shape[0] == 1, "one residual block (DEPTH == 1)"
    N = B * T
    x2 = x.reshape(N, D)

    idx, scale = _inproj_perm()
    w_r = _wprep_t(w_in[0].T, g_norm_mix[0].reshape(1, D), idx, scale)
    b_r = _gather_cols(b_in[0].reshape(1, -1), idx, scale, F32)
    cw = jnp.zeros((8, 2 * M_WIDTH), F32).at[:CONV_WIDTH].set(conv_qk[0])
    mqk, mv, mo, nq, cin, nsw, mg, small = _inproj(x2, w_r, b_r)

    bf8 = jnp.concatenate([jnp.zeros((M_HEADS,), F32), b_fgate[0].astype(F32)])
    bfc = jnp.broadcast_to(jnp.tile(bf8, T // M_CHUNK)[:, None], (8 * (T // M_CHUNK), M_CHUNK))
    ltri = jnp.asarray(np.tril(np.ones((M_CHUNK, M_CHUNK), np.float32)))
    r3 = lambda a: a.reshape(B, T, a.shape[-1])
    ym = _mlstm(r3(mqk), r3(mv), r3(mo), r3(small), cw, bfc, g_mlstm_head[0].reshape(1, M_WIDTH), ltri)

    pe2, w1t, w1b, w2 = _compress_weights(pe_cmp[0], w_cmp1[0], w_cmp2[0])
    ckv = _compress(cin.reshape(2, B, T, LANES), pe2, w1t, w1b, w2)
    tbl = rel_bias.astype(F32).T.reshape(-1)
    nqt = T // TQ
    bt = _bias_tiles(tbl, nqt + 1, "tok")
    wb = _bias_tiles(tbl, min(WINDOW // TQ + 1, nqt) + 1, "win")
    cb = _bias_tiles(tbl, nqt, "cmp", n_cmp=(T - CMP_BLOCK) // CMP_STRIDE + 1)
    ov, emat = _nsa_constants(T)
    yn = _nsa(r3(nq), r3(nsw), ckv, r3(small), bt, wb, cb, ov, emat)

    wbm = w_branch[0, 0].astype(BF16)
    wbn = jnp.concatenate(
        [w_branch[0, 1, (g * N_REP + r) * N_HEAD_DIM:(g * N_REP + r + 1) * N_HEAD_DIM]
         for r in range(N_REP) for g in range(N_KV_GROUPS)], axis=0).astype(BF16)
    ident = lambda n: (np.arange(n, dtype=np.int32), np.ones((n,), np.float32))
    g_ffn = g_norm_ffn[0].reshape(D, 1)
    out = _merge(x2, ym.reshape(N, M_WIDTH), yn.reshape(N, N_WIDTH), mg, wbm, wbn,
                 w_out[0].astype(BF16), _wprep(w_gate[0], g_ffn, *ident(D_FF)),
                 _wprep(w_up[0], g_ffn, *ident(D_FF)),
                 _wprep(w_down[0], jnp.ones((D_FF, 1), F32), *ident(D)),
                 g_final.reshape(1, D))
    return out.reshape(B, T, D)
```

```python
import functools

import numpy as np
import jax
import jax.numpy as jnp
from jax import lax
from jax.experimental import pallas as pl
from jax.experimental.pallas import tpu as pltpu

F32 = jnp.float32
BF16 = jnp.bfloat16
HI = lax.Precision.HIGHEST

D_MODEL = 1024
M_HEADS = 4
M_HEAD_DIM = 128
M_WIDTH = M_HEADS * M_HEAD_DIM
M_CHUNK = 128
CONV_WIDTH = 4
N_HEADS = 8
N_KV_GROUPS = 2
N_REP = N_HEADS // N_KV_GROUPS
N_HEAD_DIM = 64
N_WIDTH = N_HEADS * N_HEAD_DIM
N_KV_WIDTH = N_KV_GROUPS * N_HEAD_DIM
CMP_BLOCK = 32
CMP_STRIDE = 16
CMP_HIDDEN = 2 * N_HEAD_DIM
SEL_BLOCK = 64
SEL_TOPK = 16
WINDOW = 512
REL_BUCKETS = 32
REL_MAX_DIST = 1024
N_BRANCH = 2
D_FF = 2816
RMS_EPS = 1e-6
BIG = 1e9
NEG = -1e30

LANES = 128
TQ = 128
NSA_BATCH = 1
VMEM_LIMIT = 56 * 1024 * 1024

_OFF_MQ, _OFF_MK, _OFF_MV, _OFF_MO = 0, 512, 1024, 1536
_OFF_MI, _OFF_MF, _OFF_NQ, _OFF_NKV = 2048, 2052, 2056, 2568
_OFF_NGATE, _OFF_MERGE, _D_IN = 3336, 3360, 5408
_SMALL_NGATE = 8

_SEGS = (("mqk", 1024), ("mv", 512), ("mo", 512), ("nq", 512), ("kc", 128),
         ("vc", 128), ("nsw", 512), ("mg", 2048), ("small", 128))
_D_IN_PAD = sum(w for _, w in _SEGS)
LOG2E = 1.4426950408889634


def _dot(a, b, **kw):
    return jnp.dot(a, b, preferred_element_type=F32, **kw)


def _dot_nt(a, b):
    return lax.dot_general(a, b, (((1,), (1,)), ((), ())), preferred_element_type=F32)


def _dot_tn(a, b):
    return lax.dot_general(a, b, (((0,), (0,)), ((), ())), preferred_element_type=F32)


def _const_spec(shape):
    nd = len(shape)
    return pl.BlockSpec(shape, lambda *_: (0,) * nd, pipeline_mode=pl.Buffered(1))


def _bucket_thresholds():
    max_exact = REL_BUCKETS // 2
    assert REL_MAX_DIST == 64 * max_exact and REL_BUCKETS - max_exact == 16
    thr = []
    for k in range(1, REL_BUCKETS - max_exact):
        t = max_exact
        while t ** 8 < (max_exact ** 8) * (2 ** (3 * k)):
            t += 1
        thr.append(t)
    return tuple(thr)


_BUCKET_THR = _bucket_thresholds()


def _inproj_perm():
    idx = np.zeros((_D_IN_PAD,), np.int32)
    scale = np.zeros((_D_IN_PAD,), np.float32)
    pos = 0

    def put(cols, s=1.0):
        nonlocal pos
        n = len(cols)
        idx[pos:pos + n] = cols
        scale[pos:pos + n] = s
        pos += n

    put(np.arange(_OFF_MQ, _OFF_MV))
    put(np.arange(_OFF_MV, _OFF_MO))
    put(np.arange(_OFF_MO, _OFF_MI))
    nq = np.zeros((N_WIDTH,), np.int32)
    for r in range(N_REP):
        for g in range(N_KV_GROUPS):
            for d in range(N_HEAD_DIM):
                nq[r * 128 + g * 64 + d] = _OFF_NQ + (g * N_REP + r) * N_HEAD_DIM + d
    put(nq, N_HEAD_DIM ** -0.5 * LOG2E)
    kv = lambda j, g: np.arange(_OFF_NKV + (j * N_KV_GROUPS + g) * N_HEAD_DIM,
                                _OFF_NKV + (j * N_KV_GROUPS + g + 1) * N_HEAD_DIM)
    put(np.arange(_OFF_NKV, _OFF_NKV + 256))
    put(np.concatenate([kv(2, 0), kv(2, 1)]))
    put(np.concatenate([kv(4, 0), kv(4, 1)]))
    put(np.concatenate([kv(3, 0), kv(3, 1)]))
    put(np.concatenate([kv(5, 0), kv(5, 1)]))
    put(np.arange(_OFF_MERGE, _D_IN))
    put(np.arange(_OFF_MI, _OFF_MI + 8))
    put(np.arange(_OFF_NGATE, _OFF_NGATE + 24))
    pos += LANES - 32
    assert pos == _D_IN_PAD
    return idx, scale


def _gather_cols(a, idx, scale, dtype):
    pieces = []
    start = 0
    n = len(idx)
    for c in range(1, n + 1):
        same = c < n and scale[c] == scale[start] and (scale[c] == 0.0 or idx[c] == idx[c - 1] + 1)
        if not same:
            if scale[start] == 0.0:
                piece = jnp.zeros((a.shape[0], c - start), dtype)
            else:
                piece = a[:, int(idx[start]):int(idx[start]) + (c - start)]
                if scale[start] != 1.0:
                    piece = piece * float(scale[start])
            pieces.append(piece.astype(dtype))
            start = c
    return jnp.concatenate(pieces, axis=1)


def _col_runs(idx, scale):
    runs = []
    start = 0
    n = len(idx)
    for c in range(1, n + 1):
        same = c < n and scale[c] == scale[start] and (scale[c] == 0.0 or idx[c] == idx[c - 1] + 1)
        if not same:
            runs.append((start, int(idx[start]), c - start, float(scale[start])))
            start = c
    return runs


def _wprep_kernel(w_ref, g_ref, o_ref, *, runs):
    g = g_ref[...]
    for dst, src, n, scale in runs:
        if scale == 0.0:
            o_ref[:, dst:dst + n] = jnp.zeros((o_ref.shape[0], n), o_ref.dtype)
        else:
            o_ref[:, dst:dst + n] = (w_ref[:, src:src + n] * (g * scale)).astype(o_ref.dtype)


def _wprep_t_kernel(wt_ref, g_ref, o_ref, *, runs):
    g = g_ref[...]
    for dst, src, n, scale in runs:
        if scale == 0.0:
            o_ref[dst:dst + n, :] = jnp.zeros((n, o_ref.shape[1]), o_ref.dtype)
        else:
            o_ref[dst:dst + n, :] = (wt_ref[src:src + n, :] * (g * scale)).astype(o_ref.dtype)


def _wprep_t(wt, g, idx, scale, tc=256):
    rows, cols = wt.shape
    assert cols % tc == 0
    return pl.pallas_call(
        functools.partial(_wprep_t_kernel, runs=_col_runs(idx, scale)),
        out_shape=jax.ShapeDtypeStruct((len(idx), cols), BF16), grid=(cols // tc,),
        in_specs=[pl.BlockSpec((rows, tc), lambda i: (0, i)), pl.BlockSpec((1, tc), lambda i: (0, i))],
        out_specs=pl.BlockSpec((len(idx), tc), lambda i: (0, i)), name="wprep_t",
        compiler_params=pltpu.CompilerParams(dimension_semantics=("arbitrary",),
                                             vmem_limit_bytes=VMEM_LIMIT),
    )(wt, g)


def _wprep(w, g, idx, scale, tr=256):
    rows, cols = w.shape
    assert rows % tr == 0
    return pl.pallas_call(
        functools.partial(_wprep_kernel, runs=_col_runs(idx, scale)),
        out_shape=jax.ShapeDtypeStruct((rows, len(idx)), BF16), grid=(rows // tr,),
        in_specs=[pl.BlockSpec((tr, cols), lambda i: (i, 0)), pl.BlockSpec((tr, 1), lambda i: (i, 0))],
        out_specs=pl.BlockSpec((tr, len(idx)), lambda i: (i, 0)), name="wprep",
        compiler_params=pltpu.CompilerParams(dimension_semantics=("arbitrary",),
                                             vmem_limit_bytes=VMEM_LIMIT),
    )(w, g)


def _inproj_kernel(x_ref, w_ref, b_ref, mqk_ref, mv_ref, mo_ref, nq_ref,
                   cin_ref, nsw_ref, mg_ref, small_ref):
    x = x_ref[...]
    rs = lax.rsqrt(jnp.mean(x * x, axis=-1, keepdims=True) + RMS_EPS)
    xb = x.astype(BF16)

    def seg(a, n):
        return _dot_nt(xb, w_ref[a:a + n, :]) * rs + b_ref[:, a:a + n]

    off = {}
    pos = 0
    for name, w in _SEGS:
        off[name] = pos
        pos += w

    mqk_ref[...] = seg(off["mqk"], 1024).astype(BF16)
    mv_ref[...] = seg(off["mv"], 512).astype(BF16)
    mo_ref[...] = seg(off["mo"], 512).astype(BF16)
    nq_ref[...] = seg(off["nq"], 512).astype(BF16)
    cc = seg(off["kc"], 2 * LANES)
    cin_ref[0] = cc[:, :LANES]
    cin_ref[1] = cc[:, LANES:]
    nsw_ref[:, 0:2 * LANES] = seg(off["nsw"], 2 * LANES).astype(BF16)
    vv = seg(off["nsw"] + 2 * LANES, 2 * LANES)
    lo = lax.broadcasted_iota(jnp.int32, (x.shape[0], LANES), 1) < N_HEAD_DIM
    for j in range(2):
        vj = vv[:, j * LANES:(j + 1) * LANES]
        nsw_ref[:, (2 + 2 * j) * LANES:(3 + 2 * j) * LANES] = jnp.where(lo, vj, 1.0).astype(BF16)
        nsw_ref[:, (3 + 2 * j) * LANES:(4 + 2 * j) * LANES] = jnp.where(lo, 1.0, vj).astype(BF16)
    mg_ref[...] = seg(off["mg"], 2048).astype(BF16)
    small_ref[...] = seg(off["small"], 128)


def _inproj(x2, w, b, tm=1024):
    n = x2.shape[0]
    assert n % tm == 0
    row = lambda width: pl.BlockSpec((tm, width), lambda i: (i, 0))
    out_shape = (
        jax.ShapeDtypeStruct((n, 1024), BF16), jax.ShapeDtypeStruct((n, 512), BF16),
        jax.ShapeDtypeStruct((n, 512), BF16), jax.ShapeDtypeStruct((n, 512), BF16),
        jax.ShapeDtypeStruct((2, n, 128), F32), jax.ShapeDtypeStruct((n, 768), BF16),
        jax.ShapeDtypeStruct((n, 2048), BF16), jax.ShapeDtypeStruct((n, 128), F32))
    out_specs = (row(1024), row(512), row(512), row(512),
                 pl.BlockSpec((2, tm, 128), lambda i: (0, i, 0)), row(768), row(2048), row(128))
    return pl.pallas_call(
        _inproj_kernel, out_shape=out_shape, grid=(n // tm,),
        in_specs=[row(D_MODEL), _const_spec((_D_IN_PAD, D_MODEL)), _const_spec((1, _D_IN_PAD))],
        out_specs=out_specs, name="inproj",
        compiler_params=pltpu.CompilerParams(dimension_semantics=("arbitrary",),
                                             vmem_limit_bytes=VMEM_LIMIT),
    )(x2, w, b)


def _mlstm_kernel(mqk_ref, mv_ref, mo_ref, small_ref, cw_ref, bfc_ref, gh_ref, ltri_ref,
                  ym_ref, ebuf, c_st, m_st, gx, g_bc, g_d, g_a, g_mrow, g_blast, g_amax):
    L = M_CHUNK
    H = M_HEADS
    c = pl.program_id(1)
    n_chunks = gx.shape[0] // 8
    n_sub = mqk_ref.shape[1] // L
    lanes =lax.broadcasted_iota(jnp.int32, gx.shape, 1)

    @pl.when(c == 0)
    def _():
        ebuf[0:8, :] = jnp.zeros((8, 2 * M_WIDTH), F32)
        c_st[...] = jnp.zeros_like(c_st)
        m_st[...] = jnp.zeros_like(m_st)
        for cc in range(n_chunks):
            gx[cc * 8:(cc + 1) * 8, :] = small_ref[0, cc * L:(cc + 1) * L, :].T[0:8, :]
        x = gx[...]
        fg = x + bfc_ref[...]
        lf = jnp.minimum(fg, 0.0) - jnp.log1p(jnp.exp(-jnp.abs(fg)))
        bc = lax.dot_general(lf, ltri_ref[...], (((1,), (1,)), ((), ())),
                             precision=HI, preferred_element_type=F32)
        ig = pltpu.roll(x, H, 0)
        blast = bc[:, L - 1:L]
        a_all = blast - bc + ig
        d = ig - bc
        cm = d
        sh = 1
        while sh < L:
            cm = jnp.maximum(cm, jnp.where(lanes >= sh, pltpu.roll(cm, sh, 1), -jnp.inf))
            sh *= 2
        g_bc[...] = bc
        g_d[...] = d
        g_a[...] = a_all
        g_mrow[...] = bc + cm
        g_blast[...] = jnp.broadcast_to(blast, gx.shape)
        g_amax[...] = jnp.broadcast_to(jnp.max(a_all, axis=-1, keepdims=True), gx.shape)

    row = lax.broadcasted_iota(jnp.int32, (L, L), 0)
    col = lax.broadcasted_iota(jnp.int32, (L, L), 1)
    causal = row >= col
    ones = jnp.ones((L, M_HEAD_DIM), BF16)

    def chunk(j):
        rows = slice(j * L, (j + 1) * L)
        ebuf[8:8 + L, :] = mqk_ref[0, rows, :].astype(F32)
        ext = ebuf[...]
        conv = cw_ref[CONV_WIDTH - 1:CONV_WIDTH, :] * ext[8:, :]
        for k in range(1, CONV_WIDTH):
            conv = conv + cw_ref[CONV_WIDTH - 1 - k:CONV_WIDTH - k, :] * pltpu.roll(ext, k, 0)[8:, :]
        ebuf[0:8, :] = ebuf[L:L + 8, :]
        qk = (conv * jax.nn.sigmoid(conv)).astype(BF16)

        r8 = pl.ds(pl.multiple_of((c * n_sub + j) * 8, 8), 8)
        bc = g_bc[r8, :]
        dt = g_d[r8, :]
        blast = g_blast[r8, :]
        m_prev = m_st[...]
        m_new = jnp.maximum(blast + m_prev, g_amax[r8, :])
        decay = jnp.exp(blast + m_prev - m_new)
        w_row = jnp.exp(g_a[r8, :] - m_new)
        log_inter = bc + m_prev
        m_i = jnp.maximum(g_mrow[r8, :], log_inter)
        k_scale = M_HEAD_DIM ** -0.5
        s_inter = jnp.exp(log_inter - m_i) * k_scale
        emi = jnp.exp(-m_i)
        u = bc - m_i + float(np.log(k_scale))
        cols = jnp.concatenate([u, s_inter, emi, w_row, jnp.zeros((LANES - 32, L), F32)], axis=0).T
        u_all, s_inter_all = cols[:, 0:8], cols[:, 8:16]
        emi_all, w_all = cols[:, 16:24], cols[:, 24:32]

        for h in range(M_HEADS):
            sl = slice(h * M_HEAD_DIM, (h + 1) * M_HEAD_DIM)
            qb = qk[:, sl]
            kb = qk[:, M_WIDTH + h * M_HEAD_DIM:M_WIDTH + (h + 1) * M_HEAD_DIM]
            v = mv_ref[0, rows, sl]
            cn_prev = c_st[h]

            g = H + h
            logp = jnp.where(causal, u_all[:, g:g + 1] + dt[g:g + 1, :], -jnp.inf)
            p = jnp.exp(logp) * _dot_nt(qb, kb)
            nd = (_dot(p.astype(BF16), jnp.concatenate([v, ones], axis=1))
                  + s_inter_all[:, g:g + 1] * _dot(qb, cn_prev.astype(BF16)))
            hh = nd[:, :M_HEAD_DIM] / jnp.maximum(jnp.abs(nd[:, M_HEAD_DIM:]), emi_all[:, g:g + 1])
            hh = hh * lax.rsqrt(jnp.mean(hh * hh, axis=-1, keepdims=True) + RMS_EPS) * gh_ref[:, sl]
            ym_ref[0, rows, sl] = (jax.nn.sigmoid(mo_ref[0, rows, sl].astype(F32)) * hh).astype(ym_ref.dtype)

            w_col = w_all[:, g:g + 1]
            vw = jnp.concatenate([v.astype(F32) * w_col, jnp.broadcast_to(w_col, (L, M_HEAD_DIM))], axis=1)
            c_st[h] = decay[g:g + 1, 0:1] * cn_prev + _dot_tn(kb, vw.astype(BF16))

        m_st[...] = m_new

    for j in range(n_sub):
        chunk(j)


def _mlstm(mqk, mv, mo, small, cw, bf, gh, ltri, n_sub=2):
    B, T, _ = mqk.shape
    L = M_CHUNK
    assert T % (n_sub * L) == 0
    nc = T // L
    assert L == LANES and 8 * nc <= LANES
    blk = lambda w: pl.BlockSpec((1, n_sub * L, w), lambda b, c: (b, c, 0))
    gate = lambda: pltpu.VMEM((8 * nc, L), F32)
    return pl.pallas_call(
        _mlstm_kernel, out_shape=jax.ShapeDtypeStruct((B, T, M_WIDTH), BF16),
        grid=(B, nc // n_sub),
        in_specs=[blk(2 * M_WIDTH), blk(M_WIDTH), blk(M_WIDTH),
                  pl.BlockSpec((1, T, LANES), lambda b, c: (b, 0, 0)),
                  _const_spec((8, 2 * M_WIDTH)), _const_spec((8 * nc, L)),
                  _const_spec((1, M_WIDTH)), _const_spec((L, L))],
        out_specs=blk(M_WIDTH),
        scratch_shapes=[pltpu.VMEM((L + 8, 2 * M_WIDTH), F32),
                        pltpu.VMEM((M_HEADS, M_HEAD_DIM, 2 * M_HEAD_DIM), F32),
                        pltpu.VMEM((8, L), F32),
                        gate(), gate(), gate(), gate(), gate(), gate(), gate()],
        name="mlstm",
        compiler_params=pltpu.CompilerParams(dimension_semantics=("arbitrary", "arbitrary")),
    )(mqk, mv, mo, small, cw, bf, gh, ltri)


def _compress_kernel(x_ref, pe_ref, w1t_ref, w1b_ref, w2_ref, o_ref, *, nch, nbt):
    half = CMP_BLOCK // 2
    rows = nbt * nch
    top = jnp.zeros((rows, 2 * CMP_HIDDEN), F32)
    bot = jnp.zeros((rows, 2 * CMP_HIDDEN), F32)
    for p in range(half):
        xp = jnp.concatenate([x_ref[0, e, pl.ds(p, nch, stride=CMP_STRIDE), :] for e in range(nbt)], axis=0)
        top = top + _dot((xp + pe_ref[0, p:p + 1, :]).astype(BF16), w1t_ref[0, p])
        bot = bot + _dot((xp + pe_ref[0, half + p:half + p + 1, :]).astype(BF16), w1b_ref[0, p])
    hid = top + pltpu.roll(bot, rows - 1, 0)
    act = hid * jax.nn.sigmoid(hid)
    out = _dot(act.astype(BF16), w2_ref[0]).astype(o_ref.dtype)
    for e in range(nbt):
        o_ref[0, e] = out[e * nch:(e + 1) * nch]


def _compress(cin, pe2, w1t, w1b, w2):
    _, B, T, _ = cin.shape
    nch = T // CMP_STRIDE
    nbt = 4 if B % 4 == 0 else 1
    sel = lambda *shape: pl.BlockSpec((1,) + shape, lambda j, b: (j,) + (0,) * len(shape))
    return pl.pallas_call(
        functools.partial(_compress_kernel, nch=nch, nbt=nbt),
        out_shape=jax.ShapeDtypeStruct((2, B, nch, LANES), BF16),
        grid=(2, B // nbt),
        in_specs=[pl.BlockSpec((1, nbt, T, LANES), lambda j, b: (j, b, 0, 0)),
                  sel(CMP_BLOCK, LANES), sel(CMP_BLOCK // 2, LANES, 2 * CMP_HIDDEN),
                  sel(CMP_BLOCK // 2, LANES, 2 * CMP_HIDDEN), sel(2 * CMP_HIDDEN, LANES)],
        out_specs=pl.BlockSpec((1, nbt, nch, LANES), lambda j, b: (j, b, 0, 0)),
        name="compress",
        compiler_params=pltpu.CompilerParams(dimension_semantics=("arbitrary", "arbitrary")),
    )(cin, pe2, w1t, w1b, w2)


def _bias_kernel(tbl_ref, o_ref, *, kind, n_cmp):
    pid = pl.program_id(0)
    toeplitz = kind != "cmp"
    if toeplitz:
        k = lax.broadcasted_iota(jnp.int32, (8, 2 * LANES), 1)
        dist = (pid - 1) * TQ + jnp.where(k < LANES, -k, 2 * LANES - k)
    else:
        al = lax.broadcasted_iota(jnp.int32, (CMP_STRIDE, 2 * LANES), 0)
        m = lax.broadcasted_iota(jnp.int32, (CMP_STRIDE, 2 * LANES), 1)
        dist = pid * TQ - (CMP_BLOCK - 1) + al - CMP_STRIDE * jnp.where(m < LANES, m, m - 2 * LANES)
    n = jnp.maximum(dist, 0)
    cnt = jnp.zeros_like(n)
    for t in _BUCKET_THR:
        cnt = cnt + jnp.where(n >= t, 1, 0)
    bucket = jnp.where(n < REL_BUCKETS // 2, n, REL_BUCKETS // 2 + cnt)
    if kind == "tok":
        madd = jnp.where((dist >= 0) & (pid > 0), 0.0, NEG)
    elif kind == "win":
        madd = jnp.where((dist >= 0) & (dist < WINDOW) & (pid > 0), 0.0, NEG)
    else:
        madd = jnp.where(dist >= 0, 0.0, NEG)
        valid_c = lax.broadcasted_iota(jnp.int32, (TQ, LANES), 1) < n_cmp
    for h in range(N_HEADS):
        val = jnp.zeros(dist.shape, F32)
        for bb in range(REL_BUCKETS):
            val = jnp.where(bucket == bb, tbl_ref[h * REL_BUCKETS + bb], val)
        val = val * LOG2E + madd
        if toeplitz:
            g = jnp.broadcast_to(val[0:1, :], (TQ, 2 * LANES))
            val = pltpu.roll(g, 0, 1, stride=1, stride_axis=0)[:, :LANES]
        else:
            bands = [val[:, :LANES]] + [pltpu.roll(val, ah, 1)[:, :LANES] for ah in range(1, TQ // CMP_STRIDE)]
            val = jnp.where(valid_c, jnp.concatenate(bands, axis=0), NEG)
        o_ref[0, h * TQ:(h + 1) * TQ, :] = val


def _bias_tiles(tbl, n_tiles, kind, n_cmp=0):
    return pl.pallas_call(
        functools.partial(_bias_kernel, kind=kind, n_cmp=n_cmp),
        out_shape=jax.ShapeDtypeStruct((n_tiles, N_HEADS * TQ, LANES), F32),
        grid=(n_tiles,),
        in_specs=[pl.BlockSpec(memory_space=pltpu.SMEM)],
        out_specs=pl.BlockSpec((1, N_HEADS * TQ, LANES), lambda i: (i, 0, 0)),
        name="bias_" + kind,
        compiler_params=pltpu.CompilerParams(dimension_semantics=("arbitrary",)),
    )(tbl)


def _nsa_kernel(nq_ref, nq4_ref, ks_ref, kw_ref, vs0_ref, vs1_ref, vw0_ref, vw1_ref, kc_ref, vc_ref,
                small_ref, small4_ref, bt_ref, wb_ref, cb4_ref, ovt_ref, et_ref, yn_ref,
                lhs_sc, z_sc, zw_sc, seln_sc, partc_sc, *, nb, n_slc, n_top, n_win, n_chunks_max):
    qi = pl.program_id(1)
    HR = N_HEADS * TQ
    GR = N_REP * TQ
    CH = 4 * TQ
    G4 = 4
    vs_refs = (vs0_ref, vs1_ref)
    vw_refs = (vw0_ref, vw1_ref)
    lane = lax.broadcasted_iota(jnp.int32, (TQ, LANES), 1)
    lo = lane < N_HEAD_DIM

    def stack_heads(q_all):
        zero = jnp.zeros((TQ, LANES), BF16)
        parts = []
        for g in range(N_KV_GROUPS):
            for r in range(N_REP):
                qr = q_all[:, r * LANES:(r + 1) * LANES]
                parts.append(jnp.where(lo if g == 0 else jnp.logical_not(lo), qr, zero))
        return parts

    def pair(o, r):
        return jnp.where(lo, o[r * TQ:(r + 1) * TQ], o[(N_REP + r) * TQ:(N_REP + r + 1) * TQ])

    def normed(acc, r):
        den = jnp.where(lo, acc[(N_REP + r) * TQ:(N_REP + r + 1) * TQ], acc[r * TQ:(r + 1) * TQ])
        return pair(acc, r) / pltpu.roll(den, N_HEAD_DIM, 1)

    def gate(sg, r, ci):
        c0 = _SMALL_NGATE + r * 3 + ci
        c1 = _SMALL_NGATE + (N_REP + r) * 3 + ci
        return jnp.where(lo, sg[:, c0:c0 + 1], sg[:, c1:c1 + 1])

    w0 = jnp.maximum(qi - (n_win - 1), 0)
    woff = pl.multiple_of(w0 * TQ, TQ)

    def group_prep(s, n_chunks):
        qs4 = jnp.concatenate(
            [p for j in range(G4) for p in stack_heads(nq4_ref[s, j * TQ:(j + 1) * TQ, :])], axis=0)
        rows = lax.broadcasted_iota(jnp.int32, (G4 * HR, 1), 0)
        t_rows = (qi + jnp.right_shift(rows, 10)) * TQ + (rows & (TQ - 1))

        z = _dot_nt(qs4, kc_ref[0, s]) + cb4_ref[...].reshape(G4 * HR, LANES)
        e = jnp.exp2(z - jnp.max(z, axis=-1, keepdims=True))
        l = _dot(e.astype(BF16), jnp.ones((LANES, LANES), BF16))
        p_c = e * jnp.where(t_rows >= CMP_BLOCK - 1, 1.0 / jnp.maximum(l, 1e-30), 0.0)
        o_c = _dot(p_c.astype(BF16), vc_ref[0, s])
        sg4 = jax.nn.sigmoid(small4_ref[s])
        for j in range(G4):
            for r in range(N_REP):
                partc_sc[s, j, r] = (gate(sg4[j * TQ:(j + 1) * TQ], r, 0)
                                     * pair(o_c[j * HR:(j + 1) * HR], r))

        W = G4 * N_KV_GROUPS * TQ
        jb = lax.broadcasted_iota(jnp.int32, (n_slc, W), 0)
        col = lax.broadcasted_iota(jnp.int32, (n_slc, W), 1)
        tq = (qi + jnp.right_shift(col, 8)) * TQ + (col & (TQ - 1))
        cur = jnp.right_shift(tq, 6)
        elig = jb <= cur
        if n_chunks * CH <= n_top * SEL_BLOCK:
            sel = jnp.where(elig, 1.0, 0.0)
        else:
            forced = (jb == 0) | (jb == cur) | (jb == cur - 1)
            psums = []
            for j in range(G4):
                for g in range(N_KV_GROUPS):
                    base = j * HR + g * GR
                    ps = p_c[base:base + TQ]
                    for r in range(1, N_REP):
                        ps = ps + p_c[base + r * TQ:base + (r + 1) * TQ]
                    psums.append(ps)
            imp = lax.dot_general(ovt_ref[...], jnp.concatenate(psums, axis=0), (((1,), (1,)), ((), ())),
                                  precision=HI, preferred_element_type=F32)
            score = jnp.where(elig, jnp.where(forced, BIG, imp), -BIG)
            cnt = jnp.zeros((n_slc, W), F32)
            for i in range(n_slc):
                si = score[i:i + 1, :]
                tie = jnp.where(jb > i, 1.0, 0.0)
                cnt = cnt + jnp.where(si > score, 1.0, jnp.where(si == score, tie, 0.0))
            sel = jnp.where((cnt < n_top) & (score > -BIG / 2), 1.0, 0.0)
        seln = jnp.concatenate([sel - 1.0, jnp.zeros((LANES - n_slc, W), F32)], axis=0).T.astype(BF16)
        for j in range(G4):
            for g in range(N_KV_GROUPS):
                seln_sc[s, j, g] = seln[(j * N_KV_GROUPS + g) * TQ:(j * N_KV_GROUPS + g + 1) * TQ]

    def tiles(n_pairs):
        jq = qi & (G4 - 1)
        elems = range(nb)
        cat = lambda xs: xs[0] if len(xs) == 1 else jnp.concatenate(xs, axis=0)
        rep = lambda t: t if nb == 1 else jnp.concatenate([t] * nb, axis=0)
        parts = [stack_heads(nq_ref[s]) for s in elems]
        qs = [jnp.concatenate(parts[s], axis=0) for s in elems]

        def bias_tiles(ref, first, count, tile0):
            tiles_ = [ref[jnp.maximum(qi - tile0 - (first + u) + 1, 0)] for u in range(count)]
            return rep(tiles_[0] if count == 1 else jnp.concatenate(tiles_, axis=1))

        zmax = None
        for j0 in range(0, n_win, 2):
            nj = min(2, n_win - j0)
            zp = (cat([_dot_nt(qs[s], kw_ref[s, pl.ds(woff + j0 * TQ, nj * TQ), :]) for s in elems])
                  + bias_tiles(wb_ref, j0, nj, w0))
            zw_sc[:, j0 * TQ:(j0 + nj) * TQ] = zp
            for u in range(nj):
                zj = zp[:, u * TQ:(u + 1) * TQ]
                zmax = zj if zmax is None else jnp.maximum(zmax, zj)
        mw = jnp.max(zmax, axis=-1, keepdims=True)
        pw = jnp.exp2(zw_sc[...] - mw).astype(BF16)
        acc_w = cat([_dot(pw[s * HR + g * GR:s * HR + (g + 1) * GR],
                          vw_refs[g][s, pl.ds(woff, n_win * TQ), :])
                     for s in elems for g in range(N_KV_GROUPS)])

        for s in elems:
            for g in range(N_KV_GROUPS):
                seln = seln_sc[s, jq, g]
                for r in range(N_REP):
                    h = g * N_REP + r
                    lhs_sc[s, h * TQ:(h + 1) * TQ, :] = jnp.concatenate([parts[s][h], seln], axis=1)

        zmax = None
        for pi in range(n_pairs):
            k0 = pi * 2 * TQ
            zp = (cat([_dot_nt(lhs_sc[s], jnp.concatenate([ks_ref[s, k0:k0 + 2 * TQ, :],
                                                           et_ref[k0:k0 + 2 * TQ, :]], axis=1))
                       for s in elems])
                  + bias_tiles(bt_ref, 2 * pi, 2, 0))
            z_sc[pi // 2, :, (pi % 2) * 2 * TQ:(pi % 2 + 1) * 2 * TQ] = zp
            for u in range(2):
                zt = zp[:, u * TQ:(u + 1) * TQ]
                zmax = zt if zmax is None else jnp.maximum(zmax, zt)
        ms = jnp.max(zmax, axis=-1, keepdims=True)
        acc = [None] * (nb * N_KV_GROUPS)
        for c in range((n_pairs + 1) // 2):
            width = min(CH, n_pairs * 2 * TQ - c * CH)
            p = jnp.exp2(z_sc[c, :, 0:width] - ms).astype(BF16)
            for s in elems:
                for g in range(N_KV_GROUPS):
                    i = s * N_KV_GROUPS + g
                    d = _dot(p[i * GR:(i + 1) * GR], vs_refs[g][s, c * CH:c * CH + width, :])
                    acc[i] = d if acc[i] is None else acc[i] + d
        acc_s = cat(acc)

        for s in elems:
            sg = jax.nn.sigmoid(small_ref[s])
            a_s = acc_s[s * HR:(s + 1) * HR]
            a_w = acc_w[s * HR:(s + 1) * HR]
            for r in range(N_REP):
                out = (partc_sc[s, jq, r] + gate(sg, r, 1) * normed(a_s, r)
                       + gate(sg, r, 2) * normed(a_w, r))
                yn_ref[s, :, r * LANES:(r + 1) * LANES] = out.astype(yn_ref.dtype)

    for nc in range(1, n_chunks_max + 1):
        @pl.when(qi // G4 == nc - 1)
        def _(nc=nc):
            @pl.when(qi % G4 == 0)
            def _():
                for s in range(nb):
                    group_prep(s, nc)

            tiles(2 * nc)


def _nsa(nq, nsw, ckv, small, bt, wb, cb, ov, emat):
    B, T, _ = nq.shape
    assert T == 2048, "single 128-wide compressed-key tile assumes T == 2048"
    nqt = T // TQ
    n_cmp = (T - CMP_BLOCK) // CMP_STRIDE + 1
    n_slc = T // SEL_BLOCK
    n_top = min(SEL_TOPK, n_slc)
    n_win = wb.shape[0] - 1
    HR = N_HEADS * TQ
    assert nqt % 4 == 0 and n_slc % 8 == 0 and n_slc <= LANES
    nb = NSA_BATCH if B % NSA_BATCH == 0 else 1
    n_chunks_max = nqt // 4
    kv = lambda j: pl.BlockSpec((nb, T, LANES), lambda b, q: (b, 0, j))
    ck = lambda j: pl.BlockSpec((1, nb, T // CMP_STRIDE, LANES), lambda b, q: (j, b, 0, 0))
    kern = functools.partial(_nsa_kernel, nb=nb, n_slc=n_slc, n_top=n_top, n_win=n_win,
                             n_chunks_max=n_chunks_max)
    return pl.pallas_call(
        kern, out_shape=jax.ShapeDtypeStruct((B, T, N_WIDTH), BF16),
        grid=(B // nb, nqt),
        in_specs=[pl.BlockSpec((nb, TQ, N_WIDTH), lambda b, q: (b, q, 0)),
                  pl.BlockSpec((nb, 4 * TQ, N_WIDTH), lambda b, q: (b, q // 4, 0)),
                  kv(0), kv(1), kv(2), kv(3), kv(4), kv(5), ck(0), ck(1),
                  pl.BlockSpec((nb, TQ, LANES), lambda b, q: (b, q, 0)),
                  pl.BlockSpec((nb, 4 * TQ, LANES), lambda b, q: (b, q // 4, 0)),
                  _const_spec(bt.shape), _const_spec(wb.shape),
                  pl.BlockSpec((4, HR, LANES), lambda b, q: (q // 4, 0, 0)),
                  _const_spec(ov.shape), _const_spec(emat.shape)],
        out_specs=pl.BlockSpec((nb, TQ, N_WIDTH), lambda b, q: (b, q, 0)),
        scratch_shapes=[pltpu.VMEM((nb, HR, 2 * LANES), BF16),
                        pltpu.VMEM((n_chunks_max, nb * HR, 4 * TQ), F32),
                        pltpu.VMEM((nb * HR, n_win * TQ), F32),
                        pltpu.VMEM((nb, 4, N_KV_GROUPS, TQ, LANES), BF16),
                        pltpu.VMEM((nb, 4, N_REP, TQ, LANES), F32)],
        name="nsa",
        compiler_params=pltpu.CompilerParams(dimension_semantics=("arbitrary", "arbitrary"),
                                             vmem_limit_bytes=VMEM_LIMIT),
    )(nq, nq, nsw, nsw, nsw, nsw, nsw, nsw, ckv, ckv, small, small, bt, wb, cb, ov, emat)


def _merge_kernel(x_ref, ym_ref, yn_ref, mg_ref, wbm_ref, wbn_ref, wo_ref,
                  wg_ref, wu_ref, wd_ref, gfin_ref, o_ref, *, tf):
    bm = _dot(ym_ref[...], wbm_ref[...])
    bn = _dot(yn_ref[...], wbn_ref[...])
    mixed = (jax.nn.sigmoid(mg_ref[:, :D_MODEL].astype(F32)) * bm
             + jax.nn.sigmoid(mg_ref[:, D_MODEL:].astype(F32)) * bn)
    h = x_ref[...] + _dot(mixed.astype(BF16), wo_ref[...])
    rs = lax.rsqrt(jnp.mean(h * h, axis=-1, keepdims=True) + RMS_EPS)
    hb = h.astype(BF16)
    acc = jnp.zeros(h.shape, F32)
    for j in range(D_FF // tf):
        gg = _dot(hb, wg_ref[:, j * tf:(j + 1) * tf]) * rs
        uu = _dot(hb, wu_ref[:, j * tf:(j + 1) * tf]) * rs
        act = (gg * jax.nn.sigmoid(gg) * uu).astype(BF16)
        acc = acc + _dot(act, wd_ref[j * tf:(j + 1) * tf, :])
    h2 = h + acc
    o_ref[...] = h2 * lax.rsqrt(jnp.mean(h2 * h2, axis=-1, keepdims=True) + RMS_EPS) * gfin_ref[...]


def _merge(x2, ym, yn, mg, wbm, wbn, wo, wg, wu, wd, gfin, tm=512, tf=256):
    n = x2.shape[0]
    assert n % tm == 0 and D_FF % tf == 0
    row = lambda width: pl.BlockSpec((tm, width), lambda i: (i, 0))
    return pl.pallas_call(
        functools.partial(_merge_kernel, tf=tf),
        out_shape=jax.ShapeDtypeStruct((n, D_MODEL), F32), grid=(n // tm,),
        in_specs=[row(D_MODEL), row(M_WIDTH), row(N_WIDTH), row(N_BRANCH * D_MODEL),
                  _const_spec(wbm.shape), _const_spec(wbn.shape), _const_spec(wo.shape),
                  _const_spec(wg.shape), _const_spec(wu.shape),
                  _const_spec(wd.shape), _const_spec(gfin.shape)],
        out_specs=row(D_MODEL), name="merge_ffn",
        compiler_params=pltpu.CompilerParams(dimension_semantics=("arbitrary",),
                                             vmem_limit_bytes=VMEM_LIMIT),
    )(x2, ym, yn, mg, wbm, wbn, wo, wg, wu, wd, gfin)


def _nsa_constants(T):
    n_cmp = (T - CMP_BLOCK) // CMP_STRIDE + 1
    n_slc = T // SEL_BLOCK
    cs = np.arange(n_cmp) * CMP_STRIDE
    ss = np.arange(n_slc) * SEL_BLOCK
    ov = np.clip(np.minimum(cs[:, None] + CMP_BLOCK, ss[None, :] + SEL_BLOCK)
                 - np.maximum(cs[:, None], ss[None, :]), 0, None) / CMP_STRIDE
    ovt = np.zeros((n_slc, LANES), np.float32)
    ovt[:, :n_cmp] = ov.T
    et = (np.arange(T)[:, None] // SEL_BLOCK == np.arange(LANES)[None, :]).astype(np.float32) * (-NEG)
    return jnp.asarray(ovt), jnp.asarray(et, dtype=BF16)


def _compress_weights(pe_cmp, w_cmp1, w_cmp2):
    half = CMP_BLOCK // 2
    assert N_KV_GROUPS == 2

    def blockdiag(w):
        z = jnp.zeros_like(w)
        return jnp.concatenate([jnp.concatenate([w, z], axis=-1), jnp.concatenate([z, w], axis=-1)], axis=-2)

    w1bd = blockdiag(w_cmp1.reshape(2, CMP_BLOCK, N_HEAD_DIM, CMP_HIDDEN).astype(BF16))
    w2bd = blockdiag(w_cmp2)
    pe2 = jnp.tile(pe_cmp, (1, 1, N_KV_GROUPS))
    return pe2, w1bd[:, :half].astype(BF16), w1bd[:, half:].astype(BF16), w2bd.astype(BF16)


def kernel(x, g_norm_mix, w_in, b_in, b_fgate, conv_qk, g_mlstm_head, pe_cmp, w_cmp1, w_cmp2,
           rel_bias, w_branch, w_out, g_norm_ffn, w_gate, w_up, w_down, g_final):
    B, T, D = x.shape
    assert D == D_MODEL and w_in.shape[0] == 1, "one residual block (DEPTH == 1)"
    N = B * T
    x2 = x.reshape(N, D)

    idx, scale = _inproj_perm()
    w_r = _wprep_t(w_in[0].T, g_norm_mix[0].reshape(1, D), idx, scale)
    b_r = _gather_cols(b_in[0].reshape(1, -1), idx, scale, F32)
    cw = jnp.zeros((8, 2 * M_WIDTH), F32).at[:CONV_WIDTH].set(conv_qk[0])
    mqk, mv, mo, nq, cin, nsw, mg, small = _inproj(x2, w_r, b_r)

    bf8 = jnp.concatenate([jnp.zeros((M_HEADS,), F32), b_fgate[0].astype(F32)])
    bfc = jnp.broadcast_to(jnp.tile(bf8, T // M_CHUNK)[:, None], (8 * (T // M_CHUNK), M_CHUNK))
    ltri = jnp.asarray(np.tril(np.ones((M_CHUNK, M_CHUNK), np.float32)))
    r3 = lambda a: a.reshape(B, T, a.shape[-1])
    ym = _mlstm(r3(mqk), r3(mv), r3(mo), r3(small), cw, bfc, g_mlstm_head[0].reshape(1, M_WIDTH), ltri)

    pe2, w1t, w1b, w2 = _compress_weights(pe_cmp[0], w_cmp1[0], w_cmp2[0])
    ckv = _compress(cin.reshape(2, B, T, LANES), pe2, w1t, w1b, w2)
    tbl = rel_bias.astype(F32).T.reshape(-1)
    nqt = T // TQ
    bt = _bias_tiles(tbl, nqt + 1, "tok")
    wb = _bias_tiles(tbl, min(WINDOW // TQ + 1, nqt) + 1, "win")
    cb = _bias_tiles(tbl, nqt, "cmp", n_cmp=(T - CMP_BLOCK) // CMP_STRIDE + 1)
    ov, emat = _nsa_constants(T)
    yn = _nsa(r3(nq), r3(nsw), ckv, r3(small), bt, wb, cb, ov, emat)

    wbm = w_branch[0, 0].astype(BF16)
    wbn = jnp.concatenate(
        [w_branch[0, 1, (g * N_REP + r) * N_HEAD_DIM:(g * N_REP + r + 1) * N_HEAD_DIM]
         for r in range(N_REP) for g in range(N_KV_GROUPS)], axis=0).astype(BF16)
    ident = lambda n: (np.arange(n, dtype=np.int32), np.ones((n,), np.float32))
    g_ffn = g_norm_ffn[0].reshape(D, 1)
    out = _merge(x2, ym.reshape(N, M_WIDTH), yn.reshape(N, N_WIDTH), mg, wbm, wbn,
                 w_out[0].astype(BF16), _wprep(w_gate[0], g_ffn, *ident(D_FF)),
                 _wprep(w_up[0], g_ffn, *ident(D_FF)),
                 _wprep(w_down[0], jnp.ones((D_FF, 1), F32), *ident(D)),
                 g_final.reshape(1, D))
    return out.reshape(B, T, D)
```

```python
import functools

import numpy as np
import jax
import jax.numpy as jnp
from jax import lax
from jax.experimental import pallas as pl
from jax.experimental.pallas import tpu as pltpu

F32 = jnp.float32
BF16 = jnp.bfloat16
HI = lax.Precision.HIGHEST

D_MODEL = 1024
M_HEADS = 4
M_HEAD_DIM = 128
M_WIDTH = M_HEADS * M_HEAD_DIM
M_CHUNK = 128
CONV_WIDTH = 4
N_HEADS = 8
N_KV_GROUPS = 2
N_REP = N_HEADS // N_KV_GROUPS
N_HEAD_DIM = 64
N_WIDTH = N_HEADS * N_HEAD_DIM
N_KV_WIDTH = N_KV_GROUPS * N_HEAD_DIM
CMP_BLOCK = 32
CMP_STRIDE = 16
CMP_HIDDEN = 2 * N_HEAD_DIM
SEL_BLOCK = 64
SEL_TOPK = 16
WINDOW = 512
REL_BUCKETS = 32
REL_MAX_DIST = 1024
N_BRANCH = 2
D_FF = 2816
RMS_EPS = 1e-6
BIG = 1e9
NEG = -1e30

LANES = 128
TQ = 128
NSA_BATCH = 1
VMEM_LIMIT = 56 * 1024 * 1024

_OFF_MQ, _OFF_MK, _OFF_MV, _OFF_MO = 0, 512, 1024, 1536
_OFF_MI, _OFF_MF, _OFF_NQ, _OFF_NKV = 2048, 2052, 2056, 2568
_OFF_NGATE, _OFF_MERGE, _D_IN = 3336, 3360, 5408
_SMALL_NGATE = 8

_SEGS = (("mqk", 1024), ("mv", 512), ("mo", 512), ("nq", 512), ("kc", 128),
         ("vc", 128), ("nsw", 512), ("mg", 2048), ("small", 128))
_D_IN_PAD = sum(w for _, w in _SEGS)
LOG2E = 1.4426950408889634


def _dot(a, b, **kw):
    return jnp.dot(a, b, preferred_element_type=F32, **kw)


def _dot_nt(a, b):
    return lax.dot_general(a, b, (((1,), (1,)), ((), ())), preferred_element_type=F32)


def _dot_tn(a, b):
    return lax.dot_general(a, b, (((0,), (0,)), ((), ())), preferred_element_type=F32)


def _const_spec(shape):
    nd = len(shape)
    return pl.BlockSpec(shape, lambda *_: (0,) * nd, pipeline_mode=pl.Buffered(1))


def _bucket_thresholds():
    max_exact = REL_BUCKETS // 2
    assert REL_MAX_DIST == 64 * max_exact and REL_BUCKETS - max_exact == 16
    thr = []
    for k in range(1, REL_BUCKETS - max_exact):
        t = max_exact
        while t ** 8 < (max_exact ** 8) * (2 ** (3 * k)):
            t += 1
        thr.append(t)
    return tuple(thr)


_BUCKET_THR = _bucket_thresholds()


def _inproj_perm():
    idx = np.zeros((_D_IN_PAD,), np.int32)
    scale = np.zeros((_D_IN_PAD,), np.float32)
    pos = 0

    def put(cols, s=1.0):
        nonlocal pos
        n = len(cols)
        idx[pos:pos + n] = cols
        scale[pos:pos + n] = s
        pos += n

    put(np.arange(_OFF_MQ, _OFF_MV))
    put(np.arange(_OFF_MV, _OFF_MO))
    put(np.arange(_OFF_MO, _OFF_MI))
    nq = np.zeros((N_WIDTH,), np.int32)
    for r in range(N_REP):
        for g in range(N_KV_GROUPS):
            for d in range(N_HEAD_DIM):
                nq[r * 128 + g * 64 + d] = _OFF_NQ + (g * N_REP + r) * N_HEAD_DIM + d
    put(nq, N_HEAD_DIM ** -0.5 * LOG2E)
    kv = lambda j, g: np.arange(_OFF_NKV + (j * N_KV_GROUPS + g) * N_HEAD_DIM,
                                _OFF_NKV + (j * N_KV_GROUPS + g + 1) * N_HEAD_DIM)
    put(np.arange(_OFF_NKV, _OFF_NKV + 256))
    put(np.concatenate([kv(2, 0), kv(2, 1)]))
    put(np.concatenate([kv(4, 0), kv(4, 1)]))
    put(np.concatenate([kv(3, 0), kv(3, 1)]))
    put(np.concatenate([kv(5, 0), kv(5, 1)]))
    put(np.arange(_OFF_MERGE, _D_IN))
    put(np.arange(_OFF_MI, _OFF_MI + 8))
    put(np.arange(_OFF_NGATE, _OFF_NGATE + 24))
    pos += LANES - 32
    assert pos == _D_IN_PAD
    return idx, scale


def _gather_cols(a, idx, scale, dtype):
    pieces = []
    start = 0
    n = len(idx)
    for c in range(1, n + 1):
        same = c < n and scale[c] == scale[start] and (scale[c] == 0.0 or idx[c] == idx[c - 1] + 1)
        if not same:
            if scale[start] == 0.0:
                piece = jnp.zeros((a.shape[0], c - start), dtype)
            else:
                piece = a[:, int(idx[start]):int(idx[start]) + (c - start)]
                if scale[start] != 1.0:
                    piece = piece * float(scale[start])
            pieces.append(piece.astype(dtype))
            start = c
    return jnp.concatenate(pieces, axis=1)


def _col_runs(idx, scale):
    runs = []
    start = 0
    n = len(idx)
    for c in range(1, n + 1):
        same = c < n and scale[c] == scale[start] and (scale[c] == 0.0 or idx[c] == idx[c - 1] + 1)
        if not same:
            runs.append((start, int(idx[start]), c - start, float(scale[start])))
            start = c
    return runs


def _wprep_kernel(w_ref, g_ref, o_ref, *, runs):
    g = g_ref[...]
    for dst, src, n, scale in runs:
        if scale == 0.0:
            o_ref[:, dst:dst + n] = jnp.zeros((o_ref.shape[0], n), o_ref.dtype)
        else:
            o_ref[:, dst:dst + n] = (w_ref[:, src:src + n] * (g * scale)).astype(o_ref.dtype)


def _wprep_t_kernel(wt_ref, g_ref, o_ref, *, runs):
    g = g_ref[...]
    for dst, src, n, scale in runs:
        if scale == 0.0:
            o_ref[dst:dst + n, :] = jnp.zeros((n, o_ref.shape[1]), o_ref.dtype)
        else:
            o_ref[dst:dst + n, :] = (wt_ref[src:src + n, :] * (g * scale)).astype(o_ref.dtype)


def _wprep_t(wt, g, idx, scale, tc=256):
    rows, cols = wt.shape
    assert cols % tc == 0
    return pl.pallas_call(
        functools.partial(_wprep_t_kernel, runs=_col_runs(idx, scale)),
        out_shape=jax.ShapeDtypeStruct((len(idx), cols), BF16), grid=(cols // tc,),
        in_specs=[pl.BlockSpec((rows, tc), lambda i: (0, i)), pl.BlockSpec((1, tc), lambda i: (0, i))],
        out_specs=pl.BlockSpec((len(idx), tc), lambda i: (0, i)), name="wprep_t",
        compiler_params=pltpu.CompilerParams(dimension_semantics=("arbitrary",),
                                             vmem_limit_bytes=VMEM_LIMIT),
    )(wt, g)


def _wprep(w, g, idx, scale, tr=256):
    rows, cols = w.shape
    assert rows % tr == 0
    return pl.pallas_call(
        functools.partial(_wprep_kernel, runs=_col_runs(idx, scale)),
        out_shape=jax.ShapeDtypeStruct((rows, len(idx)), BF16), grid=(rows // tr,),
        in_specs=[pl.BlockSpec((tr, cols), lambda i: (i, 0)), pl.BlockSpec((tr, 1), lambda i: (i, 0))],
        out_specs=pl.BlockSpec((tr, len(idx)), lambda i: (i, 0)), name="wprep",
        compiler_params=pltpu.CompilerParams(dimension_semantics=("arbitrary",),
                                             vmem_limit_bytes=VMEM_LIMIT),
    )(w, g)


def _inproj_kernel(x_ref, w_ref, b_ref, mqk_ref, mv_ref, mo_ref, nq_ref,
                   cin_ref, nsw_ref, mg_ref, small_ref):
    x = x_ref[...]
    rs = lax.rsqrt(jnp.mean(x * x, axis=-1, keepdims=True) + RMS_EPS)
    xb = x.astype(BF16)

    def seg(a, n):
        return _dot_nt(xb, w_ref[a:a + n, :]) * rs + b_ref[:, a:a + n]

    off = {}
    pos = 0
    for name, w in _SEGS:
        off[name] = pos
        pos += w

    mqk_ref[...] = seg(off["mqk"], 1024).astype(BF16)
    mv_ref[...] = seg(off["mv"], 512).astype(BF16)
    mo_ref[...] = seg(off["mo"], 512).astype(BF16)
    nq_ref[...] = seg(off["nq"], 512).astype(BF16)
    cc = seg(off["kc"], 2 * LANES)
    cin_ref[0] = cc[:, :LANES]
    cin_ref[1] = cc[:, LANES:]
    nsw_ref[:, 0:2 * LANES] = seg(off["nsw"], 2 * LANES).astype(BF16)
    vv = seg(off["nsw"] + 2 * LANES, 2 * LANES)
    lo = lax.broadcasted_iota(jnp.int32, (x.shape[0], LANES), 1) < N_HEAD_DIM
    for j in range(2):
        vj = vv[:, j * LANES:(j + 1) * LANES]
        nsw_ref[:, (2 + 2 * j) * LANES:(3 + 2 * j) * LANES] = jnp.where(lo, vj, 1.0).astype(BF16)
        nsw_ref[:, (3 + 2 * j) * LANES:(4 + 2 * j) * LANES] = jnp.where(lo, 1.0, vj).astype(BF16)
    mg_ref[...] = seg(off["mg"], 2048).astype(BF16)
    small_ref[...] = seg(off["small"], 128)


def _inproj(x2, w, b, tm=1024):
    n = x2.shape[0]
    assert n % tm == 0
    row = lambda width: pl.BlockSpec((tm, width), lambda i: (i, 0))
    out_shape = (
        jax.ShapeDtypeStruct((n, 1024), BF16), jax.ShapeDtypeStruct((n, 512), BF16),
        jax.ShapeDtypeStruct((n, 512), BF16), jax.ShapeDtypeStruct((n, 512), BF16),
        jax.ShapeDtypeStruct((2, n, 128), F32), jax.ShapeDtypeStruct((n, 768), BF16),
        jax.ShapeDtypeStruct((n, 2048), BF16), jax.ShapeDtypeStruct((n, 128), F32))
    out_specs = (row(1024), row(512), row(512), row(512),
                 pl.BlockSpec((2, tm, 128), lambda i: (0, i, 0)), row(768), row(2048), row(128))
    return pl.pallas_call(
        _inproj_kernel, out_shape=out_shape, grid=(n // tm,),
        in_specs=[row(D_MODEL), _const_spec((_D_IN_PAD, D_MODEL)), _const_spec((1, _D_IN_PAD))],
        out_specs=out_specs, name="inproj",
        compiler_params=pltpu.CompilerParams(dimension_semantics=("arbitrary",),
                                             vmem_limit_bytes=VMEM_LIMIT),
    )(x2, w, b)


def _mlstm_kernel(mqk_ref, mv_ref, mo_ref, small_ref, cw_ref, bfc_ref, gh_ref, ltri_ref,
                  ym_ref, ebuf, c_st, m_st, gx, g_bc, g_d, g_a, g_mrow, g_blast, g_amax):
    L = M_CHUNK
    H = M_HEADS
    c = pl.program_id(1)
    n_chunks = gx.shape[0] // 8
    n_sub = mqk_ref.shape[1] // L
    lanes =lax.broadcasted_iota(jnp.int32, gx.shape, 1)

    @pl.when(c == 0)
    def _():
        ebuf[0:8, :] = jnp.zeros((8, 2 * M_WIDTH), F32)
        c_st[...] = jnp.zeros_like(c_st)
        m_st[...] = jnp.zeros_like(m_st)
        for cc in range(n_chunks):
            gx[cc * 8:(cc + 1) * 8, :] = small_ref[0, cc * L:(cc + 1) * L, :].T[0:8, :]
        x = gx[...]
        fg = x + bfc_ref[...]
        lf = jnp.minimum(fg, 0.0) - jnp.log1p(jnp.exp(-jnp.abs(fg)))
        bc = lax.dot_general(lf, ltri_ref[...], (((1,), (1,)), ((), ())),
                             precision=HI, preferred_element_type=F32)
        ig = pltpu.roll(x, H, 0)
        blast = bc[:, L - 1:L]
        a_all = blast - bc + ig
        d = ig - bc
        cm = d
        sh = 1
        while sh < L:
            cm = jnp.maximum(cm, jnp.where(lanes >= sh, pltpu.roll(cm, sh, 1), -jnp.inf))
            sh *= 2
        g_bc[...] = bc
        g_d[...] = d
        g_a[...] = a_all
        g_mrow[...] = bc + cm
        g_blast[...] = jnp.broadcast_to(blast, gx.shape)
        g_amax[...] = jnp.broadcast_to(jnp.max(a_all, axis=-1, keepdims=True), gx.shape)

    row = lax.broadcasted_iota(jnp.int32, (L, L), 0)
    col = lax.broadcasted_iota(jnp.int32, (L, L), 1)
    causal = row >= col
    ones = jnp.ones((L, M_HEAD_DIM), BF16)

    def chunk(j):
        rows = slice(j * L, (j + 1) * L)
        ebuf[8:8 + L, :] = mqk_ref[0, rows, :].astype(F32)
        ext = ebuf[...]
        conv = cw_ref[CONV_WIDTH - 1:CONV_WIDTH, :] * ext[8:, :]
        for k in range(1, CONV_WIDTH):
            conv = conv + cw_ref[CONV_WIDTH - 1 - k:CONV_WIDTH - k, :] * pltpu.roll(ext, k, 0)[8:, :]
        ebuf[0:8, :] = ebuf[L:L + 8, :]
        qk = (conv * jax.nn.sigmoid(conv)).astype(BF16)

        r8 = pl.ds(pl.multiple_of((c * n_sub + j) * 8, 8), 8)
        bc = g_bc[r8, :]
        dt = g_d[r8, :]
        blast = g_blast[r8, :]
        m_prev = m_st[...]
        m_new = jnp.maximum(blast + m_prev, g_amax[r8, :])
        decay = jnp.exp(blast + m_prev - m_new)
        w_row = jnp.exp(g_a[r8, :] - m_new)
        log_inter = bc + m_prev
        m_i = jnp.maximum(g_mrow[r8, :], log_inter)
        k_scale = M_HEAD_DIM ** -0.5
        s_inter = jnp.exp(log_inter - m_i) * k_scale
        emi = jnp.exp(-m_i)
        u = bc - m_i + float(np.log(k_scale))
        cols = jnp.concatenate([u, s_inter, emi, w_row, jnp.zeros((LANES - 32, L), F32)], axis=0).T
        u_all, s_inter_all = cols[:, 0:8], cols[:, 8:16]
        emi_all, w_all = cols[:, 16:24], cols[:, 24:32]

        for h in range(M_HEADS):
            sl = slice(h * M_HEAD_DIM, (h + 1) * M_HEAD_DIM)
            qb = qk[:, sl]
            kb = qk[:, M_WIDTH + h * M_HEAD_DIM:M_WIDTH + (h + 1) * M_HEAD_DIM]
            v = mv_ref[0, rows, sl]
            cn_prev = c_st[h]

            g = H + h
            logp = jnp.where(causal, u_all[:, g:g + 1] + dt[g:g + 1, :], -jnp.inf)
            p = jnp.exp(logp) * _dot_nt(qb, kb)
            nd = (_dot(p.astype(BF16), jnp.concatenate([v, ones], axis=1))
                  + s_inter_all[:, g:g + 1] * _dot(qb, cn_prev.astype(BF16)))
            hh = nd[:, :M_HEAD_DIM] / jnp.maximum(jnp.abs(nd[:, M_HEAD_DIM:]), emi_all[:, g:g + 1])
            hh = hh * lax.rsqrt(jnp.mean(hh * hh, axis=-1, keepdims=True) + RMS_EPS) * gh_ref[:, sl]
            ym_ref[0, rows, sl] = (jax.nn.sigmoid(mo_ref[0, rows, sl].astype(F32)) * hh).astype(ym_ref.dtype)

            w_col = w_all[:, g:g + 1]
            vw = jnp.concatenate([v.astype(F32) * w_col, jnp.broadcast_to(w_col, (L, M_HEAD_DIM))], axis=1)
            c_st[h] = decay[g:g + 1, 0:1] * cn_prev + _dot_tn(kb, vw.astype(BF16))

        m_st[...] = m_new

    for j in range(n_sub):
        chunk(j)


def _mlstm(mqk, mv, mo, small, cw, bf, gh, ltri, n_sub=4):
    B, T, _ = mqk.shape
    L = M_CHUNK
    assert T % (n_sub * L) == 0
    nc = T // L
    assert L == LANES and 8 * nc <= LANES
    blk = lambda w: pl.BlockSpec((1, n_sub * L, w), lambda b, c: (b, c, 0))
    gate = lambda: pltpu.VMEM((8 * nc, L), F32)
    return pl.pallas_call(
        _mlstm_kernel, out_shape=jax.ShapeDtypeStruct((B, T, M_WIDTH), BF16),
        grid=(B, nc // n_sub),
        in_specs=[blk(2 * M_WIDTH), blk(M_WIDTH), blk(M_WIDTH),
                  pl.BlockSpec((1, T, LANES), lambda b, c: (b, 0, 0)),
                  _const_spec((8, 2 * M_WIDTH)), _const_spec((8 * nc, L)),
                  _const_spec((1, M_WIDTH)), _const_spec((L, L))],
        out_specs=blk(M_WIDTH),
        scratch_shapes=[pltpu.VMEM((L + 8, 2 * M_WIDTH), F32),
                        pltpu.VMEM((M_HEADS, M_HEAD_DIM, 2 * M_HEAD_DIM), F32),
                        pltpu.VMEM((8, L), F32),
                        gate(), gate(), gate(), gate(), gate(), gate(), gate()],
        name="mlstm",
        compiler_params=pltpu.CompilerParams(dimension_semantics=("arbitrary", "arbitrary")),
    )(mqk, mv, mo, small, cw, bf, gh, ltri)


def _compress_kernel(x_ref, pe_ref, w1t_ref, w1b_ref, w2_ref, o_ref, *, nch, nbt):
    half = CMP_BLOCK // 2
    rows = nbt * nch
    top = jnp.zeros((rows, 2 * CMP_HIDDEN), F32)
    bot = jnp.zeros((rows, 2 * CMP_HIDDEN), F32)
    for p in range(half):
        xp = jnp.concatenate([x_ref[0, e, pl.ds(p, nch, stride=CMP_STRIDE), :] for e in range(nbt)], axis=0)
        top = top + _dot((xp + pe_ref[0, p:p + 1, :]).astype(BF16), w1t_ref[0, p])
        bot = bot + _dot((xp + pe_ref[0, half + p:half + p + 1, :]).astype(BF16), w1b_ref[0, p])
    hid = top + pltpu.roll(bot, rows - 1, 0)
    act = hid * jax.nn.sigmoid(hid)
    out = _dot(act.astype(BF16), w2_ref[0]).astype(o_ref.dtype)
    for e in range(nbt):
        o_ref[0, e] = out[e * nch:(e + 1) * nch]


def _compress(cin, pe2, w1t, w1b, w2):
    _, B, T, _ = cin.shape
    nch = T // CMP_STRIDE
    nbt = 4 if B % 4 == 0 else 1
    sel = lambda *shape: pl.BlockSpec((1,) + shape, lambda j, b: (j,) + (0,) * len(shape))
    return pl.pallas_call(
        functools.partial(_compress_kernel, nch=nch, nbt=nbt),
        out_shape=jax.ShapeDtypeStruct((2, B, nch, LANES), BF16),
        grid=(2, B // nbt),
        in_specs=[pl.BlockSpec((1, nbt, T, LANES), lambda j, b: (j, b, 0, 0)),
                  sel(CMP_BLOCK, LANES), sel(CMP_BLOCK // 2, LANES, 2 * CMP_HIDDEN),
                  sel(CMP_BLOCK // 2, LANES, 2 * CMP_HIDDEN), sel(2 * CMP_HIDDEN, LANES)],
        out_specs=pl.BlockSpec((1, nbt, nch, LANES), lambda j, b: (j, b, 0, 0)),
        name="compress",
        compiler_params=pltpu.CompilerParams(dimension_semantics=("arbitrary", "arbitrary")),
    )(cin, pe2, w1t, w1b, w2)


def _bias_kernel(tbl_ref, o_ref, *, kind, n_cmp):
    pid = pl.program_id(0)
    toeplitz = kind != "cmp"
    if toeplitz:
        k = lax.broadcasted_iota(jnp.int32, (8, 2 * LANES), 1)
        dist = (pid - 1) * TQ + jnp.where(k < LANES, -k, 2 * LANES - k)
    else:
        al = lax.broadcasted_iota(jnp.int32, (CMP_STRIDE, 2 * LANES), 0)
        m = lax.broadcasted_iota(jnp.int32, (CMP_STRIDE, 2 * LANES), 1)
        dist = pid * TQ - (CMP_BLOCK - 1) + al - CMP_STRIDE * jnp.where(m < LANES, m, m - 2 * LANES)
    n = jnp.maximum(dist, 0)
    cnt = jnp.zeros_like(n)
    for t in _BUCKET_THR:
        cnt = cnt + jnp.where(n >= t, 1, 0)
    bucket = jnp.where(n < REL_BUCKETS // 2, n, REL_BUCKETS // 2 + cnt)
    if kind == "tok":
        madd = jnp.where((dist >= 0) & (pid > 0), 0.0, NEG)
    elif kind == "win":
        madd = jnp.where((dist >= 0) & (dist < WINDOW) & (pid > 0), 0.0, NEG)
    else:
        madd = jnp.where(dist >= 0, 0.0, NEG)
        valid_c = lax.broadcasted_iota(jnp.int32, (TQ, LANES), 1) < n_cmp
    for h in range(N_HEADS):
        val = jnp.zeros(dist.shape, F32)
        for bb in range(REL_BUCKETS):
            val = jnp.where(bucket == bb, tbl_ref[h * REL_BUCKETS + bb], val)
        val = val * LOG2E + madd
        if toeplitz:
            g = jnp.broadcast_to(val[0:1, :], (TQ, 2 * LANES))
            val = pltpu.roll(g, 0, 1, stride=1, stride_axis=0)[:, :LANES]
        else:
            bands = [val[:, :LANES]] + [pltpu.roll(val, ah, 1)[:, :LANES] for ah in range(1, TQ // CMP_STRIDE)]
            val = jnp.where(valid_c, jnp.concatenate(bands, axis=0), NEG)
        o_ref[0, h * TQ:(h + 1) * TQ, :] = val


def _bias_tiles(tbl, n_tiles, kind, n_cmp=0):
    return pl.pallas_call(
        functools.partial(_bias_kernel, kind=kind, n_cmp=n_cmp),
        out_shape=jax.ShapeDtypeStruct((n_tiles, N_HEADS * TQ, LANES), F32),
        grid=(n_tiles,),
        in_specs=[pl.BlockSpec(memory_space=pltpu.SMEM)],
        out_specs=pl.BlockSpec((1, N_HEADS * TQ, LANES), lambda i: (i, 0, 0)),
        name="bias_" + kind,
        compiler_params=pltpu.CompilerParams(dimension_semantics=("arbitrary",)),
    )(tbl)


def _nsa_kernel(nq_ref, nq4_ref, ks_ref, kw_ref, vs0_ref, vs1_ref, vw0_ref, vw1_ref, kc_ref, vc_ref,
                small_ref, small4_ref, bt_ref, wb_ref, cb4_ref, ovt_ref, et_ref, yn_ref,
                lhs_sc, z_sc, zw_sc, seln_sc, partc_sc, *, nb, n_slc, n_top, n_win, n_chunks_max):
    qi = pl.program_id(1)
    HR = N_HEADS * TQ
    GR = N_REP * TQ
    CH = 4 * TQ
    G4 = 4
    vs_refs = (vs0_ref, vs1_ref)
    vw_refs = (vw0_ref, vw1_ref)
    lane = lax.broadcasted_iota(jnp.int32, (TQ, LANES), 1)
    lo = lane < N_HEAD_DIM

    def stack_heads(q_all):
        zero = jnp.zeros((TQ, LANES), BF16)
        parts = []
        for g in range(N_KV_GROUPS):
            for r in range(N_REP):
                qr = q_all[:, r * LANES:(r + 1) * LANES]
                parts.append(jnp.where(lo if g == 0 else jnp.logical_not(lo), qr, zero))
        return parts

    def pair(o, r):
        return jnp.where(lo, o[r * TQ:(r + 1) * TQ], o[(N_REP + r) * TQ:(N_REP + r + 1) * TQ])

    def normed(acc, r):
        den = jnp.where(lo, acc[(N_REP + r) * TQ:(N_REP + r + 1) * TQ], acc[r * TQ:(r + 1) * TQ])
        return pair(acc, r) / pltpu.roll(den, N_HEAD_DIM, 1)

    def gate(sg, r, ci):
        c0 = _SMALL_NGATE + r * 3 + ci
        c1 = _SMALL_NGATE + (N_REP + r) * 3 + ci
        return jnp.where(lo, sg[:, c0:c0 + 1], sg[:, c1:c1 + 1])

    w0 = jnp.maximum(qi - (n_win - 1), 0)
    woff = pl.multiple_of(w0 * TQ, TQ)

    def group_prep(s, n_chunks):
        qs4 = jnp.concatenate(
            [p for j in range(G4) for p in stack_heads(nq4_ref[s, j * TQ:(j + 1) * TQ, :])], axis=0)
        rows = lax.broadcasted_iota(jnp.int32, (G4 * HR, 1), 0)
        t_rows = (qi + jnp.right_shift(rows, 10)) * TQ + (rows & (TQ - 1))

        z = _dot_nt(qs4, kc_ref[0, s]) + cb4_ref[...].reshape(G4 * HR, LANES)
        e = jnp.exp2(z - jnp.max(z, axis=-1, keepdims=True))
        l = _dot(e.astype(BF16), jnp.ones((LANES, LANES), BF16))
        p_c = e * jnp.where(t_rows >= CMP_BLOCK - 1, 1.0 / jnp.maximum(l, 1e-30), 0.0)
        o_c = _dot(p_c.astype(BF16), vc_ref[0, s])
        sg4 = jax.nn.sigmoid(small4_ref[s])
        for j in range(G4):
            for r in range(N_REP):
                partc_sc[s, j, r] = (gate(sg4[j * TQ:(j + 1) * TQ], r, 0)
                                     * pair(o_c[j * HR:(j + 1) * HR], r))

        W = G4 * N_KV_GROUPS * TQ
        jb = lax.broadcasted_iota(jnp.int32, (n_slc, W), 0)
        col = lax.broadcasted_iota(jnp.int32, (n_slc, W), 1)
        tq = (qi + jnp.right_shift(col, 8)) * TQ + (col & (TQ - 1))
        cur = jnp.right_shift(tq, 6)
        elig = jb <= cur
        if n_chunks * CH <= n_top * SEL_BLOCK:
            sel = jnp.where(elig, 1.0, 0.0)
        else:
            forced = (jb == 0) | (jb == cur) | (jb == cur - 1)
            psums = []
            for j in range(G4):
                for g in range(N_KV_GROUPS):
                    base = j * HR + g * GR
                    ps = p_c[base:base + TQ]
                    for r in range(1, N_REP):
                        ps = ps + p_c[base + r * TQ:base + (r + 1) * TQ]
                    psums.append(ps)
            imp = lax.dot_general(ovt_ref[...], jnp.concatenate(psums, axis=0), (((1,), (1,)), ((), ())),
                                  precision=HI, preferred_element_type=F32)
            score = jnp.where(elig, jnp.where(forced, BIG, imp), -BIG)
            cnt = jnp.zeros((n_slc, W), F32)
            for i in range(n_slc):
                si = score[i:i + 1, :]
                tie = jnp.where(jb > i, 1.0, 0.0)
                cnt = cnt + jnp.where(si > score, 1.0, jnp.where(si == score, tie, 0.0))
            sel = jnp.where((cnt < n_top) & (score > -BIG / 2), 1.0, 0.0)
        seln = jnp.concatenate([sel - 1.0, jnp.zeros((LANES - n_slc, W), F32)], axis=0).T.astype(BF16)
        for j in range(G4):
            for g in range(N_KV_GROUPS):
                seln_sc[s, j, g] = seln[(j * N_KV_GROUPS + g) * TQ:(j * N_KV_GROUPS + g + 1) * TQ]

    def tiles(n_pairs):
        jq = qi & (G4 - 1)
        elems = range(nb)
        cat = lambda xs: xs[0] if len(xs) == 1 else jnp.concatenate(xs, axis=0)
        rep = lambda t: t if nb == 1 else jnp.concatenate([t] * nb, axis=0)
        parts = [stack_heads(nq_ref[s]) for s in elems]
        qs = [jnp.concatenate(parts[s], axis=0) for s in elems]

        def bias_tiles(ref, first, count, tile0):
            tiles_ = [ref[jnp.maximum(qi - tile0 - (first + u) + 1, 0)] for u in range(count)]
            return rep(tiles_[0] if count == 1 else jnp.concatenate(tiles_, axis=1))

        zmax = None
        for j0 in range(0, n_win, 2):
            nj = min(2, n_win - j0)
            zp = (cat([_dot_nt(qs[s], kw_ref[s, pl.ds(woff + j0 * TQ, nj * TQ), :]) for s in elems])
                  + bias_tiles(wb_ref, j0, nj, w0))
            zw_sc[:, j0 * TQ:(j0 + nj) * TQ] = zp
            for u in range(nj):
                zj = zp[:, u * TQ:(u + 1) * TQ]
                zmax = zj if zmax is None else jnp.maximum(zmax, zj)
        mw = jnp.max(zmax, axis=-1, keepdims=True)
        pw = jnp.exp2(zw_sc[...] - mw).astype(BF16)
        acc_w = cat([_dot(pw[s * HR + g * GR:s * HR + (g + 1) * GR],
                          vw_refs[g][s, pl.ds(woff, n_win * TQ), :])
                     for s in elems for g in range(N_KV_GROUPS)])

        for s in elems:
            for g in range(N_KV_GROUPS):
                seln = seln_sc[s, jq, g]
                for r in range(N_REP):
                    h = g * N_REP + r
                    lhs_sc[s, h * TQ:(h + 1) * TQ, :] = jnp.concatenate([parts[s][h], seln], axis=1)

        zmax = None
        for pi in range(n_pairs):
            k0 = pi * 2 * TQ
            zp = (cat([_dot_nt(lhs_sc[s], jnp.concatenate([ks_ref[s, k0:k0 + 2 * TQ, :],
                                                           et_ref[k0:k0 + 2 * TQ, :]], axis=1))
                       for s in elems])
                  + bias_tiles(bt_ref, 2 * pi, 2, 0))
            z_sc[pi // 2, :, (pi % 2) * 2 * TQ:(pi % 2 + 1) * 2 * TQ] = zp
            for u in range(2):
                zt = zp[:, u * TQ:(u + 1) * TQ]
                zmax = zt if zmax is None else jnp.maximum(zmax, zt)
        ms = jnp.max(zmax, axis=-1, keepdims=True)
        acc = [None] * (nb * N_KV_GROUPS)
        for c in range((n_pairs + 1) // 2):
            width = min(CH, n_pairs * 2 * TQ - c * CH)
            p = jnp.exp2(z_sc[c, :, 0:width] - ms).astype(BF16)
            for s in elems:
                for g in range(N_KV_GROUPS):
                    i = s * N_KV_GROUPS + g
                    d = _dot(p[i * GR:(i + 1) * GR], vs_refs[g][s, c * CH:c * CH + width, :])
                    acc[i] = d if acc[i] is None else acc[i] + d
        acc_s = cat(acc)

        for s in elems:
            sg = jax.nn.sigmoid(small_ref[s])
            a_s = acc_s[s * HR:(s + 1) * HR]
            a_w = acc_w[s * HR:(s + 1) * HR]
            for r in range(N_REP):
                out = (partc_sc[s, jq, r] + gate(sg, r, 1) * normed(a_s, r)
                       + gate(sg, r, 2) * normed(a_w, r))
                yn_ref[s, :, r * LANES:(r + 1) * LANES] = out.astype(yn_ref.dtype)

    for nc in range(1, n_chunks_max + 1):
        @pl.when(qi // G4 == nc - 1)
        def _(nc=nc):
            @pl.when(qi % G4 == 0)
            def _():
                for s in range(nb):
                    group_prep(s, nc)

            tiles(2 * nc)


def _nsa(nq, nsw, ckv, small, bt, wb, cb, ov, emat):
    B, T, _ = nq.shape
    assert T == 2048, "single 128-wide compressed-key tile assumes T == 2048"
    nqt = T // TQ
    n_cmp = (T - CMP_BLOCK) // CMP_STRIDE + 1
    n_slc = T // SEL_BLOCK
    n_top = min(SEL_TOPK, n_slc)
    n_win = wb.shape[0] - 1
    HR = N_HEADS * TQ
    assert nqt % 4 == 0 and n_slc % 8 == 0 and n_slc <= LANES
    nb = NSA_BATCH if B % NSA_BATCH == 0 else 1
    n_chunks_max = nqt // 4
    kv = lambda j: pl.BlockSpec((nb, T, LANES), lambda b, q: (b, 0, j))
    ck = lambda j: pl.BlockSpec((1, nb, T // CMP_STRIDE, LANES), lambda b, q: (j, b, 0, 0))
    kern = functools.partial(_nsa_kernel, nb=nb, n_slc=n_slc, n_top=n_top, n_win=n_win,
                             n_chunks_max=n_chunks_max)
    return pl.pallas_call(
        kern, out_shape=jax.ShapeDtypeStruct((B, T, N_WIDTH), BF16),
        grid=(B // nb, nqt),
        in_specs=[pl.BlockSpec((nb, TQ, N_WIDTH), lambda b, q: (b, q, 0)),
                  pl.BlockSpec((nb, 4 * TQ, N_WIDTH), lambda b, q: (b, q // 4, 0)),
                  kv(0), kv(1), kv(2), kv(3), kv(4), kv(5), ck(0), ck(1),
                  pl.BlockSpec((nb, TQ, LANES), lambda b, q: (b, q, 0)),
                  pl.BlockSpec((nb, 4 * TQ, LANES), lambda b, q: (b, q // 4, 0)),
                  _const_spec(bt.shape), _const_spec(wb.shape),
                  pl.BlockSpec((4, HR, LANES), lambda b, q: (q // 4, 0, 0)),
                  _const_spec(ov.shape), _const_spec(emat.shape)],
        out_specs=pl.BlockSpec((nb, TQ, N_WIDTH), lambda b, q: (b, q, 0)),
        scratch_shapes=[pltpu.VMEM((nb, HR, 2 * LANES), BF16),
                        pltpu.VMEM((n_chunks_max, nb * HR, 4 * TQ), F32),
                        pltpu.VMEM((nb * HR, n_win * TQ), F32),
                        pltpu.VMEM((nb, 4, N_KV_GROUPS, TQ, LANES), BF16),
                        pltpu.VMEM((nb, 4, N_REP, TQ, LANES), F32)],
        name="nsa",
        compiler_params=pltpu.CompilerParams(dimension_semantics=("arbitrary", "arbitrary"),
                                             vmem_limit_bytes=VMEM_LIMIT),
    )(nq, nq, nsw, nsw, nsw, nsw, nsw, nsw, ckv, ckv, small, small, bt, wb, cb, ov, emat)


def _merge_kernel(x_ref, ym_ref, yn_ref, mg_ref, wbm_ref, wbn_ref, wo_ref,
                  wg_ref, wu_ref, wd_ref, gfin_ref, o_ref, *, tf):
    bm = _dot(ym_ref[...], wbm_ref[...])
    bn = _dot(yn_ref[...], wbn_ref[...])
    mixed = (jax.nn.sigmoid(mg_ref[:, :D_MODEL].astype(F32)) * bm
             + jax.nn.sigmoid(mg_ref[:, D_MODEL:].astype(F32)) * bn)
    h = x_ref[...] + _dot(mixed.astype(BF16), wo_ref[...])
    rs = lax.rsqrt(jnp.mean(h * h, axis=-1, keepdims=True) + RMS_EPS)
    hb = h.astype(BF16)
    acc = jnp.zeros(h.shape, F32)
    for j in range(D_FF // tf):
        gg = _dot(hb, wg_ref[:, j * tf:(j + 1) * tf]) * rs
        uu = _dot(hb, wu_ref[:, j * tf:(j + 1) * tf]) * rs
        act = (gg * jax.nn.sigmoid(gg) * uu).astype(BF16)
        acc = acc + _dot(act, wd_ref[j * tf:(j + 1) * tf, :])
    h2 = h + acc
    o_ref[...] = h2 * lax.rsqrt(jnp.mean(h2 * h2, axis=-1, keepdims=True) + RMS_EPS) * gfin_ref[...]


def _merge(x2, ym, yn, mg, wbm, wbn, wo, wg, wu, wd, gfin, tm=512, tf=256):
    n = x2.shape[0]
    assert n % tm == 0 and D_FF % tf == 0
    row = lambda width: pl.BlockSpec((tm, width), lambda i: (i, 0))
    return pl.pallas_call(
        functools.partial(_merge_kernel, tf=tf),
        out_shape=jax.ShapeDtypeStruct((n, D_MODEL), F32), grid=(n // tm,),
        in_specs=[row(D_MODEL), row(M_WIDTH), row(N_WIDTH), row(N_BRANCH * D_MODEL),
                  _const_spec(wbm.shape), _const_spec(wbn.shape), _const_spec(wo.shape),
                  _const_spec(wg.shape), _const_spec(wu.shape),
                  _const_spec(wd.shape), _const_spec(gfin.shape)],
        out_specs=row(D_MODEL), name="merge_ffn",
        compiler_params=pltpu.CompilerParams(dimension_semantics=("arbitrary",),
                                             vmem_limit_bytes=VMEM_LIMIT),
    )(x2, ym, yn, mg, wbm, wbn, wo, wg, wu, wd, gfin)


def _nsa_constants(T):
    n_cmp = (T - CMP_BLOCK) // CMP_STRIDE + 1
    n_slc = T // SEL_BLOCK
    cs = np.arange(n_cmp) * CMP_STRIDE
    ss = np.arange(n_slc) * SEL_BLOCK
    ov = np.clip(np.minimum(cs[:, None] + CMP_BLOCK, ss[None, :] + SEL_BLOCK)
                 - np.maximum(cs[:, None], ss[None, :]), 0, None) / CMP_STRIDE
    ovt = np.zeros((n_slc, LANES), np.float32)
    ovt[:, :n_cmp] = ov.T
    et = (np.arange(T)[:, None] // SEL_BLOCK == np.arange(LANES)[None, :]).astype(np.float32) * (-NEG)
    return jnp.asarray(ovt), jnp.asarray(et, dtype=BF16)


def _compress_weights(pe_cmp, w_cmp1, w_cmp2):
    half = CMP_BLOCK // 2
    assert N_KV_GROUPS == 2

    def blockdiag(w):
        z = jnp.zeros_like(w)
        return jnp.concatenate([jnp.concatenate([w, z], axis=-1), jnp.concatenate([z, w], axis=-1)], axis=-2)

    w1bd = blockdiag(w_cmp1.reshape(2, CMP_BLOCK, N_HEAD_DIM, CMP_HIDDEN).astype(BF16))
    w2bd = blockdiag(w_cmp2)
    pe2 = jnp.tile(pe_cmp, (1, 1, N_KV_GROUPS))
    return pe2, w1bd[:, :half].astype(BF16), w1bd[:, half:].astype(BF16), w2bd.astype(BF16)


def kernel(x, g_norm_mix, w_in, b_in, b_fgate, conv_qk, g_mlstm_head, pe_cmp, w_cmp1, w_cmp2,
           rel_bias, w_branch, w_out, g_norm_ffn, w_gate, w_up, w_down, g_final):
    B, T, D = x.shape
    assert D == D_MODEL and w_in.shape[0] == 1, "one residual block (DEPTH == 1)"
    N = B * T
    x2 = x.reshape(N, D)

    idx, scale = _inproj_perm()
    w_r = _wprep_t(w_in[0].T, g_norm_mix[0].reshape(1, D), idx, scale)
    b_r = _gather_cols(b_in[0].reshape(1, -1), idx, scale, F32)
    cw = jnp.zeros((8, 2 * M_WIDTH), F32).at[:CONV_WIDTH].set(conv_qk[0])
    mqk, mv, mo, nq, cin, nsw, mg, small = _inproj(x2, w_r, b_r)

    bf8 = jnp.concatenate([jnp.zeros((M_HEADS,), F32), b_fgate[0].astype(F32)])
    bfc = jnp.broadcast_to(jnp.tile(bf8, T // M_CHUNK)[:, None], (8 * (T // M_CHUNK), M_CHUNK))
    ltri = jnp.asarray(np.tril(np.ones((M_CHUNK, M_CHUNK), np.float32)))
    r3 = lambda a: a.reshape(B, T, a.shape[-1])
    ym = _mlstm(r3(mqk), r3(mv), r3(mo), r3(small), cw, bfc, g_mlstm_head[0].reshape(1, M_WIDTH), ltri)

    pe2, w1t, w1b, w2 = _compress_weights(pe_cmp[0], w_cmp1[0], w_cmp2[0])
    ckv = _compress(cin.reshape(2, B, T, LANES), pe2, w1t, w1b, w2)
    tbl = rel_bias.astype(F32).T.reshape(-1)
    nqt = T // TQ
    bt = _bias_tiles(tbl, nqt + 1, "tok")
    wb = _bias_tiles(tbl, min(WINDOW // TQ + 1, nqt) + 1, "win")
    cb = _bias_tiles(tbl, nqt, "cmp", n_cmp=(T - CMP_BLOCK) // CMP_STRIDE + 1)
    ov, emat = _nsa_constants(T)
    yn = _nsa(r3(nq), r3(nsw), ckv, r3(small), bt, wb, cb, ov, emat)

    wbm = w_branch[0, 0].astype(BF16)
    wbn = jnp.concatenate(
        [w_branch[0, 1, (g * N_REP + r) * N_HEAD_DIM:(g * N_REP + r + 1) * N_HEAD_DIM]
         for r in range(N_REP) for g in range(N_KV_GROUPS)], axis=0).astype(BF16)
    ident = lambda n: (np.arange(n, dtype=np.int32), np.ones((n,), np.float32))
    g_ffn = g_norm_ffn[0].reshape(D, 1)
    out = _merge(x2, ym.reshape(N, M_WIDTH), yn.reshape(N, N_WIDTH), mg, wbm, wbn,
                 w_out[0].astype(BF16), _wprep(w_gate[0], g_ffn, *ident(D_FF)),
                 _wprep(w_up[0], g_ffn, *ident(D_FF)),
                 _wprep(w_down[0], jnp.ones((D_FF, 1), F32), *ident(D)),
                 g_final.reshape(1, D))
    return out.reshape(B, T, D)
```

```python
import functools

import numpy as np
import jax
import jax.numpy as jnp
from jax import lax
from jax.experimental import pallas as pl
from jax.experimental.pallas import tpu as pltpu

F32 = jnp.float32
BF16 = jnp.bfloat16
HI = lax.Precision.HIGHEST

D_MODEL = 1024
M_HEADS = 4
M_HEAD_DIM = 128
M_WIDTH = M_HEADS * M_HEAD_DIM
M_CHUNK = 128
CONV_WIDTH = 4
N_HEADS = 8
N_KV_GROUPS = 2
N_REP = N_HEADS // N_KV_GROUPS
N_HEAD_DIM = 64
N_WIDTH = N_HEADS * N_HEAD_DIM
N_KV_WIDTH = N_KV_GROUPS * N_HEAD_DIM
CMP_BLOCK = 32
CMP_STRIDE = 16
CMP_HIDDEN = 2 * N_HEAD_DIM
SEL_BLOCK = 64
SEL_TOPK = 16
WINDOW = 512
REL_BUCKETS = 32
REL_MAX_DIST = 1024
N_BRANCH = 2
D_FF = 2816
RMS_EPS = 1e-6
BIG = 1e9
NEG = -1e30

LANES = 128
TQ = 128
NSA_BATCH = 1
NSA_UNROLL_TILES = False
VMEM_LIMIT = 56 * 1024 * 1024

_OFF_MQ, _OFF_MK, _OFF_MV, _OFF_MO = 0, 512, 1024, 1536
_OFF_MI, _OFF_MF, _OFF_NQ, _OFF_NKV = 2048, 2052, 2056, 2568
_OFF_NGATE, _OFF_MERGE, _D_IN = 3336, 3360, 5408
_SMALL_NGATE = 8

_SEGS = (("mqk", 1024), ("mv", 512), ("mo", 512), ("nq", 512), ("kc", 128),
         ("vc", 128), ("nsw", 512), ("mg", 2048), ("small", 128))
_D_IN_PAD = sum(w for _, w in _SEGS)
LOG2E = 1.4426950408889634


def _dot(a, b, **kw):
    return jnp.dot(a, b, preferred_element_type=F32, **kw)


def _dot_nt(a, b):
    return lax.dot_general(a, b, (((1,), (1,)), ((), ())), preferred_element_type=F32)


def _dot_tn(a, b):
    return lax.dot_general(a, b, (((0,), (0,)), ((), ())), preferred_element_type=F32)


def _const_spec(shape):
    nd = len(shape)
    return pl.BlockSpec(shape, lambda *_: (0,) * nd, pipeline_mode=pl.Buffered(1))


def _bucket_thresholds():
    max_exact = REL_BUCKETS // 2
    assert REL_MAX_DIST == 64 * max_exact and REL_BUCKETS - max_exact == 16
    thr = []
    for k in range(1, REL_BUCKETS - max_exact):
        t = max_exact
        while t ** 8 < (max_exact ** 8) * (2 ** (3 * k)):
            t += 1
        thr.append(t)
    return tuple(thr)


_BUCKET_THR = _bucket_thresholds()


def _inproj_perm():
    idx = np.zeros((_D_IN_PAD,), np.int32)
    scale = np.zeros((_D_IN_PAD,), np.float32)
    pos = 0

    def put(cols, s=1.0):
        nonlocal pos
        n = len(cols)
        idx[pos:pos + n] = cols
        scale[pos:pos + n] = s
        pos += n

    put(np.arange(_OFF_MQ, _OFF_MV))
    put(np.arange(_OFF_MV, _OFF_MO))
    put(np.arange(_OFF_MO, _OFF_MI))
    nq = np.zeros((N_WIDTH,), np.int32)
    for r in range(N_REP):
        for g in range(N_KV_GROUPS):
            for d in range(N_HEAD_DIM):
                nq[r * 128 + g * 64 + d] = _OFF_NQ + (g * N_REP + r) * N_HEAD_DIM + d
    put(nq, N_HEAD_DIM ** -0.5 * LOG2E)
    kv = lambda j, g: np.arange(_OFF_NKV + (j * N_KV_GROUPS + g) * N_HEAD_DIM,
                                _OFF_NKV + (j * N_KV_GROUPS + g + 1) * N_HEAD_DIM)
    put(np.arange(_OFF_NKV, _OFF_NKV + 256))
    put(np.concatenate([kv(2, 0), kv(2, 1)]))
    put(np.concatenate([kv(4, 0), kv(4, 1)]))
    put(np.concatenate([kv(3, 0), kv(3, 1)]))
    put(np.concatenate([kv(5, 0), kv(5, 1)]))
    put(np.arange(_OFF_MERGE, _D_IN))
    put(np.arange(_OFF_MI, _OFF_MI + 8))
    put(np.arange(_OFF_NGATE, _OFF_NGATE + 24))
    pos += LANES - 32
    assert pos == _D_IN_PAD
    return idx, scale


def _gather_cols(a, idx, scale, dtype):
    pieces = []
    start = 0
    n = len(idx)
    for c in range(1, n + 1):
        same = c < n and scale[c] == scale[start] and (scale[c] == 0.0 or idx[c] == idx[c - 1] + 1)
        if not same:
            if scale[start] == 0.0:
                piece = jnp.zeros((a.shape[0], c - start), dtype)
            else:
                piece = a[:, int(idx[start]):int(idx[start]) + (c - start)]
                if scale[start] != 1.0:
                    piece = piece * float(scale[start])
            pieces.append(piece.astype(dtype))
            start = c
    return jnp.concatenate(pieces, axis=1)


def _col_runs(idx, scale):
    runs = []
    start = 0
    n = len(idx)
    for c in range(1, n + 1):
        same = c < n and scale[c] == scale[start] and (scale[c] == 0.0 or idx[c] == idx[c - 1] + 1)
        if not same:
            runs.append((start, int(idx[start]), c - start, float(scale[start])))
            start = c
    return runs


def _wprep_kernel(w_ref, g_ref, o_ref, *, runs):
    g = g_ref[...]
    for dst, src, n, scale in runs:
        if scale == 0.0:
            o_ref[:, dst:dst + n] = jnp.zeros((o_ref.shape[0], n), o_ref.dtype)
        else:
            o_ref[:, dst:dst + n] = (w_ref[:, src:src + n] * (g * scale)).astype(o_ref.dtype)


def _wprep_t_kernel(wt_ref, g_ref, o_ref, *, runs):
    g = g_ref[...]
    for dst, src, n, scale in runs:
        if scale == 0.0:
            o_ref[dst:dst + n, :] = jnp.zeros((n, o_ref.shape[1]), o_ref.dtype)
        else:
            o_ref[dst:dst + n, :] = (wt_ref[src:src + n, :] * (g * scale)).astype(o_ref.dtype)


def _wprep_t(wt, g, idx, scale, tc=256):
    rows, cols = wt.shape
    assert cols % tc == 0
    return pl.pallas_call(
        functools.partial(_wprep_t_kernel, runs=_col_runs(idx, scale)),
        out_shape=jax.ShapeDtypeStruct((len(idx), cols), BF16), grid=(cols // tc,),
        in_specs=[pl.BlockSpec((rows, tc), lambda i: (0, i)), pl.BlockSpec((1, tc), lambda i: (0, i))],
        out_specs=pl.BlockSpec((len(idx), tc), lambda i: (0, i)), name="wprep_t",
        compiler_params=pltpu.CompilerParams(dimension_semantics=("arbitrary",),
                                             vmem_limit_bytes=VMEM_LIMIT),
    )(wt, g)


def _wprep(w, g, idx, scale, tr=256):
    rows, cols = w.shape
    assert rows % tr == 0
    return pl.pallas_call(
        functools.partial(_wprep_kernel, runs=_col_runs(idx, scale)),
        out_shape=jax.ShapeDtypeStruct((rows, len(idx)), BF16), grid=(rows // tr,),
        in_specs=[pl.BlockSpec((tr, cols), lambda i: (i, 0)), pl.BlockSpec((tr, 1), lambda i: (i, 0))],
        out_specs=pl.BlockSpec((tr, len(idx)), lambda i: (i, 0)), name="wprep",
        compiler_params=pltpu.CompilerParams(dimension_semantics=("arbitrary",),
                                             vmem_limit_bytes=VMEM_LIMIT),
    )(w, g)


def _inproj_kernel(x_ref, w_ref, b_ref, mqk_ref, mv_ref, mo_ref, nq_ref,
                   cin_ref, nsw_ref, mg_ref, small_ref):
    x = x_ref[...]
    rs = lax.rsqrt(jnp.mean(x * x, axis=-1, keepdims=True) + RMS_EPS)
    xb = x.astype(BF16)

    def seg(a, n):
        return _dot_nt(xb, w_ref[a:a + n, :]) * rs + b_ref[:, a:a + n]

    off = {}
    pos = 0
    for name, w in _SEGS:
        off[name] = pos
        pos += w

    mqk_ref[...] = seg(off["mqk"], 1024).astype(BF16)
    mv_ref[...] = seg(off["mv"], 512).astype(BF16)
    mo_ref[...] = seg(off["mo"], 512).astype(BF16)
    nq_ref[...] = seg(off["nq"], 512).astype(BF16)
    cc = seg(off["kc"], 2 * LANES)
    cin_ref[0] = cc[:, :LANES]
    cin_ref[1] = cc[:, LANES:]
    nsw_ref[:, 0:2 * LANES] = seg(off["nsw"], 2 * LANES).astype(BF16)
    vv = seg(off["nsw"] + 2 * LANES, 2 * LANES)
    lo = lax.broadcasted_iota(jnp.int32, (x.shape[0], LANES), 1) < N_HEAD_DIM
    for j in range(2):
        vj = vv[:, j * LANES:(j + 1) * LANES]
        nsw_ref[:, (2 + 2 * j) * LANES:(3 + 2 * j) * LANES] = jnp.where(lo, vj, 1.0).astype(BF16)
        nsw_ref[:, (3 + 2 * j) * LANES:(4 + 2 * j) * LANES] = jnp.where(lo, 1.0, vj).astype(BF16)
    mg_ref[...] = seg(off["mg"], 2048).astype(BF16)
    small_ref[...] = seg(off["small"], 128)


def _inproj(x2, w, b, tm=1024):
    n = x2.shape[0]
    assert n % tm == 0
    row = lambda width: pl.BlockSpec((tm, width), lambda i: (i, 0))
    out_shape = (
        jax.ShapeDtypeStruct((n, 1024), BF16), jax.ShapeDtypeStruct((n, 512), BF16),
        jax.ShapeDtypeStruct((n, 512), BF16), jax.ShapeDtypeStruct((n, 512), BF16),
        jax.ShapeDtypeStruct((2, n, 128), F32), jax.ShapeDtypeStruct((n, 768), BF16),
        jax.ShapeDtypeStruct((n, 2048), BF16), jax.ShapeDtypeStruct((n, 128), F32))
    out_specs = (row(1024), row(512), row(512), row(512),
                 pl.BlockSpec((2, tm, 128), lambda i: (0, i, 0)), row(768), row(2048), row(128))
    return pl.pallas_call(
        _inproj_kernel, out_shape=out_shape, grid=(n // tm,),
        in_specs=[row(D_MODEL), _const_spec((_D_IN_PAD, D_MODEL)), _const_spec((1, _D_IN_PAD))],
        out_specs=out_specs, name="inproj",
        compiler_params=pltpu.CompilerParams(dimension_semantics=("arbitrary",),
                                             vmem_limit_bytes=VMEM_LIMIT),
    )(x2, w, b)


def _mlstm_kernel(mqk_ref, mv_ref, mo_ref, small_ref, cw_ref, bfc_ref, gh_ref, ltri_ref,
                  ym_ref, ebuf, c_st, m_st, gx, g_bc, g_d, g_a, g_mrow, g_blast, g_amax):
    L = M_CHUNK
    H = M_HEADS
    c = pl.program_id(1)
    n_chunks = gx.shape[0] // 8
    n_sub = mqk_ref.shape[1] // L
    lanes =lax.broadcasted_iota(jnp.int32, gx.shape, 1)

    @pl.when(c == 0)
    def _():
        ebuf[0:8, :] = jnp.zeros((8, 2 * M_WIDTH), F32)
        c_st[...] = jnp.zeros_like(c_st)
        m_st[...] = jnp.zeros_like(m_st)
        for cc in range(n_chunks):
            gx[cc * 8:(cc + 1) * 8, :] = small_ref[0, cc * L:(cc + 1) * L, :].T[0:8, :]
        x = gx[...]
        fg = x + bfc_ref[...]
        lf = jnp.minimum(fg, 0.0) - jnp.log1p(jnp.exp(-jnp.abs(fg)))
        bc = lax.dot_general(lf, ltri_ref[...], (((1,), (1,)), ((), ())),
                             precision=HI, preferred_element_type=F32)
        ig = pltpu.roll(x, H, 0)
        blast = bc[:, L - 1:L]
        a_all = blast - bc + ig
        d = ig - bc
        cm = d
        sh = 1
        while sh < L:
            cm = jnp.maximum(cm, jnp.where(lanes >= sh, pltpu.roll(cm, sh, 1), -jnp.inf))
            sh *= 2
        g_bc[...] = bc
        g_d[...] = d
        g_a[...] = a_all
        g_mrow[...] = bc + cm
        g_blast[...] = jnp.broadcast_to(blast, gx.shape)
        g_amax[...] = jnp.broadcast_to(jnp.max(a_all, axis=-1, keepdims=True), gx.shape)

    row = lax.broadcasted_iota(jnp.int32, (L, L), 0)
    col = lax.broadcasted_iota(jnp.int32, (L, L), 1)
    causal = row >= col
    ones = jnp.ones((L, M_HEAD_DIM), BF16)

    def chunk(j):
        rows = slice(j * L, (j + 1) * L)
        ebuf[8:8 + L, :] = mqk_ref[0, rows, :].astype(F32)
        ext = ebuf[...]
        conv = cw_ref[CONV_WIDTH - 1:CONV_WIDTH, :] * ext[8:, :]
        for k in range(1, CONV_WIDTH):
            conv = conv + cw_ref[CONV_WIDTH - 1 - k:CONV_WIDTH - k, :] * pltpu.roll(ext, k, 0)[8:, :]
        ebuf[0:8, :] = ebuf[L:L + 8, :]
        qk = (conv * jax.nn.sigmoid(conv)).astype(BF16)

        r8 = pl.ds(pl.multiple_of((c * n_sub + j) * 8, 8), 8)
        bc = g_bc[r8, :]
        dt = g_d[r8, :]
        blast = g_blast[r8, :]
        m_prev = m_st[...]
        m_new = jnp.maximum(blast + m_prev, g_amax[r8, :])
        decay = jnp.exp(blast + m_prev - m_new)
        w_row = jnp.exp(g_a[r8, :] - m_new)
        log_inter = bc + m_prev
        m_i = jnp.maximum(g_mrow[r8, :], log_inter)
        k_scale = M_HEAD_DIM ** -0.5
        s_inter = jnp.exp(log_inter - m_i) * k_scale
        emi = jnp.exp(-m_i)
        u = bc - m_i + float(np.log(k_scale))
        cols = jnp.concatenate([u, s_inter, emi, w_row, jnp.zeros((LANES - 32, L), F32)], axis=0).T
        u_all, s_inter_all = cols[:, 0:8], cols[:, 8:16]
        emi_all, w_all = cols[:, 16:24], cols[:, 24:32]

        for h in range(M_HEADS):
            sl = slice(h * M_HEAD_DIM, (h + 1) * M_HEAD_DIM)
            qb = qk[:, sl]
            kb = qk[:, M_WIDTH + h * M_HEAD_DIM:M_WIDTH + (h + 1) * M_HEAD_DIM]
            v = mv_ref[0, rows, sl]
            cn_prev = c_st[h]

            g = H + h
            logp = jnp.where(causal, u_all[:, g:g + 1] + dt[g:g + 1, :], -jnp.inf)
            p = jnp.exp(logp) * _dot_nt(qb, kb)
            nd = (_dot(p.astype(BF16), jnp.concatenate([v, ones], axis=1))
                  + s_inter_all[:, g:g + 1] * _dot(qb, cn_prev.astype(BF16)))
            hh = nd[:, :M_HEAD_DIM] / jnp.maximum(jnp.abs(nd[:, M_HEAD_DIM:]), emi_all[:, g:g + 1])
            hh = hh * lax.rsqrt(jnp.mean(hh * hh, axis=-1, keepdims=True) + RMS_EPS) * gh_ref[:, sl]
            ym_ref[0, rows, sl] = (jax.nn.sigmoid(mo_ref[0, rows, sl].astype(F32)) * hh).astype(ym_ref.dtype)

            w_col = w_all[:, g:g + 1]
            vw = jnp.concatenate([v.astype(F32) * w_col, jnp.broadcast_to(w_col, (L, M_HEAD_DIM))], axis=1)
            c_st[h] = decay[g:g + 1, 0:1] * cn_prev + _dot_tn(kb, vw.astype(BF16))

        m_st[...] = m_new

    for j in range(n_sub):
        chunk(j)


def _mlstm(mqk, mv, mo, small, cw, bf, gh, ltri, n_sub=4):
    B, T, _ = mqk.shape
    L = M_CHUNK
    assert T % (n_sub * L) == 0
    nc = T // L
    assert L == LANES and 8 * nc <= LANES
    blk = lambda w: pl.BlockSpec((1, n_sub * L, w), lambda b, c: (b, c, 0))
    gate = lambda: pltpu.VMEM((8 * nc, L), F32)
    return pl.pallas_call(
        _mlstm_kernel, out_shape=jax.ShapeDtypeStruct((B, T, M_WIDTH), BF16),
        grid=(B, nc // n_sub),
        in_specs=[blk(2 * M_WIDTH), blk(M_WIDTH), blk(M_WIDTH),
                  pl.BlockSpec((1, T, LANES), lambda b, c: (b, 0, 0)),
                  _const_spec((8, 2 * M_WIDTH)), _const_spec((8 * nc, L)),
                  _const_spec((1, M_WIDTH)), _const_spec((L, L))],
        out_specs=blk(M_WIDTH),
        scratch_shapes=[pltpu.VMEM((L + 8, 2 * M_WIDTH), F32),
                        pltpu.VMEM((M_HEADS, M_HEAD_DIM, 2 * M_HEAD_DIM), F32),
                        pltpu.VMEM((8, L), F32),
                        gate(), gate(), gate(), gate(), gate(), gate(), gate()],
        name="mlstm",
        compiler_params=pltpu.CompilerParams(dimension_semantics=("arbitrary", "arbitrary")),
    )(mqk, mv, mo, small, cw, bf, gh, ltri)


def _compress_kernel(x_ref, pe_ref, w1t_ref, w1b_ref, w2_ref, o_ref, *, nch, nbt):
    half = CMP_BLOCK // 2
    rows = nbt * nch
    top = jnp.zeros((rows, 2 * CMP_HIDDEN), F32)
    bot = jnp.zeros((rows, 2 * CMP_HIDDEN), F32)
    for p in range(half):
        xp = jnp.concatenate([x_ref[0, e, pl.ds(p, nch, stride=CMP_STRIDE), :] for e in range(nbt)], axis=0)
        top = top + _dot((xp + pe_ref[0, p:p + 1, :]).astype(BF16), w1t_ref[0, p])
        bot = bot + _dot((xp + pe_ref[0, half + p:half + p + 1, :]).astype(BF16), w1b_ref[0, p])
    hid = top + pltpu.roll(bot, rows - 1, 0)
    act = hid * jax.nn.sigmoid(hid)
    out = _dot(act.astype(BF16), w2_ref[0]).astype(o_ref.dtype)
    for e in range(nbt):
        o_ref[0, e] = out[e * nch:(e + 1) * nch]


def _compress(cin, pe2, w1t, w1b, w2):
    _, B, T, _ = cin.shape
    nch = T // CMP_STRIDE
    nbt = 4 if B % 4 == 0 else 1
    sel = lambda *shape: pl.BlockSpec((1,) + shape, lambda j, b: (j,) + (0,) * len(shape))
    return pl.pallas_call(
        functools.partial(_compress_kernel, nch=nch, nbt=nbt),
        out_shape=jax.ShapeDtypeStruct((2, B, nch, LANES), BF16),
        grid=(2, B // nbt),
        in_specs=[pl.BlockSpec((1, nbt, T, LANES), lambda j, b: (j, b, 0, 0)),
                  sel(CMP_BLOCK, LANES), sel(CMP_BLOCK // 2, LANES, 2 * CMP_HIDDEN),
                  sel(CMP_BLOCK // 2, LANES, 2 * CMP_HIDDEN), sel(2 * CMP_HIDDEN, LANES)],
        out_specs=pl.BlockSpec((1, nbt, nch, LANES), lambda j, b: (j, b, 0, 0)),
        name="compress",
        compiler_params=pltpu.CompilerParams(dimension_semantics=("arbitrary", "arbitrary")),
    )(cin, pe2, w1t, w1b, w2)


def _bias_kernel(tbl_ref, o_ref, *, kind, n_cmp):
    pid = pl.program_id(0)
    toeplitz = kind != "cmp"
    if toeplitz:
        k = lax.broadcasted_iota(jnp.int32, (8, 2 * LANES), 1)
        dist = (pid - 1) * TQ + jnp.where(k < LANES, -k, 2 * LANES - k)
    else:
        al = lax.broadcasted_iota(jnp.int32, (CMP_STRIDE, 2 * LANES), 0)
        m = lax.broadcasted_iota(jnp.int32, (CMP_STRIDE, 2 * LANES), 1)
        dist = pid * TQ - (CMP_BLOCK - 1) + al - CMP_STRIDE * jnp.where(m < LANES, m, m - 2 * LANES)
    n = jnp.maximum(dist, 0)
    cnt = jnp.zeros_like(n)
    for t in _BUCKET_THR:
        cnt = cnt + jnp.where(n >= t, 1, 0)
    bucket = jnp.where(n < REL_BUCKETS // 2, n, REL_BUCKETS // 2 + cnt)
    if kind == "tok":
        madd = jnp.where((dist >= 0) & (pid > 0), 0.0, NEG)
    elif kind == "win":
        madd = jnp.where((dist >= 0) & (dist < WINDOW) & (pid > 0), 0.0, NEG)
    else:
        madd = jnp.where(dist >= 0, 0.0, NEG)
        valid_c = lax.broadcasted_iota(jnp.int32, (TQ, LANES), 1) < n_cmp
    for h in range(N_HEADS):
        val = jnp.zeros(dist.shape, F32)
        for bb in range(REL_BUCKETS):
            val = jnp.where(bucket == bb, tbl_ref[h * REL_BUCKETS + bb], val)
        val = val * LOG2E + madd
        if toeplitz:
            g = jnp.broadcast_to(val[0:1, :], (TQ, 2 * LANES))
            val = pltpu.roll(g, 0, 1, stride=1, stride_axis=0)[:, :LANES]
        else:
            bands = [val[:, :LANES]] + [pltpu.roll(val, ah, 1)[:, :LANES] for ah in range(1, TQ // CMP_STRIDE)]
            val = jnp.where(valid_c, jnp.concatenate(bands, axis=0), NEG)
        o_ref[0, h * TQ:(h + 1) * TQ, :] = val


def _bias_tiles(tbl, n_tiles, kind, n_cmp=0):
    return pl.pallas_call(
        functools.partial(_bias_kernel, kind=kind, n_cmp=n_cmp),
        out_shape=jax.ShapeDtypeStruct((n_tiles, N_HEADS * TQ, LANES), F32),
        grid=(n_tiles,),
        in_specs=[pl.BlockSpec(memory_space=pltpu.SMEM)],
        out_specs=pl.BlockSpec((1, N_HEADS * TQ, LANES), lambda i: (i, 0, 0)),
        name="bias_" + kind,
        compiler_params=pltpu.CompilerParams(dimension_semantics=("arbitrary",)),
    )(tbl)


def _nsa_kernel(nq4_ref, ks_ref, kw_ref, vs0_ref, vs1_ref, vw0_ref, vw1_ref, kc_ref, vc_ref,
                small4_ref, bt_ref, wb_ref, cb4_ref, ovt_ref, et_ref, yn_ref,
                lhs_sc, z_sc, zw_sc, seln_sc, partc_sc, *, nb, n_slc, n_top, n_win, n_chunks_max,
                unroll_tiles):
    HR = N_HEADS * TQ
    GR = N_REP * TQ
    CH = 4 * TQ
    G4 = 4
    vs_refs = (vs0_ref, vs1_ref)
    vw_refs = (vw0_ref, vw1_ref)
    lane = lax.broadcasted_iota(jnp.int32, (TQ, LANES), 1)
    lo = lane < N_HEAD_DIM

    def stack_heads(q_all):
        zero = jnp.zeros((TQ, LANES), BF16)
        parts = []
        for g in range(N_KV_GROUPS):
            for r in range(N_REP):
                qr = q_all[:, r * LANES:(r + 1) * LANES]
                parts.append(jnp.where(lo if g == 0 else jnp.logical_not(lo), qr, zero))
        return parts

    def pair(o, r):
        return jnp.where(lo, o[r * TQ:(r + 1) * TQ], o[(N_REP + r) * TQ:(N_REP + r + 1) * TQ])

    def normed(acc, r):
        den = jnp.where(lo, acc[(N_REP + r) * TQ:(N_REP + r + 1) * TQ], acc[r * TQ:(r + 1) * TQ])
        return pair(acc, r) / pltpu.roll(den, N_HEAD_DIM, 1)

    def gate(sg, r, ci):
        c0 = _SMALL_NGATE + r * 3 + ci
        c1 = _SMALL_NGATE + (N_REP + r) * 3 + ci
        return jnp.where(lo, sg[:, c0:c0 + 1], sg[:, c1:c1 + 1])

    def group_prep(s, n_chunks, qi):
        qs4 = jnp.concatenate(
            [p for j in range(G4) for p in stack_heads(nq4_ref[s, j * TQ:(j + 1) * TQ, :])], axis=0)
        rows = lax.broadcasted_iota(jnp.int32, (G4 * HR, 1), 0)
        t_rows = (qi + jnp.right_shift(rows, 10)) * TQ + (rows & (TQ - 1))

        z = _dot_nt(qs4, kc_ref[0, s]) + cb4_ref[...].reshape(G4 * HR, LANES)
        e = jnp.exp2(z - jnp.max(z, axis=-1, keepdims=True))
        l = _dot(e.astype(BF16), jnp.ones((LANES, LANES), BF16))
        p_c = e * jnp.where(t_rows >= CMP_BLOCK - 1, 1.0 / jnp.maximum(l, 1e-30), 0.0)
        o_c = _dot(p_c.astype(BF16), vc_ref[0, s])
        sg4 = jax.nn.sigmoid(small4_ref[s])
        for j in range(G4):
            for r in range(N_REP):
                partc_sc[s, j, r] = (gate(sg4[j * TQ:(j + 1) * TQ], r, 0)
                                     * pair(o_c[j * HR:(j + 1) * HR], r))

        W = G4 * N_KV_GROUPS * TQ
        jb = lax.broadcasted_iota(jnp.int32, (n_slc, W), 0)
        col = lax.broadcasted_iota(jnp.int32, (n_slc, W), 1)
        tq = (qi + jnp.right_shift(col, 8)) * TQ + (col & (TQ - 1))
        cur = jnp.right_shift(tq, 6)
        elig = jb <= cur
        if n_chunks * CH <= n_top * SEL_BLOCK:
            sel = jnp.where(elig, 1.0, 0.0)
        else:
            forced = (jb == 0) | (jb == cur) | (jb == cur - 1)
            psums = []
            for j in range(G4):
                for g in range(N_KV_GROUPS):
                    base = j * HR + g * GR
                    ps = p_c[base:base + TQ]
                    for r in range(1, N_REP):
                        ps = ps + p_c[base + r * TQ:base + (r + 1) * TQ]
                    psums.append(ps)
            imp = lax.dot_general(ovt_ref[...], jnp.concatenate(psums, axis=0), (((1,), (1,)), ((), ())),
                                  precision=HI, preferred_element_type=F32)
            score = jnp.where(elig, jnp.where(forced, BIG, imp), -BIG)
            cnt = jnp.zeros((n_slc, W), F32)
            for i in range(n_slc):
                si = score[i:i + 1, :]
                tie = jnp.where(jb > i, 1.0, 0.0)
                cnt = cnt + jnp.where(si > score, 1.0, jnp.where(si == score, tie, 0.0))
            sel = jnp.where((cnt < n_top) & (score > -BIG / 2), 1.0, 0.0)
        seln = jnp.concatenate([sel - 1.0, jnp.zeros((LANES - n_slc, W), F32)], axis=0).T.astype(BF16)
        for j in range(G4):
            for g in range(N_KV_GROUPS):
                seln_sc[s, j, g] = seln[(j * N_KV_GROUPS + g) * TQ:(j * N_KV_GROUPS + g + 1) * TQ]

    def tiles(n_pairs, qi, jq):
        static = isinstance(jq, int)
        clamp0 = (lambda i: max(i, 0)) if static else (lambda i: jnp.maximum(i, 0))
        qrows = slice(jq * TQ, (jq + 1) * TQ) if static else pl.ds(pl.multiple_of(jq * TQ, TQ), TQ)
        n_wt = min(qi + 1, n_win) if static else n_win
        w0 = clamp0(qi - (n_win - 1))
        woff = w0 * TQ if static else pl.multiple_of(w0 * TQ, TQ)
        elems = range(nb)
        cat = lambda xs: xs[0] if len(xs) == 1 else jnp.concatenate(xs, axis=0)
        rep = lambda t: t if nb == 1 else jnp.concatenate([t] * nb, axis=0)
        parts = [stack_heads(nq4_ref[s, qrows, :]) for s in elems]
        qs = [jnp.concatenate(parts[s], axis=0) for s in elems]

        def bias_tiles(ref, first, count, tile0):
            tiles_ = [ref[clamp0(qi - tile0 - (first + u) + 1)] for u in range(count)]
            return rep(tiles_[0] if count == 1 else jnp.concatenate(tiles_, axis=1))

        zmax = None
        for j0 in range(0, n_wt, 2):
            nj = min(2, n_wt - j0)
            zp = (cat([_dot_nt(qs[s], kw_ref[s, pl.ds(woff + j0 * TQ, nj * TQ), :]) for s in elems])
                  + bias_tiles(wb_ref, j0, nj, w0))
            zw_sc[:, j0 * TQ:(j0 + nj) * TQ] = zp
            for u in range(nj):
                zj = zp[:, u * TQ:(u + 1) * TQ]
                zmax = zj if zmax is None else jnp.maximum(zmax, zj)
        mw = jnp.max(zmax, axis=-1, keepdims=True)
        pw = jnp.exp2(zw_sc[:, 0:n_wt * TQ] - mw).astype(BF16)
        acc_w = cat([_dot(pw[s * HR + g * GR:s * HR + (g + 1) * GR],
                          vw_refs[g][s, pl.ds(woff, n_wt * TQ), :])
                     for s in elems for g in range(N_KV_GROUPS)])

        for s in elems:
            for g in range(N_KV_GROUPS):
                seln = seln_sc[s, jq, g]
                for r in range(N_REP):
                    h = g * N_REP + r
                    lhs_sc[s, h * TQ:(h + 1) * TQ, :] = jnp.concatenate([parts[s][h], seln], axis=1)

        zmax = None
        for pi in range(n_pairs):
            k0 = pi * 2 * TQ
            zp = (cat([_dot_nt(lhs_sc[s], jnp.concatenate([ks_ref[s, k0:k0 + 2 * TQ, :],
                                                           et_ref[k0:k0 + 2 * TQ, :]], axis=1))
                       for s in elems])
                  + bias_tiles(bt_ref, 2 * pi, 2, 0))
            z_sc[pi // 2, :, (pi % 2) * 2 * TQ:(pi % 2 + 1) * 2 * TQ] = zp
            for u in range(2):
                zt = zp[:, u * TQ:(u + 1) * TQ]
                zmax = zt if zmax is None else jnp.maximum(zmax, zt)
        ms = jnp.max(zmax, axis=-1, keepdims=True)
        acc = [None] * (nb * N_KV_GROUPS)
        for c in range((n_pairs + 1) // 2):
            width = min(CH, n_pairs * 2 * TQ - c * CH)
            p = jnp.exp2(z_sc[c, :, 0:width] - ms).astype(BF16)
            for s in elems:
                for g in range(N_KV_GROUPS):
                    i = s * N_KV_GROUPS + g
                    d = _dot(p[i * GR:(i + 1) * GR], vs_refs[g][s, c * CH:c * CH + width, :])
                    acc[i] = d if acc[i] is None else acc[i] + d
        acc_s = cat(acc)

        for s in elems:
            sg = jax.nn.sigmoid(small4_ref[s, qrows, :])
            a_s = acc_s[s * HR:(s + 1) * HR]
            a_w = acc_w[s * HR:(s + 1) * HR]
            for r in range(N_REP):
                out = (partc_sc[s, jq, r] + gate(sg, r, 1) * normed(a_s, r)
                       + gate(sg, r, 2) * normed(a_w, r))
                yn_ref[s, qrows, r * LANES:(r + 1) * LANES] = out.astype(yn_ref.dtype)

    for nc in range(1, n_chunks_max + 1):
        @pl.when(pl.program_id(1) == nc - 1)
        def _(nc=nc):
            q0 = G4 * (nc - 1)
            for s in range(nb):
                group_prep(s, nc, q0)
            if unroll_tiles:
                for jq in range(G4):
                    tiles((q0 + jq) // 2 + 1, q0 + jq, jq)
            else:
                def body(jq, carry):
                    tiles(2 * nc, q0 + jq, jq)
                    return carry
                lax.fori_loop(0, G4, body, 0)


def _nsa(nq, nsw, ckv, small, bt, wb, cb, ov, emat):
    B, T, _ = nq.shape
    assert T == 2048, "single 128-wide compressed-key tile assumes T == 2048"
    nqt = T // TQ
    n_cmp = (T - CMP_BLOCK) // CMP_STRIDE + 1
    n_slc = T // SEL_BLOCK
    n_top = min(SEL_TOPK, n_slc)
    n_win = wb.shape[0] - 1
    HR = N_HEADS * TQ
    assert nqt % 4 == 0 and n_slc % 8 == 0 and n_slc <= LANES
    nb = NSA_BATCH if B % NSA_BATCH == 0 else 1
    n_chunks_max = nqt // 4
    kv = lambda j: pl.BlockSpec((nb, T, LANES), lambda b, g: (b, 0, j))
    ck = lambda j: pl.BlockSpec((1, nb, T // CMP_STRIDE, LANES), lambda b, g: (j, b, 0, 0))
    kern = functools.partial(_nsa_kernel, nb=nb, n_slc=n_slc, n_top=n_top, n_win=n_win,
                             n_chunks_max=n_chunks_max, unroll_tiles=NSA_UNROLL_TILES)
    return pl.pallas_call(
        kern, out_shape=jax.ShapeDtypeStruct((B, T, N_WIDTH), BF16),
        grid=(B // nb, n_chunks_max),
        in_specs=[pl.BlockSpec((nb, 4 * TQ, N_WIDTH), lambda b, g: (b, g, 0)),
                  kv(0), kv(1), kv(2), kv(3), kv(4), kv(5), ck(0), ck(1),
                  pl.BlockSpec((nb, 4 * TQ, LANES), lambda b, g: (b, g, 0)),
                  _const_spec(bt.shape), _const_spec(wb.shape),
                  pl.BlockSpec((4, HR, LANES), lambda b, g: (g, 0, 0)),
                  _const_spec(ov.shape), _const_spec(emat.shape)],
        out_specs=pl.BlockSpec((nb, 4 * TQ, N_WIDTH), lambda b, g: (b, g, 0)),
        scratch_shapes=[pltpu.VMEM((nb, HR, 2 * LANES), BF16),
                        pltpu.VMEM((n_chunks_max, nb * HR, 4 * TQ), F32),
                        pltpu.VMEM((nb * HR, n_win * TQ), F32),
                        pltpu.VMEM((nb, 4, N_KV_GROUPS, TQ, LANES), BF16),
                        pltpu.VMEM((nb, 4, N_REP, TQ, LANES), F32)],
        name="nsa",
        compiler_params=pltpu.CompilerParams(dimension_semantics=("arbitrary", "arbitrary"),
                                             vmem_limit_bytes=VMEM_LIMIT),
    )(nq, nsw, nsw, nsw, nsw, nsw, nsw, ckv, ckv, small, bt, wb, cb, ov, emat)


def _merge_kernel(x_ref, ym_ref, yn_ref, mg_ref, wbm_ref, wbn_ref, wo_ref,
                  wg_ref, wu_ref, wd_ref, gfin_ref, o_ref, *, tf):
    bm = _dot(ym_ref[...], wbm_ref[...])
    bn = _dot(yn_ref[...], wbn_ref[...])
    mixed = (jax.nn.sigmoid(mg_ref[:, :D_MODEL].astype(F32)) * bm
             + jax.nn.sigmoid(mg_ref[:, D_MODEL:].astype(F32)) * bn)
    h = x_ref[...] + _dot(mixed.astype(BF16), wo_ref[...])
    rs = lax.rsqrt(jnp.mean(h * h, axis=-1, keepdims=True) + RMS_EPS)
    hb = h.astype(BF16)
    acc = jnp.zeros(h.shape, F32)
    for j in range(D_FF // tf):
        gg = _dot(hb, wg_ref[:, j * tf:(j + 1) * tf]) * rs
        uu = _dot(hb, wu_ref[:, j * tf:(j + 1) * tf]) * rs
        act = (gg * jax.nn.sigmoid(gg) * uu).astype(BF16)
        acc = acc + _dot(act, wd_ref[j * tf:(j + 1) * tf, :])
    h2 = h + acc
    o_ref[...] = h2 * lax.rsqrt(jnp.mean(h2 * h2, axis=-1, keepdims=True) + RMS_EPS) * gfin_ref[...]


def _merge(x2, ym, yn, mg, wbm, wbn, wo, wg, wu, wd, gfin, tm=512, tf=256):
    n = x2.shape[0]
    assert n % tm == 0 and D_FF % tf == 0
    row = lambda width: pl.BlockSpec((tm, width), lambda i: (i, 0))
    return pl.pallas_call(
        functools.partial(_merge_kernel, tf=tf),
        out_shape=jax.ShapeDtypeStruct((n, D_MODEL), F32), grid=(n // tm,),
        in_specs=[row(D_MODEL), row(M_WIDTH), row(N_WIDTH), row(N_BRANCH * D_MODEL),
                  _const_spec(wbm.shape), _const_spec(wbn.shape), _const_spec(wo.shape),
                  _const_spec(wg.shape), _const_spec(wu.shape),
                  _const_spec(wd.shape), _const_spec(gfin.shape)],
        out_specs=row(D_MODEL), name="merge_ffn",
        compiler_params=pltpu.CompilerParams(dimension_semantics=("arbitrary",),
                                             vmem_limit_bytes=VMEM_LIMIT),
    )(x2, ym, yn, mg, wbm, wbn, wo, wg, wu, wd, gfin)


def _nsa_constants(T):
    n_cmp = (T - CMP_BLOCK) // CMP_STRIDE + 1
    n_slc = T // SEL_BLOCK
    cs = np.arange(n_cmp) * CMP_STRIDE
    ss = np.arange(n_slc) * SEL_BLOCK
    ov = np.clip(np.minimum(cs[:, None] + CMP_BLOCK, ss[None, :] + SEL_BLOCK)
                 - np.maximum(cs[:, None], ss[None, :]), 0, None) / CMP_STRIDE
    ovt = np.zeros((n_slc, LANES), np.float32)
    ovt[:, :n_cmp] = ov.T
    et = (np.arange(T)[:, None] // SEL_BLOCK == np.arange(LANES)[None, :]).astype(np.float32) * (-NEG)
    return jnp.asarray(ovt), jnp.asarray(et, dtype=BF16)


def _compress_weights(pe_cmp, w_cmp1, w_cmp2):
    half = CMP_BLOCK // 2
    assert N_KV_GROUPS == 2

    def blockdiag(w):
        z = jnp.zeros_like(w)
        return jnp.concatenate([jnp.concatenate([w, z], axis=-1), jnp.concatenate([z, w], axis=-1)], axis=-2)

    w1bd = blockdiag(w_cmp1.reshape(2, CMP_BLOCK, N_HEAD_DIM, CMP_HIDDEN).astype(BF16))
    w2bd = blockdiag(w_cmp2)
    pe2 = jnp.tile(pe_cmp, (1, 1, N_KV_GROUPS))
    return pe2, w1bd[:, :half].astype(BF16), w1bd[:, half:].astype(BF16), w2bd.astype(BF16)


def kernel(x, g_norm_mix, w_in, b_in, b_fgate, conv_qk, g_mlstm_head, pe_cmp, w_cmp1, w_cmp2,
           rel_bias, w_branch, w_out, g_norm_ffn, w_gate, w_up, w_down, g_final):
    B, T, D = x.shape
    assert D == D_MODEL and w_in.shape[0] == 1, "one residual block (DEPTH == 1)"
    N = B * T
    x2 = x.reshape(N, D)

    idx, scale = _inproj_perm()
    w_r = _wprep_t(w_in[0].T, g_norm_mix[0].reshape(1, D), idx, scale)
    b_r = _gather_cols(b_in[0].reshape(1, -1), idx, scale, F32)
    cw = jnp.zeros((8, 2 * M_WIDTH), F32).at[:CONV_WIDTH].set(conv_qk[0])
    mqk, mv, mo, nq, cin, nsw, mg, small = _inproj(x2, w_r, b_r)

    bf8 = jnp.concatenate([jnp.zeros((M_HEADS,), F32), b_fgate[0].astype(F32)])
    bfc = jnp.broadcast_to(jnp.tile(bf8, T // M_CHUNK)[:, None], (8 * (T // M_CHUNK), M_CHUNK))
    ltri = jnp.asarray(np.tril(np.ones((M_CHUNK, M_CHUNK), np.float32)))
    r3 = lambda a: a.reshape(B, T, a.shape[-1])
    ym = _mlstm(r3(mqk), r3(mv), r3(mo), r3(small), cw, bfc, g_mlstm_head[0].reshape(1, M_WIDTH), ltri)

    pe2, w1t, w1b, w2 = _compress_weights(pe_cmp[0], w_cmp1[0], w_cmp2[0])
    ckv = _compress(cin.reshape(2, B, T, LANES), pe2, w1t, w1b, w2)
    tbl = rel_bias.astype(F32).T.reshape(-1)
    nqt = T // TQ
    bt = _bias_tiles(tbl, nqt + 1, "tok")
    wb = _bias_tiles(tbl, min(WINDOW // TQ + 1, nqt) + 1, "win")
    cb = _bias_tiles(tbl, nqt, "cmp", n_cmp=(T - CMP_BLOCK) // CMP_STRIDE + 1)
    ov, emat = _nsa_constants(T)
    yn = _nsa(r3(nq), r3(nsw), ckv, r3(small), bt, wb, cb, ov, emat)

    wbm = w_branch[0, 0].astype(BF16)
    wbn = jnp.concatenate(
        [w_branch[0, 1, (g * N_REP + r) * N_HEAD_DIM:(g * N_REP + r + 1) * N_HEAD_DIM]
         for r in range(N_REP) for g in range(N_KV_GROUPS)], axis=0).astype(BF16)
    ident = lambda n: (np.arange(n, dtype=np.int32), np.ones((n,), np.float32))
    g_ffn = g_norm_ffn[0].reshape(D, 1)
    out = _merge(x2, ym.reshape(N, M_WIDTH), yn.reshape(N, N_WIDTH), mg, wbm, wbn,
                 w_out[0].astype(BF16), _wprep(w_gate[0], g_ffn, *ident(D_FF)),
                 _wprep(w_up[0], g_ffn, *ident(D_FF)),
                 _wprep(w_down[0], jnp.ones((D_FF, 1), F32), *ident(D)),
                 g_final.reshape(1, D))
    return out.reshape(B, T, D)
```

```python
import functools

import numpy as np
import jax
import jax.numpy as jnp
from jax import lax
from jax.experimental import pallas as pl
from jax.experimental.pallas import tpu as pltpu

F32 = jnp.float32
BF16 = jnp.bfloat16
HI = lax.Precision.HIGHEST

D_MODEL = 1024
M_HEADS = 4
M_HEAD_DIM = 128
M_WIDTH = M_HEADS * M_HEAD_DIM
M_CHUNK = 128
CONV_WIDTH = 4
N_HEADS = 8
N_KV_GROUPS = 2
N_REP = N_HEADS // N_KV_GROUPS
N_HEAD_DIM = 64
N_WIDTH = N_HEADS * N_HEAD_DIM
N_KV_WIDTH = N_KV_GROUPS * N_HEAD_DIM
CMP_BLOCK = 32
CMP_STRIDE = 16
CMP_HIDDEN = 2 * N_HEAD_DIM
SEL_BLOCK = 64
SEL_TOPK = 16
WINDOW = 512
REL_BUCKETS = 32
REL_MAX_DIST = 1024
N_BRANCH = 2
D_FF = 2816
RMS_EPS = 1e-6
BIG = 1e9
NEG = -1e30

LANES = 128
TQ = 128
NSA_BATCH = 1
NSA_UNROLL_TILES = True
VMEM_LIMIT = 56 * 1024 * 1024

_OFF_MQ, _OFF_MK, _OFF_MV, _OFF_MO = 0, 512, 1024, 1536
_OFF_MI, _OFF_MF, _OFF_NQ, _OFF_NKV = 2048, 2052, 2056, 2568
_OFF_NGATE, _OFF_MERGE, _D_IN = 3336, 3360, 5408
_SMALL_NGATE = 8

_SEGS = (("mqk", 1024), ("mv", 512), ("mo", 512), ("nq", 512), ("kc", 128),
         ("vc", 128), ("nsw", 512), ("mg", 2048), ("small", 128))
_D_IN_PAD = sum(w for _, w in _SEGS)
LOG2E = 1.4426950408889634


def _dot(a, b, **kw):
    return jnp.dot(a, b, preferred_element_type=F32, **kw)


def _dot_nt(a, b):
    return lax.dot_general(a, b, (((1,), (1,)), ((), ())), preferred_element_type=F32)


def _dot_tn(a, b):
    return lax.dot_general(a, b, (((0,), (0,)), ((), ())), preferred_element_type=F32)


def _const_spec(shape):
    nd = len(shape)
    return pl.BlockSpec(shape, lambda *_: (0,) * nd, pipeline_mode=pl.Buffered(1))


def _bucket_thresholds():
    max_exact = REL_BUCKETS // 2
    assert REL_MAX_DIST == 64 * max_exact and REL_BUCKETS - max_exact == 16
    thr = []
    for k in range(1, REL_BUCKETS - max_exact):
        t = max_exact
        while t ** 8 < (max_exact ** 8) * (2 ** (3 * k)):
            t += 1
        thr.append(t)
    return tuple(thr)


_BUCKET_THR = _bucket_thresholds()


def _inproj_perm():
    idx = np.zeros((_D_IN_PAD,), np.int32)
    scale = np.zeros((_D_IN_PAD,), np.float32)
    pos = 0

    def put(cols, s=1.0):
        nonlocal pos
        n = len(cols)
        idx[pos:pos + n] = cols
        scale[pos:pos + n] = s
        pos += n

    put(np.arange(_OFF_MQ, _OFF_MV))
    put(np.arange(_OFF_MV, _OFF_MO))
    put(np.arange(_OFF_MO, _OFF_MI))
    nq = np.zeros((N_WIDTH,), np.int32)
    for r in range(N_REP):
        for g in range(N_KV_GROUPS):
            for d in range(N_HEAD_DIM):
                nq[r * 128 + g * 64 + d] = _OFF_NQ + (g * N_REP + r) * N_HEAD_DIM + d
    put(nq, N_HEAD_DIM ** -0.5 * LOG2E)
    kv = lambda j, g: np.arange(_OFF_NKV + (j * N_KV_GROUPS + g) * N_HEAD_DIM,
                                _OFF_NKV + (j * N_KV_GROUPS + g + 1) * N_HEAD_DIM)
    put(np.arange(_OFF_NKV, _OFF_NKV + 256))
    put(np.concatenate([kv(2, 0), kv(2, 1)]))
    put(np.concatenate([kv(4, 0), kv(4, 1)]))
    put(np.concatenate([kv(3, 0), kv(3, 1)]))
    put(np.concatenate([kv(5, 0), kv(5, 1)]))
    put(np.arange(_OFF_MERGE, _D_IN))
    put(np.arange(_OFF_MI, _OFF_MI + 8))
    put(np.arange(_OFF_NGATE, _OFF_NGATE + 24))
    pos += LANES - 32
    assert pos == _D_IN_PAD
    return idx, scale


def _gather_cols(a, idx, scale, dtype):
    pieces = []
    start = 0
    n = len(idx)
    for c in range(1, n + 1):
        same = c < n and scale[c] == scale[start] and (scale[c] == 0.0 or idx[c] == idx[c - 1] + 1)
        if not same:
            if scale[start] == 0.0:
                piece = jnp.zeros((a.shape[0], c - start), dtype)
            else:
                piece = a[:, int(idx[start]):int(idx[start]) + (c - start)]
                if scale[start] != 1.0:
                    piece = piece * float(scale[start])
            pieces.append(piece.astype(dtype))
            start = c
    return jnp.concatenate(pieces, axis=1)


def _col_runs(idx, scale):
    runs = []
    start = 0
    n = len(idx)
    for c in range(1, n + 1):
        same = c < n and scale[c] == scale[start] and (scale[c] == 0.0 or idx[c] == idx[c - 1] + 1)
        if not same:
            runs.append((start, int(idx[start]), c - start, float(scale[start])))
            start = c
    return runs


def _wprep_kernel(w_ref, g_ref, o_ref, *, runs):
    g = g_ref[...]
    for dst, src, n, scale in runs:
        if scale == 0.0:
            o_ref[:, dst:dst + n] = jnp.zeros((o_ref.shape[0], n), o_ref.dtype)
        else:
            o_ref[:, dst:dst + n] = (w_ref[:, src:src + n] * (g * scale)).astype(o_ref.dtype)


def _wprep_t_kernel(wt_ref, g_ref, o_ref, *, runs):
    g = g_ref[...]
    for dst, src, n, scale in runs:
        if scale == 0.0:
            o_ref[dst:dst + n, :] = jnp.zeros((n, o_ref.shape[1]), o_ref.dtype)
        else:
            o_ref[dst:dst + n, :] = (wt_ref[src:src + n, :] * (g * scale)).astype(o_ref.dtype)


def _wprep_t(wt, g, idx, scale, tc=256):
    rows, cols = wt.shape
    assert cols % tc == 0
    return pl.pallas_call(
        functools.partial(_wprep_t_kernel, runs=_col_runs(idx, scale)),
        out_shape=jax.ShapeDtypeStruct((len(idx), cols), BF16), grid=(cols // tc,),
        in_specs=[pl.BlockSpec((rows, tc), lambda i: (0, i)), pl.BlockSpec((1, tc), lambda i: (0, i))],
        out_specs=pl.BlockSpec((len(idx), tc), lambda i: (0, i)), name="wprep_t",
        compiler_params=pltpu.CompilerParams(dimension_semantics=("arbitrary",),
                                             vmem_limit_bytes=VMEM_LIMIT),
    )(wt, g)


def _wprep(w, g, idx, scale, tr=256):
    rows, cols = w.shape
    assert rows % tr == 0
    return pl.pallas_call(
        functools.partial(_wprep_kernel, runs=_col_runs(idx, scale)),
        out_shape=jax.ShapeDtypeStruct((rows, len(idx)), BF16), grid=(rows // tr,),
        in_specs=[pl.BlockSpec((tr, cols), lambda i: (i, 0)), pl.BlockSpec((tr, 1), lambda i: (i, 0))],
        out_specs=pl.BlockSpec((tr, len(idx)), lambda i: (i, 0)), name="wprep",
        compiler_params=pltpu.CompilerParams(dimension_semantics=("arbitrary",),
                                             vmem_limit_bytes=VMEM_LIMIT),
    )(w, g)


def _inproj_kernel(x_ref, w_ref, b_ref, mqk_ref, mv_ref, mo_ref, nq_ref,
                   cin_ref, nsw_ref, mg_ref, small_ref):
    x = x_ref[...]
    rs = lax.rsqrt(jnp.mean(x * x, axis=-1, keepdims=True) + RMS_EPS)
    xb = x.astype(BF16)

    def seg(a, n):
        return _dot_nt(xb, w_ref[a:a + n, :]) * rs + b_ref[:, a:a + n]

    off = {}
    pos = 0
    for name, w in _SEGS:
        off[name] = pos
        pos += w

    mqk_ref[...] = seg(off["mqk"], 1024).astype(BF16)
    mv_ref[...] = seg(off["mv"], 512).astype(BF16)
    mo_ref[...] = seg(off["mo"], 512).astype(BF16)
    nq_ref[...] = seg(off["nq"], 512).astype(BF16)
    cc = seg(off["kc"], 2 * LANES)
    cin_ref[0] = cc[:, :LANES]
    cin_ref[1] = cc[:, LANES:]
    nsw_ref[:, 0:2 * LANES] = seg(off["nsw"], 2 * LANES).astype(BF16)
    vv = seg(off["nsw"] + 2 * LANES, 2 * LANES)
    lo = lax.broadcasted_iota(jnp.int32, (x.shape[0], LANES), 1) < N_HEAD_DIM
    for j in range(2):
        vj = vv[:, j * LANES:(j + 1) * LANES]
        nsw_ref[:, (2 + 2 * j) * LANES:(3 + 2 * j) * LANES] = jnp.where(lo, vj, 1.0).astype(BF16)
        nsw_ref[:, (3 + 2 * j) * LANES:(4 + 2 * j) * LANES] = jnp.where(lo, 1.0, vj).astype(BF16)
    mg_ref[...] = seg(off["mg"], 2048).astype(BF16)
    small_ref[...] = seg(off["small"], 128)


def _inproj(x2, w, b, tm=1024):
    n = x2.shape[0]
    assert n % tm == 0
    row = lambda width: pl.BlockSpec((tm, width), lambda i: (i, 0))
    out_shape = (
        jax.ShapeDtypeStruct((n, 1024), BF16), jax.ShapeDtypeStruct((n, 512), BF16),
        jax.ShapeDtypeStruct((n, 512), BF16), jax.ShapeDtypeStruct((n, 512), BF16),
        jax.ShapeDtypeStruct((2, n, 128), F32), jax.ShapeDtypeStruct((n, 768), BF16),
        jax.ShapeDtypeStruct((n, 2048), BF16), jax.ShapeDtypeStruct((n, 128), F32))
    out_specs = (row(1024), row(512), row(512), row(512),
                 pl.BlockSpec((2, tm, 128), lambda i: (0, i, 0)), row(768), row(2048), row(128))
    return pl.pallas_call(
        _inproj_kernel, out_shape=out_shape, grid=(n // tm,),
        in_specs=[row(D_MODEL), _const_spec((_D_IN_PAD, D_MODEL)), _const_spec((1, _D_IN_PAD))],
        out_specs=out_specs, name="inproj",
        compiler_params=pltpu.CompilerParams(dimension_semantics=("arbitrary",),
                                             vmem_limit_bytes=VMEM_LIMIT),
    )(x2, w, b)


def _mlstm_kernel(mqk_ref, mv_ref, mo_ref, small_ref, cw_ref, bfc_ref, gh_ref, ltri_ref,
                  ym_ref, ebuf, c_st, m_st, gx, g_bc, g_d, g_a, g_mrow, g_blast, g_amax):
    L = M_CHUNK
    H = M_HEADS
    c = pl.program_id(1)
    n_chunks = gx.shape[0] // 8
    n_sub = mqk_ref.shape[1] // L
    lanes =lax.broadcasted_iota(jnp.int32, gx.shape, 1)

    @pl.when(c == 0)
    def _():
        ebuf[0:8, :] = jnp.zeros((8, 2 * M_WIDTH), F32)
        c_st[...] = jnp.zeros_like(c_st)
        m_st[...] = jnp.zeros_like(m_st)
        for cc in range(n_chunks):
            gx[cc * 8:(cc + 1) * 8, :] = small_ref[0, cc * L:(cc + 1) * L, :].T[0:8, :]
        x = gx[...]
        fg = x + bfc_ref[...]
        lf = jnp.minimum(fg, 0.0) - jnp.log1p(jnp.exp(-jnp.abs(fg)))
        bc = lax.dot_general(lf, ltri_ref[...], (((1,), (1,)), ((), ())),
                             precision=HI, preferred_element_type=F32)
        ig = pltpu.roll(x, H, 0)
        blast = bc[:, L - 1:L]
        a_all = blast - bc + ig
        d = ig - bc
        cm = d
        sh = 1
        while sh < L:
            cm = jnp.maximum(cm, jnp.where(lanes >= sh, pltpu.roll(cm, sh, 1), -jnp.inf))
            sh *= 2
        g_bc[...] = bc
        g_d[...] = d
        g_a[...] = a_all
        g_mrow[...] = bc + cm
        g_blast[...] = jnp.broadcast_to(blast, gx.shape)
        g_amax[...] = jnp.broadcast_to(jnp.max(a_all, axis=-1, keepdims=True), gx.shape)

    row = lax.broadcasted_iota(jnp.int32, (L, L), 0)
    col = lax.broadcasted_iota(jnp.int32, (L, L), 1)
    causal = row >= col
    ones = jnp.ones((L, M_HEAD_DIM), BF16)

    def chunk(j):
        rows = slice(j * L, (j + 1) * L)
        ebuf[8:8 + L, :] = mqk_ref[0, rows, :].astype(F32)
        ext = ebuf[...]
        conv = cw_ref[CONV_WIDTH - 1:CONV_WIDTH, :] * ext[8:, :]
        for k in range(1, CONV_WIDTH):
            conv = conv + cw_ref[CONV_WIDTH - 1 - k:CONV_WIDTH - k, :] * pltpu.roll(ext, k, 0)[8:, :]
        ebuf[0:8, :] = ebuf[L:L + 8, :]
        qk = (conv * jax.nn.sigmoid(conv)).astype(BF16)

        r8 = pl.ds(pl.multiple_of((c * n_sub + j) * 8, 8), 8)
        bc = g_bc[r8, :]
        dt = g_d[r8, :]
        blast = g_blast[r8, :]
        m_prev = m_st[...]
        m_new = jnp.maximum(blast + m_prev, g_amax[r8, :])
        decay = jnp.exp(blast + m_prev - m_new)
        w_row = jnp.exp(g_a[r8, :] - m_new)
        log_inter = bc + m_prev
        m_i = jnp.maximum(g_mrow[r8, :], log_inter)
        k_scale = M_HEAD_DIM ** -0.5
        s_inter = jnp.exp(log_inter - m_i) * k_scale
        emi = jnp.exp(-m_i)
        u = bc - m_i + float(np.log(k_scale))
        cols = jnp.concatenate([u, s_inter, emi, w_row, jnp.zeros((LANES - 32, L), F32)], axis=0).T
        u_all, s_inter_all = cols[:, 0:8], cols[:, 8:16]
        emi_all, w_all = cols[:, 16:24], cols[:, 24:32]

        for h in range(M_HEADS):
            sl = slice(h * M_HEAD_DIM, (h + 1) * M_HEAD_DIM)
            qb = qk[:, sl]
            kb = qk[:, M_WIDTH + h * M_HEAD_DIM:M_WIDTH + (h + 1) * M_HEAD_DIM]
            v = mv_ref[0, rows, sl]
            cn_prev = c_st[h]

            g = H + h
            logp = jnp.where(causal, u_all[:, g:g + 1] + dt[g:g + 1, :], -jnp.inf)
            p = jnp.exp(logp) * _dot_nt(qb, kb)
            nd = (_dot(p.astype(BF16), jnp.concatenate([v, ones], axis=1))
                  + s_inter_all[:, g:g + 1] * _dot(qb, cn_prev.astype(BF16)))
            hh = nd[:, :M_HEAD_DIM] / jnp.maximum(jnp.abs(nd[:, M_HEAD_DIM:]), emi_all[:, g:g + 1])
            hh = hh * lax.rsqrt(jnp.mean(hh * hh, axis=-1, keepdims=True) + RMS_EPS) * gh_ref[:, sl]
            ym_ref[0, rows, sl] = (jax.nn.sigmoid(mo_ref[0, rows, sl].astype(F32)) * hh).astype(ym_ref.dtype)

            w_col = w_all[:, g:g + 1]
            vw = jnp.concatenate([v.astype(F32) * w_col, jnp.broadcast_to(w_col, (L, M_HEAD_DIM))], axis=1)
            c_st[h] = decay[g:g + 1, 0:1] * cn_prev + _dot_tn(kb, vw.astype(BF16))

        m_st[...] = m_new

    for j in range(n_sub):
        chunk(j)


def _mlstm(mqk, mv, mo, small, cw, bf, gh, ltri, n_sub=4):
    B, T, _ = mqk.shape
    L = M_CHUNK
    assert T % (n_sub * L) == 0
    nc = T // L
    assert L == LANES and 8 * nc <= LANES
    blk = lambda w: pl.BlockSpec((1, n_sub * L, w), lambda b, c: (b, c, 0))
    gate = lambda: pltpu.VMEM((8 * nc, L), F32)
    return pl.pallas_call(
        _mlstm_kernel, out_shape=jax.ShapeDtypeStruct((B, T, M_WIDTH), BF16),
        grid=(B, nc // n_sub),
        in_specs=[blk(2 * M_WIDTH), blk(M_WIDTH), blk(M_WIDTH),
                  pl.BlockSpec((1, T, LANES), lambda b, c: (b, 0, 0)),
                  _const_spec((8, 2 * M_WIDTH)), _const_spec((8 * nc, L)),
                  _const_spec((1, M_WIDTH)), _const_spec((L, L))],
        out_specs=blk(M_WIDTH),
        scratch_shapes=[pltpu.VMEM((L + 8, 2 * M_WIDTH), F32),
                        pltpu.VMEM((M_HEADS, M_HEAD_DIM, 2 * M_HEAD_DIM), F32),
                        pltpu.VMEM((8, L), F32),
                        gate(), gate(), gate(), gate(), gate(), gate(), gate()],
        name="mlstm",
        compiler_params=pltpu.CompilerParams(dimension_semantics=("arbitrary", "arbitrary")),
    )(mqk, mv, mo, small, cw, bf, gh, ltri)


def _compress_kernel(x_ref, pe_ref, w1t_ref, w1b_ref, w2_ref, o_ref, *, nch, nbt):
    half = CMP_BLOCK // 2
    rows = nbt * nch
    top = jnp.zeros((rows, 2 * CMP_HIDDEN), F32)
    bot = jnp.zeros((rows, 2 * CMP_HIDDEN), F32)
    for p in range(half):
        xp = jnp.concatenate([x_ref[0, e, pl.ds(p, nch, stride=CMP_STRIDE), :] for e in range(nbt)], axis=0)
        top = top + _dot((xp + pe_ref[0, p:p + 1, :]).astype(BF16), w1t_ref[0, p])
        bot = bot + _dot((xp + pe_ref[0, half + p:half + p + 1, :]).astype(BF16), w1b_ref[0, p])
    hid = top + pltpu.roll(bot, rows - 1, 0)
    act = hid * jax.nn.sigmoid(hid)
    out = _dot(act.astype(BF16), w2_ref[0]).astype(o_ref.dtype)
    for e in range(nbt):
        o_ref[0, e] = out[e * nch:(e + 1) * nch]


def _compress(cin, pe2, w1t, w1b, w2):
    _, B, T, _ = cin.shape
    nch = T // CMP_STRIDE
    nbt = 4 if B % 4 == 0 else 1
    sel = lambda *shape: pl.BlockSpec((1,) + shape, lambda j, b: (j,) + (0,) * len(shape))
    return pl.pallas_call(
        functools.partial(_compress_kernel, nch=nch, nbt=nbt),
        out_shape=jax.ShapeDtypeStruct((2, B, nch, LANES), BF16),
        grid=(2, B // nbt),
        in_specs=[pl.BlockSpec((1, nbt, T, LANES), lambda j, b: (j, b, 0, 0)),
                  sel(CMP_BLOCK, LANES), sel(CMP_BLOCK // 2, LANES, 2 * CMP_HIDDEN),
                  sel(CMP_BLOCK // 2, LANES, 2 * CMP_HIDDEN), sel(2 * CMP_HIDDEN, LANES)],
        out_specs=pl.BlockSpec((1, nbt, nch, LANES), lambda j, b: (j, b, 0, 0)),
        name="compress",
        compiler_params=pltpu.CompilerParams(dimension_semantics=("arbitrary", "arbitrary")),
    )(cin, pe2, w1t, w1b, w2)


def _bias_kernel(tbl_ref, o_ref, *, kind, n_cmp):
    pid = pl.program_id(0)
    toeplitz = kind != "cmp"
    if toeplitz:
        k = lax.broadcasted_iota(jnp.int32, (8, 2 * LANES), 1)
        dist = (pid - 1) * TQ + jnp.where(k < LANES, -k, 2 * LANES - k)
    else:
        al = lax.broadcasted_iota(jnp.int32, (CMP_STRIDE, 2 * LANES), 0)
        m = lax.broadcasted_iota(jnp.int32, (CMP_STRIDE, 2 * LANES), 1)
        dist = pid * TQ - (CMP_BLOCK - 1) + al - CMP_STRIDE * jnp.where(m < LANES, m, m - 2 * LANES)
    n = jnp.maximum(dist, 0)
    cnt = jnp.zeros_like(n)
    for t in _BUCKET_THR:
        cnt = cnt + jnp.where(n >= t, 1, 0)
    bucket = jnp.where(n < REL_BUCKETS // 2, n, REL_BUCKETS // 2 + cnt)
    if kind == "tok":
        madd = jnp.where((dist >= 0) & (pid > 0), 0.0, NEG)
    elif kind == "win":
        madd = jnp.where((dist >= 0) & (dist < WINDOW) & (pid > 0), 0.0, NEG)
    else:
        madd = jnp.where(dist >= 0, 0.0, NEG)
        valid_c = lax.broadcasted_iota(jnp.int32, (TQ, LANES), 1) < n_cmp
    for h in range(N_HEADS):
        val = jnp.zeros(dist.shape, F32)
        for bb in range(REL_BUCKETS):
            val = jnp.where(bucket == bb, tbl_ref[h * REL_BUCKETS + bb], val)
        val = val * LOG2E + madd
        if toeplitz:
            g = jnp.broadcast_to(val[0:1, :], (TQ, 2 * LANES))
            val = pltpu.roll(g, 0, 1, stride=1, stride_axis=0)[:, :LANES]
        else:
            bands = [val[:, :LANES]] + [pltpu.roll(val, ah, 1)[:, :LANES] for ah in range(1, TQ // CMP_STRIDE)]
            val = jnp.where(valid_c, jnp.concatenate(bands, axis=0), NEG)
        o_ref[0, h * TQ:(h + 1) * TQ, :] = val


def _bias_tiles(tbl, n_tiles, kind, n_cmp=0):
    return pl.pallas_call(
        functools.partial(_bias_kernel, kind=kind, n_cmp=n_cmp),
        out_shape=jax.ShapeDtypeStruct((n_tiles, N_HEADS * TQ, LANES), F32),
        grid=(n_tiles,),
        in_specs=[pl.BlockSpec(memory_space=pltpu.SMEM)],
        out_specs=pl.BlockSpec((1, N_HEADS * TQ, LANES), lambda i: (i, 0, 0)),
        name="bias_" + kind,
        compiler_params=pltpu.CompilerParams(dimension_semantics=("arbitrary",)),
    )(tbl)


def _nsa_kernel(nq4_ref, ks_ref, kw_ref, vs0_ref, vs1_ref, vw0_ref, vw1_ref, kc_ref, vc_ref,
                small4_ref, bt_ref, wb_ref, cb4_ref, ovt_ref, et_ref, yn_ref,
                lhs_sc, z_sc, zw_sc, seln_sc, partc_sc, *, nb, n_slc, n_top, n_win, n_chunks_max,
                unroll_tiles):
    HR = N_HEADS * TQ
    GR = N_REP * TQ
    CH = 4 * TQ
    G4 = 4
    vs_refs = (vs0_ref, vs1_ref)
    vw_refs = (vw0_ref, vw1_ref)
    lane = lax.broadcasted_iota(jnp.int32, (TQ, LANES), 1)
    lo = lane < N_HEAD_DIM

    def stack_heads(q_all):
        zero = jnp.zeros((TQ, LANES), BF16)
        parts = []
        for g in range(N_KV_GROUPS):
            for r in range(N_REP):
                qr = q_all[:, r * LANES:(r + 1) * LANES]
                parts.append(jnp.where(lo if g == 0 else jnp.logical_not(lo), qr, zero))
        return parts

    def pair(o, r):
        return jnp.where(lo, o[r * TQ:(r + 1) * TQ], o[(N_REP + r) * TQ:(N_REP + r + 1) * TQ])

    def normed(acc, r):
        den = jnp.where(lo, acc[(N_REP + r) * TQ:(N_REP + r + 1) * TQ], acc[r * TQ:(r + 1) * TQ])
        return pair(acc, r) / pltpu.roll(den, N_HEAD_DIM, 1)

    def gate(sg, r, ci):
        c0 = _SMALL_NGATE + r * 3 + ci
        c1 = _SMALL_NGATE + (N_REP + r) * 3 + ci
        return jnp.where(lo, sg[:, c0:c0 + 1], sg[:, c1:c1 + 1])

    def group_prep(s, n_chunks, qi):
        qs4 = jnp.concatenate(
            [p for j in range(G4) for p in stack_heads(nq4_ref[s, j * TQ:(j + 1) * TQ, :])], axis=0)
        rows = lax.broadcasted_iota(jnp.int32, (G4 * HR, 1), 0)
        t_rows = (qi + jnp.right_shift(rows, 10)) * TQ + (rows & (TQ - 1))

        z = _dot_nt(qs4, kc_ref[0, s]) + cb4_ref[...].reshape(G4 * HR, LANES)
        e = jnp.exp2(z - jnp.max(z, axis=-1, keepdims=True))
        l = _dot(e.astype(BF16), jnp.ones((LANES, LANES), BF16))
        p_c = e * jnp.where(t_rows >= CMP_BLOCK - 1, 1.0 / jnp.maximum(l, 1e-30), 0.0)
        o_c = _dot(p_c.astype(BF16), vc_ref[0, s])
        sg4 = jax.nn.sigmoid(small4_ref[s])
        for j in range(G4):
            for r in range(N_REP):
                partc_sc[s, j, r] = (gate(sg4[j * TQ:(j + 1) * TQ], r, 0)
                                     * pair(o_c[j * HR:(j + 1) * HR], r))

        W = G4 * N_KV_GROUPS * TQ
        jb = lax.broadcasted_iota(jnp.int32, (n_slc, W), 0)
        col = lax.broadcasted_iota(jnp.int32, (n_slc, W), 1)
        tq = (qi + jnp.right_shift(col, 8)) * TQ + (col & (TQ - 1))
        cur = jnp.right_shift(tq, 6)
        elig = jb <= cur
        if n_chunks * CH <= n_top * SEL_BLOCK:
            sel = jnp.where(elig, 1.0, 0.0)
        else:
            forced = (jb == 0) | (jb == cur) | (jb == cur - 1)
            psums = []
            for j in range(G4):
                for g in range(N_KV_GROUPS):
                    base = j * HR + g * GR
                    ps = p_c[base:base + TQ]
                    for r in range(1, N_REP):
                        ps = ps + p_c[base + r * TQ:base + (r + 1) * TQ]
                    psums.append(ps)
            imp = lax.dot_general(ovt_ref[...], jnp.concatenate(psums, axis=0), (((1,), (1,)), ((), ())),
                                  precision=HI, preferred_element_type=F32)
            score = jnp.where(elig, jnp.where(forced, BIG, imp), -BIG)
            cnt = jnp.zeros((n_slc, W), F32)
            for i in range(n_slc):
                si = score[i:i + 1, :]
                tie = jnp.where(jb > i, 1.0, 0.0)
                cnt = cnt + jnp.where(si > score, 1.0, jnp.where(si == score, tie, 0.0))
            sel = jnp.where((cnt < n_top) & (score > -BIG / 2), 1.0, 0.0)
        seln = jnp.concatenate([sel - 1.0, jnp.zeros((LANES - n_slc, W), F32)], axis=0).T.astype(BF16)
        for j in range(G4):
            for g in range(N_KV_GROUPS):
                seln_sc[s, j, g] = seln[(j * N_KV_GROUPS + g) * TQ:(j * N_KV_GROUPS + g + 1) * TQ]

    def tiles(n_pairs, qi, jq):
        static = isinstance(jq, int)
        clamp0 = (lambda i: max(i, 0)) if static else (lambda i: jnp.maximum(i, 0))
        qrows = slice(jq * TQ, (jq + 1) * TQ) if static else pl.ds(pl.multiple_of(jq * TQ, TQ), TQ)
        n_wt = min(qi + 1, n_win) if static else n_win
        w0 = clamp0(qi - (n_win - 1))
        woff = w0 * TQ if static else pl.multiple_of(w0 * TQ, TQ)
        elems = range(nb)
        cat = lambda xs: xs[0] if len(xs) == 1 else jnp.concatenate(xs, axis=0)
        rep = lambda t: t if nb == 1 else jnp.concatenate([t] * nb, axis=0)
        parts = [stack_heads(nq4_ref[s, qrows, :]) for s in elems]
        qs = [jnp.concatenate(parts[s], axis=0) for s in elems]

        def bias_tiles(ref, first, count, tile0):
            tiles_ = [ref[clamp0(qi - tile0 - (first + u) + 1)] for u in range(count)]
            return rep(tiles_[0] if count == 1 else jnp.concatenate(tiles_, axis=1))

        zmax = None
        for j0 in range(0, n_wt, 2):
            nj = min(2, n_wt - j0)
            zp = (cat([_dot_nt(qs[s], kw_ref[s, pl.ds(woff + j0 * TQ, nj * TQ), :]) for s in elems])
                  + bias_tiles(wb_ref, j0, nj, w0))
            zw_sc[:, j0 * TQ:(j0 + nj) * TQ] = zp
            for u in range(nj):
                zj = zp[:, u * TQ:(u + 1) * TQ]
                zmax = zj if zmax is None else jnp.maximum(zmax, zj)
        mw = jnp.max(zmax, axis=-1, keepdims=True)
        pw = jnp.exp2(zw_sc[:, 0:n_wt * TQ] - mw).astype(BF16)
        acc_w = cat([_dot(pw[s * HR + g * GR:s * HR + (g + 1) * GR],
                          vw_refs[g][s, pl.ds(woff, n_wt * TQ), :])
                     for s in elems for g in range(N_KV_GROUPS)])

        for s in elems:
            for g in range(N_KV_GROUPS):
                seln = seln_sc[s, jq, g]
                for r in range(N_REP):
                    h = g * N_REP + r
                    lhs_sc[s, h * TQ:(h + 1) * TQ, :] = jnp.concatenate([parts[s][h], seln], axis=1)

        zmax = None
        for pi in range(n_pairs):
            k0 = pi * 2 * TQ
            zp = (cat([_dot_nt(lhs_sc[s], jnp.concatenate([ks_ref[s, k0:k0 + 2 * TQ, :],
                                                           et_ref[k0:k0 + 2 * TQ, :]], axis=1))
                       for s in elems])
                  + bias_tiles(bt_ref, 2 * pi, 2, 0))
            z_sc[pi // 2, :, (pi % 2) * 2 * TQ:(pi % 2 + 1) * 2 * TQ] = zp
            for u in range(2):
                zt = zp[:, u * TQ:(u + 1) * TQ]
                zmax = zt if zmax is None else jnp.maximum(zmax, zt)
        ms = jnp.max(zmax, axis=-1, keepdims=True)
        acc = [None] * (nb * N_KV_GROUPS)
        for c in range((n_pairs + 1) // 2):
            width = min(CH, n_pairs * 2 * TQ - c * CH)
            p = jnp.exp2(z_sc[c, :, 0:width] - ms).astype(BF16)
            for s in elems:
                for g in range(N_KV_GROUPS):
                    i = s * N_KV_GROUPS + g
                    d = _dot(p[i * GR:(i + 1) * GR], vs_refs[g][s, c * CH:c * CH + width, :])
                    acc[i] = d if acc[i] is None else acc[i] + d
        acc_s = cat(acc)

        for s in elems:
            sg = jax.nn.sigmoid(small4_ref[s, qrows, :])
            a_s = acc_s[s * HR:(s + 1) * HR]
            a_w = acc_w[s * HR:(s + 1) * HR]
            for r in range(N_REP):
                out = (partc_sc[s, jq, r] + gate(sg, r, 1) * normed(a_s, r)
                       + gate(sg, r, 2) * normed(a_w, r))
                yn_ref[s, qrows, r * LANES:(r + 1) * LANES] = out.astype(yn_ref.dtype)

    for nc in range(1, n_chunks_max + 1):
        @pl.when(pl.program_id(1) == nc - 1)
        def _(nc=nc):
            q0 = G4 * (nc - 1)
            for s in range(nb):
                group_prep(s, nc, q0)
            if unroll_tiles:
                for jq in range(G4):
                    tiles((q0 + jq) // 2 + 1, q0 + jq, jq)
            else:
                def body(jq, carry):
                    tiles(2 * nc, q0 + jq, jq)
                    return carry
                lax.fori_loop(0, G4, body, 0)


def _nsa(nq, nsw, ckv, small, bt, wb, cb, ov, emat):
    B, T, _ = nq.shape
    assert T == 2048, "single 128-wide compressed-key tile assumes T == 2048"
    nqt = T // TQ
    n_cmp = (T - CMP_BLOCK) // CMP_STRIDE + 1
    n_slc = T // SEL_BLOCK
    n_top = min(SEL_TOPK, n_slc)
    n_win = wb.shape[0] - 1
    HR = N_HEADS * TQ
    assert nqt % 4 == 0 and n_slc % 8 == 0 and n_slc <= LANES
    nb = NSA_BATCH if B % NSA_BATCH == 0 else 1
    n_chunks_max = nqt // 4
    kv = lambda j: pl.BlockSpec((nb, T, LANES), lambda b, g: (b, 0, j))
    ck = lambda j: pl.BlockSpec((1, nb, T // CMP_STRIDE, LANES), lambda b, g: (j, b, 0, 0))
    kern = functools.partial(_nsa_kernel, nb=nb, n_slc=n_slc, n_top=n_top, n_win=n_win,
                             n_chunks_max=n_chunks_max, unroll_tiles=NSA_UNROLL_TILES)
    return pl.pallas_call(
        kern, out_shape=jax.ShapeDtypeStruct((B, T, N_WIDTH), BF16),
        grid=(B // nb, n_chunks_max),
        in_specs=[pl.BlockSpec((nb, 4 * TQ, N_WIDTH), lambda b, g: (b, g, 0)),
                  kv(0), kv(1), kv(2), kv(3), kv(4), kv(5), ck(0), ck(1),
                  pl.BlockSpec((nb, 4 * TQ, LANES), lambda b, g: (b, g, 0)),
                  _const_spec(bt.shape), _const_spec(wb.shape),
                  pl.BlockSpec((4, HR, LANES), lambda b, g: (g, 0, 0)),
                  _const_spec(ov.shape), _const_spec(emat.shape)],
        out_specs=pl.BlockSpec((nb, 4 * TQ, N_WIDTH), lambda b, g: (b, g, 0)),
        scratch_shapes=[pltpu.VMEM((nb, HR, 2 * LANES), BF16),
                        pltpu.VMEM((n_chunks_max, nb * HR, 4 * TQ), F32),
                        pltpu.VMEM((nb * HR, n_win * TQ), F32),
                        pltpu.VMEM((nb, 4, N_KV_GROUPS, TQ, LANES), BF16),
                        pltpu.VMEM((nb, 4, N_REP, TQ, LANES), F32)],
        name="nsa",
        compiler_params=pltpu.CompilerParams(dimension_semantics=("arbitrary", "arbitrary"),
                                             vmem_limit_bytes=VMEM_LIMIT),
    )(nq, nsw, nsw, nsw, nsw, nsw, nsw, ckv, ckv, small, bt, wb, cb, ov, emat)


def _merge_kernel(x_ref, ym_ref, yn_ref, mg_ref, wbm_ref, wbn_ref, wo_ref,
                  wg_ref, wu_ref, wd_ref, gfin_ref, o_ref, *, tf):
    bm = _dot(ym_ref[...], wbm_ref[...])
    bn = _dot(yn_ref[...], wbn_ref[...])
    mixed = (jax.nn.sigmoid(mg_ref[:, :D_MODEL].astype(F32)) * bm
             + jax.nn.sigmoid(mg_ref[:, D_MODEL:].astype(F32)) * bn)
    h = x_ref[...] + _dot(mixed.astype(BF16), wo_ref[...])
    rs = lax.rsqrt(jnp.mean(h * h, axis=-1, keepdims=True) + RMS_EPS)
    hb = h.astype(BF16)
    acc = jnp.zeros(h.shape, F32)
    for j in range(D_FF // tf):
        gg = _dot(hb, wg_ref[:, j * tf:(j + 1) * tf]) * rs
        uu = _dot(hb, wu_ref[:, j * tf:(j + 1) * tf]) * rs
        act = (gg * jax.nn.sigmoid(gg) * uu).astype(BF16)
        acc = acc + _dot(act, wd_ref[j * tf:(j + 1) * tf, :])
    h2 = h + acc
    o_ref[...] = h2 * lax.rsqrt(jnp.mean(h2 * h2, axis=-1, keepdims=True) + RMS_EPS) * gfin_ref[...]


def _merge(x2, ym, yn, mg, wbm, wbn, wo, wg, wu, wd, gfin, tm=512, tf=256):
    n = x2.shape[0]
    assert n % tm == 0 and D_FF % tf == 0
    row = lambda width: pl.BlockSpec((tm, width), lambda i: (i, 0))
    return pl.pallas_call(
        functools.partial(_merge_kernel, tf=tf),
        out_shape=jax.ShapeDtypeStruct((n, D_MODEL), F32), grid=(n // tm,),
        in_specs=[row(D_MODEL), row(M_WIDTH), row(N_WIDTH), row(N_BRANCH * D_MODEL),
                  _const_spec(wbm.shape), _const_spec(wbn.shape), _const_spec(wo.shape),
                  _const_spec(wg.shape), _const_spec(wu.shape),
                  _const_spec(wd.shape), _const_spec(gfin.shape)],
        out_specs=row(D_MODEL), name="merge_ffn",
        compiler_params=pltpu.CompilerParams(dimension_semantics=("arbitrary",),
                                             vmem_limit_bytes=VMEM_LIMIT),
    )(x2, ym, yn, mg, wbm, wbn, wo, wg, wu, wd, gfin)


def _nsa_constants(T):
    n_cmp = (T - CMP_BLOCK) // CMP_STRIDE + 1
    n_slc = T // SEL_BLOCK
    cs = np.arange(n_cmp) * CMP_STRIDE
    ss = np.arange(n_slc) * SEL_BLOCK
    ov = np.clip(np.minimum(cs[:, None] + CMP_BLOCK, ss[None, :] + SEL_BLOCK)
                 - np.maximum(cs[:, None], ss[None, :]), 0, None) / CMP_STRIDE
    ovt = np.zeros((n_slc, LANES), np.float32)
    ovt[:, :n_cmp] = ov.T
    et = (np.arange(T)[:, None] // SEL_BLOCK == np.arange(LANES)[None, :]).astype(np.float32) * (-NEG)
    return jnp.asarray(ovt), jnp.asarray(et, dtype=BF16)


def _compress_weights(pe_cmp, w_cmp1, w_cmp2):
    half = CMP_BLOCK // 2
    assert N_KV_GROUPS == 2

    def blockdiag(w):
        z = jnp.zeros_like(w)
        return jnp.concatenate([jnp.concatenate([w, z], axis=-1), jnp.concatenate([z, w], axis=-1)], axis=-2)

    w1bd = blockdiag(w_cmp1.reshape(2, CMP_BLOCK, N_HEAD_DIM, CMP_HIDDEN).astype(BF16))
    w2bd = blockdiag(w_cmp2)
    pe2 = jnp.tile(pe_cmp, (1, 1, N_KV_GROUPS))
    return pe2, w1bd[:, :half].astype(BF16), w1bd[:, half:].astype(BF16), w2bd.astype(BF16)


def kernel(x, g_norm_mix, w_in, b_in, b_fgate, conv_qk, g_mlstm_head, pe_cmp, w_cmp1, w_cmp2,
           rel_bias, w_branch, w_out, g_norm_ffn, w_gate, w_up, w_down, g_final):
    B, T, D = x.shape
    assert D == D_MODEL and w_in.shape[0] == 1, "one residual block (DEPTH == 1)"
    N = B * T
    x2 = x.reshape(N, D)

    idx, scale = _inproj_perm()
    w_r = _wprep_t(w_in[0].T, g_norm_mix[0].reshape(1, D), idx, scale)
    b_r = _gather_cols(b_in[0].reshape(1, -1), idx, scale, F32)
    cw = jnp.zeros((8, 2 * M_WIDTH), F32).at[:CONV_WIDTH].set(conv_qk[0])
    mqk, mv, mo, nq, cin, nsw, mg, small = _inproj(x2, w_r, b_r)

    bf8 = jnp.concatenate([jnp.zeros((M_HEADS,), F32), b_fgate[0].astype(F32)])
    bfc = jnp.broadcast_to(jnp.tile(bf8, T // M_CHUNK)[:, None], (8 * (T // M_CHUNK), M_CHUNK))
    ltri = jnp.asarray(np.tril(np.ones((M_CHUNK, M_CHUNK), np.float32)))
    r3 = lambda a: a.reshape(B, T, a.shape[-1])
    ym = _mlstm(r3(mqk), r3(mv), r3(mo), r3(small), cw, bfc, g_mlstm_head[0].reshape(1, M_WIDTH), ltri)

    pe2, w1t, w1b, w2 = _compress_weights(pe_cmp[0], w_cmp1[0], w_cmp2[0])
    ckv = _compress(cin.reshape(2, B, T, LANES), pe2, w1t, w1b, w2)
    tbl = rel_bias.astype(F32).T.reshape(-1)
    nqt = T // TQ
    bt = _bias_tiles(tbl, nqt + 1, "tok")
    wb = _bias_tiles(tbl, min(WINDOW // TQ + 1, nqt) + 1, "win")
    cb = _bias_tiles(tbl, nqt, "cmp", n_cmp=(T - CMP_BLOCK) // CMP_STRIDE + 1)
    ov, emat = _nsa_constants(T)
    yn = _nsa(r3(nq), r3(nsw), ckv, r3(small), bt, wb, cb, ov, emat)

    wbm = w_branch[0, 0].astype(BF16)
    wbn = jnp.concatenate(
        [w_branch[0, 1, (g * N_REP + r) * N_HEAD_DIM:(g * N_REP + r + 1) * N_HEAD_DIM]
         for r in range(N_REP) for g in range(N_KV_GROUPS)], axis=0).astype(BF16)
    ident = lambda n: (np.arange(n, dtype=np.int32), np.ones((n,), np.float32))
    g_ffn = g_norm_ffn[0].reshape(D, 1)
    out = _merge(x2, ym.reshape(N, M_WIDTH), yn.reshape(N, N_WIDTH), mg, wbm, wbn,
                 w_out[0].astype(BF16), _wprep(w_gate[0], g_ffn, *ident(D_FF)),
                 _wprep(w_up[0], g_ffn, *ident(D_FF)),
                 _wprep(w_down[0], jnp.ones((D_FF, 1), F32), *ident(D)),
                 g_final.reshape(1, D))
    return out.reshape(B, T, D)
```

```python
import functools

import numpy as np
import jax
import jax.numpy as jnp
from jax import lax
from jax.experimental import pallas as pl
from jax.experimental.pallas import tpu as pltpu

F32 = jnp.float32
BF16 = jnp.bfloat16
HI = lax.Precision.HIGHEST

D_MODEL = 1024
M_HEADS = 4
M_HEAD_DIM = 128
M_WIDTH = M_HEADS * M_HEAD_DIM
M_CHUNK = 128
CONV_WIDTH = 4
N_HEADS = 8
N_KV_GROUPS = 2
N_REP = N_HEADS // N_KV_GROUPS
N_HEAD_DIM = 64
N_WIDTH = N_HEADS * N_HEAD_DIM
N_KV_WIDTH = N_KV_GROUPS * N_HEAD_DIM
CMP_BLOCK = 32
CMP_STRIDE = 16
CMP_HIDDEN = 2 * N_HEAD_DIM
SEL_BLOCK = 64
SEL_TOPK = 16
WINDOW = 512
REL_BUCKETS = 32
REL_MAX_DIST = 1024
N_BRANCH = 2
D_FF = 2816
RMS_EPS = 1e-6
BIG = 1e9
NEG = -1e30

LANES = 128
TQ = 128
NSA_BATCH = 1
VMEM_LIMIT = 56 * 1024 * 1024

_OFF_MQ, _OFF_MK, _OFF_MV, _OFF_MO = 0, 512, 1024, 1536
_OFF_MI, _OFF_MF, _OFF_NQ, _OFF_NKV = 2048, 2052, 2056, 2568
_OFF_NGATE, _OFF_MERGE, _D_IN = 3336, 3360, 5408
_SMALL_NGATE = 8

_SEGS = (("mqk", 1024), ("mv", 512), ("mo", 512), ("nq", 512), ("kc", 128),
         ("vc", 128), ("nsw", 512), ("mg", 2048), ("small", 128))
_D_IN_PAD = sum(w for _, w in _SEGS)
LOG2E = 1.4426950408889634


def _dot(a, b, **kw):
    return jnp.dot(a, b, preferred_element_type=F32, **kw)


def _dot_nt(a, b):
    return lax.dot_general(a, b, (((1,), (1,)), ((), ())), preferred_element_type=F32)


def _dot_tn(a, b):
    return lax.dot_general(a, b, (((0,), (0,)), ((), ())), preferred_element_type=F32)


def _const_spec(shape):
    nd = len(shape)
    return pl.BlockSpec(shape, lambda *_: (0,) * nd, pipeline_mode=pl.Buffered(1))


def _bucket_thresholds():
    max_exact = REL_BUCKETS // 2
    assert REL_MAX_DIST == 64 * max_exact and REL_BUCKETS - max_exact == 16
    thr = []
    for k in range(1, REL_BUCKETS - max_exact):
        t = max_exact
        while t ** 8 < (max_exact ** 8) * (2 ** (3 * k)):
            t += 1
        thr.append(t)
    return tuple(thr)


_BUCKET_THR = _bucket_thresholds()


def _inproj_perm():
    idx = np.zeros((_D_IN_PAD,), np.int32)
    scale = np.zeros((_D_IN_PAD,), np.float32)
    pos = 0

    def put(cols, s=1.0):
        nonlocal pos
        n = len(cols)
        idx[pos:pos + n] = cols
        scale[pos:pos + n] = s
        pos += n

    put(np.arange(_OFF_MQ, _OFF_MV))
    put(np.arange(_OFF_MV, _OFF_MO))
    put(np.arange(_OFF_MO, _OFF_MI))
    nq = np.zeros((N_WIDTH,), np.int32)
    for r in range(N_REP):
        for g in range(N_KV_GROUPS):
            for d in range(N_HEAD_DIM):
                nq[r * 128 + g * 64 + d] = _OFF_NQ + (g * N_REP + r) * N_HEAD_DIM + d
    put(nq, N_HEAD_DIM ** -0.5 * LOG2E)
    kv = lambda j, g: np.arange(_OFF_NKV + (j * N_KV_GROUPS + g) * N_HEAD_DIM,
                                _OFF_NKV + (j * N_KV_GROUPS + g + 1) * N_HEAD_DIM)
    put(np.arange(_OFF_NKV, _OFF_NKV + 256))
    put(np.concatenate([kv(2, 0), kv(2, 1)]))
    put(np.concatenate([kv(4, 0), kv(4, 1)]))
    put(np.concatenate([kv(3, 0), kv(3, 1)]))
    put(np.concatenate([kv(5, 0), kv(5, 1)]))
    put(np.arange(_OFF_MERGE, _D_IN))
    put(np.arange(_OFF_MI, _OFF_MI + 8))
    put(np.arange(_OFF_NGATE, _OFF_NGATE + 24))
    pos += LANES - 32
    assert pos == _D_IN_PAD
    return idx, scale


def _gather_cols(a, idx, scale, dtype):
    pieces = []
    start = 0
    n = len(idx)
    for c in range(1, n + 1):
        same = c < n and scale[c] == scale[start] and (scale[c] == 0.0 or idx[c] == idx[c - 1] + 1)
        if not same:
            if scale[start] == 0.0:
                piece = jnp.zeros((a.shape[0], c - start), dtype)
            else:
                piece = a[:, int(idx[start]):int(idx[start]) + (c - start)]
                if scale[start] != 1.0:
                    piece = piece * float(scale[start])
            pieces.append(piece.astype(dtype))
            start = c
    return jnp.concatenate(pieces, axis=1)


def _col_runs(idx, scale):
    runs = []
    start = 0
    n = len(idx)
    for c in range(1, n + 1):
        same = c < n and scale[c] == scale[start] and (scale[c] == 0.0 or idx[c] == idx[c - 1] + 1)
        if not same:
            runs.append((start, int(idx[start]), c - start, float(scale[start])))
            start = c
    return runs


def _wprep_kernel(w_ref, g_ref, o_ref, *, runs):
    g = g_ref[...]
    for dst, src, n, scale in runs:
        if scale == 0.0:
            o_ref[:, dst:dst + n] = jnp.zeros((o_ref.shape[0], n), o_ref.dtype)
        else:
            o_ref[:, dst:dst + n] = (w_ref[:, src:src + n] * (g * scale)).astype(o_ref.dtype)


def _wprep_t_kernel(wt_ref, g_ref, o_ref, *, runs):
    g = g_ref[...]
    for dst, src, n, scale in runs:
        if scale == 0.0:
            o_ref[dst:dst + n, :] = jnp.zeros((n, o_ref.shape[1]), o_ref.dtype)
        else:
            o_ref[dst:dst + n, :] = (wt_ref[src:src + n, :] * (g * scale)).astype(o_ref.dtype)


def _wprep_t(wt, g, idx, scale, tc=256):
    rows, cols = wt.shape
    assert cols % tc == 0
    return pl.pallas_call(
        functools.partial(_wprep_t_kernel, runs=_col_runs(idx, scale)),
        out_shape=jax.ShapeDtypeStruct((len(idx), cols), BF16), grid=(cols // tc,),
        in_specs=[pl.BlockSpec((rows, tc), lambda i: (0, i)), pl.BlockSpec((1, tc), lambda i: (0, i))],
        out_specs=pl.BlockSpec((len(idx), tc), lambda i: (0, i)), name="wprep_t",
        compiler_params=pltpu.CompilerParams(dimension_semantics=("arbitrary",),
                                             vmem_limit_bytes=VMEM_LIMIT),
    )(wt, g)


def _wprep(w, g, idx, scale, tr=256):
    rows, cols = w.shape
    assert rows % tr == 0
    return pl.pallas_call(
        functools.partial(_wprep_kernel, runs=_col_runs(idx, scale)),
        out_shape=jax.ShapeDtypeStruct((rows, len(idx)), BF16), grid=(rows // tr,),
        in_specs=[pl.BlockSpec((tr, cols), lambda i: (i, 0)), pl.BlockSpec((tr, 1), lambda i: (i, 0))],
        out_specs=pl.BlockSpec((tr, len(idx)), lambda i: (i, 0)), name="wprep",
        compiler_params=pltpu.CompilerParams(dimension_semantics=("arbitrary",),
                                             vmem_limit_bytes=VMEM_LIMIT),
    )(w, g)


def _inproj_kernel(x_ref, w_ref, b_ref, mqk_ref, mv_ref, mo_ref, nq_ref,
                   cin_ref, nsw_ref, mg_ref, small_ref):
    x = x_ref[...]
    rs = lax.rsqrt(jnp.mean(x * x, axis=-1, keepdims=True) + RMS_EPS)
    xb = x.astype(BF16)

    def seg(a, n):
        return _dot_nt(xb, w_ref[a:a + n, :]) * rs + b_ref[:, a:a + n]

    off = {}
    pos = 0
    for name, w in _SEGS:
        off[name] = pos
        pos += w

    mqk_ref[...] = seg(off["mqk"], 1024).astype(BF16)
    mv_ref[...] = seg(off["mv"], 512).astype(BF16)
    mo_ref[...] = seg(off["mo"], 512).astype(BF16)
    nq_ref[...] = seg(off["nq"], 512).astype(BF16)
    cc = seg(off["kc"], 2 * LANES)
    cin_ref[0] = cc[:, :LANES]
    cin_ref[1] = cc[:, LANES:]
    nsw_ref[:, 0:2 * LANES] = seg(off["nsw"], 2 * LANES).astype(BF16)
    vv = seg(off["nsw"] + 2 * LANES, 2 * LANES)
    lo = lax.broadcasted_iota(jnp.int32, (x.shape[0], LANES), 1) < N_HEAD_DIM
    for j in range(2):
        vj = vv[:, j * LANES:(j + 1) * LANES]
        nsw_ref[:, (2 + 2 * j) * LANES:(3 + 2 * j) * LANES] = jnp.where(lo, vj, 1.0).astype(BF16)
        nsw_ref[:, (3 + 2 * j) * LANES:(4 + 2 * j) * LANES] = jnp.where(lo, 1.0, vj).astype(BF16)
    mg_ref[...] = seg(off["mg"], 2048).astype(BF16)
    small_ref[...] = seg(off["small"], 128)


def _inproj(x2, w, b, tm=1024):
    n = x2.shape[0]
    assert n % tm == 0
    row = lambda width: pl.BlockSpec((tm, width), lambda i: (i, 0))
    out_shape = (
        jax.ShapeDtypeStruct((n, 1024), BF16), jax.ShapeDtypeStruct((n, 512), BF16),
        jax.ShapeDtypeStruct((n, 512), BF16), jax.ShapeDtypeStruct((n, 512), BF16),
        jax.ShapeDtypeStruct((2, n, 128), F32), jax.ShapeDtypeStruct((n, 768), BF16),
        jax.ShapeDtypeStruct((n, 2048), BF16), jax.ShapeDtypeStruct((n, 128), F32))
    out_specs = (row(1024), row(512), row(512), row(512),
                 pl.BlockSpec((2, tm, 128), lambda i: (0, i, 0)), row(768), row(2048), row(128))
    return pl.pallas_call(
        _inproj_kernel, out_shape=out_shape, grid=(n // tm,),
        in_specs=[row(D_MODEL), _const_spec((_D_IN_PAD, D_MODEL)), _const_spec((1, _D_IN_PAD))],
        out_specs=out_specs, name="inproj",
        compiler_params=pltpu.CompilerParams(dimension_semantics=("arbitrary",),
                                             vmem_limit_bytes=VMEM_LIMIT),
    )(x2, w, b)


def _mlstm_kernel(mqk_ref, mv_ref, mo_ref, small_ref, cw_ref, bfc_ref, gh_ref, ltri_ref,
                  ym_ref, ebuf, c_st, m_st, gx, g_bc, g_d, g_a, g_mrow, g_blast, g_amax):
    L = M_CHUNK
    H = M_HEADS
    c = pl.program_id(1)
    n_chunks = gx.shape[0] // 8
    n_sub = mqk_ref.shape[1] // L
    lanes =lax.broadcasted_iota(jnp.int32, gx.shape, 1)

    @pl.when(c == 0)
    def _():
        ebuf[0:8, :] = jnp.zeros((8, 2 * M_WIDTH), F32)
        c_st[...] = jnp.zeros_like(c_st)
        m_st[...] = jnp.zeros_like(m_st)
        for cc in range(n_chunks):
            gx[cc * 8:(cc + 1) * 8, :] = small_ref[0, cc * L:(cc + 1) * L, :].T[0:8, :]
        x = gx[...]
        fg = x + bfc_ref[...]
        lf = jnp.minimum(fg, 0.0) - jnp.log1p(jnp.exp(-jnp.abs(fg)))
        bc = lax.dot_general(lf, ltri_ref[...], (((1,), (1,)), ((), ())),
                             precision=HI, preferred_element_type=F32)
        ig = pltpu.roll(x, H, 0)
        blast = bc[:, L - 1:L]
        a_all = blast - bc + ig
        d = ig - bc
        cm = d
        sh = 1
        while sh < L:
            cm = jnp.maximum(cm, jnp.where(lanes >= sh, pltpu.roll(cm, sh, 1), -jnp.inf))
            sh *= 2
        g_bc[...] = bc
        g_d[...] = d
        g_a[...] = a_all
        g_mrow[...] = bc + cm
        g_blast[...] = jnp.broadcast_to(blast, gx.shape)
        g_amax[...] = jnp.broadcast_to(jnp.max(a_all, axis=-1, keepdims=True), gx.shape)

    row = lax.broadcasted_iota(jnp.int32, (L, L), 0)
    col = lax.broadcasted_iota(jnp.int32, (L, L), 1)
    causal = row >= col
    ones = jnp.ones((L, M_HEAD_DIM), BF16)

    def chunk(j):
        rows = slice(j * L, (j + 1) * L)
        ebuf[8:8 + L, :] = mqk_ref[0, rows, :].astype(F32)
        ext = ebuf[...]
        conv = cw_ref[CONV_WIDTH - 1:CONV_WIDTH, :] * ext[8:, :]
        for k in range(1, CONV_WIDTH):
            conv = conv + cw_ref[CONV_WIDTH - 1 - k:CONV_WIDTH - k, :] * pltpu.roll(ext, k, 0)[8:, :]
        ebuf[0:8, :] = ebuf[L:L + 8, :]
        qk = (conv * jax.nn.sigmoid(conv)).astype(BF16)

        r8 = pl.ds(pl.multiple_of((c * n_sub + j) * 8, 8), 8)
        bc = g_bc[r8, :]
        dt = g_d[r8, :]
        blast = g_blast[r8, :]
        m_prev = m_st[...]
        m_new = jnp.maximum(blast + m_prev, g_amax[r8, :])
        decay = jnp.exp(blast + m_prev - m_new)
        w_row = jnp.exp(g_a[r8, :] - m_new)
        log_inter = bc + m_prev
        m_i = jnp.maximum(g_mrow[r8, :], log_inter)
        k_scale = M_HEAD_DIM ** -0.5
        s_inter = jnp.exp(log_inter - m_i) * k_scale
        emi = jnp.exp(-m_i)
        u = bc - m_i + float(np.log(k_scale))
        cols = jnp.concatenate([u, s_inter, emi, w_row, jnp.zeros((LANES - 32, L), F32)], axis=0).T
        u_all, s_inter_all = cols[:, 0:8], cols[:, 8:16]
        emi_all, w_all = cols[:, 16:24], cols[:, 24:32]

        for h in range(M_HEADS):
            sl = slice(h * M_HEAD_DIM, (h + 1) * M_HEAD_DIM)
            qb = qk[:, sl]
            kb = qk[:, M_WIDTH + h * M_HEAD_DIM:M_WIDTH + (h + 1) * M_HEAD_DIM]
            v = mv_ref[0, rows, sl]
            cn_prev = c_st[h]

            g = H + h
            logp = jnp.where(causal, u_all[:, g:g + 1] + dt[g:g + 1, :], -jnp.inf)
            p = jnp.exp(logp) * _dot_nt(qb, kb)
            nd = (_dot(p.astype(BF16), jnp.concatenate([v, ones], axis=1))
                  + s_inter_all[:, g:g + 1] * _dot(qb, cn_prev.astype(BF16)))
            hh = nd[:, :M_HEAD_DIM] / jnp.maximum(jnp.abs(nd[:, M_HEAD_DIM:]), emi_all[:, g:g + 1])
            hh = hh * lax.rsqrt(jnp.mean(hh * hh, axis=-1, keepdims=True) + RMS_EPS) * gh_ref[:, sl]
            ym_ref[0, rows, sl] = (jax.nn.sigmoid(mo_ref[0, rows, sl].astype(F32)) * hh).astype(ym_ref.dtype)

            w_col = w_all[:, g:g + 1]
            vw = jnp.concatenate([v.astype(F32) * w_col, jnp.broadcast_to(w_col, (L, M_HEAD_DIM))], axis=1)
            c_st[h] = decay[g:g + 1, 0:1] * cn_prev + _dot_tn(kb, vw.astype(BF16))

        m_st[...] = m_new

    for j in range(n_sub):
        chunk(j)


def _mlstm(mqk, mv, mo, small, cw, bf, gh, ltri, n_sub=4):
    B, T, _ = mqk.shape
    L = M_CHUNK
    assert T % (n_sub * L) == 0
    nc = T // L
    assert L == LANES and 8 * nc <= LANES
    blk = lambda w: pl.BlockSpec((1, n_sub * L, w), lambda b, c: (b, c, 0))
    gate = lambda: pltpu.VMEM((8 * nc, L), F32)
    return pl.pallas_call(
        _mlstm_kernel, out_shape=jax.ShapeDtypeStruct((B, T, M_WIDTH), BF16),
        grid=(B, nc // n_sub),
        in_specs=[blk(2 * M_WIDTH), blk(M_WIDTH), blk(M_WIDTH),
                  pl.BlockSpec((1, T, LANES), lambda b, c: (b, 0, 0)),
                  _const_spec((8, 2 * M_WIDTH)), _const_spec((8 * nc, L)),
                  _const_spec((1, M_WIDTH)), _const_spec((L, L))],
        out_specs=blk(M_WIDTH),
        scratch_shapes=[pltpu.VMEM((L + 8, 2 * M_WIDTH), F32),
                        pltpu.VMEM((M_HEADS, M_HEAD_DIM, 2 * M_HEAD_DIM), F32),
                        pltpu.VMEM((8, L), F32),
                        gate(), gate(), gate(), gate(), gate(), gate(), gate()],
        name="mlstm",
        compiler_params=pltpu.CompilerParams(dimension_semantics=("arbitrary", "arbitrary")),
    )(mqk, mv, mo, small, cw, bf, gh, ltri)


def _compress_kernel(x_ref, pe_ref, w1t_ref, w1b_ref, w2_ref, o_ref, *, nch, nbt):
    half = CMP_BLOCK // 2
    rows = nbt * nch
    top = jnp.zeros((rows, 2 * CMP_HIDDEN), F32)
    bot = jnp.zeros((rows, 2 * CMP_HIDDEN), F32)
    for p in range(half):
        xp = jnp.concatenate([x_ref[0, e, pl.ds(p, nch, stride=CMP_STRIDE), :] for e in range(nbt)], axis=0)
        top = top + _dot((xp + pe_ref[0, p:p + 1, :]).astype(BF16), w1t_ref[0, p])
        bot = bot + _dot((xp + pe_ref[0, half + p:half + p + 1, :]).astype(BF16), w1b_ref[0, p])
    hid = top + pltpu.roll(bot, rows - 1, 0)
    act = hid * jax.nn.sigmoid(hid)
    out = _dot(act.astype(BF16), w2_ref[0]).astype(o_ref.dtype)
    for e in range(nbt):
        o_ref[0, e] = out[e * nch:(e + 1) * nch]


def _compress(cin, pe2, w1t, w1b, w2):
    _, B, T, _ = cin.shape
    nch = T // CMP_STRIDE
    nbt = 4 if B % 4 == 0 else 1
    sel = lambda *shape: pl.BlockSpec((1,) + shape, lambda j, b: (j,) + (0,) * len(shape))
    return pl.pallas_call(
        functools.partial(_compress_kernel, nch=nch, nbt=nbt),
        out_shape=jax.ShapeDtypeStruct((2, B, nch, LANES), BF16),
        grid=(2, B // nbt),
        in_specs=[pl.BlockSpec((1, nbt, T, LANES), lambda j, b: (j, b, 0, 0)),
                  sel(CMP_BLOCK, LANES), sel(CMP_BLOCK // 2, LANES, 2 * CMP_HIDDEN),
                  sel(CMP_BLOCK // 2, LANES, 2 * CMP_HIDDEN), sel(2 * CMP_HIDDEN, LANES)],
        out_specs=pl.BlockSpec((1, nbt, nch, LANES), lambda j, b: (j, b, 0, 0)),
        name="compress",
        compiler_params=pltpu.CompilerParams(dimension_semantics=("arbitrary", "arbitrary")),
    )(cin, pe2, w1t, w1b, w2)


def _bias_kernel(tbl_ref, o_ref, *, kind, n_cmp):
    pid = pl.program_id(0)
    toeplitz = kind != "cmp"
    if toeplitz:
        k = lax.broadcasted_iota(jnp.int32, (8, 2 * LANES), 1)
        dist = (pid - 1) * TQ + jnp.where(k < LANES, -k, 2 * LANES - k)
    else:
        al = lax.broadcasted_iota(jnp.int32, (CMP_STRIDE, 2 * LANES), 0)
        m = lax.broadcasted_iota(jnp.int32, (CMP_STRIDE, 2 * LANES), 1)
        dist = pid * TQ - (CMP_BLOCK - 1) + al - CMP_STRIDE * jnp.where(m < LANES, m, m - 2 * LANES)
    n = jnp.maximum(dist, 0)
    cnt = jnp.zeros_like(n)
    for t in _BUCKET_THR:
        cnt = cnt + jnp.where(n >= t, 1, 0)
    bucket = jnp.where(n < REL_BUCKETS // 2, n, REL_BUCKETS // 2 + cnt)
    if kind == "tok":
        madd = jnp.where((dist >= 0) & (pid > 0), 0.0, NEG)
    elif kind == "win":
        madd = jnp.where((dist >= 0) & (dist < WINDOW) & (pid > 0), 0.0, NEG)
    else:
        madd = jnp.where(dist >= 0, 0.0, NEG)
        valid_c = lax.broadcasted_iota(jnp.int32, (TQ, LANES), 1) < n_cmp
    for h in range(N_HEADS):
        val = jnp.zeros(dist.shape, F32)
        for bb in range(REL_BUCKETS):
            val = jnp.where(bucket == bb, tbl_ref[h * REL_BUCKETS + bb], val)
        val = val * LOG2E + madd
        if toeplitz:
            g = jnp.broadcast_to(val[0:1, :], (TQ, 2 * LANES))
            val = pltpu.roll(g, 0, 1, stride=1, stride_axis=0)[:, :LANES]
        else:
            bands = [val[:, :LANES]] + [pltpu.roll(val, ah, 1)[:, :LANES] for ah in range(1, TQ // CMP_STRIDE)]
            val = jnp.where(valid_c, jnp.concatenate(bands, axis=0), NEG)
        o_ref[0, h * TQ:(h + 1) * TQ, :] = val


def _bias_tiles(tbl, n_tiles, kind, n_cmp=0):
    return pl.pallas_call(
        functools.partial(_bias_kernel, kind=kind, n_cmp=n_cmp),
        out_shape=jax.ShapeDtypeStruct((n_tiles, N_HEADS * TQ, LANES), F32),
        grid=(n_tiles,),
        in_specs=[pl.BlockSpec(memory_space=pltpu.SMEM)],
        out_specs=pl.BlockSpec((1, N_HEADS * TQ, LANES), lambda i: (i, 0, 0)),
        name="bias_" + kind,
        compiler_params=pltpu.CompilerParams(dimension_semantics=("arbitrary",)),
    )(tbl)


def _nsa_kernel(nq4_ref, ks_ref, kw_ref, vs0_ref, vs1_ref, vw0_ref, vw1_ref, kc_ref, vc_ref,
                small4_ref, bt_ref, wb_ref, cb4_ref, ovt_ref, et_ref, yn_ref,
                lhs_sc, z_sc, zw_sc, seln_sc, partc_sc, *, nb, n_slc, n_top, n_win, n_chunks_max):
    HR = N_HEADS * TQ
    GR = N_REP * TQ
    CH = 4 * TQ
    G4 = 4
    vs_refs = (vs0_ref, vs1_ref)
    vw_refs = (vw0_ref, vw1_ref)
    lane = lax.broadcasted_iota(jnp.int32, (TQ, LANES), 1)
    lo = lane < N_HEAD_DIM

    def stack_heads(q_all):
        zero = jnp.zeros((TQ, LANES), BF16)
        parts = []
        for g in range(N_KV_GROUPS):
            for r in range(N_REP):
                qr = q_all[:, r * LANES:(r + 1) * LANES]
                parts.append(jnp.where(lo if g == 0 else jnp.logical_not(lo), qr, zero))
        return parts

    def pair(o, r):
        return jnp.where(lo, o[r * TQ:(r + 1) * TQ], o[(N_REP + r) * TQ:(N_REP + r + 1) * TQ])

    def normed(acc, r):
        den = jnp.where(lo, acc[(N_REP + r) * TQ:(N_REP + r + 1) * TQ], acc[r * TQ:(r + 1) * TQ])
        return pair(acc, r) / pltpu.roll(den, N_HEAD_DIM, 1)

    def gate(sg, r, ci):
        c0 = _SMALL_NGATE + r * 3 + ci
        c1 = _SMALL_NGATE + (N_REP + r) * 3 + ci
        return jnp.where(lo, sg[:, c0:c0 + 1], sg[:, c1:c1 + 1])

    def group_prep(s, n_chunks, qi):
        qs4 = jnp.concatenate(
            [p for j in range(G4) for p in stack_heads(nq4_ref[s, j * TQ:(j + 1) * TQ, :])], axis=0)
        rows = lax.broadcasted_iota(jnp.int32, (G4 * HR, 1), 0)
        t_rows = (qi + jnp.right_shift(rows, 10)) * TQ + (rows & (TQ - 1))

        z = _dot_nt(qs4, kc_ref[0, s]) + cb4_ref[...].reshape(G4 * HR, LANES)
        e = jnp.exp2(z - jnp.max(z, axis=-1, keepdims=True))
        l = _dot(e.astype(BF16), jnp.ones((LANES, LANES), BF16))
        p_c = e * jnp.where(t_rows >= CMP_BLOCK - 1, 1.0 / jnp.maximum(l, 1e-30), 0.0)
        o_c = _dot(p_c.astype(BF16), vc_ref[0, s])
        sg4 = jax.nn.sigmoid(small4_ref[s])
        for j in range(G4):
            for r in range(N_REP):
                partc_sc[s, j, r] = (gate(sg4[j * TQ:(j + 1) * TQ], r, 0)
                                     * pair(o_c[j * HR:(j + 1) * HR], r))

        W = G4 * N_KV_GROUPS * TQ
        jb = lax.broadcasted_iota(jnp.int32, (n_slc, W), 0)
        col = lax.broadcasted_iota(jnp.int32, (n_slc, W), 1)
        tq = (qi + jnp.right_shift(col, 8)) * TQ + (col & (TQ - 1))
        cur = jnp.right_shift(tq, 6)
        elig = jb <= cur
        if n_chunks * CH <= n_top * SEL_BLOCK:
            sel = jnp.where(elig, 1.0, 0.0)
        else:
            forced = (jb == 0) | (jb == cur) | (jb == cur - 1)
            psums = []
            for j in range(G4):
                for g in range(N_KV_GROUPS):
                    base = j * HR + g * GR
                    ps = p_c[base:base + TQ]
                    for r in range(1, N_REP):
                        ps = ps + p_c[base + r * TQ:base + (r + 1) * TQ]
                    psums.append(ps)
            imp = lax.dot_general(ovt_ref[...], jnp.concatenate(psums, axis=0), (((1,), (1,)), ((), ())),
                                  precision=HI, preferred_element_type=F32)
            score = jnp.where(elig, jnp.where(forced, BIG, imp), -BIG)
            cnt = jnp.zeros((n_slc, W), F32)
            for i in range(n_slc):
                si = score[i:i + 1, :]
                tie = jnp.where(jb > i, 1.0, 0.0)
                cnt = cnt + jnp.where(si > score, 1.0, jnp.where(si == score, tie, 0.0))
            sel = jnp.where((cnt < n_top) & (score > -BIG / 2), 1.0, 0.0)
        seln = jnp.concatenate([sel - 1.0, jnp.zeros((LANES - n_slc, W), F32)], axis=0).T.astype(BF16)
        for j in range(G4):
            for g in range(N_KV_GROUPS):
                seln_sc[s, j, g] = seln[(j * N_KV_GROUPS + g) * TQ:(j * N_KV_GROUPS + g + 1) * TQ]

    def tiles(n_pairs, qi, jq):
        qrows = pl.ds(pl.multiple_of(jq * TQ, TQ), TQ)
        w0 = jnp.maximum(qi - (n_win - 1), 0)
        woff = pl.multiple_of(w0 * TQ, TQ)
        elems = range(nb)
        cat = lambda xs: xs[0] if len(xs) == 1 else jnp.concatenate(xs, axis=0)
        rep = lambda t: t if nb == 1 else jnp.concatenate([t] * nb, axis=0)
        parts = [stack_heads(nq4_ref[s, qrows, :]) for s in elems]
        qs = [jnp.concatenate(parts[s], axis=0) for s in elems]

        def bias_tiles(ref, first, count, tile0):
            tiles_ = [ref[jnp.maximum(qi - tile0 - (first + u) + 1, 0)] for u in range(count)]
            return rep(tiles_[0] if count == 1 else jnp.concatenate(tiles_, axis=1))

        zmax = None
        for j0 in range(0, n_win, 2):
            nj = min(2, n_win - j0)
            zp = (cat([_dot_nt(qs[s], kw_ref[s, pl.ds(woff + j0 * TQ, nj * TQ), :]) for s in elems])
                  + bias_tiles(wb_ref, j0, nj, w0))
            zw_sc[:, j0 * TQ:(j0 + nj) * TQ] = zp
            for u in range(nj):
                zj = zp[:, u * TQ:(u + 1) * TQ]
                zmax = zj if zmax is None else jnp.maximum(zmax, zj)
        mw = jnp.max(zmax, axis=-1, keepdims=True)
        pw = jnp.exp2(zw_sc[...] - mw).astype(BF16)
        acc_w = cat([_dot(pw[s * HR + g * GR:s * HR + (g + 1) * GR],
                          vw_refs[g][s, pl.ds(woff, n_win * TQ), :])
                     for s in elems for g in range(N_KV_GROUPS)])

        for s in elems:
            for g in range(N_KV_GROUPS):
                seln = seln_sc[s, jq, g]
                for r in range(N_REP):
                    h = g * N_REP + r
                    lhs_sc[s, h * TQ:(h + 1) * TQ, :] = jnp.concatenate([parts[s][h], seln], axis=1)

        zmax = None
        for pi in range(n_pairs):
            k0 = pi * 2 * TQ
            zp = (cat([_dot_nt(lhs_sc[s], jnp.concatenate([ks_ref[s, k0:k0 + 2 * TQ, :],
                                                           et_ref[k0:k0 + 2 * TQ, :]], axis=1))
                       for s in elems])
                  + bias_tiles(bt_ref, 2 * pi, 2, 0))
            z_sc[pi // 2, :, (pi % 2) * 2 * TQ:(pi % 2 + 1) * 2 * TQ] = zp
            for u in range(2):
                zt = zp[:, u * TQ:(u + 1) * TQ]
                zmax = zt if zmax is None else jnp.maximum(zmax, zt)
        ms = jnp.max(zmax, axis=-1, keepdims=True)
        acc = [None] * (nb * N_KV_GROUPS)
        for c in range((n_pairs + 1) // 2):
            width = min(CH, n_pairs * 2 * TQ - c * CH)
            p = jnp.exp2(z_sc[c, :, 0:width] - ms).astype(BF16)
            for s in elems:
                for g in range(N_KV_GROUPS):
                    i = s * N_KV_GROUPS + g
                    d = _dot(p[i * GR:(i + 1) * GR], vs_refs[g][s, c * CH:c * CH + width, :])
                    acc[i] = d if acc[i] is None else acc[i] + d
        acc_s = cat(acc)

        for s in elems:
            sg = jax.nn.sigmoid(small4_ref[s, qrows, :])
            a_s = acc_s[s * HR:(s + 1) * HR]
            a_w = acc_w[s * HR:(s + 1) * HR]
            for r in range(N_REP):
                out = (partc_sc[s, jq, r] + gate(sg, r, 1) * normed(a_s, r)
                       + gate(sg, r, 2) * normed(a_w, r))
                yn_ref[s, qrows, r * LANES:(r + 1) * LANES] = out.astype(yn_ref.dtype)

    for nc in range(1, n_chunks_max + 1):
        @pl.when(pl.program_id(1) == nc - 1)
        def _(nc=nc):
            q0 = G4 * (nc - 1)
            for s in range(nb):
                group_prep(s, nc, q0)
            for half in range(2):
                def body(jq, carry, n_pairs=2 * nc - 1 + half):
                    tiles(n_pairs, q0 + jq, jq)
                    return carry
                lax.fori_loop(half * G4 // 2, (half + 1) * G4 // 2, body, 0)


def _nsa(nq, nsw, ckv, small, bt, wb, cb, ov, emat):
    B, T, _ = nq.shape
    assert T == 2048, "single 128-wide compressed-key tile assumes T == 2048"
    nqt = T // TQ
    n_cmp = (T - CMP_BLOCK) // CMP_STRIDE + 1
    n_slc = T // SEL_BLOCK
    n_top = min(SEL_TOPK, n_slc)
    n_win = wb.shape[0] - 1
    HR = N_HEADS * TQ
    assert nqt % 4 == 0 and n_slc % 8 == 0 and n_slc <= LANES
    nb = NSA_BATCH if B % NSA_BATCH == 0 else 1
    n_chunks_max = nqt // 4
    kv = lambda j: pl.BlockSpec((nb, T, LANES), lambda b, g: (b, 0, j))
    ck = lambda j: pl.BlockSpec((1, nb, T // CMP_STRIDE, LANES), lambda b, g: (j, b, 0, 0))
    kern = functools.partial(_nsa_kernel, nb=nb, n_slc=n_slc, n_top=n_top, n_win=n_win,
                             n_chunks_max=n_chunks_max)
    return pl.pallas_call(
        kern, out_shape=jax.ShapeDtypeStruct((B, T, N_WIDTH), BF16),
        grid=(B // nb, n_chunks_max),
        in_specs=[pl.BlockSpec((nb, 4 * TQ, N_WIDTH), lambda b, g: (b, g, 0)),
                  kv(0), kv(1), kv(2), kv(3), kv(4), kv(5), ck(0), ck(1),
                  pl.BlockSpec((nb, 4 * TQ, LANES), lambda b, g: (b, g, 0)),
                  _const_spec(bt.shape), _const_spec(wb.shape),
                  pl.BlockSpec((4, HR, LANES), lambda b, g: (g, 0, 0)),
                  _const_spec(ov.shape), _const_spec(emat.shape)],
        out_specs=pl.BlockSpec((nb, 4 * TQ, N_WIDTH), lambda b, g: (b, g, 0)),
        scratch_shapes=[pltpu.VMEM((nb, HR, 2 * LANES), BF16),
                        pltpu.VMEM((n_chunks_max, nb * HR, 4 * TQ), F32),
                        pltpu.VMEM((nb * HR, n_win * TQ), F32),
                        pltpu.VMEM((nb, 4, N_KV_GROUPS, TQ, LANES), BF16),
                        pltpu.VMEM((nb, 4, N_REP, TQ, LANES), F32)],
        name="nsa",
        compiler_params=pltpu.CompilerParams(dimension_semantics=("arbitrary", "arbitrary"),
                                             vmem_limit_bytes=VMEM_LIMIT),
    )(nq, nsw, nsw, nsw, nsw, nsw, nsw, ckv, ckv, small, bt, wb, cb, ov, emat)


def _merge_kernel(x_ref, ym_ref, yn_ref, mg_ref, wbm_ref, wbn_ref, wo_ref,
                  wg_ref, wu_ref, wd_ref, gfin_ref, o_ref, *, tf):
    bm = _dot(ym_ref[...], wbm_ref[...])
    bn = _dot(yn_ref[...], wbn_ref[...])
    mixed = (jax.nn.sigmoid(mg_ref[:, :D_MODEL].astype(F32)) * bm
             + jax.nn.sigmoid(mg_ref[:, D_MODEL:].astype(F32)) * bn)
    h = x_ref[...] + _dot(mixed.astype(BF16), wo_ref[...])
    rs = lax.rsqrt(jnp.mean(h * h, axis=-1, keepdims=True) + RMS_EPS)
    hb = h.astype(BF16)
    acc = jnp.zeros(h.shape, F32)
    for j in range(D_FF // tf):
        gg = _dot(hb, wg_ref[:, j * tf:(j + 1) * tf]) * rs
        uu = _dot(hb, wu_ref[:, j * tf:(j + 1) * tf]) * rs
        act = (gg * jax.nn.sigmoid(gg) * uu).astype(BF16)
        acc = acc + _dot(act, wd_ref[j * tf:(j + 1) * tf, :])
    h2 = h + acc
    o_ref[...] = h2 * lax.rsqrt(jnp.mean(h2 * h2, axis=-1, keepdims=True) + RMS_EPS) * gfin_ref[...]


def _merge(x2, ym, yn, mg, wbm, wbn, wo, wg, wu, wd, gfin, tm=512, tf=256):
    n = x2.shape[0]
    assert n % tm == 0 and D_FF % tf == 0
    row = lambda width: pl.BlockSpec((tm, width), lambda i: (i, 0))
    return pl.pallas_call(
        functools.partial(_merge_kernel, tf=tf),
        out_shape=jax.ShapeDtypeStruct((n, D_MODEL), F32), grid=(n // tm,),
        in_specs=[row(D_MODEL), row(M_WIDTH), row(N_WIDTH), row(N_BRANCH * D_MODEL),
                  _const_spec(wbm.shape), _const_spec(wbn.shape), _const_spec(wo.shape),
                  _const_spec(wg.shape), _const_spec(wu.shape),
                  _const_spec(wd.shape), _const_spec(gfin.shape)],
        out_specs=row(D_MODEL), name="merge_ffn",
        compiler_params=pltpu.CompilerParams(dimension_semantics=("arbitrary",),
                                             vmem_limit_bytes=VMEM_LIMIT),
    )(x2, ym, yn, mg, wbm, wbn, wo, wg, wu, wd, gfin)


def _nsa_constants(T):
    n_cmp = (T - CMP_BLOCK) // CMP_STRIDE + 1
    n_slc = T // SEL_BLOCK
    cs = np.arange(n_cmp) * CMP_STRIDE
    ss = np.arange(n_slc) * SEL_BLOCK
    ov = np.clip(np.minimum(cs[:, None] + CMP_BLOCK, ss[None, :] + SEL_BLOCK)
                 - np.maximum(cs[:, None], ss[None, :]), 0, None) / CMP_STRIDE
    ovt = np.zeros((n_slc, LANES), np.float32)
    ovt[:, :n_cmp] = ov.T
    et = (np.arange(T)[:, None] // SEL_BLOCK == np.arange(LANES)[None, :]).astype(np.float32) * (-NEG)
    return jnp.asarray(ovt), jnp.asarray(et, dtype=BF16)


def _compress_weights(pe_cmp, w_cmp1, w_cmp2):
    half = CMP_BLOCK // 2
    assert N_KV_GROUPS == 2

    def blockdiag(w):
        z = jnp.zeros_like(w)
        return jnp.concatenate([jnp.concatenate([w, z], axis=-1), jnp.concatenate([z, w], axis=-1)], axis=-2)

    w1bd = blockdiag(w_cmp1.reshape(2, CMP_BLOCK, N_HEAD_DIM, CMP_HIDDEN).astype(BF16))
    w2bd = blockdiag(w_cmp2)
    pe2 = jnp.tile(pe_cmp, (1, 1, N_KV_GROUPS))
    return pe2, w1bd[:, :half].astype(BF16), w1bd[:, half:].astype(BF16), w2bd.astype(BF16)


def kernel(x, g_norm_mix, w_in, b_in, b_fgate, conv_qk, g_mlstm_head, pe_cmp, w_cmp1, w_cmp2,
           rel_bias, w_branch, w_out, g_norm_ffn, w_gate, w_up, w_down, g_final):
    B, T, D = x.shape
    assert D == D_MODEL and w_in.shape[0] == 1, "one residual block (DEPTH == 1)"
    N = B * T
    x2 = x.reshape(N, D)

    idx, scale = _inproj_perm()
    w_r = _wprep_t(w_in[0].T, g_norm_mix[0].reshape(1, D), idx, scale)
    b_r = _gather_cols(b_in[0].reshape(1, -1), idx, scale, F32)
    cw = jnp.zeros((8, 2 * M_WIDTH), F32).at[:CONV_WIDTH].set(conv_qk[0])
    mqk, mv, mo, nq, cin, nsw, mg, small = _inproj(x2, w_r, b_r)

    bf8 = jnp.concatenate([jnp.zeros((M_HEADS,), F32), b_fgate[0].astype(F32)])
    bfc = jnp.broadcast_to(jnp.tile(bf8, T // M_CHUNK)[:, None], (8 * (T // M_CHUNK), M_CHUNK))
    ltri = jnp.asarray(np.tril(np.ones((M_CHUNK, M_CHUNK), np.float32)))
    r3 = lambda a: a.reshape(B, T, a.shape[-1])
    ym = _mlstm(r3(mqk), r3(mv), r3(mo), r3(small), cw, bfc, g_mlstm_head[0].reshape(1, M_WIDTH), ltri)

    pe2, w1t, w1b, w2 = _compress_weights(pe_cmp[0], w_cmp1[0], w_cmp2[0])
    ckv = _compress(cin.reshape(2, B, T, LANES), pe2, w1t, w1b, w2)
    tbl = rel_bias.astype(F32).T.reshape(-1)
    nqt = T // TQ
    bt = _bias_tiles(tbl, nqt + 1, "tok")
    wb = _bias_tiles(tbl, min(WINDOW // TQ + 1, nqt) + 1, "win")
    cb = _bias_tiles(tbl, nqt, "cmp", n_cmp=(T - CMP_BLOCK) // CMP_STRIDE + 1)
    ov, emat = _nsa_constants(T)
    yn = _nsa(r3(nq), r3(nsw), ckv, r3(small), bt, wb, cb, ov, emat)

    wbm = w_branch[0, 0].astype(BF16)
    wbn = jnp.concatenate(
        [w_branch[0, 1, (g * N_REP + r) * N_HEAD_DIM:(g * N_REP + r + 1) * N_HEAD_DIM]
         for r in range(N_REP) for g in range(N_KV_GROUPS)], axis=0).astype(BF16)
    ident = lambda n: (np.arange(n, dtype=np.int32), np.ones((n,), np.float32))
    g_ffn = g_norm_ffn[0].reshape(D, 1)
    out = _merge(x2, ym.reshape(N, M_WIDTH), yn.reshape(N, N_WIDTH), mg, wbm, wbn,
                 w_out[0].astype(BF16), _wprep(w_gate[0], g_ffn, *ident(D_FF)),
                 _wprep(w_up[0], g_ffn, *ident(D_FF)),
                 _wprep(w_down[0], jnp.ones((D_FF, 1), F32), *ident(D)),
                 g_final.reshape(1, D))
    return out.reshape(B, T, D)
```

```python
import functools

import numpy as np
import jax
import jax.numpy as jnp
from jax import lax
from jax.experimental import pallas as pl
from jax.experimental.pallas import tpu as pltpu

F32 = jnp.float32
BF16 = jnp.bfloat16
HI = lax.Precision.HIGHEST

D_MODEL = 1024
M_HEADS = 4
M_HEAD_DIM = 128
M_WIDTH = M_HEADS * M_HEAD_DIM
M_CHUNK = 128
CONV_WIDTH = 4
N_HEADS = 8
N_KV_GROUPS = 2
N_REP = N_HEADS // N_KV_GROUPS
N_HEAD_DIM = 64
N_WIDTH = N_HEADS * N_HEAD_DIM
N_KV_WIDTH = N_KV_GROUPS * N_HEAD_DIM
CMP_BLOCK = 32
CMP_STRIDE = 16
CMP_HIDDEN = 2 * N_HEAD_DIM
SEL_BLOCK = 64
SEL_TOPK = 16
WINDOW = 512
REL_BUCKETS = 32
REL_MAX_DIST = 1024
N_BRANCH = 2
D_FF = 2816
RMS_EPS = 1e-6
BIG = 1e9
NEG = -1e30

LANES = 128
TQ = 128
NSA_BATCH = 1
VMEM_LIMIT = 56 * 1024 * 1024

_OFF_MQ, _OFF_MK, _OFF_MV, _OFF_MO = 0, 512, 1024, 1536
_OFF_MI, _OFF_MF, _OFF_NQ, _OFF_NKV = 2048, 2052, 2056, 2568
_OFF_NGATE, _OFF_MERGE, _D_IN = 3336, 3360, 5408
_SMALL_NGATE = 8

_SEGS = (("mqk", 1024), ("mv", 512), ("mo", 512), ("nq", 512), ("kc", 128),
         ("vc", 128), ("nsw", 512), ("mg", 2048), ("small", 128))
_D_IN_PAD = sum(w for _, w in _SEGS)
LOG2E = 1.4426950408889634


def _dot(a, b, **kw):
    return jnp.dot(a, b, preferred_element_type=F32, **kw)


def _dot_nt(a, b):
    return lax.dot_general(a, b, (((1,), (1,)), ((), ())), preferred_element_type=F32)


def _dot_tn(a, b):
    return lax.dot_general(a, b, (((0,), (0,)), ((), ())), preferred_element_type=F32)


def _const_spec(shape):
    nd = len(shape)
    return pl.BlockSpec(shape, lambda *_: (0,) * nd, pipeline_mode=pl.Buffered(1))


def _bucket_thresholds():
    max_exact = REL_BUCKETS // 2
    assert REL_MAX_DIST == 64 * max_exact and REL_BUCKETS - max_exact == 16
    thr = []
    for k in range(1, REL_BUCKETS - max_exact):
        t = max_exact
        while t ** 8 < (max_exact ** 8) * (2 ** (3 * k)):
            t += 1
        thr.append(t)
    return tuple(thr)


_BUCKET_THR = _bucket_thresholds()


def _inproj_perm():
    idx = np.zeros((_D_IN_PAD,), np.int32)
    scale = np.zeros((_D_IN_PAD,), np.float32)
    pos = 0

    def put(cols, s=1.0):
        nonlocal pos
        n = len(cols)
        idx[pos:pos + n] = cols
        scale[pos:pos + n] = s
        pos += n

    put(np.arange(_OFF_MQ, _OFF_MV))
    put(np.arange(_OFF_MV, _OFF_MO))
    put(np.arange(_OFF_MO, _OFF_MI))
    nq = np.zeros((N_WIDTH,), np.int32)
    for r in range(N_REP):
        for g in range(N_KV_GROUPS):
            for d in range(N_HEAD_DIM):
                nq[r * 128 + g * 64 + d] = _OFF_NQ + (g * N_REP + r) * N_HEAD_DIM + d
    put(nq, N_HEAD_DIM ** -0.5 * LOG2E)
    kv = lambda j, g: np.arange(_OFF_NKV + (j * N_KV_GROUPS + g) * N_HEAD_DIM,
                                _OFF_NKV + (j * N_KV_GROUPS + g + 1) * N_HEAD_DIM)
    put(np.arange(_OFF_NKV, _OFF_NKV + 256))
    put(np.concatenate([kv(2, 0), kv(2, 1)]))
    put(np.concatenate([kv(4, 0), kv(4, 1)]))
    put(np.concatenate([kv(3, 0), kv(3, 1)]))
    put(np.concatenate([kv(5, 0), kv(5, 1)]))
    put(np.arange(_OFF_MERGE, _D_IN))
    put(np.arange(_OFF_MI, _OFF_MI + 8))
    put(np.arange(_OFF_NGATE, _OFF_NGATE + 24))
    pos += LANES - 32
    assert pos == _D_IN_PAD
    return idx, scale


def _gather_cols(a, idx, scale, dtype):
    pieces = []
    start = 0
    n = len(idx)
    for c in range(1, n + 1):
        same = c < n and scale[c] == scale[start] and (scale[c] == 0.0 or idx[c] == idx[c - 1] + 1)
        if not same:
            if scale[start] == 0.0:
                piece = jnp.zeros((a.shape[0], c - start), dtype)
            else:
                piece = a[:, int(idx[start]):int(idx[start]) + (c - start)]
                if scale[start] != 1.0:
                    piece = piece * float(scale[start])
            pieces.append(piece.astype(dtype))
            start = c
    return jnp.concatenate(pieces, axis=1)


def _col_runs(idx, scale):
    runs = []
    start = 0
    n = len(idx)
    for c in range(1, n + 1):
        same = c < n and scale[c] == scale[start] and (scale[c] == 0.0 or idx[c] == idx[c - 1] + 1)
        if not same:
            runs.append((start, int(idx[start]), c - start, float(scale[start])))
            start = c
    return runs


def _wprep_kernel(w_ref, g_ref, o_ref, *, runs):
    g = g_ref[...]
    for dst, src, n, scale in runs:
        if scale == 0.0:
            o_ref[:, dst:dst + n] = jnp.zeros((o_ref.shape[0], n), o_ref.dtype)
        else:
            o_ref[:, dst:dst + n] = (w_ref[:, src:src + n] * (g * scale)).astype(o_ref.dtype)


def _wprep_t_kernel(wt_ref, g_ref, o_ref, *, runs):
    g = g_ref[...]
    for dst, src, n, scale in runs:
        if scale == 0.0:
            o_ref[dst:dst + n, :] = jnp.zeros((n, o_ref.shape[1]), o_ref.dtype)
        else:
            o_ref[dst:dst + n, :] = (wt_ref[src:src + n, :] * (g * scale)).astype(o_ref.dtype)


def _wprep_t(wt, g, idx, scale, tc=256):
    rows, cols = wt.shape
    assert cols % tc == 0
    return pl.pallas_call(
        functools.partial(_wprep_t_kernel, runs=_col_runs(idx, scale)),
        out_shape=jax.ShapeDtypeStruct((len(idx), cols), BF16), grid=(cols // tc,),
        in_specs=[pl.BlockSpec((rows, tc), lambda i: (0, i)), pl.BlockSpec((1, tc), lambda i: (0, i))],
        out_specs=pl.BlockSpec((len(idx), tc), lambda i: (0, i)), name="wprep_t",
        compiler_params=pltpu.CompilerParams(dimension_semantics=("arbitrary",),
                                             vmem_limit_bytes=VMEM_LIMIT),
    )(wt, g)


def _wprep(w, g, idx, scale, tr=256):
    rows, cols = w.shape
    assert rows % tr == 0
    return pl.pallas_call(
        functools.partial(_wprep_kernel, runs=_col_runs(idx, scale)),
        out_shape=jax.ShapeDtypeStruct((rows, len(idx)), BF16), grid=(rows // tr,),
        in_specs=[pl.BlockSpec((tr, cols), lambda i: (i, 0)), pl.BlockSpec((tr, 1), lambda i: (i, 0))],
        out_specs=pl.BlockSpec((tr, len(idx)), lambda i: (i, 0)), name="wprep",
        compiler_params=pltpu.CompilerParams(dimension_semantics=("arbitrary",),
                                             vmem_limit_bytes=VMEM_LIMIT),
    )(w, g)


def _inproj_kernel(x_ref, w_ref, b_ref, mqk_ref, mv_ref, mo_ref, nq_ref,
                   cin_ref, nsw_ref, mg_ref, small_ref):
    x = x_ref[...]
    rs = lax.rsqrt(jnp.mean(x * x, axis=-1, keepdims=True) + RMS_EPS)
    xb = x.astype(BF16)

    def seg(a, n):
        return _dot_nt(xb, w_ref[a:a + n, :]) * rs + b_ref[:, a:a + n]

    off = {}
    pos = 0
    for name, w in _SEGS:
        off[name] = pos
        pos += w

    mqk_ref[...] = seg(off["mqk"], 1024).astype(BF16)
    mv_ref[...] = seg(off["mv"], 512).astype(BF16)
    mo_ref[...] = seg(off["mo"], 512).astype(BF16)
    nq_ref[...] = seg(off["nq"], 512).astype(BF16)
    cc = seg(off["kc"], 2 * LANES)
    cin_ref[0] = cc[:, :LANES]
    cin_ref[1] = cc[:, LANES:]
    nsw_ref[:, 0:2 * LANES] = seg(off["nsw"], 2 * LANES).astype(BF16)
    vv = seg(off["nsw"] + 2 * LANES, 2 * LANES)
    lo = lax.broadcasted_iota(jnp.int32, (x.shape[0], LANES), 1) < N_HEAD_DIM
    for j in range(2):
        vj = vv[:, j * LANES:(j + 1) * LANES]
        nsw_ref[:, (2 + 2 * j) * LANES:(3 + 2 * j) * LANES] = jnp.where(lo, vj, 1.0).astype(BF16)
        nsw_ref[:, (3 + 2 * j) * LANES:(4 + 2 * j) * LANES] = jnp.where(lo, 1.0, vj).astype(BF16)
    mg_ref[...] = seg(off["mg"], 2048).astype(BF16)
    small_ref[...] = seg(off["small"], 128)


def _inproj(x2, w, b, tm=1024):
    n = x2.shape[0]
    assert n % tm == 0
    row = lambda width: pl.BlockSpec((tm, width), lambda i: (i, 0))
    out_shape = (
        jax.ShapeDtypeStruct((n, 1024), BF16), jax.ShapeDtypeStruct((n, 512), BF16),
        jax.ShapeDtypeStruct((n, 512), BF16), jax.ShapeDtypeStruct((n, 512), BF16),
        jax.ShapeDtypeStruct((2, n, 128), F32), jax.ShapeDtypeStruct((n, 768), BF16),
        jax.ShapeDtypeStruct((n, 2048), BF16), jax.ShapeDtypeStruct((n, 128), F32))
    out_specs = (row(1024), row(512), row(512), row(512),
                 pl.BlockSpec((2, tm, 128), lambda i: (0, i, 0)), row(768), row(2048), row(128))
    return pl.pallas_call(
        _inproj_kernel, out_shape=out_shape, grid=(n // tm,),
        in_specs=[row(D_MODEL), _const_spec((_D_IN_PAD, D_MODEL)), _const_spec((1, _D_IN_PAD))],
        out_specs=out_specs, name="inproj",
        compiler_params=pltpu.CompilerParams(dimension_semantics=("arbitrary",),
                                             vmem_limit_bytes=VMEM_LIMIT),
    )(x2, w, b)


def _mlstm_kernel(mqk_ref, mv_ref, mo_ref, small_ref, cw_ref, bfc_ref, gh_ref, ltri_ref,
                  ym_ref, ebuf, c_st, m_st, gx, g_bc, g_d, g_a, g_mrow, g_blast, g_amax):
    L = M_CHUNK
    H = M_HEADS
    c = pl.program_id(1)
    n_chunks = gx.shape[0] // 8
    n_sub = mqk_ref.shape[1] // L
    lanes =lax.broadcasted_iota(jnp.int32, gx.shape, 1)

    @pl.when(c == 0)
    def _():
        ebuf[0:8, :] = jnp.zeros((8, 2 * M_WIDTH), F32)
        c_st[...] = jnp.zeros_like(c_st)
        m_st[...] = jnp.zeros_like(m_st)
        for cc in range(n_chunks):
            gx[cc * 8:(cc + 1) * 8, :] = small_ref[0, cc * L:(cc + 1) * L, :].T[0:8, :]
        x = gx[...]
        fg = x + bfc_ref[...]
        lf = jnp.minimum(fg, 0.0) - jnp.log1p(jnp.exp(-jnp.abs(fg)))
        bc = lax.dot_general(lf, ltri_ref[...], (((1,), (1,)), ((), ())),
                             precision=HI, preferred_element_type=F32)
        ig = pltpu.roll(x, H, 0)
        blast = bc[:, L - 1:L]
        a_all = blast - bc + ig
        d = ig - bc
        cm = d
        sh = 1
        while sh < L:
            cm = jnp.maximum(cm, jnp.where(lanes >= sh, pltpu.roll(cm, sh, 1), -jnp.inf))
            sh *= 2
        g_bc[...] = bc
        g_d[...] = d
        g_a[...] = a_all
        g_mrow[...] = bc + cm
        g_blast[...] = jnp.broadcast_to(blast, gx.shape)
        g_amax[...] = jnp.broadcast_to(jnp.max(a_all, axis=-1, keepdims=True), gx.shape)

    row = lax.broadcasted_iota(jnp.int32, (L, L), 0)
    col = lax.broadcasted_iota(jnp.int32, (L, L), 1)
    causal = row >= col
    ones = jnp.ones((L, M_HEAD_DIM), BF16)

    def chunk(j):
        rows = slice(j * L, (j + 1) * L)
        ebuf[8:8 + L, :] = mqk_ref[0, rows, :].astype(F32)
        ext = ebuf[...]
        conv = cw_ref[CONV_WIDTH - 1:CONV_WIDTH, :] * ext[8:, :]
        for k in range(1, CONV_WIDTH):
            conv = conv + cw_ref[CONV_WIDTH - 1 - k:CONV_WIDTH - k, :] * pltpu.roll(ext, k, 0)[8:, :]
        ebuf[0:8, :] = ebuf[L:L + 8, :]
        qk = (conv * jax.nn.sigmoid(conv)).astype(BF16)

        r8 = pl.ds(pl.multiple_of((c * n_sub + j) * 8, 8), 8)
        bc = g_bc[r8, :]
        dt = g_d[r8, :]
        blast = g_blast[r8, :]
        m_prev = m_st[...]
        m_new = jnp.maximum(blast + m_prev, g_amax[r8, :])
        decay = jnp.exp(blast + m_prev - m_new)
        w_row = jnp.exp(g_a[r8, :] - m_new)
        log_inter = bc + m_prev
        m_i = jnp.maximum(g_mrow[r8, :], log_inter)
        k_scale = M_HEAD_DIM ** -0.5
        s_inter = jnp.exp(log_inter - m_i) * k_scale
        emi = jnp.exp(-m_i)
        u = bc - m_i + float(np.log(k_scale))
        cols = jnp.concatenate([u, s_inter, emi, w_row, jnp.zeros((LANES - 32, L), F32)], axis=0).T
        u_all, s_inter_all = cols[:, 0:8], cols[:, 8:16]
        emi_all, w_all = cols[:, 16:24], cols[:, 24:32]

        for h in range(M_HEADS):
            sl = slice(h * M_HEAD_DIM, (h + 1) * M_HEAD_DIM)
            qb = qk[:, sl]
            kb = qk[:, M_WIDTH + h * M_HEAD_DIM:M_WIDTH + (h + 1) * M_HEAD_DIM]
            v = mv_ref[0, rows, sl]
            cn_prev = c_st[h]

            g = H + h
            logp = jnp.where(causal, u_all[:, g:g + 1] + dt[g:g + 1, :], -jnp.inf)
            p = jnp.exp(logp) * _dot_nt(qb, kb)
            nd = (_dot(p.astype(BF16), jnp.concatenate([v, ones], axis=1))
                  + s_inter_all[:, g:g + 1] * _dot(qb, cn_prev.astype(BF16)))
            hh = nd[:, :M_HEAD_DIM] / jnp.maximum(jnp.abs(nd[:, M_HEAD_DIM:]), emi_all[:, g:g + 1])
            hh = hh * lax.rsqrt(jnp.mean(hh * hh, axis=-1, keepdims=True) + RMS_EPS) * gh_ref[:, sl]
            ym_ref[0, rows, sl] = (jax.nn.sigmoid(mo_ref[0, rows, sl].astype(F32)) * hh).astype(ym_ref.dtype)

            w_col = w_all[:, g:g + 1]
            vw = jnp.concatenate([v.astype(F32) * w_col, jnp.broadcast_to(w_col, (L, M_HEAD_DIM))], axis=1)
            c_st[h] = decay[g:g + 1, 0:1] * cn_prev + _dot_tn(kb, vw.astype(BF16))

        m_st[...] = m_new

    for j in range(n_sub):
        chunk(j)


def _mlstm(mqk, mv, mo, small, cw, bf, gh, ltri, n_sub=4):
    B, T, _ = mqk.shape
    L = M_CHUNK
    assert T % (n_sub * L) == 0
    nc = T // L
    assert L == LANES and 8 * nc <= LANES
    blk = lambda w: pl.BlockSpec((1, n_sub * L, w), lambda b, c: (b, c, 0))
    gate = lambda: pltpu.VMEM((8 * nc, L), F32)
    return pl.pallas_call(
        _mlstm_kernel, out_shape=jax.ShapeDtypeStruct((B, T, M_WIDTH), BF16),
        grid=(B, nc // n_sub),
        in_specs=[blk(2 * M_WIDTH), blk(M_WIDTH), blk(M_WIDTH),
                  pl.BlockSpec((1, T, LANES), lambda b, c: (b, 0, 0)),
                  _const_spec((8, 2 * M_WIDTH)), _const_spec((8 * nc, L)),
                  _const_spec((1, M_WIDTH)), _const_spec((L, L))],
        out_specs=blk(M_WIDTH),
        scratch_shapes=[pltpu.VMEM((L + 8, 2 * M_WIDTH), F32),
                        pltpu.VMEM((M_HEADS, M_HEAD_DIM, 2 * M_HEAD_DIM), F32),
                        pltpu.VMEM((8, L), F32),
                        gate(), gate(), gate(), gate(), gate(), gate(), gate()],
        name="mlstm",
        compiler_params=pltpu.CompilerParams(dimension_semantics=("arbitrary", "arbitrary")),
    )(mqk, mv, mo, small, cw, bf, gh, ltri)


def _compress_kernel(x_ref, pe_ref, w1t_ref, w1b_ref, w2_ref, o_ref, *, nch, nbt):
    half = CMP_BLOCK // 2
    rows = nbt * nch
    top = jnp.zeros((rows, 2 * CMP_HIDDEN), F32)
    bot = jnp.zeros((rows, 2 * CMP_HIDDEN), F32)
    for p in range(half):
        xp = jnp.concatenate([x_ref[0, e, pl.ds(p, nch, stride=CMP_STRIDE), :] for e in range(nbt)], axis=0)
        top = top + _dot((xp + pe_ref[0, p:p + 1, :]).astype(BF16), w1t_ref[0, p])
        bot = bot + _dot((xp + pe_ref[0, half + p:half + p + 1, :]).astype(BF16), w1b_ref[0, p])
    hid = top + pltpu.roll(bot, rows - 1, 0)
    act = hid * jax.nn.sigmoid(hid)
    out = _dot(act.astype(BF16), w2_ref[0]).astype(o_ref.dtype)
    for e in range(nbt):
        o_ref[0, e] = out[e * nch:(e + 1) * nch]


def _compress(cin, pe2, w1t, w1b, w2):
    _, B, T, _ = cin.shape
    nch = T // CMP_STRIDE
    nbt = 4 if B % 4 == 0 else 1
    sel = lambda *shape: pl.BlockSpec((1,) + shape, lambda j, b: (j,) + (0,) * len(shape))
    return pl.pallas_call(
        functools.partial(_compress_kernel, nch=nch, nbt=nbt),
        out_shape=jax.ShapeDtypeStruct((2, B, nch, LANES), BF16),
        grid=(2, B // nbt),
        in_specs=[pl.BlockSpec((1, nbt, T, LANES), lambda j, b: (j, b, 0, 0)),
                  sel(CMP_BLOCK, LANES), sel(CMP_BLOCK // 2, LANES, 2 * CMP_HIDDEN),
                  sel(CMP_BLOCK // 2, LANES, 2 * CMP_HIDDEN), sel(2 * CMP_HIDDEN, LANES)],
        out_specs=pl.BlockSpec((1, nbt, nch, LANES), lambda j, b: (j, b, 0, 0)),
        name="compress",
        compiler_params=pltpu.CompilerParams(dimension_semantics=("arbitrary", "arbitrary")),
    )(cin, pe2, w1t, w1b, w2)


def _bias_kernel(tbl_ref, o_ref, *, kind, n_cmp):
    pid = pl.program_id(0)
    toeplitz = kind != "cmp"
    if toeplitz:
        k = lax.broadcasted_iota(jnp.int32, (8, 2 * LANES), 1)
        dist = (pid - 1) * TQ + jnp.where(k < LANES, -k, 2 * LANES - k)
    else:
        al = lax.broadcasted_iota(jnp.int32, (CMP_STRIDE, 2 * LANES), 0)
        m = lax.broadcasted_iota(jnp.int32, (CMP_STRIDE, 2 * LANES), 1)
        dist = pid * TQ - (CMP_BLOCK - 1) + al - CMP_STRIDE * jnp.where(m < LANES, m, m - 2 * LANES)
    n = jnp.maximum(dist, 0)
    cnt = jnp.zeros_like(n)
    for t in _BUCKET_THR:
        cnt = cnt + jnp.where(n >= t, 1, 0)
    bucket = jnp.where(n < REL_BUCKETS // 2, n, REL_BUCKETS // 2 + cnt)
    if kind == "tok":
        madd = jnp.where((dist >= 0) & (pid > 0), 0.0, NEG)
    elif kind == "win":
        madd = jnp.where((dist >= 0) & (dist < WINDOW) & (pid > 0), 0.0, NEG)
    else:
        madd = jnp.where(dist >= 0, 0.0, NEG)
        valid_c = lax.broadcasted_iota(jnp.int32, (TQ, LANES), 1) < n_cmp
    for h in range(N_HEADS):
        val = jnp.zeros(dist.shape, F32)
        for bb in range(REL_BUCKETS):
            val = jnp.where(bucket == bb, tbl_ref[h * REL_BUCKETS + bb], val)
        val = val * LOG2E + madd
        if toeplitz:
            g = jnp.broadcast_to(val[0:1, :], (TQ, 2 * LANES))
            val = pltpu.roll(g, 0, 1, stride=1, stride_axis=0)[:, :LANES]
        else:
            bands = [val[:, :LANES]] + [pltpu.roll(val, ah, 1)[:, :LANES] for ah in range(1, TQ // CMP_STRIDE)]
            val = jnp.where(valid_c, jnp.concatenate(bands, axis=0), NEG)
        o_ref[0, h * TQ:(h + 1) * TQ, :] = val


def _bias_tiles(tbl, n_tiles, kind, n_cmp=0):
    return pl.pallas_call(
        functools.partial(_bias_kernel, kind=kind, n_cmp=n_cmp),
        out_shape=jax.ShapeDtypeStruct((n_tiles, N_HEADS * TQ, LANES), F32),
        grid=(n_tiles,),
        in_specs=[pl.BlockSpec(memory_space=pltpu.SMEM)],
        out_specs=pl.BlockSpec((1, N_HEADS * TQ, LANES), lambda i: (i, 0, 0)),
        name="bias_" + kind,
        compiler_params=pltpu.CompilerParams(dimension_semantics=("arbitrary",)),
    )(tbl)


def _nsa_kernel(nq4_ref, ks_ref, kw_ref, vs0_ref, vs1_ref, vw0_ref, vw1_ref, kc_ref, vc_ref,
                small4_ref, bt_ref, wb_ref, cb4_ref, ovt_ref, et_ref, yn_ref,
                lhs_sc, z_sc, zw_sc, seln_sc, partc_sc, *, nb, n_slc, n_top, n_win, n_chunks_max):
    HR = N_HEADS * TQ
    GR = N_REP * TQ
    CH = 4 * TQ
    G4 = 4
    vs_refs = (vs0_ref, vs1_ref)
    vw_refs = (vw0_ref, vw1_ref)
    lane = lax.broadcasted_iota(jnp.int32, (TQ, LANES), 1)
    lo = lane < N_HEAD_DIM

    def stack_heads(q_all):
        zero = jnp.zeros((TQ, LANES), BF16)
        parts = []
        for g in range(N_KV_GROUPS):
            for r in range(N_REP):
                qr = q_all[:, r * LANES:(r + 1) * LANES]
                parts.append(jnp.where(lo if g == 0 else jnp.logical_not(lo), qr, zero))
        return parts

    def pair(o, r):
        return jnp.where(lo, o[r * TQ:(r + 1) * TQ], o[(N_REP + r) * TQ:(N_REP + r + 1) * TQ])

    def normed(acc, r):
        den = jnp.where(lo, acc[(N_REP + r) * TQ:(N_REP + r + 1) * TQ], acc[r * TQ:(r + 1) * TQ])
        return pair(acc, r) / pltpu.roll(den, N_HEAD_DIM, 1)

    def gate(sg, r, ci):
        c0 = _SMALL_NGATE + r * 3 + ci
        c1 = _SMALL_NGATE + (N_REP + r) * 3 + ci
        return jnp.where(lo, sg[:, c0:c0 + 1], sg[:, c1:c1 + 1])

    def group_prep(s, n_chunks, qi):
        qs4 = jnp.concatenate(
            [p for j in range(G4) for p in stack_heads(nq4_ref[s, j * TQ:(j + 1) * TQ, :])], axis=0)
        rows = lax.broadcasted_iota(jnp.int32, (G4 * HR, 1), 0)
        t_rows = (qi + jnp.right_shift(rows, 10)) * TQ + (rows & (TQ - 1))

        z = _dot_nt(qs4, kc_ref[0, s]) + cb4_ref[...].reshape(G4 * HR, LANES)
        e = jnp.exp2(z - jnp.max(z, axis=-1, keepdims=True))
        l = _dot(e.astype(BF16), jnp.ones((LANES, LANES), BF16))
        p_c = e * jnp.where(t_rows >= CMP_BLOCK - 1, 1.0 / jnp.maximum(l, 1e-30), 0.0)
        o_c = _dot(p_c.astype(BF16), vc_ref[0, s])
        sg4 = jax.nn.sigmoid(small4_ref[s])
        for j in range(G4):
            for r in range(N_REP):
                partc_sc[s, j, r] = (gate(sg4[j * TQ:(j + 1) * TQ], r, 0)
                                     * pair(o_c[j * HR:(j + 1) * HR], r))

        W = G4 * N_KV_GROUPS * TQ
        jb = lax.broadcasted_iota(jnp.int32, (n_slc, W), 0)
        col = lax.broadcasted_iota(jnp.int32, (n_slc, W), 1)
        tq = (qi + jnp.right_shift(col, 8)) * TQ + (col & (TQ - 1))
        cur = jnp.right_shift(tq, 6)
        elig = jb <= cur
        if n_chunks * CH <= n_top * SEL_BLOCK:
            sel = jnp.where(elig, 1.0, 0.0)
        else:
            forced = (jb == 0) | (jb == cur) | (jb == cur - 1)
            psums = []
            for j in range(G4):
                for g in range(N_KV_GROUPS):
                    base = j * HR + g * GR
                    ps = p_c[base:base + TQ]
                    for r in range(1, N_REP):
                        ps = ps + p_c[base + r * TQ:base + (r + 1) * TQ]
                    psums.append(ps)
            imp = lax.dot_general(ovt_ref[...], jnp.concatenate(psums, axis=0), (((1,), (1,)), ((), ())),
                                  precision=HI, preferred_element_type=F32)
            score = jnp.where(elig, jnp.where(forced, BIG, imp), -BIG)
            cnt = jnp.zeros((n_slc, W), F32)
            for i in range(n_slc):
                si = score[i:i + 1, :]
                tie = jnp.where(jb > i, 1.0, 0.0)
                cnt = cnt + jnp.where(si > score, 1.0, jnp.where(si == score, tie, 0.0))
            sel = jnp.where((cnt < n_top) & (score > -BIG / 2), 1.0, 0.0)
        seln = jnp.concatenate([sel - 1.0, jnp.zeros((LANES - n_slc, W), F32)], axis=0).T.astype(BF16)
        for j in range(G4):
            for g in range(N_KV_GROUPS):
                seln_sc[s, j, g] = seln[(j * N_KV_GROUPS + g) * TQ:(j * N_KV_GROUPS + g + 1) * TQ]

    def tiles(n_pairs, qi, jq):
        qrows = pl.ds(pl.multiple_of(jq * TQ, TQ), TQ)
        w0 = jnp.maximum(qi - (n_win - 1), 0)
        woff = pl.multiple_of(w0 * TQ, TQ)
        elems = range(nb)
        cat = lambda xs: xs[0] if len(xs) == 1 else jnp.concatenate(xs, axis=0)
        rep = lambda t: t if nb == 1 else jnp.concatenate([t] * nb, axis=0)
        parts = [stack_heads(nq4_ref[s, qrows, :]) for s in elems]
        qs = [jnp.concatenate(parts[s], axis=0) for s in elems]

        def bias_tiles(ref, first, count, tile0):
            tiles_ = [ref[jnp.maximum(qi - tile0 - (first + u) + 1, 0)] for u in range(count)]
            return rep(tiles_[0] if count == 1 else jnp.concatenate(tiles_, axis=1))

        zmax = None
        for j0 in range(0, n_win, 2):
            nj = min(2, n_win - j0)
            zp = (cat([_dot_nt(qs[s], kw_ref[s, pl.ds(woff + j0 * TQ, nj * TQ), :]) for s in elems])
                  + bias_tiles(wb_ref, j0, nj, w0))
            zw_sc[:, j0 * TQ:(j0 + nj) * TQ] = zp
            for u in range(nj):
                zj = zp[:, u * TQ:(u + 1) * TQ]
                zmax = zj if zmax is None else jnp.maximum(zmax, zj)
        mw = jnp.max(zmax, axis=-1, keepdims=True)
        pw = jnp.exp2(zw_sc[...] - mw).astype(BF16)
        acc_w = cat([_dot(pw[s * HR + g * GR:s * HR + (g + 1) * GR],
                          vw_refs[g][s, pl.ds(woff, n_win * TQ), :])
                     for s in elems for g in range(N_KV_GROUPS)])

        for s in elems:
            for g in range(N_KV_GROUPS):
                seln = seln_sc[s, jq, g]
                for r in range(N_REP):
                    h = g * N_REP + r
                    lhs_sc[s, h * TQ:(h + 1) * TQ, :] = jnp.concatenate([parts[s][h], seln], axis=1)

        zmax = None
        for pi in range(n_pairs):
            k0 = pi * 2 * TQ
            zp = (cat([_dot_nt(lhs_sc[s], jnp.concatenate([ks_ref[s, k0:k0 + 2 * TQ, :],
                                                           et_ref[k0:k0 + 2 * TQ, :]], axis=1))
                       for s in elems])
                  + bias_tiles(bt_ref, 2 * pi, 2, 0))
            z_sc[pi // 2, :, (pi % 2) * 2 * TQ:(pi % 2 + 1) * 2 * TQ] = zp
            for u in range(2):
                zt = zp[:, u * TQ:(u + 1) * TQ]
                zmax = zt if zmax is None else jnp.maximum(zmax, zt)
        ms = jnp.max(zmax, axis=-1, keepdims=True)
        acc = [None] * (nb * N_KV_GROUPS)
        for c in range((n_pairs + 1) // 2):
            width = min(CH, n_pairs * 2 * TQ - c * CH)
            p = jnp.exp2(z_sc[c, :, 0:width] - ms).astype(BF16)
            for s in elems:
                for g in range(N_KV_GROUPS):
                    i = s * N_KV_GROUPS + g
                    d = _dot(p[i * GR:(i + 1) * GR], vs_refs[g][s, c * CH:c * CH + width, :])
                    acc[i] = d if acc[i] is None else acc[i] + d
        acc_s = cat(acc)

        for s in elems:
            sg = jax.nn.sigmoid(small4_ref[s, qrows, :])
            a_s = acc_s[s * HR:(s + 1) * HR]
            a_w = acc_w[s * HR:(s + 1) * HR]
            for r in range(N_REP):
                out = (partc_sc[s, jq, r] + gate(sg, r, 1) * normed(a_s, r)
                       + gate(sg, r, 2) * normed(a_w, r))
                yn_ref[s, qrows, r * LANES:(r + 1) * LANES] = out.astype(yn_ref.dtype)

    for nc in range(1, n_chunks_max + 1):
        @pl.when(pl.program_id(1) == nc - 1)
        def _(nc=nc):
            q0 = G4 * (nc - 1)
            for s in range(nb):
                group_prep(s, nc, q0)

            def body(jq, carry):
                tiles(2 * nc, q0 + jq, jq)
                return carry
            lax.fori_loop(0, G4, body, 0)


def _nsa(nq, nsw, ckv, small, bt, wb, cb, ov, emat):
    B, T, _ = nq.shape
    assert T == 2048, "single 128-wide compressed-key tile assumes T == 2048"
    nqt = T // TQ
    n_cmp = (T - CMP_BLOCK) // CMP_STRIDE + 1
    n_slc = T // SEL_BLOCK
    n_top = min(SEL_TOPK, n_slc)
    n_win = wb.shape[0] - 1
    HR = N_HEADS * TQ
    assert nqt % 4 == 0 and n_slc % 8 == 0 and n_slc <= LANES
    nb = NSA_BATCH if B % NSA_BATCH == 0 else 1
    n_chunks_max = nqt // 4
    kv = lambda j: pl.BlockSpec((nb, T, LANES), lambda b, g: (b, 0, j))
    ck = lambda j: pl.BlockSpec((1, nb, T // CMP_STRIDE, LANES), lambda b, g: (j, b, 0, 0))
    kern = functools.partial(_nsa_kernel, nb=nb, n_slc=n_slc, n_top=n_top, n_win=n_win,
                             n_chunks_max=n_chunks_max)
    return pl.pallas_call(
        kern, out_shape=jax.ShapeDtypeStruct((B, T, N_WIDTH), BF16),
        grid=(B // nb, n_chunks_max),
        in_specs=[pl.BlockSpec((nb, 4 * TQ, N_WIDTH), lambda b, g: (b, g, 0)),
                  kv(0), kv(1), kv(2), kv(3), kv(4), kv(5), ck(0), ck(1),
                  pl.BlockSpec((nb, 4 * TQ, LANES), lambda b, g: (b, g, 0)),
                  _const_spec(bt.shape), _const_spec(wb.shape),
                  pl.BlockSpec((4, HR, LANES), lambda b, g: (g, 0, 0)),
                  _const_spec(ov.shape), _const_spec(emat.shape)],
        out_specs=pl.BlockSpec((nb, 4 * TQ, N_WIDTH), lambda b, g: (b, g, 0)),
        scratch_shapes=[pltpu.VMEM((nb, HR, 2 * LANES), BF16),
                        pltpu.VMEM((n_chunks_max, nb * HR, 4 * TQ), F32),
                        pltpu.VMEM((nb * HR, n_win * TQ), F32),
                        pltpu.VMEM((nb, 4, N_KV_GROUPS, TQ, LANES), BF16),
                        pltpu.VMEM((nb, 4, N_REP, TQ, LANES), F32)],
        name="nsa",
        compiler_params=pltpu.CompilerParams(dimension_semantics=("arbitrary", "arbitrary"),
                                             vmem_limit_bytes=VMEM_LIMIT),
    )(nq, nsw, nsw, nsw, nsw, nsw, nsw, ckv, ckv, small, bt, wb, cb, ov, emat)


def _merge_kernel(x_ref, ym_ref, yn_ref, mg_ref, wbm_ref, wbn_ref, wo_ref,
                  wg_ref, wu_ref, wd_ref, gfin_ref, o_ref, *, tf):
    bm = _dot(ym_ref[...], wbm_ref[...])
    bn = _dot(yn_ref[...], wbn_ref[...])
    mixed = (jax.nn.sigmoid(mg_ref[:, :D_MODEL].astype(F32)) * bm
             + jax.nn.sigmoid(mg_ref[:, D_MODEL:].astype(F32)) * bn)
    h = x_ref[...] + _dot(mixed.astype(BF16), wo_ref[...])
    rs = lax.rsqrt(jnp.mean(h * h, axis=-1, keepdims=True) + RMS_EPS)
    hb = h.astype(BF16)
    acc = jnp.zeros(h.shape, F32)
    for j in range(D_FF // tf):
        gg = _dot(hb, wg_ref[:, j * tf:(j + 1) * tf]) * rs
        uu = _dot(hb, wu_ref[:, j * tf:(j + 1) * tf]) * rs
        act = (gg * jax.nn.sigmoid(gg) * uu).astype(BF16)
        acc = acc + _dot(act, wd_ref[j * tf:(j + 1) * tf, :])
    h2 = h + acc
    o_ref[...] = h2 * lax.rsqrt(jnp.mean(h2 * h2, axis=-1, keepdims=True) + RMS_EPS) * gfin_ref[...]


def _merge(x2, ym, yn, mg, wbm, wbn, wo, wg, wu, wd, gfin, tm=512, tf=256):
    n = x2.shape[0]
    assert n % tm == 0 and D_FF % tf == 0
    row = lambda width: pl.BlockSpec((tm, width), lambda i: (i, 0))
    return pl.pallas_call(
        functools.partial(_merge_kernel, tf=tf),
        out_shape=jax.ShapeDtypeStruct((n, D_MODEL), F32), grid=(n // tm,),
        in_specs=[row(D_MODEL), row(M_WIDTH), row(N_WIDTH), row(N_BRANCH * D_MODEL),
                  _const_spec(wbm.shape), _const_spec(wbn.shape), _const_spec(wo.shape),
                  _const_spec(wg.shape), _const_spec(wu.shape),
                  _const_spec(wd.shape), _const_spec(gfin.shape)],
        out_specs=row(D_MODEL), name="merge_ffn",
        compiler_params=pltpu.CompilerParams(dimension_semantics=("arbitrary",),
                                             vmem_limit_bytes=VMEM_LIMIT),
    )(x2, ym, yn, mg, wbm, wbn, wo, wg, wu, wd, gfin)


def _nsa_constants(T):
    n_cmp = (T - CMP_BLOCK) // CMP_STRIDE + 1
    n_slc = T // SEL_BLOCK
    cs = np.arange(n_cmp) * CMP_STRIDE
    ss = np.arange(n_slc) * SEL_BLOCK
    ov = np.clip(np.minimum(cs[:, None] + CMP_BLOCK, ss[None, :] + SEL_BLOCK)
                 - np.maximum(cs[:, None], ss[None, :]), 0, None) / CMP_STRIDE
    ovt = np.zeros((n_slc, LANES), np.float32)
    ovt[:, :n_cmp] = ov.T
    et = (np.arange(T)[:, None] // SEL_BLOCK == np.arange(LANES)[None, :]).astype(np.float32) * (-NEG)
    return jnp.asarray(ovt), jnp.asarray(et, dtype=BF16)


def _compress_weights(pe_cmp, w_cmp1, w_cmp2):
    half = CMP_BLOCK // 2
    assert N_KV_GROUPS == 2

    def blockdiag(w):
        z = jnp.zeros_like(w)
        return jnp.concatenate([jnp.concatenate([w, z], axis=-1), jnp.concatenate([z, w], axis=-1)], axis=-2)

    w1bd = blockdiag(w_cmp1.reshape(2, CMP_BLOCK, N_HEAD_DIM, CMP_HIDDEN).astype(BF16))
    w2bd = blockdiag(w_cmp2)
    pe2 = jnp.tile(pe_cmp, (1, 1, N_KV_GROUPS))
    return pe2, w1bd[:, :half].astype(BF16), w1bd[:, half:].astype(BF16), w2bd.astype(BF16)


def kernel(x, g_norm_mix, w_in, b_in, b_fgate, conv_qk, g_mlstm_head, pe_cmp, w_cmp1, w_cmp2,
           rel_bias, w_branch, w_out, g_norm_ffn, w_gate, w_up, w_down, g_final):
    B, T, D = x.shape
    assert D == D_MODEL and w_in.shape[0] == 1, "one residual block (DEPTH == 1)"
    N = B * T
    x2 = x.reshape(N, D)

    idx, scale = _inproj_perm()
    w_r = _wprep_t(w_in[0].T, g_norm_mix[0].reshape(1, D), idx, scale)
    b_r = _gather_cols(b_in[0].reshape(1, -1), idx, scale, F32)
    cw = jnp.zeros((8, 2 * M_WIDTH), F32).at[:CONV_WIDTH].set(conv_qk[0])
    mqk, mv, mo, nq, cin, nsw, mg, small = _inproj(x2, w_r, b_r)

    bf8 = jnp.concatenate([jnp.zeros((M_HEADS,), F32), b_fgate[0].astype(F32)])
    bfc = jnp.broadcast_to(jnp.tile(bf8, T // M_CHUNK)[:, None], (8 * (T // M_CHUNK), M_CHUNK))
    ltri = jnp.asarray(np.tril(np.ones((M_CHUNK, M_CHUNK), np.float32)))
    r3 = lambda a: a.reshape(B, T, a.shape[-1])
    ym = _mlstm(r3(mqk), r3(mv), r3(mo), r3(small), cw, bfc, g_mlstm_head[0].reshape(1, M_WIDTH), ltri)

    pe2, w1t, w1b, w2 = _compress_weights(pe_cmp[0], w_cmp1[0], w_cmp2[0])
    ckv = _compress(cin.reshape(2, B, T, LANES), pe2, w1t, w1b, w2)
    tbl = rel_bias.astype(F32).T.reshape(-1)
    nqt = T // TQ
    bt = _bias_tiles(tbl, nqt + 1, "tok")
    wb = _bias_tiles(tbl, min(WINDOW // TQ + 1, nqt) + 1, "win")
    cb = _bias_tiles(tbl, nqt, "cmp", n_cmp=(T - CMP_BLOCK) // CMP_STRIDE + 1)
    ov, emat = _nsa_constants(T)
    yn = _nsa(r3(nq), r3(nsw), ckv, r3(small), bt, wb, cb, ov, emat)

    wbm = w_branch[0, 0].astype(BF16)
    wbn = jnp.concatenate(
        [w_branch[0, 1, (g * N_REP + r) * N_HEAD_DIM:(g * N_REP + r + 1) * N_HEAD_DIM]
         for r in range(N_REP) for g in range(N_KV_GROUPS)], axis=0).astype(BF16)
    ident = lambda n: (np.arange(n, dtype=np.int32), np.ones((n,), np.float32))
    g_ffn = g_norm_ffn[0].reshape(D, 1)
    out = _merge(x2, ym.reshape(N, M_WIDTH), yn.reshape(N, N_WIDTH), mg, wbm, wbn,
                 w_out[0].astype(BF16), _wprep(w_gate[0], g_ffn, *ident(D_FF)),
                 _wprep(w_up[0], g_ffn, *ident(D_FF)),
                 _wprep(w_down[0], jnp.ones((D_FF, 1), F32), *ident(D)),
                 g_final.reshape(1, D))
    return out.reshape(B, T, D)
```

```python
import functools

import numpy as np
import jax
import jax.numpy as jnp
from jax import lax
from jax.experimental import pallas as pl
from jax.experimental.pallas import tpu as pltpu

F32 = jnp.float32
BF16 = jnp.bfloat16
HI = lax.Precision.HIGHEST

D_MODEL = 1024
M_HEADS = 4
M_HEAD_DIM = 128
M_WIDTH = M_HEADS * M_HEAD_DIM
M_CHUNK = 128
CONV_WIDTH = 4
N_HEADS = 8
N_KV_GROUPS = 2
N_REP = N_HEADS // N_KV_GROUPS
N_HEAD_DIM = 64
N_WIDTH = N_HEADS * N_HEAD_DIM
N_KV_WIDTH = N_KV_GROUPS * N_HEAD_DIM
CMP_BLOCK = 32
CMP_STRIDE = 16
CMP_HIDDEN = 2 * N_HEAD_DIM
SEL_BLOCK = 64
SEL_TOPK = 16
WINDOW = 512
REL_BUCKETS = 32
REL_MAX_DIST = 1024
N_BRANCH = 2
D_FF = 2816
RMS_EPS = 1e-6
BIG = 1e9
NEG = -1e30

LANES = 128
TQ = 128
NSA_BATCH = 1
VMEM_LIMIT = 56 * 1024 * 1024

_OFF_MQ, _OFF_MK, _OFF_MV, _OFF_MO = 0, 512, 1024, 1536
_OFF_MI, _OFF_MF, _OFF_NQ, _OFF_NKV = 2048, 2052, 2056, 2568
_OFF_NGATE, _OFF_MERGE, _D_IN = 3336, 3360, 5408
_SMALL_NGATE = 8

_SEGS = (("mqk", 1024), ("mv", 512), ("mo", 512), ("nq", 512), ("kc", 128),
         ("vc", 128), ("nsw", 512), ("mg", 2048), ("small", 128))
_D_IN_PAD = sum(w for _, w in _SEGS)
LOG2E = 1.4426950408889634


def _dot(a, b, **kw):
    return jnp.dot(a, b, preferred_element_type=F32, **kw)


def _dot_nt(a, b):
    return lax.dot_general(a, b, (((1,), (1,)), ((), ())), preferred_element_type=F32)


def _dot_tn(a, b):
    return lax.dot_general(a, b, (((0,), (0,)), ((), ())), preferred_element_type=F32)


def _const_spec(shape):
    nd = len(shape)
    return pl.BlockSpec(shape, lambda *_: (0,) * nd, pipeline_mode=pl.Buffered(1))


def _bucket_thresholds():
    max_exact = REL_BUCKETS // 2
    assert REL_MAX_DIST == 64 * max_exact and REL_BUCKETS - max_exact == 16
    thr = []
    for k in range(1, REL_BUCKETS - max_exact):
        t = max_exact
        while t ** 8 < (max_exact ** 8) * (2 ** (3 * k)):
            t += 1
        thr.append(t)
    return tuple(thr)


_BUCKET_THR = _bucket_thresholds()


def _inproj_perm():
    idx = np.zeros((_D_IN_PAD,), np.int32)
    scale = np.zeros((_D_IN_PAD,), np.float32)
    pos = 0

    def put(cols, s=1.0):
        nonlocal pos
        n = len(cols)
        idx[pos:pos + n] = cols
        scale[pos:pos + n] = s
        pos += n

    put(np.arange(_OFF_MQ, _OFF_MV))
    put(np.arange(_OFF_MV, _OFF_MO))
    put(np.arange(_OFF_MO, _OFF_MI))
    nq = np.zeros((N_WIDTH,), np.int32)
    for r in range(N_REP):
        for g in range(N_KV_GROUPS):
            for d in range(N_HEAD_DIM):
                nq[r * 128 + g * 64 + d] = _OFF_NQ + (g * N_REP + r) * N_HEAD_DIM + d
    put(nq, N_HEAD_DIM ** -0.5 * LOG2E)
    kv = lambda j, g: np.arange(_OFF_NKV + (j * N_KV_GROUPS + g) * N_HEAD_DIM,
                                _OFF_NKV + (j * N_KV_GROUPS + g + 1) * N_HEAD_DIM)
    put(np.arange(_OFF_NKV, _OFF_NKV + 256))
    put(np.concatenate([kv(2, 0), kv(2, 1)]))
    put(np.concatenate([kv(4, 0), kv(4, 1)]))
    put(np.concatenate([kv(3, 0), kv(3, 1)]))
    put(np.concatenate([kv(5, 0), kv(5, 1)]))
    put(np.arange(_OFF_MERGE, _D_IN))
    put(np.arange(_OFF_MI, _OFF_MI + 8))
    put(np.arange(_OFF_NGATE, _OFF_NGATE + 24))
    pos += LANES - 32
    assert pos == _D_IN_PAD
    return idx, scale


def _gather_cols(a, idx, scale, dtype):
    pieces = []
    start = 0
    n = len(idx)
    for c in range(1, n + 1):
        same = c < n and scale[c] == scale[start] and (scale[c] == 0.0 or idx[c] == idx[c - 1] + 1)
        if not same:
            if scale[start] == 0.0:
                piece = jnp.zeros((a.shape[0], c - start), dtype)
            else:
                piece = a[:, int(idx[start]):int(idx[start]) + (c - start)]
                if scale[start] != 1.0:
                    piece = piece * float(scale[start])
            pieces.append(piece.astype(dtype))
            start = c
    return jnp.concatenate(pieces, axis=1)


def _col_runs(idx, scale):
    runs = []
    start = 0
    n = len(idx)
    for c in range(1, n + 1):
        same = c < n and scale[c] == scale[start] and (scale[c] == 0.0 or idx[c] == idx[c - 1] + 1)
        if not same:
            runs.append((start, int(idx[start]), c - start, float(scale[start])))
            start = c
    return runs


def _wprep_kernel(w_ref, g_ref, o_ref, *, runs):
    g = g_ref[...]
    for dst, src, n, scale in runs:
        if scale == 0.0:
            o_ref[:, dst:dst + n] = jnp.zeros((o_ref.shape[0], n), o_ref.dtype)
        else:
            o_ref[:, dst:dst + n] = (w_ref[:, src:src + n] * (g * scale)).astype(o_ref.dtype)


def _wprep_t_kernel(wt_ref, g_ref, o_ref, *, runs):
    g = g_ref[...]
    for dst, src, n, scale in runs:
        if scale == 0.0:
            o_ref[dst:dst + n, :] = jnp.zeros((n, o_ref.shape[1]), o_ref.dtype)
        else:
            o_ref[dst:dst + n, :] = (wt_ref[src:src + n, :] * (g * scale)).astype(o_ref.dtype)


def _wprep_t(wt, g, idx, scale, tc=256):
    rows, cols = wt.shape
    assert cols % tc == 0
    return pl.pallas_call(
        functools.partial(_wprep_t_kernel, runs=_col_runs(idx, scale)),
        out_shape=jax.ShapeDtypeStruct((len(idx), cols), BF16), grid=(cols // tc,),
        in_specs=[pl.BlockSpec((rows, tc), lambda i: (0, i)), pl.BlockSpec((1, tc), lambda i: (0, i))],
        out_specs=pl.BlockSpec((len(idx), tc), lambda i: (0, i)), name="wprep_t",
        compiler_params=pltpu.CompilerParams(dimension_semantics=("arbitrary",),
                                             vmem_limit_bytes=VMEM_LIMIT),
    )(wt, g)


def _wprep(w, g, idx, scale, tr=256):
    rows, cols = w.shape
    assert rows % tr == 0
    return pl.pallas_call(
        functools.partial(_wprep_kernel, runs=_col_runs(idx, scale)),
        out_shape=jax.ShapeDtypeStruct((rows, len(idx)), BF16), grid=(rows // tr,),
        in_specs=[pl.BlockSpec((tr, cols), lambda i: (i, 0)), pl.BlockSpec((tr, 1), lambda i: (i, 0))],
        out_specs=pl.BlockSpec((tr, len(idx)), lambda i: (i, 0)), name="wprep",
        compiler_params=pltpu.CompilerParams(dimension_semantics=("arbitrary",),
                                             vmem_limit_bytes=VMEM_LIMIT),
    )(w, g)


def _inproj_kernel(x_ref, w_ref, b_ref, mqk_ref, mv_ref, mo_ref, nq_ref,
                   cin_ref, nsw_ref, mg_ref, small_ref):
    x = x_ref[...]
    rs = lax.rsqrt(jnp.mean(x * x, axis=-1, keepdims=True) + RMS_EPS)
    xb = x.astype(BF16)

    def seg(a, n):
        return _dot_nt(xb, w_ref[a:a + n, :]) * rs + b_ref[:, a:a + n]

    off = {}
    pos = 0
    for name, w in _SEGS:
        off[name] = pos
        pos += w

    mqk_ref[...] = seg(off["mqk"], 1024).astype(BF16)
    mv_ref[...] = seg(off["mv"], 512).astype(BF16)
    mo_ref[...] = seg(off["mo"], 512).astype(BF16)
    nq_ref[...] = seg(off["nq"], 512).astype(BF16)
    cc = seg(off["kc"], 2 * LANES)
    cin_ref[0] = cc[:, :LANES]
    cin_ref[1] = cc[:, LANES:]
    nsw_ref[:, 0:2 * LANES] = seg(off["nsw"], 2 * LANES).astype(BF16)
    vv = seg(off["nsw"] + 2 * LANES, 2 * LANES)
    lo = lax.broadcasted_iota(jnp.int32, (x.shape[0], LANES), 1) < N_HEAD_DIM
    for j in range(2):
        vj = vv[:, j * LANES:(j + 1) * LANES]
        nsw_ref[:, (2 + 2 * j) * LANES:(3 + 2 * j) * LANES] = jnp.where(lo, vj, 1.0).astype(BF16)
        nsw_ref[:, (3 + 2 * j) * LANES:(4 + 2 * j) * LANES] = jnp.where(lo, 1.0, vj).astype(BF16)
    mg_ref[...] = seg(off["mg"], 2048).astype(BF16)
    small_ref[...] = seg(off["small"], 128)


def _inproj(x2, w, b, tm=1024):
    n = x2.shape[0]
    assert n % tm == 0
    row = lambda width: pl.BlockSpec((tm, width), lambda i: (i, 0))
    out_shape = (
        jax.ShapeDtypeStruct((n, 1024), BF16), jax.ShapeDtypeStruct((n, 512), BF16),
        jax.ShapeDtypeStruct((n, 512), BF16), jax.ShapeDtypeStruct((n, 512), BF16),
        jax.ShapeDtypeStruct((2, n, 128), F32), jax.ShapeDtypeStruct((n, 768), BF16),
        jax.ShapeDtypeStruct((n, 2048), BF16), jax.ShapeDtypeStruct((n, 128), F32))
    out_specs = (row(1024), row(512), row(512), row(512),
                 pl.BlockSpec((2, tm, 128), lambda i: (0, i, 0)), row(768), row(2048), row(128))
    return pl.pallas_call(
        _inproj_kernel, out_shape=out_shape, grid=(n // tm,),
        in_specs=[row(D_MODEL), _const_spec((_D_IN_PAD, D_MODEL)), _const_spec((1, _D_IN_PAD))],
        out_specs=out_specs, name="inproj",
        compiler_params=pltpu.CompilerParams(dimension_semantics=("arbitrary",),
                                             vmem_limit_bytes=VMEM_LIMIT),
    )(x2, w, b)


def _mlstm_kernel(mqk_ref, mv_ref, mo_ref, small_ref, cw_ref, bfc_ref, gh_ref, ltri_ref,
                  ym_ref, ebuf, c_st, m_st, gx, g_bc, g_d, g_a, g_mrow, g_blast, g_amax):
    L = M_CHUNK
    H = M_HEADS
    c = pl.program_id(1)
    n_chunks = gx.shape[0] // 8
    n_sub = mqk_ref.shape[1] // L
    lanes =lax.broadcasted_iota(jnp.int32, gx.shape, 1)

    @pl.when(c == 0)
    def _():
        ebuf[0:8, :] = jnp.zeros((8, 2 * M_WIDTH), F32)
        c_st[...] = jnp.zeros_like(c_st)
        m_st[...] = jnp.zeros_like(m_st)
        for cc in range(n_chunks):
            gx[cc * 8:(cc + 1) * 8, :] = small_ref[0, cc * L:(cc + 1) * L, :].T[0:8, :]
        x = gx[...]
        fg = x + bfc_ref[...]
        lf = jnp.minimum(fg, 0.0) - jnp.log1p(jnp.exp(-jnp.abs(fg)))
        bc = lax.dot_general(lf, ltri_ref[...], (((1,), (1,)), ((), ())),
                             precision=HI, preferred_element_type=F32)
        ig = pltpu.roll(x, H, 0)
        blast = bc[:, L - 1:L]
        a_all = blast - bc + ig
        d = ig - bc
        cm = d
        sh = 1
        while sh < L:
            cm = jnp.maximum(cm, jnp.where(lanes >= sh, pltpu.roll(cm, sh, 1), -jnp.inf))
            sh *= 2
        g_bc[...] = bc
        g_d[...] = d
        g_a[...] = a_all
        g_mrow[...] = bc + cm
        g_blast[...] = jnp.broadcast_to(blast, gx.shape)
        g_amax[...] = jnp.broadcast_to(jnp.max(a_all, axis=-1, keepdims=True), gx.shape)

    row = lax.broadcasted_iota(jnp.int32, (L, L), 0)
    col = lax.broadcasted_iota(jnp.int32, (L, L), 1)
    causal = row >= col
    ones = jnp.ones((L, M_HEAD_DIM), BF16)

    def chunk(j):
        rows = slice(j * L, (j + 1) * L)
        ebuf[8:8 + L, :] = mqk_ref[0, rows, :].astype(F32)
        ext = ebuf[...]
        conv = cw_ref[CONV_WIDTH - 1:CONV_WIDTH, :] * ext[8:, :]
        for k in range(1, CONV_WIDTH):
            conv = conv + cw_ref[CONV_WIDTH - 1 - k:CONV_WIDTH - k, :] * pltpu.roll(ext, k, 0)[8:, :]
        ebuf[0:8, :] = ebuf[L:L + 8, :]
        qk = (conv * jax.nn.sigmoid(conv)).astype(BF16)

        r8 = pl.ds(pl.multiple_of((c * n_sub + j) * 8, 8), 8)
        bc = g_bc[r8, :]
        dt = g_d[r8, :]
        blast = g_blast[r8, :]
        m_prev = m_st[...]
        m_new = jnp.maximum(blast + m_prev, g_amax[r8, :])
        decay = jnp.exp(blast + m_prev - m_new)
        w_row = jnp.exp(g_a[r8, :] - m_new)
        log_inter = bc + m_prev
        m_i = jnp.maximum(g_mrow[r8, :], log_inter)
        k_scale = M_HEAD_DIM ** -0.5
        s_inter = jnp.exp(log_inter - m_i) * k_scale
        emi = jnp.exp(-m_i)
        u = bc - m_i + float(np.log(k_scale))
        cols = jnp.concatenate([u, s_inter, emi, w_row, jnp.zeros((LANES - 32, L), F32)], axis=0).T
        u_all, s_inter_all = cols[:, 0:8], cols[:, 8:16]
        emi_all, w_all = cols[:, 16:24], cols[:, 24:32]

        for h in range(M_HEADS):
            sl = slice(h * M_HEAD_DIM, (h + 1) * M_HEAD_DIM)
            qb = qk[:, sl]
            kb = qk[:, M_WIDTH + h * M_HEAD_DIM:M_WIDTH + (h + 1) * M_HEAD_DIM]
            v = mv_ref[0, rows, sl]
            cn_prev = c_st[h]

            g = H + h
            logp = jnp.where(causal, u_all[:, g:g + 1] + dt[g:g + 1, :], -jnp.inf)
            p = jnp.exp(logp) * _dot_nt(qb, kb)
            nd = (_dot(p.astype(BF16), jnp.concatenate([v, ones], axis=1))
                  + s_inter_all[:, g:g + 1] * _dot(qb, cn_prev.astype(BF16)))
            hh = nd[:, :M_HEAD_DIM] / jnp.maximum(jnp.abs(nd[:, M_HEAD_DIM:]), emi_all[:, g:g + 1])
            hh = hh * lax.rsqrt(jnp.mean(hh * hh, axis=-1, keepdims=True) + RMS_EPS) * gh_ref[:, sl]
            ym_ref[0, rows, sl] = (jax.nn.sigmoid(mo_ref[0, rows, sl].astype(F32)) * hh).astype(ym_ref.dtype)

            w_col = w_all[:, g:g + 1]
            vw = jnp.concatenate([v.astype(F32) * w_col, jnp.broadcast_to(w_col, (L, M_HEAD_DIM))], axis=1)
            c_st[h] = decay[g:g + 1, 0:1] * cn_prev + _dot_tn(kb, vw.astype(BF16))

        m_st[...] = m_new

    for j in range(n_sub):
        chunk(j)


def _mlstm(mqk, mv, mo, small, cw, bf, gh, ltri, n_sub=4):
    B, T, _ = mqk.shape
    L = M_CHUNK
    assert T % (n_sub * L) == 0
    nc = T // L
    assert L == LANES and 8 * nc <= LANES
    blk = lambda w: pl.BlockSpec((1, n_sub * L, w), lambda b, c: (b, c, 0))
    gate = lambda: pltpu.VMEM((8 * nc, L), F32)
    return pl.pallas_call(
        _mlstm_kernel, out_shape=jax.ShapeDtypeStruct((B, T, M_WIDTH), BF16),
        grid=(B, nc // n_sub),
        in_specs=[blk(2 * M_WIDTH), blk(M_WIDTH), blk(M_WIDTH),
                  pl.BlockSpec((1, T, LANES), lambda b, c: (b, 0, 0)),
                  _const_spec((8, 2 * M_WIDTH)), _const_spec((8 * nc, L)),
                  _const_spec((1, M_WIDTH)), _const_spec((L, L))],
        out_specs=blk(M_WIDTH),
        scratch_shapes=[pltpu.VMEM((L + 8, 2 * M_WIDTH), F32),
                        pltpu.VMEM((M_HEADS, M_HEAD_DIM, 2 * M_HEAD_DIM), F32),
                        pltpu.VMEM((8, L), F32),
                        gate(), gate(), gate(), gate(), gate(), gate(), gate()],
        name="mlstm",
        compiler_params=pltpu.CompilerParams(dimension_semantics=("arbitrary", "arbitrary")),
    )(mqk, mv, mo, small, cw, bf, gh, ltri)


def _compress_kernel(x_ref, pe_ref, w1t_ref, w1b_ref, w2_ref, o_ref, *, nch, nbt):
    half = CMP_BLOCK // 2
    rows = nbt * nch
    top = jnp.zeros((rows, 2 * CMP_HIDDEN), F32)
    bot = jnp.zeros((rows, 2 * CMP_HIDDEN), F32)
    for p in range(half):
        xp = jnp.concatenate([x_ref[0, e, pl.ds(p, nch, stride=CMP_STRIDE), :] for e in range(nbt)], axis=0)
        top = top + _dot((xp + pe_ref[0, p:p + 1, :]).astype(BF16), w1t_ref[0, p])
        bot = bot + _dot((xp + pe_ref[0, half + p:half + p + 1, :]).astype(BF16), w1b_ref[0, p])
    hid = top + pltpu.roll(bot, rows - 1, 0)
    act = hid * jax.nn.sigmoid(hid)
    out = _dot(act.astype(BF16), w2_ref[0]).astype(o_ref.dtype)
    for e in range(nbt):
        o_ref[0, e] = out[e * nch:(e + 1) * nch]


def _compress(cin, pe2, w1t, w1b, w2):
    _, B, T, _ = cin.shape
    nch = T // CMP_STRIDE
    nbt = 4 if B % 4 == 0 else 1
    sel = lambda *shape: pl.BlockSpec((1,) + shape, lambda j, b: (j,) + (0,) * len(shape))
    return pl.pallas_call(
        functools.partial(_compress_kernel, nch=nch, nbt=nbt),
        out_shape=jax.ShapeDtypeStruct((2, B, nch, LANES), BF16),
        grid=(2, B // nbt),
        in_specs=[pl.BlockSpec((1, nbt, T, LANES), lambda j, b: (j, b, 0, 0)),
                  sel(CMP_BLOCK, LANES), sel(CMP_BLOCK // 2, LANES, 2 * CMP_HIDDEN),
                  sel(CMP_BLOCK // 2, LANES, 2 * CMP_HIDDEN), sel(2 * CMP_HIDDEN, LANES)],
        out_specs=pl.BlockSpec((1, nbt, nch, LANES), lambda j, b: (j, b, 0, 0)),
        name="compress",
        compiler_params=pltpu.CompilerParams(dimension_semantics=("arbitrary", "arbitrary")),
    )(cin, pe2, w1t, w1b, w2)


def _bias_kernel(tbl_ref, o_ref, *, kind, n_cmp):
    pid = pl.program_id(0)
    toeplitz = kind != "cmp"
    if toeplitz:
        k = lax.broadcasted_iota(jnp.int32, (8, 2 * LANES), 1)
        dist = (pid - 1) * TQ + jnp.where(k < LANES, -k, 2 * LANES - k)
    else:
        al = lax.broadcasted_iota(jnp.int32, (CMP_STRIDE, 2 * LANES), 0)
        m = lax.broadcasted_iota(jnp.int32, (CMP_STRIDE, 2 * LANES), 1)
        dist = pid * TQ - (CMP_BLOCK - 1) + al - CMP_STRIDE * jnp.where(m < LANES, m, m - 2 * LANES)
    n = jnp.maximum(dist, 0)
    cnt = jnp.zeros_like(n)
    for t in _BUCKET_THR:
        cnt = cnt + jnp.where(n >= t, 1, 0)
    bucket = jnp.where(n < REL_BUCKETS // 2, n, REL_BUCKETS // 2 + cnt)
    if kind == "tok":
        madd = jnp.where((dist >= 0) & (pid > 0), 0.0, NEG)
    elif kind == "win":
        madd = jnp.where((dist >= 0) & (dist < WINDOW) & (pid > 0), 0.0, NEG)
    else:
        madd = jnp.where(dist >= 0, 0.0, NEG)
        valid_c = lax.broadcasted_iota(jnp.int32, (TQ, LANES), 1) < n_cmp
    for h in range(N_HEADS):
        val = jnp.zeros(dist.shape, F32)
        for bb in range(REL_BUCKETS):
            val = jnp.where(bucket == bb, tbl_ref[h * REL_BUCKETS + bb], val)
        val = val * LOG2E + madd
        if toeplitz:
            g = jnp.broadcast_to(val[0:1, :], (TQ, 2 * LANES))
            val = pltpu.roll(g, 0, 1, stride=1, stride_axis=0)[:, :LANES]
        else:
            bands = [val[:, :LANES]] + [pltpu.roll(val, ah, 1)[:, :LANES] for ah in range(1, TQ // CMP_STRIDE)]
            val = jnp.where(valid_c, jnp.concatenate(bands, axis=0), NEG)
        o_ref[0, h * TQ:(h + 1) * TQ, :] = val


def _bias_tiles(tbl, n_tiles, kind, n_cmp=0):
    return pl.pallas_call(
        functools.partial(_bias_kernel, kind=kind, n_cmp=n_cmp),
        out_shape=jax.ShapeDtypeStruct((n_tiles, N_HEADS * TQ, LANES), F32),
        grid=(n_tiles,),
        in_specs=[pl.BlockSpec(memory_space=pltpu.SMEM)],
        out_specs=pl.BlockSpec((1, N_HEADS * TQ, LANES), lambda i: (i, 0, 0)),
        name="bias_" + kind,
        compiler_params=pltpu.CompilerParams(dimension_semantics=("arbitrary",)),
    )(tbl)


def _nsa_kernel(nq4_ref, ks_ref, kw_ref, vs0_ref, vs1_ref, vw0_ref, vw1_ref, kc_ref, vc_ref,
                small4_ref, bt_ref, wb_ref, cb4_ref, ovt_ref, et_ref, yn_ref,
                lhs_sc, z_sc, zw_sc, seln_sc, partc_sc, *, nb, n_slc, n_top, n_win, n_chunks_max):
    HR = N_HEADS * TQ
    GR = N_REP * TQ
    CH = 4 * TQ
    G4 = 4
    vs_refs = (vs0_ref, vs1_ref)
    vw_refs = (vw0_ref, vw1_ref)
    lane = lax.broadcasted_iota(jnp.int32, (TQ, LANES), 1)
    lo = lane < N_HEAD_DIM

    def stack_heads(q_all):
        zero = jnp.zeros((TQ, LANES), BF16)
        parts = []
        for g in range(N_KV_GROUPS):
            for r in range(N_REP):
                qr = q_all[:, r * LANES:(r + 1) * LANES]
                parts.append(jnp.where(lo if g == 0 else jnp.logical_not(lo), qr, zero))
        return parts

    def pair(o, r):
        return jnp.where(lo, o[r * TQ:(r + 1) * TQ], o[(N_REP + r) * TQ:(N_REP + r + 1) * TQ])

    def normed(acc, r):
        den = jnp.where(lo, acc[(N_REP + r) * TQ:(N_REP + r + 1) * TQ], acc[r * TQ:(r + 1) * TQ])
        return pair(acc, r) / pltpu.roll(den, N_HEAD_DIM, 1)

    def gate(sg, r, ci):
        c0 = _SMALL_NGATE + r * 3 + ci
        c1 = _SMALL_NGATE + (N_REP + r) * 3 + ci
        return jnp.where(lo, sg[:, c0:c0 + 1], sg[:, c1:c1 + 1])

    def group_prep(s, all_selected, qi):
        qs4 = jnp.concatenate(
            [p for j in range(G4) for p in stack_heads(nq4_ref[s, j * TQ:(j + 1) * TQ, :])], axis=0)
        rows = lax.broadcasted_iota(jnp.int32, (G4 * HR, 1), 0)
        t_rows = (qi + jnp.right_shift(rows, 10)) * TQ + (rows & (TQ - 1))

        z = _dot_nt(qs4, kc_ref[0, s]) + cb4_ref[...].reshape(G4 * HR, LANES)
        e = jnp.exp2(z - jnp.max(z, axis=-1, keepdims=True))
        l = _dot(e.astype(BF16), jnp.ones((LANES, LANES), BF16))
        p_c = e * jnp.where(t_rows >= CMP_BLOCK - 1, 1.0 / jnp.maximum(l, 1e-30), 0.0)
        o_c = _dot(p_c.astype(BF16), vc_ref[0, s])
        sg4 = jax.nn.sigmoid(small4_ref[s])
        for j in range(G4):
            for r in range(N_REP):
                partc_sc[s, j, r] = (gate(sg4[j * TQ:(j + 1) * TQ], r, 0)
                                     * pair(o_c[j * HR:(j + 1) * HR], r))

        W = G4 * N_KV_GROUPS * TQ
        jb = lax.broadcasted_iota(jnp.int32, (n_slc, W), 0)
        col = lax.broadcasted_iota(jnp.int32, (n_slc, W), 1)
        tq = (qi + jnp.right_shift(col, 8)) * TQ + (col & (TQ - 1))
        cur = jnp.right_shift(tq, 6)
        elig = jb <= cur
        if all_selected:
            sel = jnp.where(elig, 1.0, 0.0)
        else:
            forced = (jb == 0) | (jb == cur) | (jb == cur - 1)
            psums = []
            for j in range(G4):
                for g in range(N_KV_GROUPS):
                    base = j * HR + g * GR
                    ps = p_c[base:base + TQ]
                    for r in range(1, N_REP):
                        ps = ps + p_c[base + r * TQ:base + (r + 1) * TQ]
                    psums.append(ps)
            imp = lax.dot_general(ovt_ref[...], jnp.concatenate(psums, axis=0), (((1,), (1,)), ((), ())),
                                  precision=HI, preferred_element_type=F32)
            score = jnp.where(elig, jnp.where(forced, BIG, imp), -BIG)
            cnt = jnp.zeros((n_slc, W), F32)
            for i in range(n_slc):
                si = score[i:i + 1, :]
                tie = jnp.where(jb > i, 1.0, 0.0)
                cnt = cnt + jnp.where(si > score, 1.0, jnp.where(si == score, tie, 0.0))
            sel = jnp.where((cnt < n_top) & (score > -BIG / 2), 1.0, 0.0)
        seln = jnp.concatenate([sel - 1.0, jnp.zeros((LANES - n_slc, W), F32)], axis=0).T.astype(BF16)
        for j in range(G4):
            for g in range(N_KV_GROUPS):
                seln_sc[s, j, g] = seln[(j * N_KV_GROUPS + g) * TQ:(j * N_KV_GROUPS + g + 1) * TQ]

    def tiles(n_pairs, qi, jq):
        qrows = pl.ds(pl.multiple_of(jq * TQ, TQ), TQ)
        w0 = jnp.maximum(qi - (n_win - 1), 0)
        woff = pl.multiple_of(w0 * TQ, TQ)
        elems = range(nb)
        cat = lambda xs: xs[0] if len(xs) == 1 else jnp.concatenate(xs, axis=0)
        rep = lambda t: t if nb == 1 else jnp.concatenate([t] * nb, axis=0)
        parts = [stack_heads(nq4_ref[s, qrows, :]) for s in elems]
        qs = [jnp.concatenate(parts[s], axis=0) for s in elems]

        def bias_tiles(ref, first, count, tile0):
            tiles_ = [ref[jnp.maximum(qi - tile0 - (first + u) + 1, 0)] for u in range(count)]
            return rep(tiles_[0] if count == 1 else jnp.concatenate(tiles_, axis=1))

        zmax = None
        for j0 in range(0, n_win, 2):
            nj = min(2, n_win - j0)
            zp = (cat([_dot_nt(qs[s], kw_ref[s, pl.ds(woff + j0 * TQ, nj * TQ), :]) for s in elems])
                  + bias_tiles(wb_ref, j0, nj, w0))
            zw_sc[:, j0 * TQ:(j0 + nj) * TQ] = zp
            for u in range(nj):
                zj = zp[:, u * TQ:(u + 1) * TQ]
                zmax = zj if zmax is None else jnp.maximum(zmax, zj)
        mw = jnp.max(zmax, axis=-1, keepdims=True)
        pw = jnp.exp2(zw_sc[...] - mw).astype(BF16)
        acc_w = cat([_dot(pw[s * HR + g * GR:s * HR + (g + 1) * GR],
                          vw_refs[g][s, pl.ds(woff, n_win * TQ), :])
                     for s in elems for g in range(N_KV_GROUPS)])

        for s in elems:
            for g in range(N_KV_GROUPS):
                seln = seln_sc[s, jq, g]
                for r in range(N_REP):
                    h = g * N_REP + r
                    lhs_sc[s, h * TQ:(h + 1) * TQ, :] = jnp.concatenate([parts[s][h], seln], axis=1)

        zmax = None
        for pi in range(n_pairs):
            k0 = pi * 2 * TQ
            zp = (cat([_dot_nt(lhs_sc[s], jnp.concatenate([ks_ref[s, k0:k0 + 2 * TQ, :],
                                                           et_ref[k0:k0 + 2 * TQ, :]], axis=1))
                       for s in elems])
                  + bias_tiles(bt_ref, 2 * pi, 2, 0))
            z_sc[pi // 2, :, (pi % 2) * 2 * TQ:(pi % 2 + 1) * 2 * TQ] = zp
            for u in range(2):
                zt = zp[:, u * TQ:(u + 1) * TQ]
                zmax = zt if zmax is None else jnp.maximum(zmax, zt)
        ms = jnp.max(zmax, axis=-1, keepdims=True)
        acc = [None] * (nb * N_KV_GROUPS)
        for c in range((n_pairs + 1) // 2):
            width = min(CH, n_pairs * 2 * TQ - c * CH)
            p = jnp.exp2(z_sc[c, :, 0:width] - ms).astype(BF16)
            for s in elems:
                for g in range(N_KV_GROUPS):
                    i = s * N_KV_GROUPS + g
                    d = _dot(p[i * GR:(i + 1) * GR], vs_refs[g][s, c * CH:c * CH + width, :])
                    acc[i] = d if acc[i] is None else acc[i] + d
        acc_s = cat(acc)

        for s in elems:
            sg = jax.nn.sigmoid(small4_ref[s, qrows, :])
            a_s = acc_s[s * HR:(s + 1) * HR]
            a_w = acc_w[s * HR:(s + 1) * HR]
            for r in range(N_REP):
                out = (partc_sc[s, jq, r] + gate(sg, r, 1) * normed(a_s, r)
                       + gate(sg, r, 2) * normed(a_w, r))
                yn_ref[s, qrows, r * LANES:(r + 1) * LANES] = out.astype(yn_ref.dtype)

    gi = pl.program_id(1)
    n_short = min(n_chunks_max, n_top * SEL_BLOCK // CH)

    @pl.when(gi < n_short)
    def _():
        for s in range(nb):
            group_prep(s, True, gi * G4)

    @pl.when(gi >= n_short)
    def _():
        for s in range(nb):
            group_prep(s, False, gi * G4)

    for nc in range(1, n_chunks_max + 1):
        @pl.when(gi == nc - 1)
        def _(nc=nc):
            q0 = G4 * (nc - 1)

            def body(jq, carry):
                tiles(2 * nc, q0 + jq, jq)
                return carry
            lax.fori_loop(0, G4, body, 0)


def _nsa(nq, nsw, ckv, small, bt, wb, cb, ov, emat):
    B, T, _ = nq.shape
    assert T == 2048, "single 128-wide compressed-key tile assumes T == 2048"
    nqt = T // TQ
    n_cmp = (T - CMP_BLOCK) // CMP_STRIDE + 1
    n_slc = T // SEL_BLOCK
    n_top = min(SEL_TOPK, n_slc)
    n_win = wb.shape[0] - 1
    HR = N_HEADS * TQ
    assert nqt % 4 == 0 and n_slc % 8 == 0 and n_slc <= LANES
    nb = NSA_BATCH if B % NSA_BATCH == 0 else 1
    n_chunks_max = nqt // 4
    kv = lambda j: pl.BlockSpec((nb, T, LANES), lambda b, g: (b, 0, j))
    ck = lambda j: pl.BlockSpec((1, nb, T // CMP_STRIDE, LANES), lambda b, g: (j, b, 0, 0))
    kern = functools.partial(_nsa_kernel, nb=nb, n_slc=n_slc, n_top=n_top, n_win=n_win,
                             n_chunks_max=n_chunks_max)
    return pl.pallas_call(
        kern, out_shape=jax.ShapeDtypeStruct((B, T, N_WIDTH), BF16),
        grid=(B // nb, n_chunks_max),
        in_specs=[pl.BlockSpec((nb, 4 * TQ, N_WIDTH), lambda b, g: (b, g, 0)),
                  kv(0), kv(1), kv(2), kv(3), kv(4), kv(5), ck(0), ck(1),
                  pl.BlockSpec((nb, 4 * TQ, LANES), lambda b, g: (b, g, 0)),
                  _const_spec(bt.shape), _const_spec(wb.shape),
                  pl.BlockSpec((4, HR, LANES), lambda b, g: (g, 0, 0)),
                  _const_spec(ov.shape), _const_spec(emat.shape)],
        out_specs=pl.BlockSpec((nb, 4 * TQ, N_WIDTH), lambda b, g: (b, g, 0)),
        scratch_shapes=[pltpu.VMEM((nb, HR, 2 * LANES), BF16),
                        pltpu.VMEM((n_chunks_max, nb * HR, 4 * TQ), F32),
                        pltpu.VMEM((nb * HR, n_win * TQ), F32),
                        pltpu.VMEM((nb, 4, N_KV_GROUPS, TQ, LANES), BF16),
                        pltpu.VMEM((nb, 4, N_REP, TQ, LANES), F32)],
        name="nsa",
        compiler_params=pltpu.CompilerParams(dimension_semantics=("arbitrary", "arbitrary"),
                                             vmem_limit_bytes=VMEM_LIMIT),
    )(nq, nsw, nsw, nsw, nsw, nsw, nsw, ckv, ckv, small, bt, wb, cb, ov, emat)


def _merge_kernel(x_ref, ym_ref, yn_ref, mg_ref, wbm_ref, wbn_ref, wo_ref,
                  wg_ref, wu_ref, wd_ref, gfin_ref, o_ref, *, tf):
    bm = _dot(ym_ref[...], wbm_ref[...])
    bn = _dot(yn_ref[...], wbn_ref[...])
    mixed = (jax.nn.sigmoid(mg_ref[:, :D_MODEL].astype(F32)) * bm
             + jax.nn.sigmoid(mg_ref[:, D_MODEL:].astype(F32)) * bn)
    h = x_ref[...] + _dot(mixed.astype(BF16), wo_ref[...])
    rs = lax.rsqrt(jnp.mean(h * h, axis=-1, keepdims=True) + RMS_EPS)
    hb = h.astype(BF16)
    acc = jnp.zeros(h.shape, F32)
    for j in range(D_FF // tf):
        gg = _dot(hb, wg_ref[:, j * tf:(j + 1) * tf]) * rs
        uu = _dot(hb, wu_ref[:, j * tf:(j + 1) * tf]) * rs
        act = (gg * jax.nn.sigmoid(gg) * uu).astype(BF16)
        acc = acc + _dot(act, wd_ref[j * tf:(j + 1) * tf, :])
    h2 = h + acc
    o_ref[...] = h2 * lax.rsqrt(jnp.mean(h2 * h2, axis=-1, keepdims=True) + RMS_EPS) * gfin_ref[...]


def _merge(x2, ym, yn, mg, wbm, wbn, wo, wg, wu, wd, gfin, tm=512, tf=256):
    n = x2.shape[0]
    assert n % tm == 0 and D_FF % tf == 0
    row = lambda width: pl.BlockSpec((tm, width), lambda i: (i, 0))
    return pl.pallas_call(
        functools.partial(_merge_kernel, tf=tf),
        out_shape=jax.ShapeDtypeStruct((n, D_MODEL), F32), grid=(n // tm,),
        in_specs=[row(D_MODEL), row(M_WIDTH), row(N_WIDTH), row(N_BRANCH * D_MODEL),
                  _const_spec(wbm.shape), _const_spec(wbn.shape), _const_spec(wo.shape),
                  _const_spec(wg.shape), _const_spec(wu.shape),
                  _const_spec(wd.shape), _const_spec(gfin.shape)],
        out_specs=row(D_MODEL), name="merge_ffn",
        compiler_params=pltpu.CompilerParams(dimension_semantics=("arbitrary",),
                                             vmem_limit_bytes=VMEM_LIMIT),
    )(x2, ym, yn, mg, wbm, wbn, wo, wg, wu, wd, gfin)


def _nsa_constants(T):
    n_cmp = (T - CMP_BLOCK) // CMP_STRIDE + 1
    n_slc = T // SEL_BLOCK
    cs = np.arange(n_cmp) * CMP_STRIDE
    ss = np.arange(n_slc) * SEL_BLOCK
    ov = np.clip(np.minimum(cs[:, None] + CMP_BLOCK, ss[None, :] + SEL_BLOCK)
                 - np.maximum(cs[:, None], ss[None, :]), 0, None) / CMP_STRIDE
    ovt = np.zeros((n_slc, LANES), np.float32)
    ovt[:, :n_cmp] = ov.T
    et = (np.arange(T)[:, None] // SEL_BLOCK == np.arange(LANES)[None, :]).astype(np.float32) * (-NEG)
    return jnp.asarray(ovt), jnp.asarray(et, dtype=BF16)


def _compress_weights(pe_cmp, w_cmp1, w_cmp2):
    half = CMP_BLOCK // 2
    assert N_KV_GROUPS == 2

    def blockdiag(w):
        z = jnp.zeros_like(w)
        return jnp.concatenate([jnp.concatenate([w, z], axis=-1), jnp.concatenate([z, w], axis=-1)], axis=-2)

    w1bd = blockdiag(w_cmp1.reshape(2, CMP_BLOCK, N_HEAD_DIM, CMP_HIDDEN).astype(BF16))
    w2bd = blockdiag(w_cmp2)
    pe2 = jnp.tile(pe_cmp, (1, 1, N_KV_GROUPS))
    return pe2, w1bd[:, :half].astype(BF16), w1bd[:, half:].astype(BF16), w2bd.astype(BF16)


def kernel(x, g_norm_mix, w_in, b_in, b_fgate, conv_qk, g_mlstm_head, pe_cmp, w_cmp1, w_cmp2,
           rel_bias, w_branch, w_out, g_norm_ffn, w_gate, w_up, w_down, g_final):
    B, T, D = x.shape
    assert D == D_MODEL and w_in.shape[0] == 1, "one residual block (DEPTH == 1)"
    N = B * T
    x2 = x.reshape(N, D)

    idx, scale = _inproj_perm()
    w_r = _wprep_t(w_in[0].T, g_norm_mix[0].reshape(1, D), idx, scale)
    b_r = _gather_cols(b_in[0].reshape(1, -1), idx, scale, F32)
    cw = jnp.zeros((8, 2 * M_WIDTH), F32).at[:CONV_WIDTH].set(conv_qk[0])
    mqk, mv, mo, nq, cin, nsw, mg, small = _inproj(x2, w_r, b_r)

    bf8 = jnp.concatenate([jnp.zeros((M_HEADS,), F32), b_fgate[0].astype(F32)])
    bfc = jnp.broadcast_to(jnp.tile(bf8, T // M_CHUNK)[:, None], (8 * (T // M_CHUNK), M_CHUNK))
    ltri = jnp.asarray(np.tril(np.ones((M_CHUNK, M_CHUNK), np.float32)))
    r3 = lambda a: a.reshape(B, T, a.shape[-1])
    ym = _mlstm(r3(mqk), r3(mv), r3(mo), r3(small), cw, bfc, g_mlstm_head[0].reshape(1, M_WIDTH), ltri)

    pe2, w1t, w1b, w2 = _compress_weights(pe_cmp[0], w_cmp1[0], w_cmp2[0])
    ckv = _compress(cin.reshape(2, B, T, LANES), pe2, w1t, w1b, w2)
    tbl = rel_bias.astype(F32).T.reshape(-1)
    nqt = T // TQ
    bt = _bias_tiles(tbl, nqt + 1, "tok")
    wb = _bias_tiles(tbl, min(WINDOW // TQ + 1, nqt) + 1, "win")
    cb = _bias_tiles(tbl, nqt, "cmp", n_cmp=(T - CMP_BLOCK) // CMP_STRIDE + 1)
    ov, emat = _nsa_constants(T)
    yn = _nsa(r3(nq), r3(nsw), ckv, r3(small), bt, wb, cb, ov, emat)

    wbm = w_branch[0, 0].astype(BF16)
    wbn = jnp.concatenate(
        [w_branch[0, 1, (g * N_REP + r) * N_HEAD_DIM:(g * N_REP + r + 1) * N_HEAD_DIM]
         for r in range(N_REP) for g in range(N_KV_GROUPS)], axis=0).astype(BF16)
    ident = lambda n: (np.arange(n, dtype=np.int32), np.ones((n,), np.float32))
    g_ffn = g_norm_ffn[0].reshape(D, 1)
    out = _merge(x2, ym.reshape(N, M_WIDTH), yn.reshape(N, N_WIDTH), mg, wbm, wbn,
                 w_out[0].astype(BF16), _wprep(w_gate[0], g_ffn, *ident(D_FF)),
                 _wprep(w_up[0], g_ffn, *ident(D_FF)),
                 _wprep(w_down[0], jnp.ones((D_FF, 1), F32), *ident(D)),
                 g_final.reshape(1, D))
    return out.reshape(B, T, D)
```

```python
import functools

import numpy as np
import jax
import jax.numpy as jnp
from jax import lax
from jax.experimental import pallas as pl
from jax.experimental.pallas import tpu as pltpu

F32 = jnp.float32
BF16 = jnp.bfloat16
HI = lax.Precision.HIGHEST

D_MODEL = 1024
M_HEADS = 4
M_HEAD_DIM = 128
M_WIDTH = M_HEADS * M_HEAD_DIM
M_CHUNK = 128
CONV_WIDTH = 4
N_HEADS = 8
N_KV_GROUPS = 2
N_REP = N_HEADS // N_KV_GROUPS
N_HEAD_DIM = 64
N_WIDTH = N_HEADS * N_HEAD_DIM
N_KV_WIDTH = N_KV_GROUPS * N_HEAD_DIM
CMP_BLOCK = 32
CMP_STRIDE = 16
CMP_HIDDEN = 2 * N_HEAD_DIM
SEL_BLOCK = 64
SEL_TOPK = 16
WINDOW = 512
REL_BUCKETS = 32
REL_MAX_DIST = 1024
N_BRANCH = 2
D_FF = 2816
RMS_EPS = 1e-6
BIG = 1e9
NEG = -1e30

LANES = 128
TQ = 128
NSA_BATCH = 1
VMEM_LIMIT = 56 * 1024 * 1024

_OFF_MQ, _OFF_MK, _OFF_MV, _OFF_MO = 0, 512, 1024, 1536
_OFF_MI, _OFF_MF, _OFF_NQ, _OFF_NKV = 2048, 2052, 2056, 2568
_OFF_NGATE, _OFF_MERGE, _D_IN = 3336, 3360, 5408
_SMALL_NGATE = 8

_SEGS = (("mqk", 1024), ("mv", 512), ("mo", 512), ("nq", 512), ("kc", 128),
         ("vc", 128), ("nsw", 512), ("mg", 2048), ("small", 128))
_D_IN_PAD = sum(w for _, w in _SEGS)
LOG2E = 1.4426950408889634


def _dot(a, b, **kw):
    return jnp.dot(a, b, preferred_element_type=F32, **kw)


def _dot_nt(a, b):
    return lax.dot_general(a, b, (((1,), (1,)), ((), ())), preferred_element_type=F32)


def _dot_tn(a, b):
    return lax.dot_general(a, b, (((0,), (0,)), ((), ())), preferred_element_type=F32)


def _const_spec(shape):
    nd = len(shape)
    return pl.BlockSpec(shape, lambda *_: (0,) * nd, pipeline_mode=pl.Buffered(1))


def _bucket_thresholds():
    max_exact = REL_BUCKETS // 2
    assert REL_MAX_DIST == 64 * max_exact and REL_BUCKETS - max_exact == 16
    thr = []
    for k in range(1, REL_BUCKETS - max_exact):
        t = max_exact
        while t ** 8 < (max_exact ** 8) * (2 ** (3 * k)):
            t += 1
        thr.append(t)
    return tuple(thr)


_BUCKET_THR = _bucket_thresholds()


def _inproj_perm():
    idx = np.zeros((_D_IN_PAD,), np.int32)
    scale = np.zeros((_D_IN_PAD,), np.float32)
    pos = 0

    def put(cols, s=1.0):
        nonlocal pos
        n = len(cols)
        idx[pos:pos + n] = cols
        scale[pos:pos + n] = s
        pos += n

    put(np.arange(_OFF_MQ, _OFF_MV))
    put(np.arange(_OFF_MV, _OFF_MO))
    put(np.arange(_OFF_MO, _OFF_MI))
    nq = np.zeros((N_WIDTH,), np.int32)
    for r in range(N_REP):
        for g in range(N_KV_GROUPS):
            for d in range(N_HEAD_DIM):
                nq[r * 128 + g * 64 + d] = _OFF_NQ + (g * N_REP + r) * N_HEAD_DIM + d
    put(nq, N_HEAD_DIM ** -0.5 * LOG2E)
    kv = lambda j, g: np.arange(_OFF_NKV + (j * N_KV_GROUPS + g) * N_HEAD_DIM,
                                _OFF_NKV + (j * N_KV_GROUPS + g + 1) * N_HEAD_DIM)
    put(np.arange(_OFF_NKV, _OFF_NKV + 256))
    put(np.concatenate([kv(2, 0), kv(2, 1)]))
    put(np.concatenate([kv(4, 0), kv(4, 1)]))
    put(np.concatenate([kv(3, 0), kv(3, 1)]))
    put(np.concatenate([kv(5, 0), kv(5, 1)]))
    put(np.arange(_OFF_MERGE, _D_IN))
    put(np.arange(_OFF_MI, _OFF_MI + 8))
    put(np.arange(_OFF_NGATE, _OFF_NGATE + 24))
    pos += LANES - 32
    assert pos == _D_IN_PAD
    return idx, scale


def _gather_cols(a, idx, scale, dtype):
    pieces = []
    start = 0
    n = len(idx)
    for c in range(1, n + 1):
        same = c < n and scale[c] == scale[start] and (scale[c] == 0.0 or idx[c] == idx[c - 1] + 1)
        if not same:
            if scale[start] == 0.0:
                piece = jnp.zeros((a.shape[0], c - start), dtype)
            else:
                piece = a[:, int(idx[start]):int(idx[start]) + (c - start)]
                if scale[start] != 1.0:
                    piece = piece * float(scale[start])
            pieces.append(piece.astype(dtype))
            start = c
    return jnp.concatenate(pieces, axis=1)


def _col_runs(idx, scale):
    runs = []
    start = 0
    n = len(idx)
    for c in range(1, n + 1):
        same = c < n and scale[c] == scale[start] and (scale[c] == 0.0 or idx[c] == idx[c - 1] + 1)
        if not same:
            runs.append((start, int(idx[start]), c - start, float(scale[start])))
            start = c
    return runs


def _wprep_kernel(w_ref, g_ref, o_ref, *, runs):
    g = g_ref[...]
    for dst, src, n, scale in runs:
        if scale == 0.0:
            o_ref[:, dst:dst + n] = jnp.zeros((o_ref.shape[0], n), o_ref.dtype)
        else:
            o_ref[:, dst:dst + n] = (w_ref[:, src:src + n] * (g * scale)).astype(o_ref.dtype)


def _wprep_t_kernel(wt_ref, g_ref, o_ref, *, runs):
    g = g_ref[...]
    for dst, src, n, scale in runs:
        if scale == 0.0:
            o_ref[dst:dst + n, :] = jnp.zeros((n, o_ref.shape[1]), o_ref.dtype)
        else:
            o_ref[dst:dst + n, :] = (wt_ref[src:src + n, :] * (g * scale)).astype(o_ref.dtype)


def _wprep_t(wt, g, idx, scale, tc=256):
    rows, cols = wt.shape
    assert cols % tc == 0
    return pl.pallas_call(
        functools.partial(_wprep_t_kernel, runs=_col_runs(idx, scale)),
        out_shape=jax.ShapeDtypeStruct((len(idx), cols), BF16), grid=(cols // tc,),
        in_specs=[pl.BlockSpec((rows, tc), lambda i: (0, i)), pl.BlockSpec((1, tc), lambda i: (0, i))],
        out_specs=pl.BlockSpec((len(idx), tc), lambda i: (0, i)), name="wprep_t",
        compiler_params=pltpu.CompilerParams(dimension_semantics=("arbitrary",),
                                             vmem_limit_bytes=VMEM_LIMIT),
    )(wt, g)


def _wprep(w, g, idx, scale, tr=256):
    rows, cols = w.shape
    assert rows % tr == 0
    return pl.pallas_call(
        functools.partial(_wprep_kernel, runs=_col_runs(idx, scale)),
        out_shape=jax.ShapeDtypeStruct((rows, len(idx)), BF16), grid=(rows // tr,),
        in_specs=[pl.BlockSpec((tr, cols), lambda i: (i, 0)), pl.BlockSpec((tr, 1), lambda i: (i, 0))],
        out_specs=pl.BlockSpec((tr, len(idx)), lambda i: (i, 0)), name="wprep",
        compiler_params=pltpu.CompilerParams(dimension_semantics=("arbitrary",),
                                             vmem_limit_bytes=VMEM_LIMIT),
    )(w, g)


def _inproj_kernel(x_ref, w_ref, b_ref, mqk_ref, mv_ref, mo_ref, nq_ref,
                   cin_ref, nsw_ref, mg_ref, small_ref):
    x = x_ref[...]
    rs = lax.rsqrt(jnp.mean(x * x, axis=-1, keepdims=True) + RMS_EPS)
    xb = x.astype(BF16)

    def seg(a, n):
        return _dot_nt(xb, w_ref[a:a + n, :]) * rs + b_ref[:, a:a + n]

    off = {}
    pos = 0
    for name, w in _SEGS:
        off[name] = pos
        pos += w

    mqk_ref[...] = seg(off["mqk"], 1024).astype(BF16)
    mv_ref[...] = seg(off["mv"], 512).astype(BF16)
    mo_ref[...] = seg(off["mo"], 512).astype(BF16)
    nq_ref[...] = seg(off["nq"], 512).astype(BF16)
    cc = seg(off["kc"], 2 * LANES)
    cin_ref[0] = cc[:, :LANES]
    cin_ref[1] = cc[:, LANES:]
    nsw_ref[:, 0:2 * LANES] = seg(off["nsw"], 2 * LANES).astype(BF16)
    vv = seg(off["nsw"] + 2 * LANES, 2 * LANES)
    lo = lax.broadcasted_iota(jnp.int32, (x.shape[0], LANES), 1) < N_HEAD_DIM
    for j in range(2):
        vj = vv[:, j * LANES:(j + 1) * LANES]
        nsw_ref[:, (2 + 2 * j) * LANES:(3 + 2 * j) * LANES] = jnp.where(lo, vj, 1.0).astype(BF16)
        nsw_ref[:, (3 + 2 * j) * LANES:(4 + 2 * j) * LANES] = jnp.where(lo, 1.0, vj).astype(BF16)
    mg_ref[...] = seg(off["mg"], 2048).astype(BF16)
    small_ref[...] = seg(off["small"], 128)


def _inproj(x2, w, b, tm=1024):
    n = x2.shape[0]
    assert n % tm == 0
    row = lambda width: pl.BlockSpec((tm, width), lambda i: (i, 0))
    out_shape = (
        jax.ShapeDtypeStruct((n, 1024), BF16), jax.ShapeDtypeStruct((n, 512), BF16),
        jax.ShapeDtypeStruct((n, 512), BF16), jax.ShapeDtypeStruct((n, 512), BF16),
        jax.ShapeDtypeStruct((2, n, 128), F32), jax.ShapeDtypeStruct((n, 768), BF16),
        jax.ShapeDtypeStruct((n, 2048), BF16), jax.ShapeDtypeStruct((n, 128), F32))
    out_specs = (row(1024), row(512), row(512), row(512),
                 pl.BlockSpec((2, tm, 128), lambda i: (0, i, 0)), row(768), row(2048), row(128))
    return pl.pallas_call(
        _inproj_kernel, out_shape=out_shape, grid=(n // tm,),
        in_specs=[row(D_MODEL), _const_spec((_D_IN_PAD, D_MODEL)), _const_spec((1, _D_IN_PAD))],
        out_specs=out_specs, name="inproj",
        compiler_params=pltpu.CompilerParams(dimension_semantics=("arbitrary",),
                                             vmem_limit_bytes=VMEM_LIMIT),
    )(x2, w, b)


def _mlstm_kernel(mqk_ref, mv_ref, mo_ref, small_ref, cw_ref, bfc_ref, gh_ref, ltri_ref,
                  ym_ref, ebuf, c_st, m_st, gx, g_bc, g_d, g_a, g_mrow, g_blast, g_amax, *, n_unroll):
    L = M_CHUNK
    H = M_HEADS
    c = pl.program_id(1)
    n_chunks = gx.shape[0] // 8
    n_sub = mqk_ref.shape[1] // L
    lanes =lax.broadcasted_iota(jnp.int32, gx.shape, 1)

    @pl.when(c == 0)
    def _():
        ebuf[0:8, :] = jnp.zeros((8, 2 * M_WIDTH), F32)
        c_st[...] = jnp.zeros_like(c_st)
        m_st[...] = jnp.zeros_like(m_st)
        for cc in range(n_chunks):
            gx[cc * 8:(cc + 1) * 8, :] = small_ref[0, cc * L:(cc + 1) * L, :].T[0:8, :]
        x = gx[...]
        fg = x + bfc_ref[...]
        lf = jnp.minimum(fg, 0.0) - jnp.log1p(jnp.exp(-jnp.abs(fg)))
        bc = lax.dot_general(lf, ltri_ref[...], (((1,), (1,)), ((), ())),
                             precision=HI, preferred_element_type=F32)
        ig = pltpu.roll(x, H, 0)
        blast = bc[:, L - 1:L]
        a_all = blast - bc + ig
        d = ig - bc
        cm = d
        sh = 1
        while sh < L:
            cm = jnp.maximum(cm, jnp.where(lanes >= sh, pltpu.roll(cm, sh, 1), -jnp.inf))
            sh *= 2
        g_bc[...] = bc
        g_d[...] = d
        g_a[...] = a_all
        g_mrow[...] = bc + cm
        g_blast[...] = jnp.broadcast_to(blast, gx.shape)
        g_amax[...] = jnp.broadcast_to(jnp.max(a_all, axis=-1, keepdims=True), gx.shape)

    row = lax.broadcasted_iota(jnp.int32, (L, L), 0)
    col = lax.broadcasted_iota(jnp.int32, (L, L), 1)
    causal = row >= col
    ones = jnp.ones((L, M_HEAD_DIM), BF16)

    def chunk(j):
        rows = pl.ds(pl.multiple_of(j * L, L), L)
        ebuf[8:8 + L, :] = mqk_ref[0, rows, :].astype(F32)
        ext = ebuf[...]
        conv = cw_ref[CONV_WIDTH - 1:CONV_WIDTH, :] * ext[8:, :]
        for k in range(1, CONV_WIDTH):
            conv = conv + cw_ref[CONV_WIDTH - 1 - k:CONV_WIDTH - k, :] * pltpu.roll(ext, k, 0)[8:, :]
        ebuf[0:8, :] = ebuf[L:L + 8, :]
        qk = (conv * jax.nn.sigmoid(conv)).astype(BF16)

        r8 = pl.ds(pl.multiple_of((c * n_sub + j) * 8, 8), 8)
        bc = g_bc[r8, :]
        dt = g_d[r8, :]
        blast = g_blast[r8, :]
        m_prev = m_st[...]
        m_new = jnp.maximum(blast + m_prev, g_amax[r8, :])
        decay = jnp.exp(blast + m_prev - m_new)
        w_row = jnp.exp(g_a[r8, :] - m_new)
        log_inter = bc + m_prev
        m_i = jnp.maximum(g_mrow[r8, :], log_inter)
        k_scale = M_HEAD_DIM ** -0.5
        s_inter = jnp.exp(log_inter - m_i) * k_scale
        emi = jnp.exp(-m_i)
        u = bc - m_i + float(np.log(k_scale))
        cols = jnp.concatenate([u, s_inter, emi, w_row, jnp.zeros((LANES - 32, L), F32)], axis=0).T
        u_all, s_inter_all = cols[:, 0:8], cols[:, 8:16]
        emi_all, w_all = cols[:, 16:24], cols[:, 24:32]

        for h in range(M_HEADS):
            sl = slice(h * M_HEAD_DIM, (h + 1) * M_HEAD_DIM)
            qb = qk[:, sl]
            kb = qk[:, M_WIDTH + h * M_HEAD_DIM:M_WIDTH + (h + 1) * M_HEAD_DIM]
            v = mv_ref[0, rows, sl]
            cn_prev = c_st[h]

            g = H + h
            logp = jnp.where(causal, u_all[:, g:g + 1] + dt[g:g + 1, :], -jnp.inf)
            p = jnp.exp(logp) * _dot_nt(qb, kb)
            nd = (_dot(p.astype(BF16), jnp.concatenate([v, ones], axis=1))
                  + s_inter_all[:, g:g + 1] * _dot(qb, cn_prev.astype(BF16)))
            hh = nd[:, :M_HEAD_DIM] / jnp.maximum(jnp.abs(nd[:, M_HEAD_DIM:]), emi_all[:, g:g + 1])
            hh = hh * lax.rsqrt(jnp.mean(hh * hh, axis=-1, keepdims=True) + RMS_EPS) * gh_ref[:, sl]
            ym_ref[0, rows, sl] = (jax.nn.sigmoid(mo_ref[0, rows, sl].astype(F32)) * hh).astype(ym_ref.dtype)

            w_col = w_all[:, g:g + 1]
            vw = jnp.concatenate([v.astype(F32) * w_col, jnp.broadcast_to(w_col, (L, M_HEAD_DIM))], axis=1)
            c_st[h] = decay[g:g + 1, 0:1] * cn_prev + _dot_tn(kb, vw.astype(BF16))

        m_st[...] = m_new

    def body(jo, carry):
        for ji in range(n_unroll):
            chunk(jo * n_unroll + ji)
        return carry
    lax.fori_loop(0, n_sub // n_unroll, body, 0)


def _mlstm(mqk, mv, mo, small, cw, bf, gh, ltri, n_sub=8, n_unroll=4):
    B, T, _ = mqk.shape
    L = M_CHUNK
    assert T % (n_sub * L) == 0 and n_sub % n_unroll == 0
    nc = T // L
    assert L == LANES and 8 * nc <= LANES
    blk = lambda w: pl.BlockSpec((1, n_sub * L, w), lambda b, c: (b, c, 0))
    gate = lambda: pltpu.VMEM((8 * nc, L), F32)
    return pl.pallas_call(
        functools.partial(_mlstm_kernel, n_unroll=n_unroll),
        out_shape=jax.ShapeDtypeStruct((B, T, M_WIDTH), BF16),
        grid=(B, nc // n_sub),
        in_specs=[blk(2 * M_WIDTH), blk(M_WIDTH), blk(M_WIDTH),
                  pl.BlockSpec((1, T, LANES), lambda b, c: (b, 0, 0)),
                  _const_spec((8, 2 * M_WIDTH)), _const_spec((8 * nc, L)),
                  _const_spec((1, M_WIDTH)), _const_spec((L, L))],
        out_specs=blk(M_WIDTH),
        scratch_shapes=[pltpu.VMEM((L + 8, 2 * M_WIDTH), F32),
                        pltpu.VMEM((M_HEADS, M_HEAD_DIM, 2 * M_HEAD_DIM), F32),
                        pltpu.VMEM((8, L), F32),
                        gate(), gate(), gate(), gate(), gate(), gate(), gate()],
        name="mlstm",
        compiler_params=pltpu.CompilerParams(dimension_semantics=("arbitrary", "arbitrary")),
    )(mqk, mv, mo, small, cw, bf, gh, ltri)


def _compress_kernel(x_ref, pe_ref, w1t_ref, w1b_ref, w2_ref, o_ref, *, nch, nbt):
    half = CMP_BLOCK // 2
    rows = nbt * nch
    top = jnp.zeros((rows, 2 * CMP_HIDDEN), F32)
    bot = jnp.zeros((rows, 2 * CMP_HIDDEN), F32)
    for p in range(half):
        xp = jnp.concatenate([x_ref[0, e, pl.ds(p, nch, stride=CMP_STRIDE), :] for e in range(nbt)], axis=0)
        top = top + _dot((xp + pe_ref[0, p:p + 1, :]).astype(BF16), w1t_ref[0, p])
        bot = bot + _dot((xp + pe_ref[0, half + p:half + p + 1, :]).astype(BF16), w1b_ref[0, p])
    hid = top + pltpu.roll(bot, rows - 1, 0)
    act = hid * jax.nn.sigmoid(hid)
    out = _dot(act.astype(BF16), w2_ref[0]).astype(o_ref.dtype)
    for e in range(nbt):
        o_ref[0, e] = out[e * nch:(e + 1) * nch]


def _compress(cin, pe2, w1t, w1b, w2):
    _, B, T, _ = cin.shape
    nch = T // CMP_STRIDE
    nbt = 4 if B % 4 == 0 else 1
    sel = lambda *shape: pl.BlockSpec((1,) + shape, lambda j, b: (j,) + (0,) * len(shape))
    return pl.pallas_call(
        functools.partial(_compress_kernel, nch=nch, nbt=nbt),
        out_shape=jax.ShapeDtypeStruct((2, B, nch, LANES), BF16),
        grid=(2, B // nbt),
        in_specs=[pl.BlockSpec((1, nbt, T, LANES), lambda j, b: (j, b, 0, 0)),
                  sel(CMP_BLOCK, LANES), sel(CMP_BLOCK // 2, LANES, 2 * CMP_HIDDEN),
                  sel(CMP_BLOCK // 2, LANES, 2 * CMP_HIDDEN), sel(2 * CMP_HIDDEN, LANES)],
        out_specs=pl.BlockSpec((1, nbt, nch, LANES), lambda j, b: (j, b, 0, 0)),
        name="compress",
        compiler_params=pltpu.CompilerParams(dimension_semantics=("arbitrary", "arbitrary")),
    )(cin, pe2, w1t, w1b, w2)


def _bias_kernel(tbl_ref, o_ref, *, kind, n_cmp):
    pid = pl.program_id(0)
    toeplitz = kind != "cmp"
    if toeplitz:
        k = lax.broadcasted_iota(jnp.int32, (8, 2 * LANES), 1)
        dist = (pid - 1) * TQ + jnp.where(k < LANES, -k, 2 * LANES - k)
    else:
        al = lax.broadcasted_iota(jnp.int32, (CMP_STRIDE, 2 * LANES), 0)
        m = lax.broadcasted_iota(jnp.int32, (CMP_STRIDE, 2 * LANES), 1)
        dist = pid * TQ - (CMP_BLOCK - 1) + al - CMP_STRIDE * jnp.where(m < LANES, m, m - 2 * LANES)
    n = jnp.maximum(dist, 0)
    cnt = jnp.zeros_like(n)
    for t in _BUCKET_THR:
        cnt = cnt + jnp.where(n >= t, 1, 0)
    bucket = jnp.where(n < REL_BUCKETS // 2, n, REL_BUCKETS // 2 + cnt)
    if kind == "tok":
        madd = jnp.where((dist >= 0) & (pid > 0), 0.0, NEG)
    elif kind == "win":
        madd = jnp.where((dist >= 0) & (dist < WINDOW) & (pid > 0), 0.0, NEG)
    else:
        madd = jnp.where(dist >= 0, 0.0, NEG)
        valid_c = lax.broadcasted_iota(jnp.int32, (TQ, LANES), 1) < n_cmp
    for h in range(N_HEADS):
        val = jnp.zeros(dist.shape, F32)
        for bb in range(REL_BUCKETS):
            val = jnp.where(bucket == bb, tbl_ref[h * REL_BUCKETS + bb], val)
        val = val * LOG2E + madd
        if toeplitz:
            g = jnp.broadcast_to(val[0:1, :], (TQ, 2 * LANES))
            val = pltpu.roll(g, 0, 1, stride=1, stride_axis=0)[:, :LANES]
        else:
            bands = [val[:, :LANES]] + [pltpu.roll(val, ah, 1)[:, :LANES] for ah in range(1, TQ // CMP_STRIDE)]
            val = jnp.where(valid_c, jnp.concatenate(bands, axis=0), NEG)
        o_ref[0, h * TQ:(h + 1) * TQ, :] = val


def _bias_tiles(tbl, n_tiles, kind, n_cmp=0):
    return pl.pallas_call(
        functools.partial(_bias_kernel, kind=kind, n_cmp=n_cmp),
        out_shape=jax.ShapeDtypeStruct((n_tiles, N_HEADS * TQ, LANES), F32),
        grid=(n_tiles,),
        in_specs=[pl.BlockSpec(memory_space=pltpu.SMEM)],
        out_specs=pl.BlockSpec((1, N_HEADS * TQ, LANES), lambda i: (i, 0, 0)),
        name="bias_" + kind,
        compiler_params=pltpu.CompilerParams(dimension_semantics=("arbitrary",)),
    )(tbl)


def _nsa_kernel(nq4_ref, ks_ref, kw_ref, vs0_ref, vs1_ref, vw0_ref, vw1_ref, kc_ref, vc_ref,
                small4_ref, bt_ref, wb_ref, cb4_ref, ovt_ref, et_ref, yn_ref,
                lhs_sc, z_sc, zw_sc, seln_sc, partc_sc, *, nb, n_slc, n_top, n_win, n_chunks_max):
    HR = N_HEADS * TQ
    GR = N_REP * TQ
    CH = 4 * TQ
    G4 = 4
    vs_refs = (vs0_ref, vs1_ref)
    vw_refs = (vw0_ref, vw1_ref)
    lane = lax.broadcasted_iota(jnp.int32, (TQ, LANES), 1)
    lo = lane < N_HEAD_DIM

    def stack_heads(q_all):
        zero = jnp.zeros((TQ, LANES), BF16)
        parts = []
        for g in range(N_KV_GROUPS):
            for r in range(N_REP):
                qr = q_all[:, r * LANES:(r + 1) * LANES]
                parts.append(jnp.where(lo if g == 0 else jnp.logical_not(lo), qr, zero))
        return parts

    def pair(o, r):
        return jnp.where(lo, o[r * TQ:(r + 1) * TQ], o[(N_REP + r) * TQ:(N_REP + r + 1) * TQ])

    def normed(acc, r):
        den = jnp.where(lo, acc[(N_REP + r) * TQ:(N_REP + r + 1) * TQ], acc[r * TQ:(r + 1) * TQ])
        return pair(acc, r) / pltpu.roll(den, N_HEAD_DIM, 1)

    def gate(sg, r, ci):
        c0 = _SMALL_NGATE + r * 3 + ci
        c1 = _SMALL_NGATE + (N_REP + r) * 3 + ci
        return jnp.where(lo, sg[:, c0:c0 + 1], sg[:, c1:c1 + 1])

    def group_prep(s, n_chunks, qi):
        qs4 = jnp.concatenate(
            [p for j in range(G4) for p in stack_heads(nq4_ref[s, j * TQ:(j + 1) * TQ, :])], axis=0)
        rows = lax.broadcasted_iota(jnp.int32, (G4 * HR, 1), 0)
        t_rows = (qi + jnp.right_shift(rows, 10)) * TQ + (rows & (TQ - 1))

        z = _dot_nt(qs4, kc_ref[0, s]) + cb4_ref[...].reshape(G4 * HR, LANES)
        e = jnp.exp2(z - jnp.max(z, axis=-1, keepdims=True))
        l = _dot(e.astype(BF16), jnp.ones((LANES, LANES), BF16))
        p_c = e * jnp.where(t_rows >= CMP_BLOCK - 1, 1.0 / jnp.maximum(l, 1e-30), 0.0)
        o_c = _dot(p_c.astype(BF16), vc_ref[0, s])
        sg4 = jax.nn.sigmoid(small4_ref[s])
        for j in range(G4):
            for r in range(N_REP):
                partc_sc[s, j, r] = (gate(sg4[j * TQ:(j + 1) * TQ], r, 0)
                                     * pair(o_c[j * HR:(j + 1) * HR], r))

        W = G4 * N_KV_GROUPS * TQ
        jb = lax.broadcasted_iota(jnp.int32, (n_slc, W), 0)
        col = lax.broadcasted_iota(jnp.int32, (n_slc, W), 1)
        tq = (qi + jnp.right_shift(col, 8)) * TQ + (col & (TQ - 1))
        cur = jnp.right_shift(tq, 6)
        elig = jb <= cur
        if n_chunks * CH <= n_top * SEL_BLOCK:
            sel = jnp.where(elig, 1.0, 0.0)
        else:
            forced = (jb == 0) | (jb == cur) | (jb == cur - 1)
            psums = []
            for j in range(G4):
                for g in range(N_KV_GROUPS):
                    base = j * HR + g * GR
                    ps = p_c[base:base + TQ]
                    for r in range(1, N_REP):
                        ps = ps + p_c[base + r * TQ:base + (r + 1) * TQ]
                    psums.append(ps)
            imp = lax.dot_general(ovt_ref[...], jnp.concatenate(psums, axis=0), (((1,), (1,)), ((), ())),
                                  precision=HI, preferred_element_type=F32)
            score = jnp.where(elig, jnp.where(forced, BIG, imp), -BIG)
            cnt = jnp.zeros((n_slc, W), F32)
            for i in range(n_slc):
                si = score[i:i + 1, :]
                tie = jnp.where(jb > i, 1.0, 0.0)
                cnt = cnt + jnp.where(si > score, 1.0, jnp.where(si == score, tie, 0.0))
            sel = jnp.where((cnt < n_top) & (score > -BIG / 2), 1.0, 0.0)
        seln = jnp.concatenate([sel - 1.0, jnp.zeros((LANES - n_slc, W), F32)], axis=0).T.astype(BF16)
        for j in range(G4):
            for g in range(N_KV_GROUPS):
                seln_sc[s, j, g] = seln[(j * N_KV_GROUPS + g) * TQ:(j * N_KV_GROUPS + g + 1) * TQ]

    def tiles(n_pairs, qi, jq):
        qrows = pl.ds(pl.multiple_of(jq * TQ, TQ), TQ)
        w0 = jnp.maximum(qi - (n_win - 1), 0)
        woff = pl.multiple_of(w0 * TQ, TQ)
        elems = range(nb)
        cat = lambda xs: xs[0] if len(xs) == 1 else jnp.concatenate(xs, axis=0)
        rep = lambda t: t if nb == 1 else jnp.concatenate([t] * nb, axis=0)
        parts = [stack_heads(nq4_ref[s, qrows, :]) for s in elems]
        qs = [jnp.concatenate(parts[s], axis=0) for s in elems]

        def bias_tiles(ref, first, count, tile0):
            tiles_ = [ref[jnp.maximum(qi - tile0 - (first + u) + 1, 0)] for u in range(count)]
            return rep(tiles_[0] if count == 1 else jnp.concatenate(tiles_, axis=1))

        zmax = None
        for j0 in range(0, n_win, 2):
            nj = min(2, n_win - j0)
            zp = (cat([_dot_nt(qs[s], kw_ref[s, pl.ds(woff + j0 * TQ, nj * TQ), :]) for s in elems])
                  + bias_tiles(wb_ref, j0, nj, w0))
            zw_sc[:, j0 * TQ:(j0 + nj) * TQ] = zp
            for u in range(nj):
                zj = zp[:, u * TQ:(u + 1) * TQ]
                zmax = zj if zmax is None else jnp.maximum(zmax, zj)
        mw = jnp.max(zmax, axis=-1, keepdims=True)
        pw = jnp.exp2(zw_sc[...] - mw).astype(BF16)
        acc_w = cat([_dot(pw[s * HR + g * GR:s * HR + (g + 1) * GR],
                          vw_refs[g][s, pl.ds(woff, n_win * TQ), :])
                     for s in elems for g in range(N_KV_GROUPS)])

        for s in elems:
            for g in range(N_KV_GROUPS):
                seln = seln_sc[s, jq, g]
                for r in range(N_REP):
                    h = g * N_REP + r
                    lhs_sc[s, h * TQ:(h + 1) * TQ, :] = jnp.concatenate([parts[s][h], seln], axis=1)

        zmax = None
        for pi in range(n_pairs):
            k0 = pi * 2 * TQ
            zp = (cat([_dot_nt(lhs_sc[s], jnp.concatenate([ks_ref[s, k0:k0 + 2 * TQ, :],
                                                           et_ref[k0:k0 + 2 * TQ, :]], axis=1))
                       for s in elems])
                  + bias_tiles(bt_ref, 2 * pi, 2, 0))
            z_sc[pi // 2, :, (pi % 2) * 2 * TQ:(pi % 2 + 1) * 2 * TQ] = zp
            for u in range(2):
                zt = zp[:, u * TQ:(u + 1) * TQ]
                zmax = zt if zmax is None else jnp.maximum(zmax, zt)
        ms = jnp.max(zmax, axis=-1, keepdims=True)
        acc = [None] * (nb * N_KV_GROUPS)
        for c in range((n_pairs + 1) // 2):
            width = min(CH, n_pairs * 2 * TQ - c * CH)
            p = jnp.exp2(z_sc[c, :, 0:width] - ms).astype(BF16)
            for s in elems:
                for g in range(N_KV_GROUPS):
                    i = s * N_KV_GROUPS + g
                    d = _dot(p[i * GR:(i + 1) * GR], vs_refs[g][s, c * CH:c * CH + width, :])
                    acc[i] = d if acc[i] is None else acc[i] + d
        acc_s = cat(acc)

        for s in elems:
            sg = jax.nn.sigmoid(small4_ref[s, qrows, :])
            a_s = acc_s[s * HR:(s + 1) * HR]
            a_w = acc_w[s * HR:(s + 1) * HR]
            for r in range(N_REP):
                out = (partc_sc[s, jq, r] + gate(sg, r, 1) * normed(a_s, r)
                       + gate(sg, r, 2) * normed(a_w, r))
                yn_ref[s, qrows, r * LANES:(r + 1) * LANES] = out.astype(yn_ref.dtype)

    for nc in range(1, n_chunks_max + 1):
        @pl.when(pl.program_id(1) == nc - 1)
        def _(nc=nc):
            q0 = G4 * (nc - 1)
            for s in range(nb):
                group_prep(s, nc, q0)

            def body(jq, carry):
                tiles(2 * nc, q0 + jq, jq)
                return carry
            lax.fori_loop(0, G4, body, 0)


def _nsa(nq, nsw, ckv, small, bt, wb, cb, ov, emat):
    B, T, _ = nq.shape
    assert T == 2048, "single 128-wide compressed-key tile assumes T == 2048"
    nqt = T // TQ
    n_cmp = (T - CMP_BLOCK) // CMP_STRIDE + 1
    n_slc = T // SEL_BLOCK
    n_top = min(SEL_TOPK, n_slc)
    n_win = wb.shape[0] - 1
    HR = N_HEADS * TQ
    assert nqt % 4 == 0 and n_slc % 8 == 0 and n_slc <= LANES
    nb = NSA_BATCH if B % NSA_BATCH == 0 else 1
    n_chunks_max = nqt // 4
    kv = lambda j: pl.BlockSpec((nb, T, LANES), lambda b, g: (b, 0, j))
    ck = lambda j: pl.BlockSpec((1, nb, T // CMP_STRIDE, LANES), lambda b, g: (j, b, 0, 0))
    kern = functools.partial(_nsa_kernel, nb=nb, n_slc=n_slc, n_top=n_top, n_win=n_win,
                             n_chunks_max=n_chunks_max)
    return pl.pallas_call(
        kern, out_shape=jax.ShapeDtypeStruct((B, T, N_WIDTH), BF16),
        grid=(B // nb, n_chunks_max),
        in_specs=[pl.BlockSpec((nb, 4 * TQ, N_WIDTH), lambda b, g: (b, g, 0)),
                  kv(0), kv(1), kv(2), kv(3), kv(4), kv(5), ck(0), ck(1),
                  pl.BlockSpec((nb, 4 * TQ, LANES), lambda b, g: (b, g, 0)),
                  _const_spec(bt.shape), _const_spec(wb.shape),
                  pl.BlockSpec((4, HR, LANES), lambda b, g: (g, 0, 0)),
                  _const_spec(ov.shape), _const_spec(emat.shape)],
        out_specs=pl.BlockSpec((nb, 4 * TQ, N_WIDTH), lambda b, g: (b, g, 0)),
        scratch_shapes=[pltpu.VMEM((nb, HR, 2 * LANES), BF16),
                        pltpu.VMEM((n_chunks_max, nb * HR, 4 * TQ), F32),
                        pltpu.VMEM((nb * HR, n_win * TQ), F32),
                        pltpu.VMEM((nb, 4, N_KV_GROUPS, TQ, LANES), BF16),
                        pltpu.VMEM((nb, 4, N_REP, TQ, LANES), F32)],
        name="nsa",
        compiler_params=pltpu.CompilerParams(dimension_semantics=("arbitrary", "arbitrary"),
                                             vmem_limit_bytes=VMEM_LIMIT),
    )(nq, nsw, nsw, nsw, nsw, nsw, nsw, ckv, ckv, small, bt, wb, cb, ov, emat)


def _merge_kernel(x_ref, ym_ref, yn_ref, mg_ref, wbm_ref, wbn_ref, wo_ref,
                  wg_ref, wu_ref, wd_ref, gfin_ref, o_ref, *, tf):
    bm = _dot(ym_ref[...], wbm_ref[...])
    bn = _dot(yn_ref[...], wbn_ref[...])
    mixed = (jax.nn.sigmoid(mg_ref[:, :D_MODEL].astype(F32)) * bm
             + jax.nn.sigmoid(mg_ref[:, D_MODEL:].astype(F32)) * bn)
    h = x_ref[...] + _dot(mixed.astype(BF16), wo_ref[...])
    rs = lax.rsqrt(jnp.mean(h * h, axis=-1, keepdims=True) + RMS_EPS)
    hb = h.astype(BF16)
    acc = jnp.zeros(h.shape, F32)
    for j in range(D_FF // tf):
        gg = _dot(hb, wg_ref[:, j * tf:(j + 1) * tf]) * rs
        uu = _dot(hb, wu_ref[:, j * tf:(j + 1) * tf]) * rs
        act = (gg * jax.nn.sigmoid(gg) * uu).astype(BF16)
        acc = acc + _dot(act, wd_ref[j * tf:(j + 1) * tf, :])
    h2 = h + acc
    o_ref[...] = h2 * lax.rsqrt(jnp.mean(h2 * h2, axis=-1, keepdims=True) + RMS_EPS) * gfin_ref[...]


def _merge(x2, ym, yn, mg, wbm, wbn, wo, wg, wu, wd, gfin, tm=512, tf=256):
    n = x2.shape[0]
    assert n % tm == 0 and D_FF % tf == 0
    row = lambda width: pl.BlockSpec((tm, width), lambda i: (i, 0))
    return pl.pallas_call(
        functools.partial(_merge_kernel, tf=tf),
        out_shape=jax.ShapeDtypeStruct((n, D_MODEL), F32), grid=(n // tm,),
        in_specs=[row(D_MODEL), row(M_WIDTH), row(N_WIDTH), row(N_BRANCH * D_MODEL),
                  _const_spec(wbm.shape), _const_spec(wbn.shape), _const_spec(wo.shape),
                  _const_spec(wg.shape), _const_spec(wu.shape),
                  _const_spec(wd.shape), _const_spec(gfin.shape)],
        out_specs=row(D_MODEL), name="merge_ffn",
        compiler_params=pltpu.CompilerParams(dimension_semantics=("arbitrary",),
                                             vmem_limit_bytes=VMEM_LIMIT),
    )(x2, ym, yn, mg, wbm, wbn, wo, wg, wu, wd, gfin)


def _nsa_constants(T):
    n_cmp = (T - CMP_BLOCK) // CMP_STRIDE + 1
    n_slc = T // SEL_BLOCK
    cs = np.arange(n_cmp) * CMP_STRIDE
    ss = np.arange(n_slc) * SEL_BLOCK
    ov = np.clip(np.minimum(cs[:, None] + CMP_BLOCK, ss[None, :] + SEL_BLOCK)
                 - np.maximum(cs[:, None], ss[None, :]), 0, None) / CMP_STRIDE
    ovt = np.zeros((n_slc, LANES), np.float32)
    ovt[:, :n_cmp] = ov.T
    et = (np.arange(T)[:, None] // SEL_BLOCK == np.arange(LANES)[None, :]).astype(np.float32) * (-NEG)
    return jnp.asarray(ovt), jnp.asarray(et, dtype=BF16)


def _compress_weights(pe_cmp, w_cmp1, w_cmp2):
    half = CMP_BLOCK // 2
    assert N_KV_GROUPS == 2

    def blockdiag(w):
        z = jnp.zeros_like(w)
        return jnp.concatenate([jnp.concatenate([w, z], axis=-1), jnp.concatenate([z, w], axis=-1)], axis=-2)

    w1bd = blockdiag(w_cmp1.reshape(2, CMP_BLOCK, N_HEAD_DIM, CMP_HIDDEN).astype(BF16))
    w2bd = blockdiag(w_cmp2)
    pe2 = jnp.tile(pe_cmp, (1, 1, N_KV_GROUPS))
    return pe2, w1bd[:, :half].astype(BF16), w1bd[:, half:].astype(BF16), w2bd.astype(BF16)


def kernel(x, g_norm_mix, w_in, b_in, b_fgate, conv_qk, g_mlstm_head, pe_cmp, w_cmp1, w_cmp2,
           rel_bias, w_branch, w_out, g_norm_ffn, w_gate, w_up, w_down, g_final):
    B, T, D = x.shape
    assert D == D_MODEL and w_in.shape[0] == 1, "one residual block (DEPTH == 1)"
    N = B * T
    x2 = x.reshape(N, D)

    idx, scale = _inproj_perm()
    w_r = _wprep_t(w_in[0].T, g_norm_mix[0].reshape(1, D), idx, scale)
    b_r = _gather_cols(b_in[0].reshape(1, -1), idx, scale, F32)
    cw = jnp.zeros((8, 2 * M_WIDTH), F32).at[:CONV_WIDTH].set(conv_qk[0])
    mqk, mv, mo, nq, cin, nsw, mg, small = _inproj(x2, w_r, b_r)

    bf8 = jnp.concatenate([jnp.zeros((M_HEADS,), F32), b_fgate[0].astype(F32)])
    bfc = jnp.broadcast_to(jnp.tile(bf8, T // M_CHUNK)[:, None], (8 * (T // M_CHUNK), M_CHUNK))
    ltri = jnp.asarray(np.tril(np.ones((M_CHUNK, M_CHUNK), np.float32)))
    r3 = lambda a: a.reshape(B, T, a.shape[-1])
    ym = _mlstm(r3(mqk), r3(mv), r3(mo), r3(small), cw, bfc, g_mlstm_head[0].reshape(1, M_WIDTH), ltri)

    pe2, w1t, w1b, w2 = _compress_weights(pe_cmp[0], w_cmp1[0], w_cmp2[0])
    ckv = _compress(cin.reshape(2, B, T, LANES), pe2, w1t, w1b, w2)
    tbl = rel_bias.astype(F32).T.reshape(-1)
    nqt = T // TQ
    bt = _bias_tiles(tbl, nqt + 1, "tok")
    wb = _bias_tiles(tbl, min(WINDOW // TQ + 1, nqt) + 1, "win")
    cb = _bias_tiles(tbl, nqt, "cmp", n_cmp=(T - CMP_BLOCK) // CMP_STRIDE + 1)
    ov, emat = _nsa_constants(T)
    yn = _nsa(r3(nq), r3(nsw), ckv, r3(small), bt, wb, cb, ov, emat)

    wbm = w_branch[0, 0].astype(BF16)
    wbn = jnp.concatenate(
        [w_branch[0, 1, (g * N_REP + r) * N_HEAD_DIM:(g * N_REP + r + 1) * N_HEAD_DIM]
         for r in range(N_REP) for g in range(N_KV_GROUPS)], axis=0).astype(BF16)
    ident = lambda n: (np.arange(n, dtype=np.int32), np.ones((n,), np.float32))
    g_ffn = g_norm_ffn[0].reshape(D, 1)
    out = _merge(x2, ym.reshape(N, M_WIDTH), yn.reshape(N, N_WIDTH), mg, wbm, wbn,
                 w_out[0].astype(BF16), _wprep(w_gate[0], g_ffn, *ident(D_FF)),
                 _wprep(w_up[0], g_ffn, *ident(D_FF)),
                 _wprep(w_down[0], jnp.ones((D_FF, 1), F32), *ident(D)),
                 g_final.reshape(1, D))
    return out.reshape(B, T, D)
```

```python
import functools

import numpy as np
import jax
import jax.numpy as jnp
from jax import lax
from jax.experimental import pallas as pl
from jax.experimental.pallas import tpu as pltpu

F32 = jnp.float32
BF16 = jnp.bfloat16
HI = lax.Precision.HIGHEST

D_MODEL = 1024
M_HEADS = 4
M_HEAD_DIM = 128
M_WIDTH = M_HEADS * M_HEAD_DIM
M_CHUNK = 128
CONV_WIDTH = 4
N_HEADS = 8
N_KV_GROUPS = 2
N_REP = N_HEADS // N_KV_GROUPS
N_HEAD_DIM = 64
N_WIDTH = N_HEADS * N_HEAD_DIM
N_KV_WIDTH = N_KV_GROUPS * N_HEAD_DIM
CMP_BLOCK = 32
CMP_STRIDE = 16
CMP_HIDDEN = 2 * N_HEAD_DIM
SEL_BLOCK = 64
SEL_TOPK = 16
WINDOW = 512
REL_BUCKETS = 32
REL_MAX_DIST = 1024
N_BRANCH = 2
D_FF = 2816
RMS_EPS = 1e-6
BIG = 1e9
NEG = -1e30

LANES = 128
TQ = 128
NSA_BATCH = 1
VMEM_LIMIT = 56 * 1024 * 1024

_OFF_MQ, _OFF_MK, _OFF_MV, _OFF_MO = 0, 512, 1024, 1536
_OFF_MI, _OFF_MF, _OFF_NQ, _OFF_NKV = 2048, 2052, 2056, 2568
_OFF_NGATE, _OFF_MERGE, _D_IN = 3336, 3360, 5408
_SMALL_NGATE = 8

_SEGS = (("mqk", 1024), ("mv", 512), ("mo", 512), ("nq", 512), ("kc", 128),
         ("vc", 128), ("nsw", 512), ("mg", 2048), ("small", 128))
_D_IN_PAD = sum(w for _, w in _SEGS)
LOG2E = 1.4426950408889634


def _dot(a, b, **kw):
    return jnp.dot(a, b, preferred_element_type=F32, **kw)


def _dot_nt(a, b):
    return lax.dot_general(a, b, (((1,), (1,)), ((), ())), preferred_element_type=F32)


def _dot_tn(a, b):
    return lax.dot_general(a, b, (((0,), (0,)), ((), ())), preferred_element_type=F32)


def _const_spec(shape):
    nd = len(shape)
    return pl.BlockSpec(shape, lambda *_: (0,) * nd, pipeline_mode=pl.Buffered(1))


def _bucket_thresholds():
    max_exact = REL_BUCKETS // 2
    assert REL_MAX_DIST == 64 * max_exact and REL_BUCKETS - max_exact == 16
    thr = []
    for k in range(1, REL_BUCKETS - max_exact):
        t = max_exact
        while t ** 8 < (max_exact ** 8) * (2 ** (3 * k)):
            t += 1
        thr.append(t)
    return tuple(thr)


_BUCKET_THR = _bucket_thresholds()


def _inproj_perm():
    idx = np.zeros((_D_IN_PAD,), np.int32)
    scale = np.zeros((_D_IN_PAD,), np.float32)
    pos = 0

    def put(cols, s=1.0):
        nonlocal pos
        n = len(cols)
        idx[pos:pos + n] = cols
        scale[pos:pos + n] = s
        pos += n

    put(np.arange(_OFF_MQ, _OFF_MV))
    put(np.arange(_OFF_MV, _OFF_MO))
    put(np.arange(_OFF_MO, _OFF_MI))
    nq = np.zeros((N_WIDTH,), np.int32)
    for r in range(N_REP):
        for g in range(N_KV_GROUPS):
            for d in range(N_HEAD_DIM):
                nq[r * 128 + g * 64 + d] = _OFF_NQ + (g * N_REP + r) * N_HEAD_DIM + d
    put(nq, N_HEAD_DIM ** -0.5 * LOG2E)
    kv = lambda j, g: np.arange(_OFF_NKV + (j * N_KV_GROUPS + g) * N_HEAD_DIM,
                                _OFF_NKV + (j * N_KV_GROUPS + g + 1) * N_HEAD_DIM)
    put(np.arange(_OFF_NKV, _OFF_NKV + 256))
    put(np.concatenate([kv(2, 0), kv(2, 1)]))
    put(np.concatenate([kv(4, 0), kv(4, 1)]))
    put(np.concatenate([kv(3, 0), kv(3, 1)]))
    put(np.concatenate([kv(5, 0), kv(5, 1)]))
    put(np.arange(_OFF_MERGE, _D_IN))
    put(np.arange(_OFF_MI, _OFF_MI + 8))
    put(np.arange(_OFF_NGATE, _OFF_NGATE + 24))
    pos += LANES - 32
    assert pos == _D_IN_PAD
    return idx, scale


def _gather_cols(a, idx, scale, dtype):
    pieces = []
    start = 0
    n = len(idx)
    for c in range(1, n + 1):
        same = c < n and scale[c] == scale[start] and (scale[c] == 0.0 or idx[c] == idx[c - 1] + 1)
        if not same:
            if scale[start] == 0.0:
                piece = jnp.zeros((a.shape[0], c - start), dtype)
            else:
                piece = a[:, int(idx[start]):int(idx[start]) + (c - start)]
                if scale[start] != 1.0:
                    piece = piece * float(scale[start])
            pieces.append(piece.astype(dtype))
            start = c
    return jnp.concatenate(pieces, axis=1)


def _col_runs(idx, scale):
    runs = []
    start = 0
    n = len(idx)
    for c in range(1, n + 1):
        same = c < n and scale[c] == scale[start] and (scale[c] == 0.0 or idx[c] == idx[c - 1] + 1)
        if not same:
            runs.append((start, int(idx[start]), c - start, float(scale[start])))
            start = c
    return runs


def _wprep_kernel(w_ref, g_ref, o_ref, *, runs):
    g = g_ref[...]
    for dst, src, n, scale in runs:
        if scale == 0.0:
            o_ref[:, dst:dst + n] = jnp.zeros((o_ref.shape[0], n), o_ref.dtype)
        else:
            o_ref[:, dst:dst + n] = (w_ref[:, src:src + n] * (g * scale)).astype(o_ref.dtype)


def _wprep_t_kernel(wt_ref, g_ref, o_ref, *, runs):
    g = g_ref[...]
    for dst, src, n, scale in runs:
        if scale == 0.0:
            o_ref[dst:dst + n, :] = jnp.zeros((n, o_ref.shape[1]), o_ref.dtype)
        else:
            o_ref[dst:dst + n, :] = (wt_ref[src:src + n, :] * (g * scale)).astype(o_ref.dtype)


def _wprep_t(wt, g, idx, scale, tc=256):
    rows, cols = wt.shape
    assert cols % tc == 0
    return pl.pallas_call(
        functools.partial(_wprep_t_kernel, runs=_col_runs(idx, scale)),
        out_shape=jax.ShapeDtypeStruct((len(idx), cols), BF16), grid=(cols // tc,),
        in_specs=[pl.BlockSpec((rows, tc), lambda i: (0, i)), pl.BlockSpec((1, tc), lambda i: (0, i))],
        out_specs=pl.BlockSpec((len(idx), tc), lambda i: (0, i)), name="wprep_t",
        compiler_params=pltpu.CompilerParams(dimension_semantics=("arbitrary",),
                                             vmem_limit_bytes=VMEM_LIMIT),
    )(wt, g)


def _wprep(w, g, idx, scale, tr=256):
    rows, cols = w.shape
    assert rows % tr == 0
    return pl.pallas_call(
        functools.partial(_wprep_kernel, runs=_col_runs(idx, scale)),
        out_shape=jax.ShapeDtypeStruct((rows, len(idx)), BF16), grid=(rows // tr,),
        in_specs=[pl.BlockSpec((tr, cols), lambda i: (i, 0)), pl.BlockSpec((tr, 1), lambda i: (i, 0))],
        out_specs=pl.BlockSpec((tr, len(idx)), lambda i: (i, 0)), name="wprep",
        compiler_params=pltpu.CompilerParams(dimension_semantics=("arbitrary",),
                                             vmem_limit_bytes=VMEM_LIMIT),
    )(w, g)


def _inproj_kernel(x_ref, w_ref, b_ref, mqk_ref, mv_ref, mo_ref, nq_ref,
                   cin_ref, nsw_ref, mg_ref, small_ref):
    x = x_ref[...]
    rs = lax.rsqrt(jnp.mean(x * x, axis=-1, keepdims=True) + RMS_EPS)
    xb = x.astype(BF16)

    def seg(a, n):
        return _dot_nt(xb, w_ref[a:a + n, :]) * rs + b_ref[:, a:a + n]

    off = {}
    pos = 0
    for name, w in _SEGS:
        off[name] = pos
        pos += w

    mqk_ref[...] = seg(off["mqk"], 1024).astype(BF16)
    mv_ref[...] = seg(off["mv"], 512).astype(BF16)
    mo_ref[...] = seg(off["mo"], 512).astype(BF16)
    nq_ref[...] = seg(off["nq"], 512).astype(BF16)
    cc = seg(off["kc"], 2 * LANES)
    cin_ref[0] = cc[:, :LANES]
    cin_ref[1] = cc[:, LANES:]
    nsw_ref[:, 0:2 * LANES] = seg(off["nsw"], 2 * LANES).astype(BF16)
    vv = seg(off["nsw"] + 2 * LANES, 2 * LANES)
    lo = lax.broadcasted_iota(jnp.int32, (x.shape[0], LANES), 1) < N_HEAD_DIM
    for j in range(2):
        vj = vv[:, j * LANES:(j + 1) * LANES]
        nsw_ref[:, (2 + 2 * j) * LANES:(3 + 2 * j) * LANES] = jnp.where(lo, vj, 1.0).astype(BF16)
        nsw_ref[:, (3 + 2 * j) * LANES:(4 + 2 * j) * LANES] = jnp.where(lo, 1.0, vj).astype(BF16)
    mg_ref[...] = seg(off["mg"], 2048).astype(BF16)
    small_ref[...] = seg(off["small"], 128)


def _inproj(x2, w, b, tm=1024):
    n = x2.shape[0]
    assert n % tm == 0
    row = lambda width: pl.BlockSpec((tm, width), lambda i: (i, 0))
    out_shape = (
        jax.ShapeDtypeStruct((n, 1024), BF16), jax.ShapeDtypeStruct((n, 512), BF16),
        jax.ShapeDtypeStruct((n, 512), BF16), jax.ShapeDtypeStruct((n, 512), BF16),
        jax.ShapeDtypeStruct((2, n, 128), F32), jax.ShapeDtypeStruct((n, 768), BF16),
        jax.ShapeDtypeStruct((n, 2048), BF16), jax.ShapeDtypeStruct((n, 128), F32))
    out_specs = (row(1024), row(512), row(512), row(512),
                 pl.BlockSpec((2, tm, 128), lambda i: (0, i, 0)), row(768), row(2048), row(128))
    return pl.pallas_call(
        _inproj_kernel, out_shape=out_shape, grid=(n // tm,),
        in_specs=[row(D_MODEL), _const_spec((_D_IN_PAD, D_MODEL)), _const_spec((1, _D_IN_PAD))],
        out_specs=out_specs, name="inproj",
        compiler_params=pltpu.CompilerParams(dimension_semantics=("arbitrary",),
                                             vmem_limit_bytes=VMEM_LIMIT),
    )(x2, w, b)


def _mlstm_kernel(mqk_ref, mv_ref, mo_ref, small_ref, cw_ref, bfc_ref, gh_ref, ltri_ref,
                  ym_ref, ebuf, c_st, m_st, gx, g_bc, g_d, g_a, g_mrow, g_blast, g_amax):
    L = M_CHUNK
    H = M_HEADS
    c = pl.program_id(1)
    n_chunks = gx.shape[0] // 8
    n_sub = mqk_ref.shape[1] // L
    lanes =lax.broadcasted_iota(jnp.int32, gx.shape, 1)

    @pl.when(c == 0)
    def _():
        ebuf[0:8, :] = jnp.zeros((8, 2 * M_WIDTH), F32)
        c_st[...] = jnp.zeros_like(c_st)
        m_st[...] = jnp.zeros_like(m_st)
        for cc in range(n_chunks):
            gx[cc * 8:(cc + 1) * 8, :] = small_ref[0, cc * L:(cc + 1) * L, :].T[0:8, :]
        x = gx[...]
        fg = x + bfc_ref[...]
        lf = jnp.minimum(fg, 0.0) - jnp.log1p(jnp.exp(-jnp.abs(fg)))
        bc = lax.dot_general(lf, ltri_ref[...], (((1,), (1,)), ((), ())),
                             precision=HI, preferred_element_type=F32)
        ig = pltpu.roll(x, H, 0)
        blast = bc[:, L - 1:L]
        a_all = blast - bc + ig
        d = ig - bc
        cm = d
        sh = 1
        while sh < L:
            cm = jnp.maximum(cm, jnp.where(lanes >= sh, pltpu.roll(cm, sh, 1), -jnp.inf))
            sh *= 2
        g_bc[...] = bc
        g_d[...] = d
        g_a[...] = a_all
        g_mrow[...] = bc + cm
        g_blast[...] = jnp.broadcast_to(blast, gx.shape)
        g_amax[...] = jnp.broadcast_to(jnp.max(a_all, axis=-1, keepdims=True), gx.shape)

    row = lax.broadcasted_iota(jnp.int32, (L, L), 0)
    col = lax.broadcasted_iota(jnp.int32, (L, L), 1)
    causal = row >= col
    ones = jnp.ones((L, M_HEAD_DIM), BF16)

    def chunk(j):
        rows = slice(j * L, (j + 1) * L)
        ebuf[8:8 + L, :] = mqk_ref[0, rows, :].astype(F32)

        def conv_slab(c0):
            cs = slice(c0, c0 + M_HEAD_DIM)
            ext = ebuf[:, cs]
            conv = cw_ref[CONV_WIDTH - 1:CONV_WIDTH, cs] * ext[8:, :]
            for k in range(1, CONV_WIDTH):
                conv = conv + cw_ref[CONV_WIDTH - 1 - k:CONV_WIDTH - k, cs] * pltpu.roll(ext, k, 0)[8:, :]
            return (conv * jax.nn.sigmoid(conv)).astype(BF16)

        r8 = pl.ds(pl.multiple_of((c * n_sub + j) * 8, 8), 8)
        bc = g_bc[r8, :]
        dt = g_d[r8, :]
        blast = g_blast[r8, :]
        m_prev = m_st[...]
        m_new = jnp.maximum(blast + m_prev, g_amax[r8, :])
        decay = jnp.exp(blast + m_prev - m_new)
        w_row = jnp.exp(g_a[r8, :] - m_new)
        log_inter = bc + m_prev
        m_i = jnp.maximum(g_mrow[r8, :], log_inter)
        k_scale = M_HEAD_DIM ** -0.5
        s_inter = jnp.exp(log_inter - m_i) * k_scale
        emi = jnp.exp(-m_i)
        u = bc - m_i + float(np.log(k_scale))
        cols = jnp.concatenate([u, s_inter, emi, w_row, jnp.zeros((LANES - 32, L), F32)], axis=0).T
        u_all, s_inter_all = cols[:, 0:8], cols[:, 8:16]
        emi_all, w_all = cols[:, 16:24], cols[:, 24:32]

        for h in range(M_HEADS):
            sl = slice(h * M_HEAD_DIM, (h + 1) * M_HEAD_DIM)
            qb = conv_slab(h * M_HEAD_DIM)
            kb = conv_slab(M_WIDTH + h * M_HEAD_DIM)
            v = mv_ref[0, rows, sl]
            cn_prev = c_st[h]

            g = H + h
            logp = jnp.where(causal, u_all[:, g:g + 1] + dt[g:g + 1, :], -jnp.inf)
            p = jnp.exp(logp) * _dot_nt(qb, kb)
            nd = (_dot(p.astype(BF16), jnp.concatenate([v, ones], axis=1))
                  + s_inter_all[:, g:g + 1] * _dot(qb, cn_prev.astype(BF16)))
            hh = nd[:, :M_HEAD_DIM] / jnp.maximum(jnp.abs(nd[:, M_HEAD_DIM:]), emi_all[:, g:g + 1])
            hh = hh * lax.rsqrt(jnp.mean(hh * hh, axis=-1, keepdims=True) + RMS_EPS) * gh_ref[:, sl]
            ym_ref[0, rows, sl] = (jax.nn.sigmoid(mo_ref[0, rows, sl].astype(F32)) * hh).astype(ym_ref.dtype)

            w_col = w_all[:, g:g + 1]
            vw = jnp.concatenate([v.astype(F32) * w_col, jnp.broadcast_to(w_col, (L, M_HEAD_DIM))], axis=1)
            c_st[h] = decay[g:g + 1, 0:1] * cn_prev + _dot_tn(kb, vw.astype(BF16))

        m_st[...] = m_new
        ebuf[0:8, :] = ebuf[L:L + 8, :]

    for j in range(n_sub):
        chunk(j)


def _mlstm(mqk, mv, mo, small, cw, bf, gh, ltri, n_sub=4):
    B, T, _ = mqk.shape
    L = M_CHUNK
    assert T % (n_sub * L) == 0
    nc = T // L
    assert L == LANES and 8 * nc <= LANES
    blk = lambda w: pl.BlockSpec((1, n_sub * L, w), lambda b, c: (b, c, 0))
    gate = lambda: pltpu.VMEM((8 * nc, L), F32)
    return pl.pallas_call(
        _mlstm_kernel, out_shape=jax.ShapeDtypeStruct((B, T, M_WIDTH), BF16),
        grid=(B, nc // n_sub),
        in_specs=[blk(2 * M_WIDTH), blk(M_WIDTH), blk(M_WIDTH),
                  pl.BlockSpec((1, T, LANES), lambda b, c: (b, 0, 0)),
                  _const_spec((8, 2 * M_WIDTH)), _const_spec((8 * nc, L)),
                  _const_spec((1, M_WIDTH)), _const_spec((L, L))],
        out_specs=blk(M_WIDTH),
        scratch_shapes=[pltpu.VMEM((L + 8, 2 * M_WIDTH), F32),
                        pltpu.VMEM((M_HEADS, M_HEAD_DIM, 2 * M_HEAD_DIM), F32),
                        pltpu.VMEM((8, L), F32),
                        gate(), gate(), gate(), gate(), gate(), gate(), gate()],
        name="mlstm",
        compiler_params=pltpu.CompilerParams(dimension_semantics=("arbitrary", "arbitrary")),
    )(mqk, mv, mo, small, cw, bf, gh, ltri)


def _compress_kernel(x_ref, pe_ref, w1t_ref, w1b_ref, w2_ref, o_ref, *, nch, nbt):
    half = CMP_BLOCK // 2
    rows = nbt * nch
    top = jnp.zeros((rows, 2 * CMP_HIDDEN), F32)
    bot = jnp.zeros((rows, 2 * CMP_HIDDEN), F32)
    for p in range(half):
        xp = jnp.concatenate([x_ref[0, e, pl.ds(p, nch, stride=CMP_STRIDE), :] for e in range(nbt)], axis=0)
        top = top + _dot((xp + pe_ref[0, p:p + 1, :]).astype(BF16), w1t_ref[0, p])
        bot = bot + _dot((xp + pe_ref[0, half + p:half + p + 1, :]).astype(BF16), w1b_ref[0, p])
    hid = top + pltpu.roll(bot, rows - 1, 0)
    act = hid * jax.nn.sigmoid(hid)
    out = _dot(act.astype(BF16), w2_ref[0]).astype(o_ref.dtype)
    for e in range(nbt):
        o_ref[0, e] = out[e * nch:(e + 1) * nch]


def _compress(cin, pe2, w1t, w1b, w2):
    _, B, T, _ = cin.shape
    nch = T // CMP_STRIDE
    nbt = 4 if B % 4 == 0 else 1
    sel = lambda *shape: pl.BlockSpec((1,) + shape, lambda j, b: (j,) + (0,) * len(shape))
    return pl.pallas_call(
        functools.partial(_compress_kernel, nch=nch, nbt=nbt),
        out_shape=jax.ShapeDtypeStruct((2, B, nch, LANES), BF16),
        grid=(2, B // nbt),
        in_specs=[pl.BlockSpec((1, nbt, T, LANES), lambda j, b: (j, b, 0, 0)),
                  sel(CMP_BLOCK, LANES), sel(CMP_BLOCK // 2, LANES, 2 * CMP_HIDDEN),
                  sel(CMP_BLOCK // 2, LANES, 2 * CMP_HIDDEN), sel(2 * CMP_HIDDEN, LANES)],
        out_specs=pl.BlockSpec((1, nbt, nch, LANES), lambda j, b: (j, b, 0, 0)),
        name="compress",
        compiler_params=pltpu.CompilerParams(dimension_semantics=("arbitrary", "arbitrary")),
    )(cin, pe2, w1t, w1b, w2)


def _bias_kernel(tbl_ref, o_ref, *, kind, n_cmp):
    pid = pl.program_id(0)
    toeplitz = kind != "cmp"
    if toeplitz:
        k = lax.broadcasted_iota(jnp.int32, (8, 2 * LANES), 1)
        dist = (pid - 1) * TQ + jnp.where(k < LANES, -k, 2 * LANES - k)
    else:
        al = lax.broadcasted_iota(jnp.int32, (CMP_STRIDE, 2 * LANES), 0)
        m = lax.broadcasted_iota(jnp.int32, (CMP_STRIDE, 2 * LANES), 1)
        dist = pid * TQ - (CMP_BLOCK - 1) + al - CMP_STRIDE * jnp.where(m < LANES, m, m - 2 * LANES)
    n = jnp.maximum(dist, 0)
    cnt = jnp.zeros_like(n)
    for t in _BUCKET_THR:
        cnt = cnt + jnp.where(n >= t, 1, 0)
    bucket = jnp.where(n < REL_BUCKETS // 2, n, REL_BUCKETS // 2 + cnt)
    if kind == "tok":
        madd = jnp.where((dist >= 0) & (pid > 0), 0.0, NEG)
    elif kind == "win":
        madd = jnp.where((dist >= 0) & (dist < WINDOW) & (pid > 0), 0.0, NEG)
    else:
        madd = jnp.where(dist >= 0, 0.0, NEG)
        valid_c = lax.broadcasted_iota(jnp.int32, (TQ, LANES), 1) < n_cmp
    for h in range(N_HEADS):
        val = jnp.zeros(dist.shape, F32)
        for bb in range(REL_BUCKETS):
            val = jnp.where(bucket == bb, tbl_ref[h * REL_BUCKETS + bb], val)
        val = val * LOG2E + madd
        if toeplitz:
            g = jnp.broadcast_to(val[0:1, :], (TQ, 2 * LANES))
            val = pltpu.roll(g, 0, 1, stride=1, stride_axis=0)[:, :LANES]
        else:
            bands = [val[:, :LANES]] + [pltpu.roll(val, ah, 1)[:, :LANES] for ah in range(1, TQ // CMP_STRIDE)]
            val = jnp.where(valid_c, jnp.concatenate(bands, axis=0), NEG)
        o_ref[0, h * TQ:(h + 1) * TQ, :] = val


def _bias_tiles(tbl, n_tiles, kind, n_cmp=0):
    return pl.pallas_call(
        functools.partial(_bias_kernel, kind=kind, n_cmp=n_cmp),
        out_shape=jax.ShapeDtypeStruct((n_tiles, N_HEADS * TQ, LANES), F32),
        grid=(n_tiles,),
        in_specs=[pl.BlockSpec(memory_space=pltpu.SMEM)],
        out_specs=pl.BlockSpec((1, N_HEADS * TQ, LANES), lambda i: (i, 0, 0)),
        name="bias_" + kind,
        compiler_params=pltpu.CompilerParams(dimension_semantics=("arbitrary",)),
    )(tbl)


def _nsa_kernel(nq4_ref, ks_ref, kw_ref, vs0_ref, vs1_ref, vw0_ref, vw1_ref, kc_ref, vc_ref,
                small4_ref, bt_ref, wb_ref, cb4_ref, ovt_ref, et_ref, yn_ref,
                lhs_sc, z_sc, zw_sc, seln_sc, partc_sc, *, nb, n_slc, n_top, n_win, n_chunks_max):
    HR = N_HEADS * TQ
    GR = N_REP * TQ
    CH = 4 * TQ
    G4 = 4
    vs_refs = (vs0_ref, vs1_ref)
    vw_refs = (vw0_ref, vw1_ref)
    lane = lax.broadcasted_iota(jnp.int32, (TQ, LANES), 1)
    lo = lane < N_HEAD_DIM

    def stack_heads(q_all):
        zero = jnp.zeros((TQ, LANES), BF16)
        parts = []
        for g in range(N_KV_GROUPS):
            for r in range(N_REP):
                qr = q_all[:, r * LANES:(r + 1) * LANES]
                parts.append(jnp.where(lo if g == 0 else jnp.logical_not(lo), qr, zero))
        return parts

    def pair(o, r):
        return jnp.where(lo, o[r * TQ:(r + 1) * TQ], o[(N_REP + r) * TQ:(N_REP + r + 1) * TQ])

    def normed(acc, r):
        den = jnp.where(lo, acc[(N_REP + r) * TQ:(N_REP + r + 1) * TQ], acc[r * TQ:(r + 1) * TQ])
        return pair(acc, r) / pltpu.roll(den, N_HEAD_DIM, 1)

    def gate(sg, r, ci):
        c0 = _SMALL_NGATE + r * 3 + ci
        c1 = _SMALL_NGATE + (N_REP + r) * 3 + ci
        return jnp.where(lo, sg[:, c0:c0 + 1], sg[:, c1:c1 + 1])

    def group_prep(s, n_chunks, qi):
        qs4 = jnp.concatenate(
            [p for j in range(G4) for p in stack_heads(nq4_ref[s, j * TQ:(j + 1) * TQ, :])], axis=0)
        rows = lax.broadcasted_iota(jnp.int32, (G4 * HR, 1), 0)
        t_rows = (qi + jnp.right_shift(rows, 10)) * TQ + (rows & (TQ - 1))

        z = _dot_nt(qs4, kc_ref[0, s]) + cb4_ref[...].reshape(G4 * HR, LANES)
        e = jnp.exp2(z - jnp.max(z, axis=-1, keepdims=True))
        l = _dot(e.astype(BF16), jnp.ones((LANES, LANES), BF16))
        p_c = e * jnp.where(t_rows >= CMP_BLOCK - 1, 1.0 / jnp.maximum(l, 1e-30), 0.0)
        o_c = _dot(p_c.astype(BF16), vc_ref[0, s])
        sg4 = jax.nn.sigmoid(small4_ref[s])
        for j in range(G4):
            for r in range(N_REP):
                partc_sc[s, j, r] = (gate(sg4[j * TQ:(j + 1) * TQ], r, 0)
                                     * pair(o_c[j * HR:(j + 1) * HR], r))

        W = G4 * N_KV_GROUPS * TQ
        jb = lax.broadcasted_iota(jnp.int32, (n_slc, W), 0)
        col = lax.broadcasted_iota(jnp.int32, (n_slc, W), 1)
        tq = (qi + jnp.right_shift(col, 8)) * TQ + (col & (TQ - 1))
        cur = jnp.right_shift(tq, 6)
        elig = jb <= cur
        if n_chunks * CH <= n_top * SEL_BLOCK:
            sel = jnp.where(elig, 1.0, 0.0)
        else:
            forced = (jb == 0) | (jb == cur) | (jb == cur - 1)
            psums = []
            for j in range(G4):
                for g in range(N_KV_GROUPS):
                    base = j * HR + g * GR
                    ps = p_c[base:base + TQ]
                    for r in range(1, N_REP):
                        ps = ps + p_c[base + r * TQ:base + (r + 1) * TQ]
                    psums.append(ps)
            imp = lax.dot_general(ovt_ref[...], jnp.concatenate(psums, axis=0), (((1,), (1,)), ((), ())),
                                  precision=HI, preferred_element_type=F32)
            score = jnp.where(elig, jnp.where(forced, BIG, imp), -BIG)
            cnt = jnp.zeros((n_slc, W), F32)
            for i in range(n_slc):
                si = score[i:i + 1, :]
                tie = jnp.where(jb > i, 1.0, 0.0)
                cnt = cnt + jnp.where(si > score, 1.0, jnp.where(si == score, tie, 0.0))
            sel = jnp.where((cnt < n_top) & (score > -BIG / 2), 1.0, 0.0)
        seln = jnp.concatenate([sel - 1.0, jnp.zeros((LANES - n_slc, W), F32)], axis=0).T.astype(BF16)
        for j in range(G4):
            for g in range(N_KV_GROUPS):
                seln_sc[s, j, g] = seln[(j * N_KV_GROUPS + g) * TQ:(j * N_KV_GROUPS + g + 1) * TQ]

    def tiles(n_pairs, qi, jq):
        qrows = pl.ds(pl.multiple_of(jq * TQ, TQ), TQ)
        w0 = jnp.maximum(qi - (n_win - 1), 0)
        woff = pl.multiple_of(w0 * TQ, TQ)
        elems = range(nb)
        cat = lambda xs: xs[0] if len(xs) == 1 else jnp.concatenate(xs, axis=0)
        rep = lambda t: t if nb == 1 else jnp.concatenate([t] * nb, axis=0)
        parts = [stack_heads(nq4_ref[s, qrows, :]) for s in elems]
        qs = [jnp.concatenate(parts[s], axis=0) for s in elems]

        def bias_tiles(ref, first, count, tile0):
            tiles_ = [ref[jnp.maximum(qi - tile0 - (first + u) + 1, 0)] for u in range(count)]
            return rep(tiles_[0] if count == 1 else jnp.concatenate(tiles_, axis=1))

        zmax = None
        for j0 in range(0, n_win, 2):
            nj = min(2, n_win - j0)
            zp = (cat([_dot_nt(qs[s], kw_ref[s, pl.ds(woff + j0 * TQ, nj * TQ), :]) for s in elems])
                  + bias_tiles(wb_ref, j0, nj, w0))
            zw_sc[:, j0 * TQ:(j0 + nj) * TQ] = zp
            for u in range(nj):
                zj = zp[:, u * TQ:(u + 1) * TQ]
                zmax = zj if zmax is None else jnp.maximum(zmax, zj)
        mw = jnp.max(zmax, axis=-1, keepdims=True)
        pw = jnp.exp2(zw_sc[...] - mw).astype(BF16)
        acc_w = cat([_dot(pw[s * HR + g * GR:s * HR + (g + 1) * GR],
                          vw_refs[g][s, pl.ds(woff, n_win * TQ), :])
                     for s in elems for g in range(N_KV_GROUPS)])

        for s in elems:
            for g in range(N_KV_GROUPS):
                seln = seln_sc[s, jq, g]
                for r in range(N_REP):
                    h = g * N_REP + r
                    lhs_sc[s, h * TQ:(h + 1) * TQ, :] = jnp.concatenate([parts[s][h], seln], axis=1)

        zmax = None
        for pi in range(n_pairs):
            k0 = pi * 2 * TQ
            zp = (cat([_dot_nt(lhs_sc[s], jnp.concatenate([ks_ref[s, k0:k0 + 2 * TQ, :],
                                                           et_ref[k0:k0 + 2 * TQ, :]], axis=1))
                       for s in elems])
                  + bias_tiles(bt_ref, 2 * pi, 2, 0))
            z_sc[pi // 2, :, (pi % 2) * 2 * TQ:(pi % 2 + 1) * 2 * TQ] = zp
            for u in range(2):
                zt = zp[:, u * TQ:(u + 1) * TQ]
                zmax = zt if zmax is None else jnp.maximum(zmax, zt)
        ms = jnp.max(zmax, axis=-1, keepdims=True)
        acc = [None] * (nb * N_KV_GROUPS)
        for c in range((n_pairs + 1) // 2):
            width = min(CH, n_pairs * 2 * TQ - c * CH)
            p = jnp.exp2(z_sc[c, :, 0:width] - ms).astype(BF16)
            for s in elems:
                for g in range(N_KV_GROUPS):
                    i = s * N_KV_GROUPS + g
                    d = _dot(p[i * GR:(i + 1) * GR], vs_refs[g][s, c * CH:c * CH + width, :])
                    acc[i] = d if acc[i] is None else acc[i] + d
        acc_s = cat(acc)

        for s in elems:
            sg = jax.nn.sigmoid(small4_ref[s, qrows, :])
            a_s = acc_s[s * HR:(s + 1) * HR]
            a_w = acc_w[s * HR:(s + 1) * HR]
            for r in range(N_REP):
                out = (partc_sc[s, jq, r] + gate(sg, r, 1) * normed(a_s, r)
                       + gate(sg, r, 2) * normed(a_w, r))
                yn_ref[s, qrows, r * LANES:(r + 1) * LANES] = out.astype(yn_ref.dtype)

    for nc in range(1, n_chunks_max + 1):
        @pl.when(pl.program_id(1) == nc - 1)
        def _(nc=nc):
            q0 = G4 * (nc - 1)
            for s in range(nb):
                group_prep(s, nc, q0)

            def body(jq, carry):
                tiles(2 * nc, q0 + jq, jq)
                return carry
            lax.fori_loop(0, G4, body, 0)


def _nsa(nq, nsw, ckv, small, bt, wb, cb, ov, emat):
    B, T, _ = nq.shape
    assert T == 2048, "single 128-wide compressed-key tile assumes T == 2048"
    nqt = T // TQ
    n_cmp = (T - CMP_BLOCK) // CMP_STRIDE + 1
    n_slc = T // SEL_BLOCK
    n_top = min(SEL_TOPK, n_slc)
    n_win = wb.shape[0] - 1
    HR = N_HEADS * TQ
    assert nqt % 4 == 0 and n_slc % 8 == 0 and n_slc <= LANES
    nb = NSA_BATCH if B % NSA_BATCH == 0 else 1
    n_chunks_max = nqt // 4
    kv = lambda j: pl.BlockSpec((nb, T, LANES), lambda b, g: (b, 0, j))
    ck = lambda j: pl.BlockSpec((1, nb, T // CMP_STRIDE, LANES), lambda b, g: (j, b, 0, 0))
    kern = functools.partial(_nsa_kernel, nb=nb, n_slc=n_slc, n_top=n_top, n_win=n_win,
                             n_chunks_max=n_chunks_max)
    return pl.pallas_call(
        kern, out_shape=jax.ShapeDtypeStruct((B, T, N_WIDTH), BF16),
        grid=(B // nb, n_chunks_max),
        in_specs=[pl.BlockSpec((nb, 4 * TQ, N_WIDTH), lambda b, g: (b, g, 0)),
                  kv(0), kv(1), kv(2), kv(3), kv(4), kv(5), ck(0), ck(1),
                  pl.BlockSpec((nb, 4 * TQ, LANES), lambda b, g: (b, g, 0)),
                  _const_spec(bt.shape), _const_spec(wb.shape),
                  pl.BlockSpec((4, HR, LANES), lambda b, g: (g, 0, 0)),
                  _const_spec(ov.shape), _const_spec(emat.shape)],
        out_specs=pl.BlockSpec((nb, 4 * TQ, N_WIDTH), lambda b, g: (b, g, 0)),
        scratch_shapes=[pltpu.VMEM((nb, HR, 2 * LANES), BF16),
                        pltpu.VMEM((n_chunks_max, nb * HR, 4 * TQ), F32),
                        pltpu.VMEM((nb * HR, n_win * TQ), F32),
                        pltpu.VMEM((nb, 4, N_KV_GROUPS, TQ, LANES), BF16),
                        pltpu.VMEM((nb, 4, N_REP, TQ, LANES), F32)],
        name="nsa",
        compiler_params=pltpu.CompilerParams(dimension_semantics=("arbitrary", "arbitrary"),
                                             vmem_limit_bytes=VMEM_LIMIT),
    )(nq, nsw, nsw, nsw, nsw, nsw, nsw, ckv, ckv, small, bt, wb, cb, ov, emat)


def _merge_kernel(x_ref, ym_ref, yn_ref, mg_ref, wbm_ref, wbn_ref, wo_ref,
                  wg_ref, wu_ref, wd_ref, gfin_ref, o_ref, *, tf):
    bm = _dot(ym_ref[...], wbm_ref[...])
    bn = _dot(yn_ref[...], wbn_ref[...])
    mixed = (jax.nn.sigmoid(mg_ref[:, :D_MODEL].astype(F32)) * bm
             + jax.nn.sigmoid(mg_ref[:, D_MODEL:].astype(F32)) * bn)
    h = x_ref[...] + _dot(mixed.astype(BF16), wo_ref[...])
    rs = lax.rsqrt(jnp.mean(h * h, axis=-1, keepdims=True) + RMS_EPS)
    hb = h.astype(BF16)
    acc = jnp.zeros(h.shape, F32)
    for j in range(D_FF // tf):
        gg = _dot(hb, wg_ref[:, j * tf:(j + 1) * tf]) * rs
        uu = _dot(hb, wu_ref[:, j * tf:(j + 1) * tf]) * rs
        act = (gg * jax.nn.sigmoid(gg) * uu).astype(BF16)
        acc = acc + _dot(act, wd_ref[j * tf:(j + 1) * tf, :])
    h2 = h + acc
    o_ref[...] = h2 * lax.rsqrt(jnp.mean(h2 * h2, axis=-1, keepdims=True) + RMS_EPS) * gfin_ref[...]


def _merge(x2, ym, yn, mg, wbm, wbn, wo, wg, wu, wd, gfin, tm=512, tf=256):
    n = x2.shape[0]
    assert n % tm == 0 and D_FF % tf == 0
    row = lambda width: pl.BlockSpec((tm, width), lambda i: (i, 0))
    return pl.pallas_call(
        functools.partial(_merge_kernel, tf=tf),
        out_shape=jax.ShapeDtypeStruct((n, D_MODEL), F32), grid=(n // tm,),
        in_specs=[row(D_MODEL), row(M_WIDTH), row(N_WIDTH), row(N_BRANCH * D_MODEL),
                  _const_spec(wbm.shape), _const_spec(wbn.shape), _const_spec(wo.shape),
                  _const_spec(wg.shape), _const_spec(wu.shape),
                  _const_spec(wd.shape), _const_spec(gfin.shape)],
        out_specs=row(D_MODEL), name="merge_ffn",
        compiler_params=pltpu.CompilerParams(dimension_semantics=("arbitrary",),
                                             vmem_limit_bytes=VMEM_LIMIT),
    )(x2, ym, yn, mg, wbm, wbn, wo, wg, wu, wd, gfin)


def _nsa_constants(T):
    n_cmp = (T - CMP_BLOCK) // CMP_STRIDE + 1
    n_slc = T // SEL_BLOCK
    cs = np.arange(n_cmp) * CMP_STRIDE
    ss = np.arange(n_slc) * SEL_BLOCK
    ov = np.clip(np.minimum(cs[:, None] + CMP_BLOCK, ss[None, :] + SEL_BLOCK)
                 - np.maximum(cs[:, None], ss[None, :]), 0, None) / CMP_STRIDE
    ovt = np.zeros((n_slc, LANES), np.float32)
    ovt[:, :n_cmp] = ov.T
    et = (np.arange(T)[:, None] // SEL_BLOCK == np.arange(LANES)[None, :]).astype(np.float32) * (-NEG)
    return jnp.asarray(ovt), jnp.asarray(et, dtype=BF16)


def _compress_weights(pe_cmp, w_cmp1, w_cmp2):
    half = CMP_BLOCK // 2
    assert N_KV_GROUPS == 2

    def blockdiag(w):
        z = jnp.zeros_like(w)
        return jnp.concatenate([jnp.concatenate([w, z], axis=-1), jnp.concatenate([z, w], axis=-1)], axis=-2)

    w1bd = blockdiag(w_cmp1.reshape(2, CMP_BLOCK, N_HEAD_DIM, CMP_HIDDEN).astype(BF16))
    w2bd = blockdiag(w_cmp2)
    pe2 = jnp.tile(pe_cmp, (1, 1, N_KV_GROUPS))
    return pe2, w1bd[:, :half].astype(BF16), w1bd[:, half:].astype(BF16), w2bd.astype(BF16)


def kernel(x, g_norm_mix, w_in, b_in, b_fgate, conv_qk, g_mlstm_head, pe_cmp, w_cmp1, w_cmp2,
           rel_bias, w_branch, w_out, g_norm_ffn, w_gate, w_up, w_down, g_final):
    B, T, D = x.shape
    assert D == D_MODEL and w_in.shape[0] == 1, "one residual block (DEPTH == 1)"
    N = B * T
    x2 = x.reshape(N, D)

    idx, scale = _inproj_perm()
    w_r = _wprep_t(w_in[0].T, g_norm_mix[0].reshape(1, D), idx, scale)
    b_r = _gather_cols(b_in[0].reshape(1, -1), idx, scale, F32)
    cw = jnp.zeros((8, 2 * M_WIDTH), F32).at[:CONV_WIDTH].set(conv_qk[0])
    mqk, mv, mo, nq, cin, nsw, mg, small = _inproj(x2, w_r, b_r)

    bf8 = jnp.concatenate([jnp.zeros((M_HEADS,), F32), b_fgate[0].astype(F32)])
    bfc = jnp.broadcast_to(jnp.tile(bf8, T // M_CHUNK)[:, None], (8 * (T // M_CHUNK), M_CHUNK))
    ltri = jnp.asarray(np.tril(np.ones((M_CHUNK, M_CHUNK), np.float32)))
    r3 = lambda a: a.reshape(B, T, a.shape[-1])
    ym = _mlstm(r3(mqk), r3(mv), r3(mo), r3(small), cw, bfc, g_mlstm_head[0].reshape(1, M_WIDTH), ltri)

    pe2, w1t, w1b, w2 = _compress_weights(pe_cmp[0], w_cmp1[0], w_cmp2[0])
    ckv = _compress(cin.reshape(2, B, T, LANES), pe2, w1t, w1b, w2)
    tbl = rel_bias.astype(F32).T.reshape(-1)
    nqt = T // TQ
    bt = _bias_tiles(tbl, nqt + 1, "tok")
    wb = _bias_tiles(tbl, min(WINDOW // TQ + 1, nqt) + 1, "win")
    cb = _bias_tiles(tbl, nqt, "cmp", n_cmp=(T - CMP_BLOCK) // CMP_STRIDE + 1)
    ov, emat = _nsa_constants(T)
    yn = _nsa(r3(nq), r3(nsw), ckv, r3(small), bt, wb, cb, ov, emat)

    wbm = w_branch[0, 0].astype(BF16)
    wbn = jnp.concatenate(
        [w_branch[0, 1, (g * N_REP + r) * N_HEAD_DIM:(g * N_REP + r + 1) * N_HEAD_DIM]
         for r in range(N_REP) for g in range(N_KV_GROUPS)], axis=0).astype(BF16)
    ident = lambda n: (np.arange(n, dtype=np.int32), np.ones((n,), np.float32))
    g_ffn = g_norm_ffn[0].reshape(D, 1)
    out = _merge(x2, ym.reshape(N, M_WIDTH), yn.reshape(N, N_WIDTH), mg, wbm, wbn,
                 w_out[0].astype(BF16), _wprep(w_gate[0], g_ffn, *ident(D_FF)),
                 _wprep(w_up[0], g_ffn, *ident(D_FF)),
                 _wprep(w_down[0], jnp.ones((D_FF, 1), F32), *ident(D)),
                 g_final.reshape(1, D))
    return out.reshape(B, T, D)
```
